```python
import jax, jax.numpy as jnp
from jax import lax
import numpy as np

D_MODEL = 1024
BATCH = 16
SEQ = 4096
DEPTH = 2

FOX_HEADS = 8
FOX_HEAD_DIM = 64
FOX_WIDTH = FOX_HEADS * FOX_HEAD_DIM
GDN_HEADS = 4
GDN_HEAD_DIM = 128
GDN_WIDTH = GDN_HEADS * GDN_HEAD_DIM
MIX_WIDTH = FOX_WIDTH + GDN_WIDTH
CONV_WIDTH = 4
CHUNK = 64
Q_BLOCK = 128
D_FF = 2816
EPS = 1e-6
FORGET_BIAS_INIT = 3.0
SPLIT_SIZES = (FOX_WIDTH, FOX_WIDTH, FOX_WIDTH, FOX_HEADS,
               GDN_WIDTH, GDN_WIDTH, GDN_WIDTH, GDN_HEADS, GDN_HEADS, GDN_WIDTH)
N_IN = 3 * FOX_WIDTH + FOX_HEADS + 4 * GDN_WIDTH + 2 * GDN_HEADS

kernel_name = "macaron_fox_gdn_hybrid"


def rms_norm(x, w):
    xf = x.astype(jnp.float32)
    y = xf * lax.rsqrt(jnp.mean(xf * xf, axis=-1, keepdims=True) + EPS)
    return (y * w.astype(jnp.float32)).astype(x.dtype)


def l2_norm(x):
    xf = x.astype(jnp.float32)
    return xf * lax.rsqrt(jnp.sum(xf * xf, axis=-1, keepdims=True) + EPS)


def swiglu_ffn(x, w_in, w_out):
    gate, up = jnp.split(x @ w_in, 2, axis=-1)
    return (jax.nn.silu(gate) * up) @ w_out


def causal_depthwise_conv(x, w):
    c = x.shape[-1]
    return lax.conv_general_dilated(
        x, w[:, None, :].astype(x.dtype), window_strides=(1,), padding=[(CONV_WIDTH - 1, 0)],
        dimension_numbers=("NWC", "WIO", "NWC"), feature_group_count=c)


def forgetting_attention(q, k, v, f_logit):
    seq = q.shape[2]
    scale = FOX_HEAD_DIM ** -0.5
    cum = jnp.cumsum(jax.nn.log_sigmoid(f_logit.astype(jnp.float32)), axis=-1)
    outs = []
    for blk in range(seq // Q_BLOCK):
        start, end = blk * Q_BLOCK, (blk + 1) * Q_BLOCK
        qb = q[:, :, start:end]
        kb = k[:, :, :end]
        vb = v[:, :, :end]
        s = jnp.einsum("bhqd,bhkd->bhqk", qb, kb).astype(jnp.float32) * scale
        s = s + cum[:, :, start:end, None] - cum[:, :, None, :end]
        causal = (start + jnp.arange(Q_BLOCK))[:, None] >= jnp.arange(end)[None, :]
        s = jnp.where(causal, s, -jnp.inf)
        p = jax.nn.softmax(s, axis=-1)
        outs.append(jnp.einsum("bhqk,bhkd->bhqd", p.astype(v.dtype), vb))
    return jnp.concatenate(outs, axis=2)


def gated_delta_rule_chunked(q, k, v, g, beta):
    out_dtype = v.dtype
    b, h, seq, dk = q.shape
    dv = v.shape[-1]
    n = seq // CHUNK
    q = q.astype(jnp.float32) * dk ** -0.5
    k = k.astype(jnp.float32)
    v = v.astype(jnp.float32)
    g = g.astype(jnp.float32).reshape(b, h, n, CHUNK)
    beta = beta.astype(jnp.float32).reshape(b, h, n, CHUNK)
    q = q.reshape(b, h, n, CHUNK, dk)
    k = k.reshape(b, h, n, CHUNK, dk)
    v = v.reshape(b, h, n, CHUNK, dv)

    g = jnp.cumsum(g, axis=-1)
    tri_incl = jnp.tril(jnp.ones((CHUNK, CHUNK), dtype=bool))
    tri_strict = jnp.tril(jnp.ones((CHUNK, CHUNK), dtype=bool), k=-1)
    decay = jnp.exp(jnp.where(tri_incl, g[..., :, None] - g[..., None, :], -jnp.inf))

    k_beta = k * beta[..., None]
    v_beta = v * beta[..., None]
    a_strict = jnp.where(tri_strict, jnp.einsum("bhnid,bhnjd->bhnij", k_beta, k) * decay, 0.0)
    eye = jnp.eye(CHUNK, dtype=jnp.float32)
    t_mat = lax.linalg.triangular_solve(eye + a_strict, jnp.broadcast_to(eye, a_strict.shape),
                                        left_side=True, lower=True, unit_diagonal=True)
    u = jnp.einsum("bhnij,bhnjd->bhnid", t_mat, v_beta)
    w = jnp.einsum("bhnij,bhnjd->bhnid", t_mat, k_beta * jnp.exp(g)[..., None])
    attn_intra = jnp.where(tri_incl, jnp.einsum("bhnid,bhnjd->bhnij", q, k) * decay, 0.0)
    g_last = g[..., -1]
    k_dec = k * jnp.exp(g_last[..., None] - g)[..., None]
    q_dec = q * jnp.exp(g)[..., None]

    def step(state, inp):
        qd, kd, uc, wc, ac, gl = inp
        v_new = uc - jnp.einsum("bhck,bhkv->bhcv", wc, state)
        o = jnp.einsum("bhck,bhkv->bhcv", qd, state) + jnp.einsum("bhij,bhjv->bhiv", ac, v_new)
        state = state * jnp.exp(gl)[..., None, None] + jnp.einsum("bhck,bhcv->bhkv", kd, v_new)
        return state, o

    xs = tuple(jnp.moveaxis(t, 2, 0) for t in (q_dec, k_dec, u, w, attn_intra, g_last))
    state0 = jnp.zeros((b, h, dk, dv), jnp.float32)
    _, o = lax.scan(step, state0, xs)
    o = jnp.moveaxis(o, 0, 2).reshape(b, h, seq, dv)
    return o.astype(out_dtype)


def hybrid_mixer(hn, w_in, fox_q_norm, fox_k_norm, fox_f_bias,
                 gdn_conv, gdn_a_log, gdn_dt_bias, gdn_out_norm, w_out):
    b, s, _ = hn.shape
    proj = hn @ w_in
    offsets = []
    acc = 0
    for size in SPLIT_SIZES[:-1]:
        acc += size
        offsets.append(acc)
    fq, fk, fv, ff, gq, gk, gv, ga, gb, gg = jnp.split(proj, offsets, axis=-1)

    def heads(t, n_h, d_h):
        return t.reshape(b, s, n_h, d_h).transpose(0, 2, 1, 3)

    fq = rms_norm(heads(fq, FOX_HEADS, FOX_HEAD_DIM), fox_q_norm)
    fk = rms_norm(heads(fk, FOX_HEADS, FOX_HEAD_DIM), fox_k_norm)
    fv = heads(fv, FOX_HEADS, FOX_HEAD_DIM)
    f_logit = (ff + fox_f_bias).transpose(0, 2, 1)
    y_fox = forgetting_attention(fq, fk, fv, f_logit)
    y_fox = y_fox.transpose(0, 2, 1, 3).reshape(b, s, FOX_WIDTH)

    qkv = jax.nn.silu(causal_depthwise_conv(jnp.concatenate([gq, gk, gv], axis=-1), gdn_conv))
    gq, gk, gv = jnp.split(qkv, 3, axis=-1)
    gq = l2_norm(heads(gq, GDN_HEADS, GDN_HEAD_DIM))
    gk = l2_norm(heads(gk, GDN_HEADS, GDN_HEAD_DIM))
    gv = heads(gv, GDN_HEADS, GDN_HEAD_DIM)
    beta = jax.nn.sigmoid(gb.astype(jnp.float32)).transpose(0, 2, 1)
    log_decay = (-jnp.exp(gdn_a_log.astype(jnp.float32))
                 * jax.nn.softplus(ga.astype(jnp.float32) + gdn_dt_bias.astype(jnp.float32))).transpose(0, 2, 1)
    y_gdn = gated_delta_rule_chunked(gq, gk, gv, log_decay, beta)
    y_gdn = rms_norm(y_gdn, gdn_out_norm) * jax.nn.silu(heads(gg, GDN_HEADS, GDN_HEAD_DIM))
    y_gdn = y_gdn.transpose(0, 2, 1, 3).reshape(b, s, GDN_WIDTH)

    return jnp.concatenate([y_fox, y_gdn], axis=-1) @ w_out


def _fwd_setup_inputs(seed: int = 0) -> dict:
    key = jax.random.key(seed)
    ks = jax.random.split(key, 20)
    L, D = DEPTH, D_MODEL

    def normal(k, shape, scale):
        return jax.random.normal(k, shape, jnp.float32) * scale

    def gain(k, shape):
        return 1.0 + 0.1 * jax.random.normal(k, shape, jnp.float32)

    return {
        "x": normal(ks[0], (BATCH, SEQ, D), 1.0),
        "ffn1_norm": gain(ks[1], (L, D)),
        "ffn1_w_in": normal(ks[2], (L, D, 2 * D_FF), D ** -0.5),
        "ffn1_w_out": normal(ks[3], (L, D_FF, D), D_FF ** -0.5),
        "mix_norm": gain(ks[4], (L, D)),
        "w_in": normal(ks[5], (L, D, N_IN), D ** -0.5),
        "fox_q_norm": gain(ks[6], (L, FOX_HEAD_DIM)),
        "fox_k_norm": gain(ks[7], (L, FOX_HEAD_DIM)),
        "fox_f_bias": FORGET_BIAS_INIT + 0.1 * jax.random.normal(ks[8], (L, FOX_HEADS), jnp.float32),
        "gdn_conv": normal(ks[9], (L, CONV_WIDTH, 3 * GDN_WIDTH), CONV_WIDTH ** -0.5),
        "gdn_a_log": jnp.log(jax.random.uniform(ks[10], (L, GDN_HEADS), jnp.float32, 1.0, 16.0)),
        "gdn_dt_bias": jnp.log(jnp.expm1(jax.random.uniform(ks[11], (L, GDN_HEADS), jnp.float32, 0.001, 0.1))),
        "gdn_out_norm": gain(ks[12], (L, GDN_HEAD_DIM)),
        "w_out": normal(ks[13], (L, MIX_WIDTH, D), MIX_WIDTH ** -0.5),
        "ffn2_norm": gain(ks[14], (L, D)),
        "ffn2_w_in": normal(ks[15], (L, D, 2 * D_FF), D ** -0.5),
        "ffn2_w_out": normal(ks[16], (L, D_FF, D), D_FF ** -0.5),
    }


def _fwd_reference(x, ffn1_norm, ffn1_w_in, ffn1_w_out, mix_norm, w_in, fox_q_norm, fox_k_norm,
              fox_f_bias, gdn_conv, gdn_a_log, gdn_dt_bias, gdn_out_norm, w_out,
              ffn2_norm, ffn2_w_in, ffn2_w_out):
    for l in range(DEPTH):
        x = x + 0.5 * swiglu_ffn(rms_norm(x, ffn1_norm[l]), ffn1_w_in[l], ffn1_w_out[l])
        x = x + hybrid_mixer(rms_norm(x, mix_norm[l]), w_in[l], fox_q_norm[l], fox_k_norm[l],
                             fox_f_bias[l], gdn_conv[l], gdn_a_log[l], gdn_dt_bias[l],
                             gdn_out_norm[l], w_out[l])
        x = x + 0.5 * swiglu_ffn(rms_norm(x, ffn2_norm[l]), ffn2_w_in[l], ffn2_w_out[l])
    return x


import jax as _jax
import jax.numpy as _jnp

TWIN_FORMAT = 'train_step'
FWD_PARAMS = ['x', 'ffn1_norm', 'ffn1_w_in', 'ffn1_w_out', 'mix_norm', 'w_in', 'fox_q_norm', 'fox_k_norm', 'fox_f_bias', 'gdn_conv', 'gdn_a_log', 'gdn_dt_bias', 'gdn_out_norm', 'w_out', 'ffn2_norm', 'ffn2_w_in', 'ffn2_w_out']
TWIN_WEIGHTS = ['ffn1_norm', 'ffn1_w_in', 'ffn1_w_out', 'mix_norm', 'w_in', 'fox_q_norm', 'fox_k_norm', 'fox_f_bias', 'gdn_conv', 'gdn_a_log', 'gdn_dt_bias', 'gdn_out_norm', 'w_out', 'ffn2_norm', 'ffn2_w_in', 'ffn2_w_out']
TWIN_DIFF_INPUT = 'x'
TWIN_INPUTS = ['x', 'ffn1_norm', 'ffn1_w_in', 'ffn1_w_out', 'mix_norm', 'w_in', 'fox_q_norm', 'fox_k_norm', 'fox_f_bias', 'gdn_conv', 'gdn_a_log', 'gdn_dt_bias', 'gdn_out_norm', 'w_out', 'ffn2_norm', 'ffn2_w_in', 'ffn2_w_out', 'loss_target', 'm_ffn1_norm', 'm_ffn1_w_in', 'm_ffn1_w_out', 'm_mix_norm', 'm_w_in', 'm_fox_q_norm', 'm_fox_k_norm', 'm_fox_f_bias', 'm_gdn_conv', 'm_gdn_a_log', 'm_gdn_dt_bias', 'm_gdn_out_norm', 'm_w_out', 'm_ffn2_norm', 'm_ffn2_w_in', 'm_ffn2_w_out', 'v_ffn1_norm', 'v_ffn1_w_in', 'v_ffn1_w_out', 'v_mix_norm', 'v_w_in', 'v_fox_q_norm', 'v_fox_k_norm', 'v_fox_f_bias', 'v_gdn_conv', 'v_gdn_a_log', 'v_gdn_dt_bias', 'v_gdn_out_norm', 'v_w_out', 'v_ffn2_norm', 'v_ffn2_w_in', 'v_ffn2_w_out']
TWIN_OUTPUTS = ['loss', 'grad_x', 'grad_ffn1_norm', 'grad_ffn1_w_in', 'grad_ffn1_w_out', 'grad_mix_norm', 'grad_w_in', 'grad_fox_q_norm', 'grad_fox_k_norm', 'grad_fox_f_bias', 'grad_gdn_conv', 'grad_gdn_a_log', 'grad_gdn_dt_bias', 'grad_gdn_out_norm', 'grad_w_out', 'grad_ffn2_norm', 'grad_ffn2_w_in', 'grad_ffn2_w_out', 'delta_ffn1_norm', 'delta_ffn1_w_in', 'delta_ffn1_w_out', 'delta_mix_norm', 'delta_w_in', 'delta_fox_q_norm', 'delta_fox_k_norm', 'delta_fox_f_bias', 'delta_gdn_conv', 'delta_gdn_a_log', 'delta_gdn_dt_bias', 'delta_gdn_out_norm', 'delta_w_out', 'delta_ffn2_norm', 'delta_ffn2_w_in', 'delta_ffn2_w_out', 'new_m_ffn1_norm', 'new_m_ffn1_w_in', 'new_m_ffn1_w_out', 'new_m_mix_norm', 'new_m_w_in', 'new_m_fox_q_norm', 'new_m_fox_k_norm', 'new_m_fox_f_bias', 'new_m_gdn_conv', 'new_m_gdn_a_log', 'new_m_gdn_dt_bias', 'new_m_gdn_out_norm', 'new_m_w_out', 'new_m_ffn2_norm', 'new_m_ffn2_w_in', 'new_m_ffn2_w_out', 'new_v_ffn1_norm', 'new_v_ffn1_w_in', 'new_v_ffn1_w_out', 'new_v_mix_norm', 'new_v_w_in', 'new_v_fox_q_norm', 'new_v_fox_k_norm', 'new_v_fox_f_bias', 'new_v_gdn_conv', 'new_v_gdn_a_log', 'new_v_gdn_dt_bias', 'new_v_gdn_out_norm', 'new_v_w_out', 'new_v_ffn2_norm', 'new_v_ffn2_w_in', 'new_v_ffn2_w_out']
TWIN_LEAF_KINDS = {'loss': 'loss', 'grad_x': 'grad_x', 'grad_ffn1_norm': 'grad_w', 'grad_ffn1_w_in': 'grad_w', 'grad_ffn1_w_out': 'grad_w', 'grad_mix_norm': 'grad_w', 'grad_w_in': 'grad_w', 'grad_fox_q_norm': 'grad_w', 'grad_fox_k_norm': 'grad_w', 'grad_fox_f_bias': 'grad_w', 'grad_gdn_conv': 'grad_w', 'grad_gdn_a_log': 'grad_w', 'grad_gdn_dt_bias': 'grad_w', 'grad_gdn_out_norm': 'grad_w', 'grad_w_out': 'grad_w', 'grad_ffn2_norm': 'grad_w', 'grad_ffn2_w_in': 'grad_w', 'grad_ffn2_w_out': 'grad_w', 'delta_ffn1_norm': 'delta_w', 'delta_ffn1_w_in': 'delta_w', 'delta_ffn1_w_out': 'delta_w', 'delta_mix_norm': 'delta_w', 'delta_w_in': 'delta_w', 'delta_fox_q_norm': 'delta_w', 'delta_fox_k_norm': 'delta_w', 'delta_fox_f_bias': 'delta_w', 'delta_gdn_conv': 'delta_w', 'delta_gdn_a_log': 'delta_w', 'delta_gdn_dt_bias': 'delta_w', 'delta_gdn_out_norm': 'delta_w', 'delta_w_out': 'delta_w', 'delta_ffn2_norm': 'delta_w', 'delta_ffn2_w_in': 'delta_w', 'delta_ffn2_w_out': 'delta_w', 'new_m_ffn1_norm': 'new_m', 'new_m_ffn1_w_in': 'new_m', 'new_m_ffn1_w_out': 'new_m', 'new_m_mix_norm': 'new_m', 'new_m_w_in': 'new_m', 'new_m_fox_q_norm': 'new_m', 'new_m_fox_k_norm': 'new_m', 'new_m_fox_f_bias': 'new_m', 'new_m_gdn_conv': 'new_m', 'new_m_gdn_a_log': 'new_m', 'new_m_gdn_dt_bias': 'new_m', 'new_m_gdn_out_norm': 'new_m', 'new_m_w_out': 'new_m', 'new_m_ffn2_norm': 'new_m', 'new_m_ffn2_w_in': 'new_m', 'new_m_ffn2_w_out': 'new_m', 'new_v_ffn1_norm': 'new_v', 'new_v_ffn1_w_in': 'new_v', 'new_v_ffn1_w_out': 'new_v', 'new_v_mix_norm': 'new_v', 'new_v_w_in': 'new_v', 'new_v_fox_q_norm': 'new_v', 'new_v_fox_k_norm': 'new_v', 'new_v_fox_f_bias': 'new_v', 'new_v_gdn_conv': 'new_v', 'new_v_gdn_a_log': 'new_v', 'new_v_gdn_dt_bias': 'new_v', 'new_v_gdn_out_norm': 'new_v', 'new_v_w_out': 'new_v', 'new_v_ffn2_norm': 'new_v', 'new_v_ffn2_w_in': 'new_v', 'new_v_ffn2_w_out': 'new_v'}


def _forward(args):
    return _fwd_reference(*[args[k] for k in FWD_PARAMS])


def _output_shape():
    out = _jax.eval_shape(lambda: _forward(_fwd_setup_inputs(0)))
    return out.shape, out.dtype

N_MICROBATCH = 1
ADAM_LR = 0.001
ADAM_B1 = 0.9
ADAM_B2 = 0.999
ADAM_EPS = 1e-08
ADAM_WD = 0.01
ADAM_STEP = 10
PER_EXAMPLE_BATCH_AXIS = {'x': 0, 'loss_target': 0}
SHARED_INPUTS = []
_WEIGHT_DTYPES = {'ffn1_norm': _jnp.float32, 'ffn1_w_in': _jnp.float32, 'ffn1_w_out': _jnp.float32, 'mix_norm': _jnp.float32, 'w_in': _jnp.float32, 'fox_q_norm': _jnp.float32, 'fox_k_norm': _jnp.float32, 'fox_f_bias': _jnp.float32, 'gdn_conv': _jnp.float32, 'gdn_a_log': _jnp.float32, 'gdn_dt_bias': _jnp.float32, 'gdn_out_norm': _jnp.float32, 'w_out': _jnp.float32, 'ffn2_norm': _jnp.float32, 'ffn2_w_in': _jnp.float32, 'ffn2_w_out': _jnp.float32}
MOMENT_SCALE = {'ffn1_norm': 1.218801e+01, 'ffn1_w_in': 1.911381e-01, 'ffn1_w_out': 3.215264e-01, 'mix_norm': 1.695861e+01, 'w_in': 5.288097e-01, 'fox_q_norm': 2.248354e+01, 'fox_k_norm': 2.295467e+01, 'fox_f_bias': 1.193831e+02, 'gdn_conv': 1.326310e+00, 'gdn_a_log': 1.205565e+02, 'gdn_dt_bias': 1.136227e+02, 'gdn_out_norm': 9.138318e+01, 'w_out': 8.869221e-01, 'ffn2_norm': 1.232089e+01, 'ffn2_w_in': 1.407415e-01, 'ffn2_w_out': 2.374246e-01}


def _to_microbatches(a, axis):
    t = _jnp.moveaxis(a, axis, 0)
    t = t.reshape((N_MICROBATCH, t.shape[0] // N_MICROBATCH) + t.shape[1:])
    return _jnp.moveaxis(t, 1, axis + 1)


def setup_inputs(seed: int = 0) -> dict:
    inp = _fwd_setup_inputs(seed)
    key = _jax.random.fold_in(_jax.random.key(seed), 7919)
    shape, _ = _output_shape()
    out = dict(inp)
    out["loss_target"] = _jax.random.normal(_jax.random.fold_in(key, 0), shape, _jnp.float32)
    for i, name in enumerate(TWIN_WEIGHTS):
        w = inp[name].astype(_jnp.float32)
        if MOMENT_SCALE is None:
            s = _jnp.sqrt(_jnp.mean(_jnp.square(w)) + 1e-30)
        else:
            s = MOMENT_SCALE[name]
        km, kv = _jax.random.split(_jax.random.fold_in(key, i + 1))
        out[name] = w
        out["m_" + name] = s * _jax.random.normal(km, w.shape, _jnp.float32)
        out["v_" + name] = (s * s) * _jax.random.uniform(kv, w.shape, _jnp.float32, 0.5, 1.5)
    if N_MICROBATCH > 1:
        for name, axis in PER_EXAMPLE_BATCH_AXIS.items():
            out[name] = _to_microbatches(out[name], axis)
    return {'x': out['x'], 'ffn1_norm': out['ffn1_norm'], 'ffn1_w_in': out['ffn1_w_in'], 'ffn1_w_out': out['ffn1_w_out'], 'mix_norm': out['mix_norm'], 'w_in': out['w_in'], 'fox_q_norm': out['fox_q_norm'], 'fox_k_norm': out['fox_k_norm'], 'fox_f_bias': out['fox_f_bias'], 'gdn_conv': out['gdn_conv'], 'gdn_a_log': out['gdn_a_log'], 'gdn_dt_bias': out['gdn_dt_bias'], 'gdn_out_norm': out['gdn_out_norm'], 'w_out': out['w_out'], 'ffn2_norm': out['ffn2_norm'], 'ffn2_w_in': out['ffn2_w_in'], 'ffn2_w_out': out['ffn2_w_out'], 'loss_target': out['loss_target'], 'm_ffn1_norm': out['m_ffn1_norm'], 'm_ffn1_w_in': out['m_ffn1_w_in'], 'm_ffn1_w_out': out['m_ffn1_w_out'], 'm_mix_norm': out['m_mix_norm'], 'm_w_in': out['m_w_in'], 'm_fox_q_norm': out['m_fox_q_norm'], 'm_fox_k_norm': out['m_fox_k_norm'], 'm_fox_f_bias': out['m_fox_f_bias'], 'm_gdn_conv': out['m_gdn_conv'], 'm_gdn_a_log': out['m_gdn_a_log'], 'm_gdn_dt_bias': out['m_gdn_dt_bias'], 'm_gdn_out_norm': out['m_gdn_out_norm'], 'm_w_out': out['m_w_out'], 'm_ffn2_norm': out['m_ffn2_norm'], 'm_ffn2_w_in': out['m_ffn2_w_in'], 'm_ffn2_w_out': out['m_ffn2_w_out'], 'v_ffn1_norm': out['v_ffn1_norm'], 'v_ffn1_w_in': out['v_ffn1_w_in'], 'v_ffn1_w_out': out['v_ffn1_w_out'], 'v_mix_norm': out['v_mix_norm'], 'v_w_in': out['v_w_in'], 'v_fox_q_norm': out['v_fox_q_norm'], 'v_fox_k_norm': out['v_fox_k_norm'], 'v_fox_f_bias': out['v_fox_f_bias'], 'v_gdn_conv': out['v_gdn_conv'], 'v_gdn_a_log': out['v_gdn_a_log'], 'v_gdn_dt_bias': out['v_gdn_dt_bias'], 'v_gdn_out_norm': out['v_gdn_out_norm'], 'v_w_out': out['v_w_out'], 'v_ffn2_norm': out['v_ffn2_norm'], 'v_ffn2_w_in': out['v_ffn2_w_in'], 'v_ffn2_w_out': out['v_ffn2_w_out']}


def _loss(weights, diff, rest, loss_target):
    with _jax.named_scope("forward"):
        args = {**rest, TWIN_DIFF_INPUT: diff, **{k: w.astype(_WEIGHT_DTYPES[k]) for k, w in weights.items()}}
        y = _forward(args)
    with _jax.named_scope("loss_head"):
        err = _jnp.square(y.astype(_jnp.float32) - loss_target)
        return 0.5 * _jnp.sum(_jnp.mean(err, axis=-1)) if err.ndim else 0.5 * err


def _adamw(w, g, m, v):
    m = ADAM_B1 * m + (1.0 - ADAM_B1) * g
    v = ADAM_B2 * v + (1.0 - ADAM_B2) * _jnp.square(g)
    m_hat = m / (1.0 - ADAM_B1 ** ADAM_STEP)
    v_hat = v / (1.0 - ADAM_B2 ** ADAM_STEP)
    delta = -ADAM_LR * (m_hat / (_jnp.sqrt(v_hat) + ADAM_EPS) + ADAM_WD * w)
    return delta, m, v


def reference(x, ffn1_norm, ffn1_w_in, ffn1_w_out, mix_norm, w_in, fox_q_norm, fox_k_norm, fox_f_bias, gdn_conv, gdn_a_log, gdn_dt_bias, gdn_out_norm, w_out, ffn2_norm, ffn2_w_in, ffn2_w_out, loss_target, m_ffn1_norm, m_ffn1_w_in, m_ffn1_w_out, m_mix_norm, m_w_in, m_fox_q_norm, m_fox_k_norm, m_fox_f_bias, m_gdn_conv, m_gdn_a_log, m_gdn_dt_bias, m_gdn_out_norm, m_w_out, m_ffn2_norm, m_ffn2_w_in, m_ffn2_w_out, v_ffn1_norm, v_ffn1_w_in, v_ffn1_w_out, v_mix_norm, v_w_in, v_fox_q_norm, v_fox_k_norm, v_fox_f_bias, v_gdn_conv, v_gdn_a_log, v_gdn_dt_bias, v_gdn_out_norm, v_w_out, v_ffn2_norm, v_ffn2_w_in, v_ffn2_w_out):
    given = dict(x=x, ffn1_norm=ffn1_norm, ffn1_w_in=ffn1_w_in, ffn1_w_out=ffn1_w_out, mix_norm=mix_norm, w_in=w_in, fox_q_norm=fox_q_norm, fox_k_norm=fox_k_norm, fox_f_bias=fox_f_bias, gdn_conv=gdn_conv, gdn_a_log=gdn_a_log, gdn_dt_bias=gdn_dt_bias, gdn_out_norm=gdn_out_norm, w_out=w_out, ffn2_norm=ffn2_norm, ffn2_w_in=ffn2_w_in, ffn2_w_out=ffn2_w_out, loss_target=loss_target, m_ffn1_norm=m_ffn1_norm, m_ffn1_w_in=m_ffn1_w_in, m_ffn1_w_out=m_ffn1_w_out, m_mix_norm=m_mix_norm, m_w_in=m_w_in, m_fox_q_norm=m_fox_q_norm, m_fox_k_norm=m_fox_k_norm, m_fox_f_bias=m_fox_f_bias, m_gdn_conv=m_gdn_conv, m_gdn_a_log=m_gdn_a_log, m_gdn_dt_bias=m_gdn_dt_bias, m_gdn_out_norm=m_gdn_out_norm, m_w_out=m_w_out, m_ffn2_norm=m_ffn2_norm, m_ffn2_w_in=m_ffn2_w_in, m_ffn2_w_out=m_ffn2_w_out, v_ffn1_norm=v_ffn1_norm, v_ffn1_w_in=v_ffn1_w_in, v_ffn1_w_out=v_ffn1_w_out, v_mix_norm=v_mix_norm, v_w_in=v_w_in, v_fox_q_norm=v_fox_q_norm, v_fox_k_norm=v_fox_k_norm, v_fox_f_bias=v_fox_f_bias, v_gdn_conv=v_gdn_conv, v_gdn_a_log=v_gdn_a_log, v_gdn_dt_bias=v_gdn_dt_bias, v_gdn_out_norm=v_gdn_out_norm, v_w_out=v_w_out, v_ffn2_norm=v_ffn2_norm, v_ffn2_w_in=v_ffn2_w_in, v_ffn2_w_out=v_ffn2_w_out)
    weights = {n: given[n] for n in TWIN_WEIGHTS}
    shared = {n: given[n] for n in SHARED_INPUTS}
    per_example = {n: given[n] for n in ['x']}
    grad_fn = _jax.value_and_grad(_loss, argnums=(0, 1))

    def one_microbatch(ex, loss_target):
        ex = dict(ex)
        diff = ex.pop(TWIN_DIFF_INPUT)
        return grad_fn(weights, diff, {**shared, **ex}, loss_target)

    if N_MICROBATCH == 1:
        loss, (grad_w, grad_x) = one_microbatch(per_example, given["loss_target"])
    else:
        def body(carry, xs):
            loss_sum, grad_sum = carry
            l_k, (gw_k, gx_k) = one_microbatch(xs[0], xs[1])
            with _jax.named_scope("update"):
                return (loss_sum + l_k, _jax.tree.map(_jnp.add, grad_sum, gw_k)), gx_k

        init = (_jnp.zeros((), _jnp.float32), _jax.tree.map(_jnp.zeros_like, weights))
        (loss, grad_w), grad_x = _jax.lax.scan(body, init, (per_example, given["loss_target"]))
    with _jax.named_scope("update"):
        delta_w, new_m, new_v = {}, {}, {}
        for n in TWIN_WEIGHTS:
            delta_w[n], new_m[n], new_v[n] = _adamw(weights[n], grad_w[n], given["m_" + n], given["v_" + n])
    return (loss, grad_x, *[grad_w[n] for n in TWIN_WEIGHTS], *[delta_w[n] for n in TWIN_WEIGHTS],
            *[new_m[n] for n in TWIN_WEIGHTS], *[new_v[n] for n in TWIN_WEIGHTS])
```

```python
import jax
import jax.numpy as jnp
from jax import lax
from jax.experimental import pallas as pl
from jax.experimental.pallas import tpu as pltpu

F32 = jnp.float32
BF = jnp.bfloat16
HI = lax.Precision.HIGHEST
MESH = pl.DeviceIdType.MESH

DEPTH = 2
FOX_HEADS = 8
FOX_HEAD_DIM = 64
FOX_WIDTH = 512
GDN_HEADS = 4
GDN_HEAD_DIM = 128
GDN_WIDTH = 512
CONV_WIDTH = 4
CHUNK = 64
EPS = 1e-6
N_IN = 3600
N_PAD = 3712
GATE_COL = 3584
LANES = 128
NEG = -1e30

ADAM_LR = 0.001
ADAM_B1 = 0.9
ADAM_B2 = 0.999
ADAM_EPS = 1e-08
ADAM_WD = 0.01
ADAM_STEP = 10

VMEM_LIMIT = 56 * 1024 * 1024


def _params(sem=None, **kw):
    return pltpu.CompilerParams(dimension_semantics=sem, vmem_limit_bytes=VMEM_LIMIT, **kw)


def _dot(a, b, precision=None):
    return jnp.dot(a, b, preferred_element_type=F32, precision=precision)


def _dot_nt(a, b, precision=None):
    return lax.dot_general(a, b, (((1,), (1,)), ((), ())), preferred_element_type=F32, precision=precision)


def _dot_tn(a, b, precision=None):
    return lax.dot_general(a, b, (((0,), (0,)), ((), ())), preferred_element_type=F32, precision=precision)


def _sigmoid(x):
    return 1.0 / (1.0 + jnp.exp(-x))


def _softplus(x):
    return jnp.maximum(x, 0.0) + jnp.log(1.0 + jnp.exp(-jnp.abs(x)))


def _log_sigmoid(x):
    return jnp.minimum(x, 0.0) - jnp.log(1.0 + jnp.exp(-jnp.abs(x)))


def _tile(n, t):
    t = min(n, t)
    assert n % t == 0, (n, t)
    return t


def _rms_fwd(x, gain):
    rstd = lax.rsqrt(jnp.mean(x * x, axis=-1, keepdims=True) + EPS)
    xhat = x * rstd
    return xhat * gain, xhat, rstd


def _rms_bwd(dy, xhat, rstd, gain):
    dxhat = dy * gain
    dx = rstd * (dxhat - xhat * jnp.mean(dxhat * xhat, axis=-1, keepdims=True))
    return dx, dy * xhat


def _full(shape):
    nd = len(shape)
    return pl.BlockSpec(shape, lambda *_: (0,) * nd)


HBM = pl.BlockSpec(memory_space=pltpu.HBM)


def _load_ffn_weights(win_hbm, wout_hbm, layer, win_v, wout_v, sem):
    fr = wout_hbm.shape[2]
    copies = [pltpu.make_async_copy(win_hbm.at[s, layer], win_v.at[s], sem.at[s]) for s in range(4)]
    copies += [pltpu.make_async_copy(wout_hbm.at[s, layer], wout_v.at[pl.ds(s * fr, fr)], sem.at[4 + s])
               for s in range(4)]
    for c in copies:
        c.start()
    for c in copies:
        c.wait()


def _ffn_fwd(x, gain, win_g, wout_g, layer, name):
    t, d = x.shape
    _, _, _, fs = win_g.shape
    fr = wout_g.shape[2]
    tm = _tile(t, 256)

    def body(x_ref, g_ref, win_hbm, wout_hbm, xo_ref, h_ref, win_v, wout_v, sem):
        @pl.when(pl.program_id(0) == 0)
        def _():
            _load_ffn_weights(win_hbm, wout_hbm, layer, win_v, wout_v, sem)

        xv = x_ref[...]
        hn, _, _ = _rms_fwd(xv, g_ref[...])
        hn = hn.astype(BF)
        acc = jnp.zeros((tm, d), F32)
        for s in range(2):
            g = _dot(hn, win_v[s])
            u = _dot(hn, win_v[s + 2])
            h_ref[:, s * fs:(s + 1) * fs] = g.astype(BF)
            h_ref[:, (s + 2) * fs:(s + 3) * fs] = u.astype(BF)
            a = (g * _sigmoid(g) * u).astype(BF)
            acc = acc + _dot(a, wout_v[s * fs:(s + 1) * fs, :])
        xo_ref[...] = xv + 0.5 * acc

    return pl.pallas_call(
        body, name=name, grid=(t // tm,),
        in_specs=[pl.BlockSpec((tm, d), lambda i: (i, 0)), _full((1, d)), HBM, HBM],
        out_specs=[pl.BlockSpec((tm, d), lambda i: (i, 0)), pl.BlockSpec((tm, 4 * fs), lambda i: (i, 0))],
        out_shape=[jax.ShapeDtypeStruct((t, d), F32), jax.ShapeDtypeStruct((t, 4 * fs), BF)],
        scratch_shapes=[pltpu.VMEM((4, d, fs), BF), pltpu.VMEM((4 * fr, d), BF), pltpu.SemaphoreType.DMA((8,))],
        compiler_params=_params(("arbitrary",)),
    )(x, gain, win_g, wout_g)


def _ffn_bwd(dy, x, h, gain, win_g, wout_g, layer, name):
    t, d = x.shape
    _, _, _, fs = win_g.shape
    fr = wout_g.shape[2]
    tm = _tile(t, 256)

    def body(dy_ref, x_ref, h_ref, g_ref, win_hbm, wout_hbm,
             dx_ref, dh_ref, a_ref, hn_ref, dyh_ref, dg_ref, win_v, wout_v, sem):
        @pl.when(pl.program_id(0) == 0)
        def _():
            _load_ffn_weights(win_hbm, wout_hbm, layer, win_v, wout_v, sem)
            dg_ref[...] = jnp.zeros_like(dg_ref)

        dyv = dy_ref[...]
        dyh = (0.5 * dyv).astype(BF)
        dyh_ref[...] = dyh
        dhn = jnp.zeros((tm, d), F32)
        for s in range(2):
            da = _dot_nt(dyh, wout_v[s * fs:(s + 1) * fs, :])
            g = h_ref[:, s * fs:(s + 1) * fs].astype(F32)
            u = h_ref[:, (s + 2) * fs:(s + 3) * fs].astype(F32)
            sg = _sigmoid(g)
            si = g * sg
            a_ref[:, s * fs:(s + 1) * fs] = (si * u).astype(BF)
            dgate = (da * u * (sg * (1.0 + g * (1.0 - sg)))).astype(BF)
            dup = (da * si).astype(BF)
            dh_ref[:, s * fs:(s + 1) * fs] = dgate
            dh_ref[:, (s + 2) * fs:(s + 3) * fs] = dup
            dhn = dhn + _dot_nt(dgate, win_v[s]) + _dot_nt(dup, win_v[s + 2])
        xv = x_ref[...]
        gain_v = g_ref[...]
        hn, xhat, rstd = _rms_fwd(xv, gain_v)
        hn_ref[...] = hn.astype(BF)
        dx, dgr = _rms_bwd(dhn, xhat, rstd, gain_v)
        dx_ref[...] = dyv + dx
        dg_ref[...] += jnp.sum(dgr, axis=0, keepdims=True)

    row = lambda w: pl.BlockSpec((tm, w), lambda i: (i, 0))
    return pl.pallas_call(
        body, name=name, grid=(t // tm,),
        in_specs=[row(d), row(d), row(4 * fs), _full((1, d)), HBM, HBM],
        out_specs=[row(d), row(4 * fs), row(2 * fs), row(d), row(d), _full((1, d))],
        out_shape=[jax.ShapeDtypeStruct((t, d), F32), jax.ShapeDtypeStruct((t, 4 * fs), BF),
                   jax.ShapeDtypeStruct((t, 2 * fs), BF), jax.ShapeDtypeStruct((t, d), BF),
                   jax.ShapeDtypeStruct((t, d), BF), jax.ShapeDtypeStruct((1, d), F32)],
        scratch_shapes=[pltpu.VMEM((4, d, fs), BF), pltpu.VMEM((4 * fr, d), BF), pltpu.SemaphoreType.DMA((8,))],
        compiler_params=_params(("arbitrary",)),
    )(dy, x, h, gain, win_g, wout_g)


def _wgrad(a, b, out_shape, out_spec, tm, tn, name, tk=512):
    t, m = a.shape
    _, n = b.shape
    tk = _tile(t, tk)
    nk = t // tk

    def body(a_ref, b_ref, o_ref, acc):
        k = pl.program_id(2)

        @pl.when(k == 0)
        def _():
            acc[...] = jnp.zeros_like(acc)

        acc[...] += _dot_tn(a_ref[...], b_ref[...])

        @pl.when(k == nk - 1)
        def _():
            o_ref[...] = acc[...].astype(o_ref.dtype)

    return pl.pallas_call(
        body, name=name, grid=(m // tm, n // tn, nk),
        in_specs=[pl.BlockSpec((tk, tm), lambda i, j, k: (k, i)), pl.BlockSpec((tk, tn), lambda i, j, k: (k, j))],
        out_specs=out_spec, out_shape=out_shape,
        scratch_shapes=[pltpu.VMEM((tm, tn), F32)],
        compiler_params=_params(("parallel", "parallel", "arbitrary")),
    )(a, b)


def _norm_matmul(x, gain, w, name):
    t, d = x.shape
    n = w.shape[1]
    tm = _tile(t, 256)

    def body(x_ref, g_ref, w_ref, o_ref):
        hn, _, _ = _rms_fwd(x_ref[...], g_ref[...])
        o_ref[...] = _dot(hn.astype(BF), w_ref[...])

    return pl.pallas_call(
        body, name=name, grid=(t // tm,),
        in_specs=[pl.BlockSpec((tm, d), lambda i: (i, 0)), _full((1, d)), _full((d, n))],
        out_specs=pl.BlockSpec((tm, n), lambda i: (i, 0)),
        out_shape=jax.ShapeDtypeStruct((t, n), F32),
        compiler_params=_params(("parallel",)),
    )(x, gain, w)


def _norm_matmul_bwd(dres, dproj, x, gain, w, name):
    t, d = x.shape
    n = w.shape[1]
    tm = _tile(t, 256)

    def body(dr_ref, dp_ref, x_ref, g_ref, w_ref, dx_ref, hn_ref, dg_ref):
        @pl.when(pl.program_id(0) == 0)
        def _():
            dg_ref[...] = jnp.zeros_like(dg_ref)

        dhn = _dot_nt(dp_ref[...], w_ref[...])
        gain_v = g_ref[...]
        hn, xhat, rstd = _rms_fwd(x_ref[...], gain_v)
        hn_ref[...] = hn.astype(BF)
        dx, dgr = _rms_bwd(dhn, xhat, rstd, gain_v)
        dx_ref[...] = dr_ref[...] + dx
        dg_ref[...] += jnp.sum(dgr, axis=0, keepdims=True)

    row = lambda wd: pl.BlockSpec((tm, wd), lambda i: (i, 0))
    return pl.pallas_call(
        body, name=name, grid=(t // tm,),
        in_specs=[row(d), row(n), row(d), _full((1, d)), _full((d, n))],
        out_specs=[row(d), row(d), _full((1, d))],
        out_shape=[jax.ShapeDtypeStruct((t, d), F32), jax.ShapeDtypeStruct((t, d), BF),
                   jax.ShapeDtypeStruct((1, d), F32)],
        compiler_params=_params(("arbitrary",)),
    )(dres, dproj, x, gain, w)


def _mix_out(x, yf, yg, w, name):
    t, d = x.shape
    kf = yf.shape[1]
    tm = _tile(t, 512)

    def body(x_ref, yf_ref, yg_ref, w_ref, o_ref):
        o_ref[...] = x_ref[...] + _dot(yf_ref[...], w_ref[0:kf, :]) + _dot(yg_ref[...], w_ref[kf:2 * kf, :])

    row = lambda wd: pl.BlockSpec((tm, wd), lambda i: (i, 0))
    return pl.pallas_call(
        body, name=name, grid=(t // tm,),
        in_specs=[row(d), row(kf), row(kf), _full((2 * kf, d))],
        out_specs=row(d), out_shape=jax.ShapeDtypeStruct((t, d), F32),
        compiler_params=_params(("parallel",)),
    )(x, yf, yg, w)


def _mix_out_bwd(dx, w, name):
    t, d = dx.shape
    kf = w.shape[0] // 2
    tm = _tile(t, 512)

    def body(dx_ref, w_ref, df_ref, dg_ref, dxb_ref):
        dxb = dx_ref[...].astype(BF)
        dxb_ref[...] = dxb
        df_ref[...] = _dot_nt(dxb, w_ref[0:kf, :]).astype(BF)
        dg_ref[...] = _dot_nt(dxb, w_ref[kf:2 * kf, :]).astype(BF)

    row = lambda wd: pl.BlockSpec((tm, wd), lambda i: (i, 0))
    return pl.pallas_call(
        body, name=name, grid=(t // tm,),
        in_specs=[row(d), _full((2 * kf, d))],
        out_specs=[row(kf), row(kf), row(d)],
        out_shape=[jax.ShapeDtypeStruct((t, kf), BF), jax.ShapeDtypeStruct((t, kf), BF),
                   jax.ShapeDtypeStruct((t, d), BF)],
        compiler_params=_params(("parallel",)),
    )(dx, w)


def _loss_grad(y, target, name):
    t, d = y.shape
    tm = _tile(t, 512)

    def body(y_ref, t_ref, l_ref, dy_ref):
        @pl.when(pl.program_id(0) == 0)
        def _():
            l_ref[...] = jnp.zeros_like(l_ref)

        diff = y_ref[...] - t_ref[...]
        dy_ref[...] = diff * (1.0 / d)
        part = jnp.sum(jnp.sum(diff * diff, axis=1, keepdims=True), axis=0, keepdims=True)
        l_ref[...] += part * (0.5 / d)

    row = pl.BlockSpec((tm, d), lambda i: (i, 0))
    return pl.pallas_call(
        body, name=name, grid=(t // tm,),
        in_specs=[row, row], out_specs=[_full((1, 1)), row],
        out_shape=[jax.ShapeDtypeStruct((1, 1), F32), jax.ShapeDtypeStruct((t, d), F32)],
        compiler_params=_params(("arbitrary",)),
    )(y, target)


def _head_sum_matrix(width, head):
    r = lax.broadcasted_iota(jnp.int32, (width, width), 0) // head
    c = lax.broadcasted_iota(jnp.int32, (width, width), 1) // head
    return (r == c).astype(F32)


def _fox_prep(proj, wq_t, wk_t, bias_pad, seq, name):
    t = proj.shape[0]
    ts = _tile(seq, 512)
    tpe = seq // ts
    scale = FOX_HEAD_DIM ** -0.5

    def body(q_ref, k_ref, v_ref, gt_ref, wq_ref, wk_ref, b_ref, qo_ref, ko_ref, vo_ref, cum_ref, carry):
        i = pl.program_id(0)
        bd = _head_sum_matrix(FOX_WIDTH, FOX_HEAD_DIM)

        def norm(xv, wv):
            ms = _dot(xv * xv, bd, HI) * (1.0 / FOX_HEAD_DIM)
            return xv * lax.rsqrt(ms + EPS) * wv

        qo_ref[...] = (norm(q_ref[...], wq_ref[...]) * scale).astype(BF)
        ko_ref[...] = norm(k_ref[...], wk_ref[...]).astype(BF)
        vo_ref[...] = v_ref[...].astype(BF)

        @pl.when(i % tpe == 0)
        def _():
            carry[...] = jnp.zeros_like(carry)

        ls = _log_sigmoid(gt_ref[...] + b_ref[...])
        r = lax.broadcasted_iota(jnp.int32, (ts, ts), 0)
        c = lax.broadcasted_iota(jnp.int32, (ts, ts), 1)
        cum = _dot((r >= c).astype(F32), ls, HI) + carry[...]
        cum_ref[...] = cum
        carry[...] = cum[ts - 1:ts, :]

    blk = lambda j: pl.BlockSpec((ts, FOX_WIDTH), lambda i: (i, j))
    gate = pl.BlockSpec((ts, LANES), lambda i: (i, GATE_COL // LANES))
    out = pl.BlockSpec((ts, FOX_WIDTH), lambda i: (i, 0))
    return pl.pallas_call(
        body, name=name, grid=(t // ts,),
        in_specs=[blk(0), blk(1), blk(2), gate, _full((1, FOX_WIDTH)), _full((1, FOX_WIDTH)), _full((1, LANES))],
        out_specs=[out, out, out, pl.BlockSpec((ts, LANES), lambda i: (i, 0))],
        out_shape=[jax.ShapeDtypeStruct((t, FOX_WIDTH), BF)] * 3 + [jax.ShapeDtypeStruct((t, LANES), F32)],
        scratch_shapes=[pltpu.VMEM((1, LANES), F32)],
        compiler_params=_params(("arbitrary",)),
    )(proj, proj, proj, proj, wq_t, wk_t, bias_pad)


def _fox_prep_bwd(proj, dq, dk, dv, dcum, wq_t, wk_t, bias_pad, seq, name):
    t = proj.shape[0]
    ts = _tile(seq, 512)
    tpe = seq // ts
    nt = t // ts
    scale = FOX_HEAD_DIM ** -0.5

    def body(q_ref, k_ref, gt_ref, dq_ref, dk_ref, dv_ref, dc_ref, wq_ref, wk_ref, b_ref,
             dp_ref, dff_ref, dwq_ref, dwk_ref, db_ref, carry):
        i = pl.program_id(0)
        ti = nt - 1 - i
        bd = _head_sum_matrix(FOX_WIDTH, FOX_HEAD_DIM)

        @pl.when(i == 0)
        def _():
            dwq_ref[...] = jnp.zeros_like(dwq_ref)
            dwk_ref[...] = jnp.zeros_like(dwk_ref)
            db_ref[...] = jnp.zeros_like(db_ref)

        def norm_bwd(xv, wv, dyv):
            ms = _dot(xv * xv, bd, HI) * (1.0 / FOX_HEAD_DIM)
            rstd = lax.rsqrt(ms + EPS)
            xhat = xv * rstd
            dxhat = dyv * wv
            mean = _dot(dxhat * xhat, bd, HI) * (1.0 / FOX_HEAD_DIM)
            return rstd * (dxhat - xhat * mean), jnp.sum(dyv * xhat, axis=0, keepdims=True)

        dxq, dwq = norm_bwd(q_ref[...], wq_ref[...], dq_ref[...] * scale)
        dxk, dwk = norm_bwd(k_ref[...], wk_ref[...], dk_ref[...])
        dp_ref[:, 0:FOX_WIDTH] = dxq.astype(BF)
        dp_ref[:, FOX_WIDTH:2 * FOX_WIDTH] = dxk.astype(BF)
        dp_ref[:, 2 * FOX_WIDTH:3 * FOX_WIDTH] = dv_ref[...].astype(BF)
        dwq_ref[...] += dwq
        dwk_ref[...] += dwk

        @pl.when(ti % tpe == tpe - 1)
        def _():
            carry[...] = jnp.zeros_like(carry)

        r = lax.broadcasted_iota(jnp.int32, (ts, ts), 0)
        c = lax.broadcasted_iota(jnp.int32, (ts, ts), 1)
        dls = _dot((c >= r).astype(F32), dc_ref[...], HI) + carry[...]
        carry[...] = dls[0:1, :]
        z = gt_ref[...] + b_ref[...]
        lane = lax.broadcasted_iota(jnp.int32, (ts, LANES), 1)
        dff = jnp.where(lane < FOX_HEADS, dls * _sigmoid(-z), 0.0)
        dff_ref[...] = dff
        db_ref[...] += jnp.sum(dff, axis=0, keepdims=True)

        @pl.when(i == nt - 1)
        def _():
            fr = lax.broadcasted_iota(jnp.int32, (FOX_WIDTH, FOX_WIDTH), 0) % FOX_HEAD_DIM
            fc = lax.broadcasted_iota(jnp.int32, (FOX_WIDTH, FOX_WIDTH), 1) % FOX_HEAD_DIM
            fold = (fr == fc).astype(F32)
            dwq_ref[...] = _dot(dwq_ref[...], fold, HI)
            dwk_ref[...] = _dot(dwk_ref[...], fold, HI)

    rev = lambda w, j: pl.BlockSpec((ts, w), lambda i: (nt - 1 - i, j))
    return pl.pallas_call(
        body, name=name, grid=(nt,),
        in_specs=[rev(FOX_WIDTH, 0), rev(FOX_WIDTH, 1), rev(LANES, GATE_COL // LANES),
                  rev(FOX_WIDTH, 0), rev(FOX_WIDTH, 0), rev(FOX_WIDTH, 0), rev(LANES, 0),
                  _full((1, FOX_WIDTH)), _full((1, FOX_WIDTH)), _full((1, LANES))],
        out_specs=[rev(3 * FOX_WIDTH, 0), rev(LANES, 0), _full((1, FOX_WIDTH)), _full((1, FOX_WIDTH)),
                   _full((1, LANES))],
        out_shape=[jax.ShapeDtypeStruct((t, 3 * FOX_WIDTH), BF), jax.ShapeDtypeStruct((t, LANES), F32),
                   jax.ShapeDtypeStruct((1, FOX_WIDTH), F32), jax.ShapeDtypeStruct((1, FOX_WIDTH), F32),
                   jax.ShapeDtypeStruct((1, LANES), F32)],
        scratch_shapes=[pltpu.VMEM((1, LANES), F32)],
        compiler_params=_params(("arbitrary",)),
    )(proj, proj, proj, dq, dk, dv, dcum, wq_t, wk_t, bias_pad)


def _fox_attn(q, k, v, cq, ck, nb, seq, name):
    t = q.shape[0]
    tq = _tile(seq, 512)
    nq = seq // tq
    npair = FOX_HEADS // 2
    hd = FOX_HEAD_DIM

    def body(q_ref, k_ref, v_ref, cq_ref, ck_ref, o_ref, lse_ref, m_s, l_s, acc_s):
        qi = pl.program_id(2)
        kj = pl.program_id(3)
        lane = lax.broadcasted_iota(jnp.int32, (tq, LANES), 1)

        @pl.when(kj == 0)
        def _():
            m_s[...] = jnp.full(m_s.shape, NEG, F32)
            l_s[...] = jnp.zeros_like(l_s)
            acc_s[...] = jnp.zeros_like(acc_s)

        @pl.when(kj <= qi)
        def _():
            qv = q_ref[...]
            kv = k_ref[...]
            vv = v_ref[...]
            row = qi * tq + lax.broadcasted_iota(jnp.int32, (tq, tq), 0)
            col = kj * tq + lax.broadcasted_iota(jnp.int32, (tq, tq), 1)
            causal = row >= col
            for hh in range(2):
                hm = (lane >= hd) if hh else (lane < hd)
                qh = jnp.where(hm, qv, jnp.zeros_like(qv))
                s = _dot_nt(qh, kv) + (cq_ref[hh] - ck_ref[hh])
                s = jnp.where(causal, s, NEG)
                m_old = m_s[hh]
                m_new = jnp.maximum(m_old, jnp.max(s, axis=-1, keepdims=True))
                p = jnp.exp(s - m_new)
                alpha = jnp.exp(m_old - m_new)
                l_s[hh] = alpha * l_s[hh] + jnp.sum(p, axis=-1, keepdims=True)
                m_s[hh] = m_new
                pv = _dot(p.astype(BF), vv)
                acc = acc_s[...]
                acc_s[...] = jnp.where(hm, alpha * acc + pv, acc)

        @pl.when(kj == qi)
        def _():
            inv = jnp.where(lane < hd, 1.0 / l_s[0], 1.0 / l_s[1])
            o_ref[...] = (acc_s[...] * inv).astype(o_ref.dtype)
            lse_ref[0] = m_s[0] + jnp.log(l_s[0])
            lse_ref[1] = m_s[1] + jnp.log(l_s[1])

    qspec = pl.BlockSpec((tq, LANES), lambda b, p, i, j: (b * nq + i, p))
    kspec = pl.BlockSpec((tq, LANES), lambda b, p, i, j: (b * nq + jnp.minimum(j, i), p))
    colspec = pl.BlockSpec((None, 2, tq, 1), lambda b, p, i, j: (b * npair + p, 0, i, 0))
    rowspec = pl.BlockSpec((None, 2, 1, tq), lambda b, p, i, j: (b * npair + p, 0, 0, jnp.minimum(j, i)))
    return pl.pallas_call(
        body, name=name, grid=(nb, npair, nq, nq),
        in_specs=[qspec, kspec, kspec, colspec, rowspec],
        out_specs=[qspec, colspec],
        out_shape=[jax.ShapeDtypeStruct((t, FOX_WIDTH), BF), jax.ShapeDtypeStruct((nb * npair, 2, seq, 1), F32)],
        scratch_shapes=[pltpu.VMEM((2, tq, 1), F32), pltpu.VMEM((2, tq, 1), F32), pltpu.VMEM((tq, LANES), F32)],
        compiler_params=_params(("parallel", "parallel", "parallel", "arbitrary")),
    )(q, k, v, cq, ck)


def _fox_attn_bwd(q, k, v, o, do, lse, cq, ck, nb, seq, name):
    t = q.shape[0]
    tq = _tile(seq, 512)
    nq = seq // tq
    npair = FOX_HEADS // 2
    hd = FOX_HEAD_DIM

    def body(q_ref, k_ref, v_ref, o_ref, do_ref, lse_ref, cq_ref, ck_ref,
             dq_ref, dk_ref, dv_ref, dcq_ref, dck_ref, dk_s, dv_s, dck_s):
        kj = pl.program_id(2)
        qi = pl.program_id(3)
        lane = lax.broadcasted_iota(jnp.int32, (tq, LANES), 1)

        @pl.when((kj == 0) & (qi == 0))
        def _():
            dq_ref[...] = jnp.zeros_like(dq_ref)
            dcq_ref[...] = jnp.zeros_like(dcq_ref)

        @pl.when(qi == 0)
        def _():
            dk_s[...] = jnp.zeros_like(dk_s)
            dv_s[...] = jnp.zeros_like(dv_s)
            dck_s[...] = jnp.zeros_like(dck_s)

        @pl.when(qi >= kj)
        def _():
            qv = q_ref[...]
            kv = k_ref[...]
            vv = v_ref[...]
            dov = do_ref[...]
            prod = dov.astype(F32) * o_ref[...].astype(F32)
            row = qi * tq + lax.broadcasted_iota(jnp.int32, (tq, tq), 0)
            col = kj * tq + lax.broadcasted_iota(jnp.int32, (tq, tq), 1)
            causal = row >= col
            rows = pl.ds(pl.multiple_of(qi * tq, tq), tq)
            dq_new = jnp.zeros((tq, LANES), F32)
            rs_new = jnp.zeros((tq, LANES), F32)
            for hh in range(2):
                hm = (lane >= hd) if hh else (lane < hd)
                zero = jnp.zeros_like(qv)
                qh = jnp.where(hm, qv, zero)
                kh = jnp.where(hm, kv, zero)
                doh = jnp.where(hm, dov, zero)
                delta = jnp.sum(jnp.where(hm, prod, 0.0), axis=-1, keepdims=True)
                s = _dot_nt(qh, kv) + (cq_ref[hh] - ck_ref[hh])
                s = jnp.where(causal, s, NEG)
                p = jnp.exp(s - lse_ref[hh])
                dp = _dot_nt(doh, vv)
                ds = p * (dp - delta)
                dsb = ds.astype(BF)
                dv_s[...] += _dot_tn(p.astype(BF), doh)
                dk_s[...] += _dot_tn(dsb, qh)
                dq_new = dq_new + _dot(dsb, kh)
                rs_new = rs_new + jnp.where(hm, jnp.sum(ds, axis=-1, keepdims=True), 0.0)
                dck_s[hh] += jnp.sum(ds, axis=0, keepdims=True)
            dq_ref[rows, :] += dq_new
            dcq_ref[rows, :] += rs_new

        @pl.when(qi == nq - 1)
        def _():
            dk_ref[...] = dk_s[...]
            dv_ref[...] = dv_s[...]
            dck_ref[...] = dck_s[...]

    kspec = pl.BlockSpec((tq, LANES), lambda b, p, j, i: (b * nq + j, p))
    qspec = pl.BlockSpec((tq, LANES), lambda b, p, j, i: (b * nq + jnp.maximum(i, j), p))
    colspec = pl.BlockSpec((None, 2, tq, 1), lambda b, p, j, i: (b * npair + p, 0, jnp.maximum(i, j), 0))
    rowspec = pl.BlockSpec((None, 2, 1, tq), lambda b, p, j, i: (b * npair + p, 0, 0, j))
    full_q = pl.BlockSpec((seq, LANES), lambda b, p, j, i: (b, p))
    return pl.pallas_call(
        body, name=name, grid=(nb, npair, nq, nq),
        in_specs=[qspec, kspec, kspec, qspec, qspec, colspec, colspec, rowspec],
        out_specs=[full_q, kspec, kspec, full_q, rowspec],
        out_shape=[jax.ShapeDtypeStruct((t, FOX_WIDTH), F32), jax.ShapeDtypeStruct((t, FOX_WIDTH), F32),
                   jax.ShapeDtypeStruct((t, FOX_WIDTH), F32), jax.ShapeDtypeStruct((t, FOX_WIDTH), F32),
                   jax.ShapeDtypeStruct((nb * npair, 2, 1, seq), F32)],
        scratch_shapes=[pltpu.VMEM((tq, LANES), F32), pltpu.VMEM((tq, LANES), F32), pltpu.VMEM((2, 1, tq), F32)],
        compiler_params=_params(("parallel", "parallel", "arbitrary", "arbitrary")),
    )(q, k, v, o, do, lse, cq, ck)


GDN_QKV = 3 * GDN_WIDTH
GDN_COL = 3 * FOX_WIDTH
GG_COL = GDN_COL + GDN_QKV
A_LANE = FOX_HEADS
B_LANE = FOX_HEADS + GDN_HEADS
HALO = 8


def _gate_lanes(ts):
    lane = lax.broadcasted_iota(jnp.int32, (ts, LANES), 1)
    return (lane >= A_LANE) & (lane < B_LANE), (lane >= B_LANE) & (lane < B_LANE + GDN_HEADS)


def _chunk_tri(ts, upper):
    r = lax.broadcasted_iota(jnp.int32, (ts, ts), 0)
    c = lax.broadcasted_iota(jnp.int32, (ts, ts), 1)
    same = (r // CHUNK) == (c // CHUNK)
    return (same & ((c >= r) if upper else (r >= c))).astype(F32)


def _conv_silu_l2(xp_ref, w, ts):
    c = w[0:1, :] * xp_ref[pl.ds(HALO - 3, ts), :]
    for kk in range(1, CONV_WIDTH):
        c = c + w[kk:kk + 1, :] * xp_ref[pl.ds(HALO - 3 + kk, ts), :]
    return c, c * _sigmoid(c)


def _gdn_prep(proj, conv_w, a_pad, dt_pad, seq, name):
    t = proj.shape[0]
    ts = _tile(seq, 256)
    tpe = seq // ts
    qscale = GDN_HEAD_DIM ** -0.5

    def body(x_ref, gt_ref, w_ref, a_ref, dt_ref, qo_ref, ko_ref, vo_ref, go_ref, xp):
        i = pl.program_id(0)
        tail = xp[pl.ds(ts, HALO), :]
        xp[pl.ds(0, HALO), :] = jnp.where(i % tpe == 0, jnp.zeros_like(tail), tail)
        xp[pl.ds(HALO, ts), :] = x_ref[...]
        _, s = _conv_silu_l2(xp, w_ref[...], ts)
        for h in range(GDN_HEADS):
            for base, ref, sc in ((0, qo_ref, qscale), (GDN_WIDTH, ko_ref, 1.0)):
                xh = s[:, base + h * LANES: base + (h + 1) * LANES]
                r = lax.rsqrt(jnp.sum(xh * xh, axis=-1, keepdims=True) + EPS)
                ref[:, h * LANES:(h + 1) * LANES] = (xh * (r * sc)).astype(BF)
        vo_ref[...] = s[:, 2 * GDN_WIDTH:].astype(BF)
        gate = gt_ref[...]
        g_raw = -jnp.exp(a_ref[...]) * _softplus(gate + dt_ref[...])
        gc = _dot(_chunk_tri(ts, False), g_raw, HI)
        is_a, is_b = _gate_lanes(ts)
        go_ref[...] = jnp.where(is_a, gc, jnp.where(is_b, _sigmoid(gate), 0.0))

    out = pl.BlockSpec((ts, GDN_WIDTH), lambda i: (i, 0))
    lanes = pl.BlockSpec((ts, LANES), lambda i: (i, 0))
    return pl.pallas_call(
        body, name=name, grid=(t // ts,),
        in_specs=[pl.BlockSpec((ts, GDN_QKV), lambda i: (i, GDN_COL // GDN_QKV)),
                  pl.BlockSpec((ts, LANES), lambda i: (i, GATE_COL // LANES)),
                  _full((CONV_WIDTH, GDN_QKV)), _full((1, LANES)), _full((1, LANES))],
        out_specs=[out, out, out, lanes],
        out_shape=[jax.ShapeDtypeStruct((t, GDN_WIDTH), BF)] * 3 + [jax.ShapeDtypeStruct((t, LANES), F32)],
        scratch_shapes=[pltpu.VMEM((ts + HALO, GDN_QKV), F32)],
        compiler_params=_params(("arbitrary",)),
    )(proj, proj, conv_w, a_pad, dt_pad)


def _gdn_prep_bwd(proj, dq, dk, dv, dgates, dff, conv_w, a_pad, dt_pad, seq, name):
    t = proj.shape[0]
    ts = _tile(seq, 256)
    tpe = seq // ts
    nt = t // ts
    qscale = GDN_HEAD_DIM ** -0.5
    hb = ts // HALO

    def body(x_ref, halo_ref, gt_ref, dq_ref, dk_ref, dv_ref, dgt_ref, dff_ref, w_ref, a_ref, dt_ref,
             dx_ref, dgo_ref, dw_ref, da_ref, ddt_ref, xp, dcp, carry):
        i = pl.program_id(0)
        ti = nt - 1 - i

        @pl.when(i == 0)
        def _():
            dw_ref[...] = jnp.zeros_like(dw_ref)
            da_ref[...] = jnp.zeros_like(da_ref)
            ddt_ref[...] = jnp.zeros_like(ddt_ref)

        halo = halo_ref[...]
        xp[pl.ds(0, HALO), :] = jnp.where(ti % tpe == 0, jnp.zeros_like(halo), halo)
        xp[pl.ds(HALO, ts), :] = x_ref[...]
        w = w_ref[...]
        c, s = _conv_silu_l2(xp, w, ts)
        for h in range(GDN_HEADS):
            for base, ref, sc in ((0, dq_ref, qscale), (GDN_WIDTH, dk_ref, 1.0)):
                lo = base + h * LANES
                xh = s[:, lo:lo + LANES]
                r = lax.rsqrt(jnp.sum(xh * xh, axis=-1, keepdims=True) + EPS)
                y = xh * r
                dy = ref[:, h * LANES:(h + 1) * LANES] * sc
                dcp[pl.ds(0, ts), lo:lo + LANES] = r * (dy - y * jnp.sum(dy * y, axis=-1, keepdims=True))
        dcp[pl.ds(0, ts), 2 * GDN_WIDTH:] = dv_ref[...]
        sg = _sigmoid(c)
        dc = dcp[pl.ds(0, ts), :] * (sg * (1.0 + c * (1.0 - sg)))
        dcp[pl.ds(0, ts), :] = dc
        nxt = carry[...]
        dcp[pl.ds(ts, HALO), :] = jnp.where(ti % tpe == tpe - 1, jnp.zeros_like(nxt), nxt)
        carry[...] = dc[0:HALO, :]
        dx = w[CONV_WIDTH - 1:CONV_WIDTH, :] * dc
        for kk in range(CONV_WIDTH - 1):
            dx = dx + w[kk:kk + 1, :] * dcp[pl.ds(CONV_WIDTH - 1 - kk, ts), :]
        dx_ref[...] = dx.astype(BF)
        for kk in range(CONV_WIDTH):
            dw_ref[kk:kk + 1, :] += jnp.sum(dc * xp[pl.ds(HALO - 3 + kk, ts), :], axis=0, keepdims=True)
        gate = gt_ref[...]
        dgt = dgt_ref[...]
        is_a, is_b = _gate_lanes(ts)
        dg_raw = _dot(_chunk_tri(ts, True), jnp.where(is_a, dgt, 0.0), HI)
        z = gate + dt_ref[...]
        na = -jnp.exp(a_ref[...])
        dga = dg_raw * na * _sigmoid(z)
        beta = _sigmoid(gate)
        dgb = jnp.where(is_b, dgt * beta * (1.0 - beta), 0.0)
        dgo_ref[...] = (dff_ref[...] + dga + dgb).astype(BF)
        ddt_ref[...] += jnp.sum(dga, axis=0, keepdims=True)
        da_ref[...] += jnp.sum(dg_raw * na * _softplus(z), axis=0, keepdims=True)

    rev = lambda wd, j: pl.BlockSpec((ts, wd), lambda i: (nt - 1 - i, j))
    halo_spec = pl.BlockSpec((HALO, GDN_QKV), lambda i: (jnp.maximum((nt - 1 - i) * hb - 1, 0), GDN_COL // GDN_QKV))
    return pl.pallas_call(
        body, name=name, grid=(nt,),
        in_specs=[rev(GDN_QKV, GDN_COL // GDN_QKV), halo_spec, rev(LANES, GATE_COL // LANES),
                  rev(GDN_WIDTH, 0), rev(GDN_WIDTH, 0), rev(GDN_WIDTH, 0), rev(LANES, 0), rev(LANES, 0),
                  _full((CONV_WIDTH, GDN_QKV)), _full((1, LANES)), _full((1, LANES))],
        out_specs=[rev(GDN_QKV, 0), rev(LANES, 0), _full((CONV_WIDTH, GDN_QKV)), _full((1, LANES)),
                   _full((1, LANES))],
        out_shape=[jax.ShapeDtypeStruct((t, GDN_QKV), BF), jax.ShapeDtypeStruct((t, LANES), BF),
                   jax.ShapeDtypeStruct((CONV_WIDTH, GDN_QKV), F32), jax.ShapeDtypeStruct((1, LANES), F32),
                   jax.ShapeDtypeStruct((1, LANES), F32)],
        scratch_shapes=[pltpu.VMEM((ts + HALO, GDN_QKV), F32), pltpu.VMEM((ts + HALO, GDN_QKV), F32),
                        pltpu.VMEM((HALO, GDN_QKV), F32)],
        compiler_params=_params(("arbitrary",)),
    )(proj, proj, proj, dq, dk, dv, dgates, dff, conv_w, a_pad, dt_pad)


def _inv_unit_lower(a):
    r = lax.broadcasted_iota(jnp.int32, (CHUNK, CHUNK), 0)
    c = lax.broadcasted_iota(jnp.int32, (CHUNK, CHUNK), 1)
    tm = (r == c).astype(F32) - a
    pw = a
    for _ in range(5):
        pw = _dot(pw, pw, HI)
        tm = tm + _dot(tm, pw, HI)
    return tm


def _gdn_chunk_local(q, k, v, gc, gr, b):
    r = lax.broadcasted_iota(jnp.int32, (CHUNK, CHUNK), 0)
    c = lax.broadcasted_iota(jnp.int32, (CHUNK, CHUNK), 1)
    incl = r >= c
    strict = r > c
    dm = jnp.exp(jnp.where(incl, gc - gr, NEG))
    e = jnp.exp(gc)
    kb = k * b
    vb = v * b
    kbe = kb * e
    amat = jnp.where(strict, _dot_nt(kb.astype(BF), k.astype(BF)) * dm, 0.0)
    pmat = jnp.where(incl, _dot_nt(q.astype(BF), k.astype(BF)) * dm, 0.0)
    lane = lax.broadcasted_iota(jnp.int32, (1, CHUNK), 1)
    gl = jnp.sum(jnp.where(lane == CHUNK - 1, gr, 0.0), axis=1, keepdims=True)
    kd = k * jnp.exp(gl - gc)
    qd = q * e
    return dict(dm=dm, e=e, kb=kb, vb=vb, kbe=kbe, amat=amat, pmat=pmat, gl=gl, kd=kd, qd=qd,
                incl=incl, strict=strict)


def _gdn_specs(nb, seq):
    n = seq // CHUNK
    blk = pl.BlockSpec((seq, LANES), lambda i: (i // GDN_HEADS, i % GDN_HEADS))
    gg = pl.BlockSpec((seq, LANES), lambda i: (i // GDN_HEADS, GG_COL // LANES + i % GDN_HEADS))
    col = pl.BlockSpec((None, seq, 1), lambda i: (i, 0, 0))
    rowb = pl.BlockSpec((None, n, HALO, CHUNK), lambda i: (i, 0, 0, 0))
    return n, blk, gg, col, rowb


def _gdn_fwd(q, k, v, proj, gcol, bcol, grow, wn, nb, seq, name):
    t = q.shape[0]
    n, blk, gg, col, rowb = _gdn_specs(nb, seq)

    def body(q_ref, k_ref, v_ref, gg_ref, gc_ref, b_ref, gr_ref, wn_ref, y_ref, s_ref):
        s_ref[...] = jnp.zeros_like(s_ref)
        wnv = wn_ref[...]

        def step(ci, carry):
            rows = pl.ds(pl.multiple_of(ci * CHUNK, CHUNK), CHUNK)
            qv = q_ref[rows, :].astype(F32)
            kv = k_ref[rows, :].astype(F32)
            vv = v_ref[rows, :].astype(F32)
            loc = _gdn_chunk_local(qv, kv, vv, gc_ref[rows, :], gr_ref[ci][0:1, :], b_ref[rows, :])
            tm = _inv_unit_lower(loc["amat"]).astype(BF)
            u = _dot(tm, loc["vb"].astype(BF))
            w = _dot(tm, loc["kbe"].astype(BF))
            sb = s_ref[...].astype(BF)
            vn = u - _dot(w.astype(BF), sb)
            o = _dot(loc["qd"].astype(BF), sb) + _dot(loc["pmat"].astype(BF), vn.astype(BF))
            s_ref[...] = s_ref[...] * jnp.exp(loc["gl"]) + _dot_tn(loc["kd"].astype(BF), vn.astype(BF))
            g = gg_ref[rows, :]
            rstd = lax.rsqrt(jnp.mean(o * o, axis=-1, keepdims=True) + EPS)
            y_ref[rows, :] = (o * rstd * wnv * (g * _sigmoid(g))).astype(BF)
            return carry

        lax.fori_loop(0, n, step, 0)

    return pl.pallas_call(
        body, name=name, grid=(nb * GDN_HEADS,),
        in_specs=[blk, blk, blk, gg, col, col, rowb, _full((1, LANES))],
        out_specs=blk, out_shape=jax.ShapeDtypeStruct((t, GDN_WIDTH), BF),
        scratch_shapes=[pltpu.VMEM((GDN_HEAD_DIM, GDN_HEAD_DIM), F32)],
        compiler_params=_params(("parallel",)),
    )(q, k, v, proj, gcol, bcol, grow, wn)


def _gdn_bwd(q, k, v, proj, gcol, bcol, grow, wn, dy, nb, seq, name):
    t = q.shape[0]
    n, blk, gg, col, rowb = _gdn_specs(nb, seq)
    dh = GDN_HEAD_DIM

    def body(q_ref, k_ref, v_ref, gg_ref, gc_ref, b_ref, gr_ref, wn_ref, dy_ref,
             dq_ref, dk_ref, dv_ref, dgg_ref, dgc_ref, dgr_ref, db_ref, dwn_ref, s_ref, ds_ref, sn_ref, tn_ref):
        @pl.when(pl.program_id(0) == 0)
        def _():
            dwn_ref[...] = jnp.zeros_like(dwn_ref)

        wnv = wn_ref[...]

        def load(ci):
            rows = pl.ds(pl.multiple_of(ci * CHUNK, CHUNK), CHUNK)
            qv = q_ref[rows, :].astype(F32)
            kv = k_ref[rows, :].astype(F32)
            vv = v_ref[rows, :].astype(F32)
            bv = b_ref[rows, :]
            loc = _gdn_chunk_local(qv, kv, vv, gc_ref[rows, :], gr_ref[ci][0:1, :], bv)
            return rows, qv, kv, vv, bv, loc

        s_ref[...] = jnp.zeros_like(s_ref)

        def fwd_step(ci, carry):
            rows, qv, kv, vv, bv, loc = load(ci)
            tf = _inv_unit_lower(loc["amat"])
            tn_ref[ci] = tf
            tm = tf.astype(BF)
            sn_ref[ci] = s_ref[...]
            u = _dot(tm, loc["vb"].astype(BF))
            w = _dot(tm, loc["kbe"].astype(BF))
            vn = u - _dot(w.astype(BF), s_ref[...].astype(BF))
            s_ref[...] = s_ref[...] * jnp.exp(loc["gl"]) + _dot_tn(loc["kd"].astype(BF), vn.astype(BF))
            return carry

        lax.fori_loop(0, n, fwd_step, 0)
        ds_ref[...] = jnp.zeros_like(ds_ref)

        def bwd_step(j, carry):
            ci = n - 1 - j
            rows, qv, kv, vv, bv, loc = load(ci)
            tf = tn_ref[ci]
            tm = tf.astype(BF)
            sv = sn_ref[ci]
            sb = sv.astype(BF)
            dsp = ds_ref[...]
            dspb = dsp.astype(BF)
            kb, vb, kbe, e, dm = loc["kb"], loc["vb"], loc["kbe"], loc["e"], loc["dm"]
            kd, qd, pmat, amat, gl = loc["kd"], loc["qd"], loc["pmat"], loc["amat"], loc["gl"]
            u = _dot(tm, vb.astype(BF))
            w = _dot(tm, kbe.astype(BF))
            vn = u - _dot(w.astype(BF), sb)
            o = _dot(qd.astype(BF), sb) + _dot(pmat.astype(BF), vn.astype(BF))
            g = gg_ref[rows, :]
            sg = _sigmoid(g)
            silu = g * sg
            rstd = lax.rsqrt(jnp.mean(o * o, axis=-1, keepdims=True) + EPS)
            xhat = o * rstd
            dyv = dy_ref[rows, :].astype(F32)
            dwn_ref[...] += jnp.sum(dyv * xhat * silu, axis=0, keepdims=True)
            dgg_ref[rows, :] = (dyv * xhat * wnv * (sg * (1.0 + g * (1.0 - sg)))).astype(BF)
            dxhat = dyv * wnv * silu
            do = rstd * (dxhat - xhat * jnp.mean(dxhat * xhat, axis=-1, keepdims=True))
            dob = do.astype(BF)
            egl = jnp.exp(gl)
            dvn = _dot_tn(pmat.astype(BF), dob) + _dot(kd.astype(BF), dspb)
            dvnb = dvn.astype(BF)
            dpm = jnp.where(loc["incl"], _dot_nt(dob, vn.astype(BF)), 0.0)
            dqd = _dot_nt(dob, sb)
            dkd = _dot_nt(vn.astype(BF), dspb)
            ds_ref[...] = egl * dsp + _dot_tn(qd.astype(BF), dob) - _dot_tn(w.astype(BF), dvnb)
            tot = lambda x: jnp.sum(jnp.sum(x, axis=1, keepdims=True), axis=0, keepdims=True)
            rsum = lambda x: jnp.sum(x, axis=1, keepdims=True)
            dgl = egl * tot(sv * dsp) + tot(dkd * kd)
            dw = -_dot_nt(dvnb, sb)
            du = dvn
            dq = dqd * e
            dgc = rsum(dqd * qd) - rsum(dkd * kd)
            dk = dkd * jnp.exp(gl - gc_ref[rows, :])
            dmp = (dpm * dm).astype(BF)
            dq = dq + _dot(dmp, kv.astype(BF))
            dk = dk + _dot_tn(dmp, qv.astype(BF))
            dt = _dot_nt(du.astype(BF), vb.astype(BF)) + _dot_nt(dw.astype(BF), kbe.astype(BF))
            dvb = _dot_tn(tm, du.astype(BF))
            dkbe = _dot_tn(tm, dw.astype(BF))
            da = jnp.where(loc["strict"], -_dot_nt(_dot_tn(tf, dt, HI), tf, HI), 0.0)
            dn = (da * dm).astype(BF)
            dkb = _dot(dn, kv.astype(BF)) + dkbe * e
            dk = dk + _dot_tn(dn, kb.astype(BF))
            gmat = dpm * pmat + da * amat
            dgc = dgc + rsum(dkbe * kbe) + rsum(gmat)
            ridx = lax.broadcasted_iota(jnp.int32, (CHUNK, 1), 0)
            dgc = dgc + jnp.where(ridx == CHUNK - 1, dgl, 0.0)
            dgr_ref[ci] = jnp.broadcast_to(jnp.sum(gmat, axis=0, keepdims=True), (HALO, CHUNK))
            dq_ref[rows, :] = dq
            dk_ref[rows, :] = dk + dkb * bv
            dv_ref[rows, :] = dvb * bv
            db_ref[rows, :] = rsum(dvb * vv) + rsum(dkb * kv)
            dgc_ref[rows, :] = dgc
            return carry

        lax.fori_loop(0, n, bwd_step, 0)

    return pl.pallas_call(
        body, name=name, grid=(nb * GDN_HEADS,),
        in_specs=[blk, blk, blk, gg, col, col, rowb, _full((1, LANES)), blk],
        out_specs=[blk, blk, blk, blk, col, rowb, col, _full((1, LANES))],
        out_shape=[jax.ShapeDtypeStruct((t, GDN_WIDTH), F32)] * 3 + [
            jax.ShapeDtypeStruct((t, GDN_WIDTH), BF),
            jax.ShapeDtypeStruct((nb * GDN_HEADS, seq, 1), F32),
            jax.ShapeDtypeStruct((nb * GDN_HEADS, n, HALO, CHUNK), F32),
            jax.ShapeDtypeStruct((nb * GDN_HEADS, seq, 1), F32),
            jax.ShapeDtypeStruct((1, LANES), F32)],
        scratch_shapes=[pltpu.VMEM((dh, dh), F32), pltpu.VMEM((dh, dh), F32),
                        pltpu.VMEM((n, dh, dh), F32), pltpu.VMEM((n, CHUNK, CHUNK), F32)],
        compiler_params=_params(("arbitrary",)),
    )(q, k, v, proj, gcol, bcol, grow, wn, dy)


def _mix_to_padded(w):
    pad = jnp.zeros(w.shape[:-1] + (N_PAD - N_IN,), w.dtype)
    return jnp.concatenate([w[..., 0:1536], w[..., 1544:3080], w[..., 3088:3600], w[..., 1536:1544],
                            w[..., 3080:3088], pad], axis=-1)


def _mix_from_padded(g):
    return jnp.concatenate([g[..., 0:1536], g[..., 3584:3592], g[..., 1536:3072], g[..., 3592:3600],
                            g[..., 3072:3584]], axis=-1)


def _pad_lanes(vec, start):
    return jnp.zeros((1, LANES), F32).at[0, start:start + vec.shape[0]].set(vec)


def _heads_to_rows(block, lane0, nheads, nb, seq):
    return block[:, lane0:lane0 + nheads].reshape(nb, seq, nheads).transpose(0, 2, 1).reshape(nb * nheads, seq)


def _rows_to_heads(rows, lane0, nheads, nb, seq):
    v = rows.reshape(nb, nheads, seq).transpose(0, 2, 1).reshape(nb * seq, nheads)
    return jnp.zeros((nb * seq, LANES), F32).at[:, lane0:lane0 + nheads].set(v)


def _mixer_small(p, l):
    wq_t = jnp.tile(p["fox_q_norm"][l], FOX_HEADS)[None, :]
    wk_t = jnp.tile(p["fox_k_norm"][l], FOX_HEADS)[None, :]
    bias = _pad_lanes(p["fox_f_bias"][l], 0)
    a_pad = _pad_lanes(p["gdn_a_log"][l], A_LANE)
    dt_pad = _pad_lanes(p["gdn_dt_bias"][l], A_LANE)
    wn = p["gdn_out_norm"][l][None, :]
    return wq_t, wk_t, bias, a_pad, dt_pad, wn


def _layer_fwd(x, p, l, nb, seq):
    npair = FOX_HEADS // 2
    n = seq // CHUNK
    wq_t, wk_t, bias, a_pad, dt_pad, wn = _mixer_small(p, l)
    x1, h1 = _ffn_fwd(x, p["ffn1_norm"][l][None, :], p["ffn1_w_in"], p["ffn1_w_out"], l, f"ffn1_fwd_{l}")
    proj = _norm_matmul(x1, p["mix_norm"][l][None, :], p["w_mix"][l], f"mix_in_{l}")
    fq, fk, fv, cum = _fox_prep(proj, wq_t, wk_t, bias, seq, f"fox_prep_{l}")
    c8 = _heads_to_rows(cum, 0, FOX_HEADS, nb, seq)
    cq = c8.reshape(nb * npair, 2, seq, 1)
    ck = c8.reshape(nb * npair, 2, 1, seq)
    o, lse = _fox_attn(fq, fk, fv, cq, ck, nb, seq, f"fox_attn_{l}")
    gq, gk, gv, gates = _gdn_prep(proj, p["gdn_conv"][l], a_pad, dt_pad, seq, f"gdn_prep_{l}")
    gc4 = _heads_to_rows(gates, A_LANE, GDN_HEADS, nb, seq)
    gcol = gc4[:, :, None]
    grow = jnp.broadcast_to(gc4.reshape(nb * GDN_HEADS, n, 1, CHUNK), (nb * GDN_HEADS, n, HALO, CHUNK))
    bcol = _heads_to_rows(gates, B_LANE, GDN_HEADS, nb, seq)[:, :, None]
    y = _gdn_fwd(gq, gk, gv, proj, gcol, bcol, grow, wn, nb, seq, f"gdn_fwd_{l}")
    x2 = _mix_out(x1, o, y, p["w_out"][l], f"mix_out_{l}")
    x3, h2 = _ffn_fwd(x2, p["ffn2_norm"][l][None, :], p["ffn2_w_in"], p["ffn2_w_out"], l, f"ffn2_fwd_{l}")
    saved = dict(x=x, h1=h1, x1=x1, proj=proj, fq=fq, fk=fk, fv=fv, cq=cq, ck=ck, o=o, lse=lse,
                 gq=gq, gk=gk, gv=gv, gcol=gcol, grow=grow, bcol=bcol, y=y, x2=x2, h2=h2)
    return x3, saved


def _ffn_grads(dy, x, h, gain, win, wout, l, tag):
    t, d = x.shape
    fs = win.shape[3]
    dx, dh, a, hn, dyh, dgain = _ffn_bwd(dy, x, h, gain, win, wout, l, f"{tag}_bwd_{l}")
    g_in = _wgrad(hn, dh, jax.ShapeDtypeStruct((4, d, fs), BF),
                  pl.BlockSpec((None, d, fs), lambda i, j, k: (j, i, 0)), d, fs, f"{tag}_gw_in_{l}")
    g_out = _wgrad(a, dyh, jax.ShapeDtypeStruct((2 * fs, d), BF),
                   pl.BlockSpec((fs, d), lambda i, j, k: (i, j)), fs, d, f"{tag}_gw_out_{l}")
    return dx, dgain[0], g_in, g_out.reshape(4, fs // 2, d)


def _layer_bwd(dx3, p, l, sv, nb, seq):
    npair = FOX_HEADS // 2
    d = dx3.shape[1]
    wq_t, wk_t, bias, a_pad, dt_pad, wn = _mixer_small(p, l)
    g = {}
    dx2, g["ffn2_norm"], g["ffn2_w_in"], g["ffn2_w_out"] = _ffn_grads(
        dx3, sv["x2"], sv["h2"], p["ffn2_norm"][l][None, :], p["ffn2_w_in"], p["ffn2_w_out"], l, "ffn2")
    dyf, dyg, dxb = _mix_out_bwd(dx2, p["w_out"][l], f"mix_out_bwd_{l}")
    half = lambda a, nm: _wgrad(a, dxb, jax.ShapeDtypeStruct((FOX_WIDTH, d), BF),
                                pl.BlockSpec((FOX_WIDTH, d), lambda i, j, k: (i, j)), FOX_WIDTH, d, nm)
    g["w_out"] = jnp.concatenate([half(sv["o"], f"gw_out_fox_{l}"), half(sv["y"], f"gw_out_gdn_{l}")], axis=0)
    dq, dk, dv, dcq, dck = _fox_attn_bwd(sv["fq"], sv["fk"], sv["fv"], sv["o"], dyf, sv["lse"], sv["cq"], sv["ck"],
                                         nb, seq, f"fox_attn_bwd_{l}")
    dcq8 = dcq.reshape(nb * seq, FOX_HEADS, FOX_HEAD_DIM)[:, :, 0]
    dck8 = dck.reshape(nb, FOX_HEADS, seq).transpose(0, 2, 1).reshape(nb * seq, FOX_HEADS)
    dcum = jnp.zeros((nb * seq, LANES), F32).at[:, 0:FOX_HEADS].set(dcq8 - dck8)
    dpf, dff, dwq, dwk, dbias = _fox_prep_bwd(sv["proj"], dq, dk, dv, dcum, wq_t, wk_t, bias, seq,
                                              f"fox_prep_bwd_{l}")
    g["fox_q_norm"] = dwq[0, :FOX_HEAD_DIM]
    g["fox_k_norm"] = dwk[0, :FOX_HEAD_DIM]
    g["fox_f_bias"] = dbias[0, :FOX_HEADS]
    dgq, dgk, dgv, dgg, dgc_col, dgc_row, dbeta, dwn = _gdn_bwd(
        sv["gq"], sv["gk"], sv["gv"], sv["proj"], sv["gcol"], sv["bcol"], sv["grow"], wn, dyg, nb, seq,
        f"gdn_bwd_{l}")
    dgc = dgc_col[:, :, 0] - dgc_row[:, :, 0, :].reshape(nb * GDN_HEADS, seq)
    dgates = _rows_to_heads(dgc, A_LANE, GDN_HEADS, nb, seq) + _rows_to_heads(dbeta[:, :, 0], B_LANE, GDN_HEADS, nb, seq)
    dpg, dgate_blk, dconv, da, ddt = _gdn_prep_bwd(sv["proj"], dgq, dgk, dgv, dgates, dff, p["gdn_conv"][l],
                                                   a_pad, dt_pad, seq, f"gdn_prep_bwd_{l}")
    g["gdn_conv"] = dconv
    g["gdn_a_log"] = da[0, A_LANE:B_LANE]
    g["gdn_dt_bias"] = ddt[0, A_LANE:B_LANE]
    g["gdn_out_norm"] = dwn[0]
    dproj = jnp.concatenate([dpf, dpg, dgg, dgate_blk], axis=1)
    dx1, hnm, dgm = _norm_matmul_bwd(dx2, dproj, sv["x1"], p["mix_norm"][l][None, :], p["w_mix"][l],
                                     f"mix_in_bwd_{l}")
    g["mix_norm"] = dgm[0]
    g["w_mix"] = _wgrad(hnm, dproj, jax.ShapeDtypeStruct((d, N_PAD), F32),
                        pl.BlockSpec((d // 2, N_PAD), lambda i, j, k: (i, j)), d // 2, N_PAD, f"gw_mix_{l}")
    dx0, g["ffn1_norm"], g["ffn1_w_in"], g["ffn1_w_out"] = _ffn_grads(
        dx1, sv["x"], sv["h1"], p["ffn1_norm"][l][None, :], p["ffn1_w_in"], p["ffn1_w_out"], l, "ffn1")
    return dx0, g


def _local_step(x, target, p):
    nb, seq, d = x.shape
    xt = x.reshape(nb * seq, d)
    saved = []
    for l in range(DEPTH):
        xt, sv = _layer_fwd(xt, p, l, nb, seq)
        saved.append(sv)
    loss, dx = _loss_grad(xt, target.reshape(nb * seq, d), "loss")
    grads = [None] * DEPTH
    for l in reversed(range(DEPTH)):
        dx, grads[l] = _layer_bwd(dx, p, l, saved[l], nb, seq)
    return loss, dx.reshape(nb, seq, d), grads


N_CHIPS = 4


def _mesh_pos():
    return lax.axis_index("x"), lax.axis_index("y"), lax.axis_index("c")


def _other_chips(x, y):
    return [(1 - x, y), (x, 1 - y), (1 - x, 1 - y)]


def _remote(src, dst, send_sem, recv_sem, to):
    return pltpu.make_async_remote_copy(src_ref=src, dst_ref=dst, send_sem=send_sem, recv_sem=recv_sem,
                                        device_id=to, device_id_type=MESH)


def _hbm_call(body, name, ins, out_shape, scratch):
    return pl.pallas_call(
        body, name=name, out_shape=out_shape, in_specs=[HBM] * len(ins),
        out_specs=jax.tree.map(lambda _: HBM, out_shape), scratch_shapes=scratch,
        compiler_params=pltpu.CompilerParams(has_side_effects=True),
    )(*ins)


def _all_gather(shards, name):
    n = len(shards)

    def body(*refs):
        ins, outs = refs[:n], refs[n:2 * n]
        send1, recv1, send2, recv2, lsem = refs[2 * n:]
        x, y, c = _mesh_pos()
        me = 2 * x + y
        chips = _other_chips(x, y)
        local = [pltpu.make_async_copy(ins[i], outs[i].at[me], lsem.at[i]) for i in range(n)]
        for cp in local:
            cp.start()
        first = []
        for i in range(n):
            for j, (px, py) in enumerate(chips):
                cp = _remote(ins[i].at[c], outs[i].at[me, c], send1.at[3 * i + j], recv1.at[3 * i + j], (px, py, c))
                cp.start()
                first.append(cp)
        passed = []
        for i in range(n):
            for j, (px, py) in enumerate(chips):
                blk = outs[i].at[2 * px + py, c]
                _remote(blk, blk, send1.at[3 * i + j], recv1.at[3 * i + j], (px, py, c)).wait_recv()
                fw = _remote(blk, blk, send2.at[3 * i + j], recv2.at[3 * i + j], (x, y, 1 - c))
                fw.start()
                passed.append(fw)
        for i in range(n):
            for j, (px, py) in enumerate(chips):
                blk = outs[i].at[2 * px + py, 1 - c]
                _remote(blk, blk, send2.at[3 * i + j], recv2.at[3 * i + j], (x, y, 1 - c)).wait_recv()
        for cp in first + passed:
            cp.wait_send()
        for cp in local:
            cp.wait()

    sem = pltpu.SemaphoreType.DMA((3 * n,))
    return _hbm_call(body, name, shards, [jax.ShapeDtypeStruct((N_CHIPS,) + s.shape, s.dtype) for s in shards],
                     [sem, sem, sem, sem, pltpu.SemaphoreType.DMA((n,))])


def _sibling_send_layers(gs, name):
    n = len(gs)

    def body(*refs):
        ins, outs = refs[:n], refs[n:2 * n]
        send, recv = refs[2 * n:]
        x, y, c = _mesh_pos()
        cps = [_remote(ins[i].at[1 - c], outs[i], send.at[i], recv.at[i], (x, y, 1 - c)) for i in range(n)]
        for cp in cps:
            cp.start()
        for cp in cps:
            cp.wait()

    sem = pltpu.SemaphoreType.DMA((n,))
    return _hbm_call(body, name, gs, [jax.ShapeDtypeStruct(g.shape[1:], g.dtype) for g in gs], [sem, sem])


def _chip_scatter(ps, name):
    n = len(ps)

    def body(*refs):
        ins, outs = refs[:n], refs[n:2 * n]
        send, recv = refs[2 * n:]
        x, y, c = _mesh_pos()
        cps = []
        for i in range(n):
            for j, (px, py) in enumerate(_other_chips(x, y)):
                cps.append(_remote(ins[i].at[2 * px + py], outs[i].at[j], send.at[3 * i + j], recv.at[3 * i + j],
                                   (px, py, c)))
        for cp in cps:
            cp.start()
        for cp in cps:
            cp.wait()

    sem = pltpu.SemaphoreType.DMA((3 * n,))
    return _hbm_call(body, name, ps, [jax.ShapeDtypeStruct((3,) + p.shape[1:], p.dtype) for p in ps], [sem, sem])


def _sibling_merge(rs, name):
    n = len(rs)

    def body(*refs):
        ins, outs = refs[:n], refs[n:2 * n]
        send, recv, lsem = refs[2 * n:]
        x, y, c = _mesh_pos()
        local = [pltpu.make_async_copy(ins[i], outs[i].at[c], lsem.at[i]) for i in range(n)]
        cps = [_remote(ins[i], outs[i].at[c], send.at[i], recv.at[i], (x, y, 1 - c)) for i in range(n)]
        for cp in local + cps:
            cp.start()
        for i in range(n):
            cps[i].wait_send()
            blk = outs[i].at[1 - c]
            _remote(blk, blk, send.at[i], recv.at[i], (x, y, 1 - c)).wait_recv()
        for cp in local:
            cp.wait()

    sem = pltpu.SemaphoreType.DMA((n,))
    return _hbm_call(body, name, rs, [jax.ShapeDtypeStruct((DEPTH,) + r.shape, r.dtype) for r in rs],
                     [sem, sem, sem])


def _small_all_reduce(vec, name):
    r = vec.shape[0]
    ndev = 8

    def body(v_ref, o_ref, buf, send, recv):
        x, y, c = _mesh_pos()
        me = 4 * x + 2 * y + c
        buf[me] = v_ref[...]
        cps = []
        for rel in range(1, ndev):
            px = 1 - x if rel & 4 else x
            py = 1 - y if rel & 2 else y
            pc = 1 - c if rel & 1 else c
            cps.append((_remote(v_ref, buf.at[me], send.at[rel - 1], recv.at[rel - 1], (px, py, pc)),
                        4 * px + 2 * py + pc))
        for cp, _ in cps:
            cp.start()
        for k, (cp, peer) in enumerate(cps):
            slot = buf.at[peer]
            _remote(slot, slot, send.at[k], recv.at[k], (x, y, c)).wait_recv()
        for cp, _ in cps:
            cp.wait_send()
        acc = buf[0]
        for k in range(1, ndev):
            acc = acc + buf[k]
        o_ref[...] = acc

    vm = pl.BlockSpec(memory_space=pltpu.VMEM)
    return pl.pallas_call(
        body, name=name, out_shape=jax.ShapeDtypeStruct(vec.shape, F32), in_specs=[vm], out_specs=vm,
        scratch_shapes=[pltpu.VMEM((ndev, r, LANES), F32), pltpu.SemaphoreType.DMA((ndev - 1,)),
                        pltpu.SemaphoreType.DMA((ndev - 1,))],
        compiler_params=pltpu.CompilerParams(has_side_effects=True),
    )(vec)


def _row_tile(rows, cap=512):
    for t in range(min(rows, cap), 0, -1):
        if rows % t == 0 and (t % 16 == 0 or t == rows):
            return t
    raise ValueError(rows)


def _add_pairs(a, b, name):
    k, r, c = a.shape
    tr = _row_tile(r)

    def body(a_ref, b_ref, o_ref):
        o_ref[...] = (a_ref[...].astype(F32) + b_ref[...].astype(F32)).astype(o_ref.dtype)

    spec = pl.BlockSpec((None, tr, c), lambda i, j: (i, j, 0))
    return pl.pallas_call(body, name=name, grid=(k, r // tr), in_specs=[spec, spec], out_specs=spec,
                          out_shape=jax.ShapeDtypeStruct(a.shape, a.dtype),
                          compiler_params=_params(("parallel", "parallel")))(a, b)


def _final_sum(own, sib, others, name):
    r, c = own.shape
    tr = _row_tile(r)

    def body(a_ref, b_ref, o_ref_in, out_ref):
        acc = a_ref[...].astype(F32) + b_ref[...].astype(F32)
        for k in range(3):
            acc = acc + o_ref_in[k].astype(F32)
        out_ref[...] = acc

    spec = pl.BlockSpec((tr, c), lambda i: (i, 0))
    return pl.pallas_call(body, name=name, grid=(r // tr,),
                          in_specs=[spec, spec, pl.BlockSpec((3, tr, c), lambda i: (0, i, 0))], out_specs=spec,
                          out_shape=jax.ShapeDtypeStruct((r, c), F32),
                          compiler_params=_params(("parallel",)))(own, sib, others)


def _adamw(g, w, m, v, name):
    r, c = g.shape
    tr = _row_tile(r, 256)

    def body(g_ref, w_ref, m_ref, v_ref, d_ref, mo_ref, vo_ref):
        gv = g_ref[...]
        mn = ADAM_B1 * m_ref[...] + (1.0 - ADAM_B1) * gv
        vn = ADAM_B2 * v_ref[...] + (1.0 - ADAM_B2) * (gv * gv)
        m_hat = mn / (1.0 - ADAM_B1 ** ADAM_STEP)
        v_hat = vn / (1.0 - ADAM_B2 ** ADAM_STEP)
        d_ref[...] = -ADAM_LR * (m_hat / (jnp.sqrt(v_hat) + ADAM_EPS) + ADAM_WD * w_ref[...])
        mo_ref[...] = mn
        vo_ref[...] = vn

    spec = pl.BlockSpec((tr, c), lambda i: (i, 0))
    shp = jax.ShapeDtypeStruct((r, c), F32)
    return pl.pallas_call(body, name=name, grid=(r // tr,), in_specs=[spec] * 4, out_specs=[spec] * 3,
                          out_shape=[shp] * 3, compiler_params=_params(("parallel",)))(g, w, m, v)


def _pack(arrays):
    flat = jnp.concatenate([a.reshape(-1).astype(F32) for a in arrays])
    pad = (-flat.shape[0]) % (8 * LANES)
    return jnp.concatenate([flat, jnp.zeros((pad,), F32)]).reshape(-1, LANES)


def _unpack(packed, shapes):
    flat = packed.reshape(-1)
    out, off = [], 0
    for s in shapes:
        size = 1
        for dim in s:
            size *= dim
        out.append(flat[off:off + size].reshape(s))
        off += size
    return out


BIG = ("ffn1_w_in", "ffn1_w_out", "w_in", "w_out", "ffn2_w_in", "ffn2_w_out")
SMALL = ("ffn1_norm", "mix_norm", "fox_q_norm", "fox_k_norm", "fox_f_bias", "gdn_a_log", "gdn_dt_bias",
         "gdn_out_norm", "ffn2_norm", "gdn_conv")
WEIGHTS = ("ffn1_norm", "ffn1_w_in", "ffn1_w_out", "mix_norm", "w_in", "fox_q_norm", "fox_k_norm", "fox_f_bias",
           "gdn_conv", "gdn_a_log", "gdn_dt_bias", "gdn_out_norm", "w_out", "ffn2_norm", "ffn2_w_in", "ffn2_w_out")


def _step(x, target, w, m, v):
    xi, yi, ci = _mesh_pos()
    me = 2 * xi + yi
    depth = DEPTH
    d = x.shape[-1]

    gathered = _all_gather([w[k].astype(BF) for k in BIG] + [w["gdn_conv"]], "all_gather_weights")
    gw = dict(zip(BIG, gathered))
    p = {k: w[k] for k in SMALL if k != "gdn_conv"}
    conv = gathered[len(BIG)]
    p["gdn_conv"] = conv.transpose(1, 2, 0, 3).reshape(depth, CONV_WIDTH, -1)
    for k in ("ffn1_w_in", "ffn1_w_out", "ffn2_w_in", "ffn2_w_out"):
        p[k] = gw[k]
    p["w_mix"] = _mix_to_padded(gw["w_in"].transpose(1, 2, 0, 3).reshape(depth, d, N_IN))
    p["w_out"] = gw["w_out"].transpose(1, 0, 2, 3).reshape(depth, 2 * FOX_WIDTH, d)

    loss, dx, grads = _local_step(x, target, p)

    def transport(k):
        per_layer = []
        for l in range(depth):
            g = grads[l]
            if k == "w_in":
                full = _mix_from_padded(g["w_mix"])
                per_layer.append(full.reshape(d, N_CHIPS, N_IN // N_CHIPS).transpose(1, 0, 2).astype(BF))
            elif k == "w_out":
                per_layer.append(g["w_out"].reshape(N_CHIPS, -1, d))
            else:
                per_layer.append(g[k])
        return jnp.stack(per_layer)

    gs = [transport(k) for k in BIG]
    from_sib = _sibling_send_layers(gs, "grad_to_sibling")
    mine = [lax.dynamic_index_in_dim(g, ci, 0, keepdims=False) for g in gs]
    chip_sums = [_add_pairs(a, b, f"grad_chip_sum_{k}") for a, b, k in zip(mine, from_sib, BIG)]
    from_chips = _chip_scatter(chip_sums, "grad_to_chips")
    reduced = [_final_sum(lax.dynamic_index_in_dim(a, me, 0, keepdims=False),
                          lax.dynamic_index_in_dim(b, me, 0, keepdims=False), o, f"grad_final_sum_{k}")
               for a, b, o, k in zip(mine, from_sib, from_chips, BIG)]
    full = dict(zip(BIG, _sibling_merge(reduced, "grad_merge_layers")))

    out_g, out_d, out_m, out_v = {}, {}, {}, {}
    for k in BIG:
        shp = w[k].shape
        two_d = lambda a: a.reshape(shp[0] * shp[1], shp[2])
        dl, mn, vn = _adamw(two_d(full[k]), two_d(w[k]), two_d(m[k]), two_d(v[k]), f"adamw_{k}")
        out_g[k], out_d[k], out_m[k], out_v[k] = full[k], dl.reshape(shp), mn.reshape(shp), vn.reshape(shp)

    small_local = [jnp.stack([grads[l][k] for l in range(depth)]) for k in SMALL]
    summed = _unpack(_small_all_reduce(_pack(small_local), "small_all_reduce"), [a.shape for a in small_local])
    sg = dict(zip(SMALL, summed))
    cs = w["gdn_conv"].shape[-1]
    sg["gdn_conv"] = lax.dynamic_slice_in_dim(sg["gdn_conv"], me * cs, cs, axis=2)
    shapes = [w[k].shape for k in SMALL]
    packs = [_pack([src[k] for k in SMALL]) for src in (sg, w, m, v)]
    dl, mn, vn = _adamw(*packs, "adamw_small")
    for k, a, b, c2 in zip(SMALL, _unpack(dl, shapes), _unpack(mn, shapes), _unpack(vn, shapes)):
        out_g[k], out_d[k], out_m[k], out_v[k] = sg[k], a, b, c2

    total = lax.psum(loss[0, 0], ("x", "y", "c"))
    return (total, dx, *[out_g[k] for k in WEIGHTS], *[out_d[k] for k in WEIGHTS],
            *[out_m[k] for k in WEIGHTS], *[out_v[k] for k in WEIGHTS])


def kernel(x, ffn1_norm, ffn1_w_in, ffn1_w_out, mix_norm, w_in, fox_q_norm, fox_k_norm, fox_f_bias, gdn_conv, gdn_a_log, gdn_dt_bias, gdn_out_norm, w_out, ffn2_norm, ffn2_w_in, ffn2_w_out, loss_target, m_ffn1_norm, m_ffn1_w_in, m_ffn1_w_out, m_mix_norm, m_w_in, m_fox_q_norm, m_fox_k_norm, m_fox_f_bias, m_gdn_conv, m_gdn_a_log, m_gdn_dt_bias, m_gdn_out_norm, m_w_out, m_ffn2_norm, m_ffn2_w_in, m_ffn2_w_out, v_ffn1_norm, v_ffn1_w_in, v_ffn1_w_out, v_mix_norm, v_w_in, v_fox_q_norm, v_fox_k_norm, v_fox_f_bias, v_gdn_conv, v_gdn_a_log, v_gdn_dt_bias, v_gdn_out_norm, v_w_out, v_ffn2_norm, v_ffn2_w_in, v_ffn2_w_out):
    w = dict(ffn1_norm=ffn1_norm, ffn1_w_in=ffn1_w_in, ffn1_w_out=ffn1_w_out, mix_norm=mix_norm, w_in=w_in,
             fox_q_norm=fox_q_norm, fox_k_norm=fox_k_norm, fox_f_bias=fox_f_bias, gdn_conv=gdn_conv,
             gdn_a_log=gdn_a_log, gdn_dt_bias=gdn_dt_bias, gdn_out_norm=gdn_out_norm, w_out=w_out,
             ffn2_norm=ffn2_norm, ffn2_w_in=ffn2_w_in, ffn2_w_out=ffn2_w_out)
    m = dict(ffn1_norm=m_ffn1_norm, ffn1_w_in=m_ffn1_w_in, ffn1_w_out=m_ffn1_w_out, mix_norm=m_mix_norm, w_in=m_w_in,
             fox_q_norm=m_fox_q_norm, fox_k_norm=m_fox_k_norm, fox_f_bias=m_fox_f_bias, gdn_conv=m_gdn_conv,
             gdn_a_log=m_gdn_a_log, gdn_dt_bias=m_gdn_dt_bias, gdn_out_norm=m_gdn_out_norm, w_out=m_w_out,
             ffn2_norm=m_ffn2_norm, ffn2_w_in=m_ffn2_w_in, ffn2_w_out=m_ffn2_w_out)
    v = dict(ffn1_norm=v_ffn1_norm, ffn1_w_in=v_ffn1_w_in, ffn1_w_out=v_ffn1_w_out, mix_norm=v_mix_norm, w_in=v_w_in,
             fox_q_norm=v_fox_q_norm, fox_k_norm=v_fox_k_norm, fox_f_bias=v_fox_f_bias, gdn_conv=v_gdn_conv,
             gdn_a_log=v_gdn_a_log, gdn_dt_bias=v_gdn_dt_bias, gdn_out_norm=v_gdn_out_norm, w_out=v_w_out,
             ffn2_norm=v_ffn2_norm, ffn2_w_in=v_ffn2_w_in, ffn2_w_out=v_ffn2_w_out)
    return _step(x, loss_target, w, m, v)
```

```python
import jax
import jax.numpy as jnp
from jax import lax
from jax.experimental import pallas as pl
from jax.experimental.pallas import tpu as pltpu

F32 = jnp.float32
BF = jnp.bfloat16
HI = lax.Precision.HIGHEST
MESH = pl.DeviceIdType.MESH

DEPTH = 2
FOX_HEADS = 8
FOX_HEAD_DIM = 64
FOX_WIDTH = 512
GDN_HEADS = 4
GDN_HEAD_DIM = 128
GDN_WIDTH = 512
CONV_WIDTH = 4
CHUNK = 64
EPS = 1e-6
N_IN = 3600
N_PAD = 3712
GATE_COL = 3584
LANES = 128
NEG = -1e30

ADAM_LR = 0.001
ADAM_B1 = 0.9
ADAM_B2 = 0.999
ADAM_EPS = 1e-08
ADAM_WD = 0.01
ADAM_STEP = 10

VMEM_LIMIT = 56 * 1024 * 1024


def _params(sem=None, **kw):
    return pltpu.CompilerParams(dimension_semantics=sem, vmem_limit_bytes=VMEM_LIMIT, **kw)


def _dot(a, b, precision=None):
    return jnp.dot(a, b, preferred_element_type=F32, precision=precision)


def _dot_nt(a, b, precision=None):
    return lax.dot_general(a, b, (((1,), (1,)), ((), ())), preferred_element_type=F32, precision=precision)


def _dot_tn(a, b, precision=None):
    return lax.dot_general(a, b, (((0,), (0,)), ((), ())), preferred_element_type=F32, precision=precision)


def _sigmoid(x):
    return 1.0 / (1.0 + jnp.exp(-x))


def _softplus(x):
    return jnp.maximum(x, 0.0) + jnp.log(1.0 + jnp.exp(-jnp.abs(x)))


def _log_sigmoid(x):
    return jnp.minimum(x, 0.0) - jnp.log(1.0 + jnp.exp(-jnp.abs(x)))


def _tile(n, t):
    t = min(n, t)
    assert n % t == 0, (n, t)
    return t


def _rms_fwd(x, gain):
    rstd = lax.rsqrt(jnp.mean(x * x, axis=-1, keepdims=True) + EPS)
    xhat = x * rstd
    return xhat * gain, xhat, rstd


def _rms_bwd(dy, xhat, rstd, gain):
    dxhat = dy * gain
    dx = rstd * (dxhat - xhat * jnp.mean(dxhat * xhat, axis=-1, keepdims=True))
    return dx, dy * xhat


def _full(shape):
    nd = len(shape)
    return pl.BlockSpec(shape, lambda *_: (0,) * nd)


HBM = pl.BlockSpec(memory_space=pltpu.HBM)


def _load_ffn_weights(win_hbm, wout_hbm, layer, win_v, wout_v, sem):
    fr = wout_hbm.shape[2]
    copies = [pltpu.make_async_copy(win_hbm.at[s, layer], win_v.at[s], sem.at[s]) for s in range(4)]
    copies += [pltpu.make_async_copy(wout_hbm.at[s, layer], wout_v.at[pl.ds(s * fr, fr)], sem.at[4 + s])
               for s in range(4)]
    for c in copies:
        c.start()
    for c in copies:
        c.wait()


def _ffn_fwd(x, gain, win_g, wout_g, layer, name):
    t, d = x.shape
    _, _, _, fs = win_g.shape
    fr = wout_g.shape[2]
    tm = _tile(t, 256)

    def body(x_ref, g_ref, win_hbm, wout_hbm, xo_ref, h_ref, win_v, wout_v, sem):
        @pl.when(pl.program_id(0) == 0)
        def _():
            _load_ffn_weights(win_hbm, wout_hbm, layer, win_v, wout_v, sem)

        xv = x_ref[...]
        hn, _, _ = _rms_fwd(xv, g_ref[...])
        hn = hn.astype(BF)
        acc = jnp.zeros((tm, d), F32)
        for s in range(2):
            g = _dot(hn, win_v[s])
            u = _dot(hn, win_v[s + 2])
            h_ref[:, s * fs:(s + 1) * fs] = g.astype(BF)
            h_ref[:, (s + 2) * fs:(s + 3) * fs] = u.astype(BF)
            a = (g * _sigmoid(g) * u).astype(BF)
            acc = acc + _dot(a, wout_v[s * fs:(s + 1) * fs, :])
        xo_ref[...] = xv + 0.5 * acc

    return pl.pallas_call(
        body, name=name, grid=(t // tm,),
        in_specs=[pl.BlockSpec((tm, d), lambda i: (i, 0)), _full((1, d)), HBM, HBM],
        out_specs=[pl.BlockSpec((tm, d), lambda i: (i, 0)), pl.BlockSpec((tm, 4 * fs), lambda i: (i, 0))],
        out_shape=[jax.ShapeDtypeStruct((t, d), F32), jax.ShapeDtypeStruct((t, 4 * fs), BF)],
        scratch_shapes=[pltpu.VMEM((4, d, fs), BF), pltpu.VMEM((4 * fr, d), BF), pltpu.SemaphoreType.DMA((8,))],
        compiler_params=_params(("arbitrary",)),
    )(x, gain, win_g, wout_g)


def _ffn_bwd(dy, x, h, gain, win_g, wout_g, layer, name):
    t, d = x.shape
    _, _, _, fs = win_g.shape
    fr = wout_g.shape[2]
    tm = _tile(t, 256)

    def body(dy_ref, x_ref, h_ref, g_ref, win_hbm, wout_hbm,
             dx_ref, dh_ref, a_ref, hn_ref, dyh_ref, dg_ref, win_v, wout_v, sem):
        @pl.when(pl.program_id(0) == 0)
        def _():
            _load_ffn_weights(win_hbm, wout_hbm, layer, win_v, wout_v, sem)
            dg_ref[...] = jnp.zeros_like(dg_ref)

        dyv = dy_ref[...]
        dyh = (0.5 * dyv).astype(BF)
        dyh_ref[...] = dyh
        dhn = jnp.zeros((tm, d), F32)
        for s in range(2):
            da = _dot_nt(dyh, wout_v[s * fs:(s + 1) * fs, :])
            g = h_ref[:, s * fs:(s + 1) * fs].astype(F32)
            u = h_ref[:, (s + 2) * fs:(s + 3) * fs].astype(F32)
            sg = _sigmoid(g)
            si = g * sg
            a_ref[:, s * fs:(s + 1) * fs] = (si * u).astype(BF)
            dgate = (da * u * (sg * (1.0 + g * (1.0 - sg)))).astype(BF)
            dup = (da * si).astype(BF)
            dh_ref[:, s * fs:(s + 1) * fs] = dgate
            dh_ref[:, (s + 2) * fs:(s + 3) * fs] = dup
            dhn = dhn + _dot_nt(dgate, win_v[s]) + _dot_nt(dup, win_v[s + 2])
        xv = x_ref[...]
        gain_v = g_ref[...]
        hn, xhat, rstd = _rms_fwd(xv, gain_v)
        hn_ref[...] = hn.astype(BF)
        dx, dgr = _rms_bwd(dhn, xhat, rstd, gain_v)
        dx_ref[...] = dyv + dx
        dg_ref[...] += jnp.sum(dgr, axis=0, keepdims=True)

    row = lambda w: pl.BlockSpec((tm, w), lambda i: (i, 0))
    return pl.pallas_call(
        body, name=name, grid=(t // tm,),
        in_specs=[row(d), row(d), row(4 * fs), _full((1, d)), HBM, HBM],
        out_specs=[row(d), row(4 * fs), row(2 * fs), row(d), row(d), _full((1, d))],
        out_shape=[jax.ShapeDtypeStruct((t, d), F32), jax.ShapeDtypeStruct((t, 4 * fs), BF),
                   jax.ShapeDtypeStruct((t, 2 * fs), BF), jax.ShapeDtypeStruct((t, d), BF),
                   jax.ShapeDtypeStruct((t, d), BF), jax.ShapeDtypeStruct((1, d), F32)],
        scratch_shapes=[pltpu.VMEM((4, d, fs), BF), pltpu.VMEM((4 * fr, d), BF), pltpu.SemaphoreType.DMA((8,))],
        compiler_params=_params(("arbitrary",)),
    )(dy, x, h, gain, win_g, wout_g)


def _wgrad(a, b, out_shape, out_spec, tm, tn, name, tk=512):
    t, m = a.shape
    _, n = b.shape
    tk = _tile(t, tk)
    nk = t // tk

    def body(a_ref, b_ref, o_ref, acc):
        k = pl.program_id(2)

        @pl.when(k == 0)
        def _():
            acc[...] = jnp.zeros_like(acc)

        acc[...] += _dot_tn(a_ref[...], b_ref[...])

        @pl.when(k == nk - 1)
        def _():
            o_ref[...] = acc[...].astype(o_ref.dtype)

    return pl.pallas_call(
        body, name=name, grid=(m // tm, n // tn, nk),
        in_specs=[pl.BlockSpec((tk, tm), lambda i, j, k: (k, i)), pl.BlockSpec((tk, tn), lambda i, j, k: (k, j))],
        out_specs=out_spec, out_shape=out_shape,
        scratch_shapes=[pltpu.VMEM((tm, tn), F32)],
        compiler_params=_params(("parallel", "parallel", "arbitrary")),
    )(a, b)


def _norm_matmul(x, gain, w, name):
    t, d = x.shape
    n = w.shape[1]
    tm = _tile(t, 256)

    def body(x_ref, g_ref, w_ref, o_ref):
        hn, _, _ = _rms_fwd(x_ref[...], g_ref[...])
        o_ref[...] = _dot(hn.astype(BF), w_ref[...])

    return pl.pallas_call(
        body, name=name, grid=(t // tm,),
        in_specs=[pl.BlockSpec((tm, d), lambda i: (i, 0)), _full((1, d)), _full((d, n))],
        out_specs=pl.BlockSpec((tm, n), lambda i: (i, 0)),
        out_shape=jax.ShapeDtypeStruct((t, n), F32),
        compiler_params=_params(("parallel",)),
    )(x, gain, w)


def _norm_matmul_bwd(dres, dproj, x, gain, w, name):
    t, d = x.shape
    n = w.shape[1]
    tm = _tile(t, 256)

    def body(dr_ref, dp_ref, x_ref, g_ref, w_ref, dx_ref, hn_ref, dg_ref):
        @pl.when(pl.program_id(0) == 0)
        def _():
            dg_ref[...] = jnp.zeros_like(dg_ref)

        dhn = _dot_nt(dp_ref[...], w_ref[...])
        gain_v = g_ref[...]
        hn, xhat, rstd = _rms_fwd(x_ref[...], gain_v)
        hn_ref[...] = hn.astype(BF)
        dx, dgr = _rms_bwd(dhn, xhat, rstd, gain_v)
        dx_ref[...] = dr_ref[...] + dx
        dg_ref[...] += jnp.sum(dgr, axis=0, keepdims=True)

    row = lambda wd: pl.BlockSpec((tm, wd), lambda i: (i, 0))
    return pl.pallas_call(
        body, name=name, grid=(t // tm,),
        in_specs=[row(d), row(n), row(d), _full((1, d)), _full((d, n))],
        out_specs=[row(d), row(d), _full((1, d))],
        out_shape=[jax.ShapeDtypeStruct((t, d), F32), jax.ShapeDtypeStruct((t, d), BF),
                   jax.ShapeDtypeStruct((1, d), F32)],
        compiler_params=_params(("arbitrary",)),
    )(dres, dproj, x, gain, w)


def _mix_out(x, yf, yg, w, name):
    t, d = x.shape
    kf = yf.shape[1]
    tm = _tile(t, 512)

    def body(x_ref, yf_ref, yg_ref, w_ref, o_ref):
        o_ref[...] = x_ref[...] + _dot(yf_ref[...], w_ref[0:kf, :]) + _dot(yg_ref[...], w_ref[kf:2 * kf, :])

    row = lambda wd: pl.BlockSpec((tm, wd), lambda i: (i, 0))
    return pl.pallas_call(
        body, name=name, grid=(t // tm,),
        in_specs=[row(d), row(kf), row(kf), _full((2 * kf, d))],
        out_specs=row(d), out_shape=jax.ShapeDtypeStruct((t, d), F32),
        compiler_params=_params(("parallel",)),
    )(x, yf, yg, w)


def _mix_out_bwd(dx, w, name):
    t, d = dx.shape
    kf = w.shape[0] // 2
    tm = _tile(t, 512)

    def body(dx_ref, w_ref, df_ref, dg_ref, dxb_ref):
        dxb = dx_ref[...].astype(BF)
        dxb_ref[...] = dxb
        df_ref[...] = _dot_nt(dxb, w_ref[0:kf, :]).astype(BF)
        dg_ref[...] = _dot_nt(dxb, w_ref[kf:2 * kf, :]).astype(BF)

    row = lambda wd: pl.BlockSpec((tm, wd), lambda i: (i, 0))
    return pl.pallas_call(
        body, name=name, grid=(t // tm,),
        in_specs=[row(d), _full((2 * kf, d))],
        out_specs=[row(kf), row(kf), row(d)],
        out_shape=[jax.ShapeDtypeStruct((t, kf), BF), jax.ShapeDtypeStruct((t, kf), BF),
                   jax.ShapeDtypeStruct((t, d), BF)],
        compiler_params=_params(("parallel",)),
    )(dx, w)


def _loss_grad(y, target, name):
    t, d = y.shape
    tm = _tile(t, 512)

    def body(y_ref, t_ref, l_ref, dy_ref):
        @pl.when(pl.program_id(0) == 0)
        def _():
            l_ref[...] = jnp.zeros_like(l_ref)

        diff = y_ref[...] - t_ref[...]
        dy_ref[...] = diff * (1.0 / d)
        part = jnp.sum(jnp.sum(diff * diff, axis=1, keepdims=True), axis=0, keepdims=True)
        l_ref[...] += part * (0.5 / d)

    row = pl.BlockSpec((tm, d), lambda i: (i, 0))
    return pl.pallas_call(
        body, name=name, grid=(t // tm,),
        in_specs=[row, row], out_specs=[_full((1, 1)), row],
        out_shape=[jax.ShapeDtypeStruct((1, 1), F32), jax.ShapeDtypeStruct((t, d), F32)],
        compiler_params=_params(("arbitrary",)),
    )(y, target)


def _head_sum_matrix(width, head):
    r = lax.broadcasted_iota(jnp.int32, (width, width), 0) // head
    c = lax.broadcasted_iota(jnp.int32, (width, width), 1) // head
    return (r == c).astype(F32)


def _fox_prep(proj, wq_t, wk_t, bias_pad, seq, name):
    t = proj.shape[0]
    ts = _tile(seq, 512)
    tpe = seq // ts
    scale = FOX_HEAD_DIM ** -0.5

    def body(q_ref, k_ref, v_ref, gt_ref, wq_ref, wk_ref, b_ref, qo_ref, ko_ref, vo_ref, cum_ref, carry):
        i = pl.program_id(0)
        bd = _head_sum_matrix(FOX_WIDTH, FOX_HEAD_DIM)

        def norm(xv, wv):
            ms = _dot(xv * xv, bd, HI) * (1.0 / FOX_HEAD_DIM)
            return xv * lax.rsqrt(ms + EPS) * wv

        qo_ref[...] = (norm(q_ref[...], wq_ref[...]) * scale).astype(BF)
        ko_ref[...] = norm(k_ref[...], wk_ref[...]).astype(BF)
        vo_ref[...] = v_ref[...].astype(BF)

        @pl.when(i % tpe == 0)
        def _():
            carry[...] = jnp.zeros_like(carry)

        ls = _log_sigmoid(gt_ref[...] + b_ref[...])
        r = lax.broadcasted_iota(jnp.int32, (ts, ts), 0)
        c = lax.broadcasted_iota(jnp.int32, (ts, ts), 1)
        cum = _dot((r >= c).astype(F32), ls, HI) + carry[...]
        cum_ref[...] = cum
        carry[...] = cum[ts - 1:ts, :]

    blk = lambda j: pl.BlockSpec((ts, FOX_WIDTH), lambda i: (i, j))
    gate = pl.BlockSpec((ts, LANES), lambda i: (i, GATE_COL // LANES))
    out = pl.BlockSpec((ts, FOX_WIDTH), lambda i: (i, 0))
    return pl.pallas_call(
        body, name=name, grid=(t // ts,),
        in_specs=[blk(0), blk(1), blk(2), gate, _full((1, FOX_WIDTH)), _full((1, FOX_WIDTH)), _full((1, LANES))],
        out_specs=[out, out, out, pl.BlockSpec((ts, LANES), lambda i: (i, 0))],
        out_shape=[jax.ShapeDtypeStruct((t, FOX_WIDTH), BF)] * 3 + [jax.ShapeDtypeStruct((t, LANES), F32)],
        scratch_shapes=[pltpu.VMEM((1, LANES), F32)],
        compiler_params=_params(("arbitrary",)),
    )(proj, proj, proj, proj, wq_t, wk_t, bias_pad)


def _fox_prep_bwd(proj, dq, dk, dv, dcum, wq_t, wk_t, bias_pad, seq, name):
    t = proj.shape[0]
    ts = _tile(seq, 512)
    tpe = seq // ts
    nt = t // ts
    scale = FOX_HEAD_DIM ** -0.5

    def body(q_ref, k_ref, gt_ref, dq_ref, dk_ref, dv_ref, dc_ref, wq_ref, wk_ref, b_ref,
             dp_ref, dff_ref, dwq_ref, dwk_ref, db_ref, carry):
        i = pl.program_id(0)
        ti = nt - 1 - i
        bd = _head_sum_matrix(FOX_WIDTH, FOX_HEAD_DIM)

        @pl.when(i == 0)
        def _():
            dwq_ref[...] = jnp.zeros_like(dwq_ref)
            dwk_ref[...] = jnp.zeros_like(dwk_ref)
            db_ref[...] = jnp.zeros_like(db_ref)

        def norm_bwd(xv, wv, dyv):
            ms = _dot(xv * xv, bd, HI) * (1.0 / FOX_HEAD_DIM)
            rstd = lax.rsqrt(ms + EPS)
            xhat = xv * rstd
            dxhat = dyv * wv
            mean = _dot(dxhat * xhat, bd, HI) * (1.0 / FOX_HEAD_DIM)
            return rstd * (dxhat - xhat * mean), jnp.sum(dyv * xhat, axis=0, keepdims=True)

        dxq, dwq = norm_bwd(q_ref[...], wq_ref[...], dq_ref[...] * scale)
        dxk, dwk = norm_bwd(k_ref[...], wk_ref[...], dk_ref[...])
        dp_ref[:, 0:FOX_WIDTH] = dxq.astype(BF)
        dp_ref[:, FOX_WIDTH:2 * FOX_WIDTH] = dxk.astype(BF)
        dp_ref[:, 2 * FOX_WIDTH:3 * FOX_WIDTH] = dv_ref[...].astype(BF)
        dwq_ref[...] += dwq
        dwk_ref[...] += dwk

        @pl.when(ti % tpe == tpe - 1)
        def _():
            carry[...] = jnp.zeros_like(carry)

        r = lax.broadcasted_iota(jnp.int32, (ts, ts), 0)
        c = lax.broadcasted_iota(jnp.int32, (ts, ts), 1)
        dls = _dot((c >= r).astype(F32), dc_ref[...], HI) + carry[...]
        carry[...] = dls[0:1, :]
        z = gt_ref[...] + b_ref[...]
        lane = lax.broadcasted_iota(jnp.int32, (ts, LANES), 1)
        dff = jnp.where(lane < FOX_HEADS, dls * _sigmoid(-z), 0.0)
        dff_ref[...] = dff
        db_ref[...] += jnp.sum(dff, axis=0, keepdims=True)

        @pl.when(i == nt - 1)
        def _():
            fr = lax.broadcasted_iota(jnp.int32, (FOX_WIDTH, FOX_WIDTH), 0) % FOX_HEAD_DIM
            fc = lax.broadcasted_iota(jnp.int32, (FOX_WIDTH, FOX_WIDTH), 1) % FOX_HEAD_DIM
            fold = (fr == fc).astype(F32)
            dwq_ref[...] = _dot(dwq_ref[...], fold, HI)
            dwk_ref[...] = _dot(dwk_ref[...], fold, HI)

    rev = lambda w, j: pl.BlockSpec((ts, w), lambda i: (nt - 1 - i, j))
    return pl.pallas_call(
        body, name=name, grid=(nt,),
        in_specs=[rev(FOX_WIDTH, 0), rev(FOX_WIDTH, 1), rev(LANES, GATE_COL // LANES),
                  rev(FOX_WIDTH, 0), rev(FOX_WIDTH, 0), rev(FOX_WIDTH, 0), rev(LANES, 0),
                  _full((1, FOX_WIDTH)), _full((1, FOX_WIDTH)), _full((1, LANES))],
        out_specs=[rev(3 * FOX_WIDTH, 0), rev(LANES, 0), _full((1, FOX_WIDTH)), _full((1, FOX_WIDTH)),
                   _full((1, LANES))],
        out_shape=[jax.ShapeDtypeStruct((t, 3 * FOX_WIDTH), BF), jax.ShapeDtypeStruct((t, LANES), F32),
                   jax.ShapeDtypeStruct((1, FOX_WIDTH), F32), jax.ShapeDtypeStruct((1, FOX_WIDTH), F32),
                   jax.ShapeDtypeStruct((1, LANES), F32)],
        scratch_shapes=[pltpu.VMEM((1, LANES), F32)],
        compiler_params=_params(("arbitrary",)),
    )(proj, proj, proj, dq, dk, dv, dcum, wq_t, wk_t, bias_pad)


def _fox_attn(q, k, v, cq, ck, nb, seq, name):
    t = q.shape[0]
    tq = _tile(seq, 512)
    nq = seq // tq
    npair = FOX_HEADS // 2
    hd = FOX_HEAD_DIM

    def body(q_ref, k_ref, v_ref, cq_ref, ck_ref, o_ref, lse_ref, m_s, l_s, acc_s):
        qi = pl.program_id(2)
        kj = pl.program_id(3)
        lane = lax.broadcasted_iota(jnp.int32, (tq, LANES), 1)

        @pl.when(kj == 0)
        def _():
            m_s[...] = jnp.full(m_s.shape, NEG, F32)
            l_s[...] = jnp.zeros_like(l_s)
            acc_s[...] = jnp.zeros_like(acc_s)

        @pl.when(kj <= qi)
        def _():
            qv = q_ref[...]
            kv = k_ref[...]
            vv = v_ref[...]
            row = qi * tq + lax.broadcasted_iota(jnp.int32, (tq, tq), 0)
            col = kj * tq + lax.broadcasted_iota(jnp.int32, (tq, tq), 1)
            causal = row >= col
            for hh in range(2):
                hm = (lane >= hd) if hh else (lane < hd)
                qh = jnp.where(hm, qv, jnp.zeros_like(qv))
                s = _dot_nt(qh, kv) + (cq_ref[hh] - ck_ref[hh])
                s = jnp.where(causal, s, NEG)
                m_old = m_s[hh]
                m_new = jnp.maximum(m_old, jnp.max(s, axis=-1, keepdims=True))
                p = jnp.exp(s - m_new)
                alpha = jnp.exp(m_old - m_new)
                l_s[hh] = alpha * l_s[hh] + jnp.sum(p, axis=-1, keepdims=True)
                m_s[hh] = m_new
                pv = _dot(p.astype(BF), vv)
                acc = acc_s[...]
                acc_s[...] = jnp.where(hm, alpha * acc + pv, acc)

        @pl.when(kj == qi)
        def _():
            inv = jnp.where(lane < hd, 1.0 / l_s[0], 1.0 / l_s[1])
            o_ref[...] = (acc_s[...] * inv).astype(o_ref.dtype)
            lse_ref[0] = m_s[0] + jnp.log(l_s[0])
            lse_ref[1] = m_s[1] + jnp.log(l_s[1])

    qspec = pl.BlockSpec((tq, LANES), lambda b, p, i, j: (b * nq + i, p))
    kspec = pl.BlockSpec((tq, LANES), lambda b, p, i, j: (b * nq + jnp.minimum(j, i), p))
    colspec = pl.BlockSpec((None, 2, tq, 1), lambda b, p, i, j: (b * npair + p, 0, i, 0))
    rowspec = pl.BlockSpec((None, 2, 1, tq), lambda b, p, i, j: (b * npair + p, 0, 0, jnp.minimum(j, i)))
    return pl.pallas_call(
        body, name=name, grid=(nb, npair, nq, nq),
        in_specs=[qspec, kspec, kspec, colspec, rowspec],
        out_specs=[qspec, colspec],
        out_shape=[jax.ShapeDtypeStruct((t, FOX_WIDTH), BF), jax.ShapeDtypeStruct((nb * npair, 2, seq, 1), F32)],
        scratch_shapes=[pltpu.VMEM((2, tq, 1), F32), pltpu.VMEM((2, tq, 1), F32), pltpu.VMEM((tq, LANES), F32)],
        compiler_params=_params(("parallel", "parallel", "parallel", "arbitrary")),
    )(q, k, v, cq, ck)


def _fox_attn_bwd(q, k, v, o, do, lse, cq, ck, nb, seq, name):
    t = q.shape[0]
    tq = _tile(seq, 512)
    nq = seq // tq
    npair = FOX_HEADS // 2
    hd = FOX_HEAD_DIM

    def body(q_ref, k_ref, v_ref, o_ref, do_ref, lse_ref, cq_ref, ck_ref,
             dq_ref, dk_ref, dv_ref, dcq_ref, dck_ref, dk_s, dv_s, dck_s):
        kj = pl.program_id(2)
        qi = pl.program_id(3)
        lane = lax.broadcasted_iota(jnp.int32, (tq, LANES), 1)

        @pl.when((kj == 0) & (qi == 0))
        def _():
            dq_ref[...] = jnp.zeros_like(dq_ref)
            dcq_ref[...] = jnp.zeros_like(dcq_ref)

        @pl.when(qi == 0)
        def _():
            dk_s[...] = jnp.zeros_like(dk_s)
            dv_s[...] = jnp.zeros_like(dv_s)
            dck_s[...] = jnp.zeros_like(dck_s)

        @pl.when(qi >= kj)
        def _():
            qv = q_ref[...]
            kv = k_ref[...]
            vv = v_ref[...]
            dov = do_ref[...]
            prod = dov.astype(F32) * o_ref[...].astype(F32)
            row = qi * tq + lax.broadcasted_iota(jnp.int32, (tq, tq), 0)
            col = kj * tq + lax.broadcasted_iota(jnp.int32, (tq, tq), 1)
            causal = row >= col
            rows = pl.ds(pl.multiple_of(qi * tq, tq), tq)
            dq_new = jnp.zeros((tq, LANES), F32)
            rs_new = jnp.zeros((tq, LANES), F32)
            for hh in range(2):
                hm = (lane >= hd) if hh else (lane < hd)
                zero = jnp.zeros_like(qv)
                qh = jnp.where(hm, qv, zero)
                kh = jnp.where(hm, kv, zero)
                doh = jnp.where(hm, dov, zero)
                delta = jnp.sum(jnp.where(hm, prod, 0.0), axis=-1, keepdims=True)
                s = _dot_nt(qh, kv) + (cq_ref[hh] - ck_ref[hh])
                s = jnp.where(causal, s, NEG)
                p = jnp.exp(s - lse_ref[hh])
                dp = _dot_nt(doh, vv)
                ds = p * (dp - delta)
                dsb = ds.astype(BF)
                dv_s[...] += _dot_tn(p.astype(BF), doh)
                dk_s[...] += _dot_tn(dsb, qh)
                dq_new = dq_new + _dot(dsb, kh)
                rs_new = rs_new + jnp.where(hm, jnp.sum(ds, axis=-1, keepdims=True), 0.0)
                dck_s[hh] += jnp.sum(ds, axis=0, keepdims=True)
            dq_ref[rows, :] += dq_new
            dcq_ref[rows, :] += rs_new

        @pl.when(qi == nq - 1)
        def _():
            dk_ref[...] = dk_s[...]
            dv_ref[...] = dv_s[...]
            dck_ref[...] = dck_s[...]

    kspec = pl.BlockSpec((tq, LANES), lambda b, p, j, i: (b * nq + j, p))
    qspec = pl.BlockSpec((tq, LANES), lambda b, p, j, i: (b * nq + jnp.maximum(i, j), p))
    colspec = pl.BlockSpec((None, 2, tq, 1), lambda b, p, j, i: (b * npair + p, 0, jnp.maximum(i, j), 0))
    rowspec = pl.BlockSpec((None, 2, 1, tq), lambda b, p, j, i: (b * npair + p, 0, 0, j))
    full_q = pl.BlockSpec((seq, LANES), lambda b, p, j, i: (b, p))
    return pl.pallas_call(
        body, name=name, grid=(nb, npair, nq, nq),
        in_specs=[qspec, kspec, kspec, qspec, qspec, colspec, colspec, rowspec],
        out_specs=[full_q, kspec, kspec, full_q, rowspec],
        out_shape=[jax.ShapeDtypeStruct((t, FOX_WIDTH), F32), jax.ShapeDtypeStruct((t, FOX_WIDTH), F32),
                   jax.ShapeDtypeStruct((t, FOX_WIDTH), F32), jax.ShapeDtypeStruct((t, FOX_WIDTH), F32),
                   jax.ShapeDtypeStruct((nb * npair, 2, 1, seq), F32)],
        scratch_shapes=[pltpu.VMEM((tq, LANES), F32), pltpu.VMEM((tq, LANES), F32), pltpu.VMEM((2, 1, tq), F32)],
        compiler_params=_params(("parallel", "parallel", "arbitrary", "arbitrary")),
    )(q, k, v, o, do, lse, cq, ck)


GDN_QKV = 3 * GDN_WIDTH
GDN_COL = 3 * FOX_WIDTH
GG_COL = GDN_COL + GDN_QKV
A_LANE = FOX_HEADS
B_LANE = FOX_HEADS + GDN_HEADS
HALO = 8


def _gate_lanes(ts):
    lane = lax.broadcasted_iota(jnp.int32, (ts, LANES), 1)
    return (lane >= A_LANE) & (lane < B_LANE), (lane >= B_LANE) & (lane < B_LANE + GDN_HEADS)


def _chunk_tri(ts, upper):
    r = lax.broadcasted_iota(jnp.int32, (ts, ts), 0)
    c = lax.broadcasted_iota(jnp.int32, (ts, ts), 1)
    same = (r // CHUNK) == (c // CHUNK)
    return (same & ((c >= r) if upper else (r >= c))).astype(F32)


def _conv_silu_l2(xp_ref, w, ts):
    c = w[0:1, :] * xp_ref[pl.ds(HALO - 3, ts), :]
    for kk in range(1, CONV_WIDTH):
        c = c + w[kk:kk + 1, :] * xp_ref[pl.ds(HALO - 3 + kk, ts), :]
    return c, c * _sigmoid(c)


def _gdn_prep(proj, conv_w, a_pad, dt_pad, seq, name):
    t = proj.shape[0]
    ts = _tile(seq, 256)
    tpe = seq // ts
    qscale = GDN_HEAD_DIM ** -0.5

    def body(x_ref, gt_ref, w_ref, a_ref, dt_ref, qo_ref, ko_ref, vo_ref, go_ref, xp):
        i = pl.program_id(0)
        tail = xp[pl.ds(ts, HALO), :]
        xp[pl.ds(0, HALO), :] = jnp.where(i % tpe == 0, jnp.zeros_like(tail), tail)
        xp[pl.ds(HALO, ts), :] = x_ref[...]
        _, s = _conv_silu_l2(xp, w_ref[...], ts)
        for h in range(GDN_HEADS):
            for base, ref, sc in ((0, qo_ref, qscale), (GDN_WIDTH, ko_ref, 1.0)):
                xh = s[:, base + h * LANES: base + (h + 1) * LANES]
                r = lax.rsqrt(jnp.sum(xh * xh, axis=-1, keepdims=True) + EPS)
                ref[:, h * LANES:(h + 1) * LANES] = (xh * (r * sc)).astype(BF)
        vo_ref[...] = s[:, 2 * GDN_WIDTH:].astype(BF)
        gate = gt_ref[...]
        g_raw = -jnp.exp(a_ref[...]) * _softplus(gate + dt_ref[...])
        gc = _dot(_chunk_tri(ts, False), g_raw, HI)
        is_a, is_b = _gate_lanes(ts)
        go_ref[...] = jnp.where(is_a, gc, jnp.where(is_b, _sigmoid(gate), 0.0))

    out = pl.BlockSpec((ts, GDN_WIDTH), lambda i: (i, 0))
    lanes = pl.BlockSpec((ts, LANES), lambda i: (i, 0))
    return pl.pallas_call(
        body, name=name, grid=(t // ts,),
        in_specs=[pl.BlockSpec((ts, GDN_QKV), lambda i: (i, GDN_COL // GDN_QKV)),
                  pl.BlockSpec((ts, LANES), lambda i: (i, GATE_COL // LANES)),
                  _full((CONV_WIDTH, GDN_QKV)), _full((1, LANES)), _full((1, LANES))],
        out_specs=[out, out, out, lanes],
        out_shape=[jax.ShapeDtypeStruct((t, GDN_WIDTH), BF)] * 3 + [jax.ShapeDtypeStruct((t, LANES), F32)],
        scratch_shapes=[pltpu.VMEM((ts + HALO, GDN_QKV), F32)],
        compiler_params=_params(("arbitrary",)),
    )(proj, proj, conv_w, a_pad, dt_pad)


def _gdn_prep_bwd(proj, dq, dk, dv, dgates, dff, conv_w, a_pad, dt_pad, seq, name):
    t = proj.shape[0]
    ts = _tile(seq, 256)
    tpe = seq // ts
    nt = t // ts
    qscale = GDN_HEAD_DIM ** -0.5
    hb = ts // HALO

    def body(x_ref, halo_ref, gt_ref, dq_ref, dk_ref, dv_ref, dgt_ref, dff_ref, w_ref, a_ref, dt_ref,
             dx_ref, dgo_ref, dw_ref, da_ref, ddt_ref, xp, dcp, carry):
        i = pl.program_id(0)
        ti = nt - 1 - i

        @pl.when(i == 0)
        def _():
            dw_ref[...] = jnp.zeros_like(dw_ref)
            da_ref[...] = jnp.zeros_like(da_ref)
            ddt_ref[...] = jnp.zeros_like(ddt_ref)

        halo = halo_ref[...]
        xp[pl.ds(0, HALO), :] = jnp.where(ti % tpe == 0, jnp.zeros_like(halo), halo)
        xp[pl.ds(HALO, ts), :] = x_ref[...]
        w = w_ref[...]
        c, s = _conv_silu_l2(xp, w, ts)
        for h in range(GDN_HEADS):
            for base, ref, sc in ((0, dq_ref, qscale), (GDN_WIDTH, dk_ref, 1.0)):
                lo = base + h * LANES
                xh = s[:, lo:lo + LANES]
                r = lax.rsqrt(jnp.sum(xh * xh, axis=-1, keepdims=True) + EPS)
                y = xh * r
                dy = ref[:, h * LANES:(h + 1) * LANES] * sc
                dcp[pl.ds(0, ts), lo:lo + LANES] = r * (dy - y * jnp.sum(dy * y, axis=-1, keepdims=True))
        dcp[pl.ds(0, ts), 2 * GDN_WIDTH:] = dv_ref[...]
        sg = _sigmoid(c)
        dc = dcp[pl.ds(0, ts), :] * (sg * (1.0 + c * (1.0 - sg)))
        dcp[pl.ds(0, ts), :] = dc
        nxt = carry[...]
        dcp[pl.ds(ts, HALO), :] = jnp.where(ti % tpe == tpe - 1, jnp.zeros_like(nxt), nxt)
        carry[...] = dc[0:HALO, :]
        dx = w[CONV_WIDTH - 1:CONV_WIDTH, :] * dc
        for kk in range(CONV_WIDTH - 1):
            dx = dx + w[kk:kk + 1, :] * dcp[pl.ds(CONV_WIDTH - 1 - kk, ts), :]
        dx_ref[...] = dx.astype(BF)
        for kk in range(CONV_WIDTH):
            dw_ref[kk:kk + 1, :] += jnp.sum(dc * xp[pl.ds(HALO - 3 + kk, ts), :], axis=0, keepdims=True)
        gate = gt_ref[...]
        dgt = dgt_ref[...]
        is_a, is_b = _gate_lanes(ts)
        dg_raw = _dot(_chunk_tri(ts, True), jnp.where(is_a, dgt, 0.0), HI)
        z = gate + dt_ref[...]
        na = -jnp.exp(a_ref[...])
        dga = dg_raw * na * _sigmoid(z)
        beta = _sigmoid(gate)
        dgb = jnp.where(is_b, dgt * beta * (1.0 - beta), 0.0)
        dgo_ref[...] = (dff_ref[...] + dga + dgb).astype(BF)
        ddt_ref[...] += jnp.sum(dga, axis=0, keepdims=True)
        da_ref[...] += jnp.sum(dg_raw * na * _softplus(z), axis=0, keepdims=True)

    rev = lambda wd, j: pl.BlockSpec((ts, wd), lambda i: (nt - 1 - i, j))
    halo_spec = pl.BlockSpec((HALO, GDN_QKV), lambda i: (jnp.maximum((nt - 1 - i) * hb - 1, 0), GDN_COL // GDN_QKV))
    return pl.pallas_call(
        body, name=name, grid=(nt,),
        in_specs=[rev(GDN_QKV, GDN_COL // GDN_QKV), halo_spec, rev(LANES, GATE_COL // LANES),
                  rev(GDN_WIDTH, 0), rev(GDN_WIDTH, 0), rev(GDN_WIDTH, 0), rev(LANES, 0), rev(LANES, 0),
                  _full((CONV_WIDTH, GDN_QKV)), _full((1, LANES)), _full((1, LANES))],
        out_specs=[rev(GDN_QKV, 0), rev(LANES, 0), _full((CONV_WIDTH, GDN_QKV)), _full((1, LANES)),
                   _full((1, LANES))],
        out_shape=[jax.ShapeDtypeStruct((t, GDN_QKV), BF), jax.ShapeDtypeStruct((t, LANES), BF),
                   jax.ShapeDtypeStruct((CONV_WIDTH, GDN_QKV), F32), jax.ShapeDtypeStruct((1, LANES), F32),
                   jax.ShapeDtypeStruct((1, LANES), F32)],
        scratch_shapes=[pltpu.VMEM((ts + HALO, GDN_QKV), F32), pltpu.VMEM((ts + HALO, GDN_QKV), F32),
                        pltpu.VMEM((HALO, GDN_QKV), F32)],
        compiler_params=_params(("arbitrary",)),
    )(proj, proj, proj, dq, dk, dv, dgates, dff, conv_w, a_pad, dt_pad)


PAIR = 2 * CHUNK


def _split_bf16(a):
    hi = a.astype(BF)
    return hi, (a - hi.astype(F32)).astype(BF)


def _dot3(a, b, dims=(((1,), (0,)), ((), ()))):
    ah, al = _split_bf16(a)
    bh, bl = _split_bf16(b)
    dg = lambda x, y: lax.dot_general(x, y, dims, preferred_element_type=F32)
    return dg(ah, bh) + dg(ah, bl) + dg(al, bh)


def _inv_unit_lower(a):
    r = lax.broadcasted_iota(jnp.int32, (PAIR, PAIR), 0)
    c = lax.broadcasted_iota(jnp.int32, (PAIR, PAIR), 1)
    tm = (r == c).astype(F32) - a
    pw = _dot3(a, a)
    for _ in range(4):
        x = _dot3(jnp.concatenate([tm, pw], axis=0), pw)
        tm = tm + x[:PAIR]
        pw = x[PAIR:]
    return tm + _dot3(tm, pw)


def _gdn_pair_local(q, k, v, gc, gr, b):
    r = lax.broadcasted_iota(jnp.int32, (PAIR, PAIR), 0)
    c = lax.broadcasted_iota(jnp.int32, (PAIR, PAIR), 1)
    same = (r // CHUNK) == (c // CHUNK)
    incl = same & (r >= c)
    strict = same & (r > c)
    dm = jnp.exp(jnp.where(incl, gc - gr, NEG))
    e = jnp.exp(gc)
    kb = k * b
    vb = v * b
    kbe = kb * e
    kq = _dot_nt(jnp.concatenate([kb, q], axis=0).astype(BF), k.astype(BF))
    amat = jnp.where(strict, kq[:PAIR] * dm, 0.0)
    pmat = jnp.where(incl, kq[PAIR:] * dm, 0.0)
    lane = lax.broadcasted_iota(jnp.int32, (1, PAIR), 1)
    gl_a = jnp.sum(jnp.where(lane == CHUNK - 1, gr, 0.0), axis=1, keepdims=True)
    gl_b = jnp.sum(jnp.where(lane == PAIR - 1, gr, 0.0), axis=1, keepdims=True)
    ridx = lax.broadcasted_iota(jnp.int32, (PAIR, 1), 0)
    edec = jnp.exp(jnp.where(ridx < CHUNK, gl_a, gl_b) - gc)
    return dict(dm=dm, e=e, kb=kb, vb=vb, kbe=kbe, amat=amat, pmat=pmat, gl_a=gl_a, gl_b=gl_b, edec=edec,
                kd=k * edec, qd=q * e, incl=incl, strict=strict, ridx=ridx)


def _gdn_pair_states(loc, tb, s_a):
    uw = _dot(tb, jnp.concatenate([loc["vb"], loc["kbe"]], axis=1).astype(BF))
    u, w = uw[:, :LANES], uw[:, LANES:]
    qd, kd, c = loc["qd"], loc["kd"], CHUNK
    xa = _dot(jnp.concatenate([qd[:c], w[:c]], axis=0).astype(BF), s_a.astype(BF))
    vn_a = u[:c] - xa[c:]
    s_b = s_a * jnp.exp(loc["gl_a"]) + _dot_tn(kd[:c].astype(BF), vn_a.astype(BF))
    xb = _dot(jnp.concatenate([qd[c:], w[c:]], axis=0).astype(BF), s_b.astype(BF))
    vn_b = u[c:] - xb[c:]
    s_c = s_b * jnp.exp(loc["gl_b"]) + _dot_tn(kd[c:].astype(BF), vn_b.astype(BF))
    vn = jnp.concatenate([vn_a, vn_b], axis=0)
    o = jnp.concatenate([xa[:c], xb[:c]], axis=0) + _dot(loc["pmat"].astype(BF), vn.astype(BF))
    return w, vn, o, s_b, s_c


def _gdn_specs(nb, seq):
    n = seq // CHUNK
    blk = pl.BlockSpec((seq, LANES), lambda i: (i // GDN_HEADS, i % GDN_HEADS))
    gg = pl.BlockSpec((seq, LANES), lambda i: (i // GDN_HEADS, GG_COL // LANES + i % GDN_HEADS))
    col = pl.BlockSpec((None, seq, 1), lambda i: (i, 0, 0))
    rowb = pl.BlockSpec((None, n // 2, HALO, PAIR), lambda i: (i, 0, 0, 0))
    per_pair = pl.BlockSpec((None, n // 2, PAIR, PAIR), lambda i: (i, 0, 0, 0))
    return n, blk, gg, col, rowb, per_pair


def _gdn_fwd(q, k, v, proj, gcol, bcol, grow, wn, nb, seq, name):
    t = q.shape[0]
    n, blk, gg, col, rowb, per_pair = _gdn_specs(nb, seq)

    def body(q_ref, k_ref, v_ref, gg_ref, gc_ref, b_ref, gr_ref, wn_ref, y_ref, tn_ref, sn_ref, s_ref):
        s_ref[...] = jnp.zeros_like(s_ref)
        wnv = wn_ref[...]

        def step(pi, carry):
            rows = pl.ds(pl.multiple_of(pi * PAIR, PAIR), PAIR)
            qv = q_ref[rows, :].astype(F32)
            kv = k_ref[rows, :].astype(F32)
            vv = v_ref[rows, :].astype(F32)
            loc = _gdn_pair_local(qv, kv, vv, gc_ref[rows, :], gr_ref[pi][0:1, :], b_ref[rows, :])
            tf = _inv_unit_lower(loc["amat"])
            tn_ref[pi] = tf
            s_a = s_ref[...]
            _, _, o, _, s_c = _gdn_pair_states(loc, tf.astype(BF), s_a)
            sn_ref[pi] = s_a
            s_ref[...] = s_c
            g = gg_ref[rows, :]
            rstd = lax.rsqrt(jnp.mean(o * o, axis=-1, keepdims=True) + EPS)
            y_ref[rows, :] = (o * rstd * wnv * (g * _sigmoid(g))).astype(BF)
            return carry

        lax.fori_loop(0, n // 2, step, 0)

    nbh = nb * GDN_HEADS
    return pl.pallas_call(
        body, name=name, grid=(nbh,),
        in_specs=[blk, blk, blk, gg, col, col, rowb, _full((1, LANES))],
        out_specs=[blk, per_pair, per_pair],
        out_shape=[jax.ShapeDtypeStruct((t, GDN_WIDTH), BF)] + [jax.ShapeDtypeStruct((nbh, n // 2, PAIR, PAIR), F32)] * 2,
        scratch_shapes=[pltpu.VMEM((GDN_HEAD_DIM, GDN_HEAD_DIM), F32)],
        compiler_params=_params(("parallel",)),
    )(q, k, v, proj, gcol, bcol, grow, wn)


def _gdn_bwd(q, k, v, proj, gcol, bcol, grow, wn, tinv_all, states_all, dy, nb, seq, name):
    t = q.shape[0]
    n, blk, gg, col, rowb, per_pair = _gdn_specs(nb, seq)
    dh = GDN_HEAD_DIM
    npair = n // 2
    c = CHUNK

    def body(q_ref, k_ref, v_ref, gg_ref, gc_ref, b_ref, gr_ref, wn_ref, tn_ref, sn_ref, dy_ref,
             dq_ref, dk_ref, dv_ref, dgg_ref, dgc_ref, dgr_ref, db_ref, dwn_ref, ds_ref):
        @pl.when(pl.program_id(0) == 0)
        def _():
            dwn_ref[...] = jnp.zeros_like(dwn_ref)

        wnv = wn_ref[...]
        ds_ref[...] = jnp.zeros_like(ds_ref)

        def bwd_step(j, carry):
            pi = npair - 1 - j
            rows = pl.ds(pl.multiple_of(pi * PAIR, PAIR), PAIR)
            qv = q_ref[rows, :].astype(F32)
            kv = k_ref[rows, :].astype(F32)
            vv = v_ref[rows, :].astype(F32)
            bv = b_ref[rows, :]
            gcv = gc_ref[rows, :]
            loc = _gdn_pair_local(qv, kv, vv, gcv, gr_ref[pi][0:1, :], bv)
            tf = tn_ref[pi]
            tm = tf.astype(BF)
            s_a = sn_ref[pi]
            kb, vb, kbe, e, dm = loc["kb"], loc["vb"], loc["kbe"], loc["e"], loc["dm"]
            kd, qd, pmat, amat = loc["kd"], loc["qd"], loc["pmat"], loc["amat"]
            w, vn, o, s_b, _ = _gdn_pair_states(loc, tm, s_a)
            g = gg_ref[rows, :]
            sg = _sigmoid(g)
            silu = g * sg
            rstd = lax.rsqrt(jnp.mean(o * o, axis=-1, keepdims=True) + EPS)
            xhat = o * rstd
            dyv = dy_ref[rows, :].astype(F32)
            dwn_ref[...] += jnp.sum(dyv * xhat * silu, axis=0, keepdims=True)
            dgg_ref[rows, :] = (dyv * xhat * wnv * (sg * (1.0 + g * (1.0 - sg)))).astype(BF)
            dxhat = dyv * wnv * silu
            do = rstd * (dxhat - xhat * jnp.mean(dxhat * xhat, axis=-1, keepdims=True))
            dob = do.astype(BF)
            tot = lambda x: jnp.sum(jnp.sum(x, axis=1, keepdims=True), axis=0, keepdims=True)
            rsum = lambda x: jnp.sum(x, axis=1, keepdims=True)
            cat = lambda xs, ax=0: jnp.concatenate(xs, axis=ax)
            wb = w.astype(BF)
            qdb = qd.astype(BF)
            kdb = kd.astype(BF)
            vnb = vn.astype(BF)
            egl_a = jnp.exp(loc["gl_a"])
            egl_b = jnp.exp(loc["gl_b"])
            ptdo = _dot_tn(pmat.astype(BF), dob)
            dsp = ds_ref[...]
            dspb = dsp.astype(BF)
            dvn_b = ptdo[c:] + _dot(kdb[c:], dspb)
            dkd_b = _dot_nt(vnb[c:], dspb)
            dgl_b = egl_b * tot(s_b * dsp) + tot(dkd_b * kd[c:])
            dsm = egl_b * dsp + _dot_tn(cat([qdb[c:], -wb[c:]]), cat([dob[c:], dvn_b.astype(BF)]))
            dsmb = dsm.astype(BF)
            dvn_a = ptdo[:c] + _dot(kdb[:c], dsmb)
            dkd_a = _dot_nt(vnb[:c], dsmb)
            dgl_a = egl_a * tot(s_a * dsm) + tot(dkd_a * kd[:c])
            ds_ref[...] = egl_a * dsm + _dot_tn(cat([qdb[:c], -wb[:c]]), cat([dob[:c], dvn_a.astype(BF)]))
            ya = _dot_nt(cat([dob[:c], dvn_a.astype(BF)]), s_a.astype(BF))
            yb = _dot_nt(cat([dob[c:], dvn_b.astype(BF)]), s_b.astype(BF))
            dqd = cat([ya[:c], yb[:c]])
            dw = -cat([ya[c:], yb[c:]])
            dvn = cat([dvn_a, dvn_b])
            dkd = cat([dkd_a, dkd_b])
            dq = dqd * e
            dgc = rsum(dqd * qd) - rsum(dkd * kd)
            dk = dkd * loc["edec"]
            dpm = jnp.where(loc["incl"], _dot_nt(dob, vnb), 0.0)
            duw = cat([dvn, dw], 1).astype(BF)
            dt = _dot_nt(duw, cat([vb, kbe], 1).astype(BF))
            tt = _dot_tn(tm, duw)
            dvb, dkbe = tt[:, :LANES], tt[:, LANES:]
            tn_dims = (((0,), (0,)), ((), ()))
            nt_dims = (((1,), (1,)), ((), ()))
            da = jnp.where(loc["strict"], -_dot3(_dot3(tf, dt, tn_dims), tf, nt_dims), 0.0)
            st = cat([da * dm, dpm * dm]).astype(BF)
            z = _dot(st, kv.astype(BF))
            dkb = z[:PAIR] + dkbe * e
            dq = dq + z[PAIR:]
            dk = dk + _dot_tn(st, cat([kb, qv]).astype(BF))
            gmat = dpm * pmat + da * amat
            dgc = dgc + rsum(dkbe * kbe) + rsum(gmat)
            ridx = loc["ridx"]
            dgc = dgc + jnp.where(ridx == c - 1, dgl_a, 0.0) + jnp.where(ridx == PAIR - 1, dgl_b, 0.0)
            dgr_ref[pi] = jnp.broadcast_to(jnp.sum(gmat, axis=0, keepdims=True), (HALO, PAIR))
            dq_ref[rows, :] = dq
            dk_ref[rows, :] = dk + dkb * bv
            dv_ref[rows, :] = dvb * bv
            db_ref[rows, :] = rsum(dvb * vv) + rsum(dkb * kv)
            dgc_ref[rows, :] = dgc
            return carry

        lax.fori_loop(0, npair, bwd_step, 0)

    nbh = nb * GDN_HEADS
    return pl.pallas_call(
        body, name=name, grid=(nbh,),
        in_specs=[blk, blk, blk, gg, col, col, rowb, _full((1, LANES)), per_pair, per_pair, blk],
        out_specs=[blk, blk, blk, blk, col, rowb, col, _full((1, LANES))],
        out_shape=[jax.ShapeDtypeStruct((t, GDN_WIDTH), F32)] * 3 + [
            jax.ShapeDtypeStruct((t, GDN_WIDTH), BF),
            jax.ShapeDtypeStruct((nbh, seq, 1), F32),
            jax.ShapeDtypeStruct((nbh, npair, HALO, PAIR), F32),
            jax.ShapeDtypeStruct((nbh, seq, 1), F32),
            jax.ShapeDtypeStruct((1, LANES), F32)],
        scratch_shapes=[pltpu.VMEM((dh, dh), F32)],
        compiler_params=_params(("arbitrary",)),
    )(q, k, v, proj, gcol, bcol, grow, wn, tinv_all, states_all, dy)


def _mix_to_padded(w):
    pad = jnp.zeros(w.shape[:-1] + (N_PAD - N_IN,), w.dtype)
    return jnp.concatenate([w[..., 0:1536], w[..., 1544:3080], w[..., 3088:3600], w[..., 1536:1544],
                            w[..., 3080:3088], pad], axis=-1)


def _mix_from_padded(g):
    return jnp.concatenate([g[..., 0:1536], g[..., 3584:3592], g[..., 1536:3072], g[..., 3592:3600],
                            g[..., 3072:3584]], axis=-1)


def _pad_lanes(vec, start):
    return jnp.pad(vec[None, :], ((0, 0), (start, LANES - start - vec.shape[0])))


def _heads_to_rows(block, lane0, nheads, nb, seq):
    return block[:, lane0:lane0 + nheads].reshape(nb, seq, nheads).transpose(0, 2, 1).reshape(nb * nheads, seq)


def _rows_to_heads(rows, lane0, nheads, nb, seq):
    v = rows.reshape(nb, nheads, seq).transpose(0, 2, 1).reshape(nb * seq, nheads)
    return jnp.pad(v, ((0, 0), (lane0, LANES - lane0 - nheads)))


def _mixer_small(p, l):
    wq_t = jnp.tile(p["fox_q_norm"][l], FOX_HEADS)[None, :]
    wk_t = jnp.tile(p["fox_k_norm"][l], FOX_HEADS)[None, :]
    bias = _pad_lanes(p["fox_f_bias"][l], 0)
    a_pad = _pad_lanes(p["gdn_a_log"][l], A_LANE)
    dt_pad = _pad_lanes(p["gdn_dt_bias"][l], A_LANE)
    wn = p["gdn_out_norm"][l][None, :]
    return wq_t, wk_t, bias, a_pad, dt_pad, wn


def _layer_fwd(x, p, l, nb, seq):
    npair = FOX_HEADS // 2
    n = seq // CHUNK
    wq_t, wk_t, bias, a_pad, dt_pad, wn = _mixer_small(p, l)
    x1, h1 = _ffn_fwd(x, p["ffn1_norm"][l][None, :], p["ffn1_w_in"], p["ffn1_w_out"], l, f"ffn1_fwd_{l}")
    proj = _norm_matmul(x1, p["mix_norm"][l][None, :], p["w_mix"][l], f"mix_in_{l}")
    fq, fk, fv, cum = _fox_prep(proj, wq_t, wk_t, bias, seq, f"fox_prep_{l}")
    c8 = _heads_to_rows(cum, 0, FOX_HEADS, nb, seq)
    cq = c8.reshape(nb * npair, 2, seq, 1)
    ck = c8.reshape(nb * npair, 2, 1, seq)
    o, lse = _fox_attn(fq, fk, fv, cq, ck, nb, seq, f"fox_attn_{l}")
    gq, gk, gv, gates = _gdn_prep(proj, p["gdn_conv"][l], a_pad, dt_pad, seq, f"gdn_prep_{l}")
    gc4 = _heads_to_rows(gates, A_LANE, GDN_HEADS, nb, seq)
    gcol = gc4[:, :, None]
    grow = jnp.broadcast_to(gc4.reshape(nb * GDN_HEADS, n // 2, 1, PAIR), (nb * GDN_HEADS, n // 2, HALO, PAIR))
    bcol = _heads_to_rows(gates, B_LANE, GDN_HEADS, nb, seq)[:, :, None]
    y, tinv, states = _gdn_fwd(gq, gk, gv, proj, gcol, bcol, grow, wn, nb, seq, f"gdn_fwd_{l}")
    x2 = _mix_out(x1, o, y, p["w_out"][l], f"mix_out_{l}")
    x3, h2 = _ffn_fwd(x2, p["ffn2_norm"][l][None, :], p["ffn2_w_in"], p["ffn2_w_out"], l, f"ffn2_fwd_{l}")
    saved = dict(x=x, h1=h1, x1=x1, proj=proj, fq=fq, fk=fk, fv=fv, cq=cq, ck=ck, o=o, lse=lse,
                 gq=gq, gk=gk, gv=gv, gcol=gcol, grow=grow, bcol=bcol, tinv=tinv, states=states, y=y, x2=x2, h2=h2)
    return x3, saved


def _ffn_grads(dy, x, h, gain, win, wout, l, tag):
    t, d = x.shape
    fs = win.shape[3]
    dx, dh, a, hn, dyh, dgain = _ffn_bwd(dy, x, h, gain, win, wout, l, f"{tag}_bwd_{l}")
    g_in = _wgrad(hn, dh, jax.ShapeDtypeStruct((4, d, fs), BF),
                  pl.BlockSpec((None, d, fs), lambda i, j, k: (j, i, 0)), d, fs, f"{tag}_gw_in_{l}")
    g_out = _wgrad(a, dyh, jax.ShapeDtypeStruct((2 * fs, d), BF),
                   pl.BlockSpec((fs, d), lambda i, j, k: (i, j)), fs, d, f"{tag}_gw_out_{l}")
    return dx, dgain[0], g_in, g_out.reshape(4, fs // 2, d)


def _layer_bwd(dx3, p, l, sv, nb, seq):
    npair = FOX_HEADS // 2
    d = dx3.shape[1]
    wq_t, wk_t, bias, a_pad, dt_pad, wn = _mixer_small(p, l)
    g = {}
    dx2, g["ffn2_norm"], g["ffn2_w_in"], g["ffn2_w_out"] = _ffn_grads(
        dx3, sv["x2"], sv["h2"], p["ffn2_norm"][l][None, :], p["ffn2_w_in"], p["ffn2_w_out"], l, "ffn2")
    dyf, dyg, dxb = _mix_out_bwd(dx2, p["w_out"][l], f"mix_out_bwd_{l}")
    half = lambda a, nm: _wgrad(a, dxb, jax.ShapeDtypeStruct((FOX_WIDTH, d), BF),
                                pl.BlockSpec((FOX_WIDTH, d), lambda i, j, k: (i, j)), FOX_WIDTH, d, nm)
    g["w_out"] = jnp.concatenate([half(sv["o"], f"gw_out_fox_{l}"), half(sv["y"], f"gw_out_gdn_{l}")], axis=0)
    dq, dk, dv, dcq, dck = _fox_attn_bwd(sv["fq"], sv["fk"], sv["fv"], sv["o"], dyf, sv["lse"], sv["cq"], sv["ck"],
                                         nb, seq, f"fox_attn_bwd_{l}")
    dcq8 = dcq.reshape(nb * seq, FOX_HEADS, FOX_HEAD_DIM)[:, :, 0]
    dck8 = dck.reshape(nb, FOX_HEADS, seq).transpose(0, 2, 1).reshape(nb * seq, FOX_HEADS)
    dcum = jnp.pad(dcq8 - dck8, ((0, 0), (0, LANES - FOX_HEADS)))
    dpf, dff, dwq, dwk, dbias = _fox_prep_bwd(sv["proj"], dq, dk, dv, dcum, wq_t, wk_t, bias, seq,
                                              f"fox_prep_bwd_{l}")
    g["fox_q_norm"] = dwq[0, :FOX_HEAD_DIM]
    g["fox_k_norm"] = dwk[0, :FOX_HEAD_DIM]
    g["fox_f_bias"] = dbias[0, :FOX_HEADS]
    dgq, dgk, dgv, dgg, dgc_col, dgc_row, dbeta, dwn = _gdn_bwd(
        sv["gq"], sv["gk"], sv["gv"], sv["proj"], sv["gcol"], sv["bcol"], sv["grow"], wn, sv["tinv"], sv["states"],
        dyg, nb, seq, f"gdn_bwd_{l}")
    dgc = dgc_col[:, :, 0] - dgc_row[:, :, 0, :].reshape(nb * GDN_HEADS, seq)
    dgates = _rows_to_heads(dgc, A_LANE, GDN_HEADS, nb, seq) + _rows_to_heads(dbeta[:, :, 0], B_LANE, GDN_HEADS, nb, seq)
    dpg, dgate_blk, dconv, da, ddt = _gdn_prep_bwd(sv["proj"], dgq, dgk, dgv, dgates, dff, p["gdn_conv"][l],
                                                   a_pad, dt_pad, seq, f"gdn_prep_bwd_{l}")
    g["gdn_conv"] = dconv
    g["gdn_a_log"] = da[0, A_LANE:B_LANE]
    g["gdn_dt_bias"] = ddt[0, A_LANE:B_LANE]
    g["gdn_out_norm"] = dwn[0]
    dproj = jnp.concatenate([dpf, dpg, dgg, dgate_blk], axis=1)
    dx1, hnm, dgm = _norm_matmul_bwd(dx2, dproj, sv["x1"], p["mix_norm"][l][None, :], p["w_mix"][l],
                                     f"mix_in_bwd_{l}")
    g["mix_norm"] = dgm[0]
    g["w_mix"] = _wgrad(hnm, dproj, jax.ShapeDtypeStruct((d, N_PAD), F32),
                        pl.BlockSpec((d // 2, N_PAD), lambda i, j, k: (i, j)), d // 2, N_PAD, f"gw_mix_{l}")
    dx0, g["ffn1_norm"], g["ffn1_w_in"], g["ffn1_w_out"] = _ffn_grads(
        dx1, sv["x"], sv["h1"], p["ffn1_norm"][l][None, :], p["ffn1_w_in"], p["ffn1_w_out"], l, "ffn1")
    return dx0, g


def _local_step(x, target, p):
    nb, seq, d = x.shape
    xt = x.reshape(nb * seq, d)
    saved = []
    for l in range(DEPTH):
        xt, sv = _layer_fwd(xt, p, l, nb, seq)
        saved.append(sv)
    loss, dx = _loss_grad(xt, target.reshape(nb * seq, d), "loss")
    grads = [None] * DEPTH
    for l in reversed(range(DEPTH)):
        dx, grads[l] = _layer_bwd(dx, p, l, saved[l], nb, seq)
    return loss, dx.reshape(nb, seq, d), grads


N_CHIPS = 4


def _mesh_pos():
    return lax.axis_index("x"), lax.axis_index("y"), lax.axis_index("c")


def _other_chips(x, y):
    return [(1 - x, y), (x, 1 - y), (1 - x, 1 - y)]


def _remote(src, dst, send_sem, recv_sem, to):
    return pltpu.make_async_remote_copy(src_ref=src, dst_ref=dst, send_sem=send_sem, recv_sem=recv_sem,
                                        device_id=to, device_id_type=MESH)


def _hbm_call(body, name, ins, out_shape, scratch):
    return pl.pallas_call(
        body, name=name, out_shape=out_shape, in_specs=[HBM] * len(ins),
        out_specs=jax.tree.map(lambda _: HBM, out_shape), scratch_shapes=scratch,
        compiler_params=pltpu.CompilerParams(has_side_effects=True),
    )(*ins)


def _all_gather(shards, name):
    n = len(shards)

    def body(*refs):
        ins, outs = refs[:n], refs[n:2 * n]
        send1, recv1, send2, recv2 = refs[2 * n:]
        x, y, c = _mesh_pos()
        me = 2 * x + y
        chips = _other_chips(x, y)
        first = []
        for i in range(n):
            for j, (px, py) in enumerate(chips):
                cp = _remote(ins[i].at[c], outs[i].at[me, c], send1.at[3 * i + j], recv1.at[3 * i + j], (px, py, c))
                cp.start()
                first.append(cp)
        passed = []
        for i in range(n):
            for j, (px, py) in enumerate(chips):
                blk = outs[i].at[2 * px + py, c]
                _remote(blk, blk, send1.at[3 * i + j], recv1.at[3 * i + j], (px, py, c)).wait_recv()
                fw = _remote(blk, blk, send2.at[3 * i + j], recv2.at[3 * i + j], (x, y, 1 - c))
                fw.start()
                passed.append(fw)
        for i in range(n):
            for j, (px, py) in enumerate(chips):
                blk = outs[i].at[2 * px + py, 1 - c]
                _remote(blk, blk, send2.at[3 * i + j], recv2.at[3 * i + j], (x, y, 1 - c)).wait_recv()
        for cp in first + passed:
            cp.wait_send()

    sem = pltpu.SemaphoreType.DMA((3 * n,))
    return _hbm_call(body, name, shards, [jax.ShapeDtypeStruct((N_CHIPS,) + s.shape, s.dtype) for s in shards],
                     [sem, sem, sem, sem])


def _sibling_send_layers(gs, name):
    n = len(gs)

    def body(*refs):
        ins, outs = refs[:n], refs[n:2 * n]
        send, recv = refs[2 * n:]
        x, y, c = _mesh_pos()
        cps = [_remote(ins[i].at[1 - c], outs[i], send.at[i], recv.at[i], (x, y, 1 - c)) for i in range(n)]
        for cp in cps:
            cp.start()
        for cp in cps:
            cp.wait()

    sem = pltpu.SemaphoreType.DMA((n,))
    return _hbm_call(body, name, gs, [jax.ShapeDtypeStruct(g.shape[1:], g.dtype) for g in gs], [sem, sem])


def _chip_scatter(ps, name):
    n = len(ps)

    def body(*refs):
        ins, outs = refs[:n], refs[n:2 * n]
        send, recv = refs[2 * n:]
        x, y, c = _mesh_pos()
        cps = []
        for i in range(n):
            for j, (px, py) in enumerate(_other_chips(x, y)):
                cps.append(_remote(ins[i].at[2 * px + py], outs[i].at[j], send.at[3 * i + j], recv.at[3 * i + j],
                                   (px, py, c)))
        for cp in cps:
            cp.start()
        for cp in cps:
            cp.wait()

    sem = pltpu.SemaphoreType.DMA((3 * n,))
    return _hbm_call(body, name, ps, [jax.ShapeDtypeStruct((3,) + p.shape[1:], p.dtype) for p in ps], [sem, sem])


def _sibling_swap(rs, name):
    n = len(rs)

    def body(*refs):
        ins, outs = refs[:n], refs[n:2 * n]
        send, recv = refs[2 * n:]
        x, y, c = _mesh_pos()
        cps = [_remote(ins[i], outs[i], send.at[i], recv.at[i], (x, y, 1 - c)) for i in range(n)]
        for cp in cps:
            cp.start()
        for cp in cps:
            cp.wait()

    sem = pltpu.SemaphoreType.DMA((n,))
    return _hbm_call(body, name, rs, [jax.ShapeDtypeStruct(r.shape, r.dtype) for r in rs], [sem, sem])


def _small_all_reduce(vec, name):
    r = vec.shape[0]
    ndev = 8

    def body(v_ref, o_ref, buf, send, recv):
        x, y, c = _mesh_pos()
        me = 4 * x + 2 * y + c
        buf[me] = v_ref[...]
        cps = []
        for rel in range(1, ndev):
            px = 1 - x if rel & 4 else x
            py = 1 - y if rel & 2 else y
            pc = 1 - c if rel & 1 else c
            cps.append((_remote(v_ref, buf.at[me], send.at[rel - 1], recv.at[rel - 1], (px, py, pc)),
                        4 * px + 2 * py + pc))
        for cp, _ in cps:
            cp.start()
        for k, (cp, peer) in enumerate(cps):
            slot = buf.at[peer]
            _remote(slot, slot, send.at[k], recv.at[k], (x, y, c)).wait_recv()
        for cp, _ in cps:
            cp.wait_send()
        acc = buf[0]
        for k in range(1, ndev):
            acc = acc + buf[k]
        o_ref[...] = acc

    vm = pl.BlockSpec(memory_space=pltpu.VMEM)
    return pl.pallas_call(
        body, name=name, out_shape=jax.ShapeDtypeStruct(vec.shape, F32), in_specs=[vm], out_specs=vm,
        scratch_shapes=[pltpu.VMEM((ndev, r, LANES), F32), pltpu.SemaphoreType.DMA((ndev - 1,)),
                        pltpu.SemaphoreType.DMA((ndev - 1,))],
        compiler_params=pltpu.CompilerParams(has_side_effects=True),
    )(vec)


def _row_tile(rows, cap=512):
    for t in range(min(rows, cap), 0, -1):
        if rows % t == 0 and (t % 16 == 0 or t == rows):
            return t
    raise ValueError(rows)


def _add_pairs(a, b, name):
    k, r, c = a.shape
    tr = _row_tile(r)

    def body(a_ref, b_ref, o_ref):
        o_ref[...] = (a_ref[...].astype(F32) + b_ref[...].astype(F32)).astype(o_ref.dtype)

    spec = pl.BlockSpec((None, tr, c), lambda i, j: (i, j, 0))
    return pl.pallas_call(body, name=name, grid=(k, r // tr), in_specs=[spec, spec], out_specs=spec,
                          out_shape=jax.ShapeDtypeStruct(a.shape, a.dtype),
                          compiler_params=_params(("parallel", "parallel")))(a, b)


def _final_sum(own, sib, others, name):
    r, c = own.shape
    tr = _row_tile(r)

    def body(a_ref, b_ref, o_ref_in, out_ref):
        acc = a_ref[...].astype(F32) + b_ref[...].astype(F32)
        for k in range(3):
            acc = acc + o_ref_in[k].astype(F32)
        out_ref[...] = acc

    spec = pl.BlockSpec((tr, c), lambda i: (i, 0))
    return pl.pallas_call(body, name=name, grid=(r // tr,),
                          in_specs=[spec, spec, pl.BlockSpec((3, tr, c), lambda i: (0, i, 0))], out_specs=spec,
                          out_shape=jax.ShapeDtypeStruct((r, c), F32),
                          compiler_params=_params(("parallel",)))(own, sib, others)


def _adamw(g, w, m, v, name):
    r, c = g.shape
    tr = _row_tile(r, 256)

    def body(g_ref, w_ref, m_ref, v_ref, d_ref, mo_ref, vo_ref):
        gv = g_ref[...]
        mn = ADAM_B1 * m_ref[...] + (1.0 - ADAM_B1) * gv
        vn = ADAM_B2 * v_ref[...] + (1.0 - ADAM_B2) * (gv * gv)
        m_hat = mn / (1.0 - ADAM_B1 ** ADAM_STEP)
        v_hat = vn / (1.0 - ADAM_B2 ** ADAM_STEP)
        d_ref[...] = -ADAM_LR * (m_hat / (jnp.sqrt(v_hat) + ADAM_EPS) + ADAM_WD * w_ref[...])
        mo_ref[...] = mn
        vo_ref[...] = vn

    spec = pl.BlockSpec((tr, c), lambda i: (i, 0))
    shp = jax.ShapeDtypeStruct((r, c), F32)
    return pl.pallas_call(body, name=name, grid=(r // tr,), in_specs=[spec] * 4, out_specs=[spec] * 3,
                          out_shape=[shp] * 3, compiler_params=_params(("parallel",)))(g, w, m, v)


def _pack(arrays):
    flat = jnp.concatenate([a.reshape(-1).astype(F32) for a in arrays])
    pad = (-flat.shape[0]) % (8 * LANES)
    return jnp.concatenate([flat, jnp.zeros((pad,), F32)]).reshape(-1, LANES)


def _unpack(packed, shapes):
    flat = packed.reshape(-1)
    out, off = [], 0
    for s in shapes:
        size = 1
        for dim in s:
            size *= dim
        out.append(flat[off:off + size].reshape(s))
        off += size
    return out


BIG = ("ffn1_w_in", "ffn1_w_out", "w_in", "w_out", "ffn2_w_in", "ffn2_w_out")
SMALL = ("ffn1_norm", "mix_norm", "fox_q_norm", "fox_k_norm", "fox_f_bias", "gdn_a_log", "gdn_dt_bias",
         "gdn_out_norm", "ffn2_norm", "gdn_conv")
WEIGHTS = ("ffn1_norm", "ffn1_w_in", "ffn1_w_out", "mix_norm", "w_in", "fox_q_norm", "fox_k_norm", "fox_f_bias",
           "gdn_conv", "gdn_a_log", "gdn_dt_bias", "gdn_out_norm", "w_out", "ffn2_norm", "ffn2_w_in", "ffn2_w_out")


def _step(x, target, w, m, v):
    xi, yi, ci = _mesh_pos()
    me = 2 * xi + yi
    depth = DEPTH
    d = x.shape[-1]

    shards = [w[k].astype(BF) for k in BIG] + [w["gdn_conv"]]
    gathered = [lax.dynamic_update_index_in_dim(g, s, me, 0)
                for g, s in zip(_all_gather(shards, "all_gather_weights"), shards)]
    gw = dict(zip(BIG, gathered))
    p = {k: w[k] for k in SMALL if k != "gdn_conv"}
    conv = gathered[len(BIG)]
    p["gdn_conv"] = conv.transpose(1, 2, 0, 3).reshape(depth, CONV_WIDTH, -1)
    for k in ("ffn1_w_in", "ffn1_w_out", "ffn2_w_in", "ffn2_w_out"):
        p[k] = gw[k]
    p["w_mix"] = _mix_to_padded(gw["w_in"].transpose(1, 2, 0, 3).reshape(depth, d, N_IN))
    p["w_out"] = gw["w_out"].transpose(1, 0, 2, 3).reshape(depth, 2 * FOX_WIDTH, d)

    loss, dx, grads = _local_step(x, target, p)

    def transport(k):
        per_layer = []
        for l in range(depth):
            g = grads[l]
            if k == "w_in":
                full = _mix_from_padded(g["w_mix"])
                per_layer.append(full.reshape(d, N_CHIPS, N_IN // N_CHIPS).transpose(1, 0, 2).astype(BF))
            elif k == "w_out":
                per_layer.append(g["w_out"].reshape(N_CHIPS, -1, d))
            else:
                per_layer.append(g[k])
        return jnp.stack(per_layer)

    gs = [transport(k) for k in BIG]
    from_sib = _sibling_send_layers(gs, "grad_to_sibling")
    mine = [lax.dynamic_index_in_dim(g, ci, 0, keepdims=False) for g in gs]
    chip_sums = [_add_pairs(a, b, f"grad_chip_sum_{k}") for a, b, k in zip(mine, from_sib, BIG)]
    from_chips = _chip_scatter(chip_sums, "grad_to_chips")
    reduced = [_final_sum(lax.dynamic_index_in_dim(a, me, 0, keepdims=False),
                          lax.dynamic_index_in_dim(b, me, 0, keepdims=False), o, f"grad_final_sum_{k}")
               for a, b, o, k in zip(mine, from_sib, from_chips, BIG)]
    from_sib_final = _sibling_swap(reduced, "grad_swap_layers")
    full = {k: jnp.stack([jnp.where(ci == 0, a, b), jnp.where(ci == 0, b, a)])
            for k, a, b in zip(BIG, reduced, from_sib_final)}

    out_g, out_d, out_m, out_v = {}, {}, {}, {}
    for k in BIG:
        shp = w[k].shape
        two_d = lambda a: a.reshape(shp[0] * shp[1], shp[2])
        dl, mn, vn = _adamw(two_d(full[k]), two_d(w[k]), two_d(m[k]), two_d(v[k]), f"adamw_{k}")
        out_g[k], out_d[k], out_m[k], out_v[k] = full[k], dl.reshape(shp), mn.reshape(shp), vn.reshape(shp)

    small_local = [jnp.stack([grads[l][k] for l in range(depth)]) for k in SMALL]
    summed = _unpack(_small_all_reduce(_pack(small_local), "small_all_reduce"), [a.shape for a in small_local])
    sg = dict(zip(SMALL, summed))
    cs = w["gdn_conv"].shape[-1]
    sg["gdn_conv"] = lax.dynamic_slice_in_dim(sg["gdn_conv"], me * cs, cs, axis=2)
    shapes = [w[k].shape for k in SMALL]
    packs = [_pack([src[k] for k in SMALL]) for src in (sg, w, m, v)]
    dl, mn, vn = _adamw(*packs, "adamw_small")
    for k, a, b, c2 in zip(SMALL, _unpack(dl, shapes), _unpack(mn, shapes), _unpack(vn, shapes)):
        out_g[k], out_d[k], out_m[k], out_v[k] = sg[k], a, b, c2

    total = lax.psum(loss[0, 0], ("x", "y", "c"))
    return (total, dx, *[out_g[k] for k in WEIGHTS], *[out_d[k] for k in WEIGHTS],
            *[out_m[k] for k in WEIGHTS], *[out_v[k] for k in WEIGHTS])


def kernel(x, ffn1_norm, ffn1_w_in, ffn1_w_out, mix_norm, w_in, fox_q_norm, fox_k_norm, fox_f_bias, gdn_conv, gdn_a_log, gdn_dt_bias, gdn_out_norm, w_out, ffn2_norm, ffn2_w_in, ffn2_w_out, loss_target, m_ffn1_norm, m_ffn1_w_in, m_ffn1_w_out, m_mix_norm, m_w_in, m_fox_q_norm, m_fox_k_norm, m_fox_f_bias, m_gdn_conv, m_gdn_a_log, m_gdn_dt_bias, m_gdn_out_norm, m_w_out, m_ffn2_norm, m_ffn2_w_in, m_ffn2_w_out, v_ffn1_norm, v_ffn1_w_in, v_ffn1_w_out, v_mix_norm, v_w_in, v_fox_q_norm, v_fox_k_norm, v_fox_f_bias, v_gdn_conv, v_gdn_a_log, v_gdn_dt_bias, v_gdn_out_norm, v_w_out, v_ffn2_norm, v_ffn2_w_in, v_ffn2_w_out):
    w = dict(ffn1_norm=ffn1_norm, ffn1_w_in=ffn1_w_in, ffn1_w_out=ffn1_w_out, mix_norm=mix_norm, w_in=w_in,
             fox_q_norm=fox_q_norm, fox_k_norm=fox_k_norm, fox_f_bias=fox_f_bias, gdn_conv=gdn_conv,
             gdn_a_log=gdn_a_log, gdn_dt_bias=gdn_dt_bias, gdn_out_norm=gdn_out_norm, w_out=w_out,
             ffn2_norm=ffn2_norm, ffn2_w_in=ffn2_w_in, ffn2_w_out=ffn2_w_out)
    m = dict(ffn1_norm=m_ffn1_norm, ffn1_w_in=m_ffn1_w_in, ffn1_w_out=m_ffn1_w_out, mix_norm=m_mix_norm, w_in=m_w_in,
             fox_q_norm=m_fox_q_norm, fox_k_norm=m_fox_k_norm, fox_f_bias=m_fox_f_bias, gdn_conv=m_gdn_conv,
             gdn_a_log=m_gdn_a_log, gdn_dt_bias=m_gdn_dt_bias, gdn_out_norm=m_gdn_out_norm, w_out=m_w_out,
             ffn2_norm=m_ffn2_norm, ffn2_w_in=m_ffn2_w_in, ffn2_w_out=m_ffn2_w_out)
    v = dict(ffn1_norm=v_ffn1_norm, ffn1_w_in=v_ffn1_w_in, ffn1_w_out=v_ffn1_w_out, mix_norm=v_mix_norm, w_in=v_w_in,
             fox_q_norm=v_fox_q_norm, fox_k_norm=v_fox_k_norm, fox_f_bias=v_fox_f_bias, gdn_conv=v_gdn_conv,
             gdn_a_log=v_gdn_a_log, gdn_dt_bias=v_gdn_dt_bias, gdn_out_norm=v_gdn_out_norm, w_out=v_w_out,
             ffn2_norm=v_ffn2_norm, ffn2_w_in=v_ffn2_w_in, ffn2_w_out=v_ffn2_w_out)
    return _step(x, loss_target, w, m, v)
```

```python
import jax
import jax.numpy as jnp
from jax import lax
from jax.experimental import pallas as pl
from jax.experimental.pallas import tpu as pltpu

F32 = jnp.float32
BF = jnp.bfloat16
HI = lax.Precision.HIGHEST
MESH = pl.DeviceIdType.MESH

DEPTH = 2
FOX_HEADS = 8
FOX_HEAD_DIM = 64
FOX_WIDTH = 512
GDN_HEADS = 4
GDN_HEAD_DIM = 128
GDN_WIDTH = 512
CONV_WIDTH = 4
CHUNK = 64
EPS = 1e-6
N_IN = 3600
N_PAD = 3712
GATE_COL = 3584
LANES = 128
NEG = -1e30

ADAM_LR = 0.001
ADAM_B1 = 0.9
ADAM_B2 = 0.999
ADAM_EPS = 1e-08
ADAM_WD = 0.01
ADAM_STEP = 10

VMEM_LIMIT = 56 * 1024 * 1024


def _params(sem=None, **kw):
    return pltpu.CompilerParams(dimension_semantics=sem, vmem_limit_bytes=VMEM_LIMIT, **kw)


def _dot(a, b, precision=None):
    return jnp.dot(a, b, preferred_element_type=F32, precision=precision)


def _dot_nt(a, b, precision=None):
    return lax.dot_general(a, b, (((1,), (1,)), ((), ())), preferred_element_type=F32, precision=precision)


def _dot_tn(a, b, precision=None):
    return lax.dot_general(a, b, (((0,), (0,)), ((), ())), preferred_element_type=F32, precision=precision)


def _sigmoid(x):
    return 0.5 * jnp.tanh(0.5 * x) + 0.5


def _softplus(x):
    return jnp.maximum(x, 0.0) + jnp.log(1.0 + jnp.exp(-jnp.abs(x)))


def _log_sigmoid(x):
    return jnp.minimum(x, 0.0) - jnp.log(1.0 + jnp.exp(-jnp.abs(x)))


def _tile(n, t):
    t = min(n, t)
    assert n % t == 0, (n, t)
    return t


def _rms_fwd(x, gain):
    rstd = lax.rsqrt(jnp.mean(x * x, axis=-1, keepdims=True) + EPS)
    xhat = x * rstd
    return xhat * gain, xhat, rstd


def _rms_bwd(dy, xhat, rstd, gain):
    dxhat = dy * gain
    dx = rstd * (dxhat - xhat * jnp.mean(dxhat * xhat, axis=-1, keepdims=True))
    return dx, dy * xhat


def _full(shape):
    nd = len(shape)
    return pl.BlockSpec(shape, lambda *_: (0,) * nd)


HBM = pl.BlockSpec(memory_space=pltpu.HBM)


def _load_ffn_weights(win_hbm, wout_hbm, layer, win_v, wout_v, sem):
    fr = wout_hbm.shape[2]
    copies = [pltpu.make_async_copy(win_hbm.at[s, layer], win_v.at[s], sem.at[s]) for s in range(4)]
    copies += [pltpu.make_async_copy(wout_hbm.at[s, layer], wout_v.at[pl.ds(s * fr, fr)], sem.at[4 + s])
               for s in range(4)]
    for c in copies:
        c.start()
    for c in copies:
        c.wait()


def _ffn_fwd(x, gain, win_g, wout_g, layer, name):
    t, d = x.shape
    _, _, _, fs = win_g.shape
    fr = wout_g.shape[2]
    tm = _tile(t, 256)

    def body(x_ref, g_ref, win_hbm, wout_hbm, xo_ref, h_ref, win_v, wout_v, sem):
        @pl.when(pl.program_id(0) == 0)
        def _():
            _load_ffn_weights(win_hbm, wout_hbm, layer, win_v, wout_v, sem)

        xv = x_ref[...]
        hn, _, _ = _rms_fwd(xv, g_ref[...])
        hn = hn.astype(BF)
        acc = jnp.zeros((tm, d), F32)
        for s in range(2):
            g = _dot(hn, win_v[s])
            u = _dot(hn, win_v[s + 2])
            h_ref[:, s * fs:(s + 1) * fs] = g.astype(BF)
            h_ref[:, (s + 2) * fs:(s + 3) * fs] = u.astype(BF)
            a = (g * _sigmoid(g) * u).astype(BF)
            acc = acc + _dot(a, wout_v[s * fs:(s + 1) * fs, :])
        xo_ref[...] = xv + 0.5 * acc

    return pl.pallas_call(
        body, name=name, grid=(t // tm,),
        in_specs=[pl.BlockSpec((tm, d), lambda i: (i, 0)), _full((1, d)), HBM, HBM],
        out_specs=[pl.BlockSpec((tm, d), lambda i: (i, 0)), pl.BlockSpec((tm, 4 * fs), lambda i: (i, 0))],
        out_shape=[jax.ShapeDtypeStruct((t, d), F32), jax.ShapeDtypeStruct((t, 4 * fs), BF)],
        scratch_shapes=[pltpu.VMEM((4, d, fs), BF), pltpu.VMEM((4 * fr, d), BF), pltpu.SemaphoreType.DMA((8,))],
        compiler_params=_params(("arbitrary",)),
    )(x, gain, win_g, wout_g)


def _ffn_bwd(dy, x, h, gain, win_g, wout_g, layer, name):
    t, d = x.shape
    _, _, _, fs = win_g.shape
    fr = wout_g.shape[2]
    tm = _tile(t, 256)

    def body(dy_ref, x_ref, h_ref, g_ref, win_hbm, wout_hbm,
             dx_ref, dh_ref, a_ref, hn_ref, dyh_ref, dg_ref, win_v, wout_v, sem):
        @pl.when(pl.program_id(0) == 0)
        def _():
            _load_ffn_weights(win_hbm, wout_hbm, layer, win_v, wout_v, sem)
            dg_ref[...] = jnp.zeros_like(dg_ref)

        dyv = dy_ref[...]
        dyh = (0.5 * dyv).astype(BF)
        dyh_ref[...] = dyh
        dhn = jnp.zeros((tm, d), F32)
        for s in range(2):
            da = _dot_nt(dyh, wout_v[s * fs:(s + 1) * fs, :])
            g = h_ref[:, s * fs:(s + 1) * fs].astype(F32)
            u = h_ref[:, (s + 2) * fs:(s + 3) * fs].astype(F32)
            sg = _sigmoid(g)
            si = g * sg
            a_ref[:, s * fs:(s + 1) * fs] = (si * u).astype(BF)
            dgate = (da * u * (sg * (1.0 + g * (1.0 - sg)))).astype(BF)
            dup = (da * si).astype(BF)
            dh_ref[:, s * fs:(s + 1) * fs] = dgate
            dh_ref[:, (s + 2) * fs:(s + 3) * fs] = dup
            dhn = dhn + _dot_nt(dgate, win_v[s]) + _dot_nt(dup, win_v[s + 2])
        xv = x_ref[...]
        gain_v = g_ref[...]
        hn, xhat, rstd = _rms_fwd(xv, gain_v)
        hn_ref[...] = hn.astype(BF)
        dx, dgr = _rms_bwd(dhn, xhat, rstd, gain_v)
        dx_ref[...] = dyv + dx
        dg_ref[...] += jnp.sum(dgr, axis=0, keepdims=True)

    row = lambda w: pl.BlockSpec((tm, w), lambda i: (i, 0))
    return pl.pallas_call(
        body, name=name, grid=(t // tm,),
        in_specs=[row(d), row(d), row(4 * fs), _full((1, d)), HBM, HBM],
        out_specs=[row(d), row(4 * fs), row(2 * fs), row(d), row(d), _full((1, d))],
        out_shape=[jax.ShapeDtypeStruct((t, d), F32), jax.ShapeDtypeStruct((t, 4 * fs), BF),
                   jax.ShapeDtypeStruct((t, 2 * fs), BF), jax.ShapeDtypeStruct((t, d), BF),
                   jax.ShapeDtypeStruct((t, d), BF), jax.ShapeDtypeStruct((1, d), F32)],
        scratch_shapes=[pltpu.VMEM((4, d, fs), BF), pltpu.VMEM((4 * fr, d), BF), pltpu.SemaphoreType.DMA((8,))],
        compiler_params=_params(("arbitrary",)),
    )(dy, x, h, gain, win_g, wout_g)


def _wgrad(a, b, out_shape, out_spec, tm, tn, name, tk=512):
    t, m = a.shape
    _, n = b.shape
    tk = _tile(t, tk)
    nk = t // tk

    def body(a_ref, b_ref, o_ref, acc):
        k = pl.program_id(2)

        @pl.when(k == 0)
        def _():
            acc[...] = jnp.zeros_like(acc)

        acc[...] += _dot_tn(a_ref[...], b_ref[...])

        @pl.when(k == nk - 1)
        def _():
            o_ref[...] = acc[...].astype(o_ref.dtype)

    return pl.pallas_call(
        body, name=name, grid=(m // tm, n // tn, nk),
        in_specs=[pl.BlockSpec((tk, tm), lambda i, j, k: (k, i)), pl.BlockSpec((tk, tn), lambda i, j, k: (k, j))],
        out_specs=out_spec, out_shape=out_shape,
        scratch_shapes=[pltpu.VMEM((tm, tn), F32)],
        compiler_params=_params(("parallel", "parallel", "arbitrary")),
    )(a, b)


def _norm_matmul(x, gain, w, name):
    t, d = x.shape
    n = w.shape[1]
    tm = _tile(t, 256)

    def body(x_ref, g_ref, w_ref, o_ref):
        hn, _, _ = _rms_fwd(x_ref[...], g_ref[...])
        o_ref[...] = _dot(hn.astype(BF), w_ref[...])

    return pl.pallas_call(
        body, name=name, grid=(t // tm,),
        in_specs=[pl.BlockSpec((tm, d), lambda i: (i, 0)), _full((1, d)), _full((d, n))],
        out_specs=pl.BlockSpec((tm, n), lambda i: (i, 0)),
        out_shape=jax.ShapeDtypeStruct((t, n), F32),
        compiler_params=_params(("parallel",)),
    )(x, gain, w)


def _norm_matmul_bwd(dres, dproj, x, gain, w, name):
    t, d = x.shape
    n = w.shape[1]
    tm = _tile(t, 256)

    def body(dr_ref, dp_ref, x_ref, g_ref, w_ref, dx_ref, hn_ref, dg_ref):
        @pl.when(pl.program_id(0) == 0)
        def _():
            dg_ref[...] = jnp.zeros_like(dg_ref)

        dhn = _dot_nt(dp_ref[...], w_ref[...])
        gain_v = g_ref[...]
        hn, xhat, rstd = _rms_fwd(x_ref[...], gain_v)
        hn_ref[...] = hn.astype(BF)
        dx, dgr = _rms_bwd(dhn, xhat, rstd, gain_v)
        dx_ref[...] = dr_ref[...] + dx
        dg_ref[...] += jnp.sum(dgr, axis=0, keepdims=True)

    row = lambda wd: pl.BlockSpec((tm, wd), lambda i: (i, 0))
    return pl.pallas_call(
        body, name=name, grid=(t // tm,),
        in_specs=[row(d), row(n), row(d), _full((1, d)), _full((d, n))],
        out_specs=[row(d), row(d), _full((1, d))],
        out_shape=[jax.ShapeDtypeStruct((t, d), F32), jax.ShapeDtypeStruct((t, d), BF),
                   jax.ShapeDtypeStruct((1, d), F32)],
        compiler_params=_params(("arbitrary",)),
    )(dres, dproj, x, gain, w)


def _mix_out(x, yf, yg, w, name):
    t, d = x.shape
    kf = yf.shape[1]
    tm = _tile(t, 512)

    def body(x_ref, yf_ref, yg_ref, w_ref, o_ref):
        o_ref[...] = x_ref[...] + _dot(yf_ref[...], w_ref[0:kf, :]) + _dot(yg_ref[...], w_ref[kf:2 * kf, :])

    row = lambda wd: pl.BlockSpec((tm, wd), lambda i: (i, 0))
    return pl.pallas_call(
        body, name=name, grid=(t // tm,),
        in_specs=[row(d), row(kf), row(kf), _full((2 * kf, d))],
        out_specs=row(d), out_shape=jax.ShapeDtypeStruct((t, d), F32),
        compiler_params=_params(("parallel",)),
    )(x, yf, yg, w)


def _mix_out_bwd(dx, w, name):
    t, d = dx.shape
    kf = w.shape[0] // 2
    tm = _tile(t, 512)

    def body(dx_ref, w_ref, df_ref, dg_ref, dxb_ref):
        dxb = dx_ref[...].astype(BF)
        dxb_ref[...] = dxb
        df_ref[...] = _dot_nt(dxb, w_ref[0:kf, :]).astype(BF)
        dg_ref[...] = _dot_nt(dxb, w_ref[kf:2 * kf, :]).astype(BF)

    row = lambda wd: pl.BlockSpec((tm, wd), lambda i: (i, 0))
    return pl.pallas_call(
        body, name=name, grid=(t // tm,),
        in_specs=[row(d), _full((2 * kf, d))],
        out_specs=[row(kf), row(kf), row(d)],
        out_shape=[jax.ShapeDtypeStruct((t, kf), BF), jax.ShapeDtypeStruct((t, kf), BF),
                   jax.ShapeDtypeStruct((t, d), BF)],
        compiler_params=_params(("parallel",)),
    )(dx, w)


def _loss_grad(y, target, name):
    t, d = y.shape
    tm = _tile(t, 512)

    def body(y_ref, t_ref, l_ref, dy_ref):
        @pl.when(pl.program_id(0) == 0)
        def _():
            l_ref[...] = jnp.zeros_like(l_ref)

        diff = y_ref[...] - t_ref[...]
        dy_ref[...] = diff * (1.0 / d)
        part = jnp.sum(jnp.sum(diff * diff, axis=1, keepdims=True), axis=0, keepdims=True)
        l_ref[...] += part * (0.5 / d)

    row = pl.BlockSpec((tm, d), lambda i: (i, 0))
    return pl.pallas_call(
        body, name=name, grid=(t // tm,),
        in_specs=[row, row], out_specs=[_full((1, 1)), row],
        out_shape=[jax.ShapeDtypeStruct((1, 1), F32), jax.ShapeDtypeStruct((t, d), F32)],
        compiler_params=_params(("arbitrary",)),
    )(y, target)


def _head_sum_matrix(width, head):
    r = lax.broadcasted_iota(jnp.int32, (width, width), 0) // head
    c = lax.broadcasted_iota(jnp.int32, (width, width), 1) // head
    return (r == c).astype(BF)


def _head_mean(x, bd):
    return _dot(x.astype(BF), bd) * (1.0 / FOX_HEAD_DIM)


def _mask_dot(mask01, x):
    mb = mask01.astype(BF)
    hi = x.astype(BF)
    r1 = x - hi.astype(F32)
    mid = r1.astype(BF)
    lo = (r1 - mid.astype(F32)).astype(BF)
    return _dot(mb, hi) + _dot(mb, mid) + _dot(mb, lo)


def _fox_prep(proj, wq_t, wk_t, bias_pad, seq, name):
    t = proj.shape[0]
    ts = _tile(seq, 512)
    tpe = seq // ts
    scale = FOX_HEAD_DIM ** -0.5

    def body(q_ref, k_ref, v_ref, gt_ref, wq_ref, wk_ref, b_ref, qo_ref, ko_ref, vo_ref, cum_ref, carry):
        i = pl.program_id(0)
        bd = _head_sum_matrix(FOX_WIDTH, FOX_HEAD_DIM)

        def norm(xv, wv):
            ms = _head_mean(xv * xv, bd)
            return xv * lax.rsqrt(ms + EPS) * wv

        qo_ref[...] = (norm(q_ref[...], wq_ref[...]) * scale).astype(BF)
        ko_ref[...] = norm(k_ref[...], wk_ref[...]).astype(BF)
        vo_ref[...] = v_ref[...].astype(BF)

        @pl.when(i % tpe == 0)
        def _():
            carry[...] = jnp.zeros_like(carry)

        ls = _log_sigmoid(gt_ref[...] + b_ref[...])
        r = lax.broadcasted_iota(jnp.int32, (ts, ts), 0)
        c = lax.broadcasted_iota(jnp.int32, (ts, ts), 1)
        cum = _mask_dot(r >= c, ls) + carry[...]
        cum_ref[...] = cum
        carry[...] = cum[ts - 1:ts, :]

    blk = lambda j: pl.BlockSpec((ts, FOX_WIDTH), lambda i: (i, j))
    gate = pl.BlockSpec((ts, LANES), lambda i: (i, GATE_COL // LANES))
    out = pl.BlockSpec((ts, FOX_WIDTH), lambda i: (i, 0))
    return pl.pallas_call(
        body, name=name, grid=(t // ts,),
        in_specs=[blk(0), blk(1), blk(2), gate, _full((1, FOX_WIDTH)), _full((1, FOX_WIDTH)), _full((1, LANES))],
        out_specs=[out, out, out, pl.BlockSpec((ts, LANES), lambda i: (i, 0))],
        out_shape=[jax.ShapeDtypeStruct((t, FOX_WIDTH), BF)] * 3 + [jax.ShapeDtypeStruct((t, LANES), F32)],
        scratch_shapes=[pltpu.VMEM((1, LANES), F32)],
        compiler_params=_params(("arbitrary",)),
    )(proj, proj, proj, proj, wq_t, wk_t, bias_pad)


def _fox_prep_bwd(proj, dqa, dqb, dk, dv, dcum, wq_t, wk_t, bias_pad, seq, name):
    t = proj.shape[0]
    ts = _tile(seq, 512)
    tpe = seq // ts
    nt = t // ts
    scale = FOX_HEAD_DIM ** -0.5

    def body(q_ref, k_ref, gt_ref, dqa_ref, dqb_ref, dk_ref, dv_ref, dc_ref, wq_ref, wk_ref, b_ref,
             dp_ref, dff_ref, dwq_ref, dwk_ref, db_ref, carry):
        i = pl.program_id(0)
        first = (lax.broadcasted_iota(jnp.int32, (ts, FOX_WIDTH), 1) % LANES) < FOX_HEAD_DIM
        dq_all = jnp.where(first, dqa_ref[...], dqb_ref[...])
        ti = nt - 1 - i
        bd = _head_sum_matrix(FOX_WIDTH, FOX_HEAD_DIM)

        @pl.when(i == 0)
        def _():
            dwq_ref[...] = jnp.zeros_like(dwq_ref)
            dwk_ref[...] = jnp.zeros_like(dwk_ref)
            db_ref[...] = jnp.zeros_like(db_ref)

        def norm_bwd(xv, wv, dyv):
            ms = _head_mean(xv * xv, bd)
            rstd = lax.rsqrt(ms + EPS)
            xhat = xv * rstd
            dxhat = dyv * wv
            mean = _head_mean(dxhat * xhat, bd)
            return rstd * (dxhat - xhat * mean), jnp.sum(dyv * xhat, axis=0, keepdims=True)

        dxq, dwq = norm_bwd(q_ref[...], wq_ref[...], dq_all * scale)
        dxk, dwk = norm_bwd(k_ref[...], wk_ref[...], dk_ref[...])
        dp_ref[:, 0:FOX_WIDTH] = dxq.astype(BF)
        dp_ref[:, FOX_WIDTH:2 * FOX_WIDTH] = dxk.astype(BF)
        dp_ref[:, 2 * FOX_WIDTH:3 * FOX_WIDTH] = dv_ref[...].astype(BF)
        dwq_ref[...] += dwq
        dwk_ref[...] += dwk

        @pl.when(ti % tpe == tpe - 1)
        def _():
            carry[...] = jnp.zeros_like(carry)

        r = lax.broadcasted_iota(jnp.int32, (ts, ts), 0)
        c = lax.broadcasted_iota(jnp.int32, (ts, ts), 1)
        dls = _mask_dot(c >= r, dc_ref[...]) + carry[...]
        carry[...] = dls[0:1, :]
        z = gt_ref[...] + b_ref[...]
        lane = lax.broadcasted_iota(jnp.int32, (ts, LANES), 1)
        dff = jnp.where(lane < FOX_HEADS, dls * _sigmoid(-z), 0.0)
        dff_ref[...] = dff
        db_ref[...] += jnp.sum(dff, axis=0, keepdims=True)

        @pl.when(i == nt - 1)
        def _():
            fr = lax.broadcasted_iota(jnp.int32, (FOX_WIDTH, FOX_WIDTH), 0) % FOX_HEAD_DIM
            fc = lax.broadcasted_iota(jnp.int32, (FOX_WIDTH, FOX_WIDTH), 1) % FOX_HEAD_DIM
            fold = (fr == fc).astype(F32)
            dwq_ref[...] = _dot(dwq_ref[...], fold, HI)
            dwk_ref[...] = _dot(dwk_ref[...], fold, HI)

    rev = lambda w, j: pl.BlockSpec((ts, w), lambda i: (nt - 1 - i, j))
    return pl.pallas_call(
        body, name=name, grid=(nt,),
        in_specs=[rev(FOX_WIDTH, 0), rev(FOX_WIDTH, 1), rev(LANES, GATE_COL // LANES),
                  rev(FOX_WIDTH, 0), rev(FOX_WIDTH, 0), rev(FOX_WIDTH, 0), rev(FOX_WIDTH, 0), rev(LANES, 0),
                  _full((1, FOX_WIDTH)), _full((1, FOX_WIDTH)), _full((1, LANES))],
        out_specs=[rev(3 * FOX_WIDTH, 0), rev(LANES, 0), _full((1, FOX_WIDTH)), _full((1, FOX_WIDTH)),
                   _full((1, LANES))],
        out_shape=[jax.ShapeDtypeStruct((t, 3 * FOX_WIDTH), BF), jax.ShapeDtypeStruct((t, LANES), F32),
                   jax.ShapeDtypeStruct((1, FOX_WIDTH), F32), jax.ShapeDtypeStruct((1, FOX_WIDTH), F32),
                   jax.ShapeDtypeStruct((1, LANES), F32)],
        scratch_shapes=[pltpu.VMEM((1, LANES), F32)],
        compiler_params=_params(("arbitrary",)),
    )(proj, proj, proj, dqa, dqb, dk, dv, dcum, wq_t, wk_t, bias_pad)


def _fox_attn(q, k, v, ck, nb, seq, name):
    t = q.shape[0]
    tq = _tile(seq, 512)
    nq = seq // tq
    npair = FOX_HEADS // 2
    hd = FOX_HEAD_DIM

    def body(q_ref, k_ref, v_ref, ck_ref, o_ref, lse_ref, m_s, acc_s):
        qi = pl.program_id(2)
        kj = pl.program_id(3)
        lane = lax.broadcasted_iota(jnp.int32, (tq, LANES), 1)

        @pl.when(kj == 0)
        def _():
            m_s[...] = jnp.full(m_s.shape, NEG, F32)
            acc_s[...] = jnp.zeros_like(acc_s)

        def tile(on_diagonal):
            qv = q_ref[...]
            kv = k_ref[...]
            vv = v_ref[...]
            if on_diagonal:
                causal = (lax.broadcasted_iota(jnp.int32, (tq, tq), 0)
                          >= lax.broadcasted_iota(jnp.int32, (tq, tq), 1))
            for hh in range(2):
                hm = (lane >= hd) if hh else (lane < hd)
                qh = jnp.where(hm, qv, jnp.zeros_like(qv))
                s = _dot_nt(qh, kv) - ck_ref[hh]
                if on_diagonal:
                    s = jnp.where(causal, s, NEG)
                m_old = m_s[hh]
                m_new = jnp.maximum(m_old, jnp.max(s, axis=-1, keepdims=True))
                p = jnp.exp(s - m_new)
                alpha = jnp.exp(m_old - m_new)
                m_s[hh] = m_new
                acc_s[hh] = alpha * acc_s[hh] + _dot(p.astype(BF), jnp.where(hm, vv, jnp.ones_like(vv)))

        @pl.when(kj < qi)
        def _():
            tile(False)

        @pl.when(kj == qi)
        def _():
            tile(True)
            a0 = acc_s[0]
            a1 = acc_s[1]
            den = jnp.where(lane < hd, pltpu.roll(a0, hd, axis=1), pltpu.roll(a1, hd, axis=1))
            o_ref[...] = (jnp.where(lane < hd, a0, a1) / den).astype(o_ref.dtype)
            l0 = jnp.sum(jnp.where(lane == hd, a0, 0.0), axis=1, keepdims=True)
            l1 = jnp.sum(jnp.where(lane == 0, a1, 0.0), axis=1, keepdims=True)
            lse_ref[0] = m_s[0] + jnp.log(l0)
            lse_ref[1] = m_s[1] + jnp.log(l1)

    qspec = pl.BlockSpec((tq, LANES), lambda b, p, i, j: (b * nq + i, p))
    kspec = pl.BlockSpec((tq, LANES), lambda b, p, i, j: (b * nq + jnp.minimum(j, i), p))
    colspec = pl.BlockSpec((None, 2, tq, 1), lambda b, p, i, j: (b * npair + p, 0, i, 0))
    rowspec = pl.BlockSpec((None, 2, 1, tq), lambda b, p, i, j: (b * npair + p, 0, 0, jnp.minimum(j, i)))
    return pl.pallas_call(
        body, name=name, grid=(nb, npair, nq, nq),
        in_specs=[qspec, kspec, kspec, rowspec],
        out_specs=[qspec, colspec],
        out_shape=[jax.ShapeDtypeStruct((t, FOX_WIDTH), BF), jax.ShapeDtypeStruct((nb * npair, 2, seq, 1), F32)],
        scratch_shapes=[pltpu.VMEM((2, tq, 1), F32), pltpu.VMEM((2, tq, LANES), F32)],
        compiler_params=_params(("parallel", "parallel", "parallel", "arbitrary")),
    )(q, k, v, ck)


def _fox_attn_bwd(q, k, v, o, do, lse, ck, nb, seq, name):
    t = q.shape[0]
    tq = _tile(seq, 512)
    nq = seq // tq
    npair = FOX_HEADS // 2
    hd = FOX_HEAD_DIM

    def body(q_ref, k_ref, v_ref, o_ref, do_ref, lse_ref, ck_ref,
             dqa_ref, dqb_ref, dk_ref, dv_ref, dkx_ref, dk_s, dv_s):
        kj = pl.program_id(2)
        qi = pl.program_id(3)
        lane = lax.broadcasted_iota(jnp.int32, (tq, LANES), 1)

        @pl.when((kj == 0) & (qi == 0))
        def _():
            dqa_ref[...] = jnp.zeros_like(dqa_ref)
            dqb_ref[...] = jnp.zeros_like(dqb_ref)

        @pl.when(qi == 0)
        def _():
            dk_s[...] = jnp.zeros_like(dk_s)
            dv_s[...] = jnp.zeros_like(dv_s)

        def tile(on_diagonal):
            qv = q_ref[...]
            kv = k_ref[...]
            vv = v_ref[...]
            dov = do_ref[...]
            prod = dov.astype(F32) * o_ref[...].astype(F32)
            if on_diagonal:
                causal = (lax.broadcasted_iota(jnp.int32, (tq, tq), 0)
                          >= lax.broadcasted_iota(jnp.int32, (tq, tq), 1))
            rows = pl.ds(pl.multiple_of(qi * tq, tq), tq)
            for hh, dq_ref in ((0, dqa_ref), (1, dqb_ref)):
                hm = (lane >= hd) if hh else (lane < hd)
                zero = jnp.zeros_like(qv)
                one = jnp.ones_like(qv)
                doh = jnp.where(hm, dov, zero)
                delta = jnp.sum(jnp.where(hm, prod, 0.0), axis=-1, keepdims=True)
                s = _dot_nt(jnp.where(hm, qv, zero), kv) - ck_ref[hh]
                if on_diagonal:
                    s = jnp.where(causal, s, NEG)
                p = jnp.exp(s - lse_ref[hh])
                dp = _dot_nt(doh, vv)
                dsb = (p * (dp - delta)).astype(BF)
                dv_s[...] += _dot_tn(p.astype(BF), doh)
                dk_s[hh] += _dot_tn(dsb, jnp.where(hm, qv, one))
                dq_ref[rows, :] += _dot(dsb, jnp.where(hm, kv, one))

        @pl.when(qi > kj)
        def _():
            tile(False)

        @pl.when(qi == kj)
        def _():
            tile(True)

        @pl.when(qi == nq - 1)
        def _():
            dk_ref[...] = jnp.where(lane < hd, dk_s[0], dk_s[1])
            dkx_ref[...] = jnp.where(lane < hd, dk_s[1], dk_s[0])
            dv_ref[...] = dv_s[...]

    kspec = pl.BlockSpec((tq, LANES), lambda b, p, j, i: (b * nq + j, p))
    qspec = pl.BlockSpec((tq, LANES), lambda b, p, j, i: (b * nq + jnp.maximum(i, j), p))
    colspec = pl.BlockSpec((None, 2, tq, 1), lambda b, p, j, i: (b * npair + p, 0, jnp.maximum(i, j), 0))
    rowspec = pl.BlockSpec((None, 2, 1, tq), lambda b, p, j, i: (b * npair + p, 0, 0, j))
    full_q = pl.BlockSpec((seq, LANES), lambda b, p, j, i: (b, p))
    return pl.pallas_call(
        body, name=name, grid=(nb, npair, nq, nq),
        in_specs=[qspec, kspec, kspec, qspec, qspec, colspec, rowspec],
        out_specs=[full_q, full_q, kspec, kspec, kspec],
        out_shape=[jax.ShapeDtypeStruct((t, FOX_WIDTH), F32)] * 5,
        scratch_shapes=[pltpu.VMEM((2, tq, LANES), F32), pltpu.VMEM((tq, LANES), F32)],
        compiler_params=_params(("parallel", "parallel", "arbitrary", "arbitrary")),
    )(q, k, v, o, do, lse, ck)


GDN_QKV = 3 * GDN_WIDTH
GDN_COL = 3 * FOX_WIDTH
GG_COL = GDN_COL + GDN_QKV
A_LANE = FOX_HEADS
B_LANE = FOX_HEADS + GDN_HEADS
HALO = 8


def _gate_lanes(ts):
    lane = lax.broadcasted_iota(jnp.int32, (ts, LANES), 1)
    return (lane >= A_LANE) & (lane < B_LANE), (lane >= B_LANE) & (lane < B_LANE + GDN_HEADS)


def _chunk_tri(ts, upper):
    r = lax.broadcasted_iota(jnp.int32, (ts, ts), 0)
    c = lax.broadcasted_iota(jnp.int32, (ts, ts), 1)
    same = (r // CHUNK) == (c // CHUNK)
    return (same & ((c >= r) if upper else (r >= c))).astype(F32)


def _conv_silu_l2(xp_ref, w, ts):
    c = w[0:1, :] * xp_ref[pl.ds(HALO - 3, ts), :]
    for kk in range(1, CONV_WIDTH):
        c = c + w[kk:kk + 1, :] * xp_ref[pl.ds(HALO - 3 + kk, ts), :]
    return c, c * _sigmoid(c)


def _gdn_prep(proj, conv_w, a_pad, dt_pad, seq, name):
    t = proj.shape[0]
    ts = _tile(seq, 256)
    tpe = seq // ts
    qscale = GDN_HEAD_DIM ** -0.5

    def body(x_ref, gt_ref, w_ref, a_ref, dt_ref, qo_ref, ko_ref, vo_ref, go_ref, xp):
        i = pl.program_id(0)
        tail = xp[pl.ds(ts, HALO), :]
        xp[pl.ds(0, HALO), :] = jnp.where(i % tpe == 0, jnp.zeros_like(tail), tail)
        xp[pl.ds(HALO, ts), :] = x_ref[...]
        _, s = _conv_silu_l2(xp, w_ref[...], ts)
        for h in range(GDN_HEADS):
            for base, ref, sc in ((0, qo_ref, qscale), (GDN_WIDTH, ko_ref, 1.0)):
                xh = s[:, base + h * LANES: base + (h + 1) * LANES]
                r = lax.rsqrt(jnp.sum(xh * xh, axis=-1, keepdims=True) + EPS)
                ref[:, h * LANES:(h + 1) * LANES] = (xh * (r * sc)).astype(BF)
        vo_ref[...] = s[:, 2 * GDN_WIDTH:].astype(BF)
        gate = gt_ref[...]
        g_raw = -jnp.exp(a_ref[...]) * _softplus(gate + dt_ref[...])
        gc = _mask_dot(_chunk_tri(ts, False), g_raw)
        is_a, is_b = _gate_lanes(ts)
        go_ref[...] = jnp.where(is_a, gc, jnp.where(is_b, _sigmoid(gate), 0.0))

    out = pl.BlockSpec((ts, GDN_WIDTH), lambda i: (i, 0))
    lanes = pl.BlockSpec((ts, LANES), lambda i: (i, 0))
    return pl.pallas_call(
        body, name=name, grid=(t // ts,),
        in_specs=[pl.BlockSpec((ts, GDN_QKV), lambda i: (i, GDN_COL // GDN_QKV)),
                  pl.BlockSpec((ts, LANES), lambda i: (i, GATE_COL // LANES)),
                  _full((CONV_WIDTH, GDN_QKV)), _full((1, LANES)), _full((1, LANES))],
        out_specs=[out, out, out, lanes],
        out_shape=[jax.ShapeDtypeStruct((t, GDN_WIDTH), BF)] * 3 + [jax.ShapeDtypeStruct((t, LANES), F32)],
        scratch_shapes=[pltpu.VMEM((ts + HALO, GDN_QKV), F32)],
        compiler_params=_params(("arbitrary",)),
    )(proj, proj, conv_w, a_pad, dt_pad)


def _gdn_prep_bwd(proj, dq, dk, dv, dgates, dff, conv_w, a_pad, dt_pad, seq, name):
    t = proj.shape[0]
    ts = _tile(seq, 256)
    tpe = seq // ts
    nt = t // ts
    qscale = GDN_HEAD_DIM ** -0.5
    hb = ts // HALO

    def body(x_ref, halo_ref, gt_ref, dq_ref, dk_ref, dv_ref, dgt_ref, dff_ref, w_ref, a_ref, dt_ref,
             dx_ref, dgo_ref, dw_ref, da_ref, ddt_ref, xp, dcp, carry):
        i = pl.program_id(0)
        ti = nt - 1 - i

        @pl.when(i == 0)
        def _():
            dw_ref[...] = jnp.zeros_like(dw_ref)
            da_ref[...] = jnp.zeros_like(da_ref)
            ddt_ref[...] = jnp.zeros_like(ddt_ref)

        halo = halo_ref[...]
        xp[pl.ds(0, HALO), :] = jnp.where(ti % tpe == 0, jnp.zeros_like(halo), halo)
        xp[pl.ds(HALO, ts), :] = x_ref[...]
        w = w_ref[...]
        c, s = _conv_silu_l2(xp, w, ts)
        for h in range(GDN_HEADS):
            for base, ref, sc in ((0, dq_ref, qscale), (GDN_WIDTH, dk_ref, 1.0)):
                lo = base + h * LANES
                xh = s[:, lo:lo + LANES]
                r = lax.rsqrt(jnp.sum(xh * xh, axis=-1, keepdims=True) + EPS)
                y = xh * r
                dy = ref[:, h * LANES:(h + 1) * LANES] * sc
                dcp[pl.ds(0, ts), lo:lo + LANES] = r * (dy - y * jnp.sum(dy * y, axis=-1, keepdims=True))
        dcp[pl.ds(0, ts), 2 * GDN_WIDTH:] = dv_ref[...]
        sg = _sigmoid(c)
        dc = dcp[pl.ds(0, ts), :] * (sg * (1.0 + c * (1.0 - sg)))
        dcp[pl.ds(0, ts), :] = dc
        nxt = carry[...]
        dcp[pl.ds(ts, HALO), :] = jnp.where(ti % tpe == tpe - 1, jnp.zeros_like(nxt), nxt)
        carry[...] = dc[0:HALO, :]
        dx = w[CONV_WIDTH - 1:CONV_WIDTH, :] * dc
        for kk in range(CONV_WIDTH - 1):
            dx = dx + w[kk:kk + 1, :] * dcp[pl.ds(CONV_WIDTH - 1 - kk, ts), :]
        dx_ref[...] = dx.astype(BF)
        for kk in range(CONV_WIDTH):
            dw_ref[kk:kk + 1, :] += jnp.sum(dc * xp[pl.ds(HALO - 3 + kk, ts), :], axis=0, keepdims=True)
        gate = gt_ref[...]
        dgt = dgt_ref[...]
        is_a, is_b = _gate_lanes(ts)
        dg_raw = _mask_dot(_chunk_tri(ts, True), jnp.where(is_a, dgt, 0.0))
        z = gate + dt_ref[...]
        na = -jnp.exp(a_ref[...])
        dga = dg_raw * na * _sigmoid(z)
        beta = _sigmoid(gate)
        dgb = jnp.where(is_b, dgt * beta * (1.0 - beta), 0.0)
        dgo_ref[...] = (dff_ref[...] + dga + dgb).astype(BF)
        ddt_ref[...] += jnp.sum(dga, axis=0, keepdims=True)
        da_ref[...] += jnp.sum(dg_raw * na * _softplus(z), axis=0, keepdims=True)

    rev = lambda wd, j: pl.BlockSpec((ts, wd), lambda i: (nt - 1 - i, j))
    halo_spec = pl.BlockSpec((HALO, GDN_QKV), lambda i: (jnp.maximum((nt - 1 - i) * hb - 1, 0), GDN_COL // GDN_QKV))
    return pl.pallas_call(
        body, name=name, grid=(nt,),
        in_specs=[rev(GDN_QKV, GDN_COL // GDN_QKV), halo_spec, rev(LANES, GATE_COL // LANES),
                  rev(GDN_WIDTH, 0), rev(GDN_WIDTH, 0), rev(GDN_WIDTH, 0), rev(LANES, 0), rev(LANES, 0),
                  _full((CONV_WIDTH, GDN_QKV)), _full((1, LANES)), _full((1, LANES))],
        out_specs=[rev(GDN_QKV, 0), rev(LANES, 0), _full((CONV_WIDTH, GDN_QKV)), _full((1, LANES)),
                   _full((1, LANES))],
        out_shape=[jax.ShapeDtypeStruct((t, GDN_QKV), BF), jax.ShapeDtypeStruct((t, LANES), BF),
                   jax.ShapeDtypeStruct((CONV_WIDTH, GDN_QKV), F32), jax.ShapeDtypeStruct((1, LANES), F32),
                   jax.ShapeDtypeStruct((1, LANES), F32)],
        scratch_shapes=[pltpu.VMEM((ts + HALO, GDN_QKV), F32), pltpu.VMEM((ts + HALO, GDN_QKV), F32),
                        pltpu.VMEM((HALO, GDN_QKV), F32)],
        compiler_params=_params(("arbitrary",)),
    )(proj, proj, proj, dq, dk, dv, dgates, dff, conv_w, a_pad, dt_pad)


PAIR = 2 * CHUNK


def _split_bf16(a):
    hi = a.astype(BF)
    return hi, (a - hi.astype(F32)).astype(BF)


def _dot3(a, b, dims=(((1,), (0,)), ((), ()))):
    ah, al = _split_bf16(a)
    bh, bl = _split_bf16(b)
    dg = lambda x, y: lax.dot_general(x, y, dims, preferred_element_type=F32)
    return dg(ah, bh) + dg(ah, bl) + dg(al, bh)


def _inv_unit_lower(a):
    r = lax.broadcasted_iota(jnp.int32, (PAIR, PAIR), 0)
    c = lax.broadcasted_iota(jnp.int32, (PAIR, PAIR), 1)
    tm = (r == c).astype(F32) - a
    pw = _dot3(a, a)
    for _ in range(4):
        x = _dot3(jnp.concatenate([tm, pw], axis=0), pw)
        tm = tm + x[:PAIR]
        pw = x[PAIR:]
    return tm + _dot3(tm, pw)


def _gdn_pair_local(q, k, v, gc, gr, b):
    r = lax.broadcasted_iota(jnp.int32, (PAIR, PAIR), 0)
    c = lax.broadcasted_iota(jnp.int32, (PAIR, PAIR), 1)
    same = (r // CHUNK) == (c // CHUNK)
    incl = same & (r >= c)
    strict = same & (r > c)
    dm = jnp.exp(jnp.where(incl, gc - gr, NEG))
    e = jnp.exp(gc)
    kb = k * b
    vb = v * b
    kbe = kb * e
    kq = _dot_nt(jnp.concatenate([kb, q], axis=0).astype(BF), k.astype(BF))
    amat = jnp.where(strict, kq[:PAIR] * dm, 0.0)
    pmat = jnp.where(incl, kq[PAIR:] * dm, 0.0)
    lane = lax.broadcasted_iota(jnp.int32, (1, PAIR), 1)
    gl_a = jnp.sum(jnp.where(lane == CHUNK - 1, gr, 0.0), axis=1, keepdims=True)
    gl_b = jnp.sum(jnp.where(lane == PAIR - 1, gr, 0.0), axis=1, keepdims=True)
    ridx = lax.broadcasted_iota(jnp.int32, (PAIR, 1), 0)
    edec = jnp.exp(jnp.where(ridx < CHUNK, gl_a, gl_b) - gc)
    return dict(dm=dm, e=e, kb=kb, vb=vb, kbe=kbe, amat=amat, pmat=pmat, gl_a=gl_a, gl_b=gl_b, edec=edec,
                kd=k * edec, qd=q * e, incl=incl, strict=strict, ridx=ridx)


def _gdn_pair_states(loc, tb, s_a):
    uw = _dot(tb, jnp.concatenate([loc["vb"], loc["kbe"]], axis=1).astype(BF))
    u, w = uw[:, :LANES], uw[:, LANES:]
    qd, kd, c = loc["qd"], loc["kd"], CHUNK
    xa = _dot(jnp.concatenate([qd[:c], w[:c]], axis=0).astype(BF), s_a.astype(BF))
    vn_a = u[:c] - xa[c:]
    s_b = s_a * jnp.exp(loc["gl_a"]) + _dot_tn(kd[:c].astype(BF), vn_a.astype(BF))
    xb = _dot(jnp.concatenate([qd[c:], w[c:]], axis=0).astype(BF), s_b.astype(BF))
    vn_b = u[c:] - xb[c:]
    s_c = s_b * jnp.exp(loc["gl_b"]) + _dot_tn(kd[c:].astype(BF), vn_b.astype(BF))
    vn = jnp.concatenate([vn_a, vn_b], axis=0)
    o = jnp.concatenate([xa[:c], xb[:c]], axis=0) + _dot(loc["pmat"].astype(BF), vn.astype(BF))
    return w, vn, o, s_b, s_c


def _gdn_specs(nb, seq):
    n = seq // CHUNK
    blk = pl.BlockSpec((seq, LANES), lambda i: (i // GDN_HEADS, i % GDN_HEADS))
    gg = pl.BlockSpec((seq, LANES), lambda i: (i // GDN_HEADS, GG_COL // LANES + i % GDN_HEADS))
    col = pl.BlockSpec((None, seq, 1), lambda i: (i, 0, 0))
    rowb = pl.BlockSpec((None, n // 2, HALO, PAIR), lambda i: (i, 0, 0, 0))
    per_pair = pl.BlockSpec((None, n // 2, PAIR, PAIR), lambda i: (i, 0, 0, 0))
    return n, blk, gg, col, rowb, per_pair


def _gdn_fwd(q, k, v, proj, gcol, bcol, grow, wn, nb, seq, name):
    t = q.shape[0]
    n, blk, gg, col, rowb, per_pair = _gdn_specs(nb, seq)

    def body(q_ref, k_ref, v_ref, gg_ref, gc_ref, b_ref, gr_ref, wn_ref, y_ref, tn_ref, sn_ref, s_ref):
        s_ref[...] = jnp.zeros_like(s_ref)
        wnv = wn_ref[...]

        def step(pi, carry):
            rows = pl.ds(pl.multiple_of(pi * PAIR, PAIR), PAIR)
            qv = q_ref[rows, :].astype(F32)
            kv = k_ref[rows, :].astype(F32)
            vv = v_ref[rows, :].astype(F32)
            loc = _gdn_pair_local(qv, kv, vv, gc_ref[rows, :], gr_ref[pi][0:1, :], b_ref[rows, :])
            tf = _inv_unit_lower(loc["amat"])
            tn_ref[pi] = tf
            s_a = s_ref[...]
            _, _, o, _, s_c = _gdn_pair_states(loc, tf.astype(BF), s_a)
            sn_ref[pi] = s_a
            s_ref[...] = s_c
            g = gg_ref[rows, :]
            rstd = lax.rsqrt(jnp.mean(o * o, axis=-1, keepdims=True) + EPS)
            y_ref[rows, :] = (o * rstd * wnv * (g * _sigmoid(g))).astype(BF)
            return carry

        lax.fori_loop(0, n // 2, step, 0)

    nbh = nb * GDN_HEADS
    return pl.pallas_call(
        body, name=name, grid=(nbh,),
        in_specs=[blk, blk, blk, gg, col, col, rowb, _full((1, LANES))],
        out_specs=[blk, per_pair, per_pair],
        out_shape=[jax.ShapeDtypeStruct((t, GDN_WIDTH), BF)] + [jax.ShapeDtypeStruct((nbh, n // 2, PAIR, PAIR), F32)] * 2,
        scratch_shapes=[pltpu.VMEM((GDN_HEAD_DIM, GDN_HEAD_DIM), F32)],
        compiler_params=_params(("parallel",)),
    )(q, k, v, proj, gcol, bcol, grow, wn)


def _gdn_bwd(q, k, v, proj, gcol, bcol, grow, wn, tinv_all, states_all, dy, nb, seq, name):
    t = q.shape[0]
    n, blk, gg, col, rowb, per_pair = _gdn_specs(nb, seq)
    dh = GDN_HEAD_DIM
    npair = n // 2
    c = CHUNK

    def body(q_ref, k_ref, v_ref, gg_ref, gc_ref, b_ref, gr_ref, wn_ref, tn_ref, sn_ref, dy_ref,
             dq_ref, dk_ref, dv_ref, dgg_ref, dgc_ref, dgr_ref, db_ref, dwn_ref, ds_ref):
        @pl.when(pl.program_id(0) == 0)
        def _():
            dwn_ref[...] = jnp.zeros_like(dwn_ref)

        wnv = wn_ref[...]
        ds_ref[...] = jnp.zeros_like(ds_ref)

        def bwd_step(j, carry):
            pi = npair - 1 - j
            rows = pl.ds(pl.multiple_of(pi * PAIR, PAIR), PAIR)
            qv = q_ref[rows, :].astype(F32)
            kv = k_ref[rows, :].astype(F32)
            vv = v_ref[rows, :].astype(F32)
            bv = b_ref[rows, :]
            gcv = gc_ref[rows, :]
            loc = _gdn_pair_local(qv, kv, vv, gcv, gr_ref[pi][0:1, :], bv)
            tf = tn_ref[pi]
            tm = tf.astype(BF)
            s_a = sn_ref[pi]
            kb, vb, kbe, e, dm = loc["kb"], loc["vb"], loc["kbe"], loc["e"], loc["dm"]
            kd, qd, pmat, amat = loc["kd"], loc["qd"], loc["pmat"], loc["amat"]
            w, vn, o, s_b, _ = _gdn_pair_states(loc, tm, s_a)
            g = gg_ref[rows, :]
            sg = _sigmoid(g)
            silu = g * sg
            rstd = lax.rsqrt(jnp.mean(o * o, axis=-1, keepdims=True) + EPS)
            xhat = o * rstd
            dyv = dy_ref[rows, :].astype(F32)
            dwn_ref[...] += jnp.sum(dyv * xhat * silu, axis=0, keepdims=True)
            dgg_ref[rows, :] = (dyv * xhat * wnv * (sg * (1.0 + g * (1.0 - sg)))).astype(BF)
            dxhat = dyv * wnv * silu
            do = rstd * (dxhat - xhat * jnp.mean(dxhat * xhat, axis=-1, keepdims=True))
            dob = do.astype(BF)
            tot = lambda x: jnp.sum(jnp.sum(x, axis=1, keepdims=True), axis=0, keepdims=True)
            rsum = lambda x: jnp.sum(x, axis=1, keepdims=True)
            cat = lambda xs, ax=0: jnp.concatenate(xs, axis=ax)
            wb = w.astype(BF)
            qdb = qd.astype(BF)
            kdb = kd.astype(BF)
            vnb = vn.astype(BF)
            egl_a = jnp.exp(loc["gl_a"])
            egl_b = jnp.exp(loc["gl_b"])
            ptdo = _dot_tn(pmat.astype(BF), dob)
            dsp = ds_ref[...]
            dspb = dsp.astype(BF)
            dvn_b = ptdo[c:] + _dot(kdb[c:], dspb)
            dkd_b = _dot_nt(vnb[c:], dspb)
            dgl_b = egl_b * tot(s_b * dsp) + tot(dkd_b * kd[c:])
            dsm = egl_b * dsp + _dot_tn(cat([qdb[c:], -wb[c:]]), cat([dob[c:], dvn_b.astype(BF)]))
            dsmb = dsm.astype(BF)
            dvn_a = ptdo[:c] + _dot(kdb[:c], dsmb)
            dkd_a = _dot_nt(vnb[:c], dsmb)
            dgl_a = egl_a * tot(s_a * dsm) + tot(dkd_a * kd[:c])
            ds_ref[...] = egl_a * dsm + _dot_tn(cat([qdb[:c], -wb[:c]]), cat([dob[:c], dvn_a.astype(BF)]))
            ya = _dot_nt(cat([dob[:c], dvn_a.astype(BF)]), s_a.astype(BF))
            yb = _dot_nt(cat([dob[c:], dvn_b.astype(BF)]), s_b.astype(BF))
            dqd = cat([ya[:c], yb[:c]])
            dw = -cat([ya[c:], yb[c:]])
            dvn = cat([dvn_a, dvn_b])
            dkd = cat([dkd_a, dkd_b])
            dq = dqd * e
            dgc = rsum(dqd * qd) - rsum(dkd * kd)
            dk = dkd * loc["edec"]
            dpm = jnp.where(loc["incl"], _dot_nt(dob, vnb), 0.0)
            duw = cat([dvn, dw], 1).astype(BF)
            dt = _dot_nt(duw, cat([vb, kbe], 1).astype(BF))
            tt = _dot_tn(tm, duw)
            dvb, dkbe = tt[:, :LANES], tt[:, LANES:]
            tn_dims = (((0,), (0,)), ((), ()))
            nt_dims = (((1,), (1,)), ((), ()))
            da = jnp.where(loc["strict"], -_dot3(_dot3(tf, dt, tn_dims), tf, nt_dims), 0.0)
            st = cat([da * dm, dpm * dm]).astype(BF)
            z = _dot(st, kv.astype(BF))
            dkb = z[:PAIR] + dkbe * e
            dq = dq + z[PAIR:]
            dk = dk + _dot_tn(st, cat([kb, qv]).astype(BF))
            gmat = dpm * pmat + da * amat
            dgc = dgc + rsum(dkbe * kbe) + rsum(gmat)
            ridx = loc["ridx"]
            dgc = dgc + jnp.where(ridx == c - 1, dgl_a, 0.0) + jnp.where(ridx == PAIR - 1, dgl_b, 0.0)
            dgr_ref[pi] = jnp.broadcast_to(jnp.sum(gmat, axis=0, keepdims=True), (HALO, PAIR))
            dq_ref[rows, :] = dq
            dk_ref[rows, :] = dk + dkb * bv
            dv_ref[rows, :] = dvb * bv
            db_ref[rows, :] = rsum(dvb * vv) + rsum(dkb * kv)
            dgc_ref[rows, :] = dgc
            return carry

        lax.fori_loop(0, npair, bwd_step, 0)

    nbh = nb * GDN_HEADS
    return pl.pallas_call(
        body, name=name, grid=(nbh,),
        in_specs=[blk, blk, blk, gg, col, col, rowb, _full((1, LANES)), per_pair, per_pair, blk],
        out_specs=[blk, blk, blk, blk, col, rowb, col, _full((1, LANES))],
        out_shape=[jax.ShapeDtypeStruct((t, GDN_WIDTH), F32)] * 3 + [
            jax.ShapeDtypeStruct((t, GDN_WIDTH), BF),
            jax.ShapeDtypeStruct((nbh, seq, 1), F32),
            jax.ShapeDtypeStruct((nbh, npair, HALO, PAIR), F32),
            jax.ShapeDtypeStruct((nbh, seq, 1), F32),
            jax.ShapeDtypeStruct((1, LANES), F32)],
        scratch_shapes=[pltpu.VMEM((dh, dh), F32)],
        compiler_params=_params(("arbitrary",)),
    )(q, k, v, proj, gcol, bcol, grow, wn, tinv_all, states_all, dy)


def _mix_to_padded(w):
    pad = jnp.zeros(w.shape[:-1] + (N_PAD - N_IN,), w.dtype)
    return jnp.concatenate([w[..., 0:1536], w[..., 1544:3080], w[..., 3088:3600], w[..., 1536:1544],
                            w[..., 3080:3088], pad], axis=-1)


def _mix_from_padded(g):
    return jnp.concatenate([g[..., 0:1536], g[..., 3584:3592], g[..., 1536:3072], g[..., 3592:3600],
                            g[..., 3072:3584]], axis=-1)


def _pad_lanes(vec, start):
    return jnp.pad(vec[None, :], ((0, 0), (start, LANES - start - vec.shape[0])))


def _heads_to_rows(block, lane0, nheads, nb, seq):
    return block[:, lane0:lane0 + nheads].reshape(nb, seq, nheads).transpose(0, 2, 1).reshape(nb * nheads, seq)


def _rows_to_heads(rows, lane0, nheads, nb, seq):
    v = rows.reshape(nb, nheads, seq).transpose(0, 2, 1).reshape(nb * seq, nheads)
    return jnp.pad(v, ((0, 0), (lane0, LANES - lane0 - nheads)))


def _mixer_small(p, l):
    wq_t = jnp.tile(p["fox_q_norm"][l], FOX_HEADS)[None, :]
    wk_t = jnp.tile(p["fox_k_norm"][l], FOX_HEADS)[None, :]
    bias = _pad_lanes(p["fox_f_bias"][l], 0)
    a_pad = _pad_lanes(p["gdn_a_log"][l], A_LANE)
    dt_pad = _pad_lanes(p["gdn_dt_bias"][l], A_LANE)
    wn = p["gdn_out_norm"][l][None, :]
    return wq_t, wk_t, bias, a_pad, dt_pad, wn


def _layer_fwd(x, p, l, nb, seq):
    npair = FOX_HEADS // 2
    n = seq // CHUNK
    wq_t, wk_t, bias, a_pad, dt_pad, wn = _mixer_small(p, l)
    x1, h1 = _ffn_fwd(x, p["ffn1_norm"][l][None, :], p["ffn1_w_in"], p["ffn1_w_out"], l, f"ffn1_fwd_{l}")
    proj = _norm_matmul(x1, p["mix_norm"][l][None, :], p["w_mix"][l], f"mix_in_{l}")
    fq, fk, fv, cum = _fox_prep(proj, wq_t, wk_t, bias, seq, f"fox_prep_{l}")
    c8 = _heads_to_rows(cum, 0, FOX_HEADS, nb, seq)
    ck = c8.reshape(nb * npair, 2, 1, seq)
    o, lse = _fox_attn(fq, fk, fv, ck, nb, seq, f"fox_attn_{l}")
    gq, gk, gv, gates = _gdn_prep(proj, p["gdn_conv"][l], a_pad, dt_pad, seq, f"gdn_prep_{l}")
    gc4 = _heads_to_rows(gates, A_LANE, GDN_HEADS, nb, seq)
    gcol = gc4[:, :, None]
    grow = jnp.broadcast_to(gc4.reshape(nb * GDN_HEADS, n // 2, 1, PAIR), (nb * GDN_HEADS, n // 2, HALO, PAIR))
    bcol = _heads_to_rows(gates, B_LANE, GDN_HEADS, nb, seq)[:, :, None]
    y, tinv, states = _gdn_fwd(gq, gk, gv, proj, gcol, bcol, grow, wn, nb, seq, f"gdn_fwd_{l}")
    x2 = _mix_out(x1, o, y, p["w_out"][l], f"mix_out_{l}")
    x3, h2 = _ffn_fwd(x2, p["ffn2_norm"][l][None, :], p["ffn2_w_in"], p["ffn2_w_out"], l, f"ffn2_fwd_{l}")
    saved = dict(x=x, h1=h1, x1=x1, proj=proj, fq=fq, fk=fk, fv=fv, ck=ck, o=o, lse=lse,
                 gq=gq, gk=gk, gv=gv, gcol=gcol, grow=grow, bcol=bcol, tinv=tinv, states=states, y=y, x2=x2, h2=h2)
    return x3, saved


def _ffn_grads(dy, x, h, gain, win, wout, l, tag):
    t, d = x.shape
    fs = win.shape[3]
    dx, dh, a, hn, dyh, dgain = _ffn_bwd(dy, x, h, gain, win, wout, l, f"{tag}_bwd_{l}")
    g_in = _wgrad(hn, dh, jax.ShapeDtypeStruct((4, d, fs), BF),
                  pl.BlockSpec((None, d, fs), lambda i, j, k: (j, i, 0)), d, fs, f"{tag}_gw_in_{l}")
    g_out = _wgrad(a, dyh, jax.ShapeDtypeStruct((2 * fs, d), BF),
                   pl.BlockSpec((fs, d), lambda i, j, k: (i, j)), fs, d, f"{tag}_gw_out_{l}")
    return dx, dgain[0], g_in, g_out.reshape(4, fs // 2, d)


def _layer_bwd(dx3, p, l, sv, nb, seq):
    npair = FOX_HEADS // 2
    d = dx3.shape[1]
    wq_t, wk_t, bias, a_pad, dt_pad, wn = _mixer_small(p, l)
    g = {}
    dx2, g["ffn2_norm"], g["ffn2_w_in"], g["ffn2_w_out"] = _ffn_grads(
        dx3, sv["x2"], sv["h2"], p["ffn2_norm"][l][None, :], p["ffn2_w_in"], p["ffn2_w_out"], l, "ffn2")
    dyf, dyg, dxb = _mix_out_bwd(dx2, p["w_out"][l], f"mix_out_bwd_{l}")
    half = lambda a, nm: _wgrad(a, dxb, jax.ShapeDtypeStruct((FOX_WIDTH, d), BF),
                                pl.BlockSpec((FOX_WIDTH, d), lambda i, j, k: (i, j)), FOX_WIDTH, d, nm)
    g["w_out"] = jnp.concatenate([half(sv["o"], f"gw_out_fox_{l}"), half(sv["y"], f"gw_out_gdn_{l}")], axis=0)
    dqa, dqb, dk, dv, dkx = _fox_attn_bwd(sv["fq"], sv["fk"], sv["fv"], sv["o"], dyf, sv["lse"], sv["ck"],
                                          nb, seq, f"fox_attn_bwd_{l}")

    def sums(first_head, second_head):
        a = first_head.reshape(nb * seq, npair, LANES)[:, :, FOX_HEAD_DIM]
        b = second_head.reshape(nb * seq, npair, LANES)[:, :, 0]
        return jnp.stack([a, b], axis=2).reshape(nb * seq, FOX_HEADS)

    dcum = jnp.pad(sums(dqa, dqb) - sums(dkx, dkx), ((0, 0), (0, LANES - FOX_HEADS)))
    dpf, dff, dwq, dwk, dbias = _fox_prep_bwd(sv["proj"], dqa, dqb, dk, dv, dcum, wq_t, wk_t, bias, seq,
                                              f"fox_prep_bwd_{l}")
    g["fox_q_norm"] = dwq[0, :FOX_HEAD_DIM]
    g["fox_k_norm"] = dwk[0, :FOX_HEAD_DIM]
    g["fox_f_bias"] = dbias[0, :FOX_HEADS]
    dgq, dgk, dgv, dgg, dgc_col, dgc_row, dbeta, dwn = _gdn_bwd(
        sv["gq"], sv["gk"], sv["gv"], sv["proj"], sv["gcol"], sv["bcol"], sv["grow"], wn, sv["tinv"], sv["states"],
        dyg, nb, seq, f"gdn_bwd_{l}")
    dgc = dgc_col[:, :, 0] - dgc_row[:, :, 0, :].reshape(nb * GDN_HEADS, seq)
    dgates = _rows_to_heads(dgc, A_LANE, GDN_HEADS, nb, seq) + _rows_to_heads(dbeta[:, :, 0], B_LANE, GDN_HEADS, nb, seq)
    dpg, dgate_blk, dconv, da, ddt = _gdn_prep_bwd(sv["proj"], dgq, dgk, dgv, dgates, dff, p["gdn_conv"][l],
                                                   a_pad, dt_pad, seq, f"gdn_prep_bwd_{l}")
    g["gdn_conv"] = dconv
    g["gdn_a_log"] = da[0, A_LANE:B_LANE]
    g["gdn_dt_bias"] = ddt[0, A_LANE:B_LANE]
    g["gdn_out_norm"] = dwn[0]
    dproj = jnp.concatenate([dpf, dpg, dgg, dgate_blk], axis=1)
    dx1, hnm, dgm = _norm_matmul_bwd(dx2, dproj, sv["x1"], p["mix_norm"][l][None, :], p["w_mix"][l],
                                     f"mix_in_bwd_{l}")
    g["mix_norm"] = dgm[0]
    g["w_mix"] = _wgrad(hnm, dproj, jax.ShapeDtypeStruct((d, N_PAD), F32),
                        pl.BlockSpec((d // 2, N_PAD), lambda i, j, k: (i, j)), d // 2, N_PAD, f"gw_mix_{l}")
    dx0, g["ffn1_norm"], g["ffn1_w_in"], g["ffn1_w_out"] = _ffn_grads(
        dx1, sv["x"], sv["h1"], p["ffn1_norm"][l][None, :], p["ffn1_w_in"], p["ffn1_w_out"], l, "ffn1")
    return dx0, g


def _local_step(x, target, p):
    nb, seq, d = x.shape
    xt = x.reshape(nb * seq, d)
    saved = []
    for l in range(DEPTH):
        xt, sv = _layer_fwd(xt, p, l, nb, seq)
        saved.append(sv)
    loss, dx = _loss_grad(xt, target.reshape(nb * seq, d), "loss")
    grads = [None] * DEPTH
    for l in reversed(range(DEPTH)):
        dx, grads[l] = _layer_bwd(dx, p, l, saved[l], nb, seq)
    return loss, dx.reshape(nb, seq, d), grads


N_CHIPS = 4


def _mesh_pos():
    return lax.axis_index("x"), lax.axis_index("y"), lax.axis_index("c")


def _other_chips(x, y):
    return [(1 - x, y), (x, 1 - y), (1 - x, 1 - y)]


def _remote(src, dst, send_sem, recv_sem, to):
    return pltpu.make_async_remote_copy(src_ref=src, dst_ref=dst, send_sem=send_sem, recv_sem=recv_sem,
                                        device_id=to, device_id_type=MESH)


def _hbm_call(body, name, ins, out_shape, scratch):
    return pl.pallas_call(
        body, name=name, out_shape=out_shape, in_specs=[HBM] * len(ins),
        out_specs=jax.tree.map(lambda _: HBM, out_shape), scratch_shapes=scratch,
        compiler_params=pltpu.CompilerParams(has_side_effects=True),
    )(*ins)


def _all_gather(shards, name):
    n = len(shards)

    def body(*refs):
        ins, outs = refs[:n], refs[n:2 * n]
        send1, recv1, send2, recv2 = refs[2 * n:]
        x, y, c = _mesh_pos()
        me = 2 * x + y
        chips = _other_chips(x, y)
        first = []
        for i in range(n):
            for j, (px, py) in enumerate(chips):
                cp = _remote(ins[i].at[c], outs[i].at[me, c], send1.at[3 * i + j], recv1.at[3 * i + j], (px, py, c))
                cp.start()
                first.append(cp)
        passed = []
        for i in range(n):
            for j, (px, py) in enumerate(chips):
                blk = outs[i].at[2 * px + py, c]
                _remote(blk, blk, send1.at[3 * i + j], recv1.at[3 * i + j], (px, py, c)).wait_recv()
                fw = _remote(blk, blk, send2.at[3 * i + j], recv2.at[3 * i + j], (x, y, 1 - c))
                fw.start()
                passed.append(fw)
        for i in range(n):
            for j, (px, py) in enumerate(chips):
                blk = outs[i].at[2 * px + py, 1 - c]
                _remote(blk, blk, send2.at[3 * i + j], recv2.at[3 * i + j], (x, y, 1 - c)).wait_recv()
        for cp in first + passed:
            cp.wait_send()

    sem = pltpu.SemaphoreType.DMA((3 * n,))
    return _hbm_call(body, name, shards, [jax.ShapeDtypeStruct((N_CHIPS,) + s.shape, s.dtype) for s in shards],
                     [sem, sem, sem, sem])


def _sibling_send_layers(gs, name):
    n = len(gs)

    def body(*refs):
        ins, outs = refs[:n], refs[n:2 * n]
        send, recv = refs[2 * n:]
        x, y, c = _mesh_pos()
        cps = [_remote(ins[i].at[1 - c], outs[i], send.at[i], recv.at[i], (x, y, 1 - c)) for i in range(n)]
        for cp in cps:
            cp.start()
        for cp in cps:
            cp.wait()

    sem = pltpu.SemaphoreType.DMA((n,))
    return _hbm_call(body, name, gs, [jax.ShapeDtypeStruct(g.shape[1:], g.dtype) for g in gs], [sem, sem])


def _chip_scatter(ps, name):
    n = len(ps)

    def body(*refs):
        ins, outs = refs[:n], refs[n:2 * n]
        send, recv = refs[2 * n:]
        x, y, c = _mesh_pos()
        cps = []
        for i in range(n):
            for j, (px, py) in enumerate(_other_chips(x, y)):
                cps.append(_remote(ins[i].at[2 * px + py], outs[i].at[j], send.at[3 * i + j], recv.at[3 * i + j],
                                   (px, py, c)))
        for cp in cps:
            cp.start()
        for cp in cps:
            cp.wait()

    sem = pltpu.SemaphoreType.DMA((3 * n,))
    return _hbm_call(body, name, ps, [jax.ShapeDtypeStruct((3,) + p.shape[1:], p.dtype) for p in ps], [sem, sem])


def _sibling_swap(rs, name):
    n = len(rs)

    def body(*refs):
        ins, outs = refs[:n], refs[n:2 * n]
        send, recv = refs[2 * n:]
        x, y, c = _mesh_pos()
        cps = [_remote(ins[i], outs[i], send.at[i], recv.at[i], (x, y, 1 - c)) for i in range(n)]
        for cp in cps:
            cp.start()
        for cp in cps:
            cp.wait()

    sem = pltpu.SemaphoreType.DMA((n,))
    return _hbm_call(body, name, rs, [jax.ShapeDtypeStruct(r.shape, r.dtype) for r in rs], [sem, sem])


def _small_all_reduce(vec, name):
    r = vec.shape[0]
    ndev = 8

    def body(v_ref, o_ref, buf, send, recv):
        x, y, c = _mesh_pos()
        me = 4 * x + 2 * y + c
        buf[me] = v_ref[...]
        cps = []
        for rel in range(1, ndev):
            px = 1 - x if rel & 4 else x
            py = 1 - y if rel & 2 else y
            pc = 1 - c if rel & 1 else c
            cps.append((_remote(v_ref, buf.at[me], send.at[rel - 1], recv.at[rel - 1], (px, py, pc)),
                        4 * px + 2 * py + pc))
        for cp, _ in cps:
            cp.start()
        for k, (cp, peer) in enumerate(cps):
            slot = buf.at[peer]
            _remote(slot, slot, send.at[k], recv.at[k], (x, y, c)).wait_recv()
        for cp, _ in cps:
            cp.wait_send()
        acc = buf[0]
        for k in range(1, ndev):
            acc = acc + buf[k]
        o_ref[...] = acc

    vm = pl.BlockSpec(memory_space=pltpu.VMEM)
    return pl.pallas_call(
        body, name=name, out_shape=jax.ShapeDtypeStruct(vec.shape, F32), in_specs=[vm], out_specs=vm,
        scratch_shapes=[pltpu.VMEM((ndev, r, LANES), F32), pltpu.SemaphoreType.DMA((ndev - 1,)),
                        pltpu.SemaphoreType.DMA((ndev - 1,))],
        compiler_params=pltpu.CompilerParams(has_side_effects=True),
    )(vec)


def _row_tile(rows, cap=512):
    for t in range(min(rows, cap), 0, -1):
        if rows % t == 0 and (t % 16 == 0 or t == rows):
            return t
    raise ValueError(rows)


def _add_pairs(a, b, name):
    k, r, c = a.shape
    tr = _row_tile(r)

    def body(a_ref, b_ref, o_ref):
        o_ref[...] = (a_ref[...].astype(F32) + b_ref[...].astype(F32)).astype(o_ref.dtype)

    spec = pl.BlockSpec((None, tr, c), lambda i, j: (i, j, 0))
    return pl.pallas_call(body, name=name, grid=(k, r // tr), in_specs=[spec, spec], out_specs=spec,
                          out_shape=jax.ShapeDtypeStruct(a.shape, a.dtype),
                          compiler_params=_params(("parallel", "parallel")))(a, b)


def _final_sum(own, sib, others, name):
    r, c = own.shape
    tr = _row_tile(r)

    def body(a_ref, b_ref, o_ref_in, out_ref):
        acc = a_ref[...].astype(F32) + b_ref[...].astype(F32)
        for k in range(3):
            acc = acc + o_ref_in[k].astype(F32)
        out_ref[...] = acc

    spec = pl.BlockSpec((tr, c), lambda i: (i, 0))
    return pl.pallas_call(body, name=name, grid=(r // tr,),
                          in_specs=[spec, spec, pl.BlockSpec((3, tr, c), lambda i: (0, i, 0))], out_specs=spec,
                          out_shape=jax.ShapeDtypeStruct((r, c), F32),
                          compiler_params=_params(("parallel",)))(own, sib, others)


def _adamw(g, w, m, v, name):
    r, c = g.shape
    tr = _row_tile(r, 256)

    def body(g_ref, w_ref, m_ref, v_ref, d_ref, mo_ref, vo_ref):
        gv = g_ref[...]
        mn = ADAM_B1 * m_ref[...] + (1.0 - ADAM_B1) * gv
        vn = ADAM_B2 * v_ref[...] + (1.0 - ADAM_B2) * (gv * gv)
        m_hat = mn / (1.0 - ADAM_B1 ** ADAM_STEP)
        v_hat = vn / (1.0 - ADAM_B2 ** ADAM_STEP)
        d_ref[...] = -ADAM_LR * (m_hat / (jnp.sqrt(v_hat) + ADAM_EPS) + ADAM_WD * w_ref[...])
        mo_ref[...] = mn
        vo_ref[...] = vn

    spec = pl.BlockSpec((tr, c), lambda i: (i, 0))
    shp = jax.ShapeDtypeStruct((r, c), F32)
    return pl.pallas_call(body, name=name, grid=(r // tr,), in_specs=[spec] * 4, out_specs=[spec] * 3,
                          out_shape=[shp] * 3, compiler_params=_params(("parallel",)))(g, w, m, v)


def _pack(arrays):
    flat = jnp.concatenate([a.reshape(-1).astype(F32) for a in arrays])
    pad = (-flat.shape[0]) % (8 * LANES)
    return jnp.concatenate([flat, jnp.zeros((pad,), F32)]).reshape(-1, LANES)


def _unpack(packed, shapes):
    flat = packed.reshape(-1)
    out, off = [], 0
    for s in shapes:
        size = 1
        for dim in s:
            size *= dim
        out.append(flat[off:off + size].reshape(s))
        off += size
    return out


BIG = ("ffn1_w_in", "ffn1_w_out", "w_in", "w_out", "ffn2_w_in", "ffn2_w_out")
SMALL = ("ffn1_norm", "mix_norm", "fox_q_norm", "fox_k_norm", "fox_f_bias", "gdn_a_log", "gdn_dt_bias",
         "gdn_out_norm", "ffn2_norm", "gdn_conv")
WEIGHTS = ("ffn1_norm", "ffn1_w_in", "ffn1_w_out", "mix_norm", "w_in", "fox_q_norm", "fox_k_norm", "fox_f_bias",
           "gdn_conv", "gdn_a_log", "gdn_dt_bias", "gdn_out_norm", "w_out", "ffn2_norm", "ffn2_w_in", "ffn2_w_out")


def _step(x, target, w, m, v):
    xi, yi, ci = _mesh_pos()
    me = 2 * xi + yi
    depth = DEPTH
    d = x.shape[-1]

    shards = [w[k].astype(BF) for k in BIG] + [w["gdn_conv"]]
    gathered = [lax.dynamic_update_index_in_dim(g, s, me, 0)
                for g, s in zip(_all_gather(shards, "all_gather_weights"), shards)]
    gw = dict(zip(BIG, gathered))
    p = {k: w[k] for k in SMALL if k != "gdn_conv"}
    conv = gathered[len(BIG)]
    p["gdn_conv"] = conv.transpose(1, 2, 0, 3).reshape(depth, CONV_WIDTH, -1)
    for k in ("ffn1_w_in", "ffn1_w_out", "ffn2_w_in", "ffn2_w_out"):
        p[k] = gw[k]
    p["w_mix"] = _mix_to_padded(gw["w_in"].transpose(1, 2, 0, 3).reshape(depth, d, N_IN))
    p["w_out"] = gw["w_out"].transpose(1, 0, 2, 3).reshape(depth, 2 * FOX_WIDTH, d)

    loss, dx, grads = _local_step(x, target, p)

    def transport(k):
        per_layer = []
        for l in range(depth):
            g = grads[l]
            if k == "w_in":
                full = _mix_from_padded(g["w_mix"])
                per_layer.append(full.reshape(d, N_CHIPS, N_IN // N_CHIPS).transpose(1, 0, 2).astype(BF))
            elif k == "w_out":
                per_layer.append(g["w_out"].reshape(N_CHIPS, -1, d))
            else:
                per_layer.append(g[k])
        return jnp.stack(per_layer)

    gs = [transport(k) for k in BIG]
    from_sib = _sibling_send_layers(gs, "grad_to_sibling")
    mine = [lax.dynamic_index_in_dim(g, ci, 0, keepdims=False) for g in gs]
    chip_sums = [_add_pairs(a, b, f"grad_chip_sum_{k}") for a, b, k in zip(mine, from_sib, BIG)]
    from_chips = _chip_scatter(chip_sums, "grad_to_chips")
    reduced = [_final_sum(lax.dynamic_index_in_dim(a, me, 0, keepdims=False),
                          lax.dynamic_index_in_dim(b, me, 0, keepdims=False), o, f"grad_final_sum_{k}")
               for a, b, o, k in zip(mine, from_sib, from_chips, BIG)]
    from_sib_final = _sibling_swap(reduced, "grad_swap_layers")
    full = {k: jnp.stack([jnp.where(ci == 0, a, b), jnp.where(ci == 0, b, a)])
            for k, a, b in zip(BIG, reduced, from_sib_final)}

    out_g, out_d, out_m, out_v = {}, {}, {}, {}
    for k in BIG:
        shp = w[k].shape
        two_d = lambda a: a.reshape(shp[0] * shp[1], shp[2])
        dl, mn, vn = _adamw(two_d(full[k]), two_d(w[k]), two_d(m[k]), two_d(v[k]), f"adamw_{k}")
        out_g[k], out_d[k], out_m[k], out_v[k] = full[k], dl.reshape(shp), mn.reshape(shp), vn.reshape(shp)

    small_local = [jnp.stack([grads[l][k] for l in range(depth)]) for k in SMALL]
    summed = _unpack(_small_all_reduce(_pack(small_local), "small_all_reduce"), [a.shape for a in small_local])
    sg = dict(zip(SMALL, summed))
    cs = w["gdn_conv"].shape[-1]
    sg["gdn_conv"] = lax.dynamic_slice_in_dim(sg["gdn_conv"], me * cs, cs, axis=2)
    shapes = [w[k].shape for k in SMALL]
    packs = [_pack([src[k] for k in SMALL]) for src in (sg, w, m, v)]
    dl, mn, vn = _adamw(*packs, "adamw_small")
    for k, a, b, c2 in zip(SMALL, _unpack(dl, shapes), _unpack(mn, shapes), _unpack(vn, shapes)):
        out_g[k], out_d[k], out_m[k], out_v[k] = sg[k], a, b, c2

    total = lax.psum(loss[0, 0], ("x", "y", "c"))
    return (total, dx, *[out_g[k] for k in WEIGHTS], *[out_d[k] for k in WEIGHTS],
            *[out_m[k] for k in WEIGHTS], *[out_v[k] for k in WEIGHTS])


def kernel(x, ffn1_norm, ffn1_w_in, ffn1_w_out, mix_norm, w_in, fox_q_norm, fox_k_norm, fox_f_bias, gdn_conv, gdn_a_log, gdn_dt_bias, gdn_out_norm, w_out, ffn2_norm, ffn2_w_in, ffn2_w_out, loss_target, m_ffn1_norm, m_ffn1_w_in, m_ffn1_w_out, m_mix_norm, m_w_in, m_fox_q_norm, m_fox_k_norm, m_fox_f_bias, m_gdn_conv, m_gdn_a_log, m_gdn_dt_bias, m_gdn_out_norm, m_w_out, m_ffn2_norm, m_ffn2_w_in, m_ffn2_w_out, v_ffn1_norm, v_ffn1_w_in, v_ffn1_w_out, v_mix_norm, v_w_in, v_fox_q_norm, v_fox_k_norm, v_fox_f_bias, v_gdn_conv, v_gdn_a_log, v_gdn_dt_bias, v_gdn_out_norm, v_w_out, v_ffn2_norm, v_ffn2_w_in, v_ffn2_w_out):
    w = dict(ffn1_norm=ffn1_norm, ffn1_w_in=ffn1_w_in, ffn1_w_out=ffn1_w_out, mix_norm=mix_norm, w_in=w_in,
             fox_q_norm=fox_q_norm, fox_k_norm=fox_k_norm, fox_f_bias=fox_f_bias, gdn_conv=gdn_conv,
             gdn_a_log=gdn_a_log, gdn_dt_bias=gdn_dt_bias, gdn_out_norm=gdn_out_norm, w_out=w_out,
             ffn2_norm=ffn2_norm, ffn2_w_in=ffn2_w_in, ffn2_w_out=ffn2_w_out)
    m = dict(ffn1_norm=m_ffn1_norm, ffn1_w_in=m_ffn1_w_in, ffn1_w_out=m_ffn1_w_out, mix_norm=m_mix_norm, w_in=m_w_in,
             fox_q_norm=m_fox_q_norm, fox_k_norm=m_fox_k_norm, fox_f_bias=m_fox_f_bias, gdn_conv=m_gdn_conv,
             gdn_a_log=m_gdn_a_log, gdn_dt_bias=m_gdn_dt_bias, gdn_out_norm=m_gdn_out_norm, w_out=m_w_out,
             ffn2_norm=m_ffn2_norm, ffn2_w_in=m_ffn2_w_in, ffn2_w_out=m_ffn2_w_out)
    v = dict(ffn1_norm=v_ffn1_norm, ffn1_w_in=v_ffn1_w_in, ffn1_w_out=v_ffn1_w_out, mix_norm=v_mix_norm, w_in=v_w_in,
             fox_q_norm=v_fox_q_norm, fox_k_norm=v_fox_k_norm, fox_f_bias=v_fox_f_bias, gdn_conv=v_gdn_conv,
             gdn_a_log=v_gdn_a_log, gdn_dt_bias=v_gdn_dt_bias, gdn_out_norm=v_gdn_out_norm, w_out=v_w_out,
             ffn2_norm=v_ffn2_norm, ffn2_w_in=v_ffn2_w_in, ffn2_w_out=v_ffn2_w_out)
    return _step(x, loss_target, w, m, v)
```

```python
import jax
import jax.numpy as jnp
from jax import lax
from jax.experimental import pallas as pl
from jax.experimental.pallas import tpu as pltpu

F32 = jnp.float32
BF = jnp.bfloat16
HI = lax.Precision.HIGHEST
MESH = pl.DeviceIdType.MESH

DEPTH = 2
FOX_HEADS = 8
FOX_HEAD_DIM = 64
FOX_WIDTH = 512
GDN_HEADS = 4
GDN_HEAD_DIM = 128
GDN_WIDTH = 512
CONV_WIDTH = 4
CHUNK = 64
EPS = 1e-6
N_IN = 3600
N_PAD = 3712
GATE_COL = 3584
LANES = 128
NEG = -1e30

ADAM_LR = 0.001
ADAM_B1 = 0.9
ADAM_B2 = 0.999
ADAM_EPS = 1e-08
ADAM_WD = 0.01
ADAM_STEP = 10

VMEM_LIMIT = 56 * 1024 * 1024


def _params(sem=None, **kw):
    return pltpu.CompilerParams(dimension_semantics=sem, vmem_limit_bytes=VMEM_LIMIT, **kw)


def _dot(a, b, precision=None):
    return jnp.dot(a, b, preferred_element_type=F32, precision=precision)


def _dot_nt(a, b, precision=None):
    return lax.dot_general(a, b, (((1,), (1,)), ((), ())), preferred_element_type=F32, precision=precision)


def _dot_tn(a, b, precision=None):
    return lax.dot_general(a, b, (((0,), (0,)), ((), ())), preferred_element_type=F32, precision=precision)


def _sigmoid(x):
    return 0.5 * jnp.tanh(0.5 * x) + 0.5


def _softplus(x):
    return jnp.maximum(x, 0.0) + jnp.log(1.0 + jnp.exp(-jnp.abs(x)))


def _log_sigmoid(x):
    return jnp.minimum(x, 0.0) - jnp.log(1.0 + jnp.exp(-jnp.abs(x)))


def _tile(n, t):
    t = min(n, t)
    assert n % t == 0, (n, t)
    return t


def _rms_fwd(x, gain):
    rstd = lax.rsqrt(jnp.mean(x * x, axis=-1, keepdims=True) + EPS)
    xhat = x * rstd
    return xhat * gain, xhat, rstd


def _rms_bwd(dy, xhat, rstd, gain):
    dxhat = dy * gain
    dx = rstd * (dxhat - xhat * jnp.mean(dxhat * xhat, axis=-1, keepdims=True))
    return dx, dy * xhat


def _full(shape):
    nd = len(shape)
    return pl.BlockSpec(shape, lambda *_: (0,) * nd)


HBM = pl.BlockSpec(memory_space=pltpu.HBM)


def _load_ffn_weights(win_hbm, wout_hbm, layer, win_v, wout_v, sem):
    fr = wout_hbm.shape[2]
    copies = [pltpu.make_async_copy(win_hbm.at[s, layer], win_v.at[s], sem.at[s]) for s in range(4)]
    copies += [pltpu.make_async_copy(wout_hbm.at[s, layer], wout_v.at[pl.ds(s * fr, fr)], sem.at[4 + s])
               for s in range(4)]
    for c in copies:
        c.start()
    for c in copies:
        c.wait()


def _ffn_fwd(x, gain, win_g, wout_g, layer, name):
    t, d = x.shape
    _, _, _, fs = win_g.shape
    fr = wout_g.shape[2]
    tm = _tile(t, 256)

    def body(x_ref, g_ref, win_hbm, wout_hbm, xo_ref, h_ref, win_v, wout_v, sem):
        @pl.when(pl.program_id(0) == 0)
        def _():
            _load_ffn_weights(win_hbm, wout_hbm, layer, win_v, wout_v, sem)

        xv = x_ref[...]
        hn, _, _ = _rms_fwd(xv, g_ref[...])
        hn = hn.astype(BF)
        acc = jnp.zeros((tm, d), F32)
        for s in range(2):
            g = _dot(hn, win_v[s])
            u = _dot(hn, win_v[s + 2])
            h_ref[:, s * fs:(s + 1) * fs] = g.astype(BF)
            h_ref[:, (s + 2) * fs:(s + 3) * fs] = u.astype(BF)
            a = (g * _sigmoid(g) * u).astype(BF)
            acc = acc + _dot(a, wout_v[s * fs:(s + 1) * fs, :])
        xo_ref[...] = xv + 0.5 * acc

    return pl.pallas_call(
        body, name=name, grid=(t // tm,),
        in_specs=[pl.BlockSpec((tm, d), lambda i: (i, 0)), _full((1, d)), HBM, HBM],
        out_specs=[pl.BlockSpec((tm, d), lambda i: (i, 0)), pl.BlockSpec((tm, 4 * fs), lambda i: (i, 0))],
        out_shape=[jax.ShapeDtypeStruct((t, d), F32), jax.ShapeDtypeStruct((t, 4 * fs), BF)],
        scratch_shapes=[pltpu.VMEM((4, d, fs), BF), pltpu.VMEM((4 * fr, d), BF), pltpu.SemaphoreType.DMA((8,))],
        compiler_params=_params(("arbitrary",)),
    )(x, gain, win_g, wout_g)


def _ffn_bwd(dy, x, h, gain, win_g, wout_g, layer, name):
    t, d = x.shape
    _, _, _, fs = win_g.shape
    fr = wout_g.shape[2]
    tm = _tile(t, 256)

    def body(dy_ref, x_ref, h_ref, g_ref, win_hbm, wout_hbm,
             dx_ref, dh_ref, a_ref, hn_ref, dyh_ref, dg_ref, win_v, wout_v, sem):
        @pl.when(pl.program_id(0) == 0)
        def _():
            _load_ffn_weights(win_hbm, wout_hbm, layer, win_v, wout_v, sem)
            dg_ref[...] = jnp.zeros_like(dg_ref)

        dyv = dy_ref[...]
        dyh = (0.5 * dyv).astype(BF)
        dyh_ref[...] = dyh
        dhn = jnp.zeros((tm, d), F32)
        for s in range(2):
            da = _dot_nt(dyh, wout_v[s * fs:(s + 1) * fs, :])
            g = h_ref[:, s * fs:(s + 1) * fs].astype(F32)
            u = h_ref[:, (s + 2) * fs:(s + 3) * fs].astype(F32)
            sg = _sigmoid(g)
            si = g * sg
            a_ref[:, s * fs:(s + 1) * fs] = (si * u).astype(BF)
            dgate = (da * u * (sg * (1.0 + g * (1.0 - sg)))).astype(BF)
            dup = (da * si).astype(BF)
            dh_ref[:, s * fs:(s + 1) * fs] = dgate
            dh_ref[:, (s + 2) * fs:(s + 3) * fs] = dup
            dhn = dhn + _dot_nt(dgate, win_v[s]) + _dot_nt(dup, win_v[s + 2])
        xv = x_ref[...]
        gain_v = g_ref[...]
        hn, xhat, rstd = _rms_fwd(xv, gain_v)
        hn_ref[...] = hn.astype(BF)
        dx, dgr = _rms_bwd(dhn, xhat, rstd, gain_v)
        dx_ref[...] = dyv + dx
        dg_ref[...] += jnp.sum(dgr, axis=0, keepdims=True)

    row = lambda w: pl.BlockSpec((tm, w), lambda i: (i, 0))
    return pl.pallas_call(
        body, name=name, grid=(t // tm,),
        in_specs=[row(d), row(d), row(4 * fs), _full((1, d)), HBM, HBM],
        out_specs=[row(d), row(4 * fs), row(2 * fs), row(d), row(d), _full((1, d))],
        out_shape=[jax.ShapeDtypeStruct((t, d), F32), jax.ShapeDtypeStruct((t, 4 * fs), BF),
                   jax.ShapeDtypeStruct((t, 2 * fs), BF), jax.ShapeDtypeStruct((t, d), BF),
                   jax.ShapeDtypeStruct((t, d), BF), jax.ShapeDtypeStruct((1, d), F32)],
        scratch_shapes=[pltpu.VMEM((4, d, fs), BF), pltpu.VMEM((4 * fr, d), BF), pltpu.SemaphoreType.DMA((8,))],
        compiler_params=_params(("arbitrary",)),
    )(dy, x, h, gain, win_g, wout_g)


def _wgrad(a, b, out_shape, out_spec, tm, tn, name, tk=512):
    t, m = a.shape
    _, n = b.shape
    tk = _tile(t, tk)
    nk = t // tk

    def body(a_ref, b_ref, o_ref, acc):
        k = pl.program_id(2)

        @pl.when(k == 0)
        def _():
            acc[...] = jnp.zeros_like(acc)

        acc[...] += _dot_tn(a_ref[...], b_ref[...])

        @pl.when(k == nk - 1)
        def _():
            o_ref[...] = acc[...].astype(o_ref.dtype)

    return pl.pallas_call(
        body, name=name, grid=(m // tm, n // tn, nk),
        in_specs=[pl.BlockSpec((tk, tm), lambda i, j, k: (k, i)), pl.BlockSpec((tk, tn), lambda i, j, k: (k, j))],
        out_specs=out_spec, out_shape=out_shape,
        scratch_shapes=[pltpu.VMEM((tm, tn), F32)],
        compiler_params=_params(("parallel", "parallel", "arbitrary")),
    )(a, b)


def _norm_matmul(x, gain, w, name):
    t, d = x.shape
    n = w.shape[1]
    tm = _tile(t, 256)

    def body(x_ref, g_ref, w_ref, o_ref):
        hn, _, _ = _rms_fwd(x_ref[...], g_ref[...])
        o_ref[...] = _dot(hn.astype(BF), w_ref[...])

    return pl.pallas_call(
        body, name=name, grid=(t // tm,),
        in_specs=[pl.BlockSpec((tm, d), lambda i: (i, 0)), _full((1, d)), _full((d, n))],
        out_specs=pl.BlockSpec((tm, n), lambda i: (i, 0)),
        out_shape=jax.ShapeDtypeStruct((t, n), F32),
        compiler_params=_params(("parallel",)),
    )(x, gain, w)


def _norm_matmul_bwd(dres, dproj, x, gain, w, name):
    t, d = x.shape
    n = w.shape[1]
    tm = _tile(t, 256)

    def body(dr_ref, dp_ref, x_ref, g_ref, w_ref, dx_ref, hn_ref, dg_ref):
        @pl.when(pl.program_id(0) == 0)
        def _():
            dg_ref[...] = jnp.zeros_like(dg_ref)

        dhn = _dot_nt(dp_ref[...], w_ref[...])
        gain_v = g_ref[...]
        hn, xhat, rstd = _rms_fwd(x_ref[...], gain_v)
        hn_ref[...] = hn.astype(BF)
        dx, dgr = _rms_bwd(dhn, xhat, rstd, gain_v)
        dx_ref[...] = dr_ref[...] + dx
        dg_ref[...] += jnp.sum(dgr, axis=0, keepdims=True)

    row = lambda wd: pl.BlockSpec((tm, wd), lambda i: (i, 0))
    return pl.pallas_call(
        body, name=name, grid=(t // tm,),
        in_specs=[row(d), row(n), row(d), _full((1, d)), _full((d, n))],
        out_specs=[row(d), row(d), _full((1, d))],
        out_shape=[jax.ShapeDtypeStruct((t, d), F32), jax.ShapeDtypeStruct((t, d), BF),
                   jax.ShapeDtypeStruct((1, d), F32)],
        compiler_params=_params(("arbitrary",)),
    )(dres, dproj, x, gain, w)


def _mix_out(x, yf, yg, w, name):
    t, d = x.shape
    kf = yf.shape[1]
    tm = _tile(t, 512)

    def body(x_ref, yf_ref, yg_ref, w_ref, o_ref):
        o_ref[...] = x_ref[...] + _dot(yf_ref[...], w_ref[0:kf, :]) + _dot(yg_ref[...], w_ref[kf:2 * kf, :])

    row = lambda wd: pl.BlockSpec((tm, wd), lambda i: (i, 0))
    return pl.pallas_call(
        body, name=name, grid=(t // tm,),
        in_specs=[row(d), row(kf), row(kf), _full((2 * kf, d))],
        out_specs=row(d), out_shape=jax.ShapeDtypeStruct((t, d), F32),
        compiler_params=_params(("parallel",)),
    )(x, yf, yg, w)


def _mix_out_bwd(dx, w, name):
    t, d = dx.shape
    kf = w.shape[0] // 2
    tm = _tile(t, 512)

    def body(dx_ref, w_ref, df_ref, dg_ref, dxb_ref):
        dxb = dx_ref[...].astype(BF)
        dxb_ref[...] = dxb
        df_ref[...] = _dot_nt(dxb, w_ref[0:kf, :]).astype(BF)
        dg_ref[...] = _dot_nt(dxb, w_ref[kf:2 * kf, :]).astype(BF)

    row = lambda wd: pl.BlockSpec((tm, wd), lambda i: (i, 0))
    return pl.pallas_call(
        body, name=name, grid=(t // tm,),
        in_specs=[row(d), _full((2 * kf, d))],
        out_specs=[row(kf), row(kf), row(d)],
        out_shape=[jax.ShapeDtypeStruct((t, kf), BF), jax.ShapeDtypeStruct((t, kf), BF),
                   jax.ShapeDtypeStruct((t, d), BF)],
        compiler_params=_params(("parallel",)),
    )(dx, w)


def _loss_grad(y, target, name):
    t, d = y.shape
    tm = _tile(t, 512)

    def body(y_ref, t_ref, l_ref, dy_ref):
        @pl.when(pl.program_id(0) == 0)
        def _():
            l_ref[...] = jnp.zeros_like(l_ref)

        diff = y_ref[...] - t_ref[...]
        dy_ref[...] = diff * (1.0 / d)
        part = jnp.sum(jnp.sum(diff * diff, axis=1, keepdims=True), axis=0, keepdims=True)
        l_ref[...] += part * (0.5 / d)

    row = pl.BlockSpec((tm, d), lambda i: (i, 0))
    return pl.pallas_call(
        body, name=name, grid=(t // tm,),
        in_specs=[row, row], out_specs=[_full((1, 1)), row],
        out_shape=[jax.ShapeDtypeStruct((1, 1), F32), jax.ShapeDtypeStruct((t, d), F32)],
        compiler_params=_params(("arbitrary",)),
    )(y, target)


def _head_sum_matrix(width, head):
    r = lax.broadcasted_iota(jnp.int32, (width, width), 0) // head
    c = lax.broadcasted_iota(jnp.int32, (width, width), 1) // head
    return (r == c).astype(BF)


def _head_mean(x, bd):
    return _dot(x.astype(BF), bd) * (1.0 / FOX_HEAD_DIM)


def _mask_dot(mask01, x):
    mb = mask01.astype(BF)
    hi = x.astype(BF)
    r1 = x - hi.astype(F32)
    mid = r1.astype(BF)
    lo = (r1 - mid.astype(F32)).astype(BF)
    return _dot(mb, hi) + _dot(mb, mid) + _dot(mb, lo)


def _fox_prep(proj, wq_t, wk_t, bias_pad, seq, name):
    t = proj.shape[0]
    ts = _tile(seq, 512)
    tpe = seq // ts
    scale = FOX_HEAD_DIM ** -0.5

    def body(q_ref, k_ref, v_ref, gt_ref, wq_ref, wk_ref, b_ref, qo_ref, ko_ref, vo_ref, cum_ref, carry):
        i = pl.program_id(0)
        bd = _head_sum_matrix(FOX_WIDTH, FOX_HEAD_DIM)

        def norm(xv, wv):
            ms = _head_mean(xv * xv, bd)
            return xv * lax.rsqrt(ms + EPS) * wv

        qo_ref[...] = (norm(q_ref[...], wq_ref[...]) * scale).astype(BF)
        ko_ref[...] = norm(k_ref[...], wk_ref[...]).astype(BF)
        vo_ref[...] = v_ref[...].astype(BF)

        @pl.when(i % tpe == 0)
        def _():
            carry[...] = jnp.zeros_like(carry)

        ls = _log_sigmoid(gt_ref[...] + b_ref[...])
        r = lax.broadcasted_iota(jnp.int32, (ts, ts), 0)
        c = lax.broadcasted_iota(jnp.int32, (ts, ts), 1)
        cum = _mask_dot(r >= c, ls) + carry[...]
        cum_ref[...] = cum
        carry[...] = cum[ts - 1:ts, :]

    blk = lambda j: pl.BlockSpec((ts, FOX_WIDTH), lambda i: (i, j))
    gate = pl.BlockSpec((ts, LANES), lambda i: (i, GATE_COL // LANES))
    out = pl.BlockSpec((ts, FOX_WIDTH), lambda i: (i, 0))
    return pl.pallas_call(
        body, name=name, grid=(t // ts,),
        in_specs=[blk(0), blk(1), blk(2), gate, _full((1, FOX_WIDTH)), _full((1, FOX_WIDTH)), _full((1, LANES))],
        out_specs=[out, out, out, pl.BlockSpec((ts, LANES), lambda i: (i, 0))],
        out_shape=[jax.ShapeDtypeStruct((t, FOX_WIDTH), BF)] * 3 + [jax.ShapeDtypeStruct((t, LANES), F32)],
        scratch_shapes=[pltpu.VMEM((1, LANES), F32)],
        compiler_params=_params(("arbitrary",)),
    )(proj, proj, proj, proj, wq_t, wk_t, bias_pad)


def _fox_prep_bwd(proj, dqa, dqb, dk, dv, dcum, wq_t, wk_t, bias_pad, seq, name):
    t = proj.shape[0]
    ts = _tile(seq, 512)
    tpe = seq // ts
    nt = t // ts
    scale = FOX_HEAD_DIM ** -0.5

    def body(q_ref, k_ref, gt_ref, dqa_ref, dqb_ref, dk_ref, dv_ref, dc_ref, wq_ref, wk_ref, b_ref,
             dp_ref, dff_ref, dwq_ref, dwk_ref, db_ref, carry):
        i = pl.program_id(0)
        first = (lax.broadcasted_iota(jnp.int32, (ts, FOX_WIDTH), 1) % LANES) < FOX_HEAD_DIM
        dq_all = jnp.where(first, dqa_ref[...], dqb_ref[...])
        ti = nt - 1 - i
        bd = _head_sum_matrix(FOX_WIDTH, FOX_HEAD_DIM)

        @pl.when(i == 0)
        def _():
            dwq_ref[...] = jnp.zeros_like(dwq_ref)
            dwk_ref[...] = jnp.zeros_like(dwk_ref)
            db_ref[...] = jnp.zeros_like(db_ref)

        def norm_bwd(xv, wv, dyv):
            ms = _head_mean(xv * xv, bd)
            rstd = lax.rsqrt(ms + EPS)
            xhat = xv * rstd
            dxhat = dyv * wv
            mean = _head_mean(dxhat * xhat, bd)
            return rstd * (dxhat - xhat * mean), jnp.sum(dyv * xhat, axis=0, keepdims=True)

        dxq, dwq = norm_bwd(q_ref[...], wq_ref[...], dq_all * scale)
        dxk, dwk = norm_bwd(k_ref[...], wk_ref[...], dk_ref[...])
        dp_ref[:, 0:FOX_WIDTH] = dxq.astype(BF)
        dp_ref[:, FOX_WIDTH:2 * FOX_WIDTH] = dxk.astype(BF)
        dp_ref[:, 2 * FOX_WIDTH:3 * FOX_WIDTH] = dv_ref[...].astype(BF)
        dwq_ref[...] += dwq
        dwk_ref[...] += dwk

        @pl.when(ti % tpe == tpe - 1)
        def _():
            carry[...] = jnp.zeros_like(carry)

        r = lax.broadcasted_iota(jnp.int32, (ts, ts), 0)
        c = lax.broadcasted_iota(jnp.int32, (ts, ts), 1)
        dls = _mask_dot(c >= r, dc_ref[...]) + carry[...]
        carry[...] = dls[0:1, :]
        z = gt_ref[...] + b_ref[...]
        lane = lax.broadcasted_iota(jnp.int32, (ts, LANES), 1)
        dff = jnp.where(lane < FOX_HEADS, dls * _sigmoid(-z), 0.0)
        dff_ref[...] = dff
        db_ref[...] += jnp.sum(dff, axis=0, keepdims=True)

        @pl.when(i == nt - 1)
        def _():
            fr = lax.broadcasted_iota(jnp.int32, (FOX_WIDTH, FOX_WIDTH), 0) % FOX_HEAD_DIM
            fc = lax.broadcasted_iota(jnp.int32, (FOX_WIDTH, FOX_WIDTH), 1) % FOX_HEAD_DIM
            fold = (fr == fc).astype(F32)
            dwq_ref[...] = _dot(dwq_ref[...], fold, HI)
            dwk_ref[...] = _dot(dwk_ref[...], fold, HI)

    rev = lambda w, j: pl.BlockSpec((ts, w), lambda i: (nt - 1 - i, j))
    return pl.pallas_call(
        body, name=name, grid=(nt,),
        in_specs=[rev(FOX_WIDTH, 0), rev(FOX_WIDTH, 1), rev(LANES, GATE_COL // LANES),
                  rev(FOX_WIDTH, 0), rev(FOX_WIDTH, 0), rev(FOX_WIDTH, 0), rev(FOX_WIDTH, 0), rev(LANES, 0),
                  _full((1, FOX_WIDTH)), _full((1, FOX_WIDTH)), _full((1, LANES))],
        out_specs=[rev(3 * FOX_WIDTH, 0), rev(LANES, 0), _full((1, FOX_WIDTH)), _full((1, FOX_WIDTH)),
                   _full((1, LANES))],
        out_shape=[jax.ShapeDtypeStruct((t, 3 * FOX_WIDTH), BF), jax.ShapeDtypeStruct((t, LANES), F32),
                   jax.ShapeDtypeStruct((1, FOX_WIDTH), F32), jax.ShapeDtypeStruct((1, FOX_WIDTH), F32),
                   jax.ShapeDtypeStruct((1, LANES), F32)],
        scratch_shapes=[pltpu.VMEM((1, LANES), F32)],
        compiler_params=_params(("arbitrary",)),
    )(proj, proj, proj, dqa, dqb, dk, dv, dcum, wq_t, wk_t, bias_pad)


def _fox_attn(q, k, v, ck, nb, seq, name):
    t = q.shape[0]
    tq = _tile(seq, 512)
    nq = seq // tq
    npair = FOX_HEADS // 2
    hd = FOX_HEAD_DIM

    def body(q_ref, k_ref, v_ref, ck_ref, o_ref, lse_ref, m_s, acc_s):
        qi = pl.program_id(2)
        lane = lax.broadcasted_iota(jnp.int32, (tq, LANES), 1)
        m_s[...] = jnp.full(m_s.shape, NEG, F32)
        acc_s[...] = jnp.zeros_like(acc_s)
        qv = q_ref[...]

        def tile(kj, on_diagonal):
            cols = pl.ds(pl.multiple_of(kj * tq, tq), tq)
            kv = k_ref[cols, :]
            vv = v_ref[cols, :]
            if on_diagonal:
                causal = (lax.broadcasted_iota(jnp.int32, (tq, tq), 0)
                          >= lax.broadcasted_iota(jnp.int32, (tq, tq), 1))
            for hh in range(2):
                hm = (lane >= hd) if hh else (lane < hd)
                qh = jnp.where(hm, qv, jnp.zeros_like(qv))
                s = _dot_nt(qh, kv) - ck_ref[hh, :, cols]
                if on_diagonal:
                    s = jnp.where(causal, s, NEG)
                m_old = m_s[hh]
                m_new = jnp.maximum(m_old, jnp.max(s, axis=-1, keepdims=True))
                p = jnp.exp(s - m_new)
                alpha = jnp.exp(m_old - m_new)
                m_s[hh] = m_new
                acc_s[hh] = alpha * acc_s[hh] + _dot(p.astype(BF), jnp.where(hm, vv, jnp.ones_like(vv)))

        def off_diagonal(kj, carry):
            tile(kj, False)
            return carry

        lax.fori_loop(0, qi, off_diagonal, 0)
        tile(qi, True)
        a0 = acc_s[0]
        a1 = acc_s[1]
        den = jnp.where(lane < hd, pltpu.roll(a0, hd, axis=1), pltpu.roll(a1, hd, axis=1))
        o_ref[...] = (jnp.where(lane < hd, a0, a1) / den).astype(o_ref.dtype)
        l0 = jnp.sum(jnp.where(lane == hd, a0, 0.0), axis=1, keepdims=True)
        l1 = jnp.sum(jnp.where(lane == 0, a1, 0.0), axis=1, keepdims=True)
        lse_ref[0] = m_s[0] + jnp.log(l0)
        lse_ref[1] = m_s[1] + jnp.log(l1)

    qspec = pl.BlockSpec((tq, LANES), lambda b, p, i: (b * nq + i, p))
    kspec = pl.BlockSpec((seq, LANES), lambda b, p, i: (b, p))
    colspec = pl.BlockSpec((None, 2, tq, 1), lambda b, p, i: (b * npair + p, 0, i, 0))
    rowspec = pl.BlockSpec((None, 2, 1, seq), lambda b, p, i: (b * npair + p, 0, 0, 0))
    return pl.pallas_call(
        body, name=name, grid=(nb, npair, nq),
        in_specs=[qspec, kspec, kspec, rowspec],
        out_specs=[qspec, colspec],
        out_shape=[jax.ShapeDtypeStruct((t, FOX_WIDTH), BF), jax.ShapeDtypeStruct((nb * npair, 2, seq, 1), F32)],
        scratch_shapes=[pltpu.VMEM((2, tq, 1), F32), pltpu.VMEM((2, tq, LANES), F32)],
        compiler_params=_params(("parallel", "parallel", "parallel")),
    )(q, k, v, ck)


def _fox_attn_bwd(q, k, v, o, do, lse, ck, nb, seq, name):
    t = q.shape[0]
    tq = _tile(seq, 512)
    nq = seq // tq
    npair = FOX_HEADS // 2
    hd = FOX_HEAD_DIM

    def body(q_ref, k_ref, v_ref, o_ref, do_ref, lse_ref, ck_ref,
             dqa_ref, dqb_ref, dk_ref, dv_ref, dkx_ref, dk_s, dv_s):
        kj = pl.program_id(2)
        lane = lax.broadcasted_iota(jnp.int32, (tq, LANES), 1)

        @pl.when(kj == 0)
        def _():
            dqa_ref[...] = jnp.zeros_like(dqa_ref)
            dqb_ref[...] = jnp.zeros_like(dqb_ref)

        dk_s[...] = jnp.zeros_like(dk_s)
        dv_s[...] = jnp.zeros_like(dv_s)
        kv = k_ref[...]
        vv = v_ref[...]

        def tile(qi, on_diagonal):
            rows = pl.ds(pl.multiple_of(qi * tq, tq), tq)
            qv = q_ref[rows, :]
            dov = do_ref[rows, :]
            prod = dov.astype(F32) * o_ref[rows, :].astype(F32)
            if on_diagonal:
                causal = (lax.broadcasted_iota(jnp.int32, (tq, tq), 0)
                          >= lax.broadcasted_iota(jnp.int32, (tq, tq), 1))
            for hh, dq_ref in ((0, dqa_ref), (1, dqb_ref)):
                hm = (lane >= hd) if hh else (lane < hd)
                zero = jnp.zeros_like(qv)
                one = jnp.ones_like(qv)
                doh = jnp.where(hm, dov, zero)
                delta = jnp.sum(jnp.where(hm, prod, 0.0), axis=-1, keepdims=True)
                s = _dot_nt(jnp.where(hm, qv, zero), kv) - ck_ref[hh]
                if on_diagonal:
                    s = jnp.where(causal, s, NEG)
                p = jnp.exp(s - lse_ref[hh, rows, :])
                dp = _dot_nt(doh, vv)
                dsb = (p * (dp - delta)).astype(BF)
                dv_s[...] += _dot_tn(p.astype(BF), doh)
                dk_s[hh] += _dot_tn(dsb, jnp.where(hm, qv, one))
                dq_ref[rows, :] += _dot(dsb, jnp.where(hm, kv, one))

        def off_diagonal(qi, carry):
            tile(qi, False)
            return carry

        tile(kj, True)
        lax.fori_loop(kj + 1, nq, off_diagonal, 0)
        dk_ref[...] = jnp.where(lane < hd, dk_s[0], dk_s[1])
        dkx_ref[...] = jnp.where(lane < hd, dk_s[1], dk_s[0])
        dv_ref[...] = dv_s[...]

    kspec = pl.BlockSpec((tq, LANES), lambda b, p, j: (b * nq + j, p))
    full_q = pl.BlockSpec((seq, LANES), lambda b, p, j: (b, p))
    colspec = pl.BlockSpec((None, 2, seq, 1), lambda b, p, j: (b * npair + p, 0, 0, 0))
    rowspec = pl.BlockSpec((None, 2, 1, tq), lambda b, p, j: (b * npair + p, 0, 0, j))
    return pl.pallas_call(
        body, name=name, grid=(nb, npair, nq),
        in_specs=[full_q, kspec, kspec, full_q, full_q, colspec, rowspec],
        out_specs=[full_q, full_q, kspec, kspec, kspec],
        out_shape=[jax.ShapeDtypeStruct((t, FOX_WIDTH), F32)] * 5,
        scratch_shapes=[pltpu.VMEM((2, tq, LANES), F32), pltpu.VMEM((tq, LANES), F32)],
        compiler_params=_params(("parallel", "parallel", "arbitrary")),
    )(q, k, v, o, do, lse, ck)


GDN_QKV = 3 * GDN_WIDTH
GDN_COL = 3 * FOX_WIDTH
GG_COL = GDN_COL + GDN_QKV
A_LANE = FOX_HEADS
B_LANE = FOX_HEADS + GDN_HEADS
HALO = 8


def _gate_lanes(ts):
    lane = lax.broadcasted_iota(jnp.int32, (ts, LANES), 1)
    return (lane >= A_LANE) & (lane < B_LANE), (lane >= B_LANE) & (lane < B_LANE + GDN_HEADS)


def _chunk_tri(ts, upper):
    r = lax.broadcasted_iota(jnp.int32, (ts, ts), 0)
    c = lax.broadcasted_iota(jnp.int32, (ts, ts), 1)
    same = (r // CHUNK) == (c // CHUNK)
    return (same & ((c >= r) if upper else (r >= c))).astype(F32)


def _conv_silu_l2(xp_ref, w, ts):
    c = w[0:1, :] * xp_ref[pl.ds(HALO - 3, ts), :]
    for kk in range(1, CONV_WIDTH):
        c = c + w[kk:kk + 1, :] * xp_ref[pl.ds(HALO - 3 + kk, ts), :]
    return c, c * _sigmoid(c)


def _gdn_prep(proj, conv_w, a_pad, dt_pad, seq, name):
    t = proj.shape[0]
    ts = _tile(seq, 256)
    tpe = seq // ts
    qscale = GDN_HEAD_DIM ** -0.5

    def body(x_ref, gt_ref, w_ref, a_ref, dt_ref, qo_ref, ko_ref, vo_ref, go_ref, xp):
        i = pl.program_id(0)
        tail = xp[pl.ds(ts, HALO), :]
        xp[pl.ds(0, HALO), :] = jnp.where(i % tpe == 0, jnp.zeros_like(tail), tail)
        xp[pl.ds(HALO, ts), :] = x_ref[...]
        _, s = _conv_silu_l2(xp, w_ref[...], ts)
        for h in range(GDN_HEADS):
            for base, ref, sc in ((0, qo_ref, qscale), (GDN_WIDTH, ko_ref, 1.0)):
                xh = s[:, base + h * LANES: base + (h + 1) * LANES]
                r = lax.rsqrt(jnp.sum(xh * xh, axis=-1, keepdims=True) + EPS)
                ref[:, h * LANES:(h + 1) * LANES] = (xh * (r * sc)).astype(BF)
        vo_ref[...] = s[:, 2 * GDN_WIDTH:].astype(BF)
        gate = gt_ref[...]
        g_raw = -jnp.exp(a_ref[...]) * _softplus(gate + dt_ref[...])
        gc = _mask_dot(_chunk_tri(ts, False), g_raw)
        is_a, is_b = _gate_lanes(ts)
        go_ref[...] = jnp.where(is_a, gc, jnp.where(is_b, _sigmoid(gate), 0.0))

    out = pl.BlockSpec((ts, GDN_WIDTH), lambda i: (i, 0))
    lanes = pl.BlockSpec((ts, LANES), lambda i: (i, 0))
    return pl.pallas_call(
        body, name=name, grid=(t // ts,),
        in_specs=[pl.BlockSpec((ts, GDN_QKV), lambda i: (i, GDN_COL // GDN_QKV)),
                  pl.BlockSpec((ts, LANES), lambda i: (i, GATE_COL // LANES)),
                  _full((CONV_WIDTH, GDN_QKV)), _full((1, LANES)), _full((1, LANES))],
        out_specs=[out, out, out, lanes],
        out_shape=[jax.ShapeDtypeStruct((t, GDN_WIDTH), BF)] * 3 + [jax.ShapeDtypeStruct((t, LANES), F32)],
        scratch_shapes=[pltpu.VMEM((ts + HALO, GDN_QKV), F32)],
        compiler_params=_params(("arbitrary",)),
    )(proj, proj, conv_w, a_pad, dt_pad)


def _gdn_prep_bwd(proj, dq, dk, dv, dgates, dff, conv_w, a_pad, dt_pad, seq, name):
    t = proj.shape[0]
    ts = _tile(seq, 256)
    tpe = seq // ts
    nt = t // ts
    qscale = GDN_HEAD_DIM ** -0.5
    hb = ts // HALO

    def body(x_ref, halo_ref, gt_ref, dq_ref, dk_ref, dv_ref, dgt_ref, dff_ref, w_ref, a_ref, dt_ref,
             dx_ref, dgo_ref, dw_ref, da_ref, ddt_ref, xp, dcp, carry):
        i = pl.program_id(0)
        ti = nt - 1 - i

        @pl.when(i == 0)
        def _():
            dw_ref[...] = jnp.zeros_like(dw_ref)
            da_ref[...] = jnp.zeros_like(da_ref)
            ddt_ref[...] = jnp.zeros_like(ddt_ref)

        halo = halo_ref[...]
        xp[pl.ds(0, HALO), :] = jnp.where(ti % tpe == 0, jnp.zeros_like(halo), halo)
        xp[pl.ds(HALO, ts), :] = x_ref[...]
        w = w_ref[...]
        c, s = _conv_silu_l2(xp, w, ts)
        for h in range(GDN_HEADS):
            for base, ref, sc in ((0, dq_ref, qscale), (GDN_WIDTH, dk_ref, 1.0)):
                lo = base + h * LANES
                xh = s[:, lo:lo + LANES]
                r = lax.rsqrt(jnp.sum(xh * xh, axis=-1, keepdims=True) + EPS)
                y = xh * r
                dy = ref[:, h * LANES:(h + 1) * LANES] * sc
                dcp[pl.ds(0, ts), lo:lo + LANES] = r * (dy - y * jnp.sum(dy * y, axis=-1, keepdims=True))
        dcp[pl.ds(0, ts), 2 * GDN_WIDTH:] = dv_ref[...]
        sg = _sigmoid(c)
        dc = dcp[pl.ds(0, ts), :] * (sg * (1.0 + c * (1.0 - sg)))
        dcp[pl.ds(0, ts), :] = dc
        nxt = carry[...]
        dcp[pl.ds(ts, HALO), :] = jnp.where(ti % tpe == tpe - 1, jnp.zeros_like(nxt), nxt)
        carry[...] = dc[0:HALO, :]
        dx = w[CONV_WIDTH - 1:CONV_WIDTH, :] * dc
        for kk in range(CONV_WIDTH - 1):
            dx = dx + w[kk:kk + 1, :] * dcp[pl.ds(CONV_WIDTH - 1 - kk, ts), :]
        dx_ref[...] = dx.astype(BF)
        for kk in range(CONV_WIDTH):
            dw_ref[kk:kk + 1, :] += jnp.sum(dc * xp[pl.ds(HALO - 3 + kk, ts), :], axis=0, keepdims=True)
        gate = gt_ref[...]
        dgt = dgt_ref[...]
        is_a, is_b = _gate_lanes(ts)
        dg_raw = _mask_dot(_chunk_tri(ts, True), jnp.where(is_a, dgt, 0.0))
        z = gate + dt_ref[...]
        na = -jnp.exp(a_ref[...])
        dga = dg_raw * na * _sigmoid(z)
        beta = _sigmoid(gate)
        dgb = jnp.where(is_b, dgt * beta * (1.0 - beta), 0.0)
        dgo_ref[...] = (dff_ref[...] + dga + dgb).astype(BF)
        ddt_ref[...] += jnp.sum(dga, axis=0, keepdims=True)
        da_ref[...] += jnp.sum(dg_raw * na * _softplus(z), axis=0, keepdims=True)

    rev = lambda wd, j: pl.BlockSpec((ts, wd), lambda i: (nt - 1 - i, j))
    halo_spec = pl.BlockSpec((HALO, GDN_QKV), lambda i: (jnp.maximum((nt - 1 - i) * hb - 1, 0), GDN_COL // GDN_QKV))
    return pl.pallas_call(
        body, name=name, grid=(nt,),
        in_specs=[rev(GDN_QKV, GDN_COL // GDN_QKV), halo_spec, rev(LANES, GATE_COL // LANES),
                  rev(GDN_WIDTH, 0), rev(GDN_WIDTH, 0), rev(GDN_WIDTH, 0), rev(LANES, 0), rev(LANES, 0),
                  _full((CONV_WIDTH, GDN_QKV)), _full((1, LANES)), _full((1, LANES))],
        out_specs=[rev(GDN_QKV, 0), rev(LANES, 0), _full((CONV_WIDTH, GDN_QKV)), _full((1, LANES)),
                   _full((1, LANES))],
        out_shape=[jax.ShapeDtypeStruct((t, GDN_QKV), BF), jax.ShapeDtypeStruct((t, LANES), BF),
                   jax.ShapeDtypeStruct((CONV_WIDTH, GDN_QKV), F32), jax.ShapeDtypeStruct((1, LANES), F32),
                   jax.ShapeDtypeStruct((1, LANES), F32)],
        scratch_shapes=[pltpu.VMEM((ts + HALO, GDN_QKV), F32), pltpu.VMEM((ts + HALO, GDN_QKV), F32),
                        pltpu.VMEM((HALO, GDN_QKV), F32)],
        compiler_params=_params(("arbitrary",)),
    )(proj, proj, proj, dq, dk, dv, dgates, dff, conv_w, a_pad, dt_pad)


PAIR = 2 * CHUNK


def _split_bf16(a):
    hi = a.astype(BF)
    return hi, (a - hi.astype(F32)).astype(BF)


def _dot3(a, b, dims=(((1,), (0,)), ((), ()))):
    ah, al = _split_bf16(a)
    bh, bl = _split_bf16(b)
    (ca,), (cb,) = dims[0]
    return lax.dot_general(jnp.concatenate([ah, al, ah], axis=ca), jnp.concatenate([bh, bh, bl], axis=cb), dims,
                           preferred_element_type=F32)


def _inv_unit_lower(a):
    r = lax.broadcasted_iota(jnp.int32, (PAIR, PAIR), 0)
    c = lax.broadcasted_iota(jnp.int32, (PAIR, PAIR), 1)
    tm = (r == c).astype(F32) - a
    pw = _dot3(a, a)
    for _ in range(4):
        x = _dot3(jnp.concatenate([tm, pw], axis=0), pw)
        tm = tm + x[:PAIR]
        pw = x[PAIR:]
    return tm + _dot3(tm, pw)


def _gdn_pair_local(q, k, v, gc, gr, b):
    r = lax.broadcasted_iota(jnp.int32, (PAIR, PAIR), 0)
    c = lax.broadcasted_iota(jnp.int32, (PAIR, PAIR), 1)
    same = (r // CHUNK) == (c // CHUNK)
    incl = same & (r >= c)
    strict = same & (r > c)
    dm = jnp.exp(jnp.where(incl, gc - gr, NEG))
    e = jnp.exp(gc)
    kb = k * b
    vb = v * b
    kbe = kb * e
    kq = _dot_nt(jnp.concatenate([kb, q], axis=0).astype(BF), k.astype(BF))
    amat = jnp.where(strict, kq[:PAIR] * dm, 0.0)
    pmat = jnp.where(incl, kq[PAIR:] * dm, 0.0)
    lane = lax.broadcasted_iota(jnp.int32, (1, PAIR), 1)
    gl_a = jnp.sum(jnp.where(lane == CHUNK - 1, gr, 0.0), axis=1, keepdims=True)
    gl_b = jnp.sum(jnp.where(lane == PAIR - 1, gr, 0.0), axis=1, keepdims=True)
    ridx = lax.broadcasted_iota(jnp.int32, (PAIR, 1), 0)
    edec = jnp.exp(jnp.where(ridx < CHUNK, gl_a, gl_b) - gc)
    return dict(dm=dm, e=e, kb=kb, vb=vb, kbe=kbe, amat=amat, pmat=pmat, gl_a=gl_a, gl_b=gl_b, edec=edec,
                kd=k * edec, qd=q * e, incl=incl, strict=strict, ridx=ridx)


def _gdn_pair_states(loc, tb, s_a):
    uw = _dot(tb, jnp.concatenate([loc["vb"], loc["kbe"]], axis=1).astype(BF))
    u, w = uw[:, :LANES], uw[:, LANES:]
    qd, kd, c = loc["qd"], loc["kd"], CHUNK
    xa = _dot(jnp.concatenate([qd[:c], w[:c]], axis=0).astype(BF), s_a.astype(BF))
    vn_a = u[:c] - xa[c:]
    s_b = s_a * jnp.exp(loc["gl_a"]) + _dot_tn(kd[:c].astype(BF), vn_a.astype(BF))
    xb = _dot(jnp.concatenate([qd[c:], w[c:]], axis=0).astype(BF), s_b.astype(BF))
    vn_b = u[c:] - xb[c:]
    s_c = s_b * jnp.exp(loc["gl_b"]) + _dot_tn(kd[c:].astype(BF), vn_b.astype(BF))
    vn = jnp.concatenate([vn_a, vn_b], axis=0)
    o = jnp.concatenate([xa[:c], xb[:c]], axis=0) + _dot(loc["pmat"].astype(BF), vn.astype(BF))
    return w, vn, o, s_b, s_c


GDN_SEG = 1024
HEADS_PER_STEP = 4
HEAD_GROUPS = GDN_HEADS // HEADS_PER_STEP


def _gdn_specs(nb, seq, reverse):
    n = seq // CHUNK
    seg = _tile(seq, GDN_SEG)
    nseg = seq // seg
    sp = seg // PAIR
    w2 = HEADS_PER_STEP * LANES
    at = (lambda s: nseg - 1 - s) if reverse else (lambda s: s)
    blk = pl.BlockSpec((seg, w2), lambda b, hp, s: (b * nseg + at(s), hp))
    gg = pl.BlockSpec((seg, w2), lambda b, hp, s: (b * nseg + at(s), GG_COL // w2 + hp))
    gates = pl.BlockSpec((seg, LANES), lambda b, hp, s: (b * nseg + at(s), 0))
    grp = lambda b, hp: b * HEAD_GROUPS + hp
    rowb = pl.BlockSpec((None, HEADS_PER_STEP, sp, HALO, PAIR), lambda b, hp, s: (grp(b, hp), 0, at(s), 0, 0))
    per_pair = pl.BlockSpec((None, HEADS_PER_STEP, sp, PAIR, PAIR), lambda b, hp, s: (grp(b, hp), 0, at(s), 0, 0))
    dgates = pl.BlockSpec((None, seg, LANES), lambda b, hp, s: (grp(b, hp), at(s), 0))
    return n, seg, nseg, sp, blk, gg, gates, rowb, per_pair, dgates


def _head_column(gt, lane, index):
    return jnp.sum(jnp.where(lane == index, gt, 0.0), axis=1, keepdims=True)


def _gdn_head_inputs(qkv_refs, gt, gr_ref, rows, pi, hp, lane):
    per_head = []
    for hh in range(HEADS_PER_STEP):
        head = HEADS_PER_STEP * hp + hh
        cols = slice(hh * LANES, (hh + 1) * LANES)
        per_head.append([r[rows, cols].astype(F32) for r in qkv_refs]
                        + [_head_column(gt, lane, A_LANE + head), gr_ref[hh, pi][0:1, :],
                           _head_column(gt, lane, B_LANE + head)])
    return [jnp.stack(xs) for xs in zip(*per_head)]


def _gdn_pair_fwd(qv, kv, vv, gcv, gr, bv, s_a):
    loc = _gdn_pair_local(qv, kv, vv, gcv, gr, bv)
    tf = _inv_unit_lower(loc["amat"])
    _, _, o, _, s_c = _gdn_pair_states(loc, tf.astype(BF), s_a)
    return tf, o, s_c


def _gdn_fwd(q, k, v, proj, gates, grow, wn, nb, seq, name):
    t = q.shape[0]
    n, seg, nseg, sp, blk, gg, gates_spec, rowb, per_pair, _ = _gdn_specs(nb, seq, False)

    def body(q_ref, k_ref, v_ref, gg_ref, gt_ref, gr_ref, wn_ref, y_ref, tn_ref, sn_ref, s_ref):
        hp = pl.program_id(1)

        @pl.when(pl.program_id(2) == 0)
        def _():
            s_ref[...] = jnp.zeros_like(s_ref)

        wnv = wn_ref[...]
        lane = lax.broadcasted_iota(jnp.int32, (PAIR, LANES), 1)

        def step(pi, carry):
            rows = pl.ds(pl.multiple_of(pi * PAIR, PAIR), PAIR)
            gt = gt_ref[rows, :]
            ins = _gdn_head_inputs((q_ref, k_ref, v_ref), gt, gr_ref, rows, pi, hp, lane)
            s_a = s_ref[...]
            tf, o, s_c = jax.vmap(_gdn_pair_fwd)(*ins, s_a)
            s_ref[...] = s_c
            for hh in range(HEADS_PER_STEP):
                cols = slice(hh * LANES, (hh + 1) * LANES)
                tn_ref[hh, pi] = tf[hh]
                sn_ref[hh, pi] = s_a[hh]
                g = gg_ref[rows, cols]
                oh = o[hh]
                rstd = lax.rsqrt(jnp.mean(oh * oh, axis=-1, keepdims=True) + EPS)
                y_ref[rows, cols] = (oh * rstd * wnv * (g * _sigmoid(g))).astype(BF)
            return carry

        lax.fori_loop(0, sp, step, 0)

    saved = jax.ShapeDtypeStruct((nb * HEAD_GROUPS, HEADS_PER_STEP, n // 2, PAIR, PAIR), F32)
    return pl.pallas_call(
        body, name=name, grid=(nb, GDN_HEADS // HEADS_PER_STEP, nseg),
        in_specs=[blk, blk, blk, gg, gates_spec, rowb, _full((1, LANES))],
        out_specs=[blk, per_pair, per_pair],
        out_shape=[jax.ShapeDtypeStruct((t, GDN_WIDTH), BF), saved, saved],
        scratch_shapes=[pltpu.VMEM((HEADS_PER_STEP, GDN_HEAD_DIM, GDN_HEAD_DIM), F32)],
        compiler_params=_params(("parallel", "parallel", "arbitrary")),
    )(q, k, v, proj, gates, grow, wn)


def _gdn_pair_bwd(qv, kv, vv, gcv, gr, bv, tf, s_a, dsp, g, dyv, wnv):
    c = CHUNK
    loc = _gdn_pair_local(qv, kv, vv, gcv, gr, bv)
    tm = tf.astype(BF)
    kb, vb, kbe, e, dm = loc["kb"], loc["vb"], loc["kbe"], loc["e"], loc["dm"]
    kd, qd, pmat, amat = loc["kd"], loc["qd"], loc["pmat"], loc["amat"]
    w, vn, o, s_b, _ = _gdn_pair_states(loc, tm, s_a)
    sg = _sigmoid(g)
    silu = g * sg
    rstd = lax.rsqrt(jnp.mean(o * o, axis=-1, keepdims=True) + EPS)
    xhat = o * rstd
    dwn = jnp.sum(dyv * xhat * silu, axis=0, keepdims=True)
    dgg = dyv * xhat * wnv * (sg * (1.0 + g * (1.0 - sg)))
    dxhat = dyv * wnv * silu
    do = rstd * (dxhat - xhat * jnp.mean(dxhat * xhat, axis=-1, keepdims=True))
    dob = do.astype(BF)
    tot = lambda x: jnp.sum(jnp.sum(x, axis=1, keepdims=True), axis=0, keepdims=True)
    rsum = lambda x: jnp.sum(x, axis=1, keepdims=True)
    cat = lambda xs, ax=0: jnp.concatenate(xs, axis=ax)
    wb = w.astype(BF)
    qdb = qd.astype(BF)
    kdb = kd.astype(BF)
    vnb = vn.astype(BF)
    egl_a = jnp.exp(loc["gl_a"])
    egl_b = jnp.exp(loc["gl_b"])
    ptdo = _dot_tn(pmat.astype(BF), dob)
    dspb = dsp.astype(BF)
    dvn_b = ptdo[c:] + _dot(kdb[c:], dspb)
    dkd_b = _dot_nt(vnb[c:], dspb)
    dgl_b = egl_b * tot(s_b * dsp) + tot(dkd_b * kd[c:])
    dsm = egl_b * dsp + _dot_tn(cat([qdb[c:], -wb[c:]]), cat([dob[c:], dvn_b.astype(BF)]))
    dsmb = dsm.astype(BF)
    dvn_a = ptdo[:c] + _dot(kdb[:c], dsmb)
    dkd_a = _dot_nt(vnb[:c], dsmb)
    dgl_a = egl_a * tot(s_a * dsm) + tot(dkd_a * kd[:c])
    ds_new = egl_a * dsm + _dot_tn(cat([qdb[:c], -wb[:c]]), cat([dob[:c], dvn_a.astype(BF)]))
    ya = _dot_nt(cat([dob[:c], dvn_a.astype(BF)]), s_a.astype(BF))
    yb = _dot_nt(cat([dob[c:], dvn_b.astype(BF)]), s_b.astype(BF))
    dqd = cat([ya[:c], yb[:c]])
    dw = -cat([ya[c:], yb[c:]])
    dvn = cat([dvn_a, dvn_b])
    dkd = cat([dkd_a, dkd_b])
    dq = dqd * e
    dgc = rsum(dqd * qd) - rsum(dkd * kd)
    dk = dkd * loc["edec"]
    dpm = jnp.where(loc["incl"], _dot_nt(dob, vnb), 0.0)
    duw = cat([dvn, dw], 1).astype(BF)
    dt = _dot_nt(duw, cat([vb, kbe], 1).astype(BF))
    tt = _dot_tn(tm, duw)
    dvb, dkbe = tt[:, :LANES], tt[:, LANES:]
    tn_dims = (((0,), (0,)), ((), ()))
    nt_dims = (((1,), (1,)), ((), ()))
    da = jnp.where(loc["strict"], -_dot3(_dot3(tf, dt, tn_dims), tf, nt_dims), 0.0)
    st = cat([da * dm, dpm * dm]).astype(BF)
    z = _dot(st, kv.astype(BF))
    dkb = z[:PAIR] + dkbe * e
    dq = dq + z[PAIR:]
    dk = dk + _dot_tn(st, cat([kb, qv]).astype(BF))
    gmat = dpm * pmat + da * amat
    dgc = dgc + rsum(dkbe * kbe) + rsum(gmat)
    ridx = loc["ridx"]
    dgc = dgc + jnp.where(ridx == c - 1, dgl_a, 0.0) + jnp.where(ridx == PAIR - 1, dgl_b, 0.0)
    dgc_row = jnp.sum(gmat, axis=0, keepdims=True)
    db = rsum(dvb * vv) + rsum(dkb * kv)
    return dq, dk + dkb * bv, dvb * bv, dgg, dgc, dgc_row, db, dwn, ds_new


def _gdn_bwd(q, k, v, proj, gates, grow, wn, tinv_all, states_all, dy, nb, seq, name):
    t = q.shape[0]
    n, seg, nseg, sp, blk, gg, gates_spec, rowb, per_pair, dgates = _gdn_specs(nb, seq, True)
    dh = GDN_HEAD_DIM

    def body(q_ref, k_ref, v_ref, gg_ref, gt_ref, gr_ref, wn_ref, tn_ref, sn_ref, dy_ref,
             dq_ref, dk_ref, dv_ref, dgg_ref, dgt_ref, dwn_ref, ds_ref):
        hp = pl.program_id(1)

        @pl.when((pl.program_id(0) == 0) & (hp == 0) & (pl.program_id(2) == 0))
        def _():
            dwn_ref[...] = jnp.zeros_like(dwn_ref)

        @pl.when(pl.program_id(2) == 0)
        def _():
            ds_ref[...] = jnp.zeros_like(ds_ref)

        wnv = wn_ref[...]
        lane = lax.broadcasted_iota(jnp.int32, (PAIR, LANES), 1)

        def bwd_step(j, carry):
            pi = sp - 1 - j
            rows = pl.ds(pl.multiple_of(pi * PAIR, PAIR), PAIR)
            gt = gt_ref[rows, :]
            ins = _gdn_head_inputs((q_ref, k_ref, v_ref), gt, gr_ref, rows, pi, hp, lane)
            halves = [slice(hh * LANES, (hh + 1) * LANES) for hh in range(HEADS_PER_STEP)]
            saved = [jnp.stack([r[hh, pi] for hh in range(HEADS_PER_STEP)]) for r in (tn_ref, sn_ref)]
            g2 = jnp.stack([gg_ref[rows, cols] for cols in halves])
            dy2 = jnp.stack([dy_ref[rows, cols].astype(F32) for cols in halves])
            dq, dk, dv, dgg, dgc, dgc_row, db, dwn, ds_new = jax.vmap(
                _gdn_pair_bwd, in_axes=(0,) * 11 + (None,))(*ins, *saved, ds_ref[...], g2, dy2, wnv)
            ds_ref[...] = ds_new
            dgt = jnp.zeros((PAIR, LANES), F32)
            for hh, cols in enumerate(halves):
                head = HEADS_PER_STEP * hp + hh
                dq_ref[rows, cols] = dq[hh]
                dk_ref[rows, cols] = dk[hh]
                dv_ref[rows, cols] = dv[hh]
                dgg_ref[rows, cols] = dgg[hh].astype(BF)
                dwn_ref[...] += dwn[hh]
                row_as_col = jnp.transpose(jnp.broadcast_to(dgc_row[hh], (PAIR, LANES)))
                dgt = (dgt + jnp.where(lane == A_LANE + head, dgc[hh] - row_as_col, 0.0)
                       + jnp.where(lane == B_LANE + head, db[hh], 0.0))
            dgt_ref[rows, :] = dgt
            return carry

        lax.fori_loop(0, sp, bwd_step, 0)

    return pl.pallas_call(
        body, name=name, grid=(nb, GDN_HEADS // HEADS_PER_STEP, nseg),
        in_specs=[blk, blk, blk, gg, gates_spec, rowb, _full((1, LANES)), per_pair, per_pair, blk],
        out_specs=[blk, blk, blk, blk, dgates, _full((1, LANES))],
        out_shape=[jax.ShapeDtypeStruct((t, GDN_WIDTH), F32)] * 3 + [
            jax.ShapeDtypeStruct((t, GDN_WIDTH), BF),
            jax.ShapeDtypeStruct((nb * HEAD_GROUPS, seq, LANES), F32),
            jax.ShapeDtypeStruct((1, LANES), F32)],
        scratch_shapes=[pltpu.VMEM((HEADS_PER_STEP, dh, dh), F32)],
        compiler_params=_params(("arbitrary", "arbitrary", "arbitrary")),
    )(q, k, v, proj, gates, grow, wn, tinv_all, states_all, dy)


def _mix_to_padded(w):
    pad = jnp.zeros(w.shape[:-1] + (N_PAD - N_IN,), w.dtype)
    return jnp.concatenate([w[..., 0:1536], w[..., 1544:3080], w[..., 3088:3600], w[..., 1536:1544],
                            w[..., 3080:3088], pad], axis=-1)


def _mix_from_padded(g):
    return jnp.concatenate([g[..., 0:1536], g[..., 3584:3592], g[..., 1536:3072], g[..., 3592:3600],
                            g[..., 3072:3584]], axis=-1)


def _pad_lanes(vec, start):
    return jnp.pad(vec[None, :], ((0, 0), (start, LANES - start - vec.shape[0])))


def _heads_to_rows(block, lane0, nheads, nb, seq):
    return block[:, lane0:lane0 + nheads].reshape(nb, seq, nheads).transpose(0, 2, 1).reshape(nb * nheads, seq)


def _mixer_small(p, l):
    wq_t = jnp.tile(p["fox_q_norm"][l], FOX_HEADS)[None, :]
    wk_t = jnp.tile(p["fox_k_norm"][l], FOX_HEADS)[None, :]
    bias = _pad_lanes(p["fox_f_bias"][l], 0)
    a_pad = _pad_lanes(p["gdn_a_log"][l], A_LANE)
    dt_pad = _pad_lanes(p["gdn_dt_bias"][l], A_LANE)
    wn = p["gdn_out_norm"][l][None, :]
    return wq_t, wk_t, bias, a_pad, dt_pad, wn


def _layer_fwd(x, p, l, nb, seq):
    npair = FOX_HEADS // 2
    n = seq // CHUNK
    wq_t, wk_t, bias, a_pad, dt_pad, wn = _mixer_small(p, l)
    x1, h1 = _ffn_fwd(x, p["ffn1_norm"][l][None, :], p["ffn1_w_in"], p["ffn1_w_out"], l, f"ffn1_fwd_{l}")
    proj = _norm_matmul(x1, p["mix_norm"][l][None, :], p["w_mix"][l], f"mix_in_{l}")
    fq, fk, fv, cum = _fox_prep(proj, wq_t, wk_t, bias, seq, f"fox_prep_{l}")
    c8 = _heads_to_rows(cum, 0, FOX_HEADS, nb, seq)
    ck = c8.reshape(nb * npair, 2, 1, seq)
    o, lse = _fox_attn(fq, fk, fv, ck, nb, seq, f"fox_attn_{l}")
    gq, gk, gv, gates = _gdn_prep(proj, p["gdn_conv"][l], a_pad, dt_pad, seq, f"gdn_prep_{l}")
    gc4 = _heads_to_rows(gates, A_LANE, GDN_HEADS, nb, seq)
    grow = jnp.broadcast_to(gc4.reshape(nb * HEAD_GROUPS, HEADS_PER_STEP, n // 2, 1, PAIR),
                            (nb * HEAD_GROUPS, HEADS_PER_STEP, n // 2, HALO, PAIR))
    y, tinv, states = _gdn_fwd(gq, gk, gv, proj, gates, grow, wn, nb, seq, f"gdn_fwd_{l}")
    x2 = _mix_out(x1, o, y, p["w_out"][l], f"mix_out_{l}")
    x3, h2 = _ffn_fwd(x2, p["ffn2_norm"][l][None, :], p["ffn2_w_in"], p["ffn2_w_out"], l, f"ffn2_fwd_{l}")
    saved = dict(x=x, h1=h1, x1=x1, proj=proj, fq=fq, fk=fk, fv=fv, ck=ck, o=o, lse=lse,
                 gq=gq, gk=gk, gv=gv, gates=gates, grow=grow, tinv=tinv, states=states, y=y, x2=x2, h2=h2)
    return x3, saved


def _ffn_grads(dy, x, h, gain, win, wout, l, tag):
    t, d = x.shape
    fs = win.shape[3]
    dx, dh, a, hn, dyh, dgain = _ffn_bwd(dy, x, h, gain, win, wout, l, f"{tag}_bwd_{l}")
    g_in = _wgrad(hn, dh, jax.ShapeDtypeStruct((4, d, fs), BF),
                  pl.BlockSpec((None, d, fs), lambda i, j, k: (j, i, 0)), d, fs, f"{tag}_gw_in_{l}")
    g_out = _wgrad(a, dyh, jax.ShapeDtypeStruct((2 * fs, d), BF),
                   pl.BlockSpec((fs, d), lambda i, j, k: (i, j)), fs, d, f"{tag}_gw_out_{l}")
    return dx, dgain[0], g_in, g_out.reshape(4, fs // 2, d)


def _layer_bwd(dx3, p, l, sv, nb, seq):
    npair = FOX_HEADS // 2
    d = dx3.shape[1]
    wq_t, wk_t, bias, a_pad, dt_pad, wn = _mixer_small(p, l)
    g = {}
    dx2, g["ffn2_norm"], g["ffn2_w_in"], g["ffn2_w_out"] = _ffn_grads(
        dx3, sv["x2"], sv["h2"], p["ffn2_norm"][l][None, :], p["ffn2_w_in"], p["ffn2_w_out"], l, "ffn2")
    dyf, dyg, dxb = _mix_out_bwd(dx2, p["w_out"][l], f"mix_out_bwd_{l}")
    half = lambda a, nm: _wgrad(a, dxb, jax.ShapeDtypeStruct((FOX_WIDTH, d), BF),
                                pl.BlockSpec((FOX_WIDTH, d), lambda i, j, k: (i, j)), FOX_WIDTH, d, nm)
    g["w_out"] = jnp.concatenate([half(sv["o"], f"gw_out_fox_{l}"), half(sv["y"], f"gw_out_gdn_{l}")], axis=0)
    dqa, dqb, dk, dv, dkx = _fox_attn_bwd(sv["fq"], sv["fk"], sv["fv"], sv["o"], dyf, sv["lse"], sv["ck"],
                                          nb, seq, f"fox_attn_bwd_{l}")

    def sums(first_head, second_head):
        a = first_head.reshape(nb * seq, npair, LANES)[:, :, FOX_HEAD_DIM]
        b = second_head.reshape(nb * seq, npair, LANES)[:, :, 0]
        return jnp.stack([a, b], axis=2).reshape(nb * seq, FOX_HEADS)

    dcum = jnp.pad(sums(dqa, dqb) - sums(dkx, dkx), ((0, 0), (0, LANES - FOX_HEADS)))
    dpf, dff, dwq, dwk, dbias = _fox_prep_bwd(sv["proj"], dqa, dqb, dk, dv, dcum, wq_t, wk_t, bias, seq,
                                              f"fox_prep_bwd_{l}")
    g["fox_q_norm"] = dwq[0, :FOX_HEAD_DIM]
    g["fox_k_norm"] = dwk[0, :FOX_HEAD_DIM]
    g["fox_f_bias"] = dbias[0, :FOX_HEADS]
    dgq, dgk, dgv, dgg, dgt, dwn = _gdn_bwd(
        sv["gq"], sv["gk"], sv["gv"], sv["proj"], sv["gates"], sv["grow"], wn, sv["tinv"], sv["states"],
        dyg, nb, seq, f"gdn_bwd_{l}")
    dgates = jnp.sum(dgt.reshape(nb, HEAD_GROUPS, seq, LANES), axis=1).reshape(nb * seq, LANES)
    dpg, dgate_blk, dconv, da, ddt = _gdn_prep_bwd(sv["proj"], dgq, dgk, dgv, dgates, dff, p["gdn_conv"][l],
                                                   a_pad, dt_pad, seq, f"gdn_prep_bwd_{l}")
    g["gdn_conv"] = dconv
    g["gdn_a_log"] = da[0, A_LANE:B_LANE]
    g["gdn_dt_bias"] = ddt[0, A_LANE:B_LANE]
    g["gdn_out_norm"] = dwn[0]
    dproj = jnp.concatenate([dpf, dpg, dgg, dgate_blk], axis=1)
    dx1, hnm, dgm = _norm_matmul_bwd(dx2, dproj, sv["x1"], p["mix_norm"][l][None, :], p["w_mix"][l],
                                     f"mix_in_bwd_{l}")
    g["mix_norm"] = dgm[0]
    g["w_mix"] = _wgrad(hnm, dproj, jax.ShapeDtypeStruct((d, N_PAD), F32),
                        pl.BlockSpec((d // 2, N_PAD), lambda i, j, k: (i, j)), d // 2, N_PAD, f"gw_mix_{l}")
    dx0, g["ffn1_norm"], g["ffn1_w_in"], g["ffn1_w_out"] = _ffn_grads(
        dx1, sv["x"], sv["h1"], p["ffn1_norm"][l][None, :], p["ffn1_w_in"], p["ffn1_w_out"], l, "ffn1")
    return dx0, g


def _local_step(x, target, p):
    nb, seq, d = x.shape
    xt = x.reshape(nb * seq, d)
    saved = []
    for l in range(DEPTH):
        xt, sv = _layer_fwd(xt, p, l, nb, seq)
        saved.append(sv)
    loss, dx = _loss_grad(xt, target.reshape(nb * seq, d), "loss")
    grads = [None] * DEPTH
    for l in reversed(range(DEPTH)):
        dx, grads[l] = _layer_bwd(dx, p, l, saved[l], nb, seq)
    return loss, dx.reshape(nb, seq, d), grads


N_CHIPS = 4


def _mesh_pos():
    return lax.axis_index("x"), lax.axis_index("y"), lax.axis_index("c")


def _other_chips(x, y):
    return [(1 - x, y), (x, 1 - y), (1 - x, 1 - y)]


def _remote(src, dst, send_sem, recv_sem, to):
    return pltpu.make_async_remote_copy(src_ref=src, dst_ref=dst, send_sem=send_sem, recv_sem=recv_sem,
                                        device_id=to, device_id_type=MESH)


def _hbm_call(body, name, ins, out_shape, scratch):
    return pl.pallas_call(
        body, name=name, out_shape=out_shape, in_specs=[HBM] * len(ins),
        out_specs=jax.tree.map(lambda _: HBM, out_shape), scratch_shapes=scratch,
        compiler_params=pltpu.CompilerParams(has_side_effects=True),
    )(*ins)


def _all_gather(shards, name):
    n = len(shards)

    def body(*refs):
        ins, outs = refs[:n], refs[n:2 * n]
        send1, recv1, send2, recv2 = refs[2 * n:]
        x, y, c = _mesh_pos()
        me = 2 * x + y
        chips = _other_chips(x, y)
        first = []
        for i in range(n):
            for j, (px, py) in enumerate(chips):
                cp = _remote(ins[i].at[c], outs[i].at[me, c], send1.at[3 * i + j], recv1.at[3 * i + j], (px, py, c))
                cp.start()
                first.append(cp)
        passed = []
        for i in range(n):
            for j, (px, py) in enumerate(chips):
                blk = outs[i].at[2 * px + py, c]
                _remote(blk, blk, send1.at[3 * i + j], recv1.at[3 * i + j], (px, py, c)).wait_recv()
                fw = _remote(blk, blk, send2.at[3 * i + j], recv2.at[3 * i + j], (x, y, 1 - c))
                fw.start()
                passed.append(fw)
        for i in range(n):
            for j, (px, py) in enumerate(chips):
                blk = outs[i].at[2 * px + py, 1 - c]
                _remote(blk, blk, send2.at[3 * i + j], recv2.at[3 * i + j], (x, y, 1 - c)).wait_recv()
        for cp in first + passed:
            cp.wait_send()

    sem = pltpu.SemaphoreType.DMA((3 * n,))
    return _hbm_call(body, name, shards, [jax.ShapeDtypeStruct((N_CHIPS,) + s.shape, s.dtype) for s in shards],
                     [sem, sem, sem, sem])


def _sibling_send_layers(gs, name):
    n = len(gs)

    def body(*refs):
        ins, outs = refs[:n], refs[n:2 * n]
        send, recv = refs[2 * n:]
        x, y, c = _mesh_pos()
        cps = [_remote(ins[i].at[1 - c], outs[i], send.at[i], recv.at[i], (x, y, 1 - c)) for i in range(n)]
        for cp in cps:
            cp.start()
        for cp in cps:
            cp.wait()

    sem = pltpu.SemaphoreType.DMA((n,))
    return _hbm_call(body, name, gs, [jax.ShapeDtypeStruct(g.shape[1:], g.dtype) for g in gs], [sem, sem])


def _chip_scatter(ps, name):
    n = len(ps)

    def body(*refs):
        ins, outs = refs[:n], refs[n:2 * n]
        send, recv = refs[2 * n:]
        x, y, c = _mesh_pos()
        cps = []
        for i in range(n):
            for j, (px, py) in enumerate(_other_chips(x, y)):
                cps.append(_remote(ins[i].at[2 * px + py], outs[i].at[j], send.at[3 * i + j], recv.at[3 * i + j],
                                   (px, py, c)))
        for cp in cps:
            cp.start()
        for cp in cps:
            cp.wait()

    sem = pltpu.SemaphoreType.DMA((3 * n,))
    return _hbm_call(body, name, ps, [jax.ShapeDtypeStruct((3,) + p.shape[1:], p.dtype) for p in ps], [sem, sem])


def _sibling_swap(rs, name):
    n = len(rs)

    def body(*refs):
        ins, outs = refs[:n], refs[n:2 * n]
        send, recv = refs[2 * n:]
        x, y, c = _mesh_pos()
        cps = [_remote(ins[i], outs[i], send.at[i], recv.at[i], (x, y, 1 - c)) for i in range(n)]
        for cp in cps:
            cp.start()
        for cp in cps:
            cp.wait()

    sem = pltpu.SemaphoreType.DMA((n,))
    return _hbm_call(body, name, rs, [jax.ShapeDtypeStruct(r.shape, r.dtype) for r in rs], [sem, sem])


def _small_all_reduce(vec, name):
    r = vec.shape[0]
    ndev = 8

    def body(v_ref, o_ref, buf, send, recv):
        x, y, c = _mesh_pos()
        me = 4 * x + 2 * y + c
        buf[me] = v_ref[...]
        cps = []
        for rel in range(1, ndev):
            px = 1 - x if rel & 4 else x
            py = 1 - y if rel & 2 else y
            pc = 1 - c if rel & 1 else c
            cps.append((_remote(v_ref, buf.at[me], send.at[rel - 1], recv.at[rel - 1], (px, py, pc)),
                        4 * px + 2 * py + pc))
        for cp, _ in cps:
            cp.start()
        for k, (cp, peer) in enumerate(cps):
            slot = buf.at[peer]
            _remote(slot, slot, send.at[k], recv.at[k], (x, y, c)).wait_recv()
        for cp, _ in cps:
            cp.wait_send()
        acc = buf[0]
        for k in range(1, ndev):
            acc = acc + buf[k]
        o_ref[...] = acc

    vm = pl.BlockSpec(memory_space=pltpu.VMEM)
    return pl.pallas_call(
        body, name=name, out_shape=jax.ShapeDtypeStruct(vec.shape, F32), in_specs=[vm], out_specs=vm,
        scratch_shapes=[pltpu.VMEM((ndev, r, LANES), F32), pltpu.SemaphoreType.DMA((ndev - 1,)),
                        pltpu.SemaphoreType.DMA((ndev - 1,))],
        compiler_params=pltpu.CompilerParams(has_side_effects=True),
    )(vec)


def _row_tile(rows, cap=512):
    for t in range(min(rows, cap), 0, -1):
        if rows % t == 0 and (t % 16 == 0 or t == rows):
            return t
    raise ValueError(rows)


def _add_pairs(a, b, name):
    k, r, c = a.shape
    tr = _row_tile(r)

    def body(a_ref, b_ref, o_ref):
        o_ref[...] = (a_ref[...].astype(F32) + b_ref[...].astype(F32)).astype(o_ref.dtype)

    spec = pl.BlockSpec((None, tr, c), lambda i, j: (i, j, 0))
    return pl.pallas_call(body, name=name, grid=(k, r // tr), in_specs=[spec, spec], out_specs=spec,
                          out_shape=jax.ShapeDtypeStruct(a.shape, a.dtype),
                          compiler_params=_params(("parallel", "parallel")))(a, b)


def _final_sum(own, sib, others, name):
    r, c = own.shape
    tr = _row_tile(r)

    def body(a_ref, b_ref, o_ref_in, out_ref):
        acc = a_ref[...].astype(F32) + b_ref[...].astype(F32)
        for k in range(3):
            acc = acc + o_ref_in[k].astype(F32)
        out_ref[...] = acc

    spec = pl.BlockSpec((tr, c), lambda i: (i, 0))
    return pl.pallas_call(body, name=name, grid=(r // tr,),
                          in_specs=[spec, spec, pl.BlockSpec((3, tr, c), lambda i: (0, i, 0))], out_specs=spec,
                          out_shape=jax.ShapeDtypeStruct((r, c), F32),
                          compiler_params=_params(("parallel",)))(own, sib, others)


def _adamw(g, w, m, v, name):
    r, c = g.shape
    tr = _row_tile(r, 256)

    def body(g_ref, w_ref, m_ref, v_ref, d_ref, mo_ref, vo_ref):
        gv = g_ref[...]
        mn = ADAM_B1 * m_ref[...] + (1.0 - ADAM_B1) * gv
        vn = ADAM_B2 * v_ref[...] + (1.0 - ADAM_B2) * (gv * gv)
        m_hat = mn / (1.0 - ADAM_B1 ** ADAM_STEP)
        v_hat = vn / (1.0 - ADAM_B2 ** ADAM_STEP)
        d_ref[...] = -ADAM_LR * (m_hat / (jnp.sqrt(v_hat) + ADAM_EPS) + ADAM_WD * w_ref[...])
        mo_ref[...] = mn
        vo_ref[...] = vn

    spec = pl.BlockSpec((tr, c), lambda i: (i, 0))
    shp = jax.ShapeDtypeStruct((r, c), F32)
    return pl.pallas_call(body, name=name, grid=(r // tr,), in_specs=[spec] * 4, out_specs=[spec] * 3,
                          out_shape=[shp] * 3, compiler_params=_params(("parallel",)))(g, w, m, v)


def _pack(arrays):
    flat = jnp.concatenate([a.reshape(-1).astype(F32) for a in arrays])
    pad = (-flat.shape[0]) % (8 * LANES)
    return jnp.concatenate([flat, jnp.zeros((pad,), F32)]).reshape(-1, LANES)


def _unpack(packed, shapes):
    flat = packed.reshape(-1)
    out, off = [], 0
    for s in shapes:
        size = 1
        for dim in s:
            size *= dim
        out.append(flat[off:off + size].reshape(s))
        off += size
    return out


BIG = ("ffn1_w_in", "ffn1_w_out", "w_in", "w_out", "ffn2_w_in", "ffn2_w_out")
SMALL = ("ffn1_norm", "mix_norm", "fox_q_norm", "fox_k_norm", "fox_f_bias", "gdn_a_log", "gdn_dt_bias",
         "gdn_out_norm", "ffn2_norm", "gdn_conv")
WEIGHTS = ("ffn1_norm", "ffn1_w_in", "ffn1_w_out", "mix_norm", "w_in", "fox_q_norm", "fox_k_norm", "fox_f_bias",
           "gdn_conv", "gdn_a_log", "gdn_dt_bias", "gdn_out_norm", "w_out", "ffn2_norm", "ffn2_w_in", "ffn2_w_out")


def _step(x, target, w, m, v):
    xi, yi, ci = _mesh_pos()
    me = 2 * xi + yi
    depth = DEPTH
    d = x.shape[-1]

    shards = [w[k].astype(BF) for k in BIG] + [w["gdn_conv"]]
    gathered = [lax.dynamic_update_index_in_dim(g, s, me, 0)
                for g, s in zip(_all_gather(shards, "all_gather_weights"), shards)]
    gw = dict(zip(BIG, gathered))
    p = {k: w[k] for k in SMALL if k != "gdn_conv"}
    conv = gathered[len(BIG)]
    p["gdn_conv"] = conv.transpose(1, 2, 0, 3).reshape(depth, CONV_WIDTH, -1)
    for k in ("ffn1_w_in", "ffn1_w_out", "ffn2_w_in", "ffn2_w_out"):
        p[k] = gw[k]
    p["w_mix"] = _mix_to_padded(gw["w_in"].transpose(1, 2, 0, 3).reshape(depth, d, N_IN))
    p["w_out"] = gw["w_out"].transpose(1, 0, 2, 3).reshape(depth, 2 * FOX_WIDTH, d)

    loss, dx, grads = _local_step(x, target, p)

    def transport(k):
        per_layer = []
        for l in range(depth):
            g = grads[l]
            if k == "w_in":
                full = _mix_from_padded(g["w_mix"])
                per_layer.append(full.reshape(d, N_CHIPS, N_IN // N_CHIPS).transpose(1, 0, 2).astype(BF))
            elif k == "w_out":
                per_layer.append(g["w_out"].reshape(N_CHIPS, -1, d))
            else:
                per_layer.append(g[k])
        return jnp.stack(per_layer)

    gs = [transport(k) for k in BIG]
    from_sib = _sibling_send_layers(gs, "grad_to_sibling")
    mine = [lax.dynamic_index_in_dim(g, ci, 0, keepdims=False) for g in gs]
    chip_sums = [_add_pairs(a, b, f"grad_chip_sum_{k}") for a, b, k in zip(mine, from_sib, BIG)]
    from_chips = _chip_scatter(chip_sums, "grad_to_chips")
    reduced = [_final_sum(lax.dynamic_index_in_dim(a, me, 0, keepdims=False),
                          lax.dynamic_index_in_dim(b, me, 0, keepdims=False), o, f"grad_final_sum_{k}")
               for a, b, o, k in zip(mine, from_sib, from_chips, BIG)]
    from_sib_final = _sibling_swap(reduced, "grad_swap_layers")
    full = {k: jnp.stack([jnp.where(ci == 0, a, b), jnp.where(ci == 0, b, a)])
            for k, a, b in zip(BIG, reduced, from_sib_final)}

    out_g, out_d, out_m, out_v = {}, {}, {}, {}
    for k in BIG:
        shp = w[k].shape
        two_d = lambda a: a.reshape(shp[0] * shp[1], shp[2])
        dl, mn, vn = _adamw(two_d(full[k]), two_d(w[k]), two_d(m[k]), two_d(v[k]), f"adamw_{k}")
        out_g[k], out_d[k], out_m[k], out_v[k] = full[k], dl.reshape(shp), mn.reshape(shp), vn.reshape(shp)

    small_local = [jnp.stack([grads[l][k] for l in range(depth)]) for k in SMALL]
    summed = _unpack(_small_all_reduce(_pack(small_local), "small_all_reduce"), [a.shape for a in small_local])
    sg = dict(zip(SMALL, summed))
    cs = w["gdn_conv"].shape[-1]
    sg["gdn_conv"] = lax.dynamic_slice_in_dim(sg["gdn_conv"], me * cs, cs, axis=2)
    shapes = [w[k].shape for k in SMALL]
    packs = [_pack([src[k] for k in SMALL]) for src in (sg, w, m, v)]
    dl, mn, vn = _adamw(*packs, "adamw_small")
    for k, a, b, c2 in zip(SMALL, _unpack(dl, shapes), _unpack(mn, shapes), _unpack(vn, shapes)):
        out_g[k], out_d[k], out_m[k], out_v[k] = sg[k], a, b, c2

    total = lax.psum(loss[0, 0], ("x", "y", "c"))
    return (total, dx, *[out_g[k] for k in WEIGHTS], *[out_d[k] for k in WEIGHTS],
            *[out_m[k] for k in WEIGHTS], *[out_v[k] for k in WEIGHTS])


def kernel(x, ffn1_norm, ffn1_w_in, ffn1_w_out, mix_norm, w_in, fox_q_norm, fox_k_norm, fox_f_bias, gdn_conv, gdn_a_log, gdn_dt_bias, gdn_out_norm, w_out, ffn2_norm, ffn2_w_in, ffn2_w_out, loss_target, m_ffn1_norm, m_ffn1_w_in, m_ffn1_w_out, m_mix_norm, m_w_in, m_fox_q_norm, m_fox_k_norm, m_fox_f_bias, m_gdn_conv, m_gdn_a_log, m_gdn_dt_bias, m_gdn_out_norm, m_w_out, m_ffn2_norm, m_ffn2_w_in, m_ffn2_w_out, v_ffn1_norm, v_ffn1_w_in, v_ffn1_w_out, v_mix_norm, v_w_in, v_fox_q_norm, v_fox_k_norm, v_fox_f_bias, v_gdn_conv, v_gdn_a_log, v_gdn_dt_bias, v_gdn_out_norm, v_w_out, v_ffn2_norm, v_ffn2_w_in, v_ffn2_w_out):
    w = dict(ffn1_norm=ffn1_norm, ffn1_w_in=ffn1_w_in, ffn1_w_out=ffn1_w_out, mix_norm=mix_norm, w_in=w_in,
             fox_q_norm=fox_q_norm, fox_k_norm=fox_k_norm, fox_f_bias=fox_f_bias, gdn_conv=gdn_conv,
             gdn_a_log=gdn_a_log, gdn_dt_bias=gdn_dt_bias, gdn_out_norm=gdn_out_norm, w_out=w_out,
             ffn2_norm=ffn2_norm, ffn2_w_in=ffn2_w_in, ffn2_w_out=ffn2_w_out)
    m = dict(ffn1_norm=m_ffn1_norm, ffn1_w_in=m_ffn1_w_in, ffn1_w_out=m_ffn1_w_out, mix_norm=m_mix_norm, w_in=m_w_in,
             fox_q_norm=m_fox_q_norm, fox_k_norm=m_fox_k_norm, fox_f_bias=m_fox_f_bias, gdn_conv=m_gdn_conv,
             gdn_a_log=m_gdn_a_log, gdn_dt_bias=m_gdn_dt_bias, gdn_out_norm=m_gdn_out_norm, w_out=m_w_out,
             ffn2_norm=m_ffn2_norm, ffn2_w_in=m_ffn2_w_in, ffn2_w_out=m_ffn2_w_out)
    v = dict(ffn1_norm=v_ffn1_norm, ffn1_w_in=v_ffn1_w_in, ffn1_w_out=v_ffn1_w_out, mix_norm=v_mix_norm, w_in=v_w_in,
             fox_q_norm=v_fox_q_norm, fox_k_norm=v_fox_k_norm, fox_f_bias=v_fox_f_bias, gdn_conv=v_gdn_conv,
             gdn_a_log=v_gdn_a_log, gdn_dt_bias=v_gdn_dt_bias, gdn_out_norm=v_gdn_out_norm, w_out=v_w_out,
             ffn2_norm=v_ffn2_norm, ffn2_w_in=v_ffn2_w_in, ffn2_w_out=v_ffn2_w_out)
    return _step(x, loss_target, w, m, v)
```

```python
import jax
import jax.numpy as jnp
from jax import lax
from jax.experimental import pallas as pl
from jax.experimental.pallas import tpu as pltpu

F32 = jnp.float32
BF = jnp.bfloat16
HI = lax.Precision.HIGHEST
MESH = pl.DeviceIdType.MESH

DEPTH = 2
FOX_HEADS = 8
FOX_HEAD_DIM = 64
FOX_WIDTH = 512
GDN_HEADS = 4
GDN_HEAD_DIM = 128
GDN_WIDTH = 512
CONV_WIDTH = 4
CHUNK = 64
EPS = 1e-6
N_IN = 3600
N_PAD = 3712
GATE_COL = 3584
LANES = 128
NEG = -1e30

ADAM_LR = 0.001
ADAM_B1 = 0.9
ADAM_B2 = 0.999
ADAM_EPS = 1e-08
ADAM_WD = 0.01
ADAM_STEP = 10

VMEM_LIMIT = 56 * 1024 * 1024


def _params(sem=None, **kw):
    return pltpu.CompilerParams(dimension_semantics=sem, vmem_limit_bytes=VMEM_LIMIT, **kw)


def _dot(a, b, precision=None):
    return jnp.dot(a, b, preferred_element_type=F32, precision=precision)


def _dot_nt(a, b, precision=None):
    return lax.dot_general(a, b, (((1,), (1,)), ((), ())), preferred_element_type=F32, precision=precision)


def _dot_tn(a, b, precision=None):
    return lax.dot_general(a, b, (((0,), (0,)), ((), ())), preferred_element_type=F32, precision=precision)


def _sigmoid(x):
    return 0.5 * jnp.tanh(0.5 * x) + 0.5


def _softplus(x):
    return jnp.maximum(x, 0.0) + jnp.log(1.0 + jnp.exp(-jnp.abs(x)))


def _log_sigmoid(x):
    return jnp.minimum(x, 0.0) - jnp.log(1.0 + jnp.exp(-jnp.abs(x)))


def _tile(n, t):
    t = min(n, t)
    assert n % t == 0, (n, t)
    return t


def _rms_fwd(x, gain):
    rstd = lax.rsqrt(jnp.mean(x * x, axis=-1, keepdims=True) + EPS)
    xhat = x * rstd
    return xhat * gain, xhat, rstd


def _rms_bwd(dy, xhat, rstd, gain):
    dxhat = dy * gain
    dx = rstd * (dxhat - xhat * jnp.mean(dxhat * xhat, axis=-1, keepdims=True))
    return dx, dy * xhat


def _full(shape):
    nd = len(shape)
    return pl.BlockSpec(shape, lambda *_: (0,) * nd)


HBM = pl.BlockSpec(memory_space=pltpu.HBM)


def _load_ffn_weights(win_hbm, wout_hbm, win_v, wout_v, sem):
    fr = wout_hbm.shape[1]
    copies = [pltpu.make_async_copy(win_hbm.at[s], win_v.at[s], sem.at[s]) for s in range(4)]
    copies += [pltpu.make_async_copy(wout_hbm.at[s], wout_v.at[pl.ds(s * fr, fr)], sem.at[4 + s])
               for s in range(4)]
    for c in copies:
        c.start()
    for c in copies:
        c.wait()


def _ffn_fwd(x, gain, win_g, wout_g, name):
    t, d = x.shape
    _, _, fs = win_g.shape
    fr = wout_g.shape[1]
    tm = _tile(t, 256)

    def body(x_ref, g_ref, win_hbm, wout_hbm, xo_ref, h_ref, win_v, wout_v, sem):
        @pl.when(pl.program_id(0) == 0)
        def _():
            _load_ffn_weights(win_hbm, wout_hbm, win_v, wout_v, sem)

        xv = x_ref[...]
        hn, _, _ = _rms_fwd(xv, g_ref[...])
        hn = hn.astype(BF)
        acc = jnp.zeros((tm, d), F32)
        for s in range(2):
            g = _dot(hn, win_v[s])
            u = _dot(hn, win_v[s + 2])
            h_ref[:, s * fs:(s + 1) * fs] = g.astype(BF)
            h_ref[:, (s + 2) * fs:(s + 3) * fs] = u.astype(BF)
            a = (g * _sigmoid(g) * u).astype(BF)
            acc = acc + _dot(a, wout_v[s * fs:(s + 1) * fs, :])
        xo_ref[...] = xv + 0.5 * acc

    return pl.pallas_call(
        body, name=name, grid=(t // tm,),
        in_specs=[pl.BlockSpec((tm, d), lambda i: (i, 0)), _full((1, d)), HBM, HBM],
        out_specs=[pl.BlockSpec((tm, d), lambda i: (i, 0)), pl.BlockSpec((tm, 4 * fs), lambda i: (i, 0))],
        out_shape=[jax.ShapeDtypeStruct((t, d), F32), jax.ShapeDtypeStruct((t, 4 * fs), BF)],
        scratch_shapes=[pltpu.VMEM((4, d, fs), BF), pltpu.VMEM((4 * fr, d), BF), pltpu.SemaphoreType.DMA((8,))],
        compiler_params=_params(("arbitrary",)),
    )(x, gain, win_g, wout_g)


def _ffn_bwd(dy, x, h, gain, win_g, wout_g, name):
    t, d = x.shape
    _, _, fs = win_g.shape
    fr = wout_g.shape[1]
    tm = _tile(t, 256)

    def body(dy_ref, x_ref, h_ref, g_ref, win_hbm, wout_hbm,
             dx_ref, dh_ref, a_ref, hn_ref, dyh_ref, dg_ref, win_v, wout_v, sem):
        @pl.when(pl.program_id(0) == 0)
        def _():
            _load_ffn_weights(win_hbm, wout_hbm, win_v, wout_v, sem)
            dg_ref[...] = jnp.zeros_like(dg_ref)

        dyv = dy_ref[...]
        dyh = (0.5 * dyv).astype(BF)
        dyh_ref[...] = dyh
        dhn = jnp.zeros((tm, d), F32)
        for s in range(2):
            da = _dot_nt(dyh, wout_v[s * fs:(s + 1) * fs, :])
            g = h_ref[:, s * fs:(s + 1) * fs].astype(F32)
            u = h_ref[:, (s + 2) * fs:(s + 3) * fs].astype(F32)
            sg = _sigmoid(g)
            si = g * sg
            a_ref[:, s * fs:(s + 1) * fs] = (si * u).astype(BF)
            dgate = (da * u * (sg * (1.0 + g * (1.0 - sg)))).astype(BF)
            dup = (da * si).astype(BF)
            dh_ref[:, s * fs:(s + 1) * fs] = dgate
            dh_ref[:, (s + 2) * fs:(s + 3) * fs] = dup
            dhn = dhn + _dot_nt(dgate, win_v[s]) + _dot_nt(dup, win_v[s + 2])
        xv = x_ref[...]
        gain_v = g_ref[...]
        hn, xhat, rstd = _rms_fwd(xv, gain_v)
        hn_ref[...] = hn.astype(BF)
        dx, dgr = _rms_bwd(dhn, xhat, rstd, gain_v)
        dx_ref[...] = dyv + dx
        dg_ref[...] += jnp.sum(dgr, axis=0, keepdims=True)

    row = lambda w: pl.BlockSpec((tm, w), lambda i: (i, 0))
    return pl.pallas_call(
        body, name=name, grid=(t // tm,),
        in_specs=[row(d), row(d), row(4 * fs), _full((1, d)), HBM, HBM],
        out_specs=[row(d), row(4 * fs), row(2 * fs), row(d), row(d), _full((1, d))],
        out_shape=[jax.ShapeDtypeStruct((t, d), F32), jax.ShapeDtypeStruct((t, 4 * fs), BF),
                   jax.ShapeDtypeStruct((t, 2 * fs), BF), jax.ShapeDtypeStruct((t, d), BF),
                   jax.ShapeDtypeStruct((t, d), BF), jax.ShapeDtypeStruct((1, d), F32)],
        scratch_shapes=[pltpu.VMEM((4, d, fs), BF), pltpu.VMEM((4 * fr, d), BF), pltpu.SemaphoreType.DMA((8,))],
        compiler_params=_params(("arbitrary",)),
    )(dy, x, h, gain, win_g, wout_g)


def _wgrad(a, b, out_shape, out_spec, tm, tn, name, tk=512):
    t, m = a.shape
    _, n = b.shape
    tk = _tile(t, tk)
    nk = t // tk

    def body(a_ref, b_ref, o_ref, acc):
        k = pl.program_id(2)

        @pl.when(k == 0)
        def _():
            acc[...] = jnp.zeros_like(acc)

        acc[...] += _dot_tn(a_ref[...], b_ref[...])

        @pl.when(k == nk - 1)
        def _():
            o_ref[...] = acc[...].astype(o_ref.dtype)

    return pl.pallas_call(
        body, name=name, grid=(m // tm, n // tn, nk),
        in_specs=[pl.BlockSpec((tk, tm), lambda i, j, k: (k, i)), pl.BlockSpec((tk, tn), lambda i, j, k: (k, j))],
        out_specs=out_spec, out_shape=out_shape,
        scratch_shapes=[pltpu.VMEM((tm, tn), F32)],
        compiler_params=_params(("parallel", "parallel", "arbitrary")),
    )(a, b)


def _norm_matmul(x, gain, w, name):
    t, d = x.shape
    n = w.shape[1]
    tm = _tile(t, 256)

    def body(x_ref, g_ref, w_ref, o_ref):
        hn, _, _ = _rms_fwd(x_ref[...], g_ref[...])
        o_ref[...] = _dot(hn.astype(BF), w_ref[...])

    return pl.pallas_call(
        body, name=name, grid=(t // tm,),
        in_specs=[pl.BlockSpec((tm, d), lambda i: (i, 0)), _full((1, d)), _full((d, n))],
        out_specs=pl.BlockSpec((tm, n), lambda i: (i, 0)),
        out_shape=jax.ShapeDtypeStruct((t, n), F32),
        compiler_params=_params(("parallel",)),
    )(x, gain, w)


def _norm_matmul_bwd(dres, dproj, x, gain, w, name):
    t, d = x.shape
    n = w.shape[1]
    tm = _tile(t, 256)

    def body(dr_ref, dp_ref, x_ref, g_ref, w_ref, dx_ref, hn_ref, dg_ref):
        @pl.when(pl.program_id(0) == 0)
        def _():
            dg_ref[...] = jnp.zeros_like(dg_ref)

        dhn = _dot_nt(dp_ref[...], w_ref[...])
        gain_v = g_ref[...]
        hn, xhat, rstd = _rms_fwd(x_ref[...], gain_v)
        hn_ref[...] = hn.astype(BF)
        dx, dgr = _rms_bwd(dhn, xhat, rstd, gain_v)
        dx_ref[...] = dr_ref[...] + dx
        dg_ref[...] += jnp.sum(dgr, axis=0, keepdims=True)

    row = lambda wd: pl.BlockSpec((tm, wd), lambda i: (i, 0))
    return pl.pallas_call(
        body, name=name, grid=(t // tm,),
        in_specs=[row(d), row(n), row(d), _full((1, d)), _full((d, n))],
        out_specs=[row(d), row(d), _full((1, d))],
        out_shape=[jax.ShapeDtypeStruct((t, d), F32), jax.ShapeDtypeStruct((t, d), BF),
                   jax.ShapeDtypeStruct((1, d), F32)],
        compiler_params=_params(("arbitrary",)),
    )(dres, dproj, x, gain, w)


def _mix_out(x, yf, yg, w, name):
    t, d = x.shape
    kf = yf.shape[1]
    tm = _tile(t, 512)

    def body(x_ref, yf_ref, yg_ref, w_ref, o_ref):
        o_ref[...] = x_ref[...] + _dot(yf_ref[...], w_ref[0:kf, :]) + _dot(yg_ref[...], w_ref[kf:2 * kf, :])

    row = lambda wd: pl.BlockSpec((tm, wd), lambda i: (i, 0))
    return pl.pallas_call(
        body, name=name, grid=(t // tm,),
        in_specs=[row(d), row(kf), row(kf), _full((2 * kf, d))],
        out_specs=row(d), out_shape=jax.ShapeDtypeStruct((t, d), F32),
        compiler_params=_params(("parallel",)),
    )(x, yf, yg, w)


def _mix_out_bwd(dx, w, name):
    t, d = dx.shape
    kf = w.shape[0] // 2
    tm = _tile(t, 512)

    def body(dx_ref, w_ref, df_ref, dg_ref, dxb_ref):
        dxb = dx_ref[...].astype(BF)
        dxb_ref[...] = dxb
        df_ref[...] = _dot_nt(dxb, w_ref[0:kf, :]).astype(BF)
        dg_ref[...] = _dot_nt(dxb, w_ref[kf:2 * kf, :]).astype(BF)

    row = lambda wd: pl.BlockSpec((tm, wd), lambda i: (i, 0))
    return pl.pallas_call(
        body, name=name, grid=(t // tm,),
        in_specs=[row(d), _full((2 * kf, d))],
        out_specs=[row(kf), row(kf), row(d)],
        out_shape=[jax.ShapeDtypeStruct((t, kf), BF), jax.ShapeDtypeStruct((t, kf), BF),
                   jax.ShapeDtypeStruct((t, d), BF)],
        compiler_params=_params(("parallel",)),
    )(dx, w)


def _loss_grad(y, target, name):
    t, d = y.shape
    tm = _tile(t, 512)

    def body(y_ref, t_ref, l_ref, dy_ref):
        @pl.when(pl.program_id(0) == 0)
        def _():
            l_ref[...] = jnp.zeros_like(l_ref)

        diff = y_ref[...] - t_ref[...]
        dy_ref[...] = diff * (1.0 / d)
        part = jnp.sum(jnp.sum(diff * diff, axis=1, keepdims=True), axis=0, keepdims=True)
        l_ref[...] += part * (0.5 / d)

    row = pl.BlockSpec((tm, d), lambda i: (i, 0))
    return pl.pallas_call(
        body, name=name, grid=(t // tm,),
        in_specs=[row, row], out_specs=[_full((1, 1)), row],
        out_shape=[jax.ShapeDtypeStruct((1, 1), F32), jax.ShapeDtypeStruct((t, d), F32)],
        compiler_params=_params(("arbitrary",)),
    )(y, target)


def _head_sum_matrix(width, head):
    r = lax.broadcasted_iota(jnp.int32, (width, width), 0) // head
    c = lax.broadcasted_iota(jnp.int32, (width, width), 1) // head
    return (r == c).astype(BF)


def _head_mean(x, bd):
    return _dot(x.astype(BF), bd) * (1.0 / FOX_HEAD_DIM)


def _mask_dot(mask01, x):
    mb = mask01.astype(BF)
    hi = x.astype(BF)
    r1 = x - hi.astype(F32)
    mid = r1.astype(BF)
    lo = (r1 - mid.astype(F32)).astype(BF)
    return _dot(mb, hi) + _dot(mb, mid) + _dot(mb, lo)


def _fox_prep(proj, wq_t, wk_t, bias_pad, seq, name):
    t = proj.shape[0]
    ts = _tile(seq, 512)
    tpe = seq // ts
    scale = FOX_HEAD_DIM ** -0.5

    def body(q_ref, k_ref, v_ref, gt_ref, wq_ref, wk_ref, b_ref, qo_ref, ko_ref, vo_ref, cum_ref, carry):
        i = pl.program_id(0)
        bd = _head_sum_matrix(FOX_WIDTH, FOX_HEAD_DIM)

        def norm(xv, wv):
            ms = _head_mean(xv * xv, bd)
            return xv * lax.rsqrt(ms + EPS) * wv

        qo_ref[...] = (norm(q_ref[...], wq_ref[...]) * scale).astype(BF)
        ko_ref[...] = norm(k_ref[...], wk_ref[...]).astype(BF)
        vo_ref[...] = v_ref[...].astype(BF)

        @pl.when(i % tpe == 0)
        def _():
            carry[...] = jnp.zeros_like(carry)

        ls = _log_sigmoid(gt_ref[...] + b_ref[...])
        r = lax.broadcasted_iota(jnp.int32, (ts, ts), 0)
        c = lax.broadcasted_iota(jnp.int32, (ts, ts), 1)
        cum = _mask_dot(r >= c, ls) + carry[...]
        cum_ref[...] = cum
        carry[...] = cum[ts - 1:ts, :]

    blk = lambda j: pl.BlockSpec((ts, FOX_WIDTH), lambda i: (i, j))
    gate = pl.BlockSpec((ts, LANES), lambda i: (i, GATE_COL // LANES))
    out = pl.BlockSpec((ts, FOX_WIDTH), lambda i: (i, 0))
    return pl.pallas_call(
        body, name=name, grid=(t // ts,),
        in_specs=[blk(0), blk(1), blk(2), gate, _full((1, FOX_WIDTH)), _full((1, FOX_WIDTH)), _full((1, LANES))],
        out_specs=[out, out, out, pl.BlockSpec((ts, LANES), lambda i: (i, 0))],
        out_shape=[jax.ShapeDtypeStruct((t, FOX_WIDTH), BF)] * 3 + [jax.ShapeDtypeStruct((t, LANES), F32)],
        scratch_shapes=[pltpu.VMEM((1, LANES), F32)],
        compiler_params=_params(("arbitrary",)),
    )(proj, proj, proj, proj, wq_t, wk_t, bias_pad)


def _fox_prep_bwd(proj, dqa, dqb, dk, dv, dcum, wq_t, wk_t, bias_pad, seq, name):
    t = proj.shape[0]
    ts = _tile(seq, 512)
    tpe = seq // ts
    nt = t // ts
    scale = FOX_HEAD_DIM ** -0.5

    def body(q_ref, k_ref, gt_ref, dqa_ref, dqb_ref, dk_ref, dv_ref, dc_ref, wq_ref, wk_ref, b_ref,
             dp_ref, dff_ref, dwq_ref, dwk_ref, db_ref, carry):
        i = pl.program_id(0)
        first = (lax.broadcasted_iota(jnp.int32, (ts, FOX_WIDTH), 1) % LANES) < FOX_HEAD_DIM
        dq_all = jnp.where(first, dqa_ref[...], dqb_ref[...])
        ti = nt - 1 - i
        bd = _head_sum_matrix(FOX_WIDTH, FOX_HEAD_DIM)

        @pl.when(i == 0)
        def _():
            dwq_ref[...] = jnp.zeros_like(dwq_ref)
            dwk_ref[...] = jnp.zeros_like(dwk_ref)
            db_ref[...] = jnp.zeros_like(db_ref)

        def norm_bwd(xv, wv, dyv):
            ms = _head_mean(xv * xv, bd)
            rstd = lax.rsqrt(ms + EPS)
            xhat = xv * rstd
            dxhat = dyv * wv
            mean = _head_mean(dxhat * xhat, bd)
            return rstd * (dxhat - xhat * mean), jnp.sum(dyv * xhat, axis=0, keepdims=True)

        dxq, dwq = norm_bwd(q_ref[...], wq_ref[...], dq_all * scale)
        dxk, dwk = norm_bwd(k_ref[...], wk_ref[...], dk_ref[...])
        dp_ref[:, 0:FOX_WIDTH] = dxq.astype(BF)
        dp_ref[:, FOX_WIDTH:2 * FOX_WIDTH] = dxk.astype(BF)
        dp_ref[:, 2 * FOX_WIDTH:3 * FOX_WIDTH] = dv_ref[...].astype(BF)
        dwq_ref[...] += dwq
        dwk_ref[...] += dwk

        @pl.when(ti % tpe == tpe - 1)
        def _():
            carry[...] = jnp.zeros_like(carry)

        r = lax.broadcasted_iota(jnp.int32, (ts, ts), 0)
        c = lax.broadcasted_iota(jnp.int32, (ts, ts), 1)
        dls = _mask_dot(c >= r, dc_ref[...]) + carry[...]
        carry[...] = dls[0:1, :]
        z = gt_ref[...] + b_ref[...]
        lane = lax.broadcasted_iota(jnp.int32, (ts, LANES), 1)
        dff = jnp.where(lane < FOX_HEADS, dls * _sigmoid(-z), 0.0)
        dff_ref[...] = dff
        db_ref[...] += jnp.sum(dff, axis=0, keepdims=True)

        @pl.when(i == nt - 1)
        def _():
            fr = lax.broadcasted_iota(jnp.int32, (FOX_WIDTH, FOX_WIDTH), 0) % FOX_HEAD_DIM
            fc = lax.broadcasted_iota(jnp.int32, (FOX_WIDTH, FOX_WIDTH), 1) % FOX_HEAD_DIM
            fold = (fr == fc).astype(F32)
            dwq_ref[...] = _dot(dwq_ref[...], fold, HI)
            dwk_ref[...] = _dot(dwk_ref[...], fold, HI)

    rev = lambda w, j: pl.BlockSpec((ts, w), lambda i: (nt - 1 - i, j))
    return pl.pallas_call(
        body, name=name, grid=(nt,),
        in_specs=[rev(FOX_WIDTH, 0), rev(FOX_WIDTH, 1), rev(LANES, GATE_COL // LANES),
                  rev(FOX_WIDTH, 0), rev(FOX_WIDTH, 0), rev(FOX_WIDTH, 0), rev(FOX_WIDTH, 0), rev(LANES, 0),
                  _full((1, FOX_WIDTH)), _full((1, FOX_WIDTH)), _full((1, LANES))],
        out_specs=[rev(3 * FOX_WIDTH, 0), rev(LANES, 0), _full((1, FOX_WIDTH)), _full((1, FOX_WIDTH)),
                   _full((1, LANES))],
        out_shape=[jax.ShapeDtypeStruct((t, 3 * FOX_WIDTH), BF), jax.ShapeDtypeStruct((t, LANES), F32),
                   jax.ShapeDtypeStruct((1, FOX_WIDTH), F32), jax.ShapeDtypeStruct((1, FOX_WIDTH), F32),
                   jax.ShapeDtypeStruct((1, LANES), F32)],
        scratch_shapes=[pltpu.VMEM((1, LANES), F32)],
        compiler_params=_params(("arbitrary",)),
    )(proj, proj, proj, dqa, dqb, dk, dv, dcum, wq_t, wk_t, bias_pad)


class _Rider:
    def __init__(self, inputs, out_shapes, sems, phases):
        self.inputs, self.out_shapes, self.sems, self.phases = list(inputs), list(out_shapes), list(sems), phases


def _rider_parts(rider):
    if rider is None:
        return [], [], []
    return rider.inputs, rider.out_shapes, rider.sems


def _ride(rider, which, when, refs):
    if rider is not None:
        @pl.when(when)
        def _():
            rider.phases[which](*refs)


def _fox_attn(q, k, v, ck, nb, seq, name, rider=None):
    t = q.shape[0]
    tq = _tile(seq, 512)
    nq = seq // tq
    npair = FOX_HEADS // 2
    hd = FOX_HEAD_DIM
    r_in, r_out, r_sem = _rider_parts(rider)
    steps = nb * npair * nq

    def body(q_ref, k_ref, v_ref, ck_ref, *rest):
        rin, (o_ref, lse_ref) = rest[:len(r_in)], rest[len(r_in):len(r_in) + 2]
        rout = rest[len(r_in) + 2:len(r_in) + 2 + len(r_out)]
        m_s, acc_s = rest[len(r_in) + 2 + len(r_out):len(r_in) + 4 + len(r_out)]
        riding = (rin, rout, rest[len(r_in) + 4 + len(r_out):])
        step = (pl.program_id(0) * npair + pl.program_id(1)) * nq + pl.program_id(2)
        _ride(rider, 0, step == 0, riding)
        _ride(rider, 1, step == steps // 2, riding)
        qi = pl.program_id(2)
        lane = lax.broadcasted_iota(jnp.int32, (tq, LANES), 1)
        m_s[...] = jnp.full(m_s.shape, NEG, F32)
        acc_s[...] = jnp.zeros_like(acc_s)
        qv = q_ref[...]

        def tile(kj, on_diagonal):
            cols = pl.ds(pl.multiple_of(kj * tq, tq), tq)
            kv = k_ref[cols, :]
            vv = v_ref[cols, :]
            if on_diagonal:
                causal = (lax.broadcasted_iota(jnp.int32, (tq, tq), 0)
                          >= lax.broadcasted_iota(jnp.int32, (tq, tq), 1))
            ck = [ck_ref[hh, :, cols] for hh in range(2)]
            m_old = [m_s[hh] for hh in range(2)]
            acc_old = [acc_s[hh] for hh in range(2)]
            m_out, acc_out = [], []
            for hh in range(2):
                hm = (lane >= hd) if hh else (lane < hd)
                qh = jnp.where(hm, qv, jnp.zeros_like(qv))
                s = _dot_nt(qh, kv) - ck[hh]
                if on_diagonal:
                    s = jnp.where(causal, s, NEG)
                m_new = jnp.maximum(m_old[hh], jnp.max(s, axis=-1, keepdims=True))
                p = jnp.exp(s - m_new)
                alpha = jnp.exp(m_old[hh] - m_new)
                m_out.append(m_new)
                acc_out.append(alpha * acc_old[hh] + _dot(p.astype(BF), jnp.where(hm, vv, jnp.ones_like(vv))))
            for hh in range(2):
                m_s[hh] = m_out[hh]
                acc_s[hh] = acc_out[hh]

        def off_diagonal(kj, carry):
            tile(kj, False)
            return carry

        lax.fori_loop(0, qi, off_diagonal, 0)
        tile(qi, True)
        a0 = acc_s[0]
        a1 = acc_s[1]
        den = jnp.where(lane < hd, pltpu.roll(a0, hd, axis=1), pltpu.roll(a1, hd, axis=1))
        o_ref[...] = (jnp.where(lane < hd, a0, a1) / den).astype(o_ref.dtype)
        l0 = jnp.sum(jnp.where(lane == hd, a0, 0.0), axis=1, keepdims=True)
        l1 = jnp.sum(jnp.where(lane == 0, a1, 0.0), axis=1, keepdims=True)
        lse_ref[0] = m_s[0] + jnp.log(l0)
        lse_ref[1] = m_s[1] + jnp.log(l1)
        _ride(rider, 2, step == steps - 1, riding)

    qspec = pl.BlockSpec((tq, LANES), lambda b, p, i: (b * nq + i, p))
    kspec = pl.BlockSpec((seq, LANES), lambda b, p, i: (b, p))
    colspec = pl.BlockSpec((None, 2, tq, 1), lambda b, p, i: (b * npair + p, 0, i, 0))
    rowspec = pl.BlockSpec((None, 2, 1, seq), lambda b, p, i: (b * npair + p, 0, 0, 0))
    sem = ("arbitrary",) * 3 if rider else ("parallel",) * 3
    return pl.pallas_call(
        body, name=name, grid=(nb, npair, nq),
        in_specs=[qspec, kspec, kspec, rowspec] + [HBM] * len(r_in),
        out_specs=[qspec, colspec] + [HBM] * len(r_out),
        out_shape=[jax.ShapeDtypeStruct((t, FOX_WIDTH), BF), jax.ShapeDtypeStruct((nb * npair, 2, seq, 1), F32)]
        + r_out,
        scratch_shapes=[pltpu.VMEM((2, tq, 1), F32), pltpu.VMEM((2, tq, LANES), F32)] + r_sem,
        compiler_params=_params(sem, has_side_effects=rider is not None),
    )(q, k, v, ck, *r_in)


def _fox_attn_bwd(q, k, v, o, do, lse, ck, nb, seq, name, rider=None):
    t = q.shape[0]
    tq = _tile(seq, 512)
    nq = seq // tq
    npair = FOX_HEADS // 2
    hd = FOX_HEAD_DIM
    r_in, r_out, r_sem = _rider_parts(rider)
    steps = nb * npair * nq

    def body(q_ref, k_ref, v_ref, o_ref, do_ref, lse_ref, ck_ref, *rest):
        rin, (dqa_ref, dqb_ref, dk_ref, dv_ref, dkx_ref) = rest[:len(r_in)], rest[len(r_in):len(r_in) + 5]
        rout = rest[len(r_in) + 5:len(r_in) + 5 + len(r_out)]
        dk_s, dv_s = rest[len(r_in) + 5 + len(r_out):len(r_in) + 7 + len(r_out)]
        riding = (rin, rout, rest[len(r_in) + 7 + len(r_out):])
        step = (pl.program_id(0) * npair + pl.program_id(1)) * nq + pl.program_id(2)
        _ride(rider, 0, step == 0, riding)
        _ride(rider, 1, step == steps // 2, riding)
        kj = pl.program_id(2)
        lane = lax.broadcasted_iota(jnp.int32, (tq, LANES), 1)

        @pl.when(kj == 0)
        def _():
            dqa_ref[...] = jnp.zeros_like(dqa_ref)
            dqb_ref[...] = jnp.zeros_like(dqb_ref)

        dk_s[...] = jnp.zeros_like(dk_s)
        dv_s[...] = jnp.zeros_like(dv_s)
        kv = k_ref[...]
        vv = v_ref[...]

        def tile(qi, on_diagonal):
            rows = pl.ds(pl.multiple_of(qi * tq, tq), tq)
            qv = q_ref[rows, :]
            dov = do_ref[rows, :]
            prod = dov.astype(F32) * o_ref[rows, :].astype(F32)
            if on_diagonal:
                causal = (lax.broadcasted_iota(jnp.int32, (tq, tq), 0)
                          >= lax.broadcasted_iota(jnp.int32, (tq, tq), 1))
            for hh, dq_ref in ((0, dqa_ref), (1, dqb_ref)):
                hm = (lane >= hd) if hh else (lane < hd)
                zero = jnp.zeros_like(qv)
                one = jnp.ones_like(qv)
                doh = jnp.where(hm, dov, zero)
                delta = jnp.sum(jnp.where(hm, prod, 0.0), axis=-1, keepdims=True)
                s = _dot_nt(jnp.where(hm, qv, zero), kv) - ck_ref[hh]
                if on_diagonal:
                    s = jnp.where(causal, s, NEG)
                p = jnp.exp(s - lse_ref[hh, rows, :])
                dp = _dot_nt(doh, vv)
                dsb = (p * (dp - delta)).astype(BF)
                dv_s[...] += _dot_tn(p.astype(BF), doh)
                dk_s[hh] += _dot_tn(dsb, jnp.where(hm, qv, one))
                dq_ref[rows, :] += _dot(dsb, jnp.where(hm, kv, one))

        def off_diagonal(qi, carry):
            tile(qi, False)
            return carry

        tile(kj, True)
        lax.fori_loop(kj + 1, nq, off_diagonal, 0)
        dk_ref[...] = jnp.where(lane < hd, dk_s[0], dk_s[1])
        dkx_ref[...] = jnp.where(lane < hd, dk_s[1], dk_s[0])
        dv_ref[...] = dv_s[...]
        _ride(rider, 2, step == steps - 1, riding)

    kspec = pl.BlockSpec((tq, LANES), lambda b, p, j: (b * nq + j, p))
    full_q = pl.BlockSpec((seq, LANES), lambda b, p, j: (b, p))
    colspec = pl.BlockSpec((None, 2, seq, 1), lambda b, p, j: (b * npair + p, 0, 0, 0))
    rowspec = pl.BlockSpec((None, 2, 1, tq), lambda b, p, j: (b * npair + p, 0, 0, j))
    sem = ("arbitrary",) * 3 if rider else ("parallel", "parallel", "arbitrary")
    return pl.pallas_call(
        body, name=name, grid=(nb, npair, nq),
        in_specs=[full_q, kspec, kspec, full_q, full_q, colspec, rowspec] + [HBM] * len(r_in),
        out_specs=[full_q, full_q, kspec, kspec, kspec] + [HBM] * len(r_out),
        out_shape=[jax.ShapeDtypeStruct((t, FOX_WIDTH), F32)] * 5 + r_out,
        scratch_shapes=[pltpu.VMEM((2, tq, LANES), F32), pltpu.VMEM((tq, LANES), F32)] + r_sem,
        compiler_params=_params(sem, has_side_effects=rider is not None),
    )(q, k, v, o, do, lse, ck, *r_in)


GDN_QKV = 3 * GDN_WIDTH
GDN_COL = 3 * FOX_WIDTH
GG_COL = GDN_COL + GDN_QKV
A_LANE = FOX_HEADS
B_LANE = FOX_HEADS + GDN_HEADS
HALO = 8


def _gate_lanes(ts):
    lane = lax.broadcasted_iota(jnp.int32, (ts, LANES), 1)
    return (lane >= A_LANE) & (lane < B_LANE), (lane >= B_LANE) & (lane < B_LANE + GDN_HEADS)


def _chunk_tri(ts, upper):
    r = lax.broadcasted_iota(jnp.int32, (ts, ts), 0)
    c = lax.broadcasted_iota(jnp.int32, (ts, ts), 1)
    same = (r // CHUNK) == (c // CHUNK)
    return (same & ((c >= r) if upper else (r >= c))).astype(F32)


def _conv_silu_l2(xp_ref, w, ts):
    c = w[0:1, :] * xp_ref[pl.ds(HALO - 3, ts), :]
    for kk in range(1, CONV_WIDTH):
        c = c + w[kk:kk + 1, :] * xp_ref[pl.ds(HALO - 3 + kk, ts), :]
    return c, c * _sigmoid(c)


def _gdn_prep(proj, conv_w, a_pad, dt_pad, seq, name):
    t = proj.shape[0]
    ts = _tile(seq, 256)
    tpe = seq // ts
    qscale = GDN_HEAD_DIM ** -0.5

    def body(x_ref, gt_ref, w_ref, a_ref, dt_ref, qo_ref, ko_ref, vo_ref, go_ref, xp):
        i = pl.program_id(0)
        tail = xp[pl.ds(ts, HALO), :]
        xp[pl.ds(0, HALO), :] = jnp.where(i % tpe == 0, jnp.zeros_like(tail), tail)
        xp[pl.ds(HALO, ts), :] = x_ref[...]
        _, s = _conv_silu_l2(xp, w_ref[...], ts)
        for h in range(GDN_HEADS):
            for base, ref, sc in ((0, qo_ref, qscale), (GDN_WIDTH, ko_ref, 1.0)):
                xh = s[:, base + h * LANES: base + (h + 1) * LANES]
                r = lax.rsqrt(jnp.sum(xh * xh, axis=-1, keepdims=True) + EPS)
                ref[:, h * LANES:(h + 1) * LANES] = (xh * (r * sc)).astype(BF)
        vo_ref[...] = s[:, 2 * GDN_WIDTH:].astype(BF)
        gate = gt_ref[...]
        g_raw = -jnp.exp(a_ref[...]) * _softplus(gate + dt_ref[...])
        gc = _mask_dot(_chunk_tri(ts, False), g_raw)
        is_a, is_b = _gate_lanes(ts)
        go_ref[...] = jnp.where(is_a, gc, jnp.where(is_b, _sigmoid(gate), 0.0))

    out = pl.BlockSpec((ts, GDN_WIDTH), lambda i: (i, 0))
    lanes = pl.BlockSpec((ts, LANES), lambda i: (i, 0))
    return pl.pallas_call(
        body, name=name, grid=(t // ts,),
        in_specs=[pl.BlockSpec((ts, GDN_QKV), lambda i: (i, GDN_COL // GDN_QKV)),
                  pl.BlockSpec((ts, LANES), lambda i: (i, GATE_COL // LANES)),
                  _full((CONV_WIDTH, GDN_QKV)), _full((1, LANES)), _full((1, LANES))],
        out_specs=[out, out, out, lanes],
        out_shape=[jax.ShapeDtypeStruct((t, GDN_WIDTH), BF)] * 3 + [jax.ShapeDtypeStruct((t, LANES), F32)],
        scratch_shapes=[pltpu.VMEM((ts + HALO, GDN_QKV), F32)],
        compiler_params=_params(("arbitrary",)),
    )(proj, proj, conv_w, a_pad, dt_pad)


def _gdn_prep_bwd(proj, dq, dk, dv, dgates, dff, conv_w, a_pad, dt_pad, seq, name):
    t = proj.shape[0]
    ts = _tile(seq, 256)
    tpe = seq // ts
    nt = t // ts
    qscale = GDN_HEAD_DIM ** -0.5
    hb = ts // HALO

    def body(x_ref, halo_ref, gt_ref, dq_ref, dk_ref, dv_ref, dgt_ref, dff_ref, w_ref, a_ref, dt_ref,
             dx_ref, dgo_ref, dw_ref, da_ref, ddt_ref, xp, dcp, carry):
        i = pl.program_id(0)
        ti = nt - 1 - i

        @pl.when(i == 0)
        def _():
            dw_ref[...] = jnp.zeros_like(dw_ref)
            da_ref[...] = jnp.zeros_like(da_ref)
            ddt_ref[...] = jnp.zeros_like(ddt_ref)

        halo = halo_ref[...]
        xp[pl.ds(0, HALO), :] = jnp.where(ti % tpe == 0, jnp.zeros_like(halo), halo)
        xp[pl.ds(HALO, ts), :] = x_ref[...]
        w = w_ref[...]
        c, s = _conv_silu_l2(xp, w, ts)
        for h in range(GDN_HEADS):
            for base, ref, sc in ((0, dq_ref, qscale), (GDN_WIDTH, dk_ref, 1.0)):
                lo = base + h * LANES
                xh = s[:, lo:lo + LANES]
                r = lax.rsqrt(jnp.sum(xh * xh, axis=-1, keepdims=True) + EPS)
                y = xh * r
                dy = ref[:, h * LANES:(h + 1) * LANES] * sc
                dcp[pl.ds(0, ts), lo:lo + LANES] = r * (dy - y * jnp.sum(dy * y, axis=-1, keepdims=True))
        dcp[pl.ds(0, ts), 2 * GDN_WIDTH:] = dv_ref[...]
        sg = _sigmoid(c)
        dc = dcp[pl.ds(0, ts), :] * (sg * (1.0 + c * (1.0 - sg)))
        dcp[pl.ds(0, ts), :] = dc
        nxt = carry[...]
        dcp[pl.ds(ts, HALO), :] = jnp.where(ti % tpe == tpe - 1, jnp.zeros_like(nxt), nxt)
        carry[...] = dc[0:HALO, :]
        dx = w[CONV_WIDTH - 1:CONV_WIDTH, :] * dc
        for kk in range(CONV_WIDTH - 1):
            dx = dx + w[kk:kk + 1, :] * dcp[pl.ds(CONV_WIDTH - 1 - kk, ts), :]
        dx_ref[...] = dx.astype(BF)
        for kk in range(CONV_WIDTH):
            dw_ref[kk:kk + 1, :] += jnp.sum(dc * xp[pl.ds(HALO - 3 + kk, ts), :], axis=0, keepdims=True)
        gate = gt_ref[...]
        dgt = dgt_ref[...]
        is_a, is_b = _gate_lanes(ts)
        dg_raw = _mask_dot(_chunk_tri(ts, True), jnp.where(is_a, dgt, 0.0))
        z = gate + dt_ref[...]
        na = -jnp.exp(a_ref[...])
        dga = dg_raw * na * _sigmoid(z)
        beta = _sigmoid(gate)
        dgb = jnp.where(is_b, dgt * beta * (1.0 - beta), 0.0)
        dgo_ref[...] = (dff_ref[...] + dga + dgb).astype(BF)
        ddt_ref[...] += jnp.sum(dga, axis=0, keepdims=True)
        da_ref[...] += jnp.sum(dg_raw * na * _softplus(z), axis=0, keepdims=True)

    rev = lambda wd, j: pl.BlockSpec((ts, wd), lambda i: (nt - 1 - i, j))
    halo_spec = pl.BlockSpec((HALO, GDN_QKV), lambda i: (jnp.maximum((nt - 1 - i) * hb - 1, 0), GDN_COL // GDN_QKV))
    return pl.pallas_call(
        body, name=name, grid=(nt,),
        in_specs=[rev(GDN_QKV, GDN_COL // GDN_QKV), halo_spec, rev(LANES, GATE_COL // LANES),
                  rev(GDN_WIDTH, 0), rev(GDN_WIDTH, 0), rev(GDN_WIDTH, 0), rev(LANES, 0), rev(LANES, 0),
                  _full((CONV_WIDTH, GDN_QKV)), _full((1, LANES)), _full((1, LANES))],
        out_specs=[rev(GDN_QKV, 0), rev(LANES, 0), _full((CONV_WIDTH, GDN_QKV)), _full((1, LANES)),
                   _full((1, LANES))],
        out_shape=[jax.ShapeDtypeStruct((t, GDN_QKV), BF), jax.ShapeDtypeStruct((t, LANES), BF),
                   jax.ShapeDtypeStruct((CONV_WIDTH, GDN_QKV), F32), jax.ShapeDtypeStruct((1, LANES), F32),
                   jax.ShapeDtypeStruct((1, LANES), F32)],
        scratch_shapes=[pltpu.VMEM((ts + HALO, GDN_QKV), F32), pltpu.VMEM((ts + HALO, GDN_QKV), F32),
                        pltpu.VMEM((HALO, GDN_QKV), F32)],
        compiler_params=_params(("arbitrary",)),
    )(proj, proj, proj, dq, dk, dv, dgates, dff, conv_w, a_pad, dt_pad)


PAIR = 2 * CHUNK


def _split_bf16(a):
    hi = a.astype(BF)
    return hi, (a - hi.astype(F32)).astype(BF)


def _dot3(a, b, dims=(((1,), (0,)), ((), ()))):
    ah, al = _split_bf16(a)
    bh, bl = _split_bf16(b)
    (ca,), (cb,) = dims[0]
    return lax.dot_general(jnp.concatenate([ah, al, ah], axis=ca), jnp.concatenate([bh, bh, bl], axis=cb), dims,
                           preferred_element_type=F32)


def _inv_unit_lower(a):
    r = lax.broadcasted_iota(jnp.int32, (PAIR, PAIR), 0)
    c = lax.broadcasted_iota(jnp.int32, (PAIR, PAIR), 1)
    tm = (r == c).astype(F32) - a
    pw = _dot3(a, a)
    for _ in range(4):
        x = _dot3(jnp.concatenate([tm, pw], axis=0), pw)
        tm = tm + x[:PAIR]
        pw = x[PAIR:]
    return tm + _dot3(tm, pw)


def _gdn_pair_local(q, k, v, gc, gr, b):
    r = lax.broadcasted_iota(jnp.int32, (PAIR, PAIR), 0)
    c = lax.broadcasted_iota(jnp.int32, (PAIR, PAIR), 1)
    same = (r // CHUNK) == (c // CHUNK)
    incl = same & (r >= c)
    strict = same & (r > c)
    dm = jnp.exp(jnp.where(incl, gc - gr, NEG))
    e = jnp.exp(gc)
    kb = k * b
    vb = v * b
    kbe = kb * e
    kq = _dot_nt(jnp.concatenate([kb, q], axis=0).astype(BF), k.astype(BF))
    amat = jnp.where(strict, kq[:PAIR] * dm, 0.0)
    pmat = jnp.where(incl, kq[PAIR:] * dm, 0.0)
    lane = lax.broadcasted_iota(jnp.int32, (1, PAIR), 1)
    gl_a = jnp.sum(jnp.where(lane == CHUNK - 1, gr, 0.0), axis=1, keepdims=True)
    gl_b = jnp.sum(jnp.where(lane == PAIR - 1, gr, 0.0), axis=1, keepdims=True)
    ridx = lax.broadcasted_iota(jnp.int32, (PAIR, 1), 0)
    edec = jnp.exp(jnp.where(ridx < CHUNK, gl_a, gl_b) - gc)
    return dict(dm=dm, e=e, kb=kb, vb=vb, kbe=kbe, amat=amat, pmat=pmat, gl_a=gl_a, gl_b=gl_b, edec=edec,
                kd=k * edec, qd=q * e, incl=incl, strict=strict, ridx=ridx)


def _gdn_pair_states(loc, tb, s_a):
    uw = _dot(tb, jnp.concatenate([loc["vb"], loc["kbe"]], axis=1).astype(BF))
    u, w = uw[:, :LANES], uw[:, LANES:]
    qd, kd, c = loc["qd"], loc["kd"], CHUNK
    xa = _dot(jnp.concatenate([qd[:c], w[:c]], axis=0).astype(BF), s_a.astype(BF))
    vn_a = u[:c] - xa[c:]
    s_b = s_a * jnp.exp(loc["gl_a"]) + _dot_tn(kd[:c].astype(BF), vn_a.astype(BF))
    xb = _dot(jnp.concatenate([qd[c:], w[c:]], axis=0).astype(BF), s_b.astype(BF))
    vn_b = u[c:] - xb[c:]
    s_c = s_b * jnp.exp(loc["gl_b"]) + _dot_tn(kd[c:].astype(BF), vn_b.astype(BF))
    vn = jnp.concatenate([vn_a, vn_b], axis=0)
    o = jnp.concatenate([xa[:c], xb[:c]], axis=0) + _dot(loc["pmat"].astype(BF), vn.astype(BF))
    return w, vn, o, s_b, s_c


GDN_SEG = 1024
HEADS_PER_STEP = 4
HEAD_GROUPS = GDN_HEADS // HEADS_PER_STEP


def _gdn_specs(nb, seq, reverse):
    n = seq // CHUNK
    seg = _tile(seq, GDN_SEG)
    nseg = seq // seg
    sp = seg // PAIR
    w2 = HEADS_PER_STEP * LANES
    at = (lambda s: nseg - 1 - s) if reverse else (lambda s: s)
    blk = pl.BlockSpec((seg, w2), lambda b, hp, s: (b * nseg + at(s), hp))
    gg = pl.BlockSpec((seg, w2), lambda b, hp, s: (b * nseg + at(s), GG_COL // w2 + hp))
    gates = pl.BlockSpec((seg, LANES), lambda b, hp, s: (b * nseg + at(s), 0))
    grp = lambda b, hp: b * HEAD_GROUPS + hp
    rowb = pl.BlockSpec((None, HEADS_PER_STEP, sp, HALO, PAIR), lambda b, hp, s: (grp(b, hp), 0, at(s), 0, 0))
    per_pair = pl.BlockSpec((None, HEADS_PER_STEP, sp, PAIR, PAIR), lambda b, hp, s: (grp(b, hp), 0, at(s), 0, 0))
    dgates = pl.BlockSpec((None, seg, LANES), lambda b, hp, s: (grp(b, hp), at(s), 0))
    return n, seg, nseg, sp, blk, gg, gates, rowb, per_pair, dgates


def _head_column(gt, lane, index):
    return jnp.sum(jnp.where(lane == index, gt, 0.0), axis=1, keepdims=True)


def _gdn_head_inputs(qkv_refs, gt, gr_ref, rows, pi, hp, lane):
    per_head = []
    for hh in range(HEADS_PER_STEP):
        head = HEADS_PER_STEP * hp + hh
        cols = slice(hh * LANES, (hh + 1) * LANES)
        per_head.append([r[rows, cols].astype(F32) for r in qkv_refs]
                        + [_head_column(gt, lane, A_LANE + head), gr_ref[hh, pi][0:1, :],
                           _head_column(gt, lane, B_LANE + head)])
    return [jnp.stack(xs) for xs in zip(*per_head)]


def _gdn_pair_fwd(qv, kv, vv, gcv, gr, bv, s_a):
    loc = _gdn_pair_local(qv, kv, vv, gcv, gr, bv)
    tf = _inv_unit_lower(loc["amat"])
    _, _, o, _, s_c = _gdn_pair_states(loc, tf.astype(BF), s_a)
    return tf, o, s_c


def _gdn_fwd(q, k, v, proj, gates, grow, wn, nb, seq, name):
    t = q.shape[0]
    n, seg, nseg, sp, blk, gg, gates_spec, rowb, per_pair, _ = _gdn_specs(nb, seq, False)

    def body(q_ref, k_ref, v_ref, gg_ref, gt_ref, gr_ref, wn_ref, y_ref, tn_ref, sn_ref, s_ref):
        hp = pl.program_id(1)

        @pl.when(pl.program_id(2) == 0)
        def _():
            s_ref[...] = jnp.zeros_like(s_ref)

        wnv = wn_ref[...]
        lane = lax.broadcasted_iota(jnp.int32, (PAIR, LANES), 1)

        def step(pi, carry):
            rows = pl.ds(pl.multiple_of(pi * PAIR, PAIR), PAIR)
            gt = gt_ref[rows, :]
            ins = _gdn_head_inputs((q_ref, k_ref, v_ref), gt, gr_ref, rows, pi, hp, lane)
            s_a = s_ref[...]
            tf, o, s_c = jax.vmap(_gdn_pair_fwd)(*ins, s_a)
            s_ref[...] = s_c
            for hh in range(HEADS_PER_STEP):
                cols = slice(hh * LANES, (hh + 1) * LANES)
                tn_ref[hh, pi] = tf[hh]
                sn_ref[hh, pi] = s_a[hh]
                g = gg_ref[rows, cols]
                oh = o[hh]
                rstd = lax.rsqrt(jnp.mean(oh * oh, axis=-1, keepdims=True) + EPS)
                y_ref[rows, cols] = (oh * rstd * wnv * (g * _sigmoid(g))).astype(BF)
            return carry

        lax.fori_loop(0, sp, step, 0)

    saved = jax.ShapeDtypeStruct((nb * HEAD_GROUPS, HEADS_PER_STEP, n // 2, PAIR, PAIR), F32)
    return pl.pallas_call(
        body, name=name, grid=(nb, GDN_HEADS // HEADS_PER_STEP, nseg),
        in_specs=[blk, blk, blk, gg, gates_spec, rowb, _full((1, LANES))],
        out_specs=[blk, per_pair, per_pair],
        out_shape=[jax.ShapeDtypeStruct((t, GDN_WIDTH), BF), saved, saved],
        scratch_shapes=[pltpu.VMEM((HEADS_PER_STEP, GDN_HEAD_DIM, GDN_HEAD_DIM), F32)],
        compiler_params=_params(("parallel", "parallel", "arbitrary")),
    )(q, k, v, proj, gates, grow, wn)


def _gdn_pair_bwd(qv, kv, vv, gcv, gr, bv, tf, s_a, dsp, g, dyv, wnv):
    c = CHUNK
    loc = _gdn_pair_local(qv, kv, vv, gcv, gr, bv)
    tm = tf.astype(BF)
    kb, vb, kbe, e, dm = loc["kb"], loc["vb"], loc["kbe"], loc["e"], loc["dm"]
    kd, qd, pmat, amat = loc["kd"], loc["qd"], loc["pmat"], loc["amat"]
    w, vn, o, s_b, _ = _gdn_pair_states(loc, tm, s_a)
    sg = _sigmoid(g)
    silu = g * sg
    rstd = lax.rsqrt(jnp.mean(o * o, axis=-1, keepdims=True) + EPS)
    xhat = o * rstd
    dwn = jnp.sum(dyv * xhat * silu, axis=0, keepdims=True)
    dgg = dyv * xhat * wnv * (sg * (1.0 + g * (1.0 - sg)))
    dxhat = dyv * wnv * silu
    do = rstd * (dxhat - xhat * jnp.mean(dxhat * xhat, axis=-1, keepdims=True))
    dob = do.astype(BF)
    tot = lambda x: jnp.sum(jnp.sum(x, axis=1, keepdims=True), axis=0, keepdims=True)
    rsum = lambda x: jnp.sum(x, axis=1, keepdims=True)
    cat = lambda xs, ax=0: jnp.concatenate(xs, axis=ax)
    wb = w.astype(BF)
    qdb = qd.astype(BF)
    kdb = kd.astype(BF)
    vnb = vn.astype(BF)
    egl_a = jnp.exp(loc["gl_a"])
    egl_b = jnp.exp(loc["gl_b"])
    ptdo = _dot_tn(pmat.astype(BF), dob)
    dspb = dsp.astype(BF)
    dvn_b = ptdo[c:] + _dot(kdb[c:], dspb)
    dkd_b = _dot_nt(vnb[c:], dspb)
    dgl_b = egl_b * tot(s_b * dsp) + tot(dkd_b * kd[c:])
    dsm = egl_b * dsp + _dot_tn(cat([qdb[c:], -wb[c:]]), cat([dob[c:], dvn_b.astype(BF)]))
    dsmb = dsm.astype(BF)
    dvn_a = ptdo[:c] + _dot(kdb[:c], dsmb)
    dkd_a = _dot_nt(vnb[:c], dsmb)
    dgl_a = egl_a * tot(s_a * dsm) + tot(dkd_a * kd[:c])
    ds_new = egl_a * dsm + _dot_tn(cat([qdb[:c], -wb[:c]]), cat([dob[:c], dvn_a.astype(BF)]))
    ya = _dot_nt(cat([dob[:c], dvn_a.astype(BF)]), s_a.astype(BF))
    yb = _dot_nt(cat([dob[c:], dvn_b.astype(BF)]), s_b.astype(BF))
    dqd = cat([ya[:c], yb[:c]])
    dw = -cat([ya[c:], yb[c:]])
    dvn = cat([dvn_a, dvn_b])
    dkd = cat([dkd_a, dkd_b])
    dq = dqd * e
    dgc = rsum(dqd * qd) - rsum(dkd * kd)
    dk = dkd * loc["edec"]
    dpm = jnp.where(loc["incl"], _dot_nt(dob, vnb), 0.0)
    duw = cat([dvn, dw], 1).astype(BF)
    dt = _dot_nt(duw, cat([vb, kbe], 1).astype(BF))
    tt = _dot_tn(tm, duw)
    dvb, dkbe = tt[:, :LANES], tt[:, LANES:]
    tn_dims = (((0,), (0,)), ((), ()))
    nt_dims = (((1,), (1,)), ((), ()))
    da = jnp.where(loc["strict"], -_dot3(_dot3(tf, dt, tn_dims), tf, nt_dims), 0.0)
    st = cat([da * dm, dpm * dm]).astype(BF)
    z = _dot(st, kv.astype(BF))
    dkb = z[:PAIR] + dkbe * e
    dq = dq + z[PAIR:]
    dk = dk + _dot_tn(st, cat([kb, qv]).astype(BF))
    gmat = dpm * pmat + da * amat
    dgc = dgc + rsum(dkbe * kbe) + rsum(gmat)
    ridx = loc["ridx"]
    dgc = dgc + jnp.where(ridx == c - 1, dgl_a, 0.0) + jnp.where(ridx == PAIR - 1, dgl_b, 0.0)
    dgc_row = jnp.sum(gmat, axis=0, keepdims=True)
    db = rsum(dvb * vv) + rsum(dkb * kv)
    return dq, dk + dkb * bv, dvb * bv, dgg, dgc, dgc_row, db, dwn, ds_new


def _gdn_bwd(q, k, v, proj, gates, grow, wn, tinv_all, states_all, dy, nb, seq, name):
    t = q.shape[0]
    n, seg, nseg, sp, blk, gg, gates_spec, rowb, per_pair, dgates = _gdn_specs(nb, seq, True)
    dh = GDN_HEAD_DIM

    def body(q_ref, k_ref, v_ref, gg_ref, gt_ref, gr_ref, wn_ref, tn_ref, sn_ref, dy_ref,
             dq_ref, dk_ref, dv_ref, dgg_ref, dgt_ref, dwn_ref, ds_ref):
        hp = pl.program_id(1)

        @pl.when((pl.program_id(0) == 0) & (hp == 0) & (pl.program_id(2) == 0))
        def _():
            dwn_ref[...] = jnp.zeros_like(dwn_ref)

        @pl.when(pl.program_id(2) == 0)
        def _():
            ds_ref[...] = jnp.zeros_like(ds_ref)

        wnv = wn_ref[...]
        lane = lax.broadcasted_iota(jnp.int32, (PAIR, LANES), 1)

        def bwd_step(j, carry):
            pi = sp - 1 - j
            rows = pl.ds(pl.multiple_of(pi * PAIR, PAIR), PAIR)
            gt = gt_ref[rows, :]
            ins = _gdn_head_inputs((q_ref, k_ref, v_ref), gt, gr_ref, rows, pi, hp, lane)
            halves = [slice(hh * LANES, (hh + 1) * LANES) for hh in range(HEADS_PER_STEP)]
            saved = [jnp.stack([r[hh, pi] for hh in range(HEADS_PER_STEP)]) for r in (tn_ref, sn_ref)]
            g2 = jnp.stack([gg_ref[rows, cols] for cols in halves])
            dy2 = jnp.stack([dy_ref[rows, cols].astype(F32) for cols in halves])
            dq, dk, dv, dgg, dgc, dgc_row, db, dwn, ds_new = jax.vmap(
                _gdn_pair_bwd, in_axes=(0,) * 11 + (None,))(*ins, *saved, ds_ref[...], g2, dy2, wnv)
            ds_ref[...] = ds_new
            dgt = jnp.zeros((PAIR, LANES), F32)
            for hh, cols in enumerate(halves):
                head = HEADS_PER_STEP * hp + hh
                dq_ref[rows, cols] = dq[hh]
                dk_ref[rows, cols] = dk[hh]
                dv_ref[rows, cols] = dv[hh]
                dgg_ref[rows, cols] = dgg[hh].astype(BF)
                dwn_ref[...] += dwn[hh]
                row_as_col = jnp.transpose(jnp.broadcast_to(dgc_row[hh], (PAIR, LANES)))
                dgt = (dgt + jnp.where(lane == A_LANE + head, dgc[hh] - row_as_col, 0.0)
                       + jnp.where(lane == B_LANE + head, db[hh], 0.0))
            dgt_ref[rows, :] = dgt
            return carry

        lax.fori_loop(0, sp, bwd_step, 0)

    return pl.pallas_call(
        body, name=name, grid=(nb, GDN_HEADS // HEADS_PER_STEP, nseg),
        in_specs=[blk, blk, blk, gg, gates_spec, rowb, _full((1, LANES)), per_pair, per_pair, blk],
        out_specs=[blk, blk, blk, blk, dgates, _full((1, LANES))],
        out_shape=[jax.ShapeDtypeStruct((t, GDN_WIDTH), F32)] * 3 + [
            jax.ShapeDtypeStruct((t, GDN_WIDTH), BF),
            jax.ShapeDtypeStruct((nb * HEAD_GROUPS, seq, LANES), F32),
            jax.ShapeDtypeStruct((1, LANES), F32)],
        scratch_shapes=[pltpu.VMEM((HEADS_PER_STEP, dh, dh), F32)],
        compiler_params=_params(("arbitrary", "arbitrary", "arbitrary")),
    )(q, k, v, proj, gates, grow, wn, tinv_all, states_all, dy)


def _mix_to_padded(w):
    pad = jnp.zeros(w.shape[:-1] + (N_PAD - N_IN,), w.dtype)
    return jnp.concatenate([w[..., 0:1536], w[..., 1544:3080], w[..., 3088:3600], w[..., 1536:1544],
                            w[..., 3080:3088], pad], axis=-1)


def _mix_from_padded(g):
    return jnp.concatenate([g[..., 0:1536], g[..., 3584:3592], g[..., 1536:3072], g[..., 3592:3600],
                            g[..., 3072:3584]], axis=-1)


def _pad_lanes(vec, start):
    return jnp.pad(vec[None, :], ((0, 0), (start, LANES - start - vec.shape[0])))


def _heads_to_rows(block, lane0, nheads, nb, seq):
    return block[:, lane0:lane0 + nheads].reshape(nb, seq, nheads).transpose(0, 2, 1).reshape(nb * nheads, seq)


def _mixer_small(p, l):
    wq_t = jnp.tile(p["fox_q_norm"][l], FOX_HEADS)[None, :]
    wk_t = jnp.tile(p["fox_k_norm"][l], FOX_HEADS)[None, :]
    bias = _pad_lanes(p["fox_f_bias"][l], 0)
    a_pad = _pad_lanes(p["gdn_a_log"][l], A_LANE)
    dt_pad = _pad_lanes(p["gdn_dt_bias"][l], A_LANE)
    wn = p["gdn_out_norm"][l][None, :]
    return wq_t, wk_t, bias, a_pad, dt_pad, wn


def _layer_fwd(x, p, l, nb, seq, rider=None):
    npair = FOX_HEADS // 2
    n = seq // CHUNK
    wq_t, wk_t, bias, a_pad, dt_pad, wn = _mixer_small(p, l)
    x1, h1 = _ffn_fwd(x, p["ffn1_norm"][l][None, :], p["ffn1_w_in"][l], p["ffn1_w_out"][l], f"ffn1_fwd_{l}")
    proj = _norm_matmul(x1, p["mix_norm"][l][None, :], p["w_mix"][l], f"mix_in_{l}")
    fq, fk, fv, cum = _fox_prep(proj, wq_t, wk_t, bias, seq, f"fox_prep_{l}")
    c8 = _heads_to_rows(cum, 0, FOX_HEADS, nb, seq)
    ck = c8.reshape(nb * npair, 2, 1, seq)
    o, lse, *rode = _fox_attn(fq, fk, fv, ck, nb, seq, f"fox_attn_{l}", rider)
    gq, gk, gv, gates = _gdn_prep(proj, p["gdn_conv"][l], a_pad, dt_pad, seq, f"gdn_prep_{l}")
    gc4 = _heads_to_rows(gates, A_LANE, GDN_HEADS, nb, seq)
    grow = jnp.broadcast_to(gc4.reshape(nb * HEAD_GROUPS, HEADS_PER_STEP, n // 2, 1, PAIR),
                            (nb * HEAD_GROUPS, HEADS_PER_STEP, n // 2, HALO, PAIR))
    y, tinv, states = _gdn_fwd(gq, gk, gv, proj, gates, grow, wn, nb, seq, f"gdn_fwd_{l}")
    x2 = _mix_out(x1, o, y, p["w_out"][l], f"mix_out_{l}")
    x3, h2 = _ffn_fwd(x2, p["ffn2_norm"][l][None, :], p["ffn2_w_in"][l], p["ffn2_w_out"][l], f"ffn2_fwd_{l}")
    saved = dict(x=x, h1=h1, x1=x1, proj=proj, fq=fq, fk=fk, fv=fv, ck=ck, o=o, lse=lse,
                 gq=gq, gk=gk, gv=gv, gates=gates, grow=grow, tinv=tinv, states=states, y=y, x2=x2, h2=h2)
    return x3, saved, rode


def _ffn_grads(dy, x, h, gain, win, wout, l, tag):
    t, d = x.shape
    fs = win.shape[2]
    dx, dh, a, hn, dyh, dgain = _ffn_bwd(dy, x, h, gain, win, wout, f"{tag}_bwd_{l}")
    g_in = _wgrad(hn, dh, jax.ShapeDtypeStruct((4, d, fs), BF),
                  pl.BlockSpec((None, d, fs), lambda i, j, k: (j, i, 0)), d, fs, f"{tag}_gw_in_{l}")
    g_out = _wgrad(a, dyh, jax.ShapeDtypeStruct((2 * fs, d), BF),
                   pl.BlockSpec((fs, d), lambda i, j, k: (i, j)), fs, d, f"{tag}_gw_out_{l}")
    return dx, dgain[0], g_in, g_out.reshape(4, fs // 2, d)


def _layer_bwd(dx3, p, l, sv, nb, seq, rider=None):
    npair = FOX_HEADS // 2
    d = dx3.shape[1]
    wq_t, wk_t, bias, a_pad, dt_pad, wn = _mixer_small(p, l)
    g = {}
    dx2, g["ffn2_norm"], g["ffn2_w_in"], g["ffn2_w_out"] = _ffn_grads(
        dx3, sv["x2"], sv["h2"], p["ffn2_norm"][l][None, :], p["ffn2_w_in"][l], p["ffn2_w_out"][l], l, "ffn2")
    dyf, dyg, dxb = _mix_out_bwd(dx2, p["w_out"][l], f"mix_out_bwd_{l}")
    half = lambda a, nm: _wgrad(a, dxb, jax.ShapeDtypeStruct((FOX_WIDTH, d), BF),
                                pl.BlockSpec((FOX_WIDTH, d), lambda i, j, k: (i, j)), FOX_WIDTH, d, nm)
    g["w_out"] = jnp.concatenate([half(sv["o"], f"gw_out_fox_{l}"), half(sv["y"], f"gw_out_gdn_{l}")], axis=0)
    dqa, dqb, dk, dv, dkx, *rode = _fox_attn_bwd(sv["fq"], sv["fk"], sv["fv"], sv["o"], dyf, sv["lse"], sv["ck"],
                                                 nb, seq, f"fox_attn_bwd_{l}", rider)

    def sums(first_head, second_head):
        a = first_head.reshape(nb * seq, npair, LANES)[:, :, FOX_HEAD_DIM]
        b = second_head.reshape(nb * seq, npair, LANES)[:, :, 0]
        return jnp.stack([a, b], axis=2).reshape(nb * seq, FOX_HEADS)

    dcum = jnp.pad(sums(dqa, dqb) - sums(dkx, dkx), ((0, 0), (0, LANES - FOX_HEADS)))
    dpf, dff, dwq, dwk, dbias = _fox_prep_bwd(sv["proj"], dqa, dqb, dk, dv, dcum, wq_t, wk_t, bias, seq,
                                              f"fox_prep_bwd_{l}")
    g["fox_q_norm"] = dwq[0, :FOX_HEAD_DIM]
    g["fox_k_norm"] = dwk[0, :FOX_HEAD_DIM]
    g["fox_f_bias"] = dbias[0, :FOX_HEADS]
    dgq, dgk, dgv, dgg, dgt, dwn = _gdn_bwd(
        sv["gq"], sv["gk"], sv["gv"], sv["proj"], sv["gates"], sv["grow"], wn, sv["tinv"], sv["states"],
        dyg, nb, seq, f"gdn_bwd_{l}")
    dgates = jnp.sum(dgt.reshape(nb, HEAD_GROUPS, seq, LANES), axis=1).reshape(nb * seq, LANES)
    dpg, dgate_blk, dconv, da, ddt = _gdn_prep_bwd(sv["proj"], dgq, dgk, dgv, dgates, dff, p["gdn_conv"][l],
                                                   a_pad, dt_pad, seq, f"gdn_prep_bwd_{l}")
    g["gdn_conv"] = dconv
    g["gdn_a_log"] = da[0, A_LANE:B_LANE]
    g["gdn_dt_bias"] = ddt[0, A_LANE:B_LANE]
    g["gdn_out_norm"] = dwn[0]
    dproj = jnp.concatenate([dpf, dpg, dgg, dgate_blk], axis=1)
    dx1, hnm, dgm = _norm_matmul_bwd(dx2, dproj, sv["x1"], p["mix_norm"][l][None, :], p["w_mix"][l],
                                     f"mix_in_bwd_{l}")
    g["mix_norm"] = dgm[0]
    g["w_mix"] = _wgrad(hnm, dproj, jax.ShapeDtypeStruct((d, N_PAD), F32),
                        pl.BlockSpec((d // 2, N_PAD), lambda i, j, k: (i, j)), d // 2, N_PAD, f"gw_mix_{l}")
    dx0, g["ffn1_norm"], g["ffn1_w_in"], g["ffn1_w_out"] = _ffn_grads(
        dx1, sv["x"], sv["h1"], p["ffn1_norm"][l][None, :], p["ffn1_w_in"][l], p["ffn1_w_out"][l], l, "ffn1")
    return dx0, g, rode


def _local_step(x, target, p):
    nb, seq, d = x.shape
    xt = x.reshape(nb * seq, d)
    saved = []
    for l in range(DEPTH):
        xt, sv, _ = _layer_fwd(xt, p, l, nb, seq)
        saved.append(sv)
    loss, dx = _loss_grad(xt, target.reshape(nb * seq, d), "loss")
    grads = [None] * DEPTH
    for l in reversed(range(DEPTH)):
        dx, grads[l], _ = _layer_bwd(dx, p, l, saved[l], nb, seq)
    return loss, dx.reshape(nb, seq, d), grads


N_CHIPS = 4


def _mesh_pos():
    return lax.axis_index("x"), lax.axis_index("y"), lax.axis_index("c")


def _other_chips(x, y):
    return [(1 - x, y), (x, 1 - y), (1 - x, 1 - y)]


def _remote(src, dst, send_sem, recv_sem, to):
    return pltpu.make_async_remote_copy(src_ref=src, dst_ref=dst, send_sem=send_sem, recv_sem=recv_sem,
                                        device_id=to, device_id_type=MESH)


def _hbm_call(body, name, ins, out_shape, scratch):
    return pl.pallas_call(
        body, name=name, out_shape=out_shape, in_specs=[HBM] * len(ins),
        out_specs=jax.tree.map(lambda _: HBM, out_shape), scratch_shapes=scratch,
        compiler_params=pltpu.CompilerParams(has_side_effects=True),
    )(*ins)


def _gather_phases(n, layer):
    def copies(ins, outs, sems):
        send1, recv1, send2, recv2 = sems
        x, y, c = _mesh_pos()
        out, back, fwd = [], [], []
        for i in range(n):
            for j, (px, py) in enumerate(_other_chips(x, y)):
                k = 3 * i + j
                blk = outs[i].at[2 * px + py]
                out.append(_remote(ins[i], outs[i].at[2 * x + y], send1.at[k], recv1.at[k], (px, py, c)))
                back.append(_remote(blk, blk, send1.at[k], recv1.at[k], (px, py, c)))
                fwd.append(_remote(blk, blk, send2.at[k], recv2.at[k], (x, y, 1 - c)))
        return c, out, back, fwd

    def first(ins, outs, sems):
        c, out, _, _ = copies(ins, outs, sems)

        @pl.when(c == layer)
        def _():
            for cp in out:
                cp.start()

    def middle(ins, outs, sems):
        c, _, back, fwd = copies(ins, outs, sems)

        @pl.when(c == layer)
        def _():
            for arrived, onward in zip(back, fwd):
                arrived.wait_recv()
                onward.start()

    def last(ins, outs, sems):
        c, out, _, fwd = copies(ins, outs, sems)

        @pl.when(c == layer)
        def _():
            for cp in out + fwd:
                cp.wait_send()

        @pl.when(c != layer)
        def _():
            for cp in fwd:
                cp.wait_recv()

    return first, middle, last


def _scatter_phases(n, layer):
    def copies(ins, outs, sems):
        send, recv = sems
        x, y, c = _mesh_pos()
        return c, [_remote(ins[i].at[2 * px + py], outs[i].at[j], send.at[3 * i + j], recv.at[3 * i + j], (px, py, c))
                   for i in range(n) for j, (px, py) in enumerate(_other_chips(x, y))]

    def first(ins, outs, sems):
        c, cps = copies(ins, outs, sems)

        @pl.when(c == layer)
        def _():
            for cp in cps:
                cp.start()

    def middle(ins, outs, sems):
        pass

    def last(ins, outs, sems):
        c, cps = copies(ins, outs, sems)

        @pl.when(c == layer)
        def _():
            for cp in cps:
                cp.wait()

    return first, middle, last


def _exchange(blocks, out_shapes, n_sems, phases, name, rider):
    sems = [pltpu.SemaphoreType.DMA((3 * len(blocks),))] * n_sems
    if rider:
        return _Rider(blocks, out_shapes, sems, phases)
    n = len(blocks)

    def body(*refs):
        for phase in phases:
            phase(refs[:n], refs[n:2 * n], refs[2 * n:])

    return _hbm_call(body, name, blocks, out_shapes, sems)


def _gather_layer(blocks, layer, name=None, rider=False):
    outs = [jax.ShapeDtypeStruct((N_CHIPS,) + b.shape, b.dtype) for b in blocks]
    return _exchange(blocks, outs, 4, _gather_phases(len(blocks), layer), name, rider)


def _scatter_layer(sums, layer, name=None, rider=False):
    outs = [jax.ShapeDtypeStruct((3,) + s.shape[1:], s.dtype) for s in sums]
    return _exchange(sums, outs, 2, _scatter_phases(len(sums), layer), name, rider)


def _to_sibling(gs, layer, name):
    n = len(gs)

    def body(*refs):
        ins, outs = refs[:n], refs[n:2 * n]
        send, recv = refs[2 * n:]
        x, y, c = _mesh_pos()
        cps = [_remote(ins[i], outs[i], send.at[i], recv.at[i], (x, y, 1 - c)) for i in range(n)]

        @pl.when(c != layer)
        def _():
            for cp in cps:
                cp.start()
            for cp in cps:
                cp.wait_send()

        @pl.when(c == layer)
        def _():
            for cp in cps:
                cp.wait_recv()

    sem = pltpu.SemaphoreType.DMA((n,))
    return _hbm_call(body, name, gs, [jax.ShapeDtypeStruct(g.shape, g.dtype) for g in gs], [sem, sem])


def _sibling_swap(rs, name):
    n = len(rs)

    def body(*refs):
        ins, outs = refs[:n], refs[n:2 * n]
        send, recv = refs[2 * n:]
        x, y, c = _mesh_pos()
        cps = [_remote(ins[i], outs[i], send.at[i], recv.at[i], (x, y, 1 - c)) for i in range(n)]
        for cp in cps:
            cp.start()
        for cp in cps:
            cp.wait()

    sem = pltpu.SemaphoreType.DMA((n,))
    return _hbm_call(body, name, rs, [jax.ShapeDtypeStruct(r.shape, r.dtype) for r in rs], [sem, sem])


def _small_all_reduce(vec, name):
    r = vec.shape[0]
    ndev = 8

    def body(v_ref, o_ref, buf, send, recv):
        x, y, c = _mesh_pos()
        me = 4 * x + 2 * y + c
        buf[me] = v_ref[...]
        cps = []
        for rel in range(1, ndev):
            px = 1 - x if rel & 4 else x
            py = 1 - y if rel & 2 else y
            pc = 1 - c if rel & 1 else c
            cps.append((_remote(v_ref, buf.at[me], send.at[rel - 1], recv.at[rel - 1], (px, py, pc)),
                        4 * px + 2 * py + pc))
        for cp, _ in cps:
            cp.start()
        for k, (cp, peer) in enumerate(cps):
            slot = buf.at[peer]
            _remote(slot, slot, send.at[k], recv.at[k], (x, y, c)).wait_recv()
        for cp, _ in cps:
            cp.wait_send()
        acc = buf[0]
        for k in range(1, ndev):
            acc = acc + buf[k]
        o_ref[...] = acc

    vm = pl.BlockSpec(memory_space=pltpu.VMEM)
    return pl.pallas_call(
        body, name=name, out_shape=jax.ShapeDtypeStruct(vec.shape, F32), in_specs=[vm], out_specs=vm,
        scratch_shapes=[pltpu.VMEM((ndev, r, LANES), F32), pltpu.SemaphoreType.DMA((ndev - 1,)),
                        pltpu.SemaphoreType.DMA((ndev - 1,))],
        compiler_params=pltpu.CompilerParams(has_side_effects=True),
    )(vec)


def _row_tile(rows, cap=512):
    for t in range(min(rows, cap), 0, -1):
        if rows % t == 0 and (t % 16 == 0 or t == rows):
            return t
    raise ValueError(rows)


def _add_pairs(a, b, name):
    k, r, c = a.shape
    tr = _row_tile(r)

    def body(a_ref, b_ref, o_ref):
        o_ref[...] = (a_ref[...].astype(F32) + b_ref[...].astype(F32)).astype(o_ref.dtype)

    spec = pl.BlockSpec((None, tr, c), lambda i, j: (i, j, 0))
    return pl.pallas_call(body, name=name, grid=(k, r // tr), in_specs=[spec, spec], out_specs=spec,
                          out_shape=jax.ShapeDtypeStruct(a.shape, a.dtype),
                          compiler_params=_params(("parallel", "parallel")))(a, b)


def _final_sum(own, sib, others, name):
    r, c = own.shape
    tr = _row_tile(r)

    def body(a_ref, b_ref, o_ref_in, out_ref):
        acc = a_ref[...].astype(F32) + b_ref[...].astype(F32)
        for k in range(3):
            acc = acc + o_ref_in[k].astype(F32)
        out_ref[...] = acc

    spec = pl.BlockSpec((tr, c), lambda i: (i, 0))
    return pl.pallas_call(body, name=name, grid=(r // tr,),
                          in_specs=[spec, spec, pl.BlockSpec((3, tr, c), lambda i: (0, i, 0))], out_specs=spec,
                          out_shape=jax.ShapeDtypeStruct((r, c), F32),
                          compiler_params=_params(("parallel",)))(own, sib, others)


def _adamw(g, w, m, v, name):
    r, c = g.shape
    tr = _row_tile(r, 256)

    def body(g_ref, w_ref, m_ref, v_ref, d_ref, mo_ref, vo_ref):
        gv = g_ref[...]
        mn = ADAM_B1 * m_ref[...] + (1.0 - ADAM_B1) * gv
        vn = ADAM_B2 * v_ref[...] + (1.0 - ADAM_B2) * (gv * gv)
        m_hat = mn / (1.0 - ADAM_B1 ** ADAM_STEP)
        v_hat = vn / (1.0 - ADAM_B2 ** ADAM_STEP)
        d_ref[...] = -ADAM_LR * (m_hat / (jnp.sqrt(v_hat) + ADAM_EPS) + ADAM_WD * w_ref[...])
        mo_ref[...] = mn
        vo_ref[...] = vn

    spec = pl.BlockSpec((tr, c), lambda i: (i, 0))
    shp = jax.ShapeDtypeStruct((r, c), F32)
    return pl.pallas_call(body, name=name, grid=(r // tr,), in_specs=[spec] * 4, out_specs=[spec] * 3,
                          out_shape=[shp] * 3, compiler_params=_params(("parallel",)))(g, w, m, v)


def _pack(arrays):
    flat = jnp.concatenate([a.reshape(-1).astype(F32) for a in arrays])
    pad = (-flat.shape[0]) % (8 * LANES)
    return jnp.concatenate([flat, jnp.zeros((pad,), F32)]).reshape(-1, LANES)


def _unpack(packed, shapes):
    flat = packed.reshape(-1)
    out, off = [], 0
    for s in shapes:
        size = 1
        for dim in s:
            size *= dim
        out.append(flat[off:off + size].reshape(s))
        off += size
    return out


BIG = ("ffn1_w_in", "ffn1_w_out", "w_in", "w_out", "ffn2_w_in", "ffn2_w_out")
SMALL = ("ffn1_norm", "mix_norm", "fox_q_norm", "fox_k_norm", "fox_f_bias", "gdn_a_log", "gdn_dt_bias",
         "gdn_out_norm", "ffn2_norm", "gdn_conv")
WEIGHTS = ("ffn1_norm", "ffn1_w_in", "ffn1_w_out", "mix_norm", "w_in", "fox_q_norm", "fox_k_norm", "fox_f_bias",
           "gdn_conv", "gdn_a_log", "gdn_dt_bias", "gdn_out_norm", "w_out", "ffn2_norm", "ffn2_w_in", "ffn2_w_out")


def _step(x, target, w, m, v):
    xi, yi, ci = _mesh_pos()
    me = 2 * xi + yi
    depth = DEPTH
    d = x.shape[-1]

    nb, seq, _ = x.shape
    assert depth == 2

    p = {k: w[k] for k in SMALL if k != "gdn_conv"}
    for k in ("ffn1_w_in", "ffn1_w_out", "ffn2_w_in", "ffn2_w_out", "w_mix", "w_out", "gdn_conv"):
        p[k] = [None] * depth

    def shards(l):
        return [w[k][l].astype(BF) for k in BIG] + [w["gdn_conv"][l]]

    def place(l, gathered):
        blocks = dict(zip(BIG + ("gdn_conv",), [lax.dynamic_update_index_in_dim(g, s, me, 0)
                                               for g, s in zip(gathered, shards(l))]))
        for k in ("ffn1_w_in", "ffn1_w_out", "ffn2_w_in", "ffn2_w_out"):
            p[k][l] = blocks[k]
        p["w_mix"][l] = _mix_to_padded(blocks["w_in"].transpose(1, 0, 2).reshape(d, N_IN))
        p["w_out"][l] = blocks["w_out"].reshape(2 * FOX_WIDTH, d)
        p["gdn_conv"][l] = blocks["gdn_conv"].transpose(1, 0, 2).reshape(CONV_WIDTH, -1)

    place(0, _gather_layer(shards(0), 0, "gather_layer0"))
    xt = x.reshape(nb * seq, d)
    xt, saved0, gathered1 = _layer_fwd(xt, p, 0, nb, seq, _gather_layer(shards(1), 1, rider=True))
    place(1, gathered1)
    xt, saved1, _ = _layer_fwd(xt, p, 1, nb, seq)
    loss, dx = _loss_grad(xt, target.reshape(nb * seq, d), "loss")

    def transport(g):
        out = []
        for k in BIG:
            if k == "w_in":
                full = _mix_from_padded(g["w_mix"])
                out.append(full.reshape(d, N_CHIPS, N_IN // N_CHIPS).transpose(1, 0, 2).astype(BF))
            elif k == "w_out":
                out.append(g["w_out"].reshape(N_CHIPS, -1, d))
            else:
                out.append(g[k])
        return out

    def chip_sums(g, l):
        own = transport(g)
        sib = _to_sibling(own, l, f"grad{l}_to_sibling")
        return own, sib, [_add_pairs(a, b, f"grad{l}_chip_sum_{k}") for a, b, k in zip(own, sib, BIG)]

    dx, grads1, _ = _layer_bwd(dx, p, 1, saved1, nb, seq)
    own1, sib1, sums1 = chip_sums(grads1, 1)
    dx, grads0, chips1 = _layer_bwd(dx, p, 0, saved0, nb, seq, _scatter_layer(sums1, 1, rider=True))
    own0, sib0, sums0 = chip_sums(grads0, 0)
    chips0 = _scatter_layer(sums0, 0, "grad0_to_chips")
    grads = [grads0, grads1]
    dx = dx.reshape(nb, seq, d)

    mine = lambda a0, a1: jnp.where(ci == 0, a0, a1)
    at_me = lambda a: lax.dynamic_index_in_dim(a, me, 0, keepdims=False)
    reduced = [_final_sum(mine(at_me(own0[i]), at_me(own1[i])), mine(at_me(sib0[i]), at_me(sib1[i])),
                          mine(chips0[i], chips1[i]), f"grad_final_sum_{k}") for i, k in enumerate(BIG)]
    from_sib_final = _sibling_swap(reduced, "grad_swap_layers")
    full = {k: jnp.stack([jnp.where(ci == 0, a, b), jnp.where(ci == 0, b, a)])
            for k, a, b in zip(BIG, reduced, from_sib_final)}

    out_g, out_d, out_m, out_v = {}, {}, {}, {}
    for k in BIG:
        shp = w[k].shape
        two_d = lambda a: a.reshape(shp[0] * shp[1], shp[2])
        dl, mn, vn = _adamw(two_d(full[k]), two_d(w[k]), two_d(m[k]), two_d(v[k]), f"adamw_{k}")
        out_g[k], out_d[k], out_m[k], out_v[k] = full[k], dl.reshape(shp), mn.reshape(shp), vn.reshape(shp)

    small_local = [jnp.stack([grads[l][k] for l in range(depth)]) for k in SMALL]
    summed = _unpack(_small_all_reduce(_pack(small_local), "small_all_reduce"), [a.shape for a in small_local])
    sg = dict(zip(SMALL, summed))
    cs = w["gdn_conv"].shape[-1]
    sg["gdn_conv"] = lax.dynamic_slice_in_dim(sg["gdn_conv"], me * cs, cs, axis=2)
    shapes = [w[k].shape for k in SMALL]
    packs = [_pack([src[k] for k in SMALL]) for src in (sg, w, m, v)]
    dl, mn, vn = _adamw(*packs, "adamw_small")
    for k, a, b, c2 in zip(SMALL, _unpack(dl, shapes), _unpack(mn, shapes), _unpack(vn, shapes)):
        out_g[k], out_d[k], out_m[k], out_v[k] = sg[k], a, b, c2

    total = lax.psum(loss[0, 0], ("x", "y", "c"))
    return (total, dx, *[out_g[k] for k in WEIGHTS], *[out_d[k] for k in WEIGHTS],
            *[out_m[k] for k in WEIGHTS], *[out_v[k] for k in WEIGHTS])


def kernel(x, ffn1_norm, ffn1_w_in, ffn1_w_out, mix_norm, w_in, fox_q_norm, fox_k_norm, fox_f_bias, gdn_conv, gdn_a_log, gdn_dt_bias, gdn_out_norm, w_out, ffn2_norm, ffn2_w_in, ffn2_w_out, loss_target, m_ffn1_norm, m_ffn1_w_in, m_ffn1_w_out, m_mix_norm, m_w_in, m_fox_q_norm, m_fox_k_norm, m_fox_f_bias, m_gdn_conv, m_gdn_a_log, m_gdn_dt_bias, m_gdn_out_norm, m_w_out, m_ffn2_norm, m_ffn2_w_in, m_ffn2_w_out, v_ffn1_norm, v_ffn1_w_in, v_ffn1_w_out, v_mix_norm, v_w_in, v_fox_q_norm, v_fox_k_norm, v_fox_f_bias, v_gdn_conv, v_gdn_a_log, v_gdn_dt_bias, v_gdn_out_norm, v_w_out, v_ffn2_norm, v_ffn2_w_in, v_ffn2_w_out):
    w = dict(ffn1_norm=ffn1_norm, ffn1_w_in=ffn1_w_in, ffn1_w_out=ffn1_w_out, mix_norm=mix_norm, w_in=w_in,
             fox_q_norm=fox_q_norm, fox_k_norm=fox_k_norm, fox_f_bias=fox_f_bias, gdn_conv=gdn_conv,
             gdn_a_log=gdn_a_log, gdn_dt_bias=gdn_dt_bias, gdn_out_norm=gdn_out_norm, w_out=w_out,
             ffn2_norm=ffn2_norm, ffn2_w_in=ffn2_w_in, ffn2_w_out=ffn2_w_out)
    m = dict(ffn1_norm=m_ffn1_norm, ffn1_w_in=m_ffn1_w_in, ffn1_w_out=m_ffn1_w_out, mix_norm=m_mix_norm, w_in=m_w_in,
             fox_q_norm=m_fox_q_norm, fox_k_norm=m_fox_k_norm, fox_f_bias=m_fox_f_bias, gdn_conv=m_gdn_conv,
             gdn_a_log=m_gdn_a_log, gdn_dt_bias=m_gdn_dt_bias, gdn_out_norm=m_gdn_out_norm, w_out=m_w_out,
             ffn2_norm=m_ffn2_norm, ffn2_w_in=m_ffn2_w_in, ffn2_w_out=m_ffn2_w_out)
    v = dict(ffn1_norm=v_ffn1_norm, ffn1_w_in=v_ffn1_w_in, ffn1_w_out=v_ffn1_w_out, mix_norm=v_mix_norm, w_in=v_w_in,
             fox_q_norm=v_fox_q_norm, fox_k_norm=v_fox_k_norm, fox_f_bias=v_fox_f_bias, gdn_conv=v_gdn_conv,
             gdn_a_log=v_gdn_a_log, gdn_dt_bias=v_gdn_dt_bias, gdn_out_norm=v_gdn_out_norm, w_out=v_w_out,
             ffn2_norm=v_ffn2_norm, ffn2_w_in=v_ffn2_w_in, ffn2_w_out=v_ffn2_w_out)
    return _step(x, loss_target, w, m, v)
```

```python
import jax
import jax.numpy as jnp
from jax import lax
from jax.experimental import pallas as pl
from jax.experimental.pallas import tpu as pltpu

F32 = jnp.float32
BF = jnp.bfloat16
HI = lax.Precision.HIGHEST
MESH = pl.DeviceIdType.MESH

DEPTH = 2
FOX_HEADS = 8
FOX_HEAD_DIM = 64
FOX_WIDTH = 512
GDN_HEADS = 4
GDN_HEAD_DIM = 128
GDN_WIDTH = 512
CONV_WIDTH = 4
CHUNK = 64
EPS = 1e-6
N_IN = 3600
N_PAD = 3712
GATE_COL = 3584
LANES = 128
NEG = -1e30

ADAM_LR = 0.001
ADAM_B1 = 0.9
ADAM_B2 = 0.999
ADAM_EPS = 1e-08
ADAM_WD = 0.01
ADAM_STEP = 10

VMEM_LIMIT = 56 * 1024 * 1024


def _params(sem=None, **kw):
    return pltpu.CompilerParams(dimension_semantics=sem, vmem_limit_bytes=VMEM_LIMIT, **kw)


def _dot(a, b, precision=None):
    return jnp.dot(a, b, preferred_element_type=F32, precision=precision)


def _dot_nt(a, b, precision=None):
    return lax.dot_general(a, b, (((1,), (1,)), ((), ())), preferred_element_type=F32, precision=precision)


def _dot_tn(a, b, precision=None):
    return lax.dot_general(a, b, (((0,), (0,)), ((), ())), preferred_element_type=F32, precision=precision)


def _sigmoid(x):
    return 0.5 * jnp.tanh(0.5 * x) + 0.5


def _softplus(x):
    return jnp.maximum(x, 0.0) + jnp.log(1.0 + jnp.exp(-jnp.abs(x)))


def _log_sigmoid(x):
    return jnp.minimum(x, 0.0) - jnp.log(1.0 + jnp.exp(-jnp.abs(x)))


def _tile(n, t):
    t = min(n, t)
    assert n % t == 0, (n, t)
    return t


def _rms_fwd(x, gain):
    rstd = lax.rsqrt(jnp.mean(x * x, axis=-1, keepdims=True) + EPS)
    xhat = x * rstd
    return xhat * gain, xhat, rstd


def _rms_bwd(dy, xhat, rstd, gain):
    dxhat = dy * gain
    dx = rstd * (dxhat - xhat * jnp.mean(dxhat * xhat, axis=-1, keepdims=True))
    return dx, dy * xhat


def _full(shape):
    nd = len(shape)
    return pl.BlockSpec(shape, lambda *_: (0,) * nd)


HBM = pl.BlockSpec(memory_space=pltpu.HBM)


def _load_ffn_weights(win_hbm, wout_hbm, win_v, wout_v, sem):
    fr = wout_hbm.shape[1]
    copies = [pltpu.make_async_copy(win_hbm.at[s], win_v.at[s], sem.at[s]) for s in range(4)]
    copies += [pltpu.make_async_copy(wout_hbm.at[s], wout_v.at[pl.ds(s * fr, fr)], sem.at[4 + s])
               for s in range(4)]
    for c in copies:
        c.start()
    for c in copies:
        c.wait()


def _ffn_fwd(x, gain, win_g, wout_g, name, rider=None):
    t, d = x.shape
    _, _, fs = win_g.shape
    fr = wout_g.shape[1]
    tm = _tile(t, 256)
    r_in, r_out, r_sem = _rider_parts(rider)
    steps = t // tm

    def body(x_ref, g_ref, win_hbm, wout_hbm, *rest):
        rin, (xo_ref, h_ref) = rest[:len(r_in)], rest[len(r_in):len(r_in) + 2]
        rout = rest[len(r_in) + 2:len(r_in) + 2 + len(r_out)]
        win_v, wout_v, sem = rest[len(r_in) + 2 + len(r_out):len(r_in) + 5 + len(r_out)]
        riding = (rin, rout, rest[len(r_in) + 5 + len(r_out):])
        step = pl.program_id(0)
        _ride(rider, 0, step == 0, riding)
        _ride(rider, 1, step == steps // 2, riding)

        @pl.when(step == 0)
        def _():
            _load_ffn_weights(win_hbm, wout_hbm, win_v, wout_v, sem)

        xv = x_ref[...]
        hn, _, _ = _rms_fwd(xv, g_ref[...])
        hn = hn.astype(BF)
        acc = jnp.zeros((tm, d), F32)
        for s in range(2):
            g = _dot(hn, win_v[s])
            u = _dot(hn, win_v[s + 2])
            h_ref[:, s * fs:(s + 1) * fs] = g.astype(BF)
            h_ref[:, (s + 2) * fs:(s + 3) * fs] = u.astype(BF)
            a = (g * _sigmoid(g) * u).astype(BF)
            acc = acc + _dot(a, wout_v[s * fs:(s + 1) * fs, :])
        xo_ref[...] = xv + 0.5 * acc
        _ride(rider, 2, step == steps - 1, riding)

    return pl.pallas_call(
        body, name=name, grid=(steps,),
        in_specs=[pl.BlockSpec((tm, d), lambda i: (i, 0)), _full((1, d)), HBM, HBM] + [HBM] * len(r_in),
        out_specs=[pl.BlockSpec((tm, d), lambda i: (i, 0)), pl.BlockSpec((tm, 4 * fs), lambda i: (i, 0))]
        + [HBM] * len(r_out),
        out_shape=[jax.ShapeDtypeStruct((t, d), F32), jax.ShapeDtypeStruct((t, 4 * fs), BF)] + r_out,
        scratch_shapes=[pltpu.VMEM((4, d, fs), BF), pltpu.VMEM((4 * fr, d), BF), pltpu.SemaphoreType.DMA((8,))]
        + r_sem,
        compiler_params=_params(("arbitrary",), has_side_effects=rider is not None),
    )(x, gain, win_g, wout_g, *r_in)


def _ffn_bwd(dy, x, h, gain, win_g, wout_g, name, rider=None):
    t, d = x.shape
    _, _, fs = win_g.shape
    fr = wout_g.shape[1]
    tm = _tile(t, 256)
    r_in, r_out, r_sem = _rider_parts(rider)
    steps = t // tm

    def body(dy_ref, x_ref, h_ref, g_ref, win_hbm, wout_hbm, *rest):
        rin, (dx_ref, dh_ref, a_ref, hn_ref, dyh_ref, dg_ref) = rest[:len(r_in)], rest[len(r_in):len(r_in) + 6]
        rout = rest[len(r_in) + 6:len(r_in) + 6 + len(r_out)]
        win_v, wout_v, sem = rest[len(r_in) + 6 + len(r_out):len(r_in) + 9 + len(r_out)]
        riding = (rin, rout, rest[len(r_in) + 9 + len(r_out):])
        step = pl.program_id(0)
        _ride(rider, 0, step == 0, riding)
        _ride(rider, 1, step == steps // 2, riding)

        @pl.when(step == 0)
        def _():
            _load_ffn_weights(win_hbm, wout_hbm, win_v, wout_v, sem)
            dg_ref[...] = jnp.zeros_like(dg_ref)

        dyv = dy_ref[...]
        dyh = (0.5 * dyv).astype(BF)
        dyh_ref[...] = dyh
        dhn = jnp.zeros((tm, d), F32)
        for s in range(2):
            da = _dot_nt(dyh, wout_v[s * fs:(s + 1) * fs, :])
            g = h_ref[:, s * fs:(s + 1) * fs].astype(F32)
            u = h_ref[:, (s + 2) * fs:(s + 3) * fs].astype(F32)
            sg = _sigmoid(g)
            si = g * sg
            a_ref[:, s * fs:(s + 1) * fs] = (si * u).astype(BF)
            dgate = (da * u * (sg * (1.0 + g * (1.0 - sg)))).astype(BF)
            dup = (da * si).astype(BF)
            dh_ref[:, s * fs:(s + 1) * fs] = dgate
            dh_ref[:, (s + 2) * fs:(s + 3) * fs] = dup
            dhn = dhn + _dot_nt(dgate, win_v[s]) + _dot_nt(dup, win_v[s + 2])
        xv = x_ref[...]
        gain_v = g_ref[...]
        hn, xhat, rstd = _rms_fwd(xv, gain_v)
        hn_ref[...] = hn.astype(BF)
        dx, dgr = _rms_bwd(dhn, xhat, rstd, gain_v)
        dx_ref[...] = dyv + dx
        dg_ref[...] += jnp.sum(dgr, axis=0, keepdims=True)
        _ride(rider, 2, step == steps - 1, riding)

    row = lambda w: pl.BlockSpec((tm, w), lambda i: (i, 0))
    return pl.pallas_call(
        body, name=name, grid=(steps,),
        in_specs=[row(d), row(d), row(4 * fs), _full((1, d)), HBM, HBM] + [HBM] * len(r_in),
        out_specs=[row(d), row(4 * fs), row(2 * fs), row(d), row(d), _full((1, d))] + [HBM] * len(r_out),
        out_shape=[jax.ShapeDtypeStruct((t, d), F32), jax.ShapeDtypeStruct((t, 4 * fs), BF),
                   jax.ShapeDtypeStruct((t, 2 * fs), BF), jax.ShapeDtypeStruct((t, d), BF),
                   jax.ShapeDtypeStruct((t, d), BF), jax.ShapeDtypeStruct((1, d), F32)] + r_out,
        scratch_shapes=[pltpu.VMEM((4, d, fs), BF), pltpu.VMEM((4 * fr, d), BF), pltpu.SemaphoreType.DMA((8,))]
        + r_sem,
        compiler_params=_params(("arbitrary",), has_side_effects=rider is not None),
    )(dy, x, h, gain, win_g, wout_g, *r_in)


def _wgrad(a, b, out_shape, out_spec, tm, tn, name, tk=512):
    t, m = a.shape
    _, n = b.shape
    tk = _tile(t, tk)
    nk = t // tk

    def body(a_ref, b_ref, o_ref, acc):
        k = pl.program_id(2)

        @pl.when(k == 0)
        def _():
            acc[...] = jnp.zeros_like(acc)

        acc[...] += _dot_tn(a_ref[...], b_ref[...])

        @pl.when(k == nk - 1)
        def _():
            o_ref[...] = acc[...].astype(o_ref.dtype)

    return pl.pallas_call(
        body, name=name, grid=(m // tm, n // tn, nk),
        in_specs=[pl.BlockSpec((tk, tm), lambda i, j, k: (k, i)), pl.BlockSpec((tk, tn), lambda i, j, k: (k, j))],
        out_specs=out_spec, out_shape=out_shape,
        scratch_shapes=[pltpu.VMEM((tm, tn), F32)],
        compiler_params=_params(("parallel", "parallel", "arbitrary")),
    )(a, b)


def _norm_matmul(x, gain, w, name):
    t, d = x.shape
    n = w.shape[1]
    tm = _tile(t, 256)

    def body(x_ref, g_ref, w_ref, o_ref):
        hn, _, _ = _rms_fwd(x_ref[...], g_ref[...])
        o_ref[...] = _dot(hn.astype(BF), w_ref[...])

    return pl.pallas_call(
        body, name=name, grid=(t // tm,),
        in_specs=[pl.BlockSpec((tm, d), lambda i: (i, 0)), _full((1, d)), _full((d, n))],
        out_specs=pl.BlockSpec((tm, n), lambda i: (i, 0)),
        out_shape=jax.ShapeDtypeStruct((t, n), F32),
        compiler_params=_params(("parallel",)),
    )(x, gain, w)


def _norm_matmul_bwd(dres, dproj, x, gain, w, name):
    t, d = x.shape
    n = w.shape[1]
    tm = _tile(t, 256)

    def body(dr_ref, dp_ref, x_ref, g_ref, w_ref, dx_ref, hn_ref, dg_ref):
        @pl.when(pl.program_id(0) == 0)
        def _():
            dg_ref[...] = jnp.zeros_like(dg_ref)

        dhn = _dot_nt(dp_ref[...], w_ref[...])
        gain_v = g_ref[...]
        hn, xhat, rstd = _rms_fwd(x_ref[...], gain_v)
        hn_ref[...] = hn.astype(BF)
        dx, dgr = _rms_bwd(dhn, xhat, rstd, gain_v)
        dx_ref[...] = dr_ref[...] + dx
        dg_ref[...] += jnp.sum(dgr, axis=0, keepdims=True)

    row = lambda wd: pl.BlockSpec((tm, wd), lambda i: (i, 0))
    return pl.pallas_call(
        body, name=name, grid=(t // tm,),
        in_specs=[row(d), row(n), row(d), _full((1, d)), _full((d, n))],
        out_specs=[row(d), row(d), _full((1, d))],
        out_shape=[jax.ShapeDtypeStruct((t, d), F32), jax.ShapeDtypeStruct((t, d), BF),
                   jax.ShapeDtypeStruct((1, d), F32)],
        compiler_params=_params(("arbitrary",)),
    )(dres, dproj, x, gain, w)


def _mix_out(x, yf, yg, w, name):
    t, d = x.shape
    kf = yf.shape[1]
    tm = _tile(t, 512)

    def body(x_ref, yf_ref, yg_ref, w_ref, o_ref):
        o_ref[...] = x_ref[...] + _dot(yf_ref[...], w_ref[0:kf, :]) + _dot(yg_ref[...], w_ref[kf:2 * kf, :])

    row = lambda wd: pl.BlockSpec((tm, wd), lambda i: (i, 0))
    return pl.pallas_call(
        body, name=name, grid=(t // tm,),
        in_specs=[row(d), row(kf), row(kf), _full((2 * kf, d))],
        out_specs=row(d), out_shape=jax.ShapeDtypeStruct((t, d), F32),
        compiler_params=_params(("parallel",)),
    )(x, yf, yg, w)


def _mix_out_bwd(dx, w, name):
    t, d = dx.shape
    kf = w.shape[0] // 2
    tm = _tile(t, 512)

    def body(dx_ref, w_ref, df_ref, dg_ref, dxb_ref):
        dxb = dx_ref[...].astype(BF)
        dxb_ref[...] = dxb
        df_ref[...] = _dot_nt(dxb, w_ref[0:kf, :]).astype(BF)
        dg_ref[...] = _dot_nt(dxb, w_ref[kf:2 * kf, :]).astype(BF)

    row = lambda wd: pl.BlockSpec((tm, wd), lambda i: (i, 0))
    return pl.pallas_call(
        body, name=name, grid=(t // tm,),
        in_specs=[row(d), _full((2 * kf, d))],
        out_specs=[row(kf), row(kf), row(d)],
        out_shape=[jax.ShapeDtypeStruct((t, kf), BF), jax.ShapeDtypeStruct((t, kf), BF),
                   jax.ShapeDtypeStruct((t, d), BF)],
        compiler_params=_params(("parallel",)),
    )(dx, w)


def _loss_grad(y, target, name):
    t, d = y.shape
    tm = _tile(t, 512)

    def body(y_ref, t_ref, l_ref, dy_ref):
        @pl.when(pl.program_id(0) == 0)
        def _():
            l_ref[...] = jnp.zeros_like(l_ref)

        diff = y_ref[...] - t_ref[...]
        dy_ref[...] = diff * (1.0 / d)
        part = jnp.sum(jnp.sum(diff * diff, axis=1, keepdims=True), axis=0, keepdims=True)
        l_ref[...] += part * (0.5 / d)

    row = pl.BlockSpec((tm, d), lambda i: (i, 0))
    return pl.pallas_call(
        body, name=name, grid=(t // tm,),
        in_specs=[row, row], out_specs=[_full((1, 1)), row],
        out_shape=[jax.ShapeDtypeStruct((1, 1), F32), jax.ShapeDtypeStruct((t, d), F32)],
        compiler_params=_params(("arbitrary",)),
    )(y, target)


def _head_sum_matrix(width, head):
    r = lax.broadcasted_iota(jnp.int32, (width, width), 0) // head
    c = lax.broadcasted_iota(jnp.int32, (width, width), 1) // head
    return (r == c).astype(BF)


def _head_mean(x, bd):
    return _dot(x.astype(BF), bd) * (1.0 / FOX_HEAD_DIM)


def _mask_dot(mask01, x):
    mb = mask01.astype(BF)
    hi = x.astype(BF)
    r1 = x - hi.astype(F32)
    mid = r1.astype(BF)
    lo = (r1 - mid.astype(F32)).astype(BF)
    return _dot(mb, hi) + _dot(mb, mid) + _dot(mb, lo)


def _fox_prep(proj, wq_t, wk_t, bias_pad, seq, name):
    t = proj.shape[0]
    ts = _tile(seq, 512)
    tpe = seq // ts
    scale = FOX_HEAD_DIM ** -0.5

    def body(q_ref, k_ref, v_ref, gt_ref, wq_ref, wk_ref, b_ref, qo_ref, ko_ref, vo_ref, cum_ref, carry):
        i = pl.program_id(0)
        bd = _head_sum_matrix(FOX_WIDTH, FOX_HEAD_DIM)

        def norm(xv, wv):
            ms = _head_mean(xv * xv, bd)
            return xv * lax.rsqrt(ms + EPS) * wv

        qo_ref[...] = (norm(q_ref[...], wq_ref[...]) * scale).astype(BF)
        ko_ref[...] = norm(k_ref[...], wk_ref[...]).astype(BF)
        vo_ref[...] = v_ref[...].astype(BF)

        @pl.when(i % tpe == 0)
        def _():
            carry[...] = jnp.zeros_like(carry)

        ls = _log_sigmoid(gt_ref[...] + b_ref[...])
        r = lax.broadcasted_iota(jnp.int32, (ts, ts), 0)
        c = lax.broadcasted_iota(jnp.int32, (ts, ts), 1)
        cum = _mask_dot(r >= c, ls) + carry[...]
        cum_ref[...] = cum
        carry[...] = cum[ts - 1:ts, :]

    blk = lambda j: pl.BlockSpec((ts, FOX_WIDTH), lambda i: (i, j))
    gate = pl.BlockSpec((ts, LANES), lambda i: (i, GATE_COL // LANES))
    out = pl.BlockSpec((ts, FOX_WIDTH), lambda i: (i, 0))
    return pl.pallas_call(
        body, name=name, grid=(t // ts,),
        in_specs=[blk(0), blk(1), blk(2), gate, _full((1, FOX_WIDTH)), _full((1, FOX_WIDTH)), _full((1, LANES))],
        out_specs=[out, out, out, pl.BlockSpec((ts, LANES), lambda i: (i, 0))],
        out_shape=[jax.ShapeDtypeStruct((t, FOX_WIDTH), BF)] * 3 + [jax.ShapeDtypeStruct((t, LANES), F32)],
        scratch_shapes=[pltpu.VMEM((1, LANES), F32)],
        compiler_params=_params(("arbitrary",)),
    )(proj, proj, proj, proj, wq_t, wk_t, bias_pad)


def _fox_prep_bwd(proj, dqa, dqb, dk, dv, dcum, wq_t, wk_t, bias_pad, seq, name):
    t = proj.shape[0]
    ts = _tile(seq, 512)
    tpe = seq // ts
    nt = t // ts
    scale = FOX_HEAD_DIM ** -0.5

    def body(q_ref, k_ref, gt_ref, dqa_ref, dqb_ref, dk_ref, dv_ref, dc_ref, wq_ref, wk_ref, b_ref,
             dp_ref, dff_ref, dwq_ref, dwk_ref, db_ref, carry):
        i = pl.program_id(0)
        first = (lax.broadcasted_iota(jnp.int32, (ts, FOX_WIDTH), 1) % LANES) < FOX_HEAD_DIM
        dq_all = jnp.where(first, dqa_ref[...], dqb_ref[...])
        ti = nt - 1 - i
        bd = _head_sum_matrix(FOX_WIDTH, FOX_HEAD_DIM)

        @pl.when(i == 0)
        def _():
            dwq_ref[...] = jnp.zeros_like(dwq_ref)
            dwk_ref[...] = jnp.zeros_like(dwk_ref)
            db_ref[...] = jnp.zeros_like(db_ref)

        def norm_bwd(xv, wv, dyv):
            ms = _head_mean(xv * xv, bd)
            rstd = lax.rsqrt(ms + EPS)
            xhat = xv * rstd
            dxhat = dyv * wv
            mean = _head_mean(dxhat * xhat, bd)
            return rstd * (dxhat - xhat * mean), jnp.sum(dyv * xhat, axis=0, keepdims=True)

        dxq, dwq = norm_bwd(q_ref[...], wq_ref[...], dq_all * scale)
        dxk, dwk = norm_bwd(k_ref[...], wk_ref[...], dk_ref[...])
        dp_ref[:, 0:FOX_WIDTH] = dxq.astype(BF)
        dp_ref[:, FOX_WIDTH:2 * FOX_WIDTH] = dxk.astype(BF)
        dp_ref[:, 2 * FOX_WIDTH:3 * FOX_WIDTH] = dv_ref[...].astype(BF)
        dwq_ref[...] += dwq
        dwk_ref[...] += dwk

        @pl.when(ti % tpe == tpe - 1)
        def _():
            carry[...] = jnp.zeros_like(carry)

        r = lax.broadcasted_iota(jnp.int32, (ts, ts), 0)
        c = lax.broadcasted_iota(jnp.int32, (ts, ts), 1)
        dls = _mask_dot(c >= r, dc_ref[...]) + carry[...]
        carry[...] = dls[0:1, :]
        z = gt_ref[...] + b_ref[...]
        lane = lax.broadcasted_iota(jnp.int32, (ts, LANES), 1)
        dff = jnp.where(lane < FOX_HEADS, dls * _sigmoid(-z), 0.0)
        dff_ref[...] = dff
        db_ref[...] += jnp.sum(dff, axis=0, keepdims=True)

        @pl.when(i == nt - 1)
        def _():
            fr = lax.broadcasted_iota(jnp.int32, (FOX_WIDTH, FOX_WIDTH), 0) % FOX_HEAD_DIM
            fc = lax.broadcasted_iota(jnp.int32, (FOX_WIDTH, FOX_WIDTH), 1) % FOX_HEAD_DIM
            fold = (fr == fc).astype(F32)
            dwq_ref[...] = _dot(dwq_ref[...], fold, HI)
            dwk_ref[...] = _dot(dwk_ref[...], fold, HI)

    rev = lambda w, j: pl.BlockSpec((ts, w), lambda i: (nt - 1 - i, j))
    return pl.pallas_call(
        body, name=name, grid=(nt,),
        in_specs=[rev(FOX_WIDTH, 0), rev(FOX_WIDTH, 1), rev(LANES, GATE_COL // LANES),
                  rev(FOX_WIDTH, 0), rev(FOX_WIDTH, 0), rev(FOX_WIDTH, 0), rev(FOX_WIDTH, 0), rev(LANES, 0),
                  _full((1, FOX_WIDTH)), _full((1, FOX_WIDTH)), _full((1, LANES))],
        out_specs=[rev(3 * FOX_WIDTH, 0), rev(LANES, 0), _full((1, FOX_WIDTH)), _full((1, FOX_WIDTH)),
                   _full((1, LANES))],
        out_shape=[jax.ShapeDtypeStruct((t, 3 * FOX_WIDTH), BF), jax.ShapeDtypeStruct((t, LANES), F32),
                   jax.ShapeDtypeStruct((1, FOX_WIDTH), F32), jax.ShapeDtypeStruct((1, FOX_WIDTH), F32),
                   jax.ShapeDtypeStruct((1, LANES), F32)],
        scratch_shapes=[pltpu.VMEM((1, LANES), F32)],
        compiler_params=_params(("arbitrary",)),
    )(proj, proj, proj, dqa, dqb, dk, dv, dcum, wq_t, wk_t, bias_pad)


class _Rider:
    def __init__(self, inputs, out_shapes, sems, phases):
        self.inputs, self.out_shapes, self.sems, self.phases = list(inputs), list(out_shapes), list(sems), phases


def _rider_parts(rider):
    if rider is None:
        return [], [], []
    return rider.inputs, rider.out_shapes, rider.sems


def _ride(rider, which, when, refs):
    if rider is not None:
        @pl.when(when)
        def _():
            rider.phases[which](*refs)


def _fox_attn(q, k, v, ck, nb, seq, name, rider=None):
    t = q.shape[0]
    tq = _tile(seq, 512)
    nq = seq // tq
    npair = FOX_HEADS // 2
    hd = FOX_HEAD_DIM
    r_in, r_out, r_sem = _rider_parts(rider)
    steps = nb * npair * nq

    def body(q_ref, k_ref, v_ref, ck_ref, *rest):
        rin, (o_ref, lse_ref) = rest[:len(r_in)], rest[len(r_in):len(r_in) + 2]
        rout = rest[len(r_in) + 2:len(r_in) + 2 + len(r_out)]
        m_s, acc_s = rest[len(r_in) + 2 + len(r_out):len(r_in) + 4 + len(r_out)]
        riding = (rin, rout, rest[len(r_in) + 4 + len(r_out):])
        step = (pl.program_id(0) * npair + pl.program_id(1)) * nq + pl.program_id(2)
        _ride(rider, 0, step == 0, riding)
        _ride(rider, 1, step == steps // 2, riding)
        qi = pl.program_id(2)
        lane = lax.broadcasted_iota(jnp.int32, (tq, LANES), 1)
        m_s[...] = jnp.full(m_s.shape, NEG, F32)
        acc_s[...] = jnp.zeros_like(acc_s)
        qv = q_ref[...]

        def tile(kj, on_diagonal):
            cols = pl.ds(pl.multiple_of(kj * tq, tq), tq)
            kv = k_ref[cols, :]
            vv = v_ref[cols, :]
            if on_diagonal:
                causal = (lax.broadcasted_iota(jnp.int32, (tq, tq), 0)
                          >= lax.broadcasted_iota(jnp.int32, (tq, tq), 1))
            ck = [ck_ref[hh, :, cols] for hh in range(2)]
            m_old = [m_s[hh] for hh in range(2)]
            acc_old = [acc_s[hh] for hh in range(2)]
            m_out, acc_out = [], []
            for hh in range(2):
                hm = (lane >= hd) if hh else (lane < hd)
                qh = jnp.where(hm, qv, jnp.zeros_like(qv))
                s = _dot_nt(qh, kv) - ck[hh]
                if on_diagonal:
                    s = jnp.where(causal, s, NEG)
                m_new = jnp.maximum(m_old[hh], jnp.max(s, axis=-1, keepdims=True))
                p = jnp.exp(s - m_new)
                alpha = jnp.exp(m_old[hh] - m_new)
                m_out.append(m_new)
                acc_out.append(alpha * acc_old[hh] + _dot(p.astype(BF), jnp.where(hm, vv, jnp.ones_like(vv))))
            for hh in range(2):
                m_s[hh] = m_out[hh]
                acc_s[hh] = acc_out[hh]

        def off_diagonal(kj, carry):
            tile(kj, False)
            return carry

        lax.fori_loop(0, qi, off_diagonal, 0)
        tile(qi, True)
        a0 = acc_s[0]
        a1 = acc_s[1]
        den = jnp.where(lane < hd, pltpu.roll(a0, hd, axis=1), pltpu.roll(a1, hd, axis=1))
        o_ref[...] = (jnp.where(lane < hd, a0, a1) / den).astype(o_ref.dtype)
        l0 = jnp.sum(jnp.where(lane == hd, a0, 0.0), axis=1, keepdims=True)
        l1 = jnp.sum(jnp.where(lane == 0, a1, 0.0), axis=1, keepdims=True)
        lse_ref[0] = m_s[0] + jnp.log(l0)
        lse_ref[1] = m_s[1] + jnp.log(l1)
        _ride(rider, 2, step == steps - 1, riding)

    qspec = pl.BlockSpec((tq, LANES), lambda b, p, i: (b * nq + i, p))
    kspec = pl.BlockSpec((seq, LANES), lambda b, p, i: (b, p))
    colspec = pl.BlockSpec((None, 2, tq, 1), lambda b, p, i: (b * npair + p, 0, i, 0))
    rowspec = pl.BlockSpec((None, 2, 1, seq), lambda b, p, i: (b * npair + p, 0, 0, 0))
    sem = ("arbitrary",) * 3 if rider else ("parallel",) * 3
    return pl.pallas_call(
        body, name=name, grid=(nb, npair, nq),
        in_specs=[qspec, kspec, kspec, rowspec] + [HBM] * len(r_in),
        out_specs=[qspec, colspec] + [HBM] * len(r_out),
        out_shape=[jax.ShapeDtypeStruct((t, FOX_WIDTH), BF), jax.ShapeDtypeStruct((nb * npair, 2, seq, 1), F32)]
        + r_out,
        scratch_shapes=[pltpu.VMEM((2, tq, 1), F32), pltpu.VMEM((2, tq, LANES), F32)] + r_sem,
        compiler_params=_params(sem, has_side_effects=rider is not None),
    )(q, k, v, ck, *r_in)


def _fox_attn_bwd(q, k, v, o, do, lse, ck, nb, seq, name, rider=None):
    t = q.shape[0]
    tq = _tile(seq, 512)
    nq = seq // tq
    npair = FOX_HEADS // 2
    hd = FOX_HEAD_DIM
    r_in, r_out, r_sem = _rider_parts(rider)
    steps = nb * npair * nq

    def body(q_ref, k_ref, v_ref, o_ref, do_ref, lse_ref, ck_ref, *rest):
        rin, (dqa_ref, dqb_ref, dk_ref, dv_ref, dkx_ref) = rest[:len(r_in)], rest[len(r_in):len(r_in) + 5]
        rout = rest[len(r_in) + 5:len(r_in) + 5 + len(r_out)]
        dk_s, dv_s = rest[len(r_in) + 5 + len(r_out):len(r_in) + 7 + len(r_out)]
        riding = (rin, rout, rest[len(r_in) + 7 + len(r_out):])
        step = (pl.program_id(0) * npair + pl.program_id(1)) * nq + pl.program_id(2)
        _ride(rider, 0, step == 0, riding)
        _ride(rider, 1, step == steps // 2, riding)
        kj = pl.program_id(2)
        lane = lax.broadcasted_iota(jnp.int32, (tq, LANES), 1)

        @pl.when(kj == 0)
        def _():
            dqa_ref[...] = jnp.zeros_like(dqa_ref)
            dqb_ref[...] = jnp.zeros_like(dqb_ref)

        dk_s[...] = jnp.zeros_like(dk_s)
        dv_s[...] = jnp.zeros_like(dv_s)
        kv = k_ref[...]
        vv = v_ref[...]

        def tile(qi, on_diagonal):
            rows = pl.ds(pl.multiple_of(qi * tq, tq), tq)
            qv = q_ref[rows, :]
            dov = do_ref[rows, :]
            prod = dov.astype(F32) * o_ref[rows, :].astype(F32)
            if on_diagonal:
                causal = (lax.broadcasted_iota(jnp.int32, (tq, tq), 0)
                          >= lax.broadcasted_iota(jnp.int32, (tq, tq), 1))
            for hh, dq_ref in ((0, dqa_ref), (1, dqb_ref)):
                hm = (lane >= hd) if hh else (lane < hd)
                zero = jnp.zeros_like(qv)
                one = jnp.ones_like(qv)
                doh = jnp.where(hm, dov, zero)
                delta = jnp.sum(jnp.where(hm, prod, 0.0), axis=-1, keepdims=True)
                s = _dot_nt(jnp.where(hm, qv, zero), kv) - ck_ref[hh]
                if on_diagonal:
                    s = jnp.where(causal, s, NEG)
                p = jnp.exp(s - lse_ref[hh, rows, :])
                dp = _dot_nt(doh, vv)
                dsb = (p * (dp - delta)).astype(BF)
                dv_s[...] += _dot_tn(p.astype(BF), doh)
                dk_s[hh] += _dot_tn(dsb, jnp.where(hm, qv, one))
                dq_ref[rows, :] += _dot(dsb, jnp.where(hm, kv, one))

        def off_diagonal(qi, carry):
            tile(qi, False)
            return carry

        tile(kj, True)
        lax.fori_loop(kj + 1, nq, off_diagonal, 0)
        dk_ref[...] = jnp.where(lane < hd, dk_s[0], dk_s[1])
        dkx_ref[...] = jnp.where(lane < hd, dk_s[1], dk_s[0])
        dv_ref[...] = dv_s[...]
        _ride(rider, 2, step == steps - 1, riding)

    kspec = pl.BlockSpec((tq, LANES), lambda b, p, j: (b * nq + j, p))
    full_q = pl.BlockSpec((seq, LANES), lambda b, p, j: (b, p))
    colspec = pl.BlockSpec((None, 2, seq, 1), lambda b, p, j: (b * npair + p, 0, 0, 0))
    rowspec = pl.BlockSpec((None, 2, 1, tq), lambda b, p, j: (b * npair + p, 0, 0, j))
    sem = ("arbitrary",) * 3 if rider else ("parallel", "parallel", "arbitrary")
    return pl.pallas_call(
        body, name=name, grid=(nb, npair, nq),
        in_specs=[full_q, kspec, kspec, full_q, full_q, colspec, rowspec] + [HBM] * len(r_in),
        out_specs=[full_q, full_q, kspec, kspec, kspec] + [HBM] * len(r_out),
        out_shape=[jax.ShapeDtypeStruct((t, FOX_WIDTH), F32)] * 5 + r_out,
        scratch_shapes=[pltpu.VMEM((2, tq, LANES), F32), pltpu.VMEM((tq, LANES), F32)] + r_sem,
        compiler_params=_params(sem, has_side_effects=rider is not None),
    )(q, k, v, o, do, lse, ck, *r_in)


GDN_QKV = 3 * GDN_WIDTH
GDN_COL = 3 * FOX_WIDTH
GG_COL = GDN_COL + GDN_QKV
A_LANE = FOX_HEADS
B_LANE = FOX_HEADS + GDN_HEADS
HALO = 8


def _gate_lanes(ts):
    lane = lax.broadcasted_iota(jnp.int32, (ts, LANES), 1)
    return (lane >= A_LANE) & (lane < B_LANE), (lane >= B_LANE) & (lane < B_LANE + GDN_HEADS)


def _chunk_tri(ts, upper):
    r = lax.broadcasted_iota(jnp.int32, (ts, ts), 0)
    c = lax.broadcasted_iota(jnp.int32, (ts, ts), 1)
    same = (r // CHUNK) == (c // CHUNK)
    return (same & ((c >= r) if upper else (r >= c))).astype(F32)


def _conv_silu_l2(xp_ref, w, ts):
    c = w[0:1, :] * xp_ref[pl.ds(HALO - 3, ts), :]
    for kk in range(1, CONV_WIDTH):
        c = c + w[kk:kk + 1, :] * xp_ref[pl.ds(HALO - 3 + kk, ts), :]
    return c, c * _sigmoid(c)


def _gdn_prep(proj, conv_w, a_pad, dt_pad, seq, name):
    t = proj.shape[0]
    ts = _tile(seq, 256)
    tpe = seq // ts
    qscale = GDN_HEAD_DIM ** -0.5

    def body(x_ref, gt_ref, w_ref, a_ref, dt_ref, qo_ref, ko_ref, vo_ref, go_ref, xp):
        i = pl.program_id(0)
        tail = xp[pl.ds(ts, HALO), :]
        xp[pl.ds(0, HALO), :] = jnp.where(i % tpe == 0, jnp.zeros_like(tail), tail)
        xp[pl.ds(HALO, ts), :] = x_ref[...]
        _, s = _conv_silu_l2(xp, w_ref[...], ts)
        for h in range(GDN_HEADS):
            for base, ref, sc in ((0, qo_ref, qscale), (GDN_WIDTH, ko_ref, 1.0)):
                xh = s[:, base + h * LANES: base + (h + 1) * LANES]
                r = lax.rsqrt(jnp.sum(xh * xh, axis=-1, keepdims=True) + EPS)
                ref[:, h * LANES:(h + 1) * LANES] = (xh * (r * sc)).astype(BF)
        vo_ref[...] = s[:, 2 * GDN_WIDTH:].astype(BF)
        gate = gt_ref[...]
        g_raw = -jnp.exp(a_ref[...]) * _softplus(gate + dt_ref[...])
        gc = _mask_dot(_chunk_tri(ts, False), g_raw)
        is_a, is_b = _gate_lanes(ts)
        go_ref[...] = jnp.where(is_a, gc, jnp.where(is_b, _sigmoid(gate), 0.0))

    out = pl.BlockSpec((ts, GDN_WIDTH), lambda i: (i, 0))
    lanes = pl.BlockSpec((ts, LANES), lambda i: (i, 0))
    return pl.pallas_call(
        body, name=name, grid=(t // ts,),
        in_specs=[pl.BlockSpec((ts, GDN_QKV), lambda i: (i, GDN_COL // GDN_QKV)),
                  pl.BlockSpec((ts, LANES), lambda i: (i, GATE_COL // LANES)),
                  _full((CONV_WIDTH, GDN_QKV)), _full((1, LANES)), _full((1, LANES))],
        out_specs=[out, out, out, lanes],
        out_shape=[jax.ShapeDtypeStruct((t, GDN_WIDTH), BF)] * 3 + [jax.ShapeDtypeStruct((t, LANES), F32)],
        scratch_shapes=[pltpu.VMEM((ts + HALO, GDN_QKV), F32)],
        compiler_params=_params(("arbitrary",)),
    )(proj, proj, conv_w, a_pad, dt_pad)


def _gdn_prep_bwd(proj, dq, dk, dv, dgates, dff, conv_w, a_pad, dt_pad, seq, name):
    t = proj.shape[0]
    ts = _tile(seq, 256)
    tpe = seq // ts
    nt = t // ts
    qscale = GDN_HEAD_DIM ** -0.5
    hb = ts // HALO

    def body(x_ref, halo_ref, gt_ref, dq_ref, dk_ref, dv_ref, dgt_ref, dff_ref, w_ref, a_ref, dt_ref,
             dx_ref, dgo_ref, dw_ref, da_ref, ddt_ref, xp, dcp, carry):
        i = pl.program_id(0)
        ti = nt - 1 - i

        @pl.when(i == 0)
        def _():
            dw_ref[...] = jnp.zeros_like(dw_ref)
            da_ref[...] = jnp.zeros_like(da_ref)
            ddt_ref[...] = jnp.zeros_like(ddt_ref)

        halo = halo_ref[...]
        xp[pl.ds(0, HALO), :] = jnp.where(ti % tpe == 0, jnp.zeros_like(halo), halo)
        xp[pl.ds(HALO, ts), :] = x_ref[...]
        w = w_ref[...]
        c, s = _conv_silu_l2(xp, w, ts)
        for h in range(GDN_HEADS):
            for base, ref, sc in ((0, dq_ref, qscale), (GDN_WIDTH, dk_ref, 1.0)):
                lo = base + h * LANES
                xh = s[:, lo:lo + LANES]
                r = lax.rsqrt(jnp.sum(xh * xh, axis=-1, keepdims=True) + EPS)
                y = xh * r
                dy = ref[:, h * LANES:(h + 1) * LANES] * sc
                dcp[pl.ds(0, ts), lo:lo + LANES] = r * (dy - y * jnp.sum(dy * y, axis=-1, keepdims=True))
        dcp[pl.ds(0, ts), 2 * GDN_WIDTH:] = dv_ref[...]
        sg = _sigmoid(c)
        dc = dcp[pl.ds(0, ts), :] * (sg * (1.0 + c * (1.0 - sg)))
        dcp[pl.ds(0, ts), :] = dc
        nxt = carry[...]
        dcp[pl.ds(ts, HALO), :] = jnp.where(ti % tpe == tpe - 1, jnp.zeros_like(nxt), nxt)
        carry[...] = dc[0:HALO, :]
        dx = w[CONV_WIDTH - 1:CONV_WIDTH, :] * dc
        for kk in range(CONV_WIDTH - 1):
            dx = dx + w[kk:kk + 1, :] * dcp[pl.ds(CONV_WIDTH - 1 - kk, ts), :]
        dx_ref[...] = dx.astype(BF)
        for kk in range(CONV_WIDTH):
            dw_ref[kk:kk + 1, :] += jnp.sum(dc * xp[pl.ds(HALO - 3 + kk, ts), :], axis=0, keepdims=True)
        gate = gt_ref[...]
        dgt = dgt_ref[...]
        is_a, is_b = _gate_lanes(ts)
        dg_raw = _mask_dot(_chunk_tri(ts, True), jnp.where(is_a, dgt, 0.0))
        z = gate + dt_ref[...]
        na = -jnp.exp(a_ref[...])
        dga = dg_raw * na * _sigmoid(z)
        beta = _sigmoid(gate)
        dgb = jnp.where(is_b, dgt * beta * (1.0 - beta), 0.0)
        dgo_ref[...] = (dff_ref[...] + dga + dgb).astype(BF)
        ddt_ref[...] += jnp.sum(dga, axis=0, keepdims=True)
        da_ref[...] += jnp.sum(dg_raw * na * _softplus(z), axis=0, keepdims=True)

    rev = lambda wd, j: pl.BlockSpec((ts, wd), lambda i: (nt - 1 - i, j))
    halo_spec = pl.BlockSpec((HALO, GDN_QKV), lambda i: (jnp.maximum((nt - 1 - i) * hb - 1, 0), GDN_COL // GDN_QKV))
    return pl.pallas_call(
        body, name=name, grid=(nt,),
        in_specs=[rev(GDN_QKV, GDN_COL // GDN_QKV), halo_spec, rev(LANES, GATE_COL // LANES),
                  rev(GDN_WIDTH, 0), rev(GDN_WIDTH, 0), rev(GDN_WIDTH, 0), rev(LANES, 0), rev(LANES, 0),
                  _full((CONV_WIDTH, GDN_QKV)), _full((1, LANES)), _full((1, LANES))],
        out_specs=[rev(GDN_QKV, 0), rev(LANES, 0), _full((CONV_WIDTH, GDN_QKV)), _full((1, LANES)),
                   _full((1, LANES))],
        out_shape=[jax.ShapeDtypeStruct((t, GDN_QKV), BF), jax.ShapeDtypeStruct((t, LANES), BF),
                   jax.ShapeDtypeStruct((CONV_WIDTH, GDN_QKV), F32), jax.ShapeDtypeStruct((1, LANES), F32),
                   jax.ShapeDtypeStruct((1, LANES), F32)],
        scratch_shapes=[pltpu.VMEM((ts + HALO, GDN_QKV), F32), pltpu.VMEM((ts + HALO, GDN_QKV), F32),
                        pltpu.VMEM((HALO, GDN_QKV), F32)],
        compiler_params=_params(("arbitrary",)),
    )(proj, proj, proj, dq, dk, dv, dgates, dff, conv_w, a_pad, dt_pad)


PAIR = 2 * CHUNK


def _split_bf16(a):
    hi = a.astype(BF)
    return hi, (a - hi.astype(F32)).astype(BF)


def _dot3(a, b, dims=(((1,), (0,)), ((), ()))):
    ah, al = _split_bf16(a)
    bh, bl = _split_bf16(b)
    (ca,), (cb,) = dims[0]
    return lax.dot_general(jnp.concatenate([ah, al, ah], axis=ca), jnp.concatenate([bh, bh, bl], axis=cb), dims,
                           preferred_element_type=F32)


def _inv_unit_lower(a):
    r = lax.broadcasted_iota(jnp.int32, (PAIR, PAIR), 0)
    c = lax.broadcasted_iota(jnp.int32, (PAIR, PAIR), 1)
    tm = (r == c).astype(F32) - a
    pw = _dot3(a, a)
    for _ in range(4):
        x = _dot3(jnp.concatenate([tm, pw], axis=0), pw)
        tm = tm + x[:PAIR]
        pw = x[PAIR:]
    return tm + _dot3(tm, pw)


def _gdn_pair_local(q, k, v, gc, gr, b):
    r = lax.broadcasted_iota(jnp.int32, (PAIR, PAIR), 0)
    c = lax.broadcasted_iota(jnp.int32, (PAIR, PAIR), 1)
    same = (r // CHUNK) == (c // CHUNK)
    incl = same & (r >= c)
    strict = same & (r > c)
    dm = jnp.exp(jnp.where(incl, gc - gr, NEG))
    e = jnp.exp(gc)
    kb = k * b
    vb = v * b
    kbe = kb * e
    kq = _dot_nt(jnp.concatenate([kb, q], axis=0).astype(BF), k.astype(BF))
    amat = jnp.where(strict, kq[:PAIR] * dm, 0.0)
    pmat = jnp.where(incl, kq[PAIR:] * dm, 0.0)
    lane = lax.broadcasted_iota(jnp.int32, (1, PAIR), 1)
    gl_a = jnp.sum(jnp.where(lane == CHUNK - 1, gr, 0.0), axis=1, keepdims=True)
    gl_b = jnp.sum(jnp.where(lane == PAIR - 1, gr, 0.0), axis=1, keepdims=True)
    ridx = lax.broadcasted_iota(jnp.int32, (PAIR, 1), 0)
    edec = jnp.exp(jnp.where(ridx < CHUNK, gl_a, gl_b) - gc)
    return dict(dm=dm, e=e, kb=kb, vb=vb, kbe=kbe, amat=amat, pmat=pmat, gl_a=gl_a, gl_b=gl_b, edec=edec,
                kd=k * edec, qd=q * e, incl=incl, strict=strict, ridx=ridx)


def _gdn_pair_states(loc, tb, s_a):
    uw = _dot(tb, jnp.concatenate([loc["vb"], loc["kbe"]], axis=1).astype(BF))
    u, w = uw[:, :LANES], uw[:, LANES:]
    qd, kd, c = loc["qd"], loc["kd"], CHUNK
    xa = _dot(jnp.concatenate([qd[:c], w[:c]], axis=0).astype(BF), s_a.astype(BF))
    vn_a = u[:c] - xa[c:]
    s_b = s_a * jnp.exp(loc["gl_a"]) + _dot_tn(kd[:c].astype(BF), vn_a.astype(BF))
    xb = _dot(jnp.concatenate([qd[c:], w[c:]], axis=0).astype(BF), s_b.astype(BF))
    vn_b = u[c:] - xb[c:]
    s_c = s_b * jnp.exp(loc["gl_b"]) + _dot_tn(kd[c:].astype(BF), vn_b.astype(BF))
    vn = jnp.concatenate([vn_a, vn_b], axis=0)
    o = jnp.concatenate([xa[:c], xb[:c]], axis=0) + _dot(loc["pmat"].astype(BF), vn.astype(BF))
    return w, vn, o, s_b, s_c


GDN_SEG = 1024
HEADS_PER_STEP = 4
HEAD_GROUPS = GDN_HEADS // HEADS_PER_STEP


def _gdn_specs(nb, seq, reverse):
    n = seq // CHUNK
    seg = _tile(seq, GDN_SEG)
    nseg = seq // seg
    sp = seg // PAIR
    w2 = HEADS_PER_STEP * LANES
    at = (lambda s: nseg - 1 - s) if reverse else (lambda s: s)
    blk = pl.BlockSpec((seg, w2), lambda b, hp, s: (b * nseg + at(s), hp))
    gg = pl.BlockSpec((seg, w2), lambda b, hp, s: (b * nseg + at(s), GG_COL // w2 + hp))
    gates = pl.BlockSpec((seg, LANES), lambda b, hp, s: (b * nseg + at(s), 0))
    grp = lambda b, hp: b * HEAD_GROUPS + hp
    rowb = pl.BlockSpec((None, HEADS_PER_STEP, sp, HALO, PAIR), lambda b, hp, s: (grp(b, hp), 0, at(s), 0, 0))
    per_pair = pl.BlockSpec((None, HEADS_PER_STEP, sp, PAIR, PAIR), lambda b, hp, s: (grp(b, hp), 0, at(s), 0, 0))
    dgates = pl.BlockSpec((None, seg, LANES), lambda b, hp, s: (grp(b, hp), at(s), 0))
    return n, seg, nseg, sp, blk, gg, gates, rowb, per_pair, dgates


def _head_column(gt, lane, index):
    return jnp.sum(jnp.where(lane == index, gt, 0.0), axis=1, keepdims=True)


def _gdn_head_inputs(qkv_refs, gt, gr_ref, rows, pi, hp, lane):
    per_head = []
    for hh in range(HEADS_PER_STEP):
        head = HEADS_PER_STEP * hp + hh
        cols = slice(hh * LANES, (hh + 1) * LANES)
        per_head.append([r[rows, cols].astype(F32) for r in qkv_refs]
                        + [_head_column(gt, lane, A_LANE + head), gr_ref[hh, pi][0:1, :],
                           _head_column(gt, lane, B_LANE + head)])
    return [jnp.stack(xs) for xs in zip(*per_head)]


def _gdn_pair_fwd(qv, kv, vv, gcv, gr, bv, s_a):
    loc = _gdn_pair_local(qv, kv, vv, gcv, gr, bv)
    tf = _inv_unit_lower(loc["amat"])
    _, _, o, _, s_c = _gdn_pair_states(loc, tf.astype(BF), s_a)
    return tf, o, s_c


def _gdn_fwd(q, k, v, proj, gates, grow, wn, nb, seq, name):
    t = q.shape[0]
    n, seg, nseg, sp, blk, gg, gates_spec, rowb, per_pair, _ = _gdn_specs(nb, seq, False)

    def body(q_ref, k_ref, v_ref, gg_ref, gt_ref, gr_ref, wn_ref, y_ref, tn_ref, sn_ref, s_ref):
        hp = pl.program_id(1)

        @pl.when(pl.program_id(2) == 0)
        def _():
            s_ref[...] = jnp.zeros_like(s_ref)

        wnv = wn_ref[...]
        lane = lax.broadcasted_iota(jnp.int32, (PAIR, LANES), 1)

        def step(pi, carry):
            rows = pl.ds(pl.multiple_of(pi * PAIR, PAIR), PAIR)
            gt = gt_ref[rows, :]
            ins = _gdn_head_inputs((q_ref, k_ref, v_ref), gt, gr_ref, rows, pi, hp, lane)
            s_a = s_ref[...]
            tf, o, s_c = jax.vmap(_gdn_pair_fwd)(*ins, s_a)
            s_ref[...] = s_c
            for hh in range(HEADS_PER_STEP):
                cols = slice(hh * LANES, (hh + 1) * LANES)
                tn_ref[hh, pi] = tf[hh]
                sn_ref[hh, pi] = s_a[hh]
                g = gg_ref[rows, cols]
                oh = o[hh]
                rstd = lax.rsqrt(jnp.mean(oh * oh, axis=-1, keepdims=True) + EPS)
                y_ref[rows, cols] = (oh * rstd * wnv * (g * _sigmoid(g))).astype(BF)
            return carry

        lax.fori_loop(0, sp, step, 0)

    saved = jax.ShapeDtypeStruct((nb * HEAD_GROUPS, HEADS_PER_STEP, n // 2, PAIR, PAIR), F32)
    return pl.pallas_call(
        body, name=name, grid=(nb, GDN_HEADS // HEADS_PER_STEP, nseg),
        in_specs=[blk, blk, blk, gg, gates_spec, rowb, _full((1, LANES))],
        out_specs=[blk, per_pair, per_pair],
        out_shape=[jax.ShapeDtypeStruct((t, GDN_WIDTH), BF), saved, saved],
        scratch_shapes=[pltpu.VMEM((HEADS_PER_STEP, GDN_HEAD_DIM, GDN_HEAD_DIM), F32)],
        compiler_params=_params(("parallel", "parallel", "arbitrary")),
    )(q, k, v, proj, gates, grow, wn)


def _gdn_pair_bwd(qv, kv, vv, gcv, gr, bv, tf, s_a, dsp, g, dyv, wnv):
    c = CHUNK
    loc = _gdn_pair_local(qv, kv, vv, gcv, gr, bv)
    tm = tf.astype(BF)
    kb, vb, kbe, e, dm = loc["kb"], loc["vb"], loc["kbe"], loc["e"], loc["dm"]
    kd, qd, pmat, amat = loc["kd"], loc["qd"], loc["pmat"], loc["amat"]
    w, vn, o, s_b, _ = _gdn_pair_states(loc, tm, s_a)
    sg = _sigmoid(g)
    silu = g * sg
    rstd = lax.rsqrt(jnp.mean(o * o, axis=-1, keepdims=True) + EPS)
    xhat = o * rstd
    dwn = jnp.sum(dyv * xhat * silu, axis=0, keepdims=True)
    dgg = dyv * xhat * wnv * (sg * (1.0 + g * (1.0 - sg)))
    dxhat = dyv * wnv * silu
    do = rstd * (dxhat - xhat * jnp.mean(dxhat * xhat, axis=-1, keepdims=True))
    dob = do.astype(BF)
    tot = lambda x: jnp.sum(jnp.sum(x, axis=1, keepdims=True), axis=0, keepdims=True)
    rsum = lambda x: jnp.sum(x, axis=1, keepdims=True)
    cat = lambda xs, ax=0: jnp.concatenate(xs, axis=ax)
    wb = w.astype(BF)
    qdb = qd.astype(BF)
    kdb = kd.astype(BF)
    vnb = vn.astype(BF)
    egl_a = jnp.exp(loc["gl_a"])
    egl_b = jnp.exp(loc["gl_b"])
    ptdo = _dot_tn(pmat.astype(BF), dob)
    dspb = dsp.astype(BF)
    dvn_b = ptdo[c:] + _dot(kdb[c:], dspb)
    dkd_b = _dot_nt(vnb[c:], dspb)
    dgl_b = egl_b * tot(s_b * dsp) + tot(dkd_b * kd[c:])
    dsm = egl_b * dsp + _dot_tn(cat([qdb[c:], -wb[c:]]), cat([dob[c:], dvn_b.astype(BF)]))
    dsmb = dsm.astype(BF)
    dvn_a = ptdo[:c] + _dot(kdb[:c], dsmb)
    dkd_a = _dot_nt(vnb[:c], dsmb)
    dgl_a = egl_a * tot(s_a * dsm) + tot(dkd_a * kd[:c])
    ds_new = egl_a * dsm + _dot_tn(cat([qdb[:c], -wb[:c]]), cat([dob[:c], dvn_a.astype(BF)]))
    ya = _dot_nt(cat([dob[:c], dvn_a.astype(BF)]), s_a.astype(BF))
    yb = _dot_nt(cat([dob[c:], dvn_b.astype(BF)]), s_b.astype(BF))
    dqd = cat([ya[:c], yb[:c]])
    dw = -cat([ya[c:], yb[c:]])
    dvn = cat([dvn_a, dvn_b])
    dkd = cat([dkd_a, dkd_b])
    dq = dqd * e
    dgc = rsum(dqd * qd) - rsum(dkd * kd)
    dk = dkd * loc["edec"]
    dpm = jnp.where(loc["incl"], _dot_nt(dob, vnb), 0.0)
    duw = cat([dvn, dw], 1).astype(BF)
    dt = _dot_nt(duw, cat([vb, kbe], 1).astype(BF))
    tt = _dot_tn(tm, duw)
    dvb, dkbe = tt[:, :LANES], tt[:, LANES:]
    tn_dims = (((0,), (0,)), ((), ()))
    nt_dims = (((1,), (1,)), ((), ()))
    da = jnp.where(loc["strict"], -_dot3(_dot3(tf, dt, tn_dims), tf, nt_dims), 0.0)
    st = cat([da * dm, dpm * dm]).astype(BF)
    z = _dot(st, kv.astype(BF))
    dkb = z[:PAIR] + dkbe * e
    dq = dq + z[PAIR:]
    dk = dk + _dot_tn(st, cat([kb, qv]).astype(BF))
    gmat = dpm * pmat + da * amat
    dgc = dgc + rsum(dkbe * kbe) + rsum(gmat)
    ridx = loc["ridx"]
    dgc = dgc + jnp.where(ridx == c - 1, dgl_a, 0.0) + jnp.where(ridx == PAIR - 1, dgl_b, 0.0)
    dgc_row = jnp.sum(gmat, axis=0, keepdims=True)
    db = rsum(dvb * vv) + rsum(dkb * kv)
    return dq, dk + dkb * bv, dvb * bv, dgg, dgc, dgc_row, db, dwn, ds_new


def _gdn_bwd(q, k, v, proj, gates, grow, wn, tinv_all, states_all, dy, nb, seq, name):
    t = q.shape[0]
    n, seg, nseg, sp, blk, gg, gates_spec, rowb, per_pair, dgates = _gdn_specs(nb, seq, True)
    dh = GDN_HEAD_DIM

    def body(q_ref, k_ref, v_ref, gg_ref, gt_ref, gr_ref, wn_ref, tn_ref, sn_ref, dy_ref,
             dq_ref, dk_ref, dv_ref, dgg_ref, dgt_ref, dwn_ref, ds_ref):
        hp = pl.program_id(1)

        @pl.when((pl.program_id(0) == 0) & (hp == 0) & (pl.program_id(2) == 0))
        def _():
            dwn_ref[...] = jnp.zeros_like(dwn_ref)

        @pl.when(pl.program_id(2) == 0)
        def _():
            ds_ref[...] = jnp.zeros_like(ds_ref)

        wnv = wn_ref[...]
        lane = lax.broadcasted_iota(jnp.int32, (PAIR, LANES), 1)

        def bwd_step(j, carry):
            pi = sp - 1 - j
            rows = pl.ds(pl.multiple_of(pi * PAIR, PAIR), PAIR)
            gt = gt_ref[rows, :]
            ins = _gdn_head_inputs((q_ref, k_ref, v_ref), gt, gr_ref, rows, pi, hp, lane)
            halves = [slice(hh * LANES, (hh + 1) * LANES) for hh in range(HEADS_PER_STEP)]
            saved = [jnp.stack([r[hh, pi] for hh in range(HEADS_PER_STEP)]) for r in (tn_ref, sn_ref)]
            g2 = jnp.stack([gg_ref[rows, cols] for cols in halves])
            dy2 = jnp.stack([dy_ref[rows, cols].astype(F32) for cols in halves])
            dq, dk, dv, dgg, dgc, dgc_row, db, dwn, ds_new = jax.vmap(
                _gdn_pair_bwd, in_axes=(0,) * 11 + (None,))(*ins, *saved, ds_ref[...], g2, dy2, wnv)
            ds_ref[...] = ds_new
            dgt = jnp.zeros((PAIR, LANES), F32)
            for hh, cols in enumerate(halves):
                head = HEADS_PER_STEP * hp + hh
                dq_ref[rows, cols] = dq[hh]
                dk_ref[rows, cols] = dk[hh]
                dv_ref[rows, cols] = dv[hh]
                dgg_ref[rows, cols] = dgg[hh].astype(BF)
                dwn_ref[...] += dwn[hh]
                row_as_col = jnp.transpose(jnp.broadcast_to(dgc_row[hh], (PAIR, LANES)))
                dgt = (dgt + jnp.where(lane == A_LANE + head, dgc[hh] - row_as_col, 0.0)
                       + jnp.where(lane == B_LANE + head, db[hh], 0.0))
            dgt_ref[rows, :] = dgt
            return carry

        lax.fori_loop(0, sp, bwd_step, 0)

    return pl.pallas_call(
        body, name=name, grid=(nb, GDN_HEADS // HEADS_PER_STEP, nseg),
        in_specs=[blk, blk, blk, gg, gates_spec, rowb, _full((1, LANES)), per_pair, per_pair, blk],
        out_specs=[blk, blk, blk, blk, dgates, _full((1, LANES))],
        out_shape=[jax.ShapeDtypeStruct((t, GDN_WIDTH), F32)] * 3 + [
            jax.ShapeDtypeStruct((t, GDN_WIDTH), BF),
            jax.ShapeDtypeStruct((nb * HEAD_GROUPS, seq, LANES), F32),
            jax.ShapeDtypeStruct((1, LANES), F32)],
        scratch_shapes=[pltpu.VMEM((HEADS_PER_STEP, dh, dh), F32)],
        compiler_params=_params(("arbitrary", "arbitrary", "arbitrary")),
    )(q, k, v, proj, gates, grow, wn, tinv_all, states_all, dy)


def _mix_to_padded(w):
    pad = jnp.zeros(w.shape[:-1] + (N_PAD - N_IN,), w.dtype)
    return jnp.concatenate([w[..., 0:1536], w[..., 1544:3080], w[..., 3088:3600], w[..., 1536:1544],
                            w[..., 3080:3088], pad], axis=-1)


def _mix_from_padded(g):
    return jnp.concatenate([g[..., 0:1536], g[..., 3584:3592], g[..., 1536:3072], g[..., 3592:3600],
                            g[..., 3072:3584]], axis=-1)


def _pad_lanes(vec, start):
    return jnp.pad(vec[None, :], ((0, 0), (start, LANES - start - vec.shape[0])))


def _heads_to_rows(block, lane0, nheads, nb, seq):
    return block[:, lane0:lane0 + nheads].reshape(nb, seq, nheads).transpose(0, 2, 1).reshape(nb * nheads, seq)


def _mixer_small(p, l):
    wq_t = jnp.tile(p["fox_q_norm"][l], FOX_HEADS)[None, :]
    wk_t = jnp.tile(p["fox_k_norm"][l], FOX_HEADS)[None, :]
    bias = _pad_lanes(p["fox_f_bias"][l], 0)
    a_pad = _pad_lanes(p["gdn_a_log"][l], A_LANE)
    dt_pad = _pad_lanes(p["gdn_dt_bias"][l], A_LANE)
    wn = p["gdn_out_norm"][l][None, :]
    return wq_t, wk_t, bias, a_pad, dt_pad, wn


def _layer_fwd(x, p, l, nb, seq, rider=None, ffn1_rider=None, after_ffn1=None):
    npair = FOX_HEADS // 2
    n = seq // CHUNK
    x1, h1, *rode1 = _ffn_fwd(x, p["ffn1_norm"][l][None, :], p["ffn1_w_in"][l], p["ffn1_w_out"][l],
                              f"ffn1_fwd_{l}", ffn1_rider)
    if after_ffn1 is not None:
        after_ffn1(rode1)
    wq_t, wk_t, bias, a_pad, dt_pad, wn = _mixer_small(p, l)
    proj = _norm_matmul(x1, p["mix_norm"][l][None, :], p["w_mix"][l], f"mix_in_{l}")
    fq, fk, fv, cum = _fox_prep(proj, wq_t, wk_t, bias, seq, f"fox_prep_{l}")
    c8 = _heads_to_rows(cum, 0, FOX_HEADS, nb, seq)
    ck = c8.reshape(nb * npair, 2, 1, seq)
    o, lse, *rode = _fox_attn(fq, fk, fv, ck, nb, seq, f"fox_attn_{l}", rider)
    gq, gk, gv, gates = _gdn_prep(proj, p["gdn_conv"][l], a_pad, dt_pad, seq, f"gdn_prep_{l}")
    gc4 = _heads_to_rows(gates, A_LANE, GDN_HEADS, nb, seq)
    grow = jnp.broadcast_to(gc4.reshape(nb * HEAD_GROUPS, HEADS_PER_STEP, n // 2, 1, PAIR),
                            (nb * HEAD_GROUPS, HEADS_PER_STEP, n // 2, HALO, PAIR))
    y, tinv, states = _gdn_fwd(gq, gk, gv, proj, gates, grow, wn, nb, seq, f"gdn_fwd_{l}")
    x2 = _mix_out(x1, o, y, p["w_out"][l], f"mix_out_{l}")
    x3, h2 = _ffn_fwd(x2, p["ffn2_norm"][l][None, :], p["ffn2_w_in"][l], p["ffn2_w_out"][l], f"ffn2_fwd_{l}")
    saved = dict(x=x, h1=h1, x1=x1, proj=proj, fq=fq, fk=fk, fv=fv, ck=ck, o=o, lse=lse,
                 gq=gq, gk=gk, gv=gv, gates=gates, grow=grow, tinv=tinv, states=states, y=y, x2=x2, h2=h2)
    return x3, saved, rode


def _ffn_grads(dy, x, h, gain, win, wout, l, tag, rider=None):
    t, d = x.shape
    fs = win.shape[2]
    dx, dh, a, hn, dyh, dgain, *rode = _ffn_bwd(dy, x, h, gain, win, wout, f"{tag}_bwd_{l}", rider)
    g_in = _wgrad(hn, dh, jax.ShapeDtypeStruct((4, d, fs), BF),
                  pl.BlockSpec((None, d, fs), lambda i, j, k: (j, i, 0)), d, fs, f"{tag}_gw_in_{l}")
    g_out = _wgrad(a, dyh, jax.ShapeDtypeStruct((2 * fs, d), BF),
                   pl.BlockSpec((fs, d), lambda i, j, k: (i, j)), fs, d, f"{tag}_gw_out_{l}")
    return dx, dgain[0], g_in, g_out.reshape(4, fs // 2, d), rode


def _layer_bwd(dx3, p, l, sv, nb, seq, rider=None, before_ffn1=None):
    npair = FOX_HEADS // 2
    d = dx3.shape[1]
    wq_t, wk_t, bias, a_pad, dt_pad, wn = _mixer_small(p, l)
    g = {}
    dx2, g["ffn2_norm"], g["ffn2_w_in"], g["ffn2_w_out"], _ = _ffn_grads(
        dx3, sv["x2"], sv["h2"], p["ffn2_norm"][l][None, :], p["ffn2_w_in"][l], p["ffn2_w_out"][l], l, "ffn2")
    dyf, dyg, dxb = _mix_out_bwd(dx2, p["w_out"][l], f"mix_out_bwd_{l}")
    half = lambda a, nm: _wgrad(a, dxb, jax.ShapeDtypeStruct((FOX_WIDTH, d), BF),
                                pl.BlockSpec((FOX_WIDTH, d), lambda i, j, k: (i, j)), FOX_WIDTH, d, nm)
    g["w_out"] = jnp.concatenate([half(sv["o"], f"gw_out_fox_{l}"), half(sv["y"], f"gw_out_gdn_{l}")], axis=0)
    dqa, dqb, dk, dv, dkx, *rode = _fox_attn_bwd(sv["fq"], sv["fk"], sv["fv"], sv["o"], dyf, sv["lse"], sv["ck"],
                                                 nb, seq, f"fox_attn_bwd_{l}", rider)

    def sums(first_head, second_head):
        a = first_head.reshape(nb * seq, npair, LANES)[:, :, FOX_HEAD_DIM]
        b = second_head.reshape(nb * seq, npair, LANES)[:, :, 0]
        return jnp.stack([a, b], axis=2).reshape(nb * seq, FOX_HEADS)

    dcum = jnp.pad(sums(dqa, dqb) - sums(dkx, dkx), ((0, 0), (0, LANES - FOX_HEADS)))
    dpf, dff, dwq, dwk, dbias = _fox_prep_bwd(sv["proj"], dqa, dqb, dk, dv, dcum, wq_t, wk_t, bias, seq,
                                              f"fox_prep_bwd_{l}")
    g["fox_q_norm"] = dwq[0, :FOX_HEAD_DIM]
    g["fox_k_norm"] = dwk[0, :FOX_HEAD_DIM]
    g["fox_f_bias"] = dbias[0, :FOX_HEADS]
    dgq, dgk, dgv, dgg, dgt, dwn = _gdn_bwd(
        sv["gq"], sv["gk"], sv["gv"], sv["proj"], sv["gates"], sv["grow"], wn, sv["tinv"], sv["states"],
        dyg, nb, seq, f"gdn_bwd_{l}")
    dgates = jnp.sum(dgt.reshape(nb, HEAD_GROUPS, seq, LANES), axis=1).reshape(nb * seq, LANES)
    dpg, dgate_blk, dconv, da, ddt = _gdn_prep_bwd(sv["proj"], dgq, dgk, dgv, dgates, dff, p["gdn_conv"][l],
                                                   a_pad, dt_pad, seq, f"gdn_prep_bwd_{l}")
    g["gdn_conv"] = dconv
    g["gdn_a_log"] = da[0, A_LANE:B_LANE]
    g["gdn_dt_bias"] = ddt[0, A_LANE:B_LANE]
    g["gdn_out_norm"] = dwn[0]
    dproj = jnp.concatenate([dpf, dpg, dgg, dgate_blk], axis=1)
    dx1, hnm, dgm = _norm_matmul_bwd(dx2, dproj, sv["x1"], p["mix_norm"][l][None, :], p["w_mix"][l],
                                     f"mix_in_bwd_{l}")
    g["mix_norm"] = dgm[0]
    g["w_mix"] = _wgrad(hnm, dproj, jax.ShapeDtypeStruct((d, N_PAD), F32),
                        pl.BlockSpec((d // 2, N_PAD), lambda i, j, k: (i, j)), d // 2, N_PAD, f"gw_mix_{l}")
    ffn1_rider = before_ffn1(g) if before_ffn1 is not None else None
    dx0, g["ffn1_norm"], g["ffn1_w_in"], g["ffn1_w_out"], rode1 = _ffn_grads(
        dx1, sv["x"], sv["h1"], p["ffn1_norm"][l][None, :], p["ffn1_w_in"][l], p["ffn1_w_out"][l], l, "ffn1",
        ffn1_rider)
    return dx0, g, rode, rode1


def _local_step(x, target, p):
    nb, seq, d = x.shape
    xt = x.reshape(nb * seq, d)
    saved = []
    for l in range(DEPTH):
        xt, sv, _ = _layer_fwd(xt, p, l, nb, seq)
        saved.append(sv)
    loss, dx = _loss_grad(xt, target.reshape(nb * seq, d), "loss")
    grads = [None] * DEPTH
    for l in reversed(range(DEPTH)):
        dx, grads[l], _, _ = _layer_bwd(dx, p, l, saved[l], nb, seq)
    return loss, dx.reshape(nb, seq, d), grads


N_CHIPS = 4


def _mesh_pos():
    return lax.axis_index("x"), lax.axis_index("y"), lax.axis_index("c")


def _other_chips(x, y):
    return [(1 - x, y), (x, 1 - y), (1 - x, 1 - y)]


def _remote(src, dst, send_sem, recv_sem, to):
    return pltpu.make_async_remote_copy(src_ref=src, dst_ref=dst, send_sem=send_sem, recv_sem=recv_sem,
                                        device_id=to, device_id_type=MESH)


def _hbm_call(body, name, ins, out_shape, scratch):
    return pl.pallas_call(
        body, name=name, out_shape=out_shape, in_specs=[HBM] * len(ins),
        out_specs=jax.tree.map(lambda _: HBM, out_shape), scratch_shapes=scratch,
        compiler_params=pltpu.CompilerParams(has_side_effects=True),
    )(*ins)


def _gather_phases(n, layer):
    def copies(ins, outs, sems):
        send1, recv1, send2, recv2 = sems
        x, y, c = _mesh_pos()
        out, back, fwd = [], [], []
        for i in range(n):
            for j, (px, py) in enumerate(_other_chips(x, y)):
                k = 3 * i + j
                blk = outs[i].at[2 * px + py]
                out.append(_remote(ins[i], outs[i].at[2 * x + y], send1.at[k], recv1.at[k], (px, py, c)))
                back.append(_remote(blk, blk, send1.at[k], recv1.at[k], (px, py, c)))
                fwd.append(_remote(blk, blk, send2.at[k], recv2.at[k], (x, y, 1 - c)))
        return c, out, back, fwd

    def first(ins, outs, sems):
        c, out, _, _ = copies(ins, outs, sems)

        @pl.when(c == layer)
        def _():
            for cp in out:
                cp.start()

    def middle(ins, outs, sems):
        c, _, back, fwd = copies(ins, outs, sems)

        @pl.when(c == layer)
        def _():
            for arrived, onward in zip(back, fwd):
                arrived.wait_recv()
                onward.start()

    def last(ins, outs, sems):
        c, out, _, fwd = copies(ins, outs, sems)

        @pl.when(c == layer)
        def _():
            for cp in out + fwd:
                cp.wait_send()

        @pl.when(c != layer)
        def _():
            for cp in fwd:
                cp.wait_recv()

    return first, middle, last


def _scatter_phases(n, layer):
    def copies(ins, outs, sems):
        send, recv = sems
        x, y, c = _mesh_pos()
        return c, [_remote(ins[i].at[2 * px + py], outs[i].at[j], send.at[3 * i + j], recv.at[3 * i + j], (px, py, c))
                   for i in range(n) for j, (px, py) in enumerate(_other_chips(x, y))]

    def first(ins, outs, sems):
        c, cps = copies(ins, outs, sems)

        @pl.when(c == layer)
        def _():
            for cp in cps:
                cp.start()

    def middle(ins, outs, sems):
        pass

    def last(ins, outs, sems):
        c, cps = copies(ins, outs, sems)

        @pl.when(c == layer)
        def _():
            for cp in cps:
                cp.wait()

    return first, middle, last


def _exchange(blocks, out_shapes, n_sems, phases, name, rider):
    sems = [pltpu.SemaphoreType.DMA((3 * len(blocks),))] * n_sems
    if rider:
        return _Rider(blocks, out_shapes, sems, phases)
    n = len(blocks)

    def body(*refs):
        for phase in phases:
            phase(refs[:n], refs[n:2 * n], refs[2 * n:])

    return list(_hbm_call(body, name, blocks, out_shapes, sems))


def _gather_layer(blocks, layer, name=None, rider=False):
    outs = [jax.ShapeDtypeStruct((N_CHIPS,) + b.shape, b.dtype) for b in blocks]
    return _exchange(blocks, outs, 4, _gather_phases(len(blocks), layer), name, rider)


def _scatter_layer(sums, layer, name=None, rider=False):
    outs = [jax.ShapeDtypeStruct((3,) + s.shape[1:], s.dtype) for s in sums]
    return _exchange(sums, outs, 2, _scatter_phases(len(sums), layer), name, rider)


def _to_sibling(gs, layer, name):
    n = len(gs)

    def body(*refs):
        ins, outs = refs[:n], refs[n:2 * n]
        send, recv = refs[2 * n:]
        x, y, c = _mesh_pos()
        cps = [_remote(ins[i], outs[i], send.at[i], recv.at[i], (x, y, 1 - c)) for i in range(n)]

        @pl.when(c != layer)
        def _():
            for cp in cps:
                cp.start()
            for cp in cps:
                cp.wait_send()

        @pl.when(c == layer)
        def _():
            for cp in cps:
                cp.wait_recv()

    sem = pltpu.SemaphoreType.DMA((n,))
    return list(_hbm_call(body, name, gs, [jax.ShapeDtypeStruct(g.shape, g.dtype) for g in gs], [sem, sem]))


def _sibling_swap(rs, name):
    n = len(rs)

    def body(*refs):
        ins, outs = refs[:n], refs[n:2 * n]
        send, recv = refs[2 * n:]
        x, y, c = _mesh_pos()
        cps = [_remote(ins[i], outs[i], send.at[i], recv.at[i], (x, y, 1 - c)) for i in range(n)]
        for cp in cps:
            cp.start()
        for cp in cps:
            cp.wait()

    sem = pltpu.SemaphoreType.DMA((n,))
    return _hbm_call(body, name, rs, [jax.ShapeDtypeStruct(r.shape, r.dtype) for r in rs], [sem, sem])


def _small_all_reduce(vec, name):
    r = vec.shape[0]
    ndev = 8

    def body(v_ref, o_ref, buf, send, recv):
        x, y, c = _mesh_pos()
        me = 4 * x + 2 * y + c
        buf[me] = v_ref[...]
        cps = []
        for rel in range(1, ndev):
            px = 1 - x if rel & 4 else x
            py = 1 - y if rel & 2 else y
            pc = 1 - c if rel & 1 else c
            cps.append((_remote(v_ref, buf.at[me], send.at[rel - 1], recv.at[rel - 1], (px, py, pc)),
                        4 * px + 2 * py + pc))
        for cp, _ in cps:
            cp.start()
        for k, (cp, peer) in enumerate(cps):
            slot = buf.at[peer]
            _remote(slot, slot, send.at[k], recv.at[k], (x, y, c)).wait_recv()
        for cp, _ in cps:
            cp.wait_send()
        acc = buf[0]
        for k in range(1, ndev):
            acc = acc + buf[k]
        o_ref[...] = acc

    vm = pl.BlockSpec(memory_space=pltpu.VMEM)
    return pl.pallas_call(
        body, name=name, out_shape=jax.ShapeDtypeStruct(vec.shape, F32), in_specs=[vm], out_specs=vm,
        scratch_shapes=[pltpu.VMEM((ndev, r, LANES), F32), pltpu.SemaphoreType.DMA((ndev - 1,)),
                        pltpu.SemaphoreType.DMA((ndev - 1,))],
        compiler_params=pltpu.CompilerParams(has_side_effects=True),
    )(vec)


def _row_tile(rows, cap=512):
    for t in range(min(rows, cap), 0, -1):
        if rows % t == 0 and (t % 16 == 0 or t == rows):
            return t
    raise ValueError(rows)


def _add_pairs(a, b, name):
    k, r, c = a.shape
    tr = _row_tile(r)

    def body(a_ref, b_ref, o_ref):
        o_ref[...] = (a_ref[...].astype(F32) + b_ref[...].astype(F32)).astype(o_ref.dtype)

    spec = pl.BlockSpec((None, tr, c), lambda i, j: (i, j, 0))
    return pl.pallas_call(body, name=name, grid=(k, r // tr), in_specs=[spec, spec], out_specs=spec,
                          out_shape=jax.ShapeDtypeStruct(a.shape, a.dtype),
                          compiler_params=_params(("parallel", "parallel")))(a, b)


def _final_sum(own, sib, others, name):
    r, c = own.shape
    tr = _row_tile(r)

    def body(a_ref, b_ref, o_ref_in, out_ref):
        acc = a_ref[...].astype(F32) + b_ref[...].astype(F32)
        for k in range(3):
            acc = acc + o_ref_in[k].astype(F32)
        out_ref[...] = acc

    spec = pl.BlockSpec((tr, c), lambda i: (i, 0))
    return pl.pallas_call(body, name=name, grid=(r // tr,),
                          in_specs=[spec, spec, pl.BlockSpec((3, tr, c), lambda i: (0, i, 0))], out_specs=spec,
                          out_shape=jax.ShapeDtypeStruct((r, c), F32),
                          compiler_params=_params(("parallel",)))(own, sib, others)


def _adamw(g, w, m, v, name):
    r, c = g.shape
    tr = _row_tile(r, 256)

    def body(g_ref, w_ref, m_ref, v_ref, d_ref, mo_ref, vo_ref):
        gv = g_ref[...]
        mn = ADAM_B1 * m_ref[...] + (1.0 - ADAM_B1) * gv
        vn = ADAM_B2 * v_ref[...] + (1.0 - ADAM_B2) * (gv * gv)
        m_hat = mn / (1.0 - ADAM_B1 ** ADAM_STEP)
        v_hat = vn / (1.0 - ADAM_B2 ** ADAM_STEP)
        d_ref[...] = -ADAM_LR * (m_hat / (jnp.sqrt(v_hat) + ADAM_EPS) + ADAM_WD * w_ref[...])
        mo_ref[...] = mn
        vo_ref[...] = vn

    spec = pl.BlockSpec((tr, c), lambda i: (i, 0))
    shp = jax.ShapeDtypeStruct((r, c), F32)
    return pl.pallas_call(body, name=name, grid=(r // tr,), in_specs=[spec] * 4, out_specs=[spec] * 3,
                          out_shape=[shp] * 3, compiler_params=_params(("parallel",)))(g, w, m, v)


def _pack(arrays):
    flat = jnp.concatenate([a.reshape(-1).astype(F32) for a in arrays])
    pad = (-flat.shape[0]) % (8 * LANES)
    return jnp.concatenate([flat, jnp.zeros((pad,), F32)]).reshape(-1, LANES)


def _unpack(packed, shapes):
    flat = packed.reshape(-1)
    out, off = [], 0
    for s in shapes:
        size = 1
        for dim in s:
            size *= dim
        out.append(flat[off:off + size].reshape(s))
        off += size
    return out


BIG = ("ffn1_w_in", "ffn1_w_out", "w_in", "w_out", "ffn2_w_in", "ffn2_w_out")
SMALL = ("ffn1_norm", "mix_norm", "fox_q_norm", "fox_k_norm", "fox_f_bias", "gdn_a_log", "gdn_dt_bias",
         "gdn_out_norm", "ffn2_norm", "gdn_conv")
WEIGHTS = ("ffn1_norm", "ffn1_w_in", "ffn1_w_out", "mix_norm", "w_in", "fox_q_norm", "fox_k_norm", "fox_f_bias",
           "gdn_conv", "gdn_a_log", "gdn_dt_bias", "gdn_out_norm", "w_out", "ffn2_norm", "ffn2_w_in", "ffn2_w_out")


def _step(x, target, w, m, v):
    xi, yi, ci = _mesh_pos()
    me = 2 * xi + yi
    depth = DEPTH
    d = x.shape[-1]

    nb, seq, _ = x.shape
    assert depth == 2

    p = {k: w[k] for k in SMALL if k != "gdn_conv"}
    for k in ("ffn1_w_in", "ffn1_w_out", "ffn2_w_in", "ffn2_w_out", "w_mix", "w_out", "gdn_conv"):
        p[k] = [None] * depth

    first, rest = BIG[:2], BIG[2:] + ("gdn_conv",)

    def shards(l, names):
        return [w[k][l] if k == "gdn_conv" else w[k][l].astype(BF) for k in names]

    def place(l, names, gathered):
        blocks = dict(zip(names, [lax.dynamic_update_index_in_dim(g, s, me, 0)
                                  for g, s in zip(gathered, shards(l, names))]))
        for k in ("ffn1_w_in", "ffn1_w_out", "ffn2_w_in", "ffn2_w_out"):
            if k in blocks:
                p[k][l] = blocks[k]
        if "w_in" in blocks:
            p["w_mix"][l] = _mix_to_padded(blocks["w_in"].transpose(1, 0, 2).reshape(d, N_IN))
            p["w_out"][l] = blocks["w_out"].reshape(2 * FOX_WIDTH, d)
            p["gdn_conv"][l] = blocks["gdn_conv"].transpose(1, 0, 2).reshape(CONV_WIDTH, -1)

    place(0, first, _gather_layer(shards(0, first), 0, "gather_first_ffn0"))
    xt = x.reshape(nb * seq, d)
    xt, saved0, gathered1 = _layer_fwd(
        xt, p, 0, nb, seq, _gather_layer(shards(1, first + rest), 1, rider=True),
        _gather_layer(shards(0, rest), 0, rider=True), lambda got: place(0, rest, got))
    place(1, first + rest, gathered1)
    xt, saved1, _ = _layer_fwd(xt, p, 1, nb, seq)
    loss, dx = _loss_grad(xt, target.reshape(nb * seq, d), "loss")

    def transport(g, names):
        out = []
        for k in names:
            if k == "w_in":
                full = _mix_from_padded(g["w_mix"])
                out.append(full.reshape(d, N_CHIPS, N_IN // N_CHIPS).transpose(1, 0, 2).astype(BF))
            elif k == "w_out":
                out.append(g["w_out"].reshape(N_CHIPS, -1, d))
            else:
                out.append(g[k])
        return out

    def chip_sums(g, l, names, tag):
        own = transport(g, names)
        sib = _to_sibling(own, l, f"grad{l}{tag}_to_sibling")
        return own, sib, [_add_pairs(a, b, f"grad{l}{tag}_chip_sum_{k}") for a, b, k in zip(own, sib, names)]

    dx, grads1, _, _ = _layer_bwd(dx, p, 1, saved1, nb, seq)
    own1, sib1, sums1 = chip_sums(grads1, 1, BIG, "")
    before = {}

    def before_ffn1(g):
        before["own"], before["sib"], sums = chip_sums(g, 0, BIG[2:], "_rest")
        return _scatter_layer(sums, 0, rider=True)

    dx, grads0, chips1, chips0_rest = _layer_bwd(dx, p, 0, saved0, nb, seq,
                                                 _scatter_layer(sums1, 1, rider=True), before_ffn1)
    own0, sib0, sums0 = chip_sums(grads0, 0, first, "_first")
    chips0 = _scatter_layer(sums0, 0, "grad0_first_to_chips") + chips0_rest
    own0, sib0 = own0 + before["own"], sib0 + before["sib"]
    grads = [grads0, grads1]
    dx = dx.reshape(nb, seq, d)

    mine = lambda a0, a1: jnp.where(ci == 0, a0, a1)
    at_me = lambda a: lax.dynamic_index_in_dim(a, me, 0, keepdims=False)
    reduced = [_final_sum(mine(at_me(own0[i]), at_me(own1[i])), mine(at_me(sib0[i]), at_me(sib1[i])),
                          mine(chips0[i], chips1[i]), f"grad_final_sum_{k}") for i, k in enumerate(BIG)]
    from_sib_final = _sibling_swap(reduced, "grad_swap_layers")
    full = {k: jnp.stack([jnp.where(ci == 0, a, b), jnp.where(ci == 0, b, a)])
            for k, a, b in zip(BIG, reduced, from_sib_final)}

    out_g, out_d, out_m, out_v = {}, {}, {}, {}
    for k in BIG:
        shp = w[k].shape
        two_d = lambda a: a.reshape(shp[0] * shp[1], shp[2])
        dl, mn, vn = _adamw(two_d(full[k]), two_d(w[k]), two_d(m[k]), two_d(v[k]), f"adamw_{k}")
        out_g[k], out_d[k], out_m[k], out_v[k] = full[k], dl.reshape(shp), mn.reshape(shp), vn.reshape(shp)

    small_local = [jnp.stack([grads[l][k] for l in range(depth)]) for k in SMALL]
    summed = _unpack(_small_all_reduce(_pack(small_local), "small_all_reduce"), [a.shape for a in small_local])
    sg = dict(zip(SMALL, summed))
    cs = w["gdn_conv"].shape[-1]
    sg["gdn_conv"] = lax.dynamic_slice_in_dim(sg["gdn_conv"], me * cs, cs, axis=2)
    shapes = [w[k].shape for k in SMALL]
    packs = [_pack([src[k] for k in SMALL]) for src in (sg, w, m, v)]
    dl, mn, vn = _adamw(*packs, "adamw_small")
    for k, a, b, c2 in zip(SMALL, _unpack(dl, shapes), _unpack(mn, shapes), _unpack(vn, shapes)):
        out_g[k], out_d[k], out_m[k], out_v[k] = sg[k], a, b, c2

    total = lax.psum(loss[0, 0], ("x", "y", "c"))
    return (total, dx, *[out_g[k] for k in WEIGHTS], *[out_d[k] for k in WEIGHTS],
            *[out_m[k] for k in WEIGHTS], *[out_v[k] for k in WEIGHTS])


def kernel(x, ffn1_norm, ffn1_w_in, ffn1_w_out, mix_norm, w_in, fox_q_norm, fox_k_norm, fox_f_bias, gdn_conv, gdn_a_log, gdn_dt_bias, gdn_out_norm, w_out, ffn2_norm, ffn2_w_in, ffn2_w_out, loss_target, m_ffn1_norm, m_ffn1_w_in, m_ffn1_w_out, m_mix_norm, m_w_in, m_fox_q_norm, m_fox_k_norm, m_fox_f_bias, m_gdn_conv, m_gdn_a_log, m_gdn_dt_bias, m_gdn_out_norm, m_w_out, m_ffn2_norm, m_ffn2_w_in, m_ffn2_w_out, v_ffn1_norm, v_ffn1_w_in, v_ffn1_w_out, v_mix_norm, v_w_in, v_fox_q_norm, v_fox_k_norm, v_fox_f_bias, v_gdn_conv, v_gdn_a_log, v_gdn_dt_bias, v_gdn_out_norm, v_w_out, v_ffn2_norm, v_ffn2_w_in, v_ffn2_w_out):
    w = dict(ffn1_norm=ffn1_norm, ffn1_w_in=ffn1_w_in, ffn1_w_out=ffn1_w_out, mix_norm=mix_norm, w_in=w_in,
             fox_q_norm=fox_q_norm, fox_k_norm=fox_k_norm, fox_f_bias=fox_f_bias, gdn_conv=gdn_conv,
             gdn_a_log=gdn_a_log, gdn_dt_bias=gdn_dt_bias, gdn_out_norm=gdn_out_norm, w_out=w_out,
             ffn2_norm=ffn2_norm, ffn2_w_in=ffn2_w_in, ffn2_w_out=ffn2_w_out)
    m = dict(ffn1_norm=m_ffn1_norm, ffn1_w_in=m_ffn1_w_in, ffn1_w_out=m_ffn1_w_out, mix_norm=m_mix_norm, w_in=m_w_in,
             fox_q_norm=m_fox_q_norm, fox_k_norm=m_fox_k_norm, fox_f_bias=m_fox_f_bias, gdn_conv=m_gdn_conv,
             gdn_a_log=m_gdn_a_log, gdn_dt_bias=m_gdn_dt_bias, gdn_out_norm=m_gdn_out_norm, w_out=m_w_out,
             ffn2_norm=m_ffn2_norm, ffn2_w_in=m_ffn2_w_in, ffn2_w_out=m_ffn2_w_out)
    v = dict(ffn1_norm=v_ffn1_norm, ffn1_w_in=v_ffn1_w_in, ffn1_w_out=v_ffn1_w_out, mix_norm=v_mix_norm, w_in=v_w_in,
             fox_q_norm=v_fox_q_norm, fox_k_norm=v_fox_k_norm, fox_f_bias=v_fox_f_bias, gdn_conv=v_gdn_conv,
             gdn_a_log=v_gdn_a_log, gdn_dt_bias=v_gdn_dt_bias, gdn_out_norm=v_gdn_out_norm, w_out=v_w_out,
             ffn2_norm=v_ffn2_norm, ffn2_w_in=v_ffn2_w_in, ffn2_w_out=v_ffn2_w_out)
    return _step(x, loss_target, w, m, v)
```

```python
import jax
import jax.numpy as jnp
from jax import lax
from jax.experimental import pallas as pl
from jax.experimental.pallas import tpu as pltpu

F32 = jnp.float32
BF = jnp.bfloat16
HI = lax.Precision.HIGHEST
MESH = pl.DeviceIdType.MESH

DEPTH = 2
FOX_HEADS = 8
FOX_HEAD_DIM = 64
FOX_WIDTH = 512
GDN_HEADS = 4
GDN_HEAD_DIM = 128
GDN_WIDTH = 512
CONV_WIDTH = 4
CHUNK = 64
EPS = 1e-6
N_IN = 3600
N_PAD = 3712
GATE_COL = 3584
LANES = 128
NEG = -1e30

ADAM_LR = 0.001
ADAM_B1 = 0.9
ADAM_B2 = 0.999
ADAM_EPS = 1e-08
ADAM_WD = 0.01
ADAM_STEP = 10

VMEM_LIMIT = 56 * 1024 * 1024


def _params(sem=None, **kw):
    return pltpu.CompilerParams(dimension_semantics=sem, vmem_limit_bytes=VMEM_LIMIT, **kw)


def _dot(a, b, precision=None):
    return jnp.dot(a, b, preferred_element_type=F32, precision=precision)


def _dot_nt(a, b, precision=None):
    return lax.dot_general(a, b, (((1,), (1,)), ((), ())), preferred_element_type=F32, precision=precision)


def _dot_tn(a, b, precision=None):
    return lax.dot_general(a, b, (((0,), (0,)), ((), ())), preferred_element_type=F32, precision=precision)


def _sigmoid(x):
    return 0.5 * jnp.tanh(0.5 * x) + 0.5


def _softplus(x):
    return jnp.maximum(x, 0.0) + jnp.log(1.0 + jnp.exp(-jnp.abs(x)))


def _log_sigmoid(x):
    return jnp.minimum(x, 0.0) - jnp.log(1.0 + jnp.exp(-jnp.abs(x)))


def _tile(n, t):
    t = min(n, t)
    assert n % t == 0, (n, t)
    return t


def _rms_fwd(x, gain):
    rstd = lax.rsqrt(jnp.mean(x * x, axis=-1, keepdims=True) + EPS)
    xhat = x * rstd
    return xhat * gain, xhat, rstd


def _rms_bwd(dy, xhat, rstd, gain):
    dxhat = dy * gain
    dx = rstd * (dxhat - xhat * jnp.mean(dxhat * xhat, axis=-1, keepdims=True))
    return dx, dy * xhat


def _full(shape):
    nd = len(shape)
    return pl.BlockSpec(shape, lambda *_: (0,) * nd)


HBM = pl.BlockSpec(memory_space=pltpu.HBM)


def _load_ffn_weights(win_hbm, wout_hbm, win_v, wout_v, sem):
    fr = wout_hbm.shape[1]
    copies = [pltpu.make_async_copy(win_hbm.at[s], win_v.at[s], sem.at[s]) for s in range(4)]
    copies += [pltpu.make_async_copy(wout_hbm.at[s], wout_v.at[pl.ds(s * fr, fr)], sem.at[4 + s])
               for s in range(4)]
    for c in copies:
        c.start()
    for c in copies:
        c.wait()


def _ffn_fwd(x, gain, win_g, wout_g, name, rider=None):
    t, d = x.shape
    _, _, fs = win_g.shape
    fr = wout_g.shape[1]
    tm = _tile(t, 256)
    r_in, r_out, r_sem = _rider_parts(rider)
    steps = t // tm

    def body(x_ref, g_ref, win_hbm, wout_hbm, *rest):
        rin, (xo_ref, h_ref) = rest[:len(r_in)], rest[len(r_in):len(r_in) + 2]
        rout = rest[len(r_in) + 2:len(r_in) + 2 + len(r_out)]
        win_v, wout_v, sem = rest[len(r_in) + 2 + len(r_out):len(r_in) + 5 + len(r_out)]
        riding = (rin, rout, rest[len(r_in) + 5 + len(r_out):])
        step = pl.program_id(0)
        _ride(rider, 0, step == 0, riding)
        _ride(rider, 1, step == steps // 2, riding)

        @pl.when(step == 0)
        def _():
            _load_ffn_weights(win_hbm, wout_hbm, win_v, wout_v, sem)

        xv = x_ref[...]
        hn, _, _ = _rms_fwd(xv, g_ref[...])
        hn = hn.astype(BF)
        acc = jnp.zeros((tm, d), F32)
        for s in range(2):
            g = _dot(hn, win_v[s])
            u = _dot(hn, win_v[s + 2])
            h_ref[:, s * fs:(s + 1) * fs] = g.astype(BF)
            h_ref[:, (s + 2) * fs:(s + 3) * fs] = u.astype(BF)
            a = (g * _sigmoid(g) * u).astype(BF)
            acc = acc + _dot(a, wout_v[s * fs:(s + 1) * fs, :])
        xo_ref[...] = xv + 0.5 * acc
        _ride(rider, 2, step == steps - 1, riding)

    return pl.pallas_call(
        body, name=name, grid=(steps,),
        in_specs=[pl.BlockSpec((tm, d), lambda i: (i, 0)), _full((1, d)), HBM, HBM] + [HBM] * len(r_in),
        out_specs=[pl.BlockSpec((tm, d), lambda i: (i, 0)), pl.BlockSpec((tm, 4 * fs), lambda i: (i, 0))]
        + [HBM] * len(r_out),
        out_shape=[jax.ShapeDtypeStruct((t, d), F32), jax.ShapeDtypeStruct((t, 4 * fs), BF)] + r_out,
        scratch_shapes=[pltpu.VMEM((4, d, fs), BF), pltpu.VMEM((4 * fr, d), BF), pltpu.SemaphoreType.DMA((8,))]
        + r_sem,
        compiler_params=_params(("arbitrary",), has_side_effects=rider is not None),
    )(x, gain, win_g, wout_g, *r_in)


def _ffn_bwd(dy, x, h, gain, win_g, wout_g, name, rider=None):
    t, d = x.shape
    _, _, fs = win_g.shape
    fr = wout_g.shape[1]
    tm = _tile(t, 256)
    r_in, r_out, r_sem = _rider_parts(rider)
    steps = t // tm

    def body(dy_ref, x_ref, h_ref, g_ref, win_hbm, wout_hbm, *rest):
        rin, (dx_ref, dh_ref, a_ref, hn_ref, dyh_ref, dg_ref) = rest[:len(r_in)], rest[len(r_in):len(r_in) + 6]
        rout = rest[len(r_in) + 6:len(r_in) + 6 + len(r_out)]
        win_v, wout_v, sem = rest[len(r_in) + 6 + len(r_out):len(r_in) + 9 + len(r_out)]
        riding = (rin, rout, rest[len(r_in) + 9 + len(r_out):])
        step = pl.program_id(0)
        _ride(rider, 0, step == 0, riding)
        _ride(rider, 1, step == steps // 2, riding)

        @pl.when(step == 0)
        def _():
            _load_ffn_weights(win_hbm, wout_hbm, win_v, wout_v, sem)
            dg_ref[...] = jnp.zeros_like(dg_ref)

        dyv = dy_ref[...]
        dyh = (0.5 * dyv).astype(BF)
        dyh_ref[...] = dyh
        dhn = jnp.zeros((tm, d), F32)
        for s in range(2):
            da = _dot_nt(dyh, wout_v[s * fs:(s + 1) * fs, :])
            g = h_ref[:, s * fs:(s + 1) * fs].astype(F32)
            u = h_ref[:, (s + 2) * fs:(s + 3) * fs].astype(F32)
            sg = _sigmoid(g)
            si = g * sg
            a_ref[:, s * fs:(s + 1) * fs] = (si * u).astype(BF)
            dgate = (da * u * (sg * (1.0 + g * (1.0 - sg)))).astype(BF)
            dup = (da * si).astype(BF)
            dh_ref[:, s * fs:(s + 1) * fs] = dgate
            dh_ref[:, (s + 2) * fs:(s + 3) * fs] = dup
            dhn = dhn + _dot_nt(dgate, win_v[s]) + _dot_nt(dup, win_v[s + 2])
        xv = x_ref[...]
        gain_v = g_ref[...]
        hn, xhat, rstd = _rms_fwd(xv, gain_v)
        hn_ref[...] = hn.astype(BF)
        dx, dgr = _rms_bwd(dhn, xhat, rstd, gain_v)
        dx_ref[...] = dyv + dx
        dg_ref[...] += jnp.sum(dgr, axis=0, keepdims=True)
        _ride(rider, 2, step == steps - 1, riding)

    row = lambda w: pl.BlockSpec((tm, w), lambda i: (i, 0))
    return pl.pallas_call(
        body, name=name, grid=(steps,),
        in_specs=[row(d), row(d), row(4 * fs), _full((1, d)), HBM, HBM] + [HBM] * len(r_in),
        out_specs=[row(d), row(4 * fs), row(2 * fs), row(d), row(d), _full((1, d))] + [HBM] * len(r_out),
        out_shape=[jax.ShapeDtypeStruct((t, d), F32), jax.ShapeDtypeStruct((t, 4 * fs), BF),
                   jax.ShapeDtypeStruct((t, 2 * fs), BF), jax.ShapeDtypeStruct((t, d), BF),
                   jax.ShapeDtypeStruct((t, d), BF), jax.ShapeDtypeStruct((1, d), F32)] + r_out,
        scratch_shapes=[pltpu.VMEM((4, d, fs), BF), pltpu.VMEM((4 * fr, d), BF), pltpu.SemaphoreType.DMA((8,))]
        + r_sem,
        compiler_params=_params(("arbitrary",), has_side_effects=rider is not None),
    )(dy, x, h, gain, win_g, wout_g, *r_in)


def _wgrad(a, b, out_shape, out_spec, tm, tn, name, tk=512):
    t, m = a.shape
    _, n = b.shape
    tk = _tile(t, tk)
    nk = t // tk

    def body(a_ref, b_ref, o_ref, acc):
        k = pl.program_id(2)

        @pl.when(k == 0)
        def _():
            acc[...] = jnp.zeros_like(acc)

        acc[...] += _dot_tn(a_ref[...], b_ref[...])

        @pl.when(k == nk - 1)
        def _():
            o_ref[...] = acc[...].astype(o_ref.dtype)

    return pl.pallas_call(
        body, name=name, grid=(m // tm, n // tn, nk),
        in_specs=[pl.BlockSpec((tk, tm), lambda i, j, k: (k, i)), pl.BlockSpec((tk, tn), lambda i, j, k: (k, j))],
        out_specs=out_spec, out_shape=out_shape,
        scratch_shapes=[pltpu.VMEM((tm, tn), F32)],
        compiler_params=_params(("parallel", "parallel", "arbitrary")),
    )(a, b)


def _norm_matmul(x, gain, w, name):
    t, d = x.shape
    n = w.shape[1]
    tm = _tile(t, 256)

    def body(x_ref, g_ref, w_ref, o_ref):
        hn, _, _ = _rms_fwd(x_ref[...], g_ref[...])
        o_ref[...] = _dot(hn.astype(BF), w_ref[...])

    return pl.pallas_call(
        body, name=name, grid=(t // tm,),
        in_specs=[pl.BlockSpec((tm, d), lambda i: (i, 0)), _full((1, d)), _full((d, n))],
        out_specs=pl.BlockSpec((tm, n), lambda i: (i, 0)),
        out_shape=jax.ShapeDtypeStruct((t, n), F32),
        compiler_params=_params(("parallel",)),
    )(x, gain, w)


def _norm_matmul_bwd(dres, dparts, x, gain, w, name):
    t, d = x.shape
    n = w.shape[1]
    tm = _tile(t, 256)
    widths = [a.shape[1] for a in dparts]
    assert sum(widths) == n
    k = len(dparts)

    def body(dr_ref, *rest):
        dp_refs, (x_ref, g_ref, w_ref, dx_ref, hn_ref, dg_ref) = rest[:k], rest[k:]

        @pl.when(pl.program_id(0) == 0)
        def _():
            dg_ref[...] = jnp.zeros_like(dg_ref)

        dhn, off = jnp.zeros((tm, d), F32), 0
        for dp_ref, wd in zip(dp_refs, widths):
            dhn = dhn + _dot_nt(dp_ref[...], w_ref[:, off:off + wd])
            off += wd
        gain_v = g_ref[...]
        hn, xhat, rstd = _rms_fwd(x_ref[...], gain_v)
        hn_ref[...] = hn.astype(BF)
        dx, dgr = _rms_bwd(dhn, xhat, rstd, gain_v)
        dx_ref[...] = dr_ref[...] + dx
        dg_ref[...] += jnp.sum(dgr, axis=0, keepdims=True)

    row = lambda wd: pl.BlockSpec((tm, wd), lambda i: (i, 0))
    return pl.pallas_call(
        body, name=name, grid=(t // tm,),
        in_specs=[row(d)] + [row(wd) for wd in widths] + [row(d), _full((1, d)), _full((d, n))],
        out_specs=[row(d), row(d), _full((1, d))],
        out_shape=[jax.ShapeDtypeStruct((t, d), F32), jax.ShapeDtypeStruct((t, d), BF),
                   jax.ShapeDtypeStruct((1, d), F32)],
        compiler_params=_params(("arbitrary",)),
    )(dres, *dparts, x, gain, w)


def _mix_out(x, yf, yg, w, name):
    t, d = x.shape
    kf = yf.shape[1]
    tm = _tile(t, 512)

    def body(x_ref, yf_ref, yg_ref, w_ref, o_ref):
        o_ref[...] = x_ref[...] + _dot(yf_ref[...], w_ref[0:kf, :]) + _dot(yg_ref[...], w_ref[kf:2 * kf, :])

    row = lambda wd: pl.BlockSpec((tm, wd), lambda i: (i, 0))
    return pl.pallas_call(
        body, name=name, grid=(t // tm,),
        in_specs=[row(d), row(kf), row(kf), _full((2 * kf, d))],
        out_specs=row(d), out_shape=jax.ShapeDtypeStruct((t, d), F32),
        compiler_params=_params(("parallel",)),
    )(x, yf, yg, w)


def _mix_out_bwd(dx, w, name):
    t, d = dx.shape
    kf = w.shape[0] // 2
    tm = _tile(t, 512)

    def body(dx_ref, w_ref, df_ref, dg_ref, dxb_ref):
        dxb = dx_ref[...].astype(BF)
        dxb_ref[...] = dxb
        df_ref[...] = _dot_nt(dxb, w_ref[0:kf, :]).astype(BF)
        dg_ref[...] = _dot_nt(dxb, w_ref[kf:2 * kf, :]).astype(BF)

    row = lambda wd: pl.BlockSpec((tm, wd), lambda i: (i, 0))
    return pl.pallas_call(
        body, name=name, grid=(t // tm,),
        in_specs=[row(d), _full((2 * kf, d))],
        out_specs=[row(kf), row(kf), row(d)],
        out_shape=[jax.ShapeDtypeStruct((t, kf), BF), jax.ShapeDtypeStruct((t, kf), BF),
                   jax.ShapeDtypeStruct((t, d), BF)],
        compiler_params=_params(("parallel",)),
    )(dx, w)


def _loss_grad(y, target, name):
    t, d = y.shape
    tm = _tile(t, 512)

    def body(y_ref, t_ref, l_ref, dy_ref):
        @pl.when(pl.program_id(0) == 0)
        def _():
            l_ref[...] = jnp.zeros_like(l_ref)

        diff = y_ref[...] - t_ref[...]
        dy_ref[...] = diff * (1.0 / d)
        part = jnp.sum(jnp.sum(diff * diff, axis=1, keepdims=True), axis=0, keepdims=True)
        l_ref[...] += part * (0.5 / d)

    row = pl.BlockSpec((tm, d), lambda i: (i, 0))
    return pl.pallas_call(
        body, name=name, grid=(t // tm,),
        in_specs=[row, row], out_specs=[_full((1, 1)), row],
        out_shape=[jax.ShapeDtypeStruct((1, 1), F32), jax.ShapeDtypeStruct((t, d), F32)],
        compiler_params=_params(("arbitrary",)),
    )(y, target)


def _head_sum_matrix(width, head):
    r = lax.broadcasted_iota(jnp.int32, (width, width), 0) // head
    c = lax.broadcasted_iota(jnp.int32, (width, width), 1) // head
    return (r == c).astype(BF)


def _head_mean(x, bd):
    return _dot(x.astype(BF), bd) * (1.0 / FOX_HEAD_DIM)


def _mask_dot(mask01, x):
    mb = mask01.astype(BF)
    hi = x.astype(BF)
    r1 = x - hi.astype(F32)
    mid = r1.astype(BF)
    lo = (r1 - mid.astype(F32)).astype(BF)
    return _dot(mb, hi) + _dot(mb, mid) + _dot(mb, lo)


def _fox_prep(proj, wq_t, wk_t, bias_pad, seq, name):
    t = proj.shape[0]
    ts = _tile(seq, 512)
    tpe = seq // ts
    scale = FOX_HEAD_DIM ** -0.5

    def body(q_ref, k_ref, v_ref, gt_ref, wq_ref, wk_ref, b_ref, qo_ref, ko_ref, vo_ref, cum_ref, carry):
        i = pl.program_id(0)
        bd = _head_sum_matrix(FOX_WIDTH, FOX_HEAD_DIM)

        def norm(xv, wv):
            ms = _head_mean(xv * xv, bd)
            return xv * lax.rsqrt(ms + EPS) * wv

        qo_ref[...] = (norm(q_ref[...], wq_ref[...]) * scale).astype(BF)
        ko_ref[...] = norm(k_ref[...], wk_ref[...]).astype(BF)
        vo_ref[...] = v_ref[...].astype(BF)

        @pl.when(i % tpe == 0)
        def _():
            carry[...] = jnp.zeros_like(carry)

        ls = _log_sigmoid(gt_ref[...] + b_ref[...])
        r = lax.broadcasted_iota(jnp.int32, (ts, ts), 0)
        c = lax.broadcasted_iota(jnp.int32, (ts, ts), 1)
        cum = _mask_dot(r >= c, ls) + carry[...]
        cum_ref[...] = cum
        carry[...] = cum[ts - 1:ts, :]

    blk = lambda j: pl.BlockSpec((ts, FOX_WIDTH), lambda i: (i, j))
    gate = pl.BlockSpec((ts, LANES), lambda i: (i, GATE_COL // LANES))
    out = pl.BlockSpec((ts, FOX_WIDTH), lambda i: (i, 0))
    return pl.pallas_call(
        body, name=name, grid=(t // ts,),
        in_specs=[blk(0), blk(1), blk(2), gate, _full((1, FOX_WIDTH)), _full((1, FOX_WIDTH)), _full((1, LANES))],
        out_specs=[out, out, out, pl.BlockSpec((ts, LANES), lambda i: (i, 0))],
        out_shape=[jax.ShapeDtypeStruct((t, FOX_WIDTH), BF)] * 3 + [jax.ShapeDtypeStruct((t, LANES), F32)],
        scratch_shapes=[pltpu.VMEM((1, LANES), F32)],
        compiler_params=_params(("arbitrary",)),
    )(proj, proj, proj, proj, wq_t, wk_t, bias_pad)


def _pick_lanes(x, lane_in_block, first_out_lane):
    r = lax.broadcasted_iota(jnp.int32, (FOX_WIDTH, LANES), 0)
    c = lax.broadcasted_iota(jnp.int32, (FOX_WIDTH, LANES), 1)
    sel = ((r % LANES == lane_in_block) & (c == first_out_lane + 2 * (r // LANES))).astype(BF)
    hi = x.astype(BF)
    r1 = x - hi.astype(F32)
    mid = r1.astype(BF)
    lo = (r1 - mid.astype(F32)).astype(BF)
    return _dot(hi, sel) + _dot(mid, sel) + _dot(lo, sel)


def _fox_prep_bwd(proj, dqa, dqb, dk, dv, dkx, wq_t, wk_t, bias_pad, seq, name):
    t = proj.shape[0]
    ts = _tile(seq, 512)
    tpe = seq // ts
    nt = t // ts
    scale = FOX_HEAD_DIM ** -0.5

    def body(q_ref, k_ref, gt_ref, dqa_ref, dqb_ref, dk_ref, dv_ref, dc_ref, wq_ref, wk_ref, b_ref,
             dp_ref, dff_ref, dwq_ref, dwk_ref, db_ref, carry):
        i = pl.program_id(0)
        first = (lax.broadcasted_iota(jnp.int32, (ts, FOX_WIDTH), 1) % LANES) < FOX_HEAD_DIM
        dq_all = jnp.where(first, dqa_ref[...], dqb_ref[...])
        ti = nt - 1 - i
        bd = _head_sum_matrix(FOX_WIDTH, FOX_HEAD_DIM)

        @pl.when(i == 0)
        def _():
            dwq_ref[...] = jnp.zeros_like(dwq_ref)
            dwk_ref[...] = jnp.zeros_like(dwk_ref)
            db_ref[...] = jnp.zeros_like(db_ref)

        def norm_bwd(xv, wv, dyv):
            ms = _head_mean(xv * xv, bd)
            rstd = lax.rsqrt(ms + EPS)
            xhat = xv * rstd
            dxhat = dyv * wv
            mean = _head_mean(dxhat * xhat, bd)
            return rstd * (dxhat - xhat * mean), jnp.sum(dyv * xhat, axis=0, keepdims=True)

        dxq, dwq = norm_bwd(q_ref[...], wq_ref[...], dq_all * scale)
        dxk, dwk = norm_bwd(k_ref[...], wk_ref[...], dk_ref[...])
        dp_ref[:, 0:FOX_WIDTH] = dxq.astype(BF)
        dp_ref[:, FOX_WIDTH:2 * FOX_WIDTH] = dxk.astype(BF)
        dp_ref[:, 2 * FOX_WIDTH:3 * FOX_WIDTH] = dv_ref[...].astype(BF)
        dwq_ref[...] += dwq
        dwk_ref[...] += dwk

        @pl.when(ti % tpe == tpe - 1)
        def _():
            carry[...] = jnp.zeros_like(carry)

        r = lax.broadcasted_iota(jnp.int32, (ts, ts), 0)
        c = lax.broadcasted_iota(jnp.int32, (ts, ts), 1)
        dkx = dc_ref[...]
        hd = FOX_HEAD_DIM
        dcum = (_pick_lanes(dqa_ref[...], hd, 0) + _pick_lanes(dqb_ref[...], 0, 1)
                - _pick_lanes(dkx, hd, 0) - _pick_lanes(dkx, 0, 1))
        dls = _mask_dot(c >= r, dcum) + carry[...]
        carry[...] = dls[0:1, :]
        z = gt_ref[...] + b_ref[...]
        lane = lax.broadcasted_iota(jnp.int32, (ts, LANES), 1)
        dff = jnp.where(lane < FOX_HEADS, dls * _sigmoid(-z), 0.0)
        dff_ref[...] = dff
        db_ref[...] += jnp.sum(dff, axis=0, keepdims=True)

        @pl.when(i == nt - 1)
        def _():
            fr = lax.broadcasted_iota(jnp.int32, (FOX_WIDTH, FOX_WIDTH), 0) % FOX_HEAD_DIM
            fc = lax.broadcasted_iota(jnp.int32, (FOX_WIDTH, FOX_WIDTH), 1) % FOX_HEAD_DIM
            fold = (fr == fc).astype(F32)
            dwq_ref[...] = _dot(dwq_ref[...], fold, HI)
            dwk_ref[...] = _dot(dwk_ref[...], fold, HI)

    rev = lambda w, j: pl.BlockSpec((ts, w), lambda i: (nt - 1 - i, j))
    return pl.pallas_call(
        body, name=name, grid=(nt,),
        in_specs=[rev(FOX_WIDTH, 0), rev(FOX_WIDTH, 1), rev(LANES, GATE_COL // LANES),
                  rev(FOX_WIDTH, 0), rev(FOX_WIDTH, 0), rev(FOX_WIDTH, 0), rev(FOX_WIDTH, 0), rev(FOX_WIDTH, 0),
                  _full((1, FOX_WIDTH)), _full((1, FOX_WIDTH)), _full((1, LANES))],
        out_specs=[rev(3 * FOX_WIDTH, 0), rev(LANES, 0), _full((1, FOX_WIDTH)), _full((1, FOX_WIDTH)),
                   _full((1, LANES))],
        out_shape=[jax.ShapeDtypeStruct((t, 3 * FOX_WIDTH), BF), jax.ShapeDtypeStruct((t, LANES), F32),
                   jax.ShapeDtypeStruct((1, FOX_WIDTH), F32), jax.ShapeDtypeStruct((1, FOX_WIDTH), F32),
                   jax.ShapeDtypeStruct((1, LANES), F32)],
        scratch_shapes=[pltpu.VMEM((1, LANES), F32)],
        compiler_params=_params(("arbitrary",)),
    )(proj, proj, proj, dqa, dqb, dk, dv, dkx, wq_t, wk_t, bias_pad)


class _Rider:
    def __init__(self, inputs, out_shapes, sems, phases):
        self.inputs, self.out_shapes, self.sems, self.phases = list(inputs), list(out_shapes), list(sems), phases


def _rider_parts(rider):
    if rider is None:
        return [], [], []
    return rider.inputs, rider.out_shapes, rider.sems


def _ride(rider, which, when, refs):
    if rider is not None:
        @pl.when(when)
        def _():
            rider.phases[which](*refs)


def _fox_attn(q, k, v, ck, nb, seq, name, rider=None):
    t = q.shape[0]
    tq = _tile(seq, 512)
    nq = seq // tq
    npair = FOX_HEADS // 2
    hd = FOX_HEAD_DIM
    r_in, r_out, r_sem = _rider_parts(rider)
    steps = nb * npair * nq

    def body(q_ref, k_ref, v_ref, ck_ref, *rest):
        rin, (o_ref, lse_ref) = rest[:len(r_in)], rest[len(r_in):len(r_in) + 2]
        rout = rest[len(r_in) + 2:len(r_in) + 2 + len(r_out)]
        m_s, acc_s = rest[len(r_in) + 2 + len(r_out):len(r_in) + 4 + len(r_out)]
        riding = (rin, rout, rest[len(r_in) + 4 + len(r_out):])
        step = (pl.program_id(0) * npair + pl.program_id(1)) * nq + pl.program_id(2)
        _ride(rider, 0, step == 0, riding)
        _ride(rider, 1, step == steps // 2, riding)
        qi = pl.program_id(2)
        lane = lax.broadcasted_iota(jnp.int32, (tq, LANES), 1)
        m_s[...] = jnp.full(m_s.shape, NEG, F32)
        acc_s[...] = jnp.zeros_like(acc_s)
        qv = q_ref[...]

        def tile(kj, on_diagonal):
            cols = pl.ds(pl.multiple_of(kj * tq, tq), tq)
            kv = k_ref[cols, :]
            vv = v_ref[cols, :]
            if on_diagonal:
                causal = (lax.broadcasted_iota(jnp.int32, (tq, tq), 0)
                          >= lax.broadcasted_iota(jnp.int32, (tq, tq), 1))
            ck = [ck_ref[hh, :, cols] for hh in range(2)]
            m_old = [m_s[hh] for hh in range(2)]
            acc_old = [acc_s[hh] for hh in range(2)]
            m_out, acc_out = [], []
            for hh in range(2):
                hm = (lane >= hd) if hh else (lane < hd)
                qh = jnp.where(hm, qv, jnp.zeros_like(qv))
                s = _dot_nt(qh, kv) - ck[hh]
                if on_diagonal:
                    s = jnp.where(causal, s, NEG)
                m_new = jnp.maximum(m_old[hh], jnp.max(s, axis=-1, keepdims=True))
                p = jnp.exp(s - m_new)
                alpha = jnp.exp(m_old[hh] - m_new)
                m_out.append(m_new)
                acc_out.append(alpha * acc_old[hh] + _dot(p.astype(BF), jnp.where(hm, vv, jnp.ones_like(vv))))
            for hh in range(2):
                m_s[hh] = m_out[hh]
                acc_s[hh] = acc_out[hh]

        def off_diagonal(kj, carry):
            tile(kj, False)
            return carry

        lax.fori_loop(0, qi, off_diagonal, 0)
        tile(qi, True)
        a0 = acc_s[0]
        a1 = acc_s[1]
        den = jnp.where(lane < hd, pltpu.roll(a0, hd, axis=1), pltpu.roll(a1, hd, axis=1))
        o_ref[...] = (jnp.where(lane < hd, a0, a1) / den).astype(o_ref.dtype)
        l0 = jnp.sum(jnp.where(lane == hd, a0, 0.0), axis=1, keepdims=True)
        l1 = jnp.sum(jnp.where(lane == 0, a1, 0.0), axis=1, keepdims=True)
        lse_ref[0] = m_s[0] + jnp.log(l0)
        lse_ref[1] = m_s[1] + jnp.log(l1)
        _ride(rider, 2, step == steps - 1, riding)

    qspec = pl.BlockSpec((tq, LANES), lambda b, p, i: (b * nq + i, p))
    kspec = pl.BlockSpec((seq, LANES), lambda b, p, i: (b, p))
    colspec = pl.BlockSpec((None, 2, tq, 1), lambda b, p, i: (b * npair + p, 0, i, 0))
    rowspec = pl.BlockSpec((None, 2, 1, seq), lambda b, p, i: (b * npair + p, 0, 0, 0))
    sem = ("arbitrary",) * 3 if rider else ("parallel",) * 3
    return pl.pallas_call(
        body, name=name, grid=(nb, npair, nq),
        in_specs=[qspec, kspec, kspec, rowspec] + [HBM] * len(r_in),
        out_specs=[qspec, colspec] + [HBM] * len(r_out),
        out_shape=[jax.ShapeDtypeStruct((t, FOX_WIDTH), BF), jax.ShapeDtypeStruct((nb * npair, 2, seq, 1), F32)]
        + r_out,
        scratch_shapes=[pltpu.VMEM((2, tq, 1), F32), pltpu.VMEM((2, tq, LANES), F32)] + r_sem,
        compiler_params=_params(sem, has_side_effects=rider is not None),
    )(q, k, v, ck, *r_in)


def _fox_attn_bwd(q, k, v, o, do, lse, ck, nb, seq, name, rider=None):
    t = q.shape[0]
    tq = _tile(seq, 512)
    nq = seq // tq
    npair = FOX_HEADS // 2
    hd = FOX_HEAD_DIM
    r_in, r_out, r_sem = _rider_parts(rider)
    steps = nb * npair * nq

    def body(q_ref, k_ref, v_ref, o_ref, do_ref, lse_ref, ck_ref, *rest):
        rin, (dqa_ref, dqb_ref, dk_ref, dv_ref, dkx_ref) = rest[:len(r_in)], rest[len(r_in):len(r_in) + 5]
        rout = rest[len(r_in) + 5:len(r_in) + 5 + len(r_out)]
        dk_s, dv_s = rest[len(r_in) + 5 + len(r_out):len(r_in) + 7 + len(r_out)]
        riding = (rin, rout, rest[len(r_in) + 7 + len(r_out):])
        step = (pl.program_id(0) * npair + pl.program_id(1)) * nq + pl.program_id(2)
        _ride(rider, 0, step == 0, riding)
        _ride(rider, 1, step == steps // 2, riding)
        kj = pl.program_id(2)
        lane = lax.broadcasted_iota(jnp.int32, (tq, LANES), 1)

        @pl.when(kj == 0)
        def _():
            dqa_ref[...] = jnp.zeros_like(dqa_ref)
            dqb_ref[...] = jnp.zeros_like(dqb_ref)

        dk_s[...] = jnp.zeros_like(dk_s)
        dv_s[...] = jnp.zeros_like(dv_s)
        kv = k_ref[...]
        vv = v_ref[...]

        def tile(qi, on_diagonal):
            rows = pl.ds(pl.multiple_of(qi * tq, tq), tq)
            qv = q_ref[rows, :]
            dov = do_ref[rows, :]
            prod = dov.astype(F32) * o_ref[rows, :].astype(F32)
            if on_diagonal:
                causal = (lax.broadcasted_iota(jnp.int32, (tq, tq), 0)
                          >= lax.broadcasted_iota(jnp.int32, (tq, tq), 1))
            for hh, dq_ref in ((0, dqa_ref), (1, dqb_ref)):
                hm = (lane >= hd) if hh else (lane < hd)
                zero = jnp.zeros_like(qv)
                one = jnp.ones_like(qv)
                doh = jnp.where(hm, dov, zero)
                delta = jnp.sum(jnp.where(hm, prod, 0.0), axis=-1, keepdims=True)
                s = _dot_nt(jnp.where(hm, qv, zero), kv) - ck_ref[hh]
                if on_diagonal:
                    s = jnp.where(causal, s, NEG)
                p = jnp.exp(s - lse_ref[hh, rows, :])
                dp = _dot_nt(doh, vv)
                dsb = (p * (dp - delta)).astype(BF)
                dv_s[...] += _dot_tn(p.astype(BF), doh)
                dk_s[hh] += _dot_tn(dsb, jnp.where(hm, qv, one))
                dq_ref[rows, :] += _dot(dsb, jnp.where(hm, kv, one))

        def off_diagonal(qi, carry):
            tile(qi, False)
            return carry

        tile(kj, True)
        lax.fori_loop(kj + 1, nq, off_diagonal, 0)
        dk_ref[...] = jnp.where(lane < hd, dk_s[0], dk_s[1])
        dkx_ref[...] = jnp.where(lane < hd, dk_s[1], dk_s[0])
        dv_ref[...] = dv_s[...]
        _ride(rider, 2, step == steps - 1, riding)

    kspec = pl.BlockSpec((tq, LANES), lambda b, p, j: (b * nq + j, p))
    full_q = pl.BlockSpec((seq, LANES), lambda b, p, j: (b, p))
    colspec = pl.BlockSpec((None, 2, seq, 1), lambda b, p, j: (b * npair + p, 0, 0, 0))
    rowspec = pl.BlockSpec((None, 2, 1, tq), lambda b, p, j: (b * npair + p, 0, 0, j))
    sem = ("arbitrary",) * 3 if rider else ("parallel", "parallel", "arbitrary")
    return pl.pallas_call(
        body, name=name, grid=(nb, npair, nq),
        in_specs=[full_q, kspec, kspec, full_q, full_q, colspec, rowspec] + [HBM] * len(r_in),
        out_specs=[full_q, full_q, kspec, kspec, kspec] + [HBM] * len(r_out),
        out_shape=[jax.ShapeDtypeStruct((t, FOX_WIDTH), F32)] * 5 + r_out,
        scratch_shapes=[pltpu.VMEM((2, tq, LANES), F32), pltpu.VMEM((tq, LANES), F32)] + r_sem,
        compiler_params=_params(sem, has_side_effects=rider is not None),
    )(q, k, v, o, do, lse, ck, *r_in)


GDN_QKV = 3 * GDN_WIDTH
GDN_COL = 3 * FOX_WIDTH
GG_COL = GDN_COL + GDN_QKV
A_LANE = FOX_HEADS
B_LANE = FOX_HEADS + GDN_HEADS
HALO = 8


def _gate_lanes(ts):
    lane = lax.broadcasted_iota(jnp.int32, (ts, LANES), 1)
    return (lane >= A_LANE) & (lane < B_LANE), (lane >= B_LANE) & (lane < B_LANE + GDN_HEADS)


def _chunk_tri(ts, upper):
    r = lax.broadcasted_iota(jnp.int32, (ts, ts), 0)
    c = lax.broadcasted_iota(jnp.int32, (ts, ts), 1)
    same = (r // CHUNK) == (c // CHUNK)
    return (same & ((c >= r) if upper else (r >= c))).astype(F32)


def _conv_silu_l2(xp_ref, w, ts):
    c = w[0:1, :] * xp_ref[pl.ds(HALO - 3, ts), :]
    for kk in range(1, CONV_WIDTH):
        c = c + w[kk:kk + 1, :] * xp_ref[pl.ds(HALO - 3 + kk, ts), :]
    return c, c * _sigmoid(c)


def _gdn_prep(proj, conv_w, a_pad, dt_pad, seq, name):
    t = proj.shape[0]
    ts = _tile(seq, 256)
    tpe = seq // ts
    qscale = GDN_HEAD_DIM ** -0.5

    def body(x_ref, gt_ref, w_ref, a_ref, dt_ref, qo_ref, ko_ref, vo_ref, go_ref, xp):
        i = pl.program_id(0)
        tail = xp[pl.ds(ts, HALO), :]
        xp[pl.ds(0, HALO), :] = jnp.where(i % tpe == 0, jnp.zeros_like(tail), tail)
        xp[pl.ds(HALO, ts), :] = x_ref[...]
        _, s = _conv_silu_l2(xp, w_ref[...], ts)
        for h in range(GDN_HEADS):
            for base, ref, sc in ((0, qo_ref, qscale), (GDN_WIDTH, ko_ref, 1.0)):
                xh = s[:, base + h * LANES: base + (h + 1) * LANES]
                r = lax.rsqrt(jnp.sum(xh * xh, axis=-1, keepdims=True) + EPS)
                ref[:, h * LANES:(h + 1) * LANES] = (xh * (r * sc)).astype(BF)
        vo_ref[...] = s[:, 2 * GDN_WIDTH:].astype(BF)
        gate = gt_ref[...]
        g_raw = -jnp.exp(a_ref[...]) * _softplus(gate + dt_ref[...])
        gc = _mask_dot(_chunk_tri(ts, False), g_raw)
        is_a, is_b = _gate_lanes(ts)
        go_ref[...] = jnp.where(is_a, gc, jnp.where(is_b, _sigmoid(gate), 0.0))

    out = pl.BlockSpec((ts, GDN_WIDTH), lambda i: (i, 0))
    lanes = pl.BlockSpec((ts, LANES), lambda i: (i, 0))
    return pl.pallas_call(
        body, name=name, grid=(t // ts,),
        in_specs=[pl.BlockSpec((ts, GDN_QKV), lambda i: (i, GDN_COL // GDN_QKV)),
                  pl.BlockSpec((ts, LANES), lambda i: (i, GATE_COL // LANES)),
                  _full((CONV_WIDTH, GDN_QKV)), _full((1, LANES)), _full((1, LANES))],
        out_specs=[out, out, out, lanes],
        out_shape=[jax.ShapeDtypeStruct((t, GDN_WIDTH), BF)] * 3 + [jax.ShapeDtypeStruct((t, LANES), F32)],
        scratch_shapes=[pltpu.VMEM((ts + HALO, GDN_QKV), F32)],
        compiler_params=_params(("arbitrary",)),
    )(proj, proj, conv_w, a_pad, dt_pad)


def _gdn_prep_bwd(proj, dq, dk, dv, dgates, dff, conv_w, a_pad, dt_pad, seq, name):
    t = proj.shape[0]
    ts = _tile(seq, 256)
    tpe = seq // ts
    nt = t // ts
    qscale = GDN_HEAD_DIM ** -0.5
    hb = ts // HALO

    def body(x_ref, halo_ref, gt_ref, dq_ref, dk_ref, dv_ref, dgt_ref, dff_ref, w_ref, a_ref, dt_ref,
             dx_ref, dgo_ref, dw_ref, da_ref, ddt_ref, xp, dcp, carry):
        i = pl.program_id(0)
        ti = nt - 1 - i

        @pl.when(i == 0)
        def _():
            dw_ref[...] = jnp.zeros_like(dw_ref)
            da_ref[...] = jnp.zeros_like(da_ref)
            ddt_ref[...] = jnp.zeros_like(ddt_ref)

        halo = halo_ref[...]
        xp[pl.ds(0, HALO), :] = jnp.where(ti % tpe == 0, jnp.zeros_like(halo), halo)
        xp[pl.ds(HALO, ts), :] = x_ref[...]
        w = w_ref[...]
        c, s = _conv_silu_l2(xp, w, ts)
        for h in range(GDN_HEADS):
            for base, ref, sc in ((0, dq_ref, qscale), (GDN_WIDTH, dk_ref, 1.0)):
                lo = base + h * LANES
                xh = s[:, lo:lo + LANES]
                r = lax.rsqrt(jnp.sum(xh * xh, axis=-1, keepdims=True) + EPS)
                y = xh * r
                dy = ref[:, h * LANES:(h + 1) * LANES] * sc
                dcp[pl.ds(0, ts), lo:lo + LANES] = r * (dy - y * jnp.sum(dy * y, axis=-1, keepdims=True))
        dcp[pl.ds(0, ts), 2 * GDN_WIDTH:] = dv_ref[...]
        sg = _sigmoid(c)
        dc = dcp[pl.ds(0, ts), :] * (sg * (1.0 + c * (1.0 - sg)))
        dcp[pl.ds(0, ts), :] = dc
        nxt = carry[...]
        dcp[pl.ds(ts, HALO), :] = jnp.where(ti % tpe == tpe - 1, jnp.zeros_like(nxt), nxt)
        carry[...] = dc[0:HALO, :]
        dx = w[CONV_WIDTH - 1:CONV_WIDTH, :] * dc
        for kk in range(CONV_WIDTH - 1):
            dx = dx + w[kk:kk + 1, :] * dcp[pl.ds(CONV_WIDTH - 1 - kk, ts), :]
        dx_ref[...] = dx.astype(BF)
        for kk in range(CONV_WIDTH):
            dw_ref[kk:kk + 1, :] += jnp.sum(dc * xp[pl.ds(HALO - 3 + kk, ts), :], axis=0, keepdims=True)
        gate = gt_ref[...]
        dgt = dgt_ref[...]
        is_a, is_b = _gate_lanes(ts)
        dg_raw = _mask_dot(_chunk_tri(ts, True), jnp.where(is_a, dgt, 0.0))
        z = gate + dt_ref[...]
        na = -jnp.exp(a_ref[...])
        dga = dg_raw * na * _sigmoid(z)
        beta = _sigmoid(gate)
        dgb = jnp.where(is_b, dgt * beta * (1.0 - beta), 0.0)
        dgo_ref[...] = (dff_ref[...] + dga + dgb).astype(BF)
        ddt_ref[...] += jnp.sum(dga, axis=0, keepdims=True)
        da_ref[...] += jnp.sum(dg_raw * na * _softplus(z), axis=0, keepdims=True)

    rev = lambda wd, j: pl.BlockSpec((ts, wd), lambda i: (nt - 1 - i, j))
    halo_spec = pl.BlockSpec((HALO, GDN_QKV), lambda i: (jnp.maximum((nt - 1 - i) * hb - 1, 0), GDN_COL // GDN_QKV))
    return pl.pallas_call(
        body, name=name, grid=(nt,),
        in_specs=[rev(GDN_QKV, GDN_COL // GDN_QKV), halo_spec, rev(LANES, GATE_COL // LANES),
                  rev(GDN_WIDTH, 0), rev(GDN_WIDTH, 0), rev(GDN_WIDTH, 0), rev(LANES, 0), rev(LANES, 0),
                  _full((CONV_WIDTH, GDN_QKV)), _full((1, LANES)), _full((1, LANES))],
        out_specs=[rev(GDN_QKV, 0), rev(LANES, 0), _full((CONV_WIDTH, GDN_QKV)), _full((1, LANES)),
                   _full((1, LANES))],
        out_shape=[jax.ShapeDtypeStruct((t, GDN_QKV), BF), jax.ShapeDtypeStruct((t, LANES), BF),
                   jax.ShapeDtypeStruct((CONV_WIDTH, GDN_QKV), F32), jax.ShapeDtypeStruct((1, LANES), F32),
                   jax.ShapeDtypeStruct((1, LANES), F32)],
        scratch_shapes=[pltpu.VMEM((ts + HALO, GDN_QKV), F32), pltpu.VMEM((ts + HALO, GDN_QKV), F32),
                        pltpu.VMEM((HALO, GDN_QKV), F32)],
        compiler_params=_params(("arbitrary",)),
    )(proj, proj, proj, dq, dk, dv, dgates, dff, conv_w, a_pad, dt_pad)


PAIR = 2 * CHUNK


def _split_bf16(a):
    hi = a.astype(BF)
    return hi, (a - hi.astype(F32)).astype(BF)


def _dot3(a, b, dims=(((1,), (0,)), ((), ()))):
    ah, al = _split_bf16(a)
    bh, bl = _split_bf16(b)
    (ca,), (cb,) = dims[0]
    return lax.dot_general(jnp.concatenate([ah, al, ah], axis=ca), jnp.concatenate([bh, bh, bl], axis=cb), dims,
                           preferred_element_type=F32)


def _inv_unit_lower(a):
    r = lax.broadcasted_iota(jnp.int32, (PAIR, PAIR), 0)
    c = lax.broadcasted_iota(jnp.int32, (PAIR, PAIR), 1)
    tm = (r == c).astype(F32) - a
    pw = _dot3(a, a)
    for _ in range(4):
        x = _dot3(jnp.concatenate([tm, pw], axis=0), pw)
        tm = tm + x[:PAIR]
        pw = x[PAIR:]
    return tm + _dot3(tm, pw)


def _gdn_pair_local(q, k, v, gc, gr, b):
    r = lax.broadcasted_iota(jnp.int32, (PAIR, PAIR), 0)
    c = lax.broadcasted_iota(jnp.int32, (PAIR, PAIR), 1)
    same = (r // CHUNK) == (c // CHUNK)
    incl = same & (r >= c)
    strict = same & (r > c)
    dm = jnp.exp(jnp.where(incl, gc - gr, NEG))
    e = jnp.exp(gc)
    kb = k * b
    vb = v * b
    kbe = kb * e
    kq = _dot_nt(jnp.concatenate([kb, q], axis=0).astype(BF), k.astype(BF))
    amat = jnp.where(strict, kq[:PAIR] * dm, 0.0)
    pmat = jnp.where(incl, kq[PAIR:] * dm, 0.0)
    lane = lax.broadcasted_iota(jnp.int32, (1, PAIR), 1)
    gl_a = jnp.sum(jnp.where(lane == CHUNK - 1, gr, 0.0), axis=1, keepdims=True)
    gl_b = jnp.sum(jnp.where(lane == PAIR - 1, gr, 0.0), axis=1, keepdims=True)
    ridx = lax.broadcasted_iota(jnp.int32, (PAIR, 1), 0)
    edec = jnp.exp(jnp.where(ridx < CHUNK, gl_a, gl_b) - gc)
    return dict(dm=dm, e=e, kb=kb, vb=vb, kbe=kbe, amat=amat, pmat=pmat, gl_a=gl_a, gl_b=gl_b, edec=edec,
                kd=k * edec, qd=q * e, incl=incl, strict=strict, ridx=ridx)


def _gdn_pair_states(loc, tb, s_a):
    uw = _dot(tb, jnp.concatenate([loc["vb"], loc["kbe"]], axis=1).astype(BF))
    u, w = uw[:, :LANES], uw[:, LANES:]
    qd, kd, c = loc["qd"], loc["kd"], CHUNK
    xa = _dot(jnp.concatenate([qd[:c], w[:c]], axis=0).astype(BF), s_a.astype(BF))
    vn_a = u[:c] - xa[c:]
    s_b = s_a * jnp.exp(loc["gl_a"]) + _dot_tn(kd[:c].astype(BF), vn_a.astype(BF))
    xb = _dot(jnp.concatenate([qd[c:], w[c:]], axis=0).astype(BF), s_b.astype(BF))
    vn_b = u[c:] - xb[c:]
    s_c = s_b * jnp.exp(loc["gl_b"]) + _dot_tn(kd[c:].astype(BF), vn_b.astype(BF))
    vn = jnp.concatenate([vn_a, vn_b], axis=0)
    o = jnp.concatenate([xa[:c], xb[:c]], axis=0) + _dot(loc["pmat"].astype(BF), vn.astype(BF))
    return w, vn, o, s_b, s_c


GDN_SEG = 1024
HEADS_PER_STEP = 4
HEAD_GROUPS = GDN_HEADS // HEADS_PER_STEP


def _gdn_specs(nb, seq, reverse):
    n = seq // CHUNK
    seg = _tile(seq, GDN_SEG)
    nseg = seq // seg
    sp = seg // PAIR
    w2 = HEADS_PER_STEP * LANES
    at = (lambda s: nseg - 1 - s) if reverse else (lambda s: s)
    blk = pl.BlockSpec((seg, w2), lambda b, hp, s: (b * nseg + at(s), hp))
    gg = pl.BlockSpec((seg, w2), lambda b, hp, s: (b * nseg + at(s), GG_COL // w2 + hp))
    gates = pl.BlockSpec((seg, LANES), lambda b, hp, s: (b * nseg + at(s), 0))
    grp = lambda b, hp: b * HEAD_GROUPS + hp
    rowb = pl.BlockSpec((None, HEADS_PER_STEP, sp, HALO, PAIR), lambda b, hp, s: (grp(b, hp), 0, at(s), 0, 0))
    per_pair = pl.BlockSpec((None, HEADS_PER_STEP, sp, PAIR, PAIR), lambda b, hp, s: (grp(b, hp), 0, at(s), 0, 0))
    dgates = pl.BlockSpec((None, seg, LANES), lambda b, hp, s: (grp(b, hp), at(s), 0))
    return n, seg, nseg, sp, blk, gg, gates, rowb, per_pair, dgates


def _head_column(gt, lane, index):
    return jnp.sum(jnp.where(lane == index, gt, 0.0), axis=1, keepdims=True)


def _gdn_head_inputs(qkv_refs, gt, gr_ref, rows, pi, hp, lane):
    per_head = []
    for hh in range(HEADS_PER_STEP):
        head = HEADS_PER_STEP * hp + hh
        cols = slice(hh * LANES, (hh + 1) * LANES)
        per_head.append([r[rows, cols].astype(F32) for r in qkv_refs]
                        + [_head_column(gt, lane, A_LANE + head), gr_ref[hh, pi][0:1, :],
                           _head_column(gt, lane, B_LANE + head)])
    return [jnp.stack(xs) for xs in zip(*per_head)]


def _gdn_pair_fwd(qv, kv, vv, gcv, gr, bv, s_a):
    loc = _gdn_pair_local(qv, kv, vv, gcv, gr, bv)
    tf = _inv_unit_lower(loc["amat"])
    _, _, o, _, s_c = _gdn_pair_states(loc, tf.astype(BF), s_a)
    return tf, o, s_c


def _gdn_fwd(q, k, v, proj, gates, grow, wn, nb, seq, name):
    t = q.shape[0]
    n, seg, nseg, sp, blk, gg, gates_spec, rowb, per_pair, _ = _gdn_specs(nb, seq, False)

    def body(q_ref, k_ref, v_ref, gg_ref, gt_ref, gr_ref, wn_ref, y_ref, tn_ref, sn_ref, s_ref):
        hp = pl.program_id(1)

        @pl.when(pl.program_id(2) == 0)
        def _():
            s_ref[...] = jnp.zeros_like(s_ref)

        wnv = wn_ref[...]
        lane = lax.broadcasted_iota(jnp.int32, (PAIR, LANES), 1)

        def step(pi, carry):
            rows = pl.ds(pl.multiple_of(pi * PAIR, PAIR), PAIR)
            gt = gt_ref[rows, :]
            ins = _gdn_head_inputs((q_ref, k_ref, v_ref), gt, gr_ref, rows, pi, hp, lane)
            s_a = s_ref[...]
            tf, o, s_c = jax.vmap(_gdn_pair_fwd)(*ins, s_a)
            s_ref[...] = s_c
            for hh in range(HEADS_PER_STEP):
                cols = slice(hh * LANES, (hh + 1) * LANES)
                tn_ref[hh, pi] = tf[hh]
                sn_ref[hh, pi] = s_a[hh]
                g = gg_ref[rows, cols]
                oh = o[hh]
                rstd = lax.rsqrt(jnp.mean(oh * oh, axis=-1, keepdims=True) + EPS)
                y_ref[rows, cols] = (oh * rstd * wnv * (g * _sigmoid(g))).astype(BF)
            return carry

        lax.fori_loop(0, sp, step, 0)

    saved = jax.ShapeDtypeStruct((nb * HEAD_GROUPS, HEADS_PER_STEP, n // 2, PAIR, PAIR), F32)
    return pl.pallas_call(
        body, name=name, grid=(nb, GDN_HEADS // HEADS_PER_STEP, nseg),
        in_specs=[blk, blk, blk, gg, gates_spec, rowb, _full((1, LANES))],
        out_specs=[blk, per_pair, per_pair],
        out_shape=[jax.ShapeDtypeStruct((t, GDN_WIDTH), BF), saved, saved],
        scratch_shapes=[pltpu.VMEM((HEADS_PER_STEP, GDN_HEAD_DIM, GDN_HEAD_DIM), F32)],
        compiler_params=_params(("parallel", "parallel", "arbitrary")),
    )(q, k, v, proj, gates, grow, wn)


def _gdn_pair_bwd(qv, kv, vv, gcv, gr, bv, tf, s_a, dsp, g, dyv, wnv):
    c = CHUNK
    loc = _gdn_pair_local(qv, kv, vv, gcv, gr, bv)
    tm = tf.astype(BF)
    kb, vb, kbe, e, dm = loc["kb"], loc["vb"], loc["kbe"], loc["e"], loc["dm"]
    kd, qd, pmat, amat = loc["kd"], loc["qd"], loc["pmat"], loc["amat"]
    w, vn, o, s_b, _ = _gdn_pair_states(loc, tm, s_a)
    sg = _sigmoid(g)
    silu = g * sg
    rstd = lax.rsqrt(jnp.mean(o * o, axis=-1, keepdims=True) + EPS)
    xhat = o * rstd
    dwn = jnp.sum(dyv * xhat * silu, axis=0, keepdims=True)
    dgg = dyv * xhat * wnv * (sg * (1.0 + g * (1.0 - sg)))
    dxhat = dyv * wnv * silu
    do = rstd * (dxhat - xhat * jnp.mean(dxhat * xhat, axis=-1, keepdims=True))
    dob = do.astype(BF)
    tot = lambda x: jnp.sum(jnp.sum(x, axis=1, keepdims=True), axis=0, keepdims=True)
    rsum = lambda x: jnp.sum(x, axis=1, keepdims=True)
    cat = lambda xs, ax=0: jnp.concatenate(xs, axis=ax)
    wb = w.astype(BF)
    qdb = qd.astype(BF)
    kdb = kd.astype(BF)
    vnb = vn.astype(BF)
    egl_a = jnp.exp(loc["gl_a"])
    egl_b = jnp.exp(loc["gl_b"])
    ptdo = _dot_tn(pmat.astype(BF), dob)
    dspb = dsp.astype(BF)
    dvn_b = ptdo[c:] + _dot(kdb[c:], dspb)
    dkd_b = _dot_nt(vnb[c:], dspb)
    dgl_b = egl_b * tot(s_b * dsp) + tot(dkd_b * kd[c:])
    dsm = egl_b * dsp + _dot_tn(cat([qdb[c:], -wb[c:]]), cat([dob[c:], dvn_b.astype(BF)]))
    dsmb = dsm.astype(BF)
    dvn_a = ptdo[:c] + _dot(kdb[:c], dsmb)
    dkd_a = _dot_nt(vnb[:c], dsmb)
    dgl_a = egl_a * tot(s_a * dsm) + tot(dkd_a * kd[:c])
    ds_new = egl_a * dsm + _dot_tn(cat([qdb[:c], -wb[:c]]), cat([dob[:c], dvn_a.astype(BF)]))
    ya = _dot_nt(cat([dob[:c], dvn_a.astype(BF)]), s_a.astype(BF))
    yb = _dot_nt(cat([dob[c:], dvn_b.astype(BF)]), s_b.astype(BF))
    dqd = cat([ya[:c], yb[:c]])
    dw = -cat([ya[c:], yb[c:]])
    dvn = cat([dvn_a, dvn_b])
    dkd = cat([dkd_a, dkd_b])
    dq = dqd * e
    dgc = rsum(dqd * qd) - rsum(dkd * kd)
    dk = dkd * loc["edec"]
    dpm = jnp.where(loc["incl"], _dot_nt(dob, vnb), 0.0)
    duw = cat([dvn, dw], 1).astype(BF)
    dt = _dot_nt(duw, cat([vb, kbe], 1).astype(BF))
    tt = _dot_tn(tm, duw)
    dvb, dkbe = tt[:, :LANES], tt[:, LANES:]
    tn_dims = (((0,), (0,)), ((), ()))
    nt_dims = (((1,), (1,)), ((), ()))
    da = jnp.where(loc["strict"], -_dot3(_dot3(tf, dt, tn_dims), tf, nt_dims), 0.0)
    st = cat([da * dm, dpm * dm]).astype(BF)
    z = _dot(st, kv.astype(BF))
    dkb = z[:PAIR] + dkbe * e
    dq = dq + z[PAIR:]
    dk = dk + _dot_tn(st, cat([kb, qv]).astype(BF))
    gmat = dpm * pmat + da * amat
    dgc = dgc + rsum(dkbe * kbe) + rsum(gmat)
    ridx = loc["ridx"]
    dgc = dgc + jnp.where(ridx == c - 1, dgl_a, 0.0) + jnp.where(ridx == PAIR - 1, dgl_b, 0.0)
    dgc_row = jnp.sum(gmat, axis=0, keepdims=True)
    db = rsum(dvb * vv) + rsum(dkb * kv)
    return dq, dk + dkb * bv, dvb * bv, dgg, dgc, dgc_row, db, dwn, ds_new


def _gdn_bwd(q, k, v, proj, gates, grow, wn, tinv_all, states_all, dy, nb, seq, name):
    t = q.shape[0]
    n, seg, nseg, sp, blk, gg, gates_spec, rowb, per_pair, dgates = _gdn_specs(nb, seq, True)
    dh = GDN_HEAD_DIM

    def body(q_ref, k_ref, v_ref, gg_ref, gt_ref, gr_ref, wn_ref, tn_ref, sn_ref, dy_ref,
             dq_ref, dk_ref, dv_ref, dgg_ref, dgt_ref, dwn_ref, ds_ref):
        hp = pl.program_id(1)

        @pl.when((pl.program_id(0) == 0) & (hp == 0) & (pl.program_id(2) == 0))
        def _():
            dwn_ref[...] = jnp.zeros_like(dwn_ref)

        @pl.when(pl.program_id(2) == 0)
        def _():
            ds_ref[...] = jnp.zeros_like(ds_ref)

        wnv = wn_ref[...]
        lane = lax.broadcasted_iota(jnp.int32, (PAIR, LANES), 1)

        def bwd_step(j, carry):
            pi = sp - 1 - j
            rows = pl.ds(pl.multiple_of(pi * PAIR, PAIR), PAIR)
            gt = gt_ref[rows, :]
            ins = _gdn_head_inputs((q_ref, k_ref, v_ref), gt, gr_ref, rows, pi, hp, lane)
            halves = [slice(hh * LANES, (hh + 1) * LANES) for hh in range(HEADS_PER_STEP)]
            saved = [jnp.stack([r[hh, pi] for hh in range(HEADS_PER_STEP)]) for r in (tn_ref, sn_ref)]
            g2 = jnp.stack([gg_ref[rows, cols] for cols in halves])
            dy2 = jnp.stack([dy_ref[rows, cols].astype(F32) for cols in halves])
            dq, dk, dv, dgg, dgc, dgc_row, db, dwn, ds_new = jax.vmap(
                _gdn_pair_bwd, in_axes=(0,) * 11 + (None,))(*ins, *saved, ds_ref[...], g2, dy2, wnv)
            ds_ref[...] = ds_new
            dgt = jnp.zeros((PAIR, LANES), F32)
            for hh, cols in enumerate(halves):
                head = HEADS_PER_STEP * hp + hh
                dq_ref[rows, cols] = dq[hh]
                dk_ref[rows, cols] = dk[hh]
                dv_ref[rows, cols] = dv[hh]
                dgg_ref[rows, cols] = dgg[hh].astype(BF)
                dwn_ref[...] += dwn[hh]
                row_as_col = jnp.transpose(jnp.broadcast_to(dgc_row[hh], (PAIR, LANES)))
                dgt = (dgt + jnp.where(lane == A_LANE + head, dgc[hh] - row_as_col, 0.0)
                       + jnp.where(lane == B_LANE + head, db[hh], 0.0))
            dgt_ref[rows, :] = dgt
            return carry

        lax.fori_loop(0, sp, bwd_step, 0)

    return pl.pallas_call(
        body, name=name, grid=(nb, GDN_HEADS // HEADS_PER_STEP, nseg),
        in_specs=[blk, blk, blk, gg, gates_spec, rowb, _full((1, LANES)), per_pair, per_pair, blk],
        out_specs=[blk, blk, blk, blk, dgates, _full((1, LANES))],
        out_shape=[jax.ShapeDtypeStruct((t, GDN_WIDTH), F32)] * 3 + [
            jax.ShapeDtypeStruct((t, GDN_WIDTH), BF),
            jax.ShapeDtypeStruct((nb * HEAD_GROUPS, seq, LANES), F32),
            jax.ShapeDtypeStruct((1, LANES), F32)],
        scratch_shapes=[pltpu.VMEM((HEADS_PER_STEP, dh, dh), F32)],
        compiler_params=_params(("arbitrary", "arbitrary", "arbitrary")),
    )(q, k, v, proj, gates, grow, wn, tinv_all, states_all, dy)


def _mix_to_padded(w):
    pad = jnp.zeros(w.shape[:-1] + (N_PAD - N_IN,), w.dtype)
    return jnp.concatenate([w[..., 0:1536], w[..., 1544:3080], w[..., 3088:3600], w[..., 1536:1544],
                            w[..., 3080:3088], pad], axis=-1)


def _pad_lanes(vec, start):
    return jnp.pad(vec[None, :], ((0, 0), (start, LANES - start - vec.shape[0])))


def _heads_to_rows(block, lane0, nheads, nb, seq):
    return block[:, lane0:lane0 + nheads].reshape(nb, seq, nheads).transpose(0, 2, 1).reshape(nb * nheads, seq)


def _mixer_small(p, l):
    wq_t = jnp.tile(p["fox_q_norm"][l], FOX_HEADS)[None, :]
    wk_t = jnp.tile(p["fox_k_norm"][l], FOX_HEADS)[None, :]
    bias = _pad_lanes(p["fox_f_bias"][l], 0)
    a_pad = _pad_lanes(p["gdn_a_log"][l], A_LANE)
    dt_pad = _pad_lanes(p["gdn_dt_bias"][l], A_LANE)
    wn = p["gdn_out_norm"][l][None, :]
    return wq_t, wk_t, bias, a_pad, dt_pad, wn


def _layer_fwd(x, p, l, nb, seq, rider=None, ffn1_rider=None, after_ffn1=None):
    npair = FOX_HEADS // 2
    n = seq // CHUNK
    x1, h1, *rode1 = _ffn_fwd(x, p["ffn1_norm"][l][None, :], p["ffn1_w_in"][l], p["ffn1_w_out"][l],
                              f"ffn1_fwd_{l}", ffn1_rider)
    if after_ffn1 is not None:
        after_ffn1(rode1)
    wq_t, wk_t, bias, a_pad, dt_pad, wn = _mixer_small(p, l)
    proj = _norm_matmul(x1, p["mix_norm"][l][None, :], p["w_mix"][l], f"mix_in_{l}")
    fq, fk, fv, cum = _fox_prep(proj, wq_t, wk_t, bias, seq, f"fox_prep_{l}")
    c8 = _heads_to_rows(cum, 0, FOX_HEADS, nb, seq)
    ck = c8.reshape(nb * npair, 2, 1, seq)
    o, lse, *rode = _fox_attn(fq, fk, fv, ck, nb, seq, f"fox_attn_{l}", rider)
    gq, gk, gv, gates = _gdn_prep(proj, p["gdn_conv"][l], a_pad, dt_pad, seq, f"gdn_prep_{l}")
    gc4 = _heads_to_rows(gates, A_LANE, GDN_HEADS, nb, seq)
    grow = jnp.broadcast_to(gc4.reshape(nb * HEAD_GROUPS, HEADS_PER_STEP, n // 2, 1, PAIR),
                            (nb * HEAD_GROUPS, HEADS_PER_STEP, n // 2, HALO, PAIR))
    y, tinv, states = _gdn_fwd(gq, gk, gv, proj, gates, grow, wn, nb, seq, f"gdn_fwd_{l}")
    x2 = _mix_out(x1, o, y, p["w_out"][l], f"mix_out_{l}")
    x3, h2 = _ffn_fwd(x2, p["ffn2_norm"][l][None, :], p["ffn2_w_in"][l], p["ffn2_w_out"][l], f"ffn2_fwd_{l}")
    saved = dict(x=x, h1=h1, x1=x1, proj=proj, fq=fq, fk=fk, fv=fv, ck=ck, o=o, lse=lse,
                 gq=gq, gk=gk, gv=gv, gates=gates, grow=grow, tinv=tinv, states=states, y=y, x2=x2, h2=h2)
    return x3, saved, rode


def _ffn_grads(dy, x, h, gain, win, wout, l, tag, rider=None):
    t, d = x.shape
    fs = win.shape[2]
    dx, dh, a, hn, dyh, dgain, *rode = _ffn_bwd(dy, x, h, gain, win, wout, f"{tag}_bwd_{l}", rider)
    g_in = _wgrad(hn, dh, jax.ShapeDtypeStruct((4, d, fs), BF),
                  pl.BlockSpec((None, d, fs), lambda i, j, k: (j, i, 0)), d, fs, f"{tag}_gw_in_{l}")
    g_out = _wgrad(a, dyh, jax.ShapeDtypeStruct((2 * fs, d), BF),
                   pl.BlockSpec((fs, d), lambda i, j, k: (i, j)), fs, d, f"{tag}_gw_out_{l}")
    return dx, dgain[0], g_in, g_out.reshape(4, fs // 2, d), rode


def _layer_bwd(dx3, p, l, sv, nb, seq, rider=None, before_ffn1=None):
    npair = FOX_HEADS // 2
    d = dx3.shape[1]
    wq_t, wk_t, bias, a_pad, dt_pad, wn = _mixer_small(p, l)
    g = {}
    dx2, g["ffn2_norm"], g["ffn2_w_in"], g["ffn2_w_out"], _ = _ffn_grads(
        dx3, sv["x2"], sv["h2"], p["ffn2_norm"][l][None, :], p["ffn2_w_in"][l], p["ffn2_w_out"][l], l, "ffn2")
    dyf, dyg, dxb = _mix_out_bwd(dx2, p["w_out"][l], f"mix_out_bwd_{l}")
    half = lambda a, nm: _wgrad(a, dxb, jax.ShapeDtypeStruct((FOX_WIDTH, d), BF),
                                pl.BlockSpec((FOX_WIDTH, d), lambda i, j, k: (i, j)), FOX_WIDTH, d, nm)
    g["w_out"] = jnp.concatenate([half(sv["o"], f"gw_out_fox_{l}"), half(sv["y"], f"gw_out_gdn_{l}")], axis=0)
    dqa, dqb, dk, dv, dkx, *rode = _fox_attn_bwd(sv["fq"], sv["fk"], sv["fv"], sv["o"], dyf, sv["lse"], sv["ck"],
                                                 nb, seq, f"fox_attn_bwd_{l}", rider)

    dpf, dff, dwq, dwk, dbias = _fox_prep_bwd(sv["proj"], dqa, dqb, dk, dv, dkx, wq_t, wk_t, bias, seq,
                                              f"fox_prep_bwd_{l}")
    g["fox_q_norm"] = dwq[0, :FOX_HEAD_DIM]
    g["fox_k_norm"] = dwk[0, :FOX_HEAD_DIM]
    g["fox_f_bias"] = dbias[0, :FOX_HEADS]
    dgq, dgk, dgv, dgg, dgt, dwn = _gdn_bwd(
        sv["gq"], sv["gk"], sv["gv"], sv["proj"], sv["gates"], sv["grow"], wn, sv["tinv"], sv["states"],
        dyg, nb, seq, f"gdn_bwd_{l}")
    dgates = jnp.sum(dgt.reshape(nb, HEAD_GROUPS, seq, LANES), axis=1).reshape(nb * seq, LANES)
    dpg, dgate_blk, dconv, da, ddt = _gdn_prep_bwd(sv["proj"], dgq, dgk, dgv, dgates, dff, p["gdn_conv"][l],
                                                   a_pad, dt_pad, seq, f"gdn_prep_bwd_{l}")
    g["gdn_conv"] = dconv
    g["gdn_a_log"] = da[0, A_LANE:B_LANE]
    g["gdn_dt_bias"] = ddt[0, A_LANE:B_LANE]
    g["gdn_out_norm"] = dwn[0]
    dparts = [dpf, dpg, dgg, dgate_blk]
    dx1, hnm, dgm = _norm_matmul_bwd(dx2, dparts, sv["x1"], p["mix_norm"][l][None, :], p["w_mix"][l],
                                     f"mix_in_bwd_{l}")
    g["mix_norm"] = dgm[0]
    gf, gg_, go, gt = [_wgrad(hnm, a, jax.ShapeDtypeStruct((d, a.shape[1]), F32),
                              pl.BlockSpec((d // 2, a.shape[1]), lambda i, j, k: (i, j)), d // 2, a.shape[1],
                              f"gw_mix_{l}_{i}") for i, a in enumerate(dparts)]
    g["w_in"] = jnp.concatenate([gf, gt[:, 0:FOX_HEADS], gg_, gt[:, A_LANE:B_LANE + GDN_HEADS], go], axis=1)
    ffn1_rider = before_ffn1(g) if before_ffn1 is not None else None
    dx0, g["ffn1_norm"], g["ffn1_w_in"], g["ffn1_w_out"], rode1 = _ffn_grads(
        dx1, sv["x"], sv["h1"], p["ffn1_norm"][l][None, :], p["ffn1_w_in"][l], p["ffn1_w_out"][l], l, "ffn1",
        ffn1_rider)
    return dx0, g, rode, rode1


def _local_step(x, target, p):
    nb, seq, d = x.shape
    xt = x.reshape(nb * seq, d)
    saved = []
    for l in range(DEPTH):
        xt, sv, _ = _layer_fwd(xt, p, l, nb, seq)
        saved.append(sv)
    loss, dx = _loss_grad(xt, target.reshape(nb * seq, d), "loss")
    grads = [None] * DEPTH
    for l in reversed(range(DEPTH)):
        dx, grads[l], _, _ = _layer_bwd(dx, p, l, saved[l], nb, seq)
    return loss, dx.reshape(nb, seq, d), grads


N_CHIPS = 4


def _mesh_pos():
    return lax.axis_index("x"), lax.axis_index("y"), lax.axis_index("c")


def _other_chips(x, y):
    return [(1 - x, y), (x, 1 - y), (1 - x, 1 - y)]


def _remote(src, dst, send_sem, recv_sem, to):
    return pltpu.make_async_remote_copy(src_ref=src, dst_ref=dst, send_sem=send_sem, recv_sem=recv_sem,
                                        device_id=to, device_id_type=MESH)


def _hbm_call(body, name, ins, out_shape, scratch):
    return pl.pallas_call(
        body, name=name, out_shape=out_shape, in_specs=[HBM] * len(ins),
        out_specs=jax.tree.map(lambda _: HBM, out_shape), scratch_shapes=scratch,
        compiler_params=pltpu.CompilerParams(has_side_effects=True),
    )(*ins)


def _gather_phases(n, layer):
    def copies(ins, outs, sems):
        send1, recv1, send2, recv2 = sems
        x, y, c = _mesh_pos()
        out, back, fwd = [], [], []
        for i in range(n):
            for j, (px, py) in enumerate(_other_chips(x, y)):
                k = 3 * i + j
                blk = outs[i].at[2 * px + py]
                out.append(_remote(ins[i], outs[i].at[2 * x + y], send1.at[k], recv1.at[k], (px, py, c)))
                back.append(_remote(blk, blk, send1.at[k], recv1.at[k], (px, py, c)))
                fwd.append(_remote(blk, blk, send2.at[k], recv2.at[k], (x, y, 1 - c)))
        return c, out, back, fwd

    def first(ins, outs, sems):
        c, out, _, _ = copies(ins, outs, sems)

        @pl.when(c == layer)
        def _():
            for cp in out:
                cp.start()

    def middle(ins, outs, sems):
        c, _, back, fwd = copies(ins, outs, sems)

        @pl.when(c == layer)
        def _():
            for arrived, onward in zip(back, fwd):
                arrived.wait_recv()
                onward.start()

    def last(ins, outs, sems):
        c, out, _, fwd = copies(ins, outs, sems)

        @pl.when(c == layer)
        def _():
            for cp in out + fwd:
                cp.wait_send()

        @pl.when(c != layer)
        def _():
            for cp in fwd:
                cp.wait_recv()

    return first, middle, last


def _scatter_phases(n, layer):
    def copies(ins, outs, sems):
        send, recv = sems
        x, y, c = _mesh_pos()
        return c, [_remote(ins[i].at[2 * px + py], outs[i].at[j], send.at[3 * i + j], recv.at[3 * i + j], (px, py, c))
                   for i in range(n) for j, (px, py) in enumerate(_other_chips(x, y))]

    def first(ins, outs, sems):
        c, cps = copies(ins, outs, sems)

        @pl.when(c == layer)
        def _():
            for cp in cps:
                cp.start()

    def middle(ins, outs, sems):
        pass

    def last(ins, outs, sems):
        c, cps = copies(ins, outs, sems)

        @pl.when(c == layer)
        def _():
            for cp in cps:
                cp.wait()

    return first, middle, last


def _exchange(blocks, out_shapes, n_sems, phases, name, rider):
    sems = [pltpu.SemaphoreType.DMA((3 * len(blocks),))] * n_sems
    if rider:
        return _Rider(blocks, out_shapes, sems, phases)
    n = len(blocks)

    def body(*refs):
        for phase in phases:
            phase(refs[:n], refs[n:2 * n], refs[2 * n:])

    return list(_hbm_call(body, name, blocks, out_shapes, sems))


def _gather_layer(blocks, layer, name=None, rider=False):
    outs = [jax.ShapeDtypeStruct((N_CHIPS,) + b.shape, b.dtype) for b in blocks]
    return _exchange(blocks, outs, 4, _gather_phases(len(blocks), layer), name, rider)


def _scatter_layer(sums, layer, name=None, rider=False):
    outs = [jax.ShapeDtypeStruct((3,) + s.shape[1:], s.dtype) for s in sums]
    return _exchange(sums, outs, 2, _scatter_phases(len(sums), layer), name, rider)


def _to_sibling(gs, layer, name):
    n = len(gs)

    def body(*refs):
        ins, outs = refs[:n], refs[n:2 * n]
        send, recv = refs[2 * n:]
        x, y, c = _mesh_pos()
        cps = [_remote(ins[i], outs[i], send.at[i], recv.at[i], (x, y, 1 - c)) for i in range(n)]

        @pl.when(c != layer)
        def _():
            for cp in cps:
                cp.start()
            for cp in cps:
                cp.wait_send()

        @pl.when(c == layer)
        def _():
            for cp in cps:
                cp.wait_recv()

    sem = pltpu.SemaphoreType.DMA((n,))
    return list(_hbm_call(body, name, gs, [jax.ShapeDtypeStruct(g.shape, g.dtype) for g in gs], [sem, sem]))


def _sibling_swap(rs, name):
    n = len(rs)

    def body(*refs):
        ins, outs = refs[:n], refs[n:2 * n]
        send, recv = refs[2 * n:]
        x, y, c = _mesh_pos()
        cps = [_remote(ins[i], outs[i], send.at[i], recv.at[i], (x, y, 1 - c)) for i in range(n)]
        for cp in cps:
            cp.start()
        for cp in cps:
            cp.wait()

    sem = pltpu.SemaphoreType.DMA((n,))
    return _hbm_call(body, name, rs, [jax.ShapeDtypeStruct(r.shape, r.dtype) for r in rs], [sem, sem])


def _small_all_reduce(vec, name):
    r = vec.shape[0]
    ndev = 8

    def body(v_ref, o_ref, buf, send, recv):
        x, y, c = _mesh_pos()
        me = 4 * x + 2 * y + c
        buf[me] = v_ref[...]
        cps = []
        for rel in range(1, ndev):
            px = 1 - x if rel & 4 else x
            py = 1 - y if rel & 2 else y
            pc = 1 - c if rel & 1 else c
            cps.append((_remote(v_ref, buf.at[me], send.at[rel - 1], recv.at[rel - 1], (px, py, pc)),
                        4 * px + 2 * py + pc))
        for cp, _ in cps:
            cp.start()
        for k, (cp, peer) in enumerate(cps):
            slot = buf.at[peer]
            _remote(slot, slot, send.at[k], recv.at[k], (x, y, c)).wait_recv()
        for cp, _ in cps:
            cp.wait_send()
        acc = buf[0]
        for k in range(1, ndev):
            acc = acc + buf[k]
        o_ref[...] = acc

    vm = pl.BlockSpec(memory_space=pltpu.VMEM)
    return pl.pallas_call(
        body, name=name, out_shape=jax.ShapeDtypeStruct(vec.shape, F32), in_specs=[vm], out_specs=vm,
        scratch_shapes=[pltpu.VMEM((ndev, r, LANES), F32), pltpu.SemaphoreType.DMA((ndev - 1,)),
                        pltpu.SemaphoreType.DMA((ndev - 1,))],
        compiler_params=pltpu.CompilerParams(has_side_effects=True),
    )(vec)


def _row_tile(rows, cap=512):
    for t in range(min(rows, cap), 0, -1):
        if rows % t == 0 and (t % 16 == 0 or t == rows):
            return t
    raise ValueError(rows)


def _add_pairs(a, b, name):
    k, r, c = a.shape
    tr = _row_tile(r)

    def body(a_ref, b_ref, o_ref):
        o_ref[...] = (a_ref[...].astype(F32) + b_ref[...].astype(F32)).astype(o_ref.dtype)

    spec = pl.BlockSpec((None, tr, c), lambda i, j: (i, j, 0))
    return pl.pallas_call(body, name=name, grid=(k, r // tr), in_specs=[spec, spec], out_specs=spec,
                          out_shape=jax.ShapeDtypeStruct(a.shape, a.dtype),
                          compiler_params=_params(("parallel", "parallel")))(a, b)


def _final_sum(own, sib, others, name):
    r, c = own.shape
    tr = _row_tile(r)

    def body(a_ref, b_ref, o_ref_in, out_ref):
        acc = a_ref[...].astype(F32) + b_ref[...].astype(F32)
        for k in range(3):
            acc = acc + o_ref_in[k].astype(F32)
        out_ref[...] = acc

    spec = pl.BlockSpec((tr, c), lambda i: (i, 0))
    return pl.pallas_call(body, name=name, grid=(r // tr,),
                          in_specs=[spec, spec, pl.BlockSpec((3, tr, c), lambda i: (0, i, 0))], out_specs=spec,
                          out_shape=jax.ShapeDtypeStruct((r, c), F32),
                          compiler_params=_params(("parallel",)))(own, sib, others)


def _adamw(g, w, m, v, name):
    r, c = g.shape
    tr = _row_tile(r, 256)

    def body(g_ref, w_ref, m_ref, v_ref, d_ref, mo_ref, vo_ref):
        gv = g_ref[...]
        mn = ADAM_B1 * m_ref[...] + (1.0 - ADAM_B1) * gv
        vn = ADAM_B2 * v_ref[...] + (1.0 - ADAM_B2) * (gv * gv)
        m_hat = mn / (1.0 - ADAM_B1 ** ADAM_STEP)
        v_hat = vn / (1.0 - ADAM_B2 ** ADAM_STEP)
        d_ref[...] = -ADAM_LR * (m_hat / (jnp.sqrt(v_hat) + ADAM_EPS) + ADAM_WD * w_ref[...])
        mo_ref[...] = mn
        vo_ref[...] = vn

    spec = pl.BlockSpec((tr, c), lambda i: (i, 0))
    shp = jax.ShapeDtypeStruct((r, c), F32)
    return pl.pallas_call(body, name=name, grid=(r // tr,), in_specs=[spec] * 4, out_specs=[spec] * 3,
                          out_shape=[shp] * 3, compiler_params=_params(("parallel",)))(g, w, m, v)


def _pack(arrays):
    flat = jnp.concatenate([a.reshape(-1).astype(F32) for a in arrays])
    pad = (-flat.shape[0]) % (8 * LANES)
    return jnp.concatenate([flat, jnp.zeros((pad,), F32)]).reshape(-1, LANES)


def _unpack(packed, shapes):
    flat = packed.reshape(-1)
    out, off = [], 0
    for s in shapes:
        size = 1
        for dim in s:
            size *= dim
        out.append(flat[off:off + size].reshape(s))
        off += size
    return out


BIG = ("ffn1_w_in", "ffn1_w_out", "w_in", "w_out", "ffn2_w_in", "ffn2_w_out")
SMALL = ("ffn1_norm", "mix_norm", "fox_q_norm", "fox_k_norm", "fox_f_bias", "gdn_a_log", "gdn_dt_bias",
         "gdn_out_norm", "ffn2_norm", "gdn_conv")
WEIGHTS = ("ffn1_norm", "ffn1_w_in", "ffn1_w_out", "mix_norm", "w_in", "fox_q_norm", "fox_k_norm", "fox_f_bias",
           "gdn_conv", "gdn_a_log", "gdn_dt_bias", "gdn_out_norm", "w_out", "ffn2_norm", "ffn2_w_in", "ffn2_w_out")


def _step(x, target, w, m, v):
    xi, yi, ci = _mesh_pos()
    me = 2 * xi + yi
    depth = DEPTH
    d = x.shape[-1]

    nb, seq, _ = x.shape
    assert depth == 2

    p = {k: w[k] for k in SMALL if k != "gdn_conv"}
    for k in ("ffn1_w_in", "ffn1_w_out", "ffn2_w_in", "ffn2_w_out", "w_mix", "w_out", "gdn_conv"):
        p[k] = [None] * depth

    first, rest = BIG[:2], BIG[2:] + ("gdn_conv",)

    def shards(l, names):
        return [w[k][l] if k == "gdn_conv" else w[k][l].astype(BF) for k in names]

    def place(l, names, gathered):
        blocks = dict(zip(names, [lax.dynamic_update_index_in_dim(g, s, me, 0)
                                  for g, s in zip(gathered, shards(l, names))]))
        for k in ("ffn1_w_in", "ffn1_w_out", "ffn2_w_in", "ffn2_w_out"):
            if k in blocks:
                p[k][l] = blocks[k]
        if "w_in" in blocks:
            p["w_mix"][l] = _mix_to_padded(blocks["w_in"].transpose(1, 0, 2).reshape(d, N_IN))
            p["w_out"][l] = blocks["w_out"].reshape(2 * FOX_WIDTH, d)
            p["gdn_conv"][l] = blocks["gdn_conv"].transpose(1, 0, 2).reshape(CONV_WIDTH, -1)

    place(0, first, _gather_layer(shards(0, first), 0, "gather_first_ffn0"))
    xt = x.reshape(nb * seq, d)
    xt, saved0, gathered1 = _layer_fwd(
        xt, p, 0, nb, seq, _gather_layer(shards(1, first + rest), 1, rider=True),
        _gather_layer(shards(0, rest), 0, rider=True), lambda got: place(0, rest, got))
    place(1, first + rest, gathered1)
    xt, saved1, _ = _layer_fwd(xt, p, 1, nb, seq)
    loss, dx = _loss_grad(xt, target.reshape(nb * seq, d), "loss")

    def transport(g, names):
        out = []
        for k in names:
            if k == "w_in":
                out.append(g["w_in"].reshape(d, N_CHIPS, N_IN // N_CHIPS).transpose(1, 0, 2).astype(BF))
            elif k == "w_out":
                out.append(g["w_out"].reshape(N_CHIPS, -1, d))
            else:
                out.append(g[k])
        return out

    def chip_sums(g, l, names, tag):
        own = transport(g, names)
        sib = _to_sibling(own, l, f"grad{l}{tag}_to_sibling")
        return own, sib, [_add_pairs(a, b, f"grad{l}{tag}_chip_sum_{k}") for a, b, k in zip(own, sib, names)]

    dx, grads1, _, _ = _layer_bwd(dx, p, 1, saved1, nb, seq)
    own1, sib1, sums1 = chip_sums(grads1, 1, BIG, "")
    before = {}

    def before_ffn1(g):
        before["own"], before["sib"], sums = chip_sums(g, 0, BIG[2:], "_rest")
        return _scatter_layer(sums, 0, rider=True)

    dx, grads0, chips1, chips0_rest = _layer_bwd(dx, p, 0, saved0, nb, seq,
                                                 _scatter_layer(sums1, 1, rider=True), before_ffn1)
    own0, sib0, sums0 = chip_sums(grads0, 0, first, "_first")
    chips0 = _scatter_layer(sums0, 0, "grad0_first_to_chips") + chips0_rest
    own0, sib0 = own0 + before["own"], sib0 + before["sib"]
    grads = [grads0, grads1]
    dx = dx.reshape(nb, seq, d)

    mine = lambda a0, a1: jnp.where(ci == 0, a0, a1)
    at_me = lambda a: lax.dynamic_index_in_dim(a, me, 0, keepdims=False)
    reduced = [_final_sum(mine(at_me(own0[i]), at_me(own1[i])), mine(at_me(sib0[i]), at_me(sib1[i])),
                          mine(chips0[i], chips1[i]), f"grad_final_sum_{k}") for i, k in enumerate(BIG)]
    from_sib_final = _sibling_swap(reduced, "grad_swap_layers")
    full = {k: jnp.stack([jnp.where(ci == 0, a, b), jnp.where(ci == 0, b, a)])
            for k, a, b in zip(BIG, reduced, from_sib_final)}

    out_g, out_d, out_m, out_v = {}, {}, {}, {}
    for k in BIG:
        shp = w[k].shape
        two_d = lambda a: a.reshape(shp[0] * shp[1], shp[2])
        dl, mn, vn = _adamw(two_d(full[k]), two_d(w[k]), two_d(m[k]), two_d(v[k]), f"adamw_{k}")
        out_g[k], out_d[k], out_m[k], out_v[k] = full[k], dl.reshape(shp), mn.reshape(shp), vn.reshape(shp)

    small_local = [jnp.stack([grads[l][k] for l in range(depth)]) for k in SMALL]
    summed = _unpack(_small_all_reduce(_pack(small_local), "small_all_reduce"), [a.shape for a in small_local])
    sg = dict(zip(SMALL, summed))
    cs = w["gdn_conv"].shape[-1]
    sg["gdn_conv"] = lax.dynamic_slice_in_dim(sg["gdn_conv"], me * cs, cs, axis=2)
    shapes = [w[k].shape for k in SMALL]
    packs = [_pack([src[k] for k in SMALL]) for src in (sg, w, m, v)]
    dl, mn, vn = _adamw(*packs, "adamw_small")
    for k, a, b, c2 in zip(SMALL, _unpack(dl, shapes), _unpack(mn, shapes), _unpack(vn, shapes)):
        out_g[k], out_d[k], out_m[k], out_v[k] = sg[k], a, b, c2

    total = lax.psum(loss[0, 0], ("x", "y", "c"))
    return (total, dx, *[out_g[k] for k in WEIGHTS], *[out_d[k] for k in WEIGHTS],
            *[out_m[k] for k in WEIGHTS], *[out_v[k] for k in WEIGHTS])


def kernel(x, ffn1_norm, ffn1_w_in, ffn1_w_out, mix_norm, w_in, fox_q_norm, fox_k_norm, fox_f_bias, gdn_conv, gdn_a_log, gdn_dt_bias, gdn_out_norm, w_out, ffn2_norm, ffn2_w_in, ffn2_w_out, loss_target, m_ffn1_norm, m_ffn1_w_in, m_ffn1_w_out, m_mix_norm, m_w_in, m_fox_q_norm, m_fox_k_norm, m_fox_f_bias, m_gdn_conv, m_gdn_a_log, m_gdn_dt_bias, m_gdn_out_norm, m_w_out, m_ffn2_norm, m_ffn2_w_in, m_ffn2_w_out, v_ffn1_norm, v_ffn1_w_in, v_ffn1_w_out, v_mix_norm, v_w_in, v_fox_q_norm, v_fox_k_norm, v_fox_f_bias, v_gdn_conv, v_gdn_a_log, v_gdn_dt_bias, v_gdn_out_norm, v_w_out, v_ffn2_norm, v_ffn2_w_in, v_ffn2_w_out):
    w = dict(ffn1_norm=ffn1_norm, ffn1_w_in=ffn1_w_in, ffn1_w_out=ffn1_w_out, mix_norm=mix_norm, w_in=w_in,
             fox_q_norm=fox_q_norm, fox_k_norm=fox_k_norm, fox_f_bias=fox_f_bias, gdn_conv=gdn_conv,
             gdn_a_log=gdn_a_log, gdn_dt_bias=gdn_dt_bias, gdn_out_norm=gdn_out_norm, w_out=w_out,
             ffn2_norm=ffn2_norm, ffn2_w_in=ffn2_w_in, ffn2_w_out=ffn2_w_out)
    m = dict(ffn1_norm=m_ffn1_norm, ffn1_w_in=m_ffn1_w_in, ffn1_w_out=m_ffn1_w_out, mix_norm=m_mix_norm, w_in=m_w_in,
             fox_q_norm=m_fox_q_norm, fox_k_norm=m_fox_k_norm, fox_f_bias=m_fox_f_bias, gdn_conv=m_gdn_conv,
             gdn_a_log=m_gdn_a_log, gdn_dt_bias=m_gdn_dt_bias, gdn_out_norm=m_gdn_out_norm, w_out=m_w_out,
             ffn2_norm=m_ffn2_norm, ffn2_w_in=m_ffn2_w_in, ffn2_w_out=m_ffn2_w_out)
    v = dict(ffn1_norm=v_ffn1_norm, ffn1_w_in=v_ffn1_w_in, ffn1_w_out=v_ffn1_w_out, mix_norm=v_mix_norm, w_in=v_w_in,
             fox_q_norm=v_fox_q_norm, fox_k_norm=v_fox_k_norm, fox_f_bias=v_fox_f_bias, gdn_conv=v_gdn_conv,
             gdn_a_log=v_gdn_a_log, gdn_dt_bias=v_gdn_dt_bias, gdn_out_norm=v_gdn_out_norm, w_out=v_w_out,
             ffn2_norm=v_ffn2_norm, ffn2_w_in=v_ffn2_w_in, ffn2_w_out=v_ffn2_w_out)
    return _step(x, loss_target, w, m, v)
```

```python
import jax
import jax.numpy as jnp
from jax import lax
from jax.experimental import pallas as pl
from jax.experimental.pallas import tpu as pltpu

F32 = jnp.float32
BF = jnp.bfloat16
HI = lax.Precision.HIGHEST
MESH = pl.DeviceIdType.MESH

DEPTH = 2
FOX_HEADS = 8
FOX_HEAD_DIM = 64
FOX_WIDTH = 512
GDN_HEADS = 4
GDN_HEAD_DIM = 128
GDN_WIDTH = 512
CONV_WIDTH = 4
CHUNK = 64
EPS = 1e-6
N_IN = 3600
N_PAD = 3712
GATE_COL = 3584
LANES = 128
NEG = -1e30

ADAM_LR = 0.001
ADAM_B1 = 0.9
ADAM_B2 = 0.999
ADAM_EPS = 1e-08
ADAM_WD = 0.01
ADAM_STEP = 10

VMEM_LIMIT = 56 * 1024 * 1024


def _params(sem=None, **kw):
    return pltpu.CompilerParams(dimension_semantics=sem, vmem_limit_bytes=VMEM_LIMIT, **kw)


def _dot(a, b, precision=None):
    return jnp.dot(a, b, preferred_element_type=F32, precision=precision)


def _dot_nt(a, b, precision=None):
    return lax.dot_general(a, b, (((1,), (1,)), ((), ())), preferred_element_type=F32, precision=precision)


def _dot_tn(a, b, precision=None):
    return lax.dot_general(a, b, (((0,), (0,)), ((), ())), preferred_element_type=F32, precision=precision)


def _sigmoid(x):
    return 0.5 * jnp.tanh(0.5 * x) + 0.5


def _softplus(x):
    return jnp.maximum(x, 0.0) + jnp.log(1.0 + jnp.exp(-jnp.abs(x)))


def _log_sigmoid(x):
    return jnp.minimum(x, 0.0) - jnp.log(1.0 + jnp.exp(-jnp.abs(x)))


def _tile(n, t):
    t = min(n, t)
    assert n % t == 0, (n, t)
    return t


def _rms_fwd(x, gain):
    rstd = lax.rsqrt(jnp.mean(x * x, axis=-1, keepdims=True) + EPS)
    xhat = x * rstd
    return xhat * gain, xhat, rstd


def _rms_bwd(dy, xhat, rstd, gain):
    dxhat = dy * gain
    dx = rstd * (dxhat - xhat * jnp.mean(dxhat * xhat, axis=-1, keepdims=True))
    return dx, dy * xhat


def _full(shape):
    nd = len(shape)
    return pl.BlockSpec(shape, lambda *_: (0,) * nd)


HBM = pl.BlockSpec(memory_space=pltpu.HBM)


def _load_ffn_weights(win_hbm, wout_hbm, win_v, wout_v, sem):
    fr = wout_hbm.shape[1]
    copies = [pltpu.make_async_copy(win_hbm.at[s], win_v.at[s], sem.at[s]) for s in range(4)]
    copies += [pltpu.make_async_copy(wout_hbm.at[s], wout_v.at[pl.ds(s * fr, fr)], sem.at[4 + s])
               for s in range(4)]
    for c in copies:
        c.start()
    for c in copies:
        c.wait()


def _ffn_fwd(x, gain, win_g, wout_g, name, rider=None):
    t, d = x.shape
    _, _, fs = win_g.shape
    fr = wout_g.shape[1]
    tm = _tile(t, 256)
    r_in, r_out, r_sem = _rider_parts(rider)
    steps = t // tm

    def body(x_ref, g_ref, win_hbm, wout_hbm, *rest):
        rin, (xo_ref, h_ref) = rest[:len(r_in)], rest[len(r_in):len(r_in) + 2]
        rout = rest[len(r_in) + 2:len(r_in) + 2 + len(r_out)]
        win_v, wout_v, sem = rest[len(r_in) + 2 + len(r_out):len(r_in) + 5 + len(r_out)]
        riding = (rin, rout, rest[len(r_in) + 5 + len(r_out):])
        step = pl.program_id(0)
        _ride(rider, 0, step == 0, riding)
        _ride(rider, 1, step == steps // 2, riding)

        @pl.when(step == 0)
        def _():
            _load_ffn_weights(win_hbm, wout_hbm, win_v, wout_v, sem)

        xv = x_ref[...]
        hn, _, _ = _rms_fwd(xv, g_ref[...])
        hn = hn.astype(BF)
        acc = jnp.zeros((tm, d), F32)
        for s in range(2):
            g = _dot(hn, win_v[s])
            u = _dot(hn, win_v[s + 2])
            h_ref[:, s * fs:(s + 1) * fs] = g.astype(BF)
            h_ref[:, (s + 2) * fs:(s + 3) * fs] = u.astype(BF)
            a = (g * _sigmoid(g) * u).astype(BF)
            acc = acc + _dot(a, wout_v[s * fs:(s + 1) * fs, :])
        xo_ref[...] = xv + 0.5 * acc
        _ride(rider, 2, step == steps - 1, riding)

    return pl.pallas_call(
        body, name=name, grid=(steps,),
        in_specs=[pl.BlockSpec((tm, d), lambda i: (i, 0)), _full((1, d)), HBM, HBM] + [HBM] * len(r_in),
        out_specs=[pl.BlockSpec((tm, d), lambda i: (i, 0)), pl.BlockSpec((tm, 4 * fs), lambda i: (i, 0))]
        + [HBM] * len(r_out),
        out_shape=[jax.ShapeDtypeStruct((t, d), F32), jax.ShapeDtypeStruct((t, 4 * fs), BF)] + r_out,
        scratch_shapes=[pltpu.VMEM((4, d, fs), BF), pltpu.VMEM((4 * fr, d), BF), pltpu.SemaphoreType.DMA((8,))]
        + r_sem,
        compiler_params=_params(("arbitrary",), has_side_effects=rider is not None),
    )(x, gain, win_g, wout_g, *r_in)


def _ffn_bwd(dy, x, h, gain, win_g, wout_g, name, rider=None):
    t, d = x.shape
    _, _, fs = win_g.shape
    fr = wout_g.shape[1]
    tm = _tile(t, 256)
    r_in, r_out, r_sem = _rider_parts(rider)
    steps = t // tm

    def body(dy_ref, x_ref, h_ref, g_ref, win_hbm, wout_hbm, *rest):
        rin, (dx_ref, dh_ref, a_ref, hn_ref, dyh_ref, dg_ref) = rest[:len(r_in)], rest[len(r_in):len(r_in) + 6]
        rout = rest[len(r_in) + 6:len(r_in) + 6 + len(r_out)]
        win_v, wout_v, sem = rest[len(r_in) + 6 + len(r_out):len(r_in) + 9 + len(r_out)]
        riding = (rin, rout, rest[len(r_in) + 9 + len(r_out):])
        step = pl.program_id(0)
        _ride(rider, 0, step == 0, riding)
        _ride(rider, 1, step == steps // 2, riding)

        @pl.when(step == 0)
        def _():
            _load_ffn_weights(win_hbm, wout_hbm, win_v, wout_v, sem)
            dg_ref[...] = jnp.zeros_like(dg_ref)

        dyv = dy_ref[...]
        dyh = (0.5 * dyv).astype(BF)
        dyh_ref[...] = dyh
        dhn = jnp.zeros((tm, d), F32)
        for s in range(2):
            da = _dot_nt(dyh, wout_v[s * fs:(s + 1) * fs, :])
            g = h_ref[:, s * fs:(s + 1) * fs].astype(F32)
            u = h_ref[:, (s + 2) * fs:(s + 3) * fs].astype(F32)
            sg = _sigmoid(g)
            si = g * sg
            a_ref[:, s * fs:(s + 1) * fs] = (si * u).astype(BF)
            dgate = (da * u * (sg * (1.0 + g * (1.0 - sg)))).astype(BF)
            dup = (da * si).astype(BF)
            dh_ref[:, s * fs:(s + 1) * fs] = dgate
            dh_ref[:, (s + 2) * fs:(s + 3) * fs] = dup
            dhn = dhn + _dot_nt(dgate, win_v[s]) + _dot_nt(dup, win_v[s + 2])
        xv = x_ref[...]
        gain_v = g_ref[...]
        hn, xhat, rstd = _rms_fwd(xv, gain_v)
        hn_ref[...] = hn.astype(BF)
        dx, dgr = _rms_bwd(dhn, xhat, rstd, gain_v)
        dx_ref[...] = dyv + dx
        dg_ref[...] += jnp.sum(dgr, axis=0, keepdims=True)
        _ride(rider, 2, step == steps - 1, riding)

    row = lambda w: pl.BlockSpec((tm, w), lambda i: (i, 0))
    return pl.pallas_call(
        body, name=name, grid=(steps,),
        in_specs=[row(d), row(d), row(4 * fs), _full((1, d)), HBM, HBM] + [HBM] * len(r_in),
        out_specs=[row(d), row(4 * fs), row(2 * fs), row(d), row(d), _full((1, d))] + [HBM] * len(r_out),
        out_shape=[jax.ShapeDtypeStruct((t, d), F32), jax.ShapeDtypeStruct((t, 4 * fs), BF),
                   jax.ShapeDtypeStruct((t, 2 * fs), BF), jax.ShapeDtypeStruct((t, d), BF),
                   jax.ShapeDtypeStruct((t, d), BF), jax.ShapeDtypeStruct((1, d), F32)] + r_out,
        scratch_shapes=[pltpu.VMEM((4, d, fs), BF), pltpu.VMEM((4 * fr, d), BF), pltpu.SemaphoreType.DMA((8,))]
        + r_sem,
        compiler_params=_params(("arbitrary",), has_side_effects=rider is not None),
    )(dy, x, h, gain, win_g, wout_g, *r_in)


def _wgrad(a, b, out_shape, out_spec, tm, tn, name, tk=512):
    t, m = a.shape
    _, n = b.shape
    tk = _tile(t, tk)
    nk = t // tk

    def body(a_ref, b_ref, o_ref, acc):
        k = pl.program_id(2)

        @pl.when(k == 0)
        def _():
            acc[...] = jnp.zeros_like(acc)

        acc[...] += _dot_tn(a_ref[...], b_ref[...])

        @pl.when(k == nk - 1)
        def _():
            o_ref[...] = acc[...].astype(o_ref.dtype)

    return pl.pallas_call(
        body, name=name, grid=(m // tm, n // tn, nk),
        in_specs=[pl.BlockSpec((tk, tm), lambda i, j, k: (k, i)), pl.BlockSpec((tk, tn), lambda i, j, k: (k, j))],
        out_specs=out_spec, out_shape=out_shape,
        scratch_shapes=[pltpu.VMEM((tm, tn), F32)],
        compiler_params=_params(("parallel", "parallel", "arbitrary")),
    )(a, b)


def _norm_matmul(x, gain, w, name):
    t, d = x.shape
    n = w.shape[1]
    tm = _tile(t, 256)

    def body(x_ref, g_ref, w_ref, o_ref):
        hn, _, _ = _rms_fwd(x_ref[...], g_ref[...])
        o_ref[...] = _dot(hn.astype(BF), w_ref[...])

    return pl.pallas_call(
        body, name=name, grid=(t // tm,),
        in_specs=[pl.BlockSpec((tm, d), lambda i: (i, 0)), _full((1, d)), _full((d, n))],
        out_specs=pl.BlockSpec((tm, n), lambda i: (i, 0)),
        out_shape=jax.ShapeDtypeStruct((t, n), F32),
        compiler_params=_params(("parallel",)),
    )(x, gain, w)


def _norm_matmul_bwd(dres, dparts, x, gain, w, name):
    t, d = x.shape
    n = w.shape[1]
    tm = _tile(t, 256)
    widths = [a.shape[1] for a in dparts]
    assert sum(widths) == n
    k = len(dparts)

    def body(dr_ref, *rest):
        dp_refs, (x_ref, g_ref, w_ref, dx_ref, hn_ref, dg_ref) = rest[:k], rest[k:]

        @pl.when(pl.program_id(0) == 0)
        def _():
            dg_ref[...] = jnp.zeros_like(dg_ref)

        dhn, off = jnp.zeros((tm, d), F32), 0
        for dp_ref, wd in zip(dp_refs, widths):
            dhn = dhn + _dot_nt(dp_ref[...], w_ref[:, off:off + wd])
            off += wd
        gain_v = g_ref[...]
        hn, xhat, rstd = _rms_fwd(x_ref[...], gain_v)
        hn_ref[...] = hn.astype(BF)
        dx, dgr = _rms_bwd(dhn, xhat, rstd, gain_v)
        dx_ref[...] = dr_ref[...] + dx
        dg_ref[...] += jnp.sum(dgr, axis=0, keepdims=True)

    row = lambda wd: pl.BlockSpec((tm, wd), lambda i: (i, 0))
    return pl.pallas_call(
        body, name=name, grid=(t // tm,),
        in_specs=[row(d)] + [row(wd) for wd in widths] + [row(d), _full((1, d)), _full((d, n))],
        out_specs=[row(d), row(d), _full((1, d))],
        out_shape=[jax.ShapeDtypeStruct((t, d), F32), jax.ShapeDtypeStruct((t, d), BF),
                   jax.ShapeDtypeStruct((1, d), F32)],
        compiler_params=_params(("arbitrary",)),
    )(dres, *dparts, x, gain, w)


def _mix_out(x, yf, yg, w, name):
    t, d = x.shape
    kf = yf.shape[1]
    tm = _tile(t, 512)

    def body(x_ref, yf_ref, yg_ref, w_ref, o_ref):
        o_ref[...] = x_ref[...] + _dot(yf_ref[...], w_ref[0:kf, :]) + _dot(yg_ref[...], w_ref[kf:2 * kf, :])

    row = lambda wd: pl.BlockSpec((tm, wd), lambda i: (i, 0))
    return pl.pallas_call(
        body, name=name, grid=(t // tm,),
        in_specs=[row(d), row(kf), row(kf), _full((2 * kf, d))],
        out_specs=row(d), out_shape=jax.ShapeDtypeStruct((t, d), F32),
        compiler_params=_params(("parallel",)),
    )(x, yf, yg, w)


def _mix_out_bwd(dx, w, name):
    t, d = dx.shape
    kf = w.shape[0] // 2
    tm = _tile(t, 512)

    def body(dx_ref, w_ref, df_ref, dg_ref, dxb_ref):
        dxb = dx_ref[...].astype(BF)
        dxb_ref[...] = dxb
        df_ref[...] = _dot_nt(dxb, w_ref[0:kf, :]).astype(BF)
        dg_ref[...] = _dot_nt(dxb, w_ref[kf:2 * kf, :]).astype(BF)

    row = lambda wd: pl.BlockSpec((tm, wd), lambda i: (i, 0))
    return pl.pallas_call(
        body, name=name, grid=(t // tm,),
        in_specs=[row(d), _full((2 * kf, d))],
        out_specs=[row(kf), row(kf), row(d)],
        out_shape=[jax.ShapeDtypeStruct((t, kf), BF), jax.ShapeDtypeStruct((t, kf), BF),
                   jax.ShapeDtypeStruct((t, d), BF)],
        compiler_params=_params(("parallel",)),
    )(dx, w)


def _loss_grad(y, target, name):
    t, d = y.shape
    tm = _tile(t, 512)

    def body(y_ref, t_ref, l_ref, dy_ref):
        @pl.when(pl.program_id(0) == 0)
        def _():
            l_ref[...] = jnp.zeros_like(l_ref)

        diff = y_ref[...] - t_ref[...]
        dy_ref[...] = diff * (1.0 / d)
        part = jnp.sum(jnp.sum(diff * diff, axis=1, keepdims=True), axis=0, keepdims=True)
        l_ref[...] += part * (0.5 / d)

    row = pl.BlockSpec((tm, d), lambda i: (i, 0))
    return pl.pallas_call(
        body, name=name, grid=(t // tm,),
        in_specs=[row, row], out_specs=[_full((1, 1)), row],
        out_shape=[jax.ShapeDtypeStruct((1, 1), F32), jax.ShapeDtypeStruct((t, d), F32)],
        compiler_params=_params(("arbitrary",)),
    )(y, target)


def _head_sum_matrix(width, head):
    r = lax.broadcasted_iota(jnp.int32, (width, width), 0) // head
    c = lax.broadcasted_iota(jnp.int32, (width, width), 1) // head
    return (r == c).astype(BF)


def _head_mean(x, bd):
    return _dot(x.astype(BF), bd) * (1.0 / FOX_HEAD_DIM)


def _mask_dot(mask01, x):
    mb = mask01.astype(BF)
    hi = x.astype(BF)
    r1 = x - hi.astype(F32)
    mid = r1.astype(BF)
    lo = (r1 - mid.astype(F32)).astype(BF)
    return _dot(mb, hi) + _dot(mb, mid) + _dot(mb, lo)


def _fox_prep(proj, wq_t, wk_t, bias_pad, seq, name):
    t = proj.shape[0]
    ts = _tile(seq, 512)
    tpe = seq // ts
    scale = FOX_HEAD_DIM ** -0.5

    def body(q_ref, k_ref, v_ref, gt_ref, wq_ref, wk_ref, b_ref, qo_ref, ko_ref, vo_ref, cum_ref, carry):
        i = pl.program_id(0)
        bd = _head_sum_matrix(FOX_WIDTH, FOX_HEAD_DIM)

        def norm(xv, wv):
            ms = _head_mean(xv * xv, bd)
            return xv * lax.rsqrt(ms + EPS) * wv

        qo_ref[...] = (norm(q_ref[...], wq_ref[...]) * scale).astype(BF)
        ko_ref[...] = norm(k_ref[...], wk_ref[...]).astype(BF)
        vo_ref[...] = v_ref[...].astype(BF)

        @pl.when(i % tpe == 0)
        def _():
            carry[...] = jnp.zeros_like(carry)

        ls = _log_sigmoid(gt_ref[...] + b_ref[...])
        r = lax.broadcasted_iota(jnp.int32, (ts, ts), 0)
        c = lax.broadcasted_iota(jnp.int32, (ts, ts), 1)
        cum = _mask_dot(r >= c, ls) + carry[...]
        cum_ref[...] = cum
        carry[...] = cum[ts - 1:ts, :]

    blk = lambda j: pl.BlockSpec((ts, FOX_WIDTH), lambda i: (i, j))
    gate = pl.BlockSpec((ts, LANES), lambda i: (i, GATE_COL // LANES))
    out = pl.BlockSpec((ts, FOX_WIDTH), lambda i: (i, 0))
    return pl.pallas_call(
        body, name=name, grid=(t // ts,),
        in_specs=[blk(0), blk(1), blk(2), gate, _full((1, FOX_WIDTH)), _full((1, FOX_WIDTH)), _full((1, LANES))],
        out_specs=[out, out, out, pl.BlockSpec((ts, LANES), lambda i: (i, 0))],
        out_shape=[jax.ShapeDtypeStruct((t, FOX_WIDTH), BF)] * 3 + [jax.ShapeDtypeStruct((t, LANES), F32)],
        scratch_shapes=[pltpu.VMEM((1, LANES), F32)],
        compiler_params=_params(("arbitrary",)),
    )(proj, proj, proj, proj, wq_t, wk_t, bias_pad)


def _pick_lanes(x, lane_in_block, first_out_lane):
    r = lax.broadcasted_iota(jnp.int32, (FOX_WIDTH, LANES), 0)
    c = lax.broadcasted_iota(jnp.int32, (FOX_WIDTH, LANES), 1)
    sel = ((r % LANES == lane_in_block) & (c == first_out_lane + 2 * (r // LANES))).astype(BF)
    hi = x.astype(BF)
    r1 = x - hi.astype(F32)
    mid = r1.astype(BF)
    lo = (r1 - mid.astype(F32)).astype(BF)
    return _dot(hi, sel) + _dot(mid, sel) + _dot(lo, sel)


def _fox_prep_bwd(proj, dqa, dqb, dk, dv, dkx, wq_t, wk_t, bias_pad, seq, name):
    t = proj.shape[0]
    ts = _tile(seq, 512)
    tpe = seq // ts
    nt = t // ts
    scale = FOX_HEAD_DIM ** -0.5

    def body(q_ref, k_ref, gt_ref, dqa_ref, dqb_ref, dk_ref, dv_ref, dc_ref, wq_ref, wk_ref, b_ref,
             dp_ref, dff_ref, dwq_ref, dwk_ref, db_ref, carry):
        i = pl.program_id(0)
        first = (lax.broadcasted_iota(jnp.int32, (ts, FOX_WIDTH), 1) % LANES) < FOX_HEAD_DIM
        dq_all = jnp.where(first, dqa_ref[...], dqb_ref[...])
        ti = nt - 1 - i
        bd = _head_sum_matrix(FOX_WIDTH, FOX_HEAD_DIM)

        @pl.when(i == 0)
        def _():
            dwq_ref[...] = jnp.zeros_like(dwq_ref)
            dwk_ref[...] = jnp.zeros_like(dwk_ref)
            db_ref[...] = jnp.zeros_like(db_ref)

        def norm_bwd(xv, wv, dyv):
            ms = _head_mean(xv * xv, bd)
            rstd = lax.rsqrt(ms + EPS)
            xhat = xv * rstd
            dxhat = dyv * wv
            mean = _head_mean(dxhat * xhat, bd)
            return rstd * (dxhat - xhat * mean), jnp.sum(dyv * xhat, axis=0, keepdims=True)

        dxq, dwq = norm_bwd(q_ref[...], wq_ref[...], dq_all * scale)
        dxk, dwk = norm_bwd(k_ref[...], wk_ref[...], dk_ref[...])
        dp_ref[:, 0:FOX_WIDTH] = dxq.astype(BF)
        dp_ref[:, FOX_WIDTH:2 * FOX_WIDTH] = dxk.astype(BF)
        dp_ref[:, 2 * FOX_WIDTH:3 * FOX_WIDTH] = dv_ref[...].astype(BF)
        dwq_ref[...] += dwq
        dwk_ref[...] += dwk

        @pl.when(ti % tpe == tpe - 1)
        def _():
            carry[...] = jnp.zeros_like(carry)

        r = lax.broadcasted_iota(jnp.int32, (ts, ts), 0)
        c = lax.broadcasted_iota(jnp.int32, (ts, ts), 1)
        dkx = dc_ref[...]
        hd = FOX_HEAD_DIM
        dcum = (_pick_lanes(dqa_ref[...], hd, 0) + _pick_lanes(dqb_ref[...], 0, 1)
                - _pick_lanes(dkx, hd, 0) - _pick_lanes(dkx, 0, 1))
        dls = _mask_dot(c >= r, dcum) + carry[...]
        carry[...] = dls[0:1, :]
        z = gt_ref[...] + b_ref[...]
        lane = lax.broadcasted_iota(jnp.int32, (ts, LANES), 1)
        dff = jnp.where(lane < FOX_HEADS, dls * _sigmoid(-z), 0.0)
        dff_ref[...] = dff
        db_ref[...] += jnp.sum(dff, axis=0, keepdims=True)

        @pl.when(i == nt - 1)
        def _():
            fr = lax.broadcasted_iota(jnp.int32, (FOX_WIDTH, FOX_WIDTH), 0) % FOX_HEAD_DIM
            fc = lax.broadcasted_iota(jnp.int32, (FOX_WIDTH, FOX_WIDTH), 1) % FOX_HEAD_DIM
            fold = (fr == fc).astype(F32)
            dwq_ref[...] = _dot(dwq_ref[...], fold, HI)
            dwk_ref[...] = _dot(dwk_ref[...], fold, HI)

    rev = lambda w, j: pl.BlockSpec((ts, w), lambda i: (nt - 1 - i, j))
    return pl.pallas_call(
        body, name=name, grid=(nt,),
        in_specs=[rev(FOX_WIDTH, 0), rev(FOX_WIDTH, 1), rev(LANES, GATE_COL // LANES),
                  rev(FOX_WIDTH, 0), rev(FOX_WIDTH, 0), rev(FOX_WIDTH, 0), rev(FOX_WIDTH, 0), rev(FOX_WIDTH, 0),
                  _full((1, FOX_WIDTH)), _full((1, FOX_WIDTH)), _full((1, LANES))],
        out_specs=[rev(3 * FOX_WIDTH, 0), rev(LANES, 0), _full((1, FOX_WIDTH)), _full((1, FOX_WIDTH)),
                   _full((1, LANES))],
        out_shape=[jax.ShapeDtypeStruct((t, 3 * FOX_WIDTH), BF), jax.ShapeDtypeStruct((t, LANES), F32),
                   jax.ShapeDtypeStruct((1, FOX_WIDTH), F32), jax.ShapeDtypeStruct((1, FOX_WIDTH), F32),
                   jax.ShapeDtypeStruct((1, LANES), F32)],
        scratch_shapes=[pltpu.VMEM((1, LANES), F32)],
        compiler_params=_params(("arbitrary",)),
    )(proj, proj, proj, dqa, dqb, dk, dv, dkx, wq_t, wk_t, bias_pad)


class _Rider:
    def __init__(self, inputs, out_shapes, sems, phases):
        self.inputs, self.out_shapes, self.sems, self.phases = list(inputs), list(out_shapes), list(sems), phases


def _rider_parts(rider):
    if rider is None:
        return [], [], []
    return rider.inputs, rider.out_shapes, rider.sems


def _ride(rider, which, when, refs):
    if rider is not None:
        @pl.when(when)
        def _():
            rider.phases[which](*refs)


def _fox_attn(q, k, v, ck, nb, seq, name, rider=None):
    t = q.shape[0]
    tq = _tile(seq, 1024)
    nq = seq // tq
    npair = FOX_HEADS // 2
    hd = FOX_HEAD_DIM
    r_in, r_out, r_sem = _rider_parts(rider)
    steps = nb * npair * nq

    def body(q_ref, k_ref, v_ref, ck_ref, *rest):
        rin, (o_ref, lse_ref) = rest[:len(r_in)], rest[len(r_in):len(r_in) + 2]
        rout = rest[len(r_in) + 2:len(r_in) + 2 + len(r_out)]
        m_s, acc_s = rest[len(r_in) + 2 + len(r_out):len(r_in) + 4 + len(r_out)]
        riding = (rin, rout, rest[len(r_in) + 4 + len(r_out):])
        step = (pl.program_id(0) * npair + pl.program_id(1)) * nq + pl.program_id(2)
        _ride(rider, 0, step == 0, riding)
        _ride(rider, 1, step == steps // 2, riding)
        qi = pl.program_id(2)
        lane = lax.broadcasted_iota(jnp.int32, (tq, LANES), 1)
        m_s[...] = jnp.full(m_s.shape, NEG, F32)
        acc_s[...] = jnp.zeros_like(acc_s)
        qv = q_ref[...]

        def tile(kj, on_diagonal):
            cols = pl.ds(pl.multiple_of(kj * tq, tq), tq)
            kv = k_ref[cols, :]
            vv = v_ref[cols, :]
            if on_diagonal:
                causal = (lax.broadcasted_iota(jnp.int32, (tq, tq), 0)
                          >= lax.broadcasted_iota(jnp.int32, (tq, tq), 1))
            ck = [ck_ref[hh, :, cols] for hh in range(2)]
            m_old = [m_s[hh] for hh in range(2)]
            acc_old = [acc_s[hh] for hh in range(2)]
            m_out, acc_out = [], []
            for hh in range(2):
                hm = (lane >= hd) if hh else (lane < hd)
                qh = jnp.where(hm, qv, jnp.zeros_like(qv))
                s = _dot_nt(qh, kv) - ck[hh]
                if on_diagonal:
                    s = jnp.where(causal, s, NEG)
                m_new = jnp.maximum(m_old[hh], jnp.max(s, axis=-1, keepdims=True))
                p = jnp.exp(s - m_new)
                alpha = jnp.exp(m_old[hh] - m_new)
                m_out.append(m_new)
                acc_out.append(alpha * acc_old[hh] + _dot(p.astype(BF), jnp.where(hm, vv, jnp.ones_like(vv))))
            for hh in range(2):
                m_s[hh] = m_out[hh]
                acc_s[hh] = acc_out[hh]

        def off_diagonal(kj, carry):
            tile(kj, False)
            return carry

        lax.fori_loop(0, qi, off_diagonal, 0)
        tile(qi, True)
        a0 = acc_s[0]
        a1 = acc_s[1]
        den = jnp.where(lane < hd, pltpu.roll(a0, hd, axis=1), pltpu.roll(a1, hd, axis=1))
        o_ref[...] = (jnp.where(lane < hd, a0, a1) / den).astype(o_ref.dtype)
        l0 = jnp.sum(jnp.where(lane == hd, a0, 0.0), axis=1, keepdims=True)
        l1 = jnp.sum(jnp.where(lane == 0, a1, 0.0), axis=1, keepdims=True)
        lse_ref[0] = m_s[0] + jnp.log(l0)
        lse_ref[1] = m_s[1] + jnp.log(l1)
        _ride(rider, 2, step == steps - 1, riding)

    qspec = pl.BlockSpec((tq, LANES), lambda b, p, i: (b * nq + i, p))
    kspec = pl.BlockSpec((seq, LANES), lambda b, p, i: (b, p))
    colspec = pl.BlockSpec((None, 2, tq, 1), lambda b, p, i: (b * npair + p, 0, i, 0))
    rowspec = pl.BlockSpec((None, 2, 1, seq), lambda b, p, i: (b * npair + p, 0, 0, 0))
    sem = ("arbitrary",) * 3 if rider else ("parallel",) * 3
    return pl.pallas_call(
        body, name=name, grid=(nb, npair, nq),
        in_specs=[qspec, kspec, kspec, rowspec] + [HBM] * len(r_in),
        out_specs=[qspec, colspec] + [HBM] * len(r_out),
        out_shape=[jax.ShapeDtypeStruct((t, FOX_WIDTH), BF), jax.ShapeDtypeStruct((nb * npair, 2, seq, 1), F32)]
        + r_out,
        scratch_shapes=[pltpu.VMEM((2, tq, 1), F32), pltpu.VMEM((2, tq, LANES), F32)] + r_sem,
        compiler_params=_params(sem, has_side_effects=rider is not None),
    )(q, k, v, ck, *r_in)


def _fox_attn_bwd(q, k, v, o, do, lse, ck, nb, seq, name, rider=None):
    t = q.shape[0]
    tq = _tile(seq, 1024)
    nq = seq // tq
    npair = FOX_HEADS // 2
    hd = FOX_HEAD_DIM
    r_in, r_out, r_sem = _rider_parts(rider)
    steps = nb * npair * nq

    def body(q_ref, k_ref, v_ref, o_ref, do_ref, lse_ref, ck_ref, *rest):
        rin, (dqa_ref, dqb_ref, dk_ref, dv_ref, dkx_ref) = rest[:len(r_in)], rest[len(r_in):len(r_in) + 5]
        rout = rest[len(r_in) + 5:len(r_in) + 5 + len(r_out)]
        dk_s, dv_s = rest[len(r_in) + 5 + len(r_out):len(r_in) + 7 + len(r_out)]
        riding = (rin, rout, rest[len(r_in) + 7 + len(r_out):])
        step = (pl.program_id(0) * npair + pl.program_id(1)) * nq + pl.program_id(2)
        _ride(rider, 0, step == 0, riding)
        _ride(rider, 1, step == steps // 2, riding)
        kj = pl.program_id(2)
        lane = lax.broadcasted_iota(jnp.int32, (tq, LANES), 1)

        @pl.when(kj == 0)
        def _():
            dqa_ref[...] = jnp.zeros_like(dqa_ref)
            dqb_ref[...] = jnp.zeros_like(dqb_ref)

        dk_s[...] = jnp.zeros_like(dk_s)
        dv_s[...] = jnp.zeros_like(dv_s)
        kv = k_ref[...]
        vv = v_ref[...]

        def tile(qi, on_diagonal):
            rows = pl.ds(pl.multiple_of(qi * tq, tq), tq)
            qv = q_ref[rows, :]
            dov = do_ref[rows, :]
            prod = dov.astype(F32) * o_ref[rows, :].astype(F32)
            if on_diagonal:
                causal = (lax.broadcasted_iota(jnp.int32, (tq, tq), 0)
                          >= lax.broadcasted_iota(jnp.int32, (tq, tq), 1))
            for hh, dq_ref in ((0, dqa_ref), (1, dqb_ref)):
                hm = (lane >= hd) if hh else (lane < hd)
                zero = jnp.zeros_like(qv)
                one = jnp.ones_like(qv)
                doh = jnp.where(hm, dov, zero)
                delta = jnp.sum(jnp.where(hm, prod, 0.0), axis=-1, keepdims=True)
                s = _dot_nt(jnp.where(hm, qv, zero), kv) - ck_ref[hh]
                if on_diagonal:
                    s = jnp.where(causal, s, NEG)
                p = jnp.exp(s - lse_ref[hh, rows, :])
                dp = _dot_nt(doh, vv)
                dsb = (p * (dp - delta)).astype(BF)
                dv_s[...] += _dot_tn(p.astype(BF), doh)
                dk_s[hh] += _dot_tn(dsb, jnp.where(hm, qv, one))
                dq_ref[rows, :] += _dot(dsb, jnp.where(hm, kv, one))

        def off_diagonal(qi, carry):
            tile(qi, False)
            return carry

        tile(kj, True)
        lax.fori_loop(kj + 1, nq, off_diagonal, 0)
        dk_ref[...] = jnp.where(lane < hd, dk_s[0], dk_s[1])
        dkx_ref[...] = jnp.where(lane < hd, dk_s[1], dk_s[0])
        dv_ref[...] = dv_s[...]
        _ride(rider, 2, step == steps - 1, riding)

    kspec = pl.BlockSpec((tq, LANES), lambda b, p, j: (b * nq + j, p))
    full_q = pl.BlockSpec((seq, LANES), lambda b, p, j: (b, p))
    colspec = pl.BlockSpec((None, 2, seq, 1), lambda b, p, j: (b * npair + p, 0, 0, 0))
    rowspec = pl.BlockSpec((None, 2, 1, tq), lambda b, p, j: (b * npair + p, 0, 0, j))
    sem = ("arbitrary",) * 3 if rider else ("parallel", "parallel", "arbitrary")
    return pl.pallas_call(
        body, name=name, grid=(nb, npair, nq),
        in_specs=[full_q, kspec, kspec, full_q, full_q, colspec, rowspec] + [HBM] * len(r_in),
        out_specs=[full_q, full_q, kspec, kspec, kspec] + [HBM] * len(r_out),
        out_shape=[jax.ShapeDtypeStruct((t, FOX_WIDTH), F32)] * 5 + r_out,
        scratch_shapes=[pltpu.VMEM((2, tq, LANES), F32), pltpu.VMEM((tq, LANES), F32)] + r_sem,
        compiler_params=_params(sem, has_side_effects=rider is not None),
    )(q, k, v, o, do, lse, ck, *r_in)


GDN_QKV = 3 * GDN_WIDTH
GDN_COL = 3 * FOX_WIDTH
GG_COL = GDN_COL + GDN_QKV
A_LANE = FOX_HEADS
B_LANE = FOX_HEADS + GDN_HEADS
HALO = 8


def _gate_lanes(ts):
    lane = lax.broadcasted_iota(jnp.int32, (ts, LANES), 1)
    return (lane >= A_LANE) & (lane < B_LANE), (lane >= B_LANE) & (lane < B_LANE + GDN_HEADS)


def _chunk_tri(ts, upper):
    r = lax.broadcasted_iota(jnp.int32, (ts, ts), 0)
    c = lax.broadcasted_iota(jnp.int32, (ts, ts), 1)
    same = (r // CHUNK) == (c // CHUNK)
    return (same & ((c >= r) if upper else (r >= c))).astype(F32)


def _conv_silu_l2(xp_ref, w, ts):
    c = w[0:1, :] * xp_ref[pl.ds(HALO - 3, ts), :]
    for kk in range(1, CONV_WIDTH):
        c = c + w[kk:kk + 1, :] * xp_ref[pl.ds(HALO - 3 + kk, ts), :]
    return c, c * _sigmoid(c)


def _gdn_prep(proj, conv_w, a_pad, dt_pad, seq, name):
    t = proj.shape[0]
    ts = _tile(seq, 256)
    tpe = seq // ts
    qscale = GDN_HEAD_DIM ** -0.5

    def body(x_ref, gt_ref, w_ref, a_ref, dt_ref, qo_ref, ko_ref, vo_ref, go_ref, xp):
        i = pl.program_id(0)
        tail = xp[pl.ds(ts, HALO), :]
        xp[pl.ds(0, HALO), :] = jnp.where(i % tpe == 0, jnp.zeros_like(tail), tail)
        xp[pl.ds(HALO, ts), :] = x_ref[...]
        _, s = _conv_silu_l2(xp, w_ref[...], ts)
        for h in range(GDN_HEADS):
            for base, ref, sc in ((0, qo_ref, qscale), (GDN_WIDTH, ko_ref, 1.0)):
                xh = s[:, base + h * LANES: base + (h + 1) * LANES]
                r = lax.rsqrt(jnp.sum(xh * xh, axis=-1, keepdims=True) + EPS)
                ref[:, h * LANES:(h + 1) * LANES] = (xh * (r * sc)).astype(BF)
        vo_ref[...] = s[:, 2 * GDN_WIDTH:].astype(BF)
        gate = gt_ref[...]
        g_raw = -jnp.exp(a_ref[...]) * _softplus(gate + dt_ref[...])
        gc = _mask_dot(_chunk_tri(ts, False), g_raw)
        is_a, is_b = _gate_lanes(ts)
        go_ref[...] = jnp.where(is_a, gc, jnp.where(is_b, _sigmoid(gate), 0.0))

    out = pl.BlockSpec((ts, GDN_WIDTH), lambda i: (i, 0))
    lanes = pl.BlockSpec((ts, LANES), lambda i: (i, 0))
    return pl.pallas_call(
        body, name=name, grid=(t // ts,),
        in_specs=[pl.BlockSpec((ts, GDN_QKV), lambda i: (i, GDN_COL // GDN_QKV)),
                  pl.BlockSpec((ts, LANES), lambda i: (i, GATE_COL // LANES)),
                  _full((CONV_WIDTH, GDN_QKV)), _full((1, LANES)), _full((1, LANES))],
        out_specs=[out, out, out, lanes],
        out_shape=[jax.ShapeDtypeStruct((t, GDN_WIDTH), BF)] * 3 + [jax.ShapeDtypeStruct((t, LANES), F32)],
        scratch_shapes=[pltpu.VMEM((ts + HALO, GDN_QKV), F32)],
        compiler_params=_params(("arbitrary",)),
    )(proj, proj, conv_w, a_pad, dt_pad)


def _gdn_prep_bwd(proj, dq, dk, dv, dgates, dff, conv_w, a_pad, dt_pad, seq, name):
    t = proj.shape[0]
    ts = _tile(seq, 256)
    tpe = seq // ts
    nt = t // ts
    qscale = GDN_HEAD_DIM ** -0.5
    hb = ts // HALO

    def body(x_ref, halo_ref, gt_ref, dq_ref, dk_ref, dv_ref, dgt_ref, dff_ref, w_ref, a_ref, dt_ref,
             dx_ref, dgo_ref, dw_ref, da_ref, ddt_ref, xp, dcp, carry):
        i = pl.program_id(0)
        ti = nt - 1 - i

        @pl.when(i == 0)
        def _():
            dw_ref[...] = jnp.zeros_like(dw_ref)
            da_ref[...] = jnp.zeros_like(da_ref)
            ddt_ref[...] = jnp.zeros_like(ddt_ref)

        halo = halo_ref[...]
        xp[pl.ds(0, HALO), :] = jnp.where(ti % tpe == 0, jnp.zeros_like(halo), halo)
        xp[pl.ds(HALO, ts), :] = x_ref[...]
        w = w_ref[...]
        c, s = _conv_silu_l2(xp, w, ts)
        for h in range(GDN_HEADS):
            for base, ref, sc in ((0, dq_ref, qscale), (GDN_WIDTH, dk_ref, 1.0)):
                lo = base + h * LANES
                xh = s[:, lo:lo + LANES]
                r = lax.rsqrt(jnp.sum(xh * xh, axis=-1, keepdims=True) + EPS)
                y = xh * r
                dy = ref[:, h * LANES:(h + 1) * LANES] * sc
                dcp[pl.ds(0, ts), lo:lo + LANES] = r * (dy - y * jnp.sum(dy * y, axis=-1, keepdims=True))
        dcp[pl.ds(0, ts), 2 * GDN_WIDTH:] = dv_ref[...]
        sg = _sigmoid(c)
        dc = dcp[pl.ds(0, ts), :] * (sg * (1.0 + c * (1.0 - sg)))
        dcp[pl.ds(0, ts), :] = dc
        nxt = carry[...]
        dcp[pl.ds(ts, HALO), :] = jnp.where(ti % tpe == tpe - 1, jnp.zeros_like(nxt), nxt)
        carry[...] = dc[0:HALO, :]
        dx = w[CONV_WIDTH - 1:CONV_WIDTH, :] * dc
        for kk in range(CONV_WIDTH - 1):
            dx = dx + w[kk:kk + 1, :] * dcp[pl.ds(CONV_WIDTH - 1 - kk, ts), :]
        dx_ref[...] = dx.astype(BF)
        for kk in range(CONV_WIDTH):
            dw_ref[kk:kk + 1, :] += jnp.sum(dc * xp[pl.ds(HALO - 3 + kk, ts), :], axis=0, keepdims=True)
        gate = gt_ref[...]
        dgt = dgt_ref[...]
        is_a, is_b = _gate_lanes(ts)
        dg_raw = _mask_dot(_chunk_tri(ts, True), jnp.where(is_a, dgt, 0.0))
        z = gate + dt_ref[...]
        na = -jnp.exp(a_ref[...])
        dga = dg_raw * na * _sigmoid(z)
        beta = _sigmoid(gate)
        dgb = jnp.where(is_b, dgt * beta * (1.0 - beta), 0.0)
        dgo_ref[...] = (dff_ref[...] + dga + dgb).astype(BF)
        ddt_ref[...] += jnp.sum(dga, axis=0, keepdims=True)
        da_ref[...] += jnp.sum(dg_raw * na * _softplus(z), axis=0, keepdims=True)

    rev = lambda wd, j: pl.BlockSpec((ts, wd), lambda i: (nt - 1 - i, j))
    halo_spec = pl.BlockSpec((HALO, GDN_QKV), lambda i: (jnp.maximum((nt - 1 - i) * hb - 1, 0), GDN_COL // GDN_QKV))
    return pl.pallas_call(
        body, name=name, grid=(nt,),
        in_specs=[rev(GDN_QKV, GDN_COL // GDN_QKV), halo_spec, rev(LANES, GATE_COL // LANES),
                  rev(GDN_WIDTH, 0), rev(GDN_WIDTH, 0), rev(GDN_WIDTH, 0), rev(LANES, 0), rev(LANES, 0),
                  _full((CONV_WIDTH, GDN_QKV)), _full((1, LANES)), _full((1, LANES))],
        out_specs=[rev(GDN_QKV, 0), rev(LANES, 0), _full((CONV_WIDTH, GDN_QKV)), _full((1, LANES)),
                   _full((1, LANES))],
        out_shape=[jax.ShapeDtypeStruct((t, GDN_QKV), BF), jax.ShapeDtypeStruct((t, LANES), BF),
                   jax.ShapeDtypeStruct((CONV_WIDTH, GDN_QKV), F32), jax.ShapeDtypeStruct((1, LANES), F32),
                   jax.ShapeDtypeStruct((1, LANES), F32)],
        scratch_shapes=[pltpu.VMEM((ts + HALO, GDN_QKV), F32), pltpu.VMEM((ts + HALO, GDN_QKV), F32),
                        pltpu.VMEM((HALO, GDN_QKV), F32)],
        compiler_params=_params(("arbitrary",)),
    )(proj, proj, proj, dq, dk, dv, dgates, dff, conv_w, a_pad, dt_pad)


PAIR = 2 * CHUNK


def _split_bf16(a):
    hi = a.astype(BF)
    return hi, (a - hi.astype(F32)).astype(BF)


def _dot3(a, b, dims=(((1,), (0,)), ((), ()))):
    ah, al = _split_bf16(a)
    bh, bl = _split_bf16(b)
    (ca,), (cb,) = dims[0]
    return lax.dot_general(jnp.concatenate([ah, al, ah], axis=ca), jnp.concatenate([bh, bh, bl], axis=cb), dims,
                           preferred_element_type=F32)


def _inv_unit_lower(a):
    r = lax.broadcasted_iota(jnp.int32, (PAIR, PAIR), 0)
    c = lax.broadcasted_iota(jnp.int32, (PAIR, PAIR), 1)
    tm = (r == c).astype(F32) - a
    pw = _dot3(a, a)
    for _ in range(4):
        x = _dot3(jnp.concatenate([tm, pw], axis=0), pw)
        tm = tm + x[:PAIR]
        pw = x[PAIR:]
    return tm + _dot3(tm, pw)


def _gdn_pair_local(q, k, v, gc, gr, b):
    r = lax.broadcasted_iota(jnp.int32, (PAIR, PAIR), 0)
    c = lax.broadcasted_iota(jnp.int32, (PAIR, PAIR), 1)
    same = (r // CHUNK) == (c // CHUNK)
    incl = same & (r >= c)
    strict = same & (r > c)
    dm = jnp.exp(jnp.where(incl, gc - gr, NEG))
    e = jnp.exp(gc)
    kb = k * b
    vb = v * b
    kbe = kb * e
    kq = _dot_nt(jnp.concatenate([kb, q], axis=0).astype(BF), k.astype(BF))
    amat = jnp.where(strict, kq[:PAIR] * dm, 0.0)
    pmat = jnp.where(incl, kq[PAIR:] * dm, 0.0)
    lane = lax.broadcasted_iota(jnp.int32, (1, PAIR), 1)
    gl_a = jnp.sum(jnp.where(lane == CHUNK - 1, gr, 0.0), axis=1, keepdims=True)
    gl_b = jnp.sum(jnp.where(lane == PAIR - 1, gr, 0.0), axis=1, keepdims=True)
    ridx = lax.broadcasted_iota(jnp.int32, (PAIR, 1), 0)
    edec = jnp.exp(jnp.where(ridx < CHUNK, gl_a, gl_b) - gc)
    return dict(dm=dm, e=e, kb=kb, vb=vb, kbe=kbe, amat=amat, pmat=pmat, gl_a=gl_a, gl_b=gl_b, edec=edec,
                kd=k * edec, qd=q * e, incl=incl, strict=strict, ridx=ridx)


def _gdn_pair_states(loc, tb, s_a):
    uw = _dot(tb, jnp.concatenate([loc["vb"], loc["kbe"]], axis=1).astype(BF))
    u, w = uw[:, :LANES], uw[:, LANES:]
    qd, kd, c = loc["qd"], loc["kd"], CHUNK
    xa = _dot(jnp.concatenate([qd[:c], w[:c]], axis=0).astype(BF), s_a.astype(BF))
    vn_a = u[:c] - xa[c:]
    s_b = s_a * jnp.exp(loc["gl_a"]) + _dot_tn(kd[:c].astype(BF), vn_a.astype(BF))
    xb = _dot(jnp.concatenate([qd[c:], w[c:]], axis=0).astype(BF), s_b.astype(BF))
    vn_b = u[c:] - xb[c:]
    s_c = s_b * jnp.exp(loc["gl_b"]) + _dot_tn(kd[c:].astype(BF), vn_b.astype(BF))
    vn = jnp.concatenate([vn_a, vn_b], axis=0)
    o = jnp.concatenate([xa[:c], xb[:c]], axis=0) + _dot(loc["pmat"].astype(BF), vn.astype(BF))
    return w, vn, o, s_b, s_c


GDN_SEG = 1024
HEADS_PER_STEP = 4
HEAD_GROUPS = GDN_HEADS // HEADS_PER_STEP


def _gdn_specs(nb, seq, reverse):
    n = seq // CHUNK
    seg = _tile(seq, GDN_SEG)
    nseg = seq // seg
    sp = seg // PAIR
    w2 = HEADS_PER_STEP * LANES
    at = (lambda s: nseg - 1 - s) if reverse else (lambda s: s)
    blk = pl.BlockSpec((seg, w2), lambda b, hp, s: (b * nseg + at(s), hp))
    gg = pl.BlockSpec((seg, w2), lambda b, hp, s: (b * nseg + at(s), GG_COL // w2 + hp))
    gates = pl.BlockSpec((seg, LANES), lambda b, hp, s: (b * nseg + at(s), 0))
    grp = lambda b, hp: b * HEAD_GROUPS + hp
    rowb = pl.BlockSpec((None, HEADS_PER_STEP, sp, HALO, PAIR), lambda b, hp, s: (grp(b, hp), 0, at(s), 0, 0))
    per_pair = pl.BlockSpec((None, HEADS_PER_STEP, sp, PAIR, PAIR), lambda b, hp, s: (grp(b, hp), 0, at(s), 0, 0))
    dgates = pl.BlockSpec((None, seg, LANES), lambda b, hp, s: (grp(b, hp), at(s), 0))
    return n, seg, nseg, sp, blk, gg, gates, rowb, per_pair, dgates


def _head_column(gt, lane, index):
    return jnp.sum(jnp.where(lane == index, gt, 0.0), axis=1, keepdims=True)


def _gdn_head_inputs(qkv_refs, gt, gr_ref, rows, pi, hp, lane):
    per_head = []
    for hh in range(HEADS_PER_STEP):
        head = HEADS_PER_STEP * hp + hh
        cols = slice(hh * LANES, (hh + 1) * LANES)
        per_head.append([r[rows, cols].astype(F32) for r in qkv_refs]
                        + [_head_column(gt, lane, A_LANE + head), gr_ref[hh, pi][0:1, :],
                           _head_column(gt, lane, B_LANE + head)])
    return [jnp.stack(xs) for xs in zip(*per_head)]


def _gdn_pair_fwd(qv, kv, vv, gcv, gr, bv, s_a):
    loc = _gdn_pair_local(qv, kv, vv, gcv, gr, bv)
    tf = _inv_unit_lower(loc["amat"])
    _, _, o, _, s_c = _gdn_pair_states(loc, tf.astype(BF), s_a)
    return tf, o, s_c


def _gdn_fwd(q, k, v, proj, gates, grow, wn, nb, seq, name):
    t = q.shape[0]
    n, seg, nseg, sp, blk, gg, gates_spec, rowb, per_pair, _ = _gdn_specs(nb, seq, False)

    def body(q_ref, k_ref, v_ref, gg_ref, gt_ref, gr_ref, wn_ref, y_ref, tn_ref, sn_ref, s_ref):
        hp = pl.program_id(1)

        @pl.when(pl.program_id(2) == 0)
        def _():
            s_ref[...] = jnp.zeros_like(s_ref)

        wnv = wn_ref[...]
        lane = lax.broadcasted_iota(jnp.int32, (PAIR, LANES), 1)

        def step(pi, carry):
            rows = pl.ds(pl.multiple_of(pi * PAIR, PAIR), PAIR)
            gt = gt_ref[rows, :]
            ins = _gdn_head_inputs((q_ref, k_ref, v_ref), gt, gr_ref, rows, pi, hp, lane)
            s_a = s_ref[...]
            tf, o, s_c = jax.vmap(_gdn_pair_fwd)(*ins, s_a)
            s_ref[...] = s_c
            for hh in range(HEADS_PER_STEP):
                cols = slice(hh * LANES, (hh + 1) * LANES)
                tn_ref[hh, pi] = tf[hh]
                sn_ref[hh, pi] = s_a[hh]
                g = gg_ref[rows, cols]
                oh = o[hh]
                rstd = lax.rsqrt(jnp.mean(oh * oh, axis=-1, keepdims=True) + EPS)
                y_ref[rows, cols] = (oh * rstd * wnv * (g * _sigmoid(g))).astype(BF)
            return carry

        lax.fori_loop(0, sp, step, 0)

    saved = jax.ShapeDtypeStruct((nb * HEAD_GROUPS, HEADS_PER_STEP, n // 2, PAIR, PAIR), F32)
    return pl.pallas_call(
        body, name=name, grid=(nb, GDN_HEADS // HEADS_PER_STEP, nseg),
        in_specs=[blk, blk, blk, gg, gates_spec, rowb, _full((1, LANES))],
        out_specs=[blk, per_pair, per_pair],
        out_shape=[jax.ShapeDtypeStruct((t, GDN_WIDTH), BF), saved, saved],
        scratch_shapes=[pltpu.VMEM((HEADS_PER_STEP, GDN_HEAD_DIM, GDN_HEAD_DIM), F32)],
        compiler_params=_params(("parallel", "parallel", "arbitrary")),
    )(q, k, v, proj, gates, grow, wn)


def _gdn_pair_bwd(qv, kv, vv, gcv, gr, bv, tf, s_a, dsp, g, dyv, wnv):
    c = CHUNK
    loc = _gdn_pair_local(qv, kv, vv, gcv, gr, bv)
    tm = tf.astype(BF)
    kb, vb, kbe, e, dm = loc["kb"], loc["vb"], loc["kbe"], loc["e"], loc["dm"]
    kd, qd, pmat, amat = loc["kd"], loc["qd"], loc["pmat"], loc["amat"]
    w, vn, o, s_b, _ = _gdn_pair_states(loc, tm, s_a)
    sg = _sigmoid(g)
    silu = g * sg
    rstd = lax.rsqrt(jnp.mean(o * o, axis=-1, keepdims=True) + EPS)
    xhat = o * rstd
    dwn = jnp.sum(dyv * xhat * silu, axis=0, keepdims=True)
    dgg = dyv * xhat * wnv * (sg * (1.0 + g * (1.0 - sg)))
    dxhat = dyv * wnv * silu
    do = rstd * (dxhat - xhat * jnp.mean(dxhat * xhat, axis=-1, keepdims=True))
    dob = do.astype(BF)
    tot = lambda x: jnp.sum(jnp.sum(x, axis=1, keepdims=True), axis=0, keepdims=True)
    rsum = lambda x: jnp.sum(x, axis=1, keepdims=True)
    cat = lambda xs, ax=0: jnp.concatenate(xs, axis=ax)
    wb = w.astype(BF)
    qdb = qd.astype(BF)
    kdb = kd.astype(BF)
    vnb = vn.astype(BF)
    egl_a = jnp.exp(loc["gl_a"])
    egl_b = jnp.exp(loc["gl_b"])
    ptdo = _dot_tn(pmat.astype(BF), dob)
    dspb = dsp.astype(BF)
    dvn_b = ptdo[c:] + _dot(kdb[c:], dspb)
    dkd_b = _dot_nt(vnb[c:], dspb)
    dgl_b = egl_b * tot(s_b * dsp) + tot(dkd_b * kd[c:])
    dsm = egl_b * dsp + _dot_tn(cat([qdb[c:], -wb[c:]]), cat([dob[c:], dvn_b.astype(BF)]))
    dsmb = dsm.astype(BF)
    dvn_a = ptdo[:c] + _dot(kdb[:c], dsmb)
    dkd_a = _dot_nt(vnb[:c], dsmb)
    dgl_a = egl_a * tot(s_a * dsm) + tot(dkd_a * kd[:c])
    ds_new = egl_a * dsm + _dot_tn(cat([qdb[:c], -wb[:c]]), cat([dob[:c], dvn_a.astype(BF)]))
    ya = _dot_nt(cat([dob[:c], dvn_a.astype(BF)]), s_a.astype(BF))
    yb = _dot_nt(cat([dob[c:], dvn_b.astype(BF)]), s_b.astype(BF))
    dqd = cat([ya[:c], yb[:c]])
    dw = -cat([ya[c:], yb[c:]])
    dvn = cat([dvn_a, dvn_b])
    dkd = cat([dkd_a, dkd_b])
    dq = dqd * e
    dgc = rsum(dqd * qd) - rsum(dkd * kd)
    dk = dkd * loc["edec"]
    dpm = jnp.where(loc["incl"], _dot_nt(dob, vnb), 0.0)
    duw = cat([dvn, dw], 1).astype(BF)
    dt = _dot_nt(duw, cat([vb, kbe], 1).astype(BF))
    tt = _dot_tn(tm, duw)
    dvb, dkbe = tt[:, :LANES], tt[:, LANES:]
    tn_dims = (((0,), (0,)), ((), ()))
    nt_dims = (((1,), (1,)), ((), ()))
    da = jnp.where(loc["strict"], -_dot3(_dot3(tf, dt, tn_dims), tf, nt_dims), 0.0)
    st = cat([da * dm, dpm * dm]).astype(BF)
    z = _dot(st, kv.astype(BF))
    dkb = z[:PAIR] + dkbe * e
    dq = dq + z[PAIR:]
    dk = dk + _dot_tn(st, cat([kb, qv]).astype(BF))
    gmat = dpm * pmat + da * amat
    dgc = dgc + rsum(dkbe * kbe) + rsum(gmat)
    ridx = loc["ridx"]
    dgc = dgc + jnp.where(ridx == c - 1, dgl_a, 0.0) + jnp.where(ridx == PAIR - 1, dgl_b, 0.0)
    dgc_row = jnp.sum(gmat, axis=0, keepdims=True)
    db = rsum(dvb * vv) + rsum(dkb * kv)
    return dq, dk + dkb * bv, dvb * bv, dgg, dgc, dgc_row, db, dwn, ds_new


def _gdn_bwd(q, k, v, proj, gates, grow, wn, tinv_all, states_all, dy, nb, seq, name):
    t = q.shape[0]
    n, seg, nseg, sp, blk, gg, gates_spec, rowb, per_pair, dgates = _gdn_specs(nb, seq, True)
    dh = GDN_HEAD_DIM

    def body(q_ref, k_ref, v_ref, gg_ref, gt_ref, gr_ref, wn_ref, tn_ref, sn_ref, dy_ref,
             dq_ref, dk_ref, dv_ref, dgg_ref, dgt_ref, dwn_ref, ds_ref):
        hp = pl.program_id(1)

        @pl.when((pl.program_id(0) == 0) & (hp == 0) & (pl.program_id(2) == 0))
        def _():
            dwn_ref[...] = jnp.zeros_like(dwn_ref)

        @pl.when(pl.program_id(2) == 0)
        def _():
            ds_ref[...] = jnp.zeros_like(ds_ref)

        wnv = wn_ref[...]
        lane = lax.broadcasted_iota(jnp.int32, (PAIR, LANES), 1)

        def bwd_step(j, carry):
            pi = sp - 1 - j
            rows = pl.ds(pl.multiple_of(pi * PAIR, PAIR), PAIR)
            gt = gt_ref[rows, :]
            ins = _gdn_head_inputs((q_ref, k_ref, v_ref), gt, gr_ref, rows, pi, hp, lane)
            halves = [slice(hh * LANES, (hh + 1) * LANES) for hh in range(HEADS_PER_STEP)]
            saved = [jnp.stack([r[hh, pi] for hh in range(HEADS_PER_STEP)]) for r in (tn_ref, sn_ref)]
            g2 = jnp.stack([gg_ref[rows, cols] for cols in halves])
            dy2 = jnp.stack([dy_ref[rows, cols].astype(F32) for cols in halves])
            dq, dk, dv, dgg, dgc, dgc_row, db, dwn, ds_new = jax.vmap(
                _gdn_pair_bwd, in_axes=(0,) * 11 + (None,))(*ins, *saved, ds_ref[...], g2, dy2, wnv)
            ds_ref[...] = ds_new
            dgt = jnp.zeros((PAIR, LANES), F32)
            for hh, cols in enumerate(halves):
                head = HEADS_PER_STEP * hp + hh
                dq_ref[rows, cols] = dq[hh]
                dk_ref[rows, cols] = dk[hh]
                dv_ref[rows, cols] = dv[hh]
                dgg_ref[rows, cols] = dgg[hh].astype(BF)
                dwn_ref[...] += dwn[hh]
                row_as_col = jnp.transpose(jnp.broadcast_to(dgc_row[hh], (PAIR, LANES)))
                dgt = (dgt + jnp.where(lane == A_LANE + head, dgc[hh] - row_as_col, 0.0)
                       + jnp.where(lane == B_LANE + head, db[hh], 0.0))
            dgt_ref[rows, :] = dgt
            return carry

        lax.fori_loop(0, sp, bwd_step, 0)

    return pl.pallas_call(
        body, name=name, grid=(nb, GDN_HEADS // HEADS_PER_STEP, nseg),
        in_specs=[blk, blk, blk, gg, gates_spec, rowb, _full((1, LANES)), per_pair, per_pair, blk],
        out_specs=[blk, blk, blk, blk, dgates, _full((1, LANES))],
        out_shape=[jax.ShapeDtypeStruct((t, GDN_WIDTH), F32)] * 3 + [
            jax.ShapeDtypeStruct((t, GDN_WIDTH), BF),
            jax.ShapeDtypeStruct((nb * HEAD_GROUPS, seq, LANES), F32),
            jax.ShapeDtypeStruct((1, LANES), F32)],
        scratch_shapes=[pltpu.VMEM((HEADS_PER_STEP, dh, dh), F32)],
        compiler_params=_params(("arbitrary", "arbitrary", "arbitrary")),
    )(q, k, v, proj, gates, grow, wn, tinv_all, states_all, dy)


def _mix_to_padded(w):
    pad = jnp.zeros(w.shape[:-1] + (N_PAD - N_IN,), w.dtype)
    return jnp.concatenate([w[..., 0:1536], w[..., 1544:3080], w[..., 3088:3600], w[..., 1536:1544],
                            w[..., 3080:3088], pad], axis=-1)


def _pad_lanes(vec, start):
    return jnp.pad(vec[None, :], ((0, 0), (start, LANES - start - vec.shape[0])))


def _heads_to_rows(block, lane0, nheads, nb, seq):
    return block[:, lane0:lane0 + nheads].reshape(nb, seq, nheads).transpose(0, 2, 1).reshape(nb * nheads, seq)


def _mixer_small(p, l):
    wq_t = jnp.tile(p["fox_q_norm"][l], FOX_HEADS)[None, :]
    wk_t = jnp.tile(p["fox_k_norm"][l], FOX_HEADS)[None, :]
    bias = _pad_lanes(p["fox_f_bias"][l], 0)
    a_pad = _pad_lanes(p["gdn_a_log"][l], A_LANE)
    dt_pad = _pad_lanes(p["gdn_dt_bias"][l], A_LANE)
    wn = p["gdn_out_norm"][l][None, :]
    return wq_t, wk_t, bias, a_pad, dt_pad, wn


def _layer_fwd(x, p, l, nb, seq, rider=None, ffn1_rider=None, after_ffn1=None):
    npair = FOX_HEADS // 2
    n = seq // CHUNK
    x1, h1, *rode1 = _ffn_fwd(x, p["ffn1_norm"][l][None, :], p["ffn1_w_in"][l], p["ffn1_w_out"][l],
                              f"ffn1_fwd_{l}", ffn1_rider)
    if after_ffn1 is not None:
        after_ffn1(rode1)
    wq_t, wk_t, bias, a_pad, dt_pad, wn = _mixer_small(p, l)
    proj = _norm_matmul(x1, p["mix_norm"][l][None, :], p["w_mix"][l], f"mix_in_{l}")
    fq, fk, fv, cum = _fox_prep(proj, wq_t, wk_t, bias, seq, f"fox_prep_{l}")
    c8 = _heads_to_rows(cum, 0, FOX_HEADS, nb, seq)
    ck = c8.reshape(nb * npair, 2, 1, seq)
    o, lse, *rode = _fox_attn(fq, fk, fv, ck, nb, seq, f"fox_attn_{l}", rider)
    gq, gk, gv, gates = _gdn_prep(proj, p["gdn_conv"][l], a_pad, dt_pad, seq, f"gdn_prep_{l}")
    gc4 = _heads_to_rows(gates, A_LANE, GDN_HEADS, nb, seq)
    grow = jnp.broadcast_to(gc4.reshape(nb * HEAD_GROUPS, HEADS_PER_STEP, n // 2, 1, PAIR),
                            (nb * HEAD_GROUPS, HEADS_PER_STEP, n // 2, HALO, PAIR))
    y, tinv, states = _gdn_fwd(gq, gk, gv, proj, gates, grow, wn, nb, seq, f"gdn_fwd_{l}")
    x2 = _mix_out(x1, o, y, p["w_out"][l], f"mix_out_{l}")
    x3, h2 = _ffn_fwd(x2, p["ffn2_norm"][l][None, :], p["ffn2_w_in"][l], p["ffn2_w_out"][l], f"ffn2_fwd_{l}")
    saved = dict(x=x, h1=h1, x1=x1, proj=proj, fq=fq, fk=fk, fv=fv, ck=ck, o=o, lse=lse,
                 gq=gq, gk=gk, gv=gv, gates=gates, grow=grow, tinv=tinv, states=states, y=y, x2=x2, h2=h2)
    return x3, saved, rode


def _ffn_grads(dy, x, h, gain, win, wout, l, tag, rider=None):
    t, d = x.shape
    fs = win.shape[2]
    dx, dh, a, hn, dyh, dgain, *rode = _ffn_bwd(dy, x, h, gain, win, wout, f"{tag}_bwd_{l}", rider)
    g_in = _wgrad(hn, dh, jax.ShapeDtypeStruct((4, d, fs), BF),
                  pl.BlockSpec((None, d, fs), lambda i, j, k: (j, i, 0)), d, fs, f"{tag}_gw_in_{l}")
    g_out = _wgrad(a, dyh, jax.ShapeDtypeStruct((2 * fs, d), BF),
                   pl.BlockSpec((fs, d), lambda i, j, k: (i, j)), fs, d, f"{tag}_gw_out_{l}")
    return dx, dgain[0], g_in, g_out.reshape(4, fs // 2, d), rode


def _layer_bwd(dx3, p, l, sv, nb, seq, rider=None, before_ffn1=None):
    npair = FOX_HEADS // 2
    d = dx3.shape[1]
    wq_t, wk_t, bias, a_pad, dt_pad, wn = _mixer_small(p, l)
    g = {}
    dx2, g["ffn2_norm"], g["ffn2_w_in"], g["ffn2_w_out"], _ = _ffn_grads(
        dx3, sv["x2"], sv["h2"], p["ffn2_norm"][l][None, :], p["ffn2_w_in"][l], p["ffn2_w_out"][l], l, "ffn2")
    dyf, dyg, dxb = _mix_out_bwd(dx2, p["w_out"][l], f"mix_out_bwd_{l}")
    half = lambda a, nm: _wgrad(a, dxb, jax.ShapeDtypeStruct((FOX_WIDTH, d), BF),
                                pl.BlockSpec((FOX_WIDTH, d), lambda i, j, k: (i, j)), FOX_WIDTH, d, nm)
    g["w_out"] = jnp.concatenate([half(sv["o"], f"gw_out_fox_{l}"), half(sv["y"], f"gw_out_gdn_{l}")], axis=0)
    dqa, dqb, dk, dv, dkx, *rode = _fox_attn_bwd(sv["fq"], sv["fk"], sv["fv"], sv["o"], dyf, sv["lse"], sv["ck"],
                                                 nb, seq, f"fox_attn_bwd_{l}", rider)

    dpf, dff, dwq, dwk, dbias = _fox_prep_bwd(sv["proj"], dqa, dqb, dk, dv, dkx, wq_t, wk_t, bias, seq,
                                              f"fox_prep_bwd_{l}")
    g["fox_q_norm"] = dwq[0, :FOX_HEAD_DIM]
    g["fox_k_norm"] = dwk[0, :FOX_HEAD_DIM]
    g["fox_f_bias"] = dbias[0, :FOX_HEADS]
    dgq, dgk, dgv, dgg, dgt, dwn = _gdn_bwd(
        sv["gq"], sv["gk"], sv["gv"], sv["proj"], sv["gates"], sv["grow"], wn, sv["tinv"], sv["states"],
        dyg, nb, seq, f"gdn_bwd_{l}")
    dgates = jnp.sum(dgt.reshape(nb, HEAD_GROUPS, seq, LANES), axis=1).reshape(nb * seq, LANES)
    dpg, dgate_blk, dconv, da, ddt = _gdn_prep_bwd(sv["proj"], dgq, dgk, dgv, dgates, dff, p["gdn_conv"][l],
                                                   a_pad, dt_pad, seq, f"gdn_prep_bwd_{l}")
    g["gdn_conv"] = dconv
    g["gdn_a_log"] = da[0, A_LANE:B_LANE]
    g["gdn_dt_bias"] = ddt[0, A_LANE:B_LANE]
    g["gdn_out_norm"] = dwn[0]
    dparts = [dpf, dpg, dgg, dgate_blk]
    dx1, hnm, dgm = _norm_matmul_bwd(dx2, dparts, sv["x1"], p["mix_norm"][l][None, :], p["w_mix"][l],
                                     f"mix_in_bwd_{l}")
    g["mix_norm"] = dgm[0]
    gf, gg_, go, gt = [_wgrad(hnm, a, jax.ShapeDtypeStruct((d, a.shape[1]), F32),
                              pl.BlockSpec((d // 2, a.shape[1]), lambda i, j, k: (i, j)), d // 2, a.shape[1],
                              f"gw_mix_{l}_{i}") for i, a in enumerate(dparts)]
    g["w_in"] = jnp.concatenate([gf, gt[:, 0:FOX_HEADS], gg_, gt[:, A_LANE:B_LANE + GDN_HEADS], go], axis=1)
    ffn1_rider = before_ffn1(g) if before_ffn1 is not None else None
    dx0, g["ffn1_norm"], g["ffn1_w_in"], g["ffn1_w_out"], rode1 = _ffn_grads(
        dx1, sv["x"], sv["h1"], p["ffn1_norm"][l][None, :], p["ffn1_w_in"][l], p["ffn1_w_out"][l], l, "ffn1",
        ffn1_rider)
    return dx0, g, rode, rode1


def _local_step(x, target, p):
    nb, seq, d = x.shape
    xt = x.reshape(nb * seq, d)
    saved = []
    for l in range(DEPTH):
        xt, sv, _ = _layer_fwd(xt, p, l, nb, seq)
        saved.append(sv)
    loss, dx = _loss_grad(xt, target.reshape(nb * seq, d), "loss")
    grads = [None] * DEPTH
    for l in reversed(range(DEPTH)):
        dx, grads[l], _, _ = _layer_bwd(dx, p, l, saved[l], nb, seq)
    return loss, dx.reshape(nb, seq, d), grads


N_CHIPS = 4


def _mesh_pos():
    return lax.axis_index("x"), lax.axis_index("y"), lax.axis_index("c")


def _other_chips(x, y):
    return [(1 - x, y), (x, 1 - y), (1 - x, 1 - y)]


def _remote(src, dst, send_sem, recv_sem, to):
    return pltpu.make_async_remote_copy(src_ref=src, dst_ref=dst, send_sem=send_sem, recv_sem=recv_sem,
                                        device_id=to, device_id_type=MESH)


def _hbm_call(body, name, ins, out_shape, scratch):
    return pl.pallas_call(
        body, name=name, out_shape=out_shape, in_specs=[HBM] * len(ins),
        out_specs=jax.tree.map(lambda _: HBM, out_shape), scratch_shapes=scratch,
        compiler_params=pltpu.CompilerParams(has_side_effects=True),
    )(*ins)


def _gather_phases(n, layer):
    def copies(ins, outs, sems):
        send1, recv1, send2, recv2 = sems
        x, y, c = _mesh_pos()
        out, back, fwd = [], [], []
        for i in range(n):
            for j, (px, py) in enumerate(_other_chips(x, y)):
                k = 3 * i + j
                blk = outs[i].at[2 * px + py]
                out.append(_remote(ins[i], outs[i].at[2 * x + y], send1.at[k], recv1.at[k], (px, py, c)))
                back.append(_remote(blk, blk, send1.at[k], recv1.at[k], (px, py, c)))
                fwd.append(_remote(blk, blk, send2.at[k], recv2.at[k], (x, y, 1 - c)))
        return c, out, back, fwd

    def first(ins, outs, sems):
        c, out, _, _ = copies(ins, outs, sems)

        @pl.when(c == layer)
        def _():
            for cp in out:
                cp.start()

    def middle(ins, outs, sems):
        c, _, back, fwd = copies(ins, outs, sems)

        @pl.when(c == layer)
        def _():
            for arrived, onward in zip(back, fwd):
                arrived.wait_recv()
                onward.start()

    def last(ins, outs, sems):
        c, out, _, fwd = copies(ins, outs, sems)

        @pl.when(c == layer)
        def _():
            for cp in out + fwd:
                cp.wait_send()

        @pl.when(c != layer)
        def _():
            for cp in fwd:
                cp.wait_recv()

    return first, middle, last


def _scatter_phases(n, layer):
    def copies(ins, outs, sems):
        send, recv = sems
        x, y, c = _mesh_pos()
        return c, [_remote(ins[i].at[2 * px + py], outs[i].at[j], send.at[3 * i + j], recv.at[3 * i + j], (px, py, c))
                   for i in range(n) for j, (px, py) in enumerate(_other_chips(x, y))]

    def first(ins, outs, sems):
        c, cps = copies(ins, outs, sems)

        @pl.when(c == layer)
        def _():
            for cp in cps:
                cp.start()

    def middle(ins, outs, sems):
        pass

    def last(ins, outs, sems):
        c, cps = copies(ins, outs, sems)

        @pl.when(c == layer)
        def _():
            for cp in cps:
                cp.wait()

    return first, middle, last


def _exchange(blocks, out_shapes, n_sems, phases, name, rider):
    sems = [pltpu.SemaphoreType.DMA((3 * len(blocks),))] * n_sems
    if rider:
        return _Rider(blocks, out_shapes, sems, phases)
    n = len(blocks)

    def body(*refs):
        for phase in phases:
            phase(refs[:n], refs[n:2 * n], refs[2 * n:])

    return list(_hbm_call(body, name, blocks, out_shapes, sems))


def _gather_layer(blocks, layer, name=None, rider=False):
    outs = [jax.ShapeDtypeStruct((N_CHIPS,) + b.shape, b.dtype) for b in blocks]
    return _exchange(blocks, outs, 4, _gather_phases(len(blocks), layer), name, rider)


def _scatter_layer(sums, layer, name=None, rider=False):
    outs = [jax.ShapeDtypeStruct((3,) + s.shape[1:], s.dtype) for s in sums]
    return _exchange(sums, outs, 2, _scatter_phases(len(sums), layer), name, rider)


def _to_sibling(gs, layer, name):
    n = len(gs)

    def body(*refs):
        ins, outs = refs[:n], refs[n:2 * n]
        send, recv = refs[2 * n:]
        x, y, c = _mesh_pos()
        cps = [_remote(ins[i], outs[i], send.at[i], recv.at[i], (x, y, 1 - c)) for i in range(n)]

        @pl.when(c != layer)
        def _():
            for cp in cps:
                cp.start()
            for cp in cps:
                cp.wait_send()

        @pl.when(c == layer)
        def _():
            for cp in cps:
                cp.wait_recv()

    sem = pltpu.SemaphoreType.DMA((n,))
    return list(_hbm_call(body, name, gs, [jax.ShapeDtypeStruct(g.shape, g.dtype) for g in gs], [sem, sem]))


def _sibling_swap(rs, name):
    n = len(rs)

    def body(*refs):
        ins, outs = refs[:n], refs[n:2 * n]
        send, recv = refs[2 * n:]
        x, y, c = _mesh_pos()
        cps = [_remote(ins[i], outs[i], send.at[i], recv.at[i], (x, y, 1 - c)) for i in range(n)]
        for cp in cps:
            cp.start()
        for cp in cps:
            cp.wait()

    sem = pltpu.SemaphoreType.DMA((n,))
    return _hbm_call(body, name, rs, [jax.ShapeDtypeStruct(r.shape, r.dtype) for r in rs], [sem, sem])


def _small_all_reduce(vec, name):
    r = vec.shape[0]
    ndev = 8

    def body(v_ref, o_ref, buf, send, recv):
        x, y, c = _mesh_pos()
        me = 4 * x + 2 * y + c
        buf[me] = v_ref[...]
        cps = []
        for rel in range(1, ndev):
            px = 1 - x if rel & 4 else x
            py = 1 - y if rel & 2 else y
            pc = 1 - c if rel & 1 else c
            cps.append((_remote(v_ref, buf.at[me], send.at[rel - 1], recv.at[rel - 1], (px, py, pc)),
                        4 * px + 2 * py + pc))
        for cp, _ in cps:
            cp.start()
        for k, (cp, peer) in enumerate(cps):
            slot = buf.at[peer]
            _remote(slot, slot, send.at[k], recv.at[k], (x, y, c)).wait_recv()
        for cp, _ in cps:
            cp.wait_send()
        acc = buf[0]
        for k in range(1, ndev):
            acc = acc + buf[k]
        o_ref[...] = acc

    vm = pl.BlockSpec(memory_space=pltpu.VMEM)
    return pl.pallas_call(
        body, name=name, out_shape=jax.ShapeDtypeStruct(vec.shape, F32), in_specs=[vm], out_specs=vm,
        scratch_shapes=[pltpu.VMEM((ndev, r, LANES), F32), pltpu.SemaphoreType.DMA((ndev - 1,)),
                        pltpu.SemaphoreType.DMA((ndev - 1,))],
        compiler_params=pltpu.CompilerParams(has_side_effects=True),
    )(vec)


def _row_tile(rows, cap=512):
    for t in range(min(rows, cap), 0, -1):
        if rows % t == 0 and (t % 16 == 0 or t == rows):
            return t
    raise ValueError(rows)


def _add_pairs(a, b, name):
    k, r, c = a.shape
    tr = _row_tile(r)

    def body(a_ref, b_ref, o_ref):
        o_ref[...] = (a_ref[...].astype(F32) + b_ref[...].astype(F32)).astype(o_ref.dtype)

    spec = pl.BlockSpec((None, tr, c), lambda i, j: (i, j, 0))
    return pl.pallas_call(body, name=name, grid=(k, r // tr), in_specs=[spec, spec], out_specs=spec,
                          out_shape=jax.ShapeDtypeStruct(a.shape, a.dtype),
                          compiler_params=_params(("parallel", "parallel")))(a, b)


def _final_sum(own, sib, others, name):
    r, c = own.shape
    tr = _row_tile(r)

    def body(a_ref, b_ref, o_ref_in, out_ref):
        acc = a_ref[...].astype(F32) + b_ref[...].astype(F32)
        for k in range(3):
            acc = acc + o_ref_in[k].astype(F32)
        out_ref[...] = acc

    spec = pl.BlockSpec((tr, c), lambda i: (i, 0))
    return pl.pallas_call(body, name=name, grid=(r // tr,),
                          in_specs=[spec, spec, pl.BlockSpec((3, tr, c), lambda i: (0, i, 0))], out_specs=spec,
                          out_shape=jax.ShapeDtypeStruct((r, c), F32),
                          compiler_params=_params(("parallel",)))(own, sib, others)


def _adamw(g, w, m, v, name):
    r, c = g.shape
    tr = _row_tile(r, 256)

    def body(g_ref, w_ref, m_ref, v_ref, d_ref, mo_ref, vo_ref):
        gv = g_ref[...]
        mn = ADAM_B1 * m_ref[...] + (1.0 - ADAM_B1) * gv
        vn = ADAM_B2 * v_ref[...] + (1.0 - ADAM_B2) * (gv * gv)
        m_hat = mn / (1.0 - ADAM_B1 ** ADAM_STEP)
        v_hat = vn / (1.0 - ADAM_B2 ** ADAM_STEP)
        d_ref[...] = -ADAM_LR * (m_hat / (jnp.sqrt(v_hat) + ADAM_EPS) + ADAM_WD * w_ref[...])
        mo_ref[...] = mn
        vo_ref[...] = vn

    spec = pl.BlockSpec((tr, c), lambda i: (i, 0))
    shp = jax.ShapeDtypeStruct((r, c), F32)
    return pl.pallas_call(body, name=name, grid=(r // tr,), in_specs=[spec] * 4, out_specs=[spec] * 3,
                          out_shape=[shp] * 3, compiler_params=_params(("parallel",)))(g, w, m, v)


def _pack(arrays):
    flat = jnp.concatenate([a.reshape(-1).astype(F32) for a in arrays])
    pad = (-flat.shape[0]) % (8 * LANES)
    return jnp.concatenate([flat, jnp.zeros((pad,), F32)]).reshape(-1, LANES)


def _unpack(packed, shapes):
    flat = packed.reshape(-1)
    out, off = [], 0
    for s in shapes:
        size = 1
        for dim in s:
            size *= dim
        out.append(flat[off:off + size].reshape(s))
        off += size
    return out


BIG = ("ffn1_w_in", "ffn1_w_out", "w_in", "w_out", "ffn2_w_in", "ffn2_w_out")
SMALL = ("ffn1_norm", "mix_norm", "fox_q_norm", "fox_k_norm", "fox_f_bias", "gdn_a_log", "gdn_dt_bias",
         "gdn_out_norm", "ffn2_norm", "gdn_conv")
WEIGHTS = ("ffn1_norm", "ffn1_w_in", "ffn1_w_out", "mix_norm", "w_in", "fox_q_norm", "fox_k_norm", "fox_f_bias",
           "gdn_conv", "gdn_a_log", "gdn_dt_bias", "gdn_out_norm", "w_out", "ffn2_norm", "ffn2_w_in", "ffn2_w_out")


def _step(x, target, w, m, v):
    xi, yi, ci = _mesh_pos()
    me = 2 * xi + yi
    depth = DEPTH
    d = x.shape[-1]

    nb, seq, _ = x.shape
    assert depth == 2

    p = {k: w[k] for k in SMALL if k != "gdn_conv"}
    for k in ("ffn1_w_in", "ffn1_w_out", "ffn2_w_in", "ffn2_w_out", "w_mix", "w_out", "gdn_conv"):
        p[k] = [None] * depth

    first, rest = BIG[:2], BIG[2:] + ("gdn_conv",)

    def shards(l, names):
        return [w[k][l] if k == "gdn_conv" else w[k][l].astype(BF) for k in names]

    def place(l, names, gathered):
        blocks = dict(zip(names, [lax.dynamic_update_index_in_dim(g, s, me, 0)
                                  for g, s in zip(gathered, shards(l, names))]))
        for k in ("ffn1_w_in", "ffn1_w_out", "ffn2_w_in", "ffn2_w_out"):
            if k in blocks:
                p[k][l] = blocks[k]
        if "w_in" in blocks:
            p["w_mix"][l] = _mix_to_padded(blocks["w_in"].transpose(1, 0, 2).reshape(d, N_IN))
            p["w_out"][l] = blocks["w_out"].reshape(2 * FOX_WIDTH, d)
            p["gdn_conv"][l] = blocks["gdn_conv"].transpose(1, 0, 2).reshape(CONV_WIDTH, -1)

    place(0, first, _gather_layer(shards(0, first), 0, "gather_first_ffn0"))
    xt = x.reshape(nb * seq, d)
    xt, saved0, gathered1 = _layer_fwd(
        xt, p, 0, nb, seq, _gather_layer(shards(1, first + rest), 1, rider=True),
        _gather_layer(shards(0, rest), 0, rider=True), lambda got: place(0, rest, got))
    place(1, first + rest, gathered1)
    xt, saved1, _ = _layer_fwd(xt, p, 1, nb, seq)
    loss, dx = _loss_grad(xt, target.reshape(nb * seq, d), "loss")

    def transport(g, names):
        out = []
        for k in names:
            if k == "w_in":
                out.append(g["w_in"].reshape(d, N_CHIPS, N_IN // N_CHIPS).transpose(1, 0, 2).astype(BF))
            elif k == "w_out":
                out.append(g["w_out"].reshape(N_CHIPS, -1, d))
            else:
                out.append(g[k])
        return out

    def chip_sums(g, l, names, tag):
        own = transport(g, names)
        sib = _to_sibling(own, l, f"grad{l}{tag}_to_sibling")
        return own, sib, [_add_pairs(a, b, f"grad{l}{tag}_chip_sum_{k}") for a, b, k in zip(own, sib, names)]

    dx, grads1, _, _ = _layer_bwd(dx, p, 1, saved1, nb, seq)
    own1, sib1, sums1 = chip_sums(grads1, 1, BIG, "")
    before = {}

    def before_ffn1(g):
        before["own"], before["sib"], sums = chip_sums(g, 0, BIG[2:], "_rest")
        return _scatter_layer(sums, 0, rider=True)

    dx, grads0, chips1, chips0_rest = _layer_bwd(dx, p, 0, saved0, nb, seq,
                                                 _scatter_layer(sums1, 1, rider=True), before_ffn1)
    own0, sib0, sums0 = chip_sums(grads0, 0, first, "_first")
    chips0 = _scatter_layer(sums0, 0, "grad0_first_to_chips") + chips0_rest
    own0, sib0 = own0 + before["own"], sib0 + before["sib"]
    grads = [grads0, grads1]
    dx = dx.reshape(nb, seq, d)

    mine = lambda a0, a1: jnp.where(ci == 0, a0, a1)
    at_me = lambda a: lax.dynamic_index_in_dim(a, me, 0, keepdims=False)
    reduced = [_final_sum(mine(at_me(own0[i]), at_me(own1[i])), mine(at_me(sib0[i]), at_me(sib1[i])),
                          mine(chips0[i], chips1[i]), f"grad_final_sum_{k}") for i, k in enumerate(BIG)]
    from_sib_final = _sibling_swap(reduced, "grad_swap_layers")
    full = {k: jnp.stack([jnp.where(ci == 0, a, b), jnp.where(ci == 0, b, a)])
            for k, a, b in zip(BIG, reduced, from_sib_final)}

    out_g, out_d, out_m, out_v = {}, {}, {}, {}
    for k in BIG:
        shp = w[k].shape
        two_d = lambda a: a.reshape(shp[0] * shp[1], shp[2])
        dl, mn, vn = _adamw(two_d(full[k]), two_d(w[k]), two_d(m[k]), two_d(v[k]), f"adamw_{k}")
        out_g[k], out_d[k], out_m[k], out_v[k] = full[k], dl.reshape(shp), mn.reshape(shp), vn.reshape(shp)

    small_local = [jnp.stack([grads[l][k] for l in range(depth)]) for k in SMALL]
    summed = _unpack(_small_all_reduce(_pack(small_local), "small_all_reduce"), [a.shape for a in small_local])
    sg = dict(zip(SMALL, summed))
    cs = w["gdn_conv"].shape[-1]
    sg["gdn_conv"] = lax.dynamic_slice_in_dim(sg["gdn_conv"], me * cs, cs, axis=2)
    shapes = [w[k].shape for k in SMALL]
    packs = [_pack([src[k] for k in SMALL]) for src in (sg, w, m, v)]
    dl, mn, vn = _adamw(*packs, "adamw_small")
    for k, a, b, c2 in zip(SMALL, _unpack(dl, shapes), _unpack(mn, shapes), _unpack(vn, shapes)):
        out_g[k], out_d[k], out_m[k], out_v[k] = sg[k], a, b, c2

    total = lax.psum(loss[0, 0], ("x", "y", "c"))
    return (total, dx, *[out_g[k] for k in WEIGHTS], *[out_d[k] for k in WEIGHTS],
            *[out_m[k] for k in WEIGHTS], *[out_v[k] for k in WEIGHTS])


def kernel(x, ffn1_norm, ffn1_w_in, ffn1_w_out, mix_norm, w_in, fox_q_norm, fox_k_norm, fox_f_bias, gdn_conv, gdn_a_log, gdn_dt_bias, gdn_out_norm, w_out, ffn2_norm, ffn2_w_in, ffn2_w_out, loss_target, m_ffn1_norm, m_ffn1_w_in, m_ffn1_w_out, m_mix_norm, m_w_in, m_fox_q_norm, m_fox_k_norm, m_fox_f_bias, m_gdn_conv, m_gdn_a_log, m_gdn_dt_bias, m_gdn_out_norm, m_w_out, m_ffn2_norm, m_ffn2_w_in, m_ffn2_w_out, v_ffn1_norm, v_ffn1_w_in, v_ffn1_w_out, v_mix_norm, v_w_in, v_fox_q_norm, v_fox_k_norm, v_fox_f_bias, v_gdn_conv, v_gdn_a_log, v_gdn_dt_bias, v_gdn_out_norm, v_w_out, v_ffn2_norm, v_ffn2_w_in, v_ffn2_w_out):
    w = dict(ffn1_norm=ffn1_norm, ffn1_w_in=ffn1_w_in, ffn1_w_out=ffn1_w_out, mix_norm=mix_norm, w_in=w_in,
             fox_q_norm=fox_q_norm, fox_k_norm=fox_k_norm, fox_f_bias=fox_f_bias, gdn_conv=gdn_conv,
             gdn_a_log=gdn_a_log, gdn_dt_bias=gdn_dt_bias, gdn_out_norm=gdn_out_norm, w_out=w_out,
             ffn2_norm=ffn2_norm, ffn2_w_in=ffn2_w_in, ffn2_w_out=ffn2_w_out)
    m = dict(ffn1_norm=m_ffn1_norm, ffn1_w_in=m_ffn1_w_in, ffn1_w_out=m_ffn1_w_out, mix_norm=m_mix_norm, w_in=m_w_in,
             fox_q_norm=m_fox_q_norm, fox_k_norm=m_fox_k_norm, fox_f_bias=m_fox_f_bias, gdn_conv=m_gdn_conv,
             gdn_a_log=m_gdn_a_log, gdn_dt_bias=m_gdn_dt_bias, gdn_out_norm=m_gdn_out_norm, w_out=m_w_out,
             ffn2_norm=m_ffn2_norm, ffn2_w_in=m_ffn2_w_in, ffn2_w_out=m_ffn2_w_out)
    v = dict(ffn1_norm=v_ffn1_norm, ffn1_w_in=v_ffn1_w_in, ffn1_w_out=v_ffn1_w_out, mix_norm=v_mix_norm, w_in=v_w_in,
             fox_q_norm=v_fox_q_norm, fox_k_norm=v_fox_k_norm, fox_f_bias=v_fox_f_bias, gdn_conv=v_gdn_conv,
             gdn_a_log=v_gdn_a_log, gdn_dt_bias=v_gdn_dt_bias, gdn_out_norm=v_gdn_out_norm, w_out=v_w_out,
             ffn2_norm=v_ffn2_norm, ffn2_w_in=v_ffn2_w_in, ffn2_w_out=v_ffn2_w_out)
    return _step(x, loss_target, w, m, v)
```

```python
import jax
import jax.numpy as jnp
from jax import lax
from jax.experimental import pallas as pl
from jax.experimental.pallas import tpu as pltpu

F32 = jnp.float32
BF = jnp.bfloat16
HI = lax.Precision.HIGHEST
MESH = pl.DeviceIdType.MESH

DEPTH = 2
FOX_HEADS = 8
FOX_HEAD_DIM = 64
FOX_WIDTH = 512
GDN_HEADS = 4
GDN_HEAD_DIM = 128
GDN_WIDTH = 512
CONV_WIDTH = 4
CHUNK = 64
EPS = 1e-6
N_IN = 3600
N_PAD = 3712
GATE_COL = 3584
LANES = 128
NEG = -1e30

ADAM_LR = 0.001
ADAM_B1 = 0.9
ADAM_B2 = 0.999
ADAM_EPS = 1e-08
ADAM_WD = 0.01
ADAM_STEP = 10

VMEM_LIMIT = 56 * 1024 * 1024


def _params(sem=None, **kw):
    return pltpu.CompilerParams(dimension_semantics=sem, vmem_limit_bytes=VMEM_LIMIT, **kw)


def _dot(a, b, precision=None):
    return jnp.dot(a, b, preferred_element_type=F32, precision=precision)


def _dot_nt(a, b, precision=None):
    return lax.dot_general(a, b, (((1,), (1,)), ((), ())), preferred_element_type=F32, precision=precision)


def _dot_tn(a, b, precision=None):
    return lax.dot_general(a, b, (((0,), (0,)), ((), ())), preferred_element_type=F32, precision=precision)


def _sigmoid(x):
    return 0.5 * jnp.tanh(0.5 * x) + 0.5


def _softplus(x):
    return jnp.maximum(x, 0.0) + jnp.log(1.0 + jnp.exp(-jnp.abs(x)))


def _log_sigmoid(x):
    return jnp.minimum(x, 0.0) - jnp.log(1.0 + jnp.exp(-jnp.abs(x)))


def _tile(n, t):
    t = min(n, t)
    assert n % t == 0, (n, t)
    return t


def _rms_fwd(x, gain):
    rstd = lax.rsqrt(jnp.mean(x * x, axis=-1, keepdims=True) + EPS)
    xhat = x * rstd
    return xhat * gain, xhat, rstd


def _rms_bwd(dy, xhat, rstd, gain):
    dxhat = dy * gain
    dx = rstd * (dxhat - xhat * jnp.mean(dxhat * xhat, axis=-1, keepdims=True))
    return dx, dy * xhat


def _full(shape):
    nd = len(shape)
    return pl.BlockSpec(shape, lambda *_: (0,) * nd)


HBM = pl.BlockSpec(memory_space=pltpu.HBM)


def _load_ffn_weights(win_hbm, wout_hbm, win_v, wout_v, sem):
    fr = wout_hbm.shape[1]
    copies = [pltpu.make_async_copy(win_hbm.at[s], win_v.at[s], sem.at[s]) for s in range(4)]
    copies += [pltpu.make_async_copy(wout_hbm.at[s], wout_v.at[pl.ds(s * fr, fr)], sem.at[4 + s])
               for s in range(4)]
    for c in copies:
        c.start()
    for c in copies:
        c.wait()


def _ffn_fwd(x, gain, win_g, wout_g, name, rider=None):
    t, d = x.shape
    _, _, fs = win_g.shape
    fr = wout_g.shape[1]
    tm = _tile(t, 512)
    r_in, r_out, r_sem = _rider_parts(rider)
    steps = t // tm

    def body(x_ref, g_ref, win_hbm, wout_hbm, *rest):
        rin, (xo_ref, h_ref) = rest[:len(r_in)], rest[len(r_in):len(r_in) + 2]
        rout = rest[len(r_in) + 2:len(r_in) + 2 + len(r_out)]
        win_v, wout_v, sem = rest[len(r_in) + 2 + len(r_out):len(r_in) + 5 + len(r_out)]
        riding = (rin, rout, rest[len(r_in) + 5 + len(r_out):])
        step = pl.program_id(0)
        _ride(rider, 0, step == 0, riding)
        _ride(rider, 1, step == steps // 2, riding)

        @pl.when(step == 0)
        def _():
            _load_ffn_weights(win_hbm, wout_hbm, win_v, wout_v, sem)

        xv = x_ref[...]
        hn, _, _ = _rms_fwd(xv, g_ref[...])
        hn = hn.astype(BF)
        acc = jnp.zeros((tm, d), F32)
        for s in range(2):
            g = _dot(hn, win_v[s])
            u = _dot(hn, win_v[s + 2])
            h_ref[:, s * fs:(s + 1) * fs] = g.astype(BF)
            h_ref[:, (s + 2) * fs:(s + 3) * fs] = u.astype(BF)
            a = (g * _sigmoid(g) * u).astype(BF)
            acc = acc + _dot(a, wout_v[s * fs:(s + 1) * fs, :])
        xo_ref[...] = xv + 0.5 * acc
        _ride(rider, 2, step == steps - 1, riding)

    return pl.pallas_call(
        body, name=name, grid=(steps,),
        in_specs=[pl.BlockSpec((tm, d), lambda i: (i, 0)), _full((1, d)), HBM, HBM] + [HBM] * len(r_in),
        out_specs=[pl.BlockSpec((tm, d), lambda i: (i, 0)), pl.BlockSpec((tm, 4 * fs), lambda i: (i, 0))]
        + [HBM] * len(r_out),
        out_shape=[jax.ShapeDtypeStruct((t, d), F32), jax.ShapeDtypeStruct((t, 4 * fs), BF)] + r_out,
        scratch_shapes=[pltpu.VMEM((4, d, fs), BF), pltpu.VMEM((4 * fr, d), BF), pltpu.SemaphoreType.DMA((8,))]
        + r_sem,
        compiler_params=_params(("arbitrary",), has_side_effects=rider is not None),
    )(x, gain, win_g, wout_g, *r_in)


def _ffn_bwd(dy, x, h, gain, win_g, wout_g, name, rider=None):
    t, d = x.shape
    _, _, fs = win_g.shape
    fr = wout_g.shape[1]
    tm = _tile(t, 256)
    r_in, r_out, r_sem = _rider_parts(rider)
    steps = t // tm

    def body(dy_ref, x_ref, h_ref, g_ref, win_hbm, wout_hbm, *rest):
        rin, (dx_ref, dh_ref, a_ref, hn_ref, dyh_ref, dg_ref) = rest[:len(r_in)], rest[len(r_in):len(r_in) + 6]
        rout = rest[len(r_in) + 6:len(r_in) + 6 + len(r_out)]
        win_v, wout_v, sem = rest[len(r_in) + 6 + len(r_out):len(r_in) + 9 + len(r_out)]
        riding = (rin, rout, rest[len(r_in) + 9 + len(r_out):])
        step = pl.program_id(0)
        _ride(rider, 0, step == 0, riding)
        _ride(rider, 1, step == steps // 2, riding)

        @pl.when(step == 0)
        def _():
            _load_ffn_weights(win_hbm, wout_hbm, win_v, wout_v, sem)
            dg_ref[...] = jnp.zeros_like(dg_ref)

        dyv = dy_ref[...]
        dyh = (0.5 * dyv).astype(BF)
        dyh_ref[...] = dyh
        dhn = jnp.zeros((tm, d), F32)
        for s in range(2):
            da = _dot_nt(dyh, wout_v[s * fs:(s + 1) * fs, :])
            g = h_ref[:, s * fs:(s + 1) * fs].astype(F32)
            u = h_ref[:, (s + 2) * fs:(s + 3) * fs].astype(F32)
            sg = _sigmoid(g)
            si = g * sg
            a_ref[:, s * fs:(s + 1) * fs] = (si * u).astype(BF)
            dgate = (da * u * (sg * (1.0 + g * (1.0 - sg)))).astype(BF)
            dup = (da * si).astype(BF)
            dh_ref[:, s * fs:(s + 1) * fs] = dgate
            dh_ref[:, (s + 2) * fs:(s + 3) * fs] = dup
            dhn = dhn + _dot_nt(dgate, win_v[s]) + _dot_nt(dup, win_v[s + 2])
        xv = x_ref[...]
        gain_v = g_ref[...]
        hn, xhat, rstd = _rms_fwd(xv, gain_v)
        hn_ref[...] = hn.astype(BF)
        dx, dgr = _rms_bwd(dhn, xhat, rstd, gain_v)
        dx_ref[...] = dyv + dx
        dg_ref[...] += jnp.sum(dgr, axis=0, keepdims=True)
        _ride(rider, 2, step == steps - 1, riding)

    row = lambda w: pl.BlockSpec((tm, w), lambda i: (i, 0))
    return pl.pallas_call(
        body, name=name, grid=(steps,),
        in_specs=[row(d), row(d), row(4 * fs), _full((1, d)), HBM, HBM] + [HBM] * len(r_in),
        out_specs=[row(d), row(4 * fs), row(2 * fs), row(d), row(d), _full((1, d))] + [HBM] * len(r_out),
        out_shape=[jax.ShapeDtypeStruct((t, d), F32), jax.ShapeDtypeStruct((t, 4 * fs), BF),
                   jax.ShapeDtypeStruct((t, 2 * fs), BF), jax.ShapeDtypeStruct((t, d), BF),
                   jax.ShapeDtypeStruct((t, d), BF), jax.ShapeDtypeStruct((1, d), F32)] + r_out,
        scratch_shapes=[pltpu.VMEM((4, d, fs), BF), pltpu.VMEM((4 * fr, d), BF), pltpu.SemaphoreType.DMA((8,))]
        + r_sem,
        compiler_params=_params(("arbitrary",), has_side_effects=rider is not None),
    )(dy, x, h, gain, win_g, wout_g, *r_in)


def _wgrad(a, b, out_shape, out_spec, tm, tn, name, tk=512):
    t, m = a.shape
    _, n = b.shape
    tk = _tile(t, tk)
    nk = t // tk

    def body(a_ref, b_ref, o_ref, acc):
        k = pl.program_id(2)

        @pl.when(k == 0)
        def _():
            acc[...] = jnp.zeros_like(acc)

        acc[...] += _dot_tn(a_ref[...], b_ref[...])

        @pl.when(k == nk - 1)
        def _():
            o_ref[...] = acc[...].astype(o_ref.dtype)

    return pl.pallas_call(
        body, name=name, grid=(m // tm, n // tn, nk),
        in_specs=[pl.BlockSpec((tk, tm), lambda i, j, k: (k, i)), pl.BlockSpec((tk, tn), lambda i, j, k: (k, j))],
        out_specs=out_spec, out_shape=out_shape,
        scratch_shapes=[pltpu.VMEM((tm, tn), F32)],
        compiler_params=_params(("parallel", "parallel", "arbitrary")),
    )(a, b)


def _norm_matmul(x, gain, w, name):
    t, d = x.shape
    n = w.shape[1]
    tm = _tile(t, 256)

    def body(x_ref, g_ref, w_ref, o_ref):
        hn, _, _ = _rms_fwd(x_ref[...], g_ref[...])
        o_ref[...] = _dot(hn.astype(BF), w_ref[...])

    return pl.pallas_call(
        body, name=name, grid=(t // tm,),
        in_specs=[pl.BlockSpec((tm, d), lambda i: (i, 0)), _full((1, d)), _full((d, n))],
        out_specs=pl.BlockSpec((tm, n), lambda i: (i, 0)),
        out_shape=jax.ShapeDtypeStruct((t, n), F32),
        compiler_params=_params(("parallel",)),
    )(x, gain, w)


def _norm_matmul_bwd(dres, dparts, x, gain, w, name):
    t, d = x.shape
    n = w.shape[1]
    tm = _tile(t, 256)
    widths = [a.shape[1] for a in dparts]
    assert sum(widths) == n
    k = len(dparts)

    def body(dr_ref, *rest):
        dp_refs, (x_ref, g_ref, w_ref, dx_ref, hn_ref, dg_ref) = rest[:k], rest[k:]

        @pl.when(pl.program_id(0) == 0)
        def _():
            dg_ref[...] = jnp.zeros_like(dg_ref)

        dhn, off = jnp.zeros((tm, d), F32), 0
        for dp_ref, wd in zip(dp_refs, widths):
            dhn = dhn + _dot_nt(dp_ref[...], w_ref[:, off:off + wd])
            off += wd
        gain_v = g_ref[...]
        hn, xhat, rstd = _rms_fwd(x_ref[...], gain_v)
        hn_ref[...] = hn.astype(BF)
        dx, dgr = _rms_bwd(dhn, xhat, rstd, gain_v)
        dx_ref[...] = dr_ref[...] + dx
        dg_ref[...] += jnp.sum(dgr, axis=0, keepdims=True)

    row = lambda wd: pl.BlockSpec((tm, wd), lambda i: (i, 0))
    return pl.pallas_call(
        body, name=name, grid=(t // tm,),
        in_specs=[row(d)] + [row(wd) for wd in widths] + [row(d), _full((1, d)), _full((d, n))],
        out_specs=[row(d), row(d), _full((1, d))],
        out_shape=[jax.ShapeDtypeStruct((t, d), F32), jax.ShapeDtypeStruct((t, d), BF),
                   jax.ShapeDtypeStruct((1, d), F32)],
        compiler_params=_params(("arbitrary",)),
    )(dres, *dparts, x, gain, w)


def _mix_out(x, yf, yg, w, name):
    t, d = x.shape
    kf = yf.shape[1]
    tm = _tile(t, 512)

    def body(x_ref, yf_ref, yg_ref, w_ref, o_ref):
        o_ref[...] = x_ref[...] + _dot(yf_ref[...], w_ref[0:kf, :]) + _dot(yg_ref[...], w_ref[kf:2 * kf, :])

    row = lambda wd: pl.BlockSpec((tm, wd), lambda i: (i, 0))
    return pl.pallas_call(
        body, name=name, grid=(t // tm,),
        in_specs=[row(d), row(kf), row(kf), _full((2 * kf, d))],
        out_specs=row(d), out_shape=jax.ShapeDtypeStruct((t, d), F32),
        compiler_params=_params(("parallel",)),
    )(x, yf, yg, w)


def _mix_out_bwd(dx, w, name):
    t, d = dx.shape
    kf = w.shape[0] // 2
    tm = _tile(t, 512)

    def body(dx_ref, w_ref, df_ref, dg_ref, dxb_ref):
        dxb = dx_ref[...].astype(BF)
        dxb_ref[...] = dxb
        df_ref[...] = _dot_nt(dxb, w_ref[0:kf, :]).astype(BF)
        dg_ref[...] = _dot_nt(dxb, w_ref[kf:2 * kf, :]).astype(BF)

    row = lambda wd: pl.BlockSpec((tm, wd), lambda i: (i, 0))
    return pl.pallas_call(
        body, name=name, grid=(t // tm,),
        in_specs=[row(d), _full((2 * kf, d))],
        out_specs=[row(kf), row(kf), row(d)],
        out_shape=[jax.ShapeDtypeStruct((t, kf), BF), jax.ShapeDtypeStruct((t, kf), BF),
                   jax.ShapeDtypeStruct((t, d), BF)],
        compiler_params=_params(("parallel",)),
    )(dx, w)


def _loss_grad(y, target, name):
    t, d = y.shape
    tm = _tile(t, 512)

    def body(y_ref, t_ref, l_ref, dy_ref):
        @pl.when(pl.program_id(0) == 0)
        def _():
            l_ref[...] = jnp.zeros_like(l_ref)

        diff = y_ref[...] - t_ref[...]
        dy_ref[...] = diff * (1.0 / d)
        part = jnp.sum(jnp.sum(diff * diff, axis=1, keepdims=True), axis=0, keepdims=True)
        l_ref[...] += part * (0.5 / d)

    row = pl.BlockSpec((tm, d), lambda i: (i, 0))
    return pl.pallas_call(
        body, name=name, grid=(t // tm,),
        in_specs=[row, row], out_specs=[_full((1, 1)), row],
        out_shape=[jax.ShapeDtypeStruct((1, 1), F32), jax.ShapeDtypeStruct((t, d), F32)],
        compiler_params=_params(("arbitrary",)),
    )(y, target)


def _head_sum_matrix(width, head):
    r = lax.broadcasted_iota(jnp.int32, (width, width), 0) // head
    c = lax.broadcasted_iota(jnp.int32, (width, width), 1) // head
    return (r == c).astype(BF)


def _head_mean(x, bd):
    return _dot(x.astype(BF), bd) * (1.0 / FOX_HEAD_DIM)


def _mask_dot(mask01, x):
    mb = mask01.astype(BF)
    hi = x.astype(BF)
    r1 = x - hi.astype(F32)
    mid = r1.astype(BF)
    lo = (r1 - mid.astype(F32)).astype(BF)
    return _dot(mb, hi) + _dot(mb, mid) + _dot(mb, lo)


def _fox_prep(proj, wq_t, wk_t, bias_pad, seq, name):
    t = proj.shape[0]
    ts = _tile(seq, 512)
    tpe = seq // ts
    scale = FOX_HEAD_DIM ** -0.5

    def body(q_ref, k_ref, v_ref, gt_ref, wq_ref, wk_ref, b_ref, qo_ref, ko_ref, vo_ref, cum_ref, carry):
        i = pl.program_id(0)
        bd = _head_sum_matrix(FOX_WIDTH, FOX_HEAD_DIM)

        def norm(xv, wv):
            ms = _head_mean(xv * xv, bd)
            return xv * lax.rsqrt(ms + EPS) * wv

        qo_ref[...] = (norm(q_ref[...], wq_ref[...]) * scale).astype(BF)
        ko_ref[...] = norm(k_ref[...], wk_ref[...]).astype(BF)
        vo_ref[...] = v_ref[...].astype(BF)

        @pl.when(i % tpe == 0)
        def _():
            carry[...] = jnp.zeros_like(carry)

        ls = _log_sigmoid(gt_ref[...] + b_ref[...])
        r = lax.broadcasted_iota(jnp.int32, (ts, ts), 0)
        c = lax.broadcasted_iota(jnp.int32, (ts, ts), 1)
        cum = _mask_dot(r >= c, ls) + carry[...]
        cum_ref[...] = cum
        carry[...] = cum[ts - 1:ts, :]

    blk = lambda j: pl.BlockSpec((ts, FOX_WIDTH), lambda i: (i, j))
    gate = pl.BlockSpec((ts, LANES), lambda i: (i, GATE_COL // LANES))
    out = pl.BlockSpec((ts, FOX_WIDTH), lambda i: (i, 0))
    return pl.pallas_call(
        body, name=name, grid=(t // ts,),
        in_specs=[blk(0), blk(1), blk(2), gate, _full((1, FOX_WIDTH)), _full((1, FOX_WIDTH)), _full((1, LANES))],
        out_specs=[out, out, out, pl.BlockSpec((ts, LANES), lambda i: (i, 0))],
        out_shape=[jax.ShapeDtypeStruct((t, FOX_WIDTH), BF)] * 3 + [jax.ShapeDtypeStruct((t, LANES), F32)],
        scratch_shapes=[pltpu.VMEM((1, LANES), F32)],
        compiler_params=_params(("arbitrary",)),
    )(proj, proj, proj, proj, wq_t, wk_t, bias_pad)


def _pick_lanes(x, lane_in_block, first_out_lane):
    r = lax.broadcasted_iota(jnp.int32, (FOX_WIDTH, LANES), 0)
    c = lax.broadcasted_iota(jnp.int32, (FOX_WIDTH, LANES), 1)
    sel = ((r % LANES == lane_in_block) & (c == first_out_lane + 2 * (r // LANES))).astype(BF)
    hi = x.astype(BF)
    r1 = x - hi.astype(F32)
    mid = r1.astype(BF)
    lo = (r1 - mid.astype(F32)).astype(BF)
    return _dot(hi, sel) + _dot(mid, sel) + _dot(lo, sel)


def _fox_prep_bwd(proj, dqa, dqb, dk, dv, dkx, wq_t, wk_t, bias_pad, seq, name):
    t = proj.shape[0]
    ts = _tile(seq, 512)
    tpe = seq // ts
    nt = t // ts
    scale = FOX_HEAD_DIM ** -0.5

    def body(q_ref, k_ref, gt_ref, dqa_ref, dqb_ref, dk_ref, dv_ref, dc_ref, wq_ref, wk_ref, b_ref,
             dp_ref, dff_ref, dwq_ref, dwk_ref, db_ref, carry):
        i = pl.program_id(0)
        first = (lax.broadcasted_iota(jnp.int32, (ts, FOX_WIDTH), 1) % LANES) < FOX_HEAD_DIM
        dq_all = jnp.where(first, dqa_ref[...], dqb_ref[...])
        ti = nt - 1 - i
        bd = _head_sum_matrix(FOX_WIDTH, FOX_HEAD_DIM)

        @pl.when(i == 0)
        def _():
            dwq_ref[...] = jnp.zeros_like(dwq_ref)
            dwk_ref[...] = jnp.zeros_like(dwk_ref)
            db_ref[...] = jnp.zeros_like(db_ref)

        def norm_bwd(xv, wv, dyv):
            ms = _head_mean(xv * xv, bd)
            rstd = lax.rsqrt(ms + EPS)
            xhat = xv * rstd
            dxhat = dyv * wv
            mean = _head_mean(dxhat * xhat, bd)
            return rstd * (dxhat - xhat * mean), jnp.sum(dyv * xhat, axis=0, keepdims=True)

        dxq, dwq = norm_bwd(q_ref[...], wq_ref[...], dq_all * scale)
        dxk, dwk = norm_bwd(k_ref[...], wk_ref[...], dk_ref[...])
        dp_ref[:, 0:FOX_WIDTH] = dxq.astype(BF)
        dp_ref[:, FOX_WIDTH:2 * FOX_WIDTH] = dxk.astype(BF)
        dp_ref[:, 2 * FOX_WIDTH:3 * FOX_WIDTH] = dv_ref[...].astype(BF)
        dwq_ref[...] += dwq
        dwk_ref[...] += dwk

        @pl.when(ti % tpe == tpe - 1)
        def _():
            carry[...] = jnp.zeros_like(carry)

        r = lax.broadcasted_iota(jnp.int32, (ts, ts), 0)
        c = lax.broadcasted_iota(jnp.int32, (ts, ts), 1)
        dkx = dc_ref[...]
        hd = FOX_HEAD_DIM
        dcum = (_pick_lanes(dqa_ref[...], hd, 0) + _pick_lanes(dqb_ref[...], 0, 1)
                - _pick_lanes(dkx, hd, 0) - _pick_lanes(dkx, 0, 1))
        dls = _mask_dot(c >= r, dcum) + carry[...]
        carry[...] = dls[0:1, :]
        z = gt_ref[...] + b_ref[...]
        lane = lax.broadcasted_iota(jnp.int32, (ts, LANES), 1)
        dff = jnp.where(lane < FOX_HEADS, dls * _sigmoid(-z), 0.0)
        dff_ref[...] = dff
        db_ref[...] += jnp.sum(dff, axis=0, keepdims=True)

        @pl.when(i == nt - 1)
        def _():
            fr = lax.broadcasted_iota(jnp.int32, (FOX_WIDTH, FOX_WIDTH), 0) % FOX_HEAD_DIM
            fc = lax.broadcasted_iota(jnp.int32, (FOX_WIDTH, FOX_WIDTH), 1) % FOX_HEAD_DIM
            fold = (fr == fc).astype(F32)
            dwq_ref[...] = _dot(dwq_ref[...], fold, HI)
            dwk_ref[...] = _dot(dwk_ref[...], fold, HI)

    rev = lambda w, j: pl.BlockSpec((ts, w), lambda i: (nt - 1 - i, j))
    return pl.pallas_call(
        body, name=name, grid=(nt,),
        in_specs=[rev(FOX_WIDTH, 0), rev(FOX_WIDTH, 1), rev(LANES, GATE_COL // LANES),
                  rev(FOX_WIDTH, 0), rev(FOX_WIDTH, 0), rev(FOX_WIDTH, 0), rev(FOX_WIDTH, 0), rev(FOX_WIDTH, 0),
                  _full((1, FOX_WIDTH)), _full((1, FOX_WIDTH)), _full((1, LANES))],
        out_specs=[rev(3 * FOX_WIDTH, 0), rev(LANES, 0), _full((1, FOX_WIDTH)), _full((1, FOX_WIDTH)),
                   _full((1, LANES))],
        out_shape=[jax.ShapeDtypeStruct((t, 3 * FOX_WIDTH), BF), jax.ShapeDtypeStruct((t, LANES), F32),
                   jax.ShapeDtypeStruct((1, FOX_WIDTH), F32), jax.ShapeDtypeStruct((1, FOX_WIDTH), F32),
                   jax.ShapeDtypeStruct((1, LANES), F32)],
        scratch_shapes=[pltpu.VMEM((1, LANES), F32)],
        compiler_params=_params(("arbitrary",)),
    )(proj, proj, proj, dqa, dqb, dk, dv, dkx, wq_t, wk_t, bias_pad)


class _Rider:
    def __init__(self, inputs, out_shapes, sems, phases):
        self.inputs, self.out_shapes, self.sems, self.phases = list(inputs), list(out_shapes), list(sems), phases


def _rider_parts(rider):
    if rider is None:
        return [], [], []
    return rider.inputs, rider.out_shapes, rider.sems


def _ride(rider, which, when, refs):
    if rider is not None:
        @pl.when(when)
        def _():
            rider.phases[which](*refs)


def _fox_attn(q, k, v, ck, nb, seq, name, rider=None):
    t = q.shape[0]
    tq = _tile(seq, 2048)
    nq = seq // tq
    npair = FOX_HEADS // 2
    hd = FOX_HEAD_DIM
    r_in, r_out, r_sem = _rider_parts(rider)
    steps = nb * npair * nq

    def body(q_ref, k_ref, v_ref, ck_ref, *rest):
        rin, (o_ref, lse_ref) = rest[:len(r_in)], rest[len(r_in):len(r_in) + 2]
        rout = rest[len(r_in) + 2:len(r_in) + 2 + len(r_out)]
        m_s, acc_s = rest[len(r_in) + 2 + len(r_out):len(r_in) + 4 + len(r_out)]
        riding = (rin, rout, rest[len(r_in) + 4 + len(r_out):])
        step = (pl.program_id(0) * npair + pl.program_id(1)) * nq + pl.program_id(2)
        _ride(rider, 0, step == 0, riding)
        _ride(rider, 1, step == steps // 2, riding)
        qi = pl.program_id(2)
        lane = lax.broadcasted_iota(jnp.int32, (tq, LANES), 1)
        m_s[...] = jnp.full(m_s.shape, NEG, F32)
        acc_s[...] = jnp.zeros_like(acc_s)
        qv = q_ref[...]

        def tile(kj, on_diagonal):
            cols = pl.ds(pl.multiple_of(kj * tq, tq), tq)
            kv = k_ref[cols, :]
            vv = v_ref[cols, :]
            if on_diagonal:
                causal = (lax.broadcasted_iota(jnp.int32, (tq, tq), 0)
                          >= lax.broadcasted_iota(jnp.int32, (tq, tq), 1))
            ck = [ck_ref[hh, :, cols] for hh in range(2)]
            m_old = [m_s[hh] for hh in range(2)]
            acc_old = [acc_s[hh] for hh in range(2)]
            m_out, acc_out = [], []
            for hh in range(2):
                hm = (lane >= hd) if hh else (lane < hd)
                qh = jnp.where(hm, qv, jnp.zeros_like(qv))
                s = _dot_nt(qh, kv) - ck[hh]
                if on_diagonal:
                    s = jnp.where(causal, s, NEG)
                m_new = jnp.maximum(m_old[hh], jnp.max(s, axis=-1, keepdims=True))
                p = jnp.exp(s - m_new)
                alpha = jnp.exp(m_old[hh] - m_new)
                m_out.append(m_new)
                acc_out.append(alpha * acc_old[hh] + _dot(p.astype(BF), jnp.where(hm, vv, jnp.ones_like(vv))))
            for hh in range(2):
                m_s[hh] = m_out[hh]
                acc_s[hh] = acc_out[hh]

        def off_diagonal(kj, carry):
            tile(kj, False)
            return carry

        lax.fori_loop(0, qi, off_diagonal, 0)
        tile(qi, True)
        a0 = acc_s[0]
        a1 = acc_s[1]
        den = jnp.where(lane < hd, pltpu.roll(a0, hd, axis=1), pltpu.roll(a1, hd, axis=1))
        o_ref[...] = (jnp.where(lane < hd, a0, a1) / den).astype(o_ref.dtype)
        l0 = jnp.sum(jnp.where(lane == hd, a0, 0.0), axis=1, keepdims=True)
        l1 = jnp.sum(jnp.where(lane == 0, a1, 0.0), axis=1, keepdims=True)
        lse_ref[0] = m_s[0] + jnp.log(l0)
        lse_ref[1] = m_s[1] + jnp.log(l1)
        _ride(rider, 2, step == steps - 1, riding)

    qspec = pl.BlockSpec((tq, LANES), lambda b, p, i: (b * nq + i, p))
    kspec = pl.BlockSpec((seq, LANES), lambda b, p, i: (b, p))
    colspec = pl.BlockSpec((None, 2, tq, 1), lambda b, p, i: (b * npair + p, 0, i, 0))
    rowspec = pl.BlockSpec((None, 2, 1, seq), lambda b, p, i: (b * npair + p, 0, 0, 0))
    sem = ("arbitrary",) * 3 if rider else ("parallel",) * 3
    return pl.pallas_call(
        body, name=name, grid=(nb, npair, nq),
        in_specs=[qspec, kspec, kspec, rowspec] + [HBM] * len(r_in),
        out_specs=[qspec, colspec] + [HBM] * len(r_out),
        out_shape=[jax.ShapeDtypeStruct((t, FOX_WIDTH), BF), jax.ShapeDtypeStruct((nb * npair, 2, seq, 1), F32)]
        + r_out,
        scratch_shapes=[pltpu.VMEM((2, tq, 1), F32), pltpu.VMEM((2, tq, LANES), F32)] + r_sem,
        compiler_params=_params(sem, has_side_effects=rider is not None),
    )(q, k, v, ck, *r_in)


def _fox_attn_bwd(q, k, v, o, do, lse, ck, nb, seq, name, rider=None):
    t = q.shape[0]
    tq = _tile(seq, 1024)
    nq = seq // tq
    npair = FOX_HEADS // 2
    hd = FOX_HEAD_DIM
    r_in, r_out, r_sem = _rider_parts(rider)
    steps = nb * npair * nq

    def body(q_ref, k_ref, v_ref, o_ref, do_ref, lse_ref, ck_ref, *rest):
        rin, (dqa_ref, dqb_ref, dk_ref, dv_ref, dkx_ref) = rest[:len(r_in)], rest[len(r_in):len(r_in) + 5]
        rout = rest[len(r_in) + 5:len(r_in) + 5 + len(r_out)]
        dk_s, dv_s = rest[len(r_in) + 5 + len(r_out):len(r_in) + 7 + len(r_out)]
        riding = (rin, rout, rest[len(r_in) + 7 + len(r_out):])
        step = (pl.program_id(0) * npair + pl.program_id(1)) * nq + pl.program_id(2)
        _ride(rider, 0, step == 0, riding)
        _ride(rider, 1, step == steps // 2, riding)
        kj = pl.program_id(2)
        lane = lax.broadcasted_iota(jnp.int32, (tq, LANES), 1)

        @pl.when(kj == 0)
        def _():
            dqa_ref[...] = jnp.zeros_like(dqa_ref)
            dqb_ref[...] = jnp.zeros_like(dqb_ref)

        dk_s[...] = jnp.zeros_like(dk_s)
        dv_s[...] = jnp.zeros_like(dv_s)
        kv = k_ref[...]
        vv = v_ref[...]

        def tile(qi, on_diagonal):
            rows = pl.ds(pl.multiple_of(qi * tq, tq), tq)
            qv = q_ref[rows, :]
            dov = do_ref[rows, :]
            prod = dov.astype(F32) * o_ref[rows, :].astype(F32)
            if on_diagonal:
                causal = (lax.broadcasted_iota(jnp.int32, (tq, tq), 0)
                          >= lax.broadcasted_iota(jnp.int32, (tq, tq), 1))
            for hh, dq_ref in ((0, dqa_ref), (1, dqb_ref)):
                hm = (lane >= hd) if hh else (lane < hd)
                zero = jnp.zeros_like(qv)
                one = jnp.ones_like(qv)
                doh = jnp.where(hm, dov, zero)
                delta = jnp.sum(jnp.where(hm, prod, 0.0), axis=-1, keepdims=True)
                s = _dot_nt(jnp.where(hm, qv, zero), kv) - ck_ref[hh]
                if on_diagonal:
                    s = jnp.where(causal, s, NEG)
                p = jnp.exp(s - lse_ref[hh, rows, :])
                dp = _dot_nt(doh, vv)
                dsb = (p * (dp - delta)).astype(BF)
                dv_s[...] += _dot_tn(p.astype(BF), doh)
                dk_s[hh] += _dot_tn(dsb, jnp.where(hm, qv, one))
                dq_ref[rows, :] += _dot(dsb, jnp.where(hm, kv, one))

        def off_diagonal(qi, carry):
            tile(qi, False)
            return carry

        tile(kj, True)
        lax.fori_loop(kj + 1, nq, off_diagonal, 0)
        dk_ref[...] = jnp.where(lane < hd, dk_s[0], dk_s[1])
        dkx_ref[...] = jnp.where(lane < hd, dk_s[1], dk_s[0])
        dv_ref[...] = dv_s[...]
        _ride(rider, 2, step == steps - 1, riding)

    kspec = pl.BlockSpec((tq, LANES), lambda b, p, j: (b * nq + j, p))
    full_q = pl.BlockSpec((seq, LANES), lambda b, p, j: (b, p))
    colspec = pl.BlockSpec((None, 2, seq, 1), lambda b, p, j: (b * npair + p, 0, 0, 0))
    rowspec = pl.BlockSpec((None, 2, 1, tq), lambda b, p, j: (b * npair + p, 0, 0, j))
    sem = ("arbitrary",) * 3 if rider else ("parallel", "parallel", "arbitrary")
    return pl.pallas_call(
        body, name=name, grid=(nb, npair, nq),
        in_specs=[full_q, kspec, kspec, full_q, full_q, colspec, rowspec] + [HBM] * len(r_in),
        out_specs=[full_q, full_q, kspec, kspec, kspec] + [HBM] * len(r_out),
        out_shape=[jax.ShapeDtypeStruct((t, FOX_WIDTH), F32)] * 5 + r_out,
        scratch_shapes=[pltpu.VMEM((2, tq, LANES), F32), pltpu.VMEM((tq, LANES), F32)] + r_sem,
        compiler_params=_params(sem, has_side_effects=rider is not None),
    )(q, k, v, o, do, lse, ck, *r_in)


GDN_QKV = 3 * GDN_WIDTH
GDN_COL = 3 * FOX_WIDTH
GG_COL = GDN_COL + GDN_QKV
A_LANE = FOX_HEADS
B_LANE = FOX_HEADS + GDN_HEADS
HALO = 8


def _gate_lanes(ts):
    lane = lax.broadcasted_iota(jnp.int32, (ts, LANES), 1)
    return (lane >= A_LANE) & (lane < B_LANE), (lane >= B_LANE) & (lane < B_LANE + GDN_HEADS)


def _chunk_tri(ts, upper):
    r = lax.broadcasted_iota(jnp.int32, (ts, ts), 0)
    c = lax.broadcasted_iota(jnp.int32, (ts, ts), 1)
    same = (r // CHUNK) == (c // CHUNK)
    return (same & ((c >= r) if upper else (r >= c))).astype(F32)


def _conv_silu_l2(xp_ref, w, ts):
    c = w[0:1, :] * xp_ref[pl.ds(HALO - 3, ts), :]
    for kk in range(1, CONV_WIDTH):
        c = c + w[kk:kk + 1, :] * xp_ref[pl.ds(HALO - 3 + kk, ts), :]
    return c, c * _sigmoid(c)


def _gdn_prep(proj, conv_w, a_pad, dt_pad, seq, name):
    t = proj.shape[0]
    ts = _tile(seq, 256)
    tpe = seq // ts
    qscale = GDN_HEAD_DIM ** -0.5

    def body(x_ref, gt_ref, w_ref, a_ref, dt_ref, qo_ref, ko_ref, vo_ref, go_ref, xp):
        i = pl.program_id(0)
        tail = xp[pl.ds(ts, HALO), :]
        xp[pl.ds(0, HALO), :] = jnp.where(i % tpe == 0, jnp.zeros_like(tail), tail)
        xp[pl.ds(HALO, ts), :] = x_ref[...]
        _, s = _conv_silu_l2(xp, w_ref[...], ts)
        for h in range(GDN_HEADS):
            for base, ref, sc in ((0, qo_ref, qscale), (GDN_WIDTH, ko_ref, 1.0)):
                xh = s[:, base + h * LANES: base + (h + 1) * LANES]
                r = lax.rsqrt(jnp.sum(xh * xh, axis=-1, keepdims=True) + EPS)
                ref[:, h * LANES:(h + 1) * LANES] = (xh * (r * sc)).astype(BF)
        vo_ref[...] = s[:, 2 * GDN_WIDTH:].astype(BF)
        gate = gt_ref[...]
        g_raw = -jnp.exp(a_ref[...]) * _softplus(gate + dt_ref[...])
        gc = _mask_dot(_chunk_tri(ts, False), g_raw)
        is_a, is_b = _gate_lanes(ts)
        go_ref[...] = jnp.where(is_a, gc, jnp.where(is_b, _sigmoid(gate), 0.0))

    out = pl.BlockSpec((ts, GDN_WIDTH), lambda i: (i, 0))
    lanes = pl.BlockSpec((ts, LANES), lambda i: (i, 0))
    return pl.pallas_call(
        body, name=name, grid=(t // ts,),
        in_specs=[pl.BlockSpec((ts, GDN_QKV), lambda i: (i, GDN_COL // GDN_QKV)),
                  pl.BlockSpec((ts, LANES), lambda i: (i, GATE_COL // LANES)),
                  _full((CONV_WIDTH, GDN_QKV)), _full((1, LANES)), _full((1, LANES))],
        out_specs=[out, out, out, lanes],
        out_shape=[jax.ShapeDtypeStruct((t, GDN_WIDTH), BF)] * 3 + [jax.ShapeDtypeStruct((t, LANES), F32)],
        scratch_shapes=[pltpu.VMEM((ts + HALO, GDN_QKV), F32)],
        compiler_params=_params(("arbitrary",)),
    )(proj, proj, conv_w, a_pad, dt_pad)


def _gdn_prep_bwd(proj, dq, dk, dv, dgates, dff, conv_w, a_pad, dt_pad, seq, name):
    t = proj.shape[0]
    ts = _tile(seq, 256)
    tpe = seq // ts
    nt = t // ts
    qscale = GDN_HEAD_DIM ** -0.5
    hb = ts // HALO

    def body(x_ref, halo_ref, gt_ref, dq_ref, dk_ref, dv_ref, dgt_ref, dff_ref, w_ref, a_ref, dt_ref,
             dx_ref, dgo_ref, dw_ref, da_ref, ddt_ref, xp, dcp, carry):
        i = pl.program_id(0)
        ti = nt - 1 - i

        @pl.when(i == 0)
        def _():
            dw_ref[...] = jnp.zeros_like(dw_ref)
            da_ref[...] = jnp.zeros_like(da_ref)
            ddt_ref[...] = jnp.zeros_like(ddt_ref)

        halo = halo_ref[...]
        xp[pl.ds(0, HALO), :] = jnp.where(ti % tpe == 0, jnp.zeros_like(halo), halo)
        xp[pl.ds(HALO, ts), :] = x_ref[...]
        w = w_ref[...]
        c, s = _conv_silu_l2(xp, w, ts)
        for h in range(GDN_HEADS):
            for base, ref, sc in ((0, dq_ref, qscale), (GDN_WIDTH, dk_ref, 1.0)):
                lo = base + h * LANES
                xh = s[:, lo:lo + LANES]
                r = lax.rsqrt(jnp.sum(xh * xh, axis=-1, keepdims=True) + EPS)
                y = xh * r
                dy = ref[:, h * LANES:(h + 1) * LANES] * sc
                dcp[pl.ds(0, ts), lo:lo + LANES] = r * (dy - y * jnp.sum(dy * y, axis=-1, keepdims=True))
        dcp[pl.ds(0, ts), 2 * GDN_WIDTH:] = dv_ref[...]
        sg = _sigmoid(c)
        dc = dcp[pl.ds(0, ts), :] * (sg * (1.0 + c * (1.0 - sg)))
        dcp[pl.ds(0, ts), :] = dc
        nxt = carry[...]
        dcp[pl.ds(ts, HALO), :] = jnp.where(ti % tpe == tpe - 1, jnp.zeros_like(nxt), nxt)
        carry[...] = dc[0:HALO, :]
        dx = w[CONV_WIDTH - 1:CONV_WIDTH, :] * dc
        for kk in range(CONV_WIDTH - 1):
            dx = dx + w[kk:kk + 1, :] * dcp[pl.ds(CONV_WIDTH - 1 - kk, ts), :]
        dx_ref[...] = dx.astype(BF)
        for kk in range(CONV_WIDTH):
            dw_ref[kk:kk + 1, :] += jnp.sum(dc * xp[pl.ds(HALO - 3 + kk, ts), :], axis=0, keepdims=True)
        gate = gt_ref[...]
        dgt = dgt_ref[...]
        is_a, is_b = _gate_lanes(ts)
        dg_raw = _mask_dot(_chunk_tri(ts, True), jnp.where(is_a, dgt, 0.0))
        z = gate + dt_ref[...]
        na = -jnp.exp(a_ref[...])
        dga = dg_raw * na * _sigmoid(z)
        beta = _sigmoid(gate)
        dgb = jnp.where(is_b, dgt * beta * (1.0 - beta), 0.0)
        dgo_ref[...] = (dff_ref[...] + dga + dgb).astype(BF)
        ddt_ref[...] += jnp.sum(dga, axis=0, keepdims=True)
        da_ref[...] += jnp.sum(dg_raw * na * _softplus(z), axis=0, keepdims=True)

    rev = lambda wd, j: pl.BlockSpec((ts, wd), lambda i: (nt - 1 - i, j))
    halo_spec = pl.BlockSpec((HALO, GDN_QKV), lambda i: (jnp.maximum((nt - 1 - i) * hb - 1, 0), GDN_COL // GDN_QKV))
    return pl.pallas_call(
        body, name=name, grid=(nt,),
        in_specs=[rev(GDN_QKV, GDN_COL // GDN_QKV), halo_spec, rev(LANES, GATE_COL // LANES),
                  rev(GDN_WIDTH, 0), rev(GDN_WIDTH, 0), rev(GDN_WIDTH, 0), rev(LANES, 0), rev(LANES, 0),
                  _full((CONV_WIDTH, GDN_QKV)), _full((1, LANES)), _full((1, LANES))],
        out_specs=[rev(GDN_QKV, 0), rev(LANES, 0), _full((CONV_WIDTH, GDN_QKV)), _full((1, LANES)),
                   _full((1, LANES))],
        out_shape=[jax.ShapeDtypeStruct((t, GDN_QKV), BF), jax.ShapeDtypeStruct((t, LANES), BF),
                   jax.ShapeDtypeStruct((CONV_WIDTH, GDN_QKV), F32), jax.ShapeDtypeStruct((1, LANES), F32),
                   jax.ShapeDtypeStruct((1, LANES), F32)],
        scratch_shapes=[pltpu.VMEM((ts + HALO, GDN_QKV), F32), pltpu.VMEM((ts + HALO, GDN_QKV), F32),
                        pltpu.VMEM((HALO, GDN_QKV), F32)],
        compiler_params=_params(("arbitrary",)),
    )(proj, proj, proj, dq, dk, dv, dgates, dff, conv_w, a_pad, dt_pad)


PAIR = 2 * CHUNK


def _split_bf16(a):
    hi = a.astype(BF)
    return hi, (a - hi.astype(F32)).astype(BF)


def _dot3(a, b, dims=(((1,), (0,)), ((), ()))):
    ah, al = _split_bf16(a)
    bh, bl = _split_bf16(b)
    (ca,), (cb,) = dims[0]
    return lax.dot_general(jnp.concatenate([ah, al, ah], axis=ca), jnp.concatenate([bh, bh, bl], axis=cb), dims,
                           preferred_element_type=F32)


def _inv_unit_lower(a):
    r = lax.broadcasted_iota(jnp.int32, (PAIR, PAIR), 0)
    c = lax.broadcasted_iota(jnp.int32, (PAIR, PAIR), 1)
    tm = (r == c).astype(F32) - a
    pw = _dot3(a, a)
    for _ in range(4):
        x = _dot3(jnp.concatenate([tm, pw], axis=0), pw)
        tm = tm + x[:PAIR]
        pw = x[PAIR:]
    return tm + _dot3(tm, pw)


def _gdn_pair_local(q, k, v, gc, gr, b):
    r = lax.broadcasted_iota(jnp.int32, (PAIR, PAIR), 0)
    c = lax.broadcasted_iota(jnp.int32, (PAIR, PAIR), 1)
    same = (r // CHUNK) == (c // CHUNK)
    incl = same & (r >= c)
    strict = same & (r > c)
    dm = jnp.exp(jnp.where(incl, gc - gr, NEG))
    e = jnp.exp(gc)
    kb = k * b
    vb = v * b
    kbe = kb * e
    kq = _dot_nt(jnp.concatenate([kb, q], axis=0).astype(BF), k.astype(BF))
    amat = jnp.where(strict, kq[:PAIR] * dm, 0.0)
    pmat = jnp.where(incl, kq[PAIR:] * dm, 0.0)
    lane = lax.broadcasted_iota(jnp.int32, (1, PAIR), 1)
    gl_a = jnp.sum(jnp.where(lane == CHUNK - 1, gr, 0.0), axis=1, keepdims=True)
    gl_b = jnp.sum(jnp.where(lane == PAIR - 1, gr, 0.0), axis=1, keepdims=True)
    ridx = lax.broadcasted_iota(jnp.int32, (PAIR, 1), 0)
    edec = jnp.exp(jnp.where(ridx < CHUNK, gl_a, gl_b) - gc)
    return dict(dm=dm, e=e, kb=kb, vb=vb, kbe=kbe, amat=amat, pmat=pmat, gl_a=gl_a, gl_b=gl_b, edec=edec,
                kd=k * edec, qd=q * e, incl=incl, strict=strict, ridx=ridx)


def _gdn_pair_states(loc, tb, s_a):
    uw = _dot(tb, jnp.concatenate([loc["vb"], loc["kbe"]], axis=1).astype(BF))
    u, w = uw[:, :LANES], uw[:, LANES:]
    qd, kd, c = loc["qd"], loc["kd"], CHUNK
    xa = _dot(jnp.concatenate([qd[:c], w[:c]], axis=0).astype(BF), s_a.astype(BF))
    vn_a = u[:c] - xa[c:]
    s_b = s_a * jnp.exp(loc["gl_a"]) + _dot_tn(kd[:c].astype(BF), vn_a.astype(BF))
    xb = _dot(jnp.concatenate([qd[c:], w[c:]], axis=0).astype(BF), s_b.astype(BF))
    vn_b = u[c:] - xb[c:]
    s_c = s_b * jnp.exp(loc["gl_b"]) + _dot_tn(kd[c:].astype(BF), vn_b.astype(BF))
    vn = jnp.concatenate([vn_a, vn_b], axis=0)
    o = jnp.concatenate([xa[:c], xb[:c]], axis=0) + _dot(loc["pmat"].astype(BF), vn.astype(BF))
    return w, vn, o, s_b, s_c


GDN_SEG = 1024
HEADS_PER_STEP = 4
HEAD_GROUPS = GDN_HEADS // HEADS_PER_STEP


def _gdn_specs(nb, seq, reverse):
    n = seq // CHUNK
    seg = _tile(seq, GDN_SEG)
    nseg = seq // seg
    sp = seg // PAIR
    w2 = HEADS_PER_STEP * LANES
    at = (lambda s: nseg - 1 - s) if reverse else (lambda s: s)
    blk = pl.BlockSpec((seg, w2), lambda b, hp, s: (b * nseg + at(s), hp))
    gg = pl.BlockSpec((seg, w2), lambda b, hp, s: (b * nseg + at(s), GG_COL // w2 + hp))
    gates = pl.BlockSpec((seg, LANES), lambda b, hp, s: (b * nseg + at(s), 0))
    grp = lambda b, hp: b * HEAD_GROUPS + hp
    rowb = pl.BlockSpec((None, HEADS_PER_STEP, sp, HALO, PAIR), lambda b, hp, s: (grp(b, hp), 0, at(s), 0, 0))
    per_pair = pl.BlockSpec((None, HEADS_PER_STEP, sp, PAIR, PAIR), lambda b, hp, s: (grp(b, hp), 0, at(s), 0, 0))
    dgates = pl.BlockSpec((None, seg, LANES), lambda b, hp, s: (grp(b, hp), at(s), 0))
    return n, seg, nseg, sp, blk, gg, gates, rowb, per_pair, dgates


def _head_column(gt, lane, index):
    return jnp.sum(jnp.where(lane == index, gt, 0.0), axis=1, keepdims=True)


def _gdn_head_inputs(qkv_refs, gt, gr_ref, rows, pi, hp, lane):
    per_head = []
    for hh in range(HEADS_PER_STEP):
        head = HEADS_PER_STEP * hp + hh
        cols = slice(hh * LANES, (hh + 1) * LANES)
        per_head.append([r[rows, cols].astype(F32) for r in qkv_refs]
                        + [_head_column(gt, lane, A_LANE + head), gr_ref[hh, pi][0:1, :],
                           _head_column(gt, lane, B_LANE + head)])
    return [jnp.stack(xs) for xs in zip(*per_head)]


def _gdn_pair_fwd(qv, kv, vv, gcv, gr, bv, s_a):
    loc = _gdn_pair_local(qv, kv, vv, gcv, gr, bv)
    tf = _inv_unit_lower(loc["amat"])
    _, _, o, _, s_c = _gdn_pair_states(loc, tf.astype(BF), s_a)
    return tf, o, s_c


def _gdn_fwd(q, k, v, proj, gates, grow, wn, nb, seq, name):
    t = q.shape[0]
    n, seg, nseg, sp, blk, gg, gates_spec, rowb, per_pair, _ = _gdn_specs(nb, seq, False)

    def body(q_ref, k_ref, v_ref, gg_ref, gt_ref, gr_ref, wn_ref, y_ref, tn_ref, sn_ref, s_ref):
        hp = pl.program_id(1)

        @pl.when(pl.program_id(2) == 0)
        def _():
            s_ref[...] = jnp.zeros_like(s_ref)

        wnv = wn_ref[...]
        lane = lax.broadcasted_iota(jnp.int32, (PAIR, LANES), 1)

        def step(pi, carry):
            rows = pl.ds(pl.multiple_of(pi * PAIR, PAIR), PAIR)
            gt = gt_ref[rows, :]
            ins = _gdn_head_inputs((q_ref, k_ref, v_ref), gt, gr_ref, rows, pi, hp, lane)
            s_a = s_ref[...]
            tf, o, s_c = jax.vmap(_gdn_pair_fwd)(*ins, s_a)
            s_ref[...] = s_c
            for hh in range(HEADS_PER_STEP):
                cols = slice(hh * LANES, (hh + 1) * LANES)
                tn_ref[hh, pi] = tf[hh]
                sn_ref[hh, pi] = s_a[hh]
                g = gg_ref[rows, cols]
                oh = o[hh]
                rstd = lax.rsqrt(jnp.mean(oh * oh, axis=-1, keepdims=True) + EPS)
                y_ref[rows, cols] = (oh * rstd * wnv * (g * _sigmoid(g))).astype(BF)
            return carry

        lax.fori_loop(0, sp, step, 0)

    saved = jax.ShapeDtypeStruct((nb * HEAD_GROUPS, HEADS_PER_STEP, n // 2, PAIR, PAIR), F32)
    return pl.pallas_call(
        body, name=name, grid=(nb, GDN_HEADS // HEADS_PER_STEP, nseg),
        in_specs=[blk, blk, blk, gg, gates_spec, rowb, _full((1, LANES))],
        out_specs=[blk, per_pair, per_pair],
        out_shape=[jax.ShapeDtypeStruct((t, GDN_WIDTH), BF), saved, saved],
        scratch_shapes=[pltpu.VMEM((HEADS_PER_STEP, GDN_HEAD_DIM, GDN_HEAD_DIM), F32)],
        compiler_params=_params(("parallel", "parallel", "arbitrary")),
    )(q, k, v, proj, gates, grow, wn)


def _gdn_pair_bwd(qv, kv, vv, gcv, gr, bv, tf, s_a, dsp, g, dyv, wnv):
    c = CHUNK
    loc = _gdn_pair_local(qv, kv, vv, gcv, gr, bv)
    tm = tf.astype(BF)
    kb, vb, kbe, e, dm = loc["kb"], loc["vb"], loc["kbe"], loc["e"], loc["dm"]
    kd, qd, pmat, amat = loc["kd"], loc["qd"], loc["pmat"], loc["amat"]
    w, vn, o, s_b, _ = _gdn_pair_states(loc, tm, s_a)
    sg = _sigmoid(g)
    silu = g * sg
    rstd = lax.rsqrt(jnp.mean(o * o, axis=-1, keepdims=True) + EPS)
    xhat = o * rstd
    dwn = jnp.sum(dyv * xhat * silu, axis=0, keepdims=True)
    dgg = dyv * xhat * wnv * (sg * (1.0 + g * (1.0 - sg)))
    dxhat = dyv * wnv * silu
    do = rstd * (dxhat - xhat * jnp.mean(dxhat * xhat, axis=-1, keepdims=True))
    dob = do.astype(BF)
    tot = lambda x: jnp.sum(jnp.sum(x, axis=1, keepdims=True), axis=0, keepdims=True)
    rsum = lambda x: jnp.sum(x, axis=1, keepdims=True)
    cat = lambda xs, ax=0: jnp.concatenate(xs, axis=ax)
    wb = w.astype(BF)
    qdb = qd.astype(BF)
    kdb = kd.astype(BF)
    vnb = vn.astype(BF)
    egl_a = jnp.exp(loc["gl_a"])
    egl_b = jnp.exp(loc["gl_b"])
    ptdo = _dot_tn(pmat.astype(BF), dob)
    dspb = dsp.astype(BF)
    dvn_b = ptdo[c:] + _dot(kdb[c:], dspb)
    dkd_b = _dot_nt(vnb[c:], dspb)
    dgl_b = egl_b * tot(s_b * dsp) + tot(dkd_b * kd[c:])
    dsm = egl_b * dsp + _dot_tn(cat([qdb[c:], -wb[c:]]), cat([dob[c:], dvn_b.astype(BF)]))
    dsmb = dsm.astype(BF)
    dvn_a = ptdo[:c] + _dot(kdb[:c], dsmb)
    dkd_a = _dot_nt(vnb[:c], dsmb)
    dgl_a = egl_a * tot(s_a * dsm) + tot(dkd_a * kd[:c])
    ds_new = egl_a * dsm + _dot_tn(cat([qdb[:c], -wb[:c]]), cat([dob[:c], dvn_a.astype(BF)]))
    ya = _dot_nt(cat([dob[:c], dvn_a.astype(BF)]), s_a.astype(BF))
    yb = _dot_nt(cat([dob[c:], dvn_b.astype(BF)]), s_b.astype(BF))
    dqd = cat([ya[:c], yb[:c]])
    dw = -cat([ya[c:], yb[c:]])
    dvn = cat([dvn_a, dvn_b])
    dkd = cat([dkd_a, dkd_b])
    dq = dqd * e
    dgc = rsum(dqd * qd) - rsum(dkd * kd)
    dk = dkd * loc["edec"]
    dpm = jnp.where(loc["incl"], _dot_nt(dob, vnb), 0.0)
    duw = cat([dvn, dw], 1).astype(BF)
    dt = _dot_nt(duw, cat([vb, kbe], 1).astype(BF))
    tt = _dot_tn(tm, duw)
    dvb, dkbe = tt[:, :LANES], tt[:, LANES:]
    tn_dims = (((0,), (0,)), ((), ()))
    nt_dims = (((1,), (1,)), ((), ()))
    da = jnp.where(loc["strict"], -_dot3(_dot3(tf, dt, tn_dims), tf, nt_dims), 0.0)
    st = cat([da * dm, dpm * dm]).astype(BF)
    z = _dot(st, kv.astype(BF))
    dkb = z[:PAIR] + dkbe * e
    dq = dq + z[PAIR:]
    dk = dk + _dot_tn(st, cat([kb, qv]).astype(BF))
    gmat = dpm * pmat + da * amat
    dgc = dgc + rsum(dkbe * kbe) + rsum(gmat)
    ridx = loc["ridx"]
    dgc = dgc + jnp.where(ridx == c - 1, dgl_a, 0.0) + jnp.where(ridx == PAIR - 1, dgl_b, 0.0)
    dgc_row = jnp.sum(gmat, axis=0, keepdims=True)
    db = rsum(dvb * vv) + rsum(dkb * kv)
    return dq, dk + dkb * bv, dvb * bv, dgg, dgc, dgc_row, db, dwn, ds_new


def _gdn_bwd(q, k, v, proj, gates, grow, wn, tinv_all, states_all, dy, nb, seq, name):
    t = q.shape[0]
    n, seg, nseg, sp, blk, gg, gates_spec, rowb, per_pair, dgates = _gdn_specs(nb, seq, True)
    dh = GDN_HEAD_DIM

    def body(q_ref, k_ref, v_ref, gg_ref, gt_ref, gr_ref, wn_ref, tn_ref, sn_ref, dy_ref,
             dq_ref, dk_ref, dv_ref, dgg_ref, dgt_ref, dwn_ref, ds_ref):
        hp = pl.program_id(1)

        @pl.when((pl.program_id(0) == 0) & (hp == 0) & (pl.program_id(2) == 0))
        def _():
            dwn_ref[...] = jnp.zeros_like(dwn_ref)

        @pl.when(pl.program_id(2) == 0)
        def _():
            ds_ref[...] = jnp.zeros_like(ds_ref)

        wnv = wn_ref[...]
        lane = lax.broadcasted_iota(jnp.int32, (PAIR, LANES), 1)

        def bwd_step(j, carry):
            pi = sp - 1 - j
            rows = pl.ds(pl.multiple_of(pi * PAIR, PAIR), PAIR)
            gt = gt_ref[rows, :]
            ins = _gdn_head_inputs((q_ref, k_ref, v_ref), gt, gr_ref, rows, pi, hp, lane)
            halves = [slice(hh * LANES, (hh + 1) * LANES) for hh in range(HEADS_PER_STEP)]
            saved = [jnp.stack([r[hh, pi] for hh in range(HEADS_PER_STEP)]) for r in (tn_ref, sn_ref)]
            g2 = jnp.stack([gg_ref[rows, cols] for cols in halves])
            dy2 = jnp.stack([dy_ref[rows, cols].astype(F32) for cols in halves])
            dq, dk, dv, dgg, dgc, dgc_row, db, dwn, ds_new = jax.vmap(
                _gdn_pair_bwd, in_axes=(0,) * 11 + (None,))(*ins, *saved, ds_ref[...], g2, dy2, wnv)
            ds_ref[...] = ds_new
            dgt = jnp.zeros((PAIR, LANES), F32)
            for hh, cols in enumerate(halves):
                head = HEADS_PER_STEP * hp + hh
                dq_ref[rows, cols] = dq[hh]
                dk_ref[rows, cols] = dk[hh]
                dv_ref[rows, cols] = dv[hh]
                dgg_ref[rows, cols] = dgg[hh].astype(BF)
                dwn_ref[...] += dwn[hh]
                row_as_col = jnp.transpose(jnp.broadcast_to(dgc_row[hh], (PAIR, LANES)))
                dgt = (dgt + jnp.where(lane == A_LANE + head, dgc[hh] - row_as_col, 0.0)
                       + jnp.where(lane == B_LANE + head, db[hh], 0.0))
            dgt_ref[rows, :] = dgt
            return carry

        lax.fori_loop(0, sp, bwd_step, 0)

    return pl.pallas_call(
        body, name=name, grid=(nb, GDN_HEADS // HEADS_PER_STEP, nseg),
        in_specs=[blk, blk, blk, gg, gates_spec, rowb, _full((1, LANES)), per_pair, per_pair, blk],
        out_specs=[blk, blk, blk, blk, dgates, _full((1, LANES))],
        out_shape=[jax.ShapeDtypeStruct((t, GDN_WIDTH), F32)] * 3 + [
            jax.ShapeDtypeStruct((t, GDN_WIDTH), BF),
            jax.ShapeDtypeStruct((nb * HEAD_GROUPS, seq, LANES), F32),
            jax.ShapeDtypeStruct((1, LANES), F32)],
        scratch_shapes=[pltpu.VMEM((HEADS_PER_STEP, dh, dh), F32)],
        compiler_params=_params(("arbitrary", "arbitrary", "arbitrary")),
    )(q, k, v, proj, gates, grow, wn, tinv_all, states_all, dy)


def _mix_to_padded(w):
    pad = jnp.zeros(w.shape[:-1] + (N_PAD - N_IN,), w.dtype)
    return jnp.concatenate([w[..., 0:1536], w[..., 1544:3080], w[..., 3088:3600], w[..., 1536:1544],
                            w[..., 3080:3088], pad], axis=-1)


def _pad_lanes(vec, start):
    return jnp.pad(vec[None, :], ((0, 0), (start, LANES - start - vec.shape[0])))


def _heads_to_rows(block, lane0, nheads, nb, seq):
    return block[:, lane0:lane0 + nheads].reshape(nb, seq, nheads).transpose(0, 2, 1).reshape(nb * nheads, seq)


def _mixer_small(p, l):
    wq_t = jnp.tile(p["fox_q_norm"][l], FOX_HEADS)[None, :]
    wk_t = jnp.tile(p["fox_k_norm"][l], FOX_HEADS)[None, :]
    bias = _pad_lanes(p["fox_f_bias"][l], 0)
    a_pad = _pad_lanes(p["gdn_a_log"][l], A_LANE)
    dt_pad = _pad_lanes(p["gdn_dt_bias"][l], A_LANE)
    wn = p["gdn_out_norm"][l][None, :]
    return wq_t, wk_t, bias, a_pad, dt_pad, wn


def _layer_fwd(x, p, l, nb, seq, rider=None, ffn1_rider=None, after_ffn1=None):
    npair = FOX_HEADS // 2
    n = seq // CHUNK
    x1, h1, *rode1 = _ffn_fwd(x, p["ffn1_norm"][l][None, :], p["ffn1_w_in"][l], p["ffn1_w_out"][l],
                              f"ffn1_fwd_{l}", ffn1_rider)
    if after_ffn1 is not None:
        after_ffn1(rode1)
    wq_t, wk_t, bias, a_pad, dt_pad, wn = _mixer_small(p, l)
    proj = _norm_matmul(x1, p["mix_norm"][l][None, :], p["w_mix"][l], f"mix_in_{l}")
    fq, fk, fv, cum = _fox_prep(proj, wq_t, wk_t, bias, seq, f"fox_prep_{l}")
    c8 = _heads_to_rows(cum, 0, FOX_HEADS, nb, seq)
    ck = c8.reshape(nb * npair, 2, 1, seq)
    o, lse, *rode = _fox_attn(fq, fk, fv, ck, nb, seq, f"fox_attn_{l}", rider)
    gq, gk, gv, gates = _gdn_prep(proj, p["gdn_conv"][l], a_pad, dt_pad, seq, f"gdn_prep_{l}")
    gc4 = _heads_to_rows(gates, A_LANE, GDN_HEADS, nb, seq)
    grow = jnp.broadcast_to(gc4.reshape(nb * HEAD_GROUPS, HEADS_PER_STEP, n // 2, 1, PAIR),
                            (nb * HEAD_GROUPS, HEADS_PER_STEP, n // 2, HALO, PAIR))
    y, tinv, states = _gdn_fwd(gq, gk, gv, proj, gates, grow, wn, nb, seq, f"gdn_fwd_{l}")
    x2 = _mix_out(x1, o, y, p["w_out"][l], f"mix_out_{l}")
    x3, h2 = _ffn_fwd(x2, p["ffn2_norm"][l][None, :], p["ffn2_w_in"][l], p["ffn2_w_out"][l], f"ffn2_fwd_{l}")
    saved = dict(x=x, h1=h1, x1=x1, proj=proj, fq=fq, fk=fk, fv=fv, ck=ck, o=o, lse=lse,
                 gq=gq, gk=gk, gv=gv, gates=gates, grow=grow, tinv=tinv, states=states, y=y, x2=x2, h2=h2)
    return x3, saved, rode


def _ffn_grads(dy, x, h, gain, win, wout, l, tag, rider=None):
    t, d = x.shape
    fs = win.shape[2]
    dx, dh, a, hn, dyh, dgain, *rode = _ffn_bwd(dy, x, h, gain, win, wout, f"{tag}_bwd_{l}", rider)
    g_in = _wgrad(hn, dh, jax.ShapeDtypeStruct((4, d, fs), BF),
                  pl.BlockSpec((None, d, fs), lambda i, j, k: (j, i, 0)), d, fs, f"{tag}_gw_in_{l}")
    g_out = _wgrad(a, dyh, jax.ShapeDtypeStruct((2 * fs, d), BF),
                   pl.BlockSpec((fs, d), lambda i, j, k: (i, j)), fs, d, f"{tag}_gw_out_{l}")
    return dx, dgain[0], g_in, g_out.reshape(4, fs // 2, d), rode


def _layer_bwd(dx3, p, l, sv, nb, seq, rider=None, before_ffn1=None):
    npair = FOX_HEADS // 2
    d = dx3.shape[1]
    wq_t, wk_t, bias, a_pad, dt_pad, wn = _mixer_small(p, l)
    g = {}
    dx2, g["ffn2_norm"], g["ffn2_w_in"], g["ffn2_w_out"], _ = _ffn_grads(
        dx3, sv["x2"], sv["h2"], p["ffn2_norm"][l][None, :], p["ffn2_w_in"][l], p["ffn2_w_out"][l], l, "ffn2")
    dyf, dyg, dxb = _mix_out_bwd(dx2, p["w_out"][l], f"mix_out_bwd_{l}")
    half = lambda a, nm: _wgrad(a, dxb, jax.ShapeDtypeStruct((FOX_WIDTH, d), BF),
                                pl.BlockSpec((FOX_WIDTH, d), lambda i, j, k: (i, j)), FOX_WIDTH, d, nm)
    g["w_out"] = jnp.concatenate([half(sv["o"], f"gw_out_fox_{l}"), half(sv["y"], f"gw_out_gdn_{l}")], axis=0)
    dqa, dqb, dk, dv, dkx, *rode = _fox_attn_bwd(sv["fq"], sv["fk"], sv["fv"], sv["o"], dyf, sv["lse"], sv["ck"],
                                                 nb, seq, f"fox_attn_bwd_{l}", rider)

    dpf, dff, dwq, dwk, dbias = _fox_prep_bwd(sv["proj"], dqa, dqb, dk, dv, dkx, wq_t, wk_t, bias, seq,
                                              f"fox_prep_bwd_{l}")
    g["fox_q_norm"] = dwq[0, :FOX_HEAD_DIM]
    g["fox_k_norm"] = dwk[0, :FOX_HEAD_DIM]
    g["fox_f_bias"] = dbias[0, :FOX_HEADS]
    dgq, dgk, dgv, dgg, dgt, dwn = _gdn_bwd(
        sv["gq"], sv["gk"], sv["gv"], sv["proj"], sv["gates"], sv["grow"], wn, sv["tinv"], sv["states"],
        dyg, nb, seq, f"gdn_bwd_{l}")
    dgates = jnp.sum(dgt.reshape(nb, HEAD_GROUPS, seq, LANES), axis=1).reshape(nb * seq, LANES)
    dpg, dgate_blk, dconv, da, ddt = _gdn_prep_bwd(sv["proj"], dgq, dgk, dgv, dgates, dff, p["gdn_conv"][l],
                                                   a_pad, dt_pad, seq, f"gdn_prep_bwd_{l}")
    g["gdn_conv"] = dconv
    g["gdn_a_log"] = da[0, A_LANE:B_LANE]
    g["gdn_dt_bias"] = ddt[0, A_LANE:B_LANE]
    g["gdn_out_norm"] = dwn[0]
    dparts = [dpf, dpg, dgg, dgate_blk]
    dx1, hnm, dgm = _norm_matmul_bwd(dx2, dparts, sv["x1"], p["mix_norm"][l][None, :], p["w_mix"][l],
                                     f"mix_in_bwd_{l}")
    g["mix_norm"] = dgm[0]
    gf, gg_, go, gt = [_wgrad(hnm, a, jax.ShapeDtypeStruct((d, a.shape[1]), F32),
                              pl.BlockSpec((d // 2, a.shape[1]), lambda i, j, k: (i, j)), d // 2, a.shape[1],
                              f"gw_mix_{l}_{i}") for i, a in enumerate(dparts)]
    g["w_in"] = jnp.concatenate([gf, gt[:, 0:FOX_HEADS], gg_, gt[:, A_LANE:B_LANE + GDN_HEADS], go], axis=1)
    ffn1_rider = before_ffn1(g) if before_ffn1 is not None else None
    dx0, g["ffn1_norm"], g["ffn1_w_in"], g["ffn1_w_out"], rode1 = _ffn_grads(
        dx1, sv["x"], sv["h1"], p["ffn1_norm"][l][None, :], p["ffn1_w_in"][l], p["ffn1_w_out"][l], l, "ffn1",
        ffn1_rider)
    return dx0, g, rode, rode1


def _local_step(x, target, p):
    nb, seq, d = x.shape
    xt = x.reshape(nb * seq, d)
    saved = []
    for l in range(DEPTH):
        xt, sv, _ = _layer_fwd(xt, p, l, nb, seq)
        saved.append(sv)
    loss, dx = _loss_grad(xt, target.reshape(nb * seq, d), "loss")
    grads = [None] * DEPTH
    for l in reversed(range(DEPTH)):
        dx, grads[l], _, _ = _layer_bwd(dx, p, l, saved[l], nb, seq)
    return loss, dx.reshape(nb, seq, d), grads


N_CHIPS = 4


def _mesh_pos():
    return lax.axis_index("x"), lax.axis_index("y"), lax.axis_index("c")


def _other_chips(x, y):
    return [(1 - x, y), (x, 1 - y), (1 - x, 1 - y)]


def _remote(src, dst, send_sem, recv_sem, to):
    return pltpu.make_async_remote_copy(src_ref=src, dst_ref=dst, send_sem=send_sem, recv_sem=recv_sem,
                                        device_id=to, device_id_type=MESH)


def _hbm_call(body, name, ins, out_shape, scratch):
    return pl.pallas_call(
        body, name=name, out_shape=out_shape, in_specs=[HBM] * len(ins),
        out_specs=jax.tree.map(lambda _: HBM, out_shape), scratch_shapes=scratch,
        compiler_params=pltpu.CompilerParams(has_side_effects=True),
    )(*ins)


def _gather_phases(n, layer):
    def copies(ins, outs, sems):
        send1, recv1, send2, recv2 = sems
        x, y, c = _mesh_pos()
        out, back, fwd = [], [], []
        for i in range(n):
            for j, (px, py) in enumerate(_other_chips(x, y)):
                k = 3 * i + j
                blk = outs[i].at[2 * px + py]
                out.append(_remote(ins[i], outs[i].at[2 * x + y], send1.at[k], recv1.at[k], (px, py, c)))
                back.append(_remote(blk, blk, send1.at[k], recv1.at[k], (px, py, c)))
                fwd.append(_remote(blk, blk, send2.at[k], recv2.at[k], (x, y, 1 - c)))
        return c, out, back, fwd

    def first(ins, outs, sems):
        c, out, _, _ = copies(ins, outs, sems)

        @pl.when(c == layer)
        def _():
            for cp in out:
                cp.start()

    def middle(ins, outs, sems):
        c, _, back, fwd = copies(ins, outs, sems)

        @pl.when(c == layer)
        def _():
            for arrived, onward in zip(back, fwd):
                arrived.wait_recv()
                onward.start()

    def last(ins, outs, sems):
        c, out, _, fwd = copies(ins, outs, sems)

        @pl.when(c == layer)
        def _():
            for cp in out + fwd:
                cp.wait_send()

        @pl.when(c != layer)
        def _():
            for cp in fwd:
                cp.wait_recv()

    return first, middle, last


def _scatter_phases(n, layer):
    def copies(ins, outs, sems):
        send, recv = sems
        x, y, c = _mesh_pos()
        return c, [_remote(ins[i].at[2 * px + py], outs[i].at[j], send.at[3 * i + j], recv.at[3 * i + j], (px, py, c))
                   for i in range(n) for j, (px, py) in enumerate(_other_chips(x, y))]

    def first(ins, outs, sems):
        c, cps = copies(ins, outs, sems)

        @pl.when(c == layer)
        def _():
            for cp in cps:
                cp.start()

    def middle(ins, outs, sems):
        pass

    def last(ins, outs, sems):
        c, cps = copies(ins, outs, sems)

        @pl.when(c == layer)
        def _():
            for cp in cps:
                cp.wait()

    return first, middle, last


def _exchange(blocks, out_shapes, n_sems, phases, name, rider):
    sems = [pltpu.SemaphoreType.DMA((3 * len(blocks),))] * n_sems
    if rider:
        return _Rider(blocks, out_shapes, sems, phases)
    n = len(blocks)

    def body(*refs):
        for phase in phases:
            phase(refs[:n], refs[n:2 * n], refs[2 * n:])

    return list(_hbm_call(body, name, blocks, out_shapes, sems))


def _gather_layer(blocks, layer, name=None, rider=False):
    outs = [jax.ShapeDtypeStruct((N_CHIPS,) + b.shape, b.dtype) for b in blocks]
    return _exchange(blocks, outs, 4, _gather_phases(len(blocks), layer), name, rider)


def _scatter_layer(sums, layer, name=None, rider=False):
    outs = [jax.ShapeDtypeStruct((3,) + s.shape[1:], s.dtype) for s in sums]
    return _exchange(sums, outs, 2, _scatter_phases(len(sums), layer), name, rider)


def _to_sibling(gs, layer, name):
    n = len(gs)

    def body(*refs):
        ins, outs = refs[:n], refs[n:2 * n]
        send, recv = refs[2 * n:]
        x, y, c = _mesh_pos()
        cps = [_remote(ins[i], outs[i], send.at[i], recv.at[i], (x, y, 1 - c)) for i in range(n)]

        @pl.when(c != layer)
        def _():
            for cp in cps:
                cp.start()
            for cp in cps:
                cp.wait_send()

        @pl.when(c == layer)
        def _():
            for cp in cps:
                cp.wait_recv()

    sem = pltpu.SemaphoreType.DMA((n,))
    return list(_hbm_call(body, name, gs, [jax.ShapeDtypeStruct(g.shape, g.dtype) for g in gs], [sem, sem]))


def _sibling_swap(rs, name):
    n = len(rs)

    def body(*refs):
        ins, outs = refs[:n], refs[n:2 * n]
        send, recv = refs[2 * n:]
        x, y, c = _mesh_pos()
        cps = [_remote(ins[i], outs[i], send.at[i], recv.at[i], (x, y, 1 - c)) for i in range(n)]
        for cp in cps:
            cp.start()
        for cp in cps:
            cp.wait()

    sem = pltpu.SemaphoreType.DMA((n,))
    return _hbm_call(body, name, rs, [jax.ShapeDtypeStruct(r.shape, r.dtype) for r in rs], [sem, sem])


def _small_all_reduce(vec, name):
    r = vec.shape[0]
    ndev = 8

    def body(v_ref, o_ref, buf, send, recv):
        x, y, c = _mesh_pos()
        me = 4 * x + 2 * y + c
        buf[me] = v_ref[...]
        cps = []
        for rel in range(1, ndev):
            px = 1 - x if rel & 4 else x
            py = 1 - y if rel & 2 else y
            pc = 1 - c if rel & 1 else c
            cps.append((_remote(v_ref, buf.at[me], send.at[rel - 1], recv.at[rel - 1], (px, py, pc)),
                        4 * px + 2 * py + pc))
        for cp, _ in cps:
            cp.start()
        for k, (cp, peer) in enumerate(cps):
            slot = buf.at[peer]
            _remote(slot, slot, send.at[k], recv.at[k], (x, y, c)).wait_recv()
        for cp, _ in cps:
            cp.wait_send()
        acc = buf[0]
        for k in range(1, ndev):
            acc = acc + buf[k]
        o_ref[...] = acc

    vm = pl.BlockSpec(memory_space=pltpu.VMEM)
    return pl.pallas_call(
        body, name=name, out_shape=jax.ShapeDtypeStruct(vec.shape, F32), in_specs=[vm], out_specs=vm,
        scratch_shapes=[pltpu.VMEM((ndev, r, LANES), F32), pltpu.SemaphoreType.DMA((ndev - 1,)),
                        pltpu.SemaphoreType.DMA((ndev - 1,))],
        compiler_params=pltpu.CompilerParams(has_side_effects=True),
    )(vec)


def _row_tile(rows, cap=512):
    for t in range(min(rows, cap), 0, -1):
        if rows % t == 0 and (t % 16 == 0 or t == rows):
            return t
    raise ValueError(rows)


def _add_pairs(a, b, name):
    k, r, c = a.shape
    tr = _row_tile(r)

    def body(a_ref, b_ref, o_ref):
        o_ref[...] = (a_ref[...].astype(F32) + b_ref[...].astype(F32)).astype(o_ref.dtype)

    spec = pl.BlockSpec((None, tr, c), lambda i, j: (i, j, 0))
    return pl.pallas_call(body, name=name, grid=(k, r // tr), in_specs=[spec, spec], out_specs=spec,
                          out_shape=jax.ShapeDtypeStruct(a.shape, a.dtype),
                          compiler_params=_params(("parallel", "parallel")))(a, b)


def _final_sum(own, sib, others, name):
    r, c = own.shape
    tr = _row_tile(r)

    def body(a_ref, b_ref, o_ref_in, out_ref):
        acc = a_ref[...].astype(F32) + b_ref[...].astype(F32)
        for k in range(3):
            acc = acc + o_ref_in[k].astype(F32)
        out_ref[...] = acc

    spec = pl.BlockSpec((tr, c), lambda i: (i, 0))
    return pl.pallas_call(body, name=name, grid=(r // tr,),
                          in_specs=[spec, spec, pl.BlockSpec((3, tr, c), lambda i: (0, i, 0))], out_specs=spec,
                          out_shape=jax.ShapeDtypeStruct((r, c), F32),
                          compiler_params=_params(("parallel",)))(own, sib, others)


def _adamw(g, w, m, v, name):
    r, c = g.shape
    tr = _row_tile(r, 256)

    def body(g_ref, w_ref, m_ref, v_ref, d_ref, mo_ref, vo_ref):
        gv = g_ref[...]
        mn = ADAM_B1 * m_ref[...] + (1.0 - ADAM_B1) * gv
        vn = ADAM_B2 * v_ref[...] + (1.0 - ADAM_B2) * (gv * gv)
        m_hat = mn / (1.0 - ADAM_B1 ** ADAM_STEP)
        v_hat = vn / (1.0 - ADAM_B2 ** ADAM_STEP)
        d_ref[...] = -ADAM_LR * (m_hat / (jnp.sqrt(v_hat) + ADAM_EPS) + ADAM_WD * w_ref[...])
        mo_ref[...] = mn
        vo_ref[...] = vn

    spec = pl.BlockSpec((tr, c), lambda i: (i, 0))
    shp = jax.ShapeDtypeStruct((r, c), F32)
    return pl.pallas_call(body, name=name, grid=(r // tr,), in_specs=[spec] * 4, out_specs=[spec] * 3,
                          out_shape=[shp] * 3, compiler_params=_params(("parallel",)))(g, w, m, v)


def _pack(arrays):
    flat = jnp.concatenate([a.reshape(-1).astype(F32) for a in arrays])
    pad = (-flat.shape[0]) % (8 * LANES)
    return jnp.concatenate([flat, jnp.zeros((pad,), F32)]).reshape(-1, LANES)


def _unpack(packed, shapes):
    flat = packed.reshape(-1)
    out, off = [], 0
    for s in shapes:
        size = 1
        for dim in s:
            size *= dim
        out.append(flat[off:off + size].reshape(s))
        off += size
    return out


BIG = ("ffn1_w_in", "ffn1_w_out", "w_in", "w_out", "ffn2_w_in", "ffn2_w_out")
SMALL = ("ffn1_norm", "mix_norm", "fox_q_norm", "fox_k_norm", "fox_f_bias", "gdn_a_log", "gdn_dt_bias",
         "gdn_out_norm", "ffn2_norm", "gdn_conv")
WEIGHTS = ("ffn1_norm", "ffn1_w_in", "ffn1_w_out", "mix_norm", "w_in", "fox_q_norm", "fox_k_norm", "fox_f_bias",
           "gdn_conv", "gdn_a_log", "gdn_dt_bias", "gdn_out_norm", "w_out", "ffn2_norm", "ffn2_w_in", "ffn2_w_out")


def _step(x, target, w, m, v):
    xi, yi, ci = _mesh_pos()
    me = 2 * xi + yi
    depth = DEPTH
    d = x.shape[-1]

    nb, seq, _ = x.shape
    assert depth == 2

    p = {k: w[k] for k in SMALL if k != "gdn_conv"}
    for k in ("ffn1_w_in", "ffn1_w_out", "ffn2_w_in", "ffn2_w_out", "w_mix", "w_out", "gdn_conv"):
        p[k] = [None] * depth

    first, rest = BIG[:2], BIG[2:] + ("gdn_conv",)

    def shards(l, names):
        return [w[k][l] if k == "gdn_conv" else w[k][l].astype(BF) for k in names]

    def place(l, names, gathered):
        blocks = dict(zip(names, [lax.dynamic_update_index_in_dim(g, s, me, 0)
                                  for g, s in zip(gathered, shards(l, names))]))
        for k in ("ffn1_w_in", "ffn1_w_out", "ffn2_w_in", "ffn2_w_out"):
            if k in blocks:
                p[k][l] = blocks[k]
        if "w_in" in blocks:
            p["w_mix"][l] = _mix_to_padded(blocks["w_in"].transpose(1, 0, 2).reshape(d, N_IN))
            p["w_out"][l] = blocks["w_out"].reshape(2 * FOX_WIDTH, d)
            p["gdn_conv"][l] = blocks["gdn_conv"].transpose(1, 0, 2).reshape(CONV_WIDTH, -1)

    place(0, first, _gather_layer(shards(0, first), 0, "gather_first_ffn0"))
    xt = x.reshape(nb * seq, d)
    xt, saved0, gathered1 = _layer_fwd(
        xt, p, 0, nb, seq, _gather_layer(shards(1, first + rest), 1, rider=True),
        _gather_layer(shards(0, rest), 0, rider=True), lambda got: place(0, rest, got))
    place(1, first + rest, gathered1)
    xt, saved1, _ = _layer_fwd(xt, p, 1, nb, seq)
    loss, dx = _loss_grad(xt, target.reshape(nb * seq, d), "loss")

    def transport(g, names):
        out = []
        for k in names:
            if k == "w_in":
                out.append(g["w_in"].reshape(d, N_CHIPS, N_IN // N_CHIPS).transpose(1, 0, 2).astype(BF))
            elif k == "w_out":
                out.append(g["w_out"].reshape(N_CHIPS, -1, d))
            else:
                out.append(g[k])
        return out

    def chip_sums(g, l, names, tag):
        own = transport(g, names)
        sib = _to_sibling(own, l, f"grad{l}{tag}_to_sibling")
        return own, sib, [_add_pairs(a, b, f"grad{l}{tag}_chip_sum_{k}") for a, b, k in zip(own, sib, names)]

    dx, grads1, _, _ = _layer_bwd(dx, p, 1, saved1, nb, seq)
    own1, sib1, sums1 = chip_sums(grads1, 1, BIG, "")
    before = {}

    def before_ffn1(g):
        before["own"], before["sib"], sums = chip_sums(g, 0, BIG[2:], "_rest")
        return _scatter_layer(sums, 0, rider=True)

    dx, grads0, chips1, chips0_rest = _layer_bwd(dx, p, 0, saved0, nb, seq,
                                                 _scatter_layer(sums1, 1, rider=True), before_ffn1)
    own0, sib0, sums0 = chip_sums(grads0, 0, first, "_first")
    chips0 = _scatter_layer(sums0, 0, "grad0_first_to_chips") + chips0_rest
    own0, sib0 = own0 + before["own"], sib0 + before["sib"]
    grads = [grads0, grads1]
    dx = dx.reshape(nb, seq, d)

    mine = lambda a0, a1: jnp.where(ci == 0, a0, a1)
    at_me = lambda a: lax.dynamic_index_in_dim(a, me, 0, keepdims=False)
    reduced = [_final_sum(mine(at_me(own0[i]), at_me(own1[i])), mine(at_me(sib0[i]), at_me(sib1[i])),
                          mine(chips0[i], chips1[i]), f"grad_final_sum_{k}") for i, k in enumerate(BIG)]
    from_sib_final = _sibling_swap(reduced, "grad_swap_layers")
    full = {k: jnp.stack([jnp.where(ci == 0, a, b), jnp.where(ci == 0, b, a)])
            for k, a, b in zip(BIG, reduced, from_sib_final)}

    out_g, out_d, out_m, out_v = {}, {}, {}, {}
    for k in BIG:
        shp = w[k].shape
        two_d = lambda a: a.reshape(shp[0] * shp[1], shp[2])
        dl, mn, vn = _adamw(two_d(full[k]), two_d(w[k]), two_d(m[k]), two_d(v[k]), f"adamw_{k}")
        out_g[k], out_d[k], out_m[k], out_v[k] = full[k], dl.reshape(shp), mn.reshape(shp), vn.reshape(shp)

    small_local = [jnp.stack([grads[l][k] for l in range(depth)]) for k in SMALL]
    summed = _unpack(_small_all_reduce(_pack(small_local), "small_all_reduce"), [a.shape for a in small_local])
    sg = dict(zip(SMALL, summed))
    cs = w["gdn_conv"].shape[-1]
    sg["gdn_conv"] = lax.dynamic_slice_in_dim(sg["gdn_conv"], me * cs, cs, axis=2)
    shapes = [w[k].shape for k in SMALL]
    packs = [_pack([src[k] for k in SMALL]) for src in (sg, w, m, v)]
    dl, mn, vn = _adamw(*packs, "adamw_small")
    for k, a, b, c2 in zip(SMALL, _unpack(dl, shapes), _unpack(mn, shapes), _unpack(vn, shapes)):
        out_g[k], out_d[k], out_m[k], out_v[k] = sg[k], a, b, c2

    total = lax.psum(loss[0, 0], ("x", "y", "c"))
    return (total, dx, *[out_g[k] for k in WEIGHTS], *[out_d[k] for k in WEIGHTS],
            *[out_m[k] for k in WEIGHTS], *[out_v[k] for k in WEIGHTS])


def kernel(x, ffn1_norm, ffn1_w_in, ffn1_w_out, mix_norm, w_in, fox_q_norm, fox_k_norm, fox_f_bias, gdn_conv, gdn_a_log, gdn_dt_bias, gdn_out_norm, w_out, ffn2_norm, ffn2_w_in, ffn2_w_out, loss_target, m_ffn1_norm, m_ffn1_w_in, m_ffn1_w_out, m_mix_norm, m_w_in, m_fox_q_norm, m_fox_k_norm, m_fox_f_bias, m_gdn_conv, m_gdn_a_log, m_gdn_dt_bias, m_gdn_out_norm, m_w_out, m_ffn2_norm, m_ffn2_w_in, m_ffn2_w_out, v_ffn1_norm, v_ffn1_w_in, v_ffn1_w_out, v_mix_norm, v_w_in, v_fox_q_norm, v_fox_k_norm, v_fox_f_bias, v_gdn_conv, v_gdn_a_log, v_gdn_dt_bias, v_gdn_out_norm, v_w_out, v_ffn2_norm, v_ffn2_w_in, v_ffn2_w_out):
    w = dict(ffn1_norm=ffn1_norm, ffn1_w_in=ffn1_w_in, ffn1_w_out=ffn1_w_out, mix_norm=mix_norm, w_in=w_in,
             fox_q_norm=fox_q_norm, fox_k_norm=fox_k_norm, fox_f_bias=fox_f_bias, gdn_conv=gdn_conv,
             gdn_a_log=gdn_a_log, gdn_dt_bias=gdn_dt_bias, gdn_out_norm=gdn_out_norm, w_out=w_out,
             ffn2_norm=ffn2_norm, ffn2_w_in=ffn2_w_in, ffn2_w_out=ffn2_w_out)
    m = dict(ffn1_norm=m_ffn1_norm, ffn1_w_in=m_ffn1_w_in, ffn1_w_out=m_ffn1_w_out, mix_norm=m_mix_norm, w_in=m_w_in,
             fox_q_norm=m_fox_q_norm, fox_k_norm=m_fox_k_norm, fox_f_bias=m_fox_f_bias, gdn_conv=m_gdn_conv,
             gdn_a_log=m_gdn_a_log, gdn_dt_bias=m_gdn_dt_bias, gdn_out_norm=m_gdn_out_norm, w_out=m_w_out,
             ffn2_norm=m_ffn2_norm, ffn2_w_in=m_ffn2_w_in, ffn2_w_out=m_ffn2_w_out)
    v = dict(ffn1_norm=v_ffn1_norm, ffn1_w_in=v_ffn1_w_in, ffn1_w_out=v_ffn1_w_out, mix_norm=v_mix_norm, w_in=v_w_in,
             fox_q_norm=v_fox_q_norm, fox_k_norm=v_fox_k_norm, fox_f_bias=v_fox_f_bias, gdn_conv=v_gdn_conv,
             gdn_a_log=v_gdn_a_log, gdn_dt_bias=v_gdn_dt_bias, gdn_out_norm=v_gdn_out_norm, w_out=v_w_out,
             ffn2_norm=v_ffn2_norm, ffn2_w_in=v_ffn2_w_in, ffn2_w_out=v_ffn2_w_out)
    return _step(x, loss_target, w, m, v)
```

```python
import jax
import jax.numpy as jnp
from jax import lax
from jax.experimental import pallas as pl
from jax.experimental.pallas import tpu as pltpu

F32 = jnp.float32
BF = jnp.bfloat16
HI = lax.Precision.HIGHEST
MESH = pl.DeviceIdType.MESH

DEPTH = 2
FOX_HEADS = 8
FOX_HEAD_DIM = 64
FOX_WIDTH = 512
GDN_HEADS = 4
GDN_HEAD_DIM = 128
GDN_WIDTH = 512
CONV_WIDTH = 4
CHUNK = 64
EPS = 1e-6
N_IN = 3600
N_PAD = 3712
GATE_COL = 3584
LANES = 128
NEG = -1e30

ADAM_LR = 0.001
ADAM_B1 = 0.9
ADAM_B2 = 0.999
ADAM_EPS = 1e-08
ADAM_WD = 0.01
ADAM_STEP = 10

VMEM_LIMIT = 56 * 1024 * 1024


def _params(sem=None, **kw):
    return pltpu.CompilerParams(dimension_semantics=sem, vmem_limit_bytes=VMEM_LIMIT, **kw)


def _dot(a, b, precision=None):
    return jnp.dot(a, b, preferred_element_type=F32, precision=precision)


def _dot_nt(a, b, precision=None):
    return lax.dot_general(a, b, (((1,), (1,)), ((), ())), preferred_element_type=F32, precision=precision)


def _dot_tn(a, b, precision=None):
    return lax.dot_general(a, b, (((0,), (0,)), ((), ())), preferred_element_type=F32, precision=precision)


def _sigmoid(x):
    return 0.5 * jnp.tanh(0.5 * x) + 0.5


def _softplus(x):
    return jnp.maximum(x, 0.0) + jnp.log(1.0 + jnp.exp(-jnp.abs(x)))


def _log_sigmoid(x):
    return jnp.minimum(x, 0.0) - jnp.log(1.0 + jnp.exp(-jnp.abs(x)))


def _tile(n, t):
    t = min(n, t)
    assert n % t == 0, (n, t)
    return t


def _rms_fwd(x, gain):
    rstd = lax.rsqrt(jnp.mean(x * x, axis=-1, keepdims=True) + EPS)
    xhat = x * rstd
    return xhat * gain, xhat, rstd


def _rms_bwd(dy, xhat, rstd, gain):
    dxhat = dy * gain
    dx = rstd * (dxhat - xhat * jnp.mean(dxhat * xhat, axis=-1, keepdims=True))
    return dx, dy * xhat


def _full(shape):
    nd = len(shape)
    return pl.BlockSpec(shape, lambda *_: (0,) * nd)


HBM = pl.BlockSpec(memory_space=pltpu.HBM)


def _load_ffn_weights(win_hbm, wout_hbm, win_v, wout_v, sem):
    fr = wout_hbm.shape[1]
    copies = [pltpu.make_async_copy(win_hbm.at[s], win_v.at[s], sem.at[s]) for s in range(4)]
    copies += [pltpu.make_async_copy(wout_hbm.at[s], wout_v.at[pl.ds(s * fr, fr)], sem.at[4 + s])
               for s in range(4)]
    for c in copies:
        c.start()
    for c in copies:
        c.wait()


def _ffn_fwd(x, gain, win_g, wout_g, name, rider=None):
    t, d = x.shape
    _, _, fs = win_g.shape
    fr = wout_g.shape[1]
    tm = _tile(t, 512)
    r_in, r_out, r_sem = _rider_parts(rider)
    steps = t // tm

    def body(x_ref, g_ref, win_hbm, wout_hbm, *rest):
        rin, (xo_ref, h_ref) = rest[:len(r_in)], rest[len(r_in):len(r_in) + 2]
        rout = rest[len(r_in) + 2:len(r_in) + 2 + len(r_out)]
        win_v, wout_v, sem = rest[len(r_in) + 2 + len(r_out):len(r_in) + 5 + len(r_out)]
        riding = (rin, rout, rest[len(r_in) + 5 + len(r_out):])
        step = pl.program_id(0)
        _ride(rider, 0, step == 0, riding)
        _ride(rider, 1, step == steps // 2, riding)

        @pl.when(step == 0)
        def _():
            _load_ffn_weights(win_hbm, wout_hbm, win_v, wout_v, sem)

        xv = x_ref[...]
        hn, _, _ = _rms_fwd(xv, g_ref[...])
        hn = hn.astype(BF)
        acc = jnp.zeros((tm, d), F32)
        for s in range(2):
            g = _dot(hn, win_v[s])
            u = _dot(hn, win_v[s + 2])
            h_ref[:, s * fs:(s + 1) * fs] = g.astype(BF)
            h_ref[:, (s + 2) * fs:(s + 3) * fs] = u.astype(BF)
            a = (g * _sigmoid(g) * u).astype(BF)
            acc = acc + _dot(a, wout_v[s * fs:(s + 1) * fs, :])
        xo_ref[...] = xv + 0.5 * acc
        _ride(rider, 2, step == steps - 1, riding)

    return pl.pallas_call(
        body, name=name, grid=(steps,),
        in_specs=[pl.BlockSpec((tm, d), lambda i: (i, 0)), _full((1, d)), HBM, HBM] + [HBM] * len(r_in),
        out_specs=[pl.BlockSpec((tm, d), lambda i: (i, 0)), pl.BlockSpec((tm, 4 * fs), lambda i: (i, 0))]
        + [HBM] * len(r_out),
        out_shape=[jax.ShapeDtypeStruct((t, d), F32), jax.ShapeDtypeStruct((t, 4 * fs), BF)] + r_out,
        scratch_shapes=[pltpu.VMEM((4, d, fs), BF), pltpu.VMEM((4 * fr, d), BF), pltpu.SemaphoreType.DMA((8,))]
        + r_sem,
        compiler_params=_params(("arbitrary",), has_side_effects=rider is not None),
    )(x, gain, win_g, wout_g, *r_in)


def _ffn_bwd(dy, x, h, gain, win_g, wout_g, name, rider=None):
    t, d = x.shape
    _, _, fs = win_g.shape
    fr = wout_g.shape[1]
    tm = _tile(t, 256)
    r_in, r_out, r_sem = _rider_parts(rider)
    steps = t // tm

    def body(dy_ref, x_ref, h_ref, g_ref, win_hbm, wout_hbm, *rest):
        rin, (dx_ref, dh_ref, a_ref, hn_ref, dyh_ref, dg_ref) = rest[:len(r_in)], rest[len(r_in):len(r_in) + 6]
        rout = rest[len(r_in) + 6:len(r_in) + 6 + len(r_out)]
        win_v, wout_v, sem = rest[len(r_in) + 6 + len(r_out):len(r_in) + 9 + len(r_out)]
        riding = (rin, rout, rest[len(r_in) + 9 + len(r_out):])
        step = pl.program_id(0)
        _ride(rider, 0, step == 0, riding)
        _ride(rider, 1, step == steps // 2, riding)

        @pl.when(step == 0)
        def _():
            _load_ffn_weights(win_hbm, wout_hbm, win_v, wout_v, sem)
            dg_ref[...] = jnp.zeros_like(dg_ref)

        dyv = dy_ref[...]
        dyh = (0.5 * dyv).astype(BF)
        dyh_ref[...] = dyh
        dhn = jnp.zeros((tm, d), F32)
        for s in range(2):
            da = _dot_nt(dyh, wout_v[s * fs:(s + 1) * fs, :])
            g = h_ref[:, s * fs:(s + 1) * fs].astype(F32)
            u = h_ref[:, (s + 2) * fs:(s + 3) * fs].astype(F32)
            sg = _sigmoid(g)
            si = g * sg
            a_ref[:, s * fs:(s + 1) * fs] = (si * u).astype(BF)
            dgate = (da * u * (sg * (1.0 + g * (1.0 - sg)))).astype(BF)
            dup = (da * si).astype(BF)
            dh_ref[:, s * fs:(s + 1) * fs] = dgate
            dh_ref[:, (s + 2) * fs:(s + 3) * fs] = dup
            dhn = dhn + _dot_nt(dgate, win_v[s]) + _dot_nt(dup, win_v[s + 2])
        xv = x_ref[...]
        gain_v = g_ref[...]
        hn, xhat, rstd = _rms_fwd(xv, gain_v)
        hn_ref[...] = hn.astype(BF)
        dx, dgr = _rms_bwd(dhn, xhat, rstd, gain_v)
        dx_ref[...] = dyv + dx
        dg_ref[...] += jnp.sum(dgr, axis=0, keepdims=True)
        _ride(rider, 2, step == steps - 1, riding)

    row = lambda w: pl.BlockSpec((tm, w), lambda i: (i, 0))
    return pl.pallas_call(
        body, name=name, grid=(steps,),
        in_specs=[row(d), row(d), row(4 * fs), _full((1, d)), HBM, HBM] + [HBM] * len(r_in),
        out_specs=[row(d), row(4 * fs), row(2 * fs), row(d), row(d), _full((1, d))] + [HBM] * len(r_out),
        out_shape=[jax.ShapeDtypeStruct((t, d), F32), jax.ShapeDtypeStruct((t, 4 * fs), BF),
                   jax.ShapeDtypeStruct((t, 2 * fs), BF), jax.ShapeDtypeStruct((t, d), BF),
                   jax.ShapeDtypeStruct((t, d), BF), jax.ShapeDtypeStruct((1, d), F32)] + r_out,
        scratch_shapes=[pltpu.VMEM((4, d, fs), BF), pltpu.VMEM((4 * fr, d), BF), pltpu.SemaphoreType.DMA((8,))]
        + r_sem,
        compiler_params=_params(("arbitrary",), has_side_effects=rider is not None),
    )(dy, x, h, gain, win_g, wout_g, *r_in)


def _wgrad(a, b, out_shape, out_spec, tm, tn, name, tk=512):
    t, m = a.shape
    _, n = b.shape
    tk = _tile(t, tk)
    nk = t // tk

    def body(a_ref, b_ref, o_ref, acc):
        k = pl.program_id(2)

        @pl.when(k == 0)
        def _():
            acc[...] = jnp.zeros_like(acc)

        acc[...] += _dot_tn(a_ref[...], b_ref[...])

        @pl.when(k == nk - 1)
        def _():
            o_ref[...] = acc[...].astype(o_ref.dtype)

    return pl.pallas_call(
        body, name=name, grid=(m // tm, n // tn, nk),
        in_specs=[pl.BlockSpec((tk, tm), lambda i, j, k: (k, i)), pl.BlockSpec((tk, tn), lambda i, j, k: (k, j))],
        out_specs=out_spec, out_shape=out_shape,
        scratch_shapes=[pltpu.VMEM((tm, tn), F32)],
        compiler_params=_params(("parallel", "parallel", "arbitrary")),
    )(a, b)


def _norm_matmul(x, gain, w, name):
    t, d = x.shape
    n = w.shape[1]
    tm = _tile(t, 256)

    def body(x_ref, g_ref, w_ref, o_ref):
        hn, _, _ = _rms_fwd(x_ref[...], g_ref[...])
        o_ref[...] = _dot(hn.astype(BF), w_ref[...])

    return pl.pallas_call(
        body, name=name, grid=(t // tm,),
        in_specs=[pl.BlockSpec((tm, d), lambda i: (i, 0)), _full((1, d)), _full((d, n))],
        out_specs=pl.BlockSpec((tm, n), lambda i: (i, 0)),
        out_shape=jax.ShapeDtypeStruct((t, n), F32),
        compiler_params=_params(("parallel",)),
    )(x, gain, w)


def _norm_matmul_bwd(dres, dparts, x, gain, w, name):
    t, d = x.shape
    n = w.shape[1]
    tm = _tile(t, 256)
    widths = [a.shape[1] for a in dparts]
    assert sum(widths) == n
    k = len(dparts)

    def body(dr_ref, *rest):
        dp_refs, (x_ref, g_ref, w_ref, dx_ref, hn_ref, dg_ref) = rest[:k], rest[k:]

        @pl.when(pl.program_id(0) == 0)
        def _():
            dg_ref[...] = jnp.zeros_like(dg_ref)

        dhn, off = jnp.zeros((tm, d), F32), 0
        for dp_ref, wd in zip(dp_refs, widths):
            dhn = dhn + _dot_nt(dp_ref[...], w_ref[:, off:off + wd])
            off += wd
        gain_v = g_ref[...]
        hn, xhat, rstd = _rms_fwd(x_ref[...], gain_v)
        hn_ref[...] = hn.astype(BF)
        dx, dgr = _rms_bwd(dhn, xhat, rstd, gain_v)
        dx_ref[...] = dr_ref[...] + dx
        dg_ref[...] += jnp.sum(dgr, axis=0, keepdims=True)

    row = lambda wd: pl.BlockSpec((tm, wd), lambda i: (i, 0))
    return pl.pallas_call(
        body, name=name, grid=(t // tm,),
        in_specs=[row(d)] + [row(wd) for wd in widths] + [row(d), _full((1, d)), _full((d, n))],
        out_specs=[row(d), row(d), _full((1, d))],
        out_shape=[jax.ShapeDtypeStruct((t, d), F32), jax.ShapeDtypeStruct((t, d), BF),
                   jax.ShapeDtypeStruct((1, d), F32)],
        compiler_params=_params(("arbitrary",)),
    )(dres, *dparts, x, gain, w)


def _mix_out(x, yf, yg, w, name):
    t, d = x.shape
    kf = yf.shape[1]
    tm = _tile(t, 512)

    def body(x_ref, yf_ref, yg_ref, w_ref, o_ref):
        o_ref[...] = x_ref[...] + _dot(yf_ref[...], w_ref[0:kf, :]) + _dot(yg_ref[...], w_ref[kf:2 * kf, :])

    row = lambda wd: pl.BlockSpec((tm, wd), lambda i: (i, 0))
    return pl.pallas_call(
        body, name=name, grid=(t // tm,),
        in_specs=[row(d), row(kf), row(kf), _full((2 * kf, d))],
        out_specs=row(d), out_shape=jax.ShapeDtypeStruct((t, d), F32),
        compiler_params=_params(("parallel",)),
    )(x, yf, yg, w)


def _mix_out_bwd(dx, w, name):
    t, d = dx.shape
    kf = w.shape[0] // 2
    tm = _tile(t, 512)

    def body(dx_ref, w_ref, df_ref, dg_ref, dxb_ref):
        dxb = dx_ref[...].astype(BF)
        dxb_ref[...] = dxb
        df_ref[...] = _dot_nt(dxb, w_ref[0:kf, :]).astype(BF)
        dg_ref[...] = _dot_nt(dxb, w_ref[kf:2 * kf, :]).astype(BF)

    row = lambda wd: pl.BlockSpec((tm, wd), lambda i: (i, 0))
    return pl.pallas_call(
        body, name=name, grid=(t // tm,),
        in_specs=[row(d), _full((2 * kf, d))],
        out_specs=[row(kf), row(kf), row(d)],
        out_shape=[jax.ShapeDtypeStruct((t, kf), BF), jax.ShapeDtypeStruct((t, kf), BF),
                   jax.ShapeDtypeStruct((t, d), BF)],
        compiler_params=_params(("parallel",)),
    )(dx, w)


def _loss_grad(y, target, name):
    t, d = y.shape
    tm = _tile(t, 512)

    def body(y_ref, t_ref, l_ref, dy_ref):
        @pl.when(pl.program_id(0) == 0)
        def _():
            l_ref[...] = jnp.zeros_like(l_ref)

        diff = y_ref[...] - t_ref[...]
        dy_ref[...] = diff * (1.0 / d)
        part = jnp.sum(jnp.sum(diff * diff, axis=1, keepdims=True), axis=0, keepdims=True)
        l_ref[...] += part * (0.5 / d)

    row = pl.BlockSpec((tm, d), lambda i: (i, 0))
    return pl.pallas_call(
        body, name=name, grid=(t // tm,),
        in_specs=[row, row], out_specs=[_full((1, 1)), row],
        out_shape=[jax.ShapeDtypeStruct((1, 1), F32), jax.ShapeDtypeStruct((t, d), F32)],
        compiler_params=_params(("arbitrary",)),
    )(y, target)


def _head_sum_matrix(width, head):
    r = lax.broadcasted_iota(jnp.int32, (width, width), 0) // head
    c = lax.broadcasted_iota(jnp.int32, (width, width), 1) // head
    return (r == c).astype(BF)


def _head_mean(x, bd):
    return _dot(x.astype(BF), bd) * (1.0 / FOX_HEAD_DIM)


def _mask_dot(mask01, x):
    mb = mask01.astype(BF)
    hi = x.astype(BF)
    r1 = x - hi.astype(F32)
    mid = r1.astype(BF)
    lo = (r1 - mid.astype(F32)).astype(BF)
    return _dot(mb, hi) + _dot(mb, mid) + _dot(mb, lo)


def _fox_prep(proj, wq_t, wk_t, bias_pad, seq, name):
    t = proj.shape[0]
    ts = _tile(seq, 512)
    tpe = seq // ts
    scale = FOX_HEAD_DIM ** -0.5

    def body(q_ref, k_ref, v_ref, gt_ref, wq_ref, wk_ref, b_ref, qo_ref, ko_ref, vo_ref, cum_ref, carry):
        i = pl.program_id(0)
        bd = _head_sum_matrix(FOX_WIDTH, FOX_HEAD_DIM)

        def norm(xv, wv):
            ms = _head_mean(xv * xv, bd)
            return xv * lax.rsqrt(ms + EPS) * wv

        qo_ref[...] = (norm(q_ref[...], wq_ref[...]) * scale).astype(BF)
        ko_ref[...] = norm(k_ref[...], wk_ref[...]).astype(BF)
        vo_ref[...] = v_ref[...].astype(BF)

        @pl.when(i % tpe == 0)
        def _():
            carry[...] = jnp.zeros_like(carry)

        ls = _log_sigmoid(gt_ref[...] + b_ref[...])
        r = lax.broadcasted_iota(jnp.int32, (ts, ts), 0)
        c = lax.broadcasted_iota(jnp.int32, (ts, ts), 1)
        cum = _mask_dot(r >= c, ls) + carry[...]
        cum_ref[...] = cum
        carry[...] = cum[ts - 1:ts, :]

    blk = lambda j: pl.BlockSpec((ts, FOX_WIDTH), lambda i: (i, j))
    gate = pl.BlockSpec((ts, LANES), lambda i: (i, GATE_COL // LANES))
    out = pl.BlockSpec((ts, FOX_WIDTH), lambda i: (i, 0))
    return pl.pallas_call(
        body, name=name, grid=(t // ts,),
        in_specs=[blk(0), blk(1), blk(2), gate, _full((1, FOX_WIDTH)), _full((1, FOX_WIDTH)), _full((1, LANES))],
        out_specs=[out, out, out, pl.BlockSpec((ts, LANES), lambda i: (i, 0))],
        out_shape=[jax.ShapeDtypeStruct((t, FOX_WIDTH), BF)] * 3 + [jax.ShapeDtypeStruct((t, LANES), F32)],
        scratch_shapes=[pltpu.VMEM((1, LANES), F32)],
        compiler_params=_params(("arbitrary",)),
    )(proj, proj, proj, proj, wq_t, wk_t, bias_pad)


def _pick_lanes(x, lane_in_block, first_out_lane):
    r = lax.broadcasted_iota(jnp.int32, (FOX_WIDTH, LANES), 0)
    c = lax.broadcasted_iota(jnp.int32, (FOX_WIDTH, LANES), 1)
    sel = ((r % LANES == lane_in_block) & (c == first_out_lane + 2 * (r // LANES))).astype(BF)
    hi = x.astype(BF)
    r1 = x - hi.astype(F32)
    mid = r1.astype(BF)
    lo = (r1 - mid.astype(F32)).astype(BF)
    return _dot(hi, sel) + _dot(mid, sel) + _dot(lo, sel)


def _fox_prep_bwd(proj, dqa, dqb, dk, dv, dkx, wq_t, wk_t, bias_pad, seq, name):
    t = proj.shape[0]
    ts = _tile(seq, 512)
    tpe = seq // ts
    nt = t // ts
    scale = FOX_HEAD_DIM ** -0.5

    def body(q_ref, k_ref, gt_ref, dqa_ref, dqb_ref, dk_ref, dv_ref, dc_ref, wq_ref, wk_ref, b_ref,
             dp_ref, dff_ref, dwq_ref, dwk_ref, db_ref, carry):
        i = pl.program_id(0)
        first = (lax.broadcasted_iota(jnp.int32, (ts, FOX_WIDTH), 1) % LANES) < FOX_HEAD_DIM
        dq_all = jnp.where(first, dqa_ref[...], dqb_ref[...])
        ti = nt - 1 - i
        bd = _head_sum_matrix(FOX_WIDTH, FOX_HEAD_DIM)

        @pl.when(i == 0)
        def _():
            dwq_ref[...] = jnp.zeros_like(dwq_ref)
            dwk_ref[...] = jnp.zeros_like(dwk_ref)
            db_ref[...] = jnp.zeros_like(db_ref)

        def norm_bwd(xv, wv, dyv):
            ms = _head_mean(xv * xv, bd)
            rstd = lax.rsqrt(ms + EPS)
            xhat = xv * rstd
            dxhat = dyv * wv
            mean = _head_mean(dxhat * xhat, bd)
            return rstd * (dxhat - xhat * mean), jnp.sum(dyv * xhat, axis=0, keepdims=True)

        dxq, dwq = norm_bwd(q_ref[...], wq_ref[...], dq_all * scale)
        dxk, dwk = norm_bwd(k_ref[...], wk_ref[...], dk_ref[...])
        dp_ref[:, 0:FOX_WIDTH] = dxq.astype(BF)
        dp_ref[:, FOX_WIDTH:2 * FOX_WIDTH] = dxk.astype(BF)
        dp_ref[:, 2 * FOX_WIDTH:3 * FOX_WIDTH] = dv_ref[...].astype(BF)
        dwq_ref[...] += dwq
        dwk_ref[...] += dwk

        @pl.when(ti % tpe == tpe - 1)
        def _():
            carry[...] = jnp.zeros_like(carry)

        r = lax.broadcasted_iota(jnp.int32, (ts, ts), 0)
        c = lax.broadcasted_iota(jnp.int32, (ts, ts), 1)
        dkx = dc_ref[...]
        hd = FOX_HEAD_DIM
        dcum = (_pick_lanes(dqa_ref[...], hd, 0) + _pick_lanes(dqb_ref[...], 0, 1)
                - _pick_lanes(dkx, hd, 0) - _pick_lanes(dkx, 0, 1))
        dls = _mask_dot(c >= r, dcum) + carry[...]
        carry[...] = dls[0:1, :]
        z = gt_ref[...] + b_ref[...]
        lane = lax.broadcasted_iota(jnp.int32, (ts, LANES), 1)
        dff = jnp.where(lane < FOX_HEADS, dls * _sigmoid(-z), 0.0)
        dff_ref[...] = dff
        db_ref[...] += jnp.sum(dff, axis=0, keepdims=True)

        @pl.when(i == nt - 1)
        def _():
            fr = lax.broadcasted_iota(jnp.int32, (FOX_WIDTH, FOX_WIDTH), 0) % FOX_HEAD_DIM
            fc = lax.broadcasted_iota(jnp.int32, (FOX_WIDTH, FOX_WIDTH), 1) % FOX_HEAD_DIM
            fold = (fr == fc).astype(F32)
            dwq_ref[...] = _dot(dwq_ref[...], fold, HI)
            dwk_ref[...] = _dot(dwk_ref[...], fold, HI)

    rev = lambda w, j: pl.BlockSpec((ts, w), lambda i: (nt - 1 - i, j))
    return pl.pallas_call(
        body, name=name, grid=(nt,),
        in_specs=[rev(FOX_WIDTH, 0), rev(FOX_WIDTH, 1), rev(LANES, GATE_COL // LANES),
                  rev(FOX_WIDTH, 0), rev(FOX_WIDTH, 0), rev(FOX_WIDTH, 0), rev(FOX_WIDTH, 0), rev(FOX_WIDTH, 0),
                  _full((1, FOX_WIDTH)), _full((1, FOX_WIDTH)), _full((1, LANES))],
        out_specs=[rev(3 * FOX_WIDTH, 0), rev(LANES, 0), _full((1, FOX_WIDTH)), _full((1, FOX_WIDTH)),
                   _full((1, LANES))],
        out_shape=[jax.ShapeDtypeStruct((t, 3 * FOX_WIDTH), BF), jax.ShapeDtypeStruct((t, LANES), F32),
                   jax.ShapeDtypeStruct((1, FOX_WIDTH), F32), jax.ShapeDtypeStruct((1, FOX_WIDTH), F32),
                   jax.ShapeDtypeStruct((1, LANES), F32)],
        scratch_shapes=[pltpu.VMEM((1, LANES), F32)],
        compiler_params=_params(("arbitrary",)),
    )(proj, proj, proj, dqa, dqb, dk, dv, dkx, wq_t, wk_t, bias_pad)


class _Rider:
    def __init__(self, inputs, out_shapes, sems, phases):
        self.inputs, self.out_shapes, self.sems, self.phases = list(inputs), list(out_shapes), list(sems), phases


def _rider_parts(rider):
    if rider is None:
        return [], [], []
    return rider.inputs, rider.out_shapes, rider.sems


def _ride(rider, which, when, refs):
    if rider is not None:
        @pl.when(when)
        def _():
            rider.phases[which](*refs)


def _fox_attn(q, k, v, ck, nb, seq, name, rider=None):
    t = q.shape[0]
    tq = _tile(seq, 2048)
    nq = seq // tq
    npair = FOX_HEADS // 2
    hd = FOX_HEAD_DIM
    r_in, r_out, r_sem = _rider_parts(rider)
    steps = nb * npair * nq

    def body(q_ref, k_ref, v_ref, ck_ref, *rest):
        rin, (o_ref, lse_ref) = rest[:len(r_in)], rest[len(r_in):len(r_in) + 2]
        rout = rest[len(r_in) + 2:len(r_in) + 2 + len(r_out)]
        m_s, acc_s = rest[len(r_in) + 2 + len(r_out):len(r_in) + 4 + len(r_out)]
        riding = (rin, rout, rest[len(r_in) + 4 + len(r_out):])
        step = (pl.program_id(0) * npair + pl.program_id(1)) * nq + pl.program_id(2)
        _ride(rider, 0, step == 0, riding)
        _ride(rider, 1, step == steps // 2, riding)
        qi = pl.program_id(2)
        lane = lax.broadcasted_iota(jnp.int32, (tq, LANES), 1)
        m_s[...] = jnp.full(m_s.shape, NEG, F32)
        acc_s[...] = jnp.zeros_like(acc_s)
        qv = q_ref[...]

        def tile(kj, on_diagonal):
            cols = pl.ds(pl.multiple_of(kj * tq, tq), tq)
            kv = k_ref[cols, :]
            vv = v_ref[cols, :]
            if on_diagonal:
                causal = (lax.broadcasted_iota(jnp.int32, (tq, tq), 0)
                          >= lax.broadcasted_iota(jnp.int32, (tq, tq), 1))
            ck = [ck_ref[hh, :, cols] for hh in range(2)]
            m_old = [m_s[hh] for hh in range(2)]
            acc_old = [acc_s[hh] for hh in range(2)]
            m_out, acc_out = [], []
            for hh in range(2):
                hm = (lane >= hd) if hh else (lane < hd)
                qh = jnp.where(hm, qv, jnp.zeros_like(qv))
                s = _dot_nt(qh, kv) - ck[hh]
                if on_diagonal:
                    s = jnp.where(causal, s, NEG)
                m_new = jnp.maximum(m_old[hh], jnp.max(s, axis=-1, keepdims=True))
                p = jnp.exp(s - m_new)
                alpha = jnp.exp(m_old[hh] - m_new)
                m_out.append(m_new)
                acc_out.append(alpha * acc_old[hh] + _dot(p.astype(BF), jnp.where(hm, vv, jnp.ones_like(vv))))
            for hh in range(2):
                m_s[hh] = m_out[hh]
                acc_s[hh] = acc_out[hh]

        def off_diagonal(kj, carry):
            tile(kj, False)
            return carry

        lax.fori_loop(0, qi, off_diagonal, 0)
        tile(qi, True)
        a0 = acc_s[0]
        a1 = acc_s[1]
        den = jnp.where(lane < hd, pltpu.roll(a0, hd, axis=1), pltpu.roll(a1, hd, axis=1))
        o_ref[...] = (jnp.where(lane < hd, a0, a1) / den).astype(o_ref.dtype)
        l0 = jnp.sum(jnp.where(lane == hd, a0, 0.0), axis=1, keepdims=True)
        l1 = jnp.sum(jnp.where(lane == 0, a1, 0.0), axis=1, keepdims=True)
        lse_ref[0] = m_s[0] + jnp.log(l0)
        lse_ref[1] = m_s[1] + jnp.log(l1)
        _ride(rider, 2, step == steps - 1, riding)

    qspec = pl.BlockSpec((tq, LANES), lambda b, p, i: (b * nq + i, p))
    kspec = pl.BlockSpec((seq, LANES), lambda b, p, i: (b, p))
    colspec = pl.BlockSpec((None, 2, tq, 1), lambda b, p, i: (b * npair + p, 0, i, 0))
    rowspec = pl.BlockSpec((None, 2, 1, seq), lambda b, p, i: (b * npair + p, 0, 0, 0))
    sem = ("arbitrary",) * 3 if rider else ("parallel",) * 3
    return pl.pallas_call(
        body, name=name, grid=(nb, npair, nq),
        in_specs=[qspec, kspec, kspec, rowspec] + [HBM] * len(r_in),
        out_specs=[qspec, colspec] + [HBM] * len(r_out),
        out_shape=[jax.ShapeDtypeStruct((t, FOX_WIDTH), BF), jax.ShapeDtypeStruct((nb * npair, 2, seq, 1), F32)]
        + r_out,
        scratch_shapes=[pltpu.VMEM((2, tq, 1), F32), pltpu.VMEM((2, tq, LANES), F32)] + r_sem,
        compiler_params=_params(sem, has_side_effects=rider is not None),
    )(q, k, v, ck, *r_in)


def _fox_attn_bwd(q, k, v, o, do, lse, ck, nb, seq, name, rider=None):
    t = q.shape[0]
    tq = _tile(seq, 1024)
    nq = seq // tq
    npair = FOX_HEADS // 2
    hd = FOX_HEAD_DIM
    r_in, r_out, r_sem = _rider_parts(rider)
    steps = nb * npair * nq

    def body(q_ref, k_ref, v_ref, o_ref, do_ref, lse_ref, ck_ref, *rest):
        rin, (dqa_ref, dqb_ref, dk_ref, dv_ref, dkx_ref) = rest[:len(r_in)], rest[len(r_in):len(r_in) + 5]
        rout = rest[len(r_in) + 5:len(r_in) + 5 + len(r_out)]
        dk_s, dv_s = rest[len(r_in) + 5 + len(r_out):len(r_in) + 7 + len(r_out)]
        riding = (rin, rout, rest[len(r_in) + 7 + len(r_out):])
        step = (pl.program_id(0) * npair + pl.program_id(1)) * nq + pl.program_id(2)
        _ride(rider, 0, step == 0, riding)
        _ride(rider, 1, step == steps // 2, riding)
        kj = pl.program_id(2)
        lane = lax.broadcasted_iota(jnp.int32, (tq, LANES), 1)

        @pl.when(kj == 0)
        def _():
            dqa_ref[...] = jnp.zeros_like(dqa_ref)
            dqb_ref[...] = jnp.zeros_like(dqb_ref)

        dk_s[...] = jnp.zeros_like(dk_s)
        dv_s[...] = jnp.zeros_like(dv_s)
        kv = k_ref[...]
        vv = v_ref[...]

        def tile(qi, on_diagonal):
            rows = pl.ds(pl.multiple_of(qi * tq, tq), tq)
            qv = q_ref[rows, :]
            dov = do_ref[rows, :]
            prod = dov.astype(F32) * o_ref[rows, :].astype(F32)
            if on_diagonal:
                causal = (lax.broadcasted_iota(jnp.int32, (tq, tq), 0)
                          >= lax.broadcasted_iota(jnp.int32, (tq, tq), 1))
            for hh, dq_ref in ((0, dqa_ref), (1, dqb_ref)):
                hm = (lane >= hd) if hh else (lane < hd)
                zero = jnp.zeros_like(qv)
                one = jnp.ones_like(qv)
                doh = jnp.where(hm, dov, zero)
                delta = jnp.sum(jnp.where(hm, prod, 0.0), axis=-1, keepdims=True)
                s = _dot_nt(jnp.where(hm, qv, zero), kv) - ck_ref[hh]
                if on_diagonal:
                    s = jnp.where(causal, s, NEG)
                p = jnp.exp(s - lse_ref[hh, rows, :])
                dp = _dot_nt(doh, vv)
                dsb = (p * (dp - delta)).astype(BF)
                dv_s[...] += _dot_tn(p.astype(BF), doh)
                dk_s[hh] += _dot_tn(dsb, jnp.where(hm, qv, one))
                dq_ref[rows, :] += _dot(dsb, jnp.where(hm, kv, one))

        def off_diagonal(qi, carry):
            tile(qi, False)
            return carry

        tile(kj, True)
        lax.fori_loop(kj + 1, nq, off_diagonal, 0)
        dk_ref[...] = jnp.where(lane < hd, dk_s[0], dk_s[1])
        dkx_ref[...] = jnp.where(lane < hd, dk_s[1], dk_s[0])
        dv_ref[...] = dv_s[...]
        _ride(rider, 2, step == steps - 1, riding)

    kspec = pl.BlockSpec((tq, LANES), lambda b, p, j: (b * nq + j, p))
    full_q = pl.BlockSpec((seq, LANES), lambda b, p, j: (b, p))
    colspec = pl.BlockSpec((None, 2, seq, 1), lambda b, p, j: (b * npair + p, 0, 0, 0))
    rowspec = pl.BlockSpec((None, 2, 1, tq), lambda b, p, j: (b * npair + p, 0, 0, j))
    sem = ("arbitrary",) * 3 if rider else ("parallel", "parallel", "arbitrary")
    return pl.pallas_call(
        body, name=name, grid=(nb, npair, nq),
        in_specs=[full_q, kspec, kspec, full_q, full_q, colspec, rowspec] + [HBM] * len(r_in),
        out_specs=[full_q, full_q, kspec, kspec, kspec] + [HBM] * len(r_out),
        out_shape=[jax.ShapeDtypeStruct((t, FOX_WIDTH), F32)] * 5 + r_out,
        scratch_shapes=[pltpu.VMEM((2, tq, LANES), F32), pltpu.VMEM((tq, LANES), F32)] + r_sem,
        compiler_params=_params(sem, has_side_effects=rider is not None),
    )(q, k, v, o, do, lse, ck, *r_in)


GDN_QKV = 3 * GDN_WIDTH
GDN_COL = 3 * FOX_WIDTH
GG_COL = GDN_COL + GDN_QKV
A_LANE = FOX_HEADS
B_LANE = FOX_HEADS + GDN_HEADS
HALO = 8


def _gate_lanes(ts):
    lane = lax.broadcasted_iota(jnp.int32, (ts, LANES), 1)
    return (lane >= A_LANE) & (lane < B_LANE), (lane >= B_LANE) & (lane < B_LANE + GDN_HEADS)


def _chunk_tri(ts, upper):
    r = lax.broadcasted_iota(jnp.int32, (ts, ts), 0)
    c = lax.broadcasted_iota(jnp.int32, (ts, ts), 1)
    same = (r // CHUNK) == (c // CHUNK)
    return (same & ((c >= r) if upper else (r >= c))).astype(F32)


def _conv_silu_l2(xp_ref, w, ts):
    c = w[0:1, :] * xp_ref[pl.ds(HALO - 3, ts), :]
    for kk in range(1, CONV_WIDTH):
        c = c + w[kk:kk + 1, :] * xp_ref[pl.ds(HALO - 3 + kk, ts), :]
    return c, c * _sigmoid(c)


def _gdn_prep(proj, conv_w, a_pad, dt_pad, seq, name):
    t = proj.shape[0]
    ts = _tile(seq, 256)
    tpe = seq // ts
    qscale = GDN_HEAD_DIM ** -0.5

    def body(x_ref, gt_ref, w_ref, a_ref, dt_ref, qo_ref, ko_ref, vo_ref, go_ref, xp):
        i = pl.program_id(0)
        tail = xp[pl.ds(ts, HALO), :]
        xp[pl.ds(0, HALO), :] = jnp.where(i % tpe == 0, jnp.zeros_like(tail), tail)
        xp[pl.ds(HALO, ts), :] = x_ref[...]
        _, s = _conv_silu_l2(xp, w_ref[...], ts)
        for h in range(GDN_HEADS):
            for base, ref, sc in ((0, qo_ref, qscale), (GDN_WIDTH, ko_ref, 1.0)):
                xh = s[:, base + h * LANES: base + (h + 1) * LANES]
                r = lax.rsqrt(jnp.sum(xh * xh, axis=-1, keepdims=True) + EPS)
                ref[:, h * LANES:(h + 1) * LANES] = (xh * (r * sc)).astype(BF)
        vo_ref[...] = s[:, 2 * GDN_WIDTH:].astype(BF)
        gate = gt_ref[...]
        g_raw = -jnp.exp(a_ref[...]) * _softplus(gate + dt_ref[...])
        gc = _mask_dot(_chunk_tri(ts, False), g_raw)
        is_a, is_b = _gate_lanes(ts)
        go_ref[...] = jnp.where(is_a, gc, jnp.where(is_b, _sigmoid(gate), 0.0))

    out = pl.BlockSpec((ts, GDN_WIDTH), lambda i: (i, 0))
    lanes = pl.BlockSpec((ts, LANES), lambda i: (i, 0))
    return pl.pallas_call(
        body, name=name, grid=(t // ts,),
        in_specs=[pl.BlockSpec((ts, GDN_QKV), lambda i: (i, GDN_COL // GDN_QKV)),
                  pl.BlockSpec((ts, LANES), lambda i: (i, GATE_COL // LANES)),
                  _full((CONV_WIDTH, GDN_QKV)), _full((1, LANES)), _full((1, LANES))],
        out_specs=[out, out, out, lanes],
        out_shape=[jax.ShapeDtypeStruct((t, GDN_WIDTH), BF)] * 3 + [jax.ShapeDtypeStruct((t, LANES), F32)],
        scratch_shapes=[pltpu.VMEM((ts + HALO, GDN_QKV), F32)],
        compiler_params=_params(("arbitrary",)),
    )(proj, proj, conv_w, a_pad, dt_pad)


def _gdn_prep_bwd(proj, dq, dk, dv, dgates, dff, conv_w, a_pad, dt_pad, seq, name):
    t = proj.shape[0]
    ts = _tile(seq, 256)
    tpe = seq // ts
    nt = t // ts
    qscale = GDN_HEAD_DIM ** -0.5
    hb = ts // HALO

    def body(x_ref, halo_ref, gt_ref, dq_ref, dk_ref, dv_ref, dgt_ref, dff_ref, w_ref, a_ref, dt_ref,
             dx_ref, dgo_ref, dw_ref, da_ref, ddt_ref, xp, dcp, carry):
        i = pl.program_id(0)
        ti = nt - 1 - i

        @pl.when(i == 0)
        def _():
            dw_ref[...] = jnp.zeros_like(dw_ref)
            da_ref[...] = jnp.zeros_like(da_ref)
            ddt_ref[...] = jnp.zeros_like(ddt_ref)

        halo = halo_ref[...]
        xp[pl.ds(0, HALO), :] = jnp.where(ti % tpe == 0, jnp.zeros_like(halo), halo)
        xp[pl.ds(HALO, ts), :] = x_ref[...]
        w = w_ref[...]
        c, s = _conv_silu_l2(xp, w, ts)
        for h in range(GDN_HEADS):
            for base, ref, sc in ((0, dq_ref, qscale), (GDN_WIDTH, dk_ref, 1.0)):
                lo = base + h * LANES
                xh = s[:, lo:lo + LANES]
                r = lax.rsqrt(jnp.sum(xh * xh, axis=-1, keepdims=True) + EPS)
                y = xh * r
                dy = ref[:, h * LANES:(h + 1) * LANES] * sc
                dcp[pl.ds(0, ts), lo:lo + LANES] = r * (dy - y * jnp.sum(dy * y, axis=-1, keepdims=True))
        dcp[pl.ds(0, ts), 2 * GDN_WIDTH:] = dv_ref[...]
        sg = _sigmoid(c)
        dc = dcp[pl.ds(0, ts), :] * (sg * (1.0 + c * (1.0 - sg)))
        dcp[pl.ds(0, ts), :] = dc
        nxt = carry[...]
        dcp[pl.ds(ts, HALO), :] = jnp.where(ti % tpe == tpe - 1, jnp.zeros_like(nxt), nxt)
        carry[...] = dc[0:HALO, :]
        dx = w[CONV_WIDTH - 1:CONV_WIDTH, :] * dc
        for kk in range(CONV_WIDTH - 1):
            dx = dx + w[kk:kk + 1, :] * dcp[pl.ds(CONV_WIDTH - 1 - kk, ts), :]
        dx_ref[...] = dx.astype(BF)
        for kk in range(CONV_WIDTH):
            dw_ref[kk:kk + 1, :] += jnp.sum(dc * xp[pl.ds(HALO - 3 + kk, ts), :], axis=0, keepdims=True)
        gate = gt_ref[...]
        dgt = dgt_ref[...]
        is_a, is_b = _gate_lanes(ts)
        dg_raw = _mask_dot(_chunk_tri(ts, True), jnp.where(is_a, dgt, 0.0))
        z = gate + dt_ref[...]
        na = -jnp.exp(a_ref[...])
        dga = dg_raw * na * _sigmoid(z)
        beta = _sigmoid(gate)
        dgb = jnp.where(is_b, dgt * beta * (1.0 - beta), 0.0)
        dgo_ref[...] = (dff_ref[...] + dga + dgb).astype(BF)
        ddt_ref[...] += jnp.sum(dga, axis=0, keepdims=True)
        da_ref[...] += jnp.sum(dg_raw * na * _softplus(z), axis=0, keepdims=True)

    rev = lambda wd, j: pl.BlockSpec((ts, wd), lambda i: (nt - 1 - i, j))
    halo_spec = pl.BlockSpec((HALO, GDN_QKV), lambda i: (jnp.maximum((nt - 1 - i) * hb - 1, 0), GDN_COL // GDN_QKV))
    return pl.pallas_call(
        body, name=name, grid=(nt,),
        in_specs=[rev(GDN_QKV, GDN_COL // GDN_QKV), halo_spec, rev(LANES, GATE_COL // LANES),
                  rev(GDN_WIDTH, 0), rev(GDN_WIDTH, 0), rev(GDN_WIDTH, 0), rev(LANES, 0), rev(LANES, 0),
                  _full((CONV_WIDTH, GDN_QKV)), _full((1, LANES)), _full((1, LANES))],
        out_specs=[rev(GDN_QKV, 0), rev(LANES, 0), _full((CONV_WIDTH, GDN_QKV)), _full((1, LANES)),
                   _full((1, LANES))],
        out_shape=[jax.ShapeDtypeStruct((t, GDN_QKV), BF), jax.ShapeDtypeStruct((t, LANES), BF),
                   jax.ShapeDtypeStruct((CONV_WIDTH, GDN_QKV), F32), jax.ShapeDtypeStruct((1, LANES), F32),
                   jax.ShapeDtypeStruct((1, LANES), F32)],
        scratch_shapes=[pltpu.VMEM((ts + HALO, GDN_QKV), F32), pltpu.VMEM((ts + HALO, GDN_QKV), F32),
                        pltpu.VMEM((HALO, GDN_QKV), F32)],
        compiler_params=_params(("arbitrary",)),
    )(proj, proj, proj, dq, dk, dv, dgates, dff, conv_w, a_pad, dt_pad)


PAIR = 2 * CHUNK


def _split_bf16(a):
    hi = a.astype(BF)
    return hi, (a - hi.astype(F32)).astype(BF)


def _dot3(a, b, dims=(((1,), (0,)), ((), ()))):
    ah, al = _split_bf16(a)
    bh, bl = _split_bf16(b)
    (ca,), (cb,) = dims[0]
    return lax.dot_general(jnp.concatenate([ah, al, ah], axis=ca), jnp.concatenate([bh, bh, bl], axis=cb), dims,
                           preferred_element_type=F32)


def _inv_unit_lower(a):
    r = lax.broadcasted_iota(jnp.int32, (PAIR, PAIR), 0)
    c = lax.broadcasted_iota(jnp.int32, (PAIR, PAIR), 1)
    tm = (r == c).astype(F32) - a
    pw = _dot3(a, a)
    for _ in range(4):
        x = _dot3(jnp.concatenate([tm, pw], axis=0), pw)
        tm = tm + x[:PAIR]
        pw = x[PAIR:]
    return tm + _dot3(tm, pw)


def _gdn_pair_local(q, k, v, gc, gr, b):
    r = lax.broadcasted_iota(jnp.int32, (PAIR, PAIR), 0)
    c = lax.broadcasted_iota(jnp.int32, (PAIR, PAIR), 1)
    same = (r // CHUNK) == (c // CHUNK)
    incl = same & (r >= c)
    strict = same & (r > c)
    dm = jnp.exp(jnp.where(incl, gc - gr, NEG))
    e = jnp.exp(gc)
    kb = k * b
    vb = v * b
    kbe = kb * e
    kq = _dot_nt(jnp.concatenate([kb, q], axis=0).astype(BF), k.astype(BF))
    amat = jnp.where(strict, kq[:PAIR] * dm, 0.0)
    pmat = jnp.where(incl, kq[PAIR:] * dm, 0.0)
    lane = lax.broadcasted_iota(jnp.int32, (1, PAIR), 1)
    gl_a = jnp.sum(jnp.where(lane == CHUNK - 1, gr, 0.0), axis=1, keepdims=True)
    gl_b = jnp.sum(jnp.where(lane == PAIR - 1, gr, 0.0), axis=1, keepdims=True)
    ridx = lax.broadcasted_iota(jnp.int32, (PAIR, 1), 0)
    edec = jnp.exp(jnp.where(ridx < CHUNK, gl_a, gl_b) - gc)
    return dict(dm=dm, e=e, kb=kb, vb=vb, kbe=kbe, amat=amat, pmat=pmat, gl_a=gl_a, gl_b=gl_b, edec=edec,
                kd=k * edec, qd=q * e, incl=incl, strict=strict, ridx=ridx)


def _gdn_pair_states(loc, tb, s_a):
    uw = _dot(tb, jnp.concatenate([loc["vb"], loc["kbe"]], axis=1).astype(BF))
    u, w = uw[:, :LANES], uw[:, LANES:]
    qd, kd, c = loc["qd"], loc["kd"], CHUNK
    xa = _dot(jnp.concatenate([qd[:c], w[:c]], axis=0).astype(BF), s_a.astype(BF))
    vn_a = u[:c] - xa[c:]
    s_b = s_a * jnp.exp(loc["gl_a"]) + _dot_tn(kd[:c].astype(BF), vn_a.astype(BF))
    xb = _dot(jnp.concatenate([qd[c:], w[c:]], axis=0).astype(BF), s_b.astype(BF))
    vn_b = u[c:] - xb[c:]
    s_c = s_b * jnp.exp(loc["gl_b"]) + _dot_tn(kd[c:].astype(BF), vn_b.astype(BF))
    vn = jnp.concatenate([vn_a, vn_b], axis=0)
    o = jnp.concatenate([xa[:c], xb[:c]], axis=0) + _dot(loc["pmat"].astype(BF), vn.astype(BF))
    return w, vn, o, s_b, s_c


GDN_SEG = 512


def _gdn_specs(nb, seq, reverse):
    n = seq // CHUNK
    seg = _tile(seq, GDN_SEG)
    nseg = seq // seg
    sp = seg // PAIR
    at = (lambda s: nseg - 1 - s) if reverse else (lambda s: s)
    blk = pl.BlockSpec((nb, seg, GDN_WIDTH), lambda s: (0, at(s), 0))
    gg = pl.BlockSpec((nb, seg, GDN_WIDTH), lambda s: (0, at(s), GG_COL // GDN_WIDTH))
    gates = pl.BlockSpec((nb, seg, LANES), lambda s: (0, at(s), 0))
    rowb = pl.BlockSpec((nb, GDN_HEADS, sp, HALO, PAIR), lambda s: (0, 0, at(s), 0, 0))
    per_pair = pl.BlockSpec((nb, GDN_HEADS, sp, PAIR, PAIR), lambda s: (0, 0, at(s), 0, 0))
    return n, seg, nseg, sp, blk, gg, gates, rowb, per_pair


def _head_column(gt, lane, index):
    return jnp.sum(jnp.where(lane == index, gt, 0.0), axis=1, keepdims=True)


def _gdn_head_inputs(qkv_refs, gt_ref, gr_ref, rows, pi, lane, chains):
    per_chain = []
    for b, hh in chains:
        gt = gt_ref[b, rows, :]
        cols = slice(hh * LANES, (hh + 1) * LANES)
        per_chain.append([r[b, rows, cols].astype(F32) for r in qkv_refs]
                         + [_head_column(gt, lane, A_LANE + hh), gr_ref[b, hh, pi][0:1, :],
                            _head_column(gt, lane, B_LANE + hh)])
    return [jnp.stack(xs) for xs in zip(*per_chain)]


def _gdn_pair_fwd(qv, kv, vv, gcv, gr, bv, s_a):
    loc = _gdn_pair_local(qv, kv, vv, gcv, gr, bv)
    tf = _inv_unit_lower(loc["amat"])
    _, _, o, _, s_c = _gdn_pair_states(loc, tf.astype(BF), s_a)
    return tf, o, s_c


def _gdn_fwd(q, k, v, proj, gates, grow, wn, nb, seq, name):
    n, seg, nseg, sp, blk, gg, gates_spec, rowb, per_pair = _gdn_specs(nb, seq, False)
    chains = [(b, hh) for b in range(nb) for hh in range(GDN_HEADS)]

    def body(q_ref, k_ref, v_ref, gg_ref, gt_ref, gr_ref, wn_ref, y_ref, tn_ref, sn_ref, s_ref):
        @pl.when(pl.program_id(0) == 0)
        def _():
            s_ref[...] = jnp.zeros_like(s_ref)

        wnv = wn_ref[...]
        lane = lax.broadcasted_iota(jnp.int32, (PAIR, LANES), 1)

        def step(pi, carry):
            rows = pl.ds(pl.multiple_of(pi * PAIR, PAIR), PAIR)
            ins = _gdn_head_inputs((q_ref, k_ref, v_ref), gt_ref, gr_ref, rows, pi, lane, chains)
            s_a = s_ref[...]
            tf, o, s_c = jax.vmap(_gdn_pair_fwd)(*ins, s_a)
            s_ref[...] = s_c
            for c, (b, hh) in enumerate(chains):
                cols = slice(hh * LANES, (hh + 1) * LANES)
                tn_ref[b, hh, pi] = tf[c]
                sn_ref[b, hh, pi] = s_a[c]
                g = gg_ref[b, rows, cols]
                oh = o[c]
                rstd = lax.rsqrt(jnp.mean(oh * oh, axis=-1, keepdims=True) + EPS)
                y_ref[b, rows, cols] = (oh * rstd * wnv * (g * _sigmoid(g))).astype(BF)
            return carry

        lax.fori_loop(0, sp, step, 0)

    saved = jax.ShapeDtypeStruct((nb, GDN_HEADS, n // 2, PAIR, PAIR), F32)
    return pl.pallas_call(
        body, name=name, grid=(nseg,),
        in_specs=[blk, blk, blk, gg, gates_spec, rowb, _full((1, LANES))],
        out_specs=[blk, per_pair, per_pair],
        out_shape=[jax.ShapeDtypeStruct((nb, seq, GDN_WIDTH), BF), saved, saved],
        scratch_shapes=[pltpu.VMEM((len(chains), GDN_HEAD_DIM, GDN_HEAD_DIM), F32)],
        compiler_params=_params(("arbitrary",)),
    )(q, k, v, proj, gates, grow, wn)


def _gdn_pair_bwd(qv, kv, vv, gcv, gr, bv, tf, s_a, dsp, g, dyv, wnv):
    c = CHUNK
    loc = _gdn_pair_local(qv, kv, vv, gcv, gr, bv)
    tm = tf.astype(BF)
    kb, vb, kbe, e, dm = loc["kb"], loc["vb"], loc["kbe"], loc["e"], loc["dm"]
    kd, qd, pmat, amat = loc["kd"], loc["qd"], loc["pmat"], loc["amat"]
    w, vn, o, s_b, _ = _gdn_pair_states(loc, tm, s_a)
    sg = _sigmoid(g)
    silu = g * sg
    rstd = lax.rsqrt(jnp.mean(o * o, axis=-1, keepdims=True) + EPS)
    xhat = o * rstd
    dwn = jnp.sum(dyv * xhat * silu, axis=0, keepdims=True)
    dgg = dyv * xhat * wnv * (sg * (1.0 + g * (1.0 - sg)))
    dxhat = dyv * wnv * silu
    do = rstd * (dxhat - xhat * jnp.mean(dxhat * xhat, axis=-1, keepdims=True))
    dob = do.astype(BF)
    tot = lambda x: jnp.sum(jnp.sum(x, axis=1, keepdims=True), axis=0, keepdims=True)
    rsum = lambda x: jnp.sum(x, axis=1, keepdims=True)
    cat = lambda xs, ax=0: jnp.concatenate(xs, axis=ax)
    wb = w.astype(BF)
    qdb = qd.astype(BF)
    kdb = kd.astype(BF)
    vnb = vn.astype(BF)
    egl_a = jnp.exp(loc["gl_a"])
    egl_b = jnp.exp(loc["gl_b"])
    ptdo = _dot_tn(pmat.astype(BF), dob)
    dspb = dsp.astype(BF)
    dvn_b = ptdo[c:] + _dot(kdb[c:], dspb)
    dkd_b = _dot_nt(vnb[c:], dspb)
    dgl_b = egl_b * tot(s_b * dsp) + tot(dkd_b * kd[c:])
    dsm = egl_b * dsp + _dot_tn(cat([qdb[c:], -wb[c:]]), cat([dob[c:], dvn_b.astype(BF)]))
    dsmb = dsm.astype(BF)
    dvn_a = ptdo[:c] + _dot(kdb[:c], dsmb)
    dkd_a = _dot_nt(vnb[:c], dsmb)
    dgl_a = egl_a * tot(s_a * dsm) + tot(dkd_a * kd[:c])
    ds_new = egl_a * dsm + _dot_tn(cat([qdb[:c], -wb[:c]]), cat([dob[:c], dvn_a.astype(BF)]))
    ya = _dot_nt(cat([dob[:c], dvn_a.astype(BF)]), s_a.astype(BF))
    yb = _dot_nt(cat([dob[c:], dvn_b.astype(BF)]), s_b.astype(BF))
    dqd = cat([ya[:c], yb[:c]])
    dw = -cat([ya[c:], yb[c:]])
    dvn = cat([dvn_a, dvn_b])
    dkd = cat([dkd_a, dkd_b])
    dq = dqd * e
    dgc = rsum(dqd * qd) - rsum(dkd * kd)
    dk = dkd * loc["edec"]
    dpm = jnp.where(loc["incl"], _dot_nt(dob, vnb), 0.0)
    duw = cat([dvn, dw], 1).astype(BF)
    dt = _dot_nt(duw, cat([vb, kbe], 1).astype(BF))
    tt = _dot_tn(tm, duw)
    dvb, dkbe = tt[:, :LANES], tt[:, LANES:]
    tn_dims = (((0,), (0,)), ((), ()))
    nt_dims = (((1,), (1,)), ((), ()))
    da = jnp.where(loc["strict"], -_dot3(_dot3(tf, dt, tn_dims), tf, nt_dims), 0.0)
    st = cat([da * dm, dpm * dm]).astype(BF)
    z = _dot(st, kv.astype(BF))
    dkb = z[:PAIR] + dkbe * e
    dq = dq + z[PAIR:]
    dk = dk + _dot_tn(st, cat([kb, qv]).astype(BF))
    gmat = dpm * pmat + da * amat
    dgc = dgc + rsum(dkbe * kbe) + rsum(gmat)
    ridx = loc["ridx"]
    dgc = dgc + jnp.where(ridx == c - 1, dgl_a, 0.0) + jnp.where(ridx == PAIR - 1, dgl_b, 0.0)
    dgc_row = jnp.sum(gmat, axis=0, keepdims=True)
    db = rsum(dvb * vv) + rsum(dkb * kv)
    return dq, dk + dkb * bv, dvb * bv, dgg, dgc, dgc_row, db, dwn, ds_new


def _gdn_bwd(q, k, v, proj, gates, grow, wn, tinv_all, states_all, dy, nb, seq, name):
    n, seg, nseg, sp, blk, gg, gates_spec, rowb, per_pair = _gdn_specs(nb, seq, True)
    dh = GDN_HEAD_DIM
    chains = [(b, hh) for b in range(nb) for hh in range(GDN_HEADS)]

    def body(q_ref, k_ref, v_ref, gg_ref, gt_ref, gr_ref, wn_ref, tn_ref, sn_ref, dy_ref,
             dq_ref, dk_ref, dv_ref, dgg_ref, dgt_ref, dwn_ref, ds_ref):
        @pl.when(pl.program_id(0) == 0)
        def _():
            dwn_ref[...] = jnp.zeros_like(dwn_ref)
            ds_ref[...] = jnp.zeros_like(ds_ref)

        wnv = wn_ref[...]
        lane = lax.broadcasted_iota(jnp.int32, (PAIR, LANES), 1)

        def bwd_step(j, carry):
            pi = sp - 1 - j
            rows = pl.ds(pl.multiple_of(pi * PAIR, PAIR), PAIR)
            ins = _gdn_head_inputs((q_ref, k_ref, v_ref), gt_ref, gr_ref, rows, pi, lane, chains)
            lanes_of = lambda hh: slice(hh * LANES, (hh + 1) * LANES)
            saved = [jnp.stack([r[b, hh, pi] for b, hh in chains]) for r in (tn_ref, sn_ref)]
            g2 = jnp.stack([gg_ref[b, rows, lanes_of(hh)] for b, hh in chains])
            dy2 = jnp.stack([dy_ref[b, rows, lanes_of(hh)].astype(F32) for b, hh in chains])
            dq, dk, dv, dgg, dgc, dgc_row, db, dwn, ds_new = jax.vmap(
                _gdn_pair_bwd, in_axes=(0,) * 11 + (None,))(*ins, *saved, ds_ref[...], g2, dy2, wnv)
            ds_ref[...] = ds_new
            dgt = [jnp.zeros((PAIR, LANES), F32) for _ in range(nb)]
            for c, (b, hh) in enumerate(chains):
                cols = lanes_of(hh)
                dq_ref[b, rows, cols] = dq[c]
                dk_ref[b, rows, cols] = dk[c]
                dv_ref[b, rows, cols] = dv[c]
                dgg_ref[b, rows, cols] = dgg[c].astype(BF)
                dwn_ref[...] += dwn[c]
                row_as_col = jnp.transpose(jnp.broadcast_to(dgc_row[c], (PAIR, LANES)))
                dgt[b] = (dgt[b] + jnp.where(lane == A_LANE + hh, dgc[c] - row_as_col, 0.0)
                          + jnp.where(lane == B_LANE + hh, db[c], 0.0))
            for b in range(nb):
                dgt_ref[b, rows, :] = dgt[b]
            return carry

        lax.fori_loop(0, sp, bwd_step, 0)

    f32_out = jax.ShapeDtypeStruct((nb, seq, GDN_WIDTH), F32)
    return pl.pallas_call(
        body, name=name, grid=(nseg,),
        in_specs=[blk, blk, blk, gg, gates_spec, rowb, _full((1, LANES)), per_pair, per_pair, blk],
        out_specs=[blk, blk, blk, blk, gates_spec, _full((1, LANES))],
        out_shape=[f32_out, f32_out, f32_out, jax.ShapeDtypeStruct((nb, seq, GDN_WIDTH), BF),
                   jax.ShapeDtypeStruct((nb, seq, LANES), F32), jax.ShapeDtypeStruct((1, LANES), F32)],
        scratch_shapes=[pltpu.VMEM((len(chains), dh, dh), F32)],
        compiler_params=_params(("arbitrary",)),
    )(q, k, v, proj, gates, grow, wn, tinv_all, states_all, dy)


def _mix_to_padded(w):
    pad = jnp.zeros(w.shape[:-1] + (N_PAD - N_IN,), w.dtype)
    return jnp.concatenate([w[..., 0:1536], w[..., 1544:3080], w[..., 3088:3600], w[..., 1536:1544],
                            w[..., 3080:3088], pad], axis=-1)


def _pad_lanes(vec, start):
    return jnp.pad(vec[None, :], ((0, 0), (start, LANES - start - vec.shape[0])))


def _heads_to_rows(block, lane0, nheads, nb, seq):
    return block[:, lane0:lane0 + nheads].reshape(nb, seq, nheads).transpose(0, 2, 1).reshape(nb * nheads, seq)


def _mixer_small(p, l):
    wq_t = jnp.tile(p["fox_q_norm"][l], FOX_HEADS)[None, :]
    wk_t = jnp.tile(p["fox_k_norm"][l], FOX_HEADS)[None, :]
    bias = _pad_lanes(p["fox_f_bias"][l], 0)
    a_pad = _pad_lanes(p["gdn_a_log"][l], A_LANE)
    dt_pad = _pad_lanes(p["gdn_dt_bias"][l], A_LANE)
    wn = p["gdn_out_norm"][l][None, :]
    return wq_t, wk_t, bias, a_pad, dt_pad, wn


def _layer_fwd(x, p, l, nb, seq, rider=None, ffn1_rider=None, after_ffn1=None):
    npair = FOX_HEADS // 2
    n = seq // CHUNK
    x1, h1, *rode1 = _ffn_fwd(x, p["ffn1_norm"][l][None, :], p["ffn1_w_in"][l], p["ffn1_w_out"][l],
                              f"ffn1_fwd_{l}", ffn1_rider)
    if after_ffn1 is not None:
        after_ffn1(rode1)
    wq_t, wk_t, bias, a_pad, dt_pad, wn = _mixer_small(p, l)
    proj = _norm_matmul(x1, p["mix_norm"][l][None, :], p["w_mix"][l], f"mix_in_{l}")
    fq, fk, fv, cum = _fox_prep(proj, wq_t, wk_t, bias, seq, f"fox_prep_{l}")
    c8 = _heads_to_rows(cum, 0, FOX_HEADS, nb, seq)
    ck = c8.reshape(nb * npair, 2, 1, seq)
    o, lse, *rode = _fox_attn(fq, fk, fv, ck, nb, seq, f"fox_attn_{l}", rider)
    gq, gk, gv, gates = _gdn_prep(proj, p["gdn_conv"][l], a_pad, dt_pad, seq, f"gdn_prep_{l}")
    gc4 = _heads_to_rows(gates, A_LANE, GDN_HEADS, nb, seq)
    grow = jnp.broadcast_to(gc4.reshape(nb, GDN_HEADS, n // 2, 1, PAIR), (nb, GDN_HEADS, n // 2, HALO, PAIR))
    per_example = lambda a: a.reshape(nb, seq, a.shape[-1])
    gq, gk, gv, gates = per_example(gq), per_example(gk), per_example(gv), per_example(gates)
    y, tinv, states = _gdn_fwd(gq, gk, gv, per_example(proj), gates, grow, wn, nb, seq, f"gdn_fwd_{l}")
    y = y.reshape(nb * seq, GDN_WIDTH)
    x2 = _mix_out(x1, o, y, p["w_out"][l], f"mix_out_{l}")
    x3, h2 = _ffn_fwd(x2, p["ffn2_norm"][l][None, :], p["ffn2_w_in"][l], p["ffn2_w_out"][l], f"ffn2_fwd_{l}")
    saved = dict(x=x, h1=h1, x1=x1, proj=proj, fq=fq, fk=fk, fv=fv, ck=ck, o=o, lse=lse,
                 gq=gq, gk=gk, gv=gv, gates=gates, grow=grow, tinv=tinv, states=states, y=y, x2=x2, h2=h2)
    return x3, saved, rode


def _ffn_grads(dy, x, h, gain, win, wout, l, tag, rider=None):
    t, d = x.shape
    fs = win.shape[2]
    dx, dh, a, hn, dyh, dgain, *rode = _ffn_bwd(dy, x, h, gain, win, wout, f"{tag}_bwd_{l}", rider)
    g_in = _wgrad(hn, dh, jax.ShapeDtypeStruct((4, d, fs), BF),
                  pl.BlockSpec((None, d, fs), lambda i, j, k: (j, i, 0)), d, fs, f"{tag}_gw_in_{l}")
    g_out = _wgrad(a, dyh, jax.ShapeDtypeStruct((2 * fs, d), BF),
                   pl.BlockSpec((fs, d), lambda i, j, k: (i, j)), fs, d, f"{tag}_gw_out_{l}")
    return dx, dgain[0], g_in, g_out.reshape(4, fs // 2, d), rode


def _layer_bwd(dx3, p, l, sv, nb, seq, rider=None, before_ffn1=None):
    npair = FOX_HEADS // 2
    d = dx3.shape[1]
    wq_t, wk_t, bias, a_pad, dt_pad, wn = _mixer_small(p, l)
    g = {}
    dx2, g["ffn2_norm"], g["ffn2_w_in"], g["ffn2_w_out"], _ = _ffn_grads(
        dx3, sv["x2"], sv["h2"], p["ffn2_norm"][l][None, :], p["ffn2_w_in"][l], p["ffn2_w_out"][l], l, "ffn2")
    dyf, dyg, dxb = _mix_out_bwd(dx2, p["w_out"][l], f"mix_out_bwd_{l}")
    half = lambda a, nm: _wgrad(a, dxb, jax.ShapeDtypeStruct((FOX_WIDTH, d), BF),
                                pl.BlockSpec((FOX_WIDTH, d), lambda i, j, k: (i, j)), FOX_WIDTH, d, nm)
    g["w_out"] = jnp.concatenate([half(sv["o"], f"gw_out_fox_{l}"), half(sv["y"], f"gw_out_gdn_{l}")], axis=0)
    dqa, dqb, dk, dv, dkx, *rode = _fox_attn_bwd(sv["fq"], sv["fk"], sv["fv"], sv["o"], dyf, sv["lse"], sv["ck"],
                                                 nb, seq, f"fox_attn_bwd_{l}", rider)

    dpf, dff, dwq, dwk, dbias = _fox_prep_bwd(sv["proj"], dqa, dqb, dk, dv, dkx, wq_t, wk_t, bias, seq,
                                              f"fox_prep_bwd_{l}")
    g["fox_q_norm"] = dwq[0, :FOX_HEAD_DIM]
    g["fox_k_norm"] = dwk[0, :FOX_HEAD_DIM]
    g["fox_f_bias"] = dbias[0, :FOX_HEADS]
    per_example = lambda a: a.reshape(nb, seq, a.shape[-1])
    flat = lambda a: a.reshape(nb * seq, a.shape[-1])
    dgq, dgk, dgv, dgg, dgates, dwn = _gdn_bwd(
        sv["gq"], sv["gk"], sv["gv"], per_example(sv["proj"]), sv["gates"], sv["grow"], wn, sv["tinv"],
        sv["states"], per_example(dyg), nb, seq, f"gdn_bwd_{l}")
    dgq, dgk, dgv, dgg, dgates = flat(dgq), flat(dgk), flat(dgv), flat(dgg), flat(dgates)
    dpg, dgate_blk, dconv, da, ddt = _gdn_prep_bwd(sv["proj"], dgq, dgk, dgv, dgates, dff, p["gdn_conv"][l],
                                                   a_pad, dt_pad, seq, f"gdn_prep_bwd_{l}")
    g["gdn_conv"] = dconv
    g["gdn_a_log"] = da[0, A_LANE:B_LANE]
    g["gdn_dt_bias"] = ddt[0, A_LANE:B_LANE]
    g["gdn_out_norm"] = dwn[0]
    dparts = [dpf, dpg, dgg, dgate_blk]
    dx1, hnm, dgm = _norm_matmul_bwd(dx2, dparts, sv["x1"], p["mix_norm"][l][None, :], p["w_mix"][l],
                                     f"mix_in_bwd_{l}")
    g["mix_norm"] = dgm[0]
    gf, gg_, go, gt = [_wgrad(hnm, a, jax.ShapeDtypeStruct((d, a.shape[1]), F32),
                              pl.BlockSpec((d // 2, a.shape[1]), lambda i, j, k: (i, j)), d // 2, a.shape[1],
                              f"gw_mix_{l}_{i}") for i, a in enumerate(dparts)]
    g["w_in"] = jnp.concatenate([gf, gt[:, 0:FOX_HEADS], gg_, gt[:, A_LANE:B_LANE + GDN_HEADS], go], axis=1)
    ffn1_rider = before_ffn1(g) if before_ffn1 is not None else None
    dx0, g["ffn1_norm"], g["ffn1_w_in"], g["ffn1_w_out"], rode1 = _ffn_grads(
        dx1, sv["x"], sv["h1"], p["ffn1_norm"][l][None, :], p["ffn1_w_in"][l], p["ffn1_w_out"][l], l, "ffn1",
        ffn1_rider)
    return dx0, g, rode, rode1


def _local_step(x, target, p):
    nb, seq, d = x.shape
    xt = x.reshape(nb * seq, d)
    saved = []
    for l in range(DEPTH):
        xt, sv, _ = _layer_fwd(xt, p, l, nb, seq)
        saved.append(sv)
    loss, dx = _loss_grad(xt, target.reshape(nb * seq, d), "loss")
    grads = [None] * DEPTH
    for l in reversed(range(DEPTH)):
        dx, grads[l], _, _ = _layer_bwd(dx, p, l, saved[l], nb, seq)
    return loss, dx.reshape(nb, seq, d), grads


N_CHIPS = 4


def _mesh_pos():
    return lax.axis_index("x"), lax.axis_index("y"), lax.axis_index("c")


def _other_chips(x, y):
    return [(1 - x, y), (x, 1 - y), (1 - x, 1 - y)]


def _remote(src, dst, send_sem, recv_sem, to):
    return pltpu.make_async_remote_copy(src_ref=src, dst_ref=dst, send_sem=send_sem, recv_sem=recv_sem,
                                        device_id=to, device_id_type=MESH)


def _hbm_call(body, name, ins, out_shape, scratch):
    return pl.pallas_call(
        body, name=name, out_shape=out_shape, in_specs=[HBM] * len(ins),
        out_specs=jax.tree.map(lambda _: HBM, out_shape), scratch_shapes=scratch,
        compiler_params=pltpu.CompilerParams(has_side_effects=True),
    )(*ins)


def _gather_phases(n, layer):
    def copies(ins, outs, sems):
        send1, recv1, send2, recv2 = sems
        x, y, c = _mesh_pos()
        out, back, fwd = [], [], []
        for i in range(n):
            for j, (px, py) in enumerate(_other_chips(x, y)):
                k = 3 * i + j
                blk = outs[i].at[2 * px + py]
                out.append(_remote(ins[i], outs[i].at[2 * x + y], send1.at[k], recv1.at[k], (px, py, c)))
                back.append(_remote(blk, blk, send1.at[k], recv1.at[k], (px, py, c)))
                fwd.append(_remote(blk, blk, send2.at[k], recv2.at[k], (x, y, 1 - c)))
        return c, out, back, fwd

    def first(ins, outs, sems):
        c, out, _, _ = copies(ins, outs, sems)

        @pl.when(c == layer)
        def _():
            for cp in out:
                cp.start()

    def middle(ins, outs, sems):
        c, _, back, fwd = copies(ins, outs, sems)

        @pl.when(c == layer)
        def _():
            for arrived, onward in zip(back, fwd):
                arrived.wait_recv()
                onward.start()

    def last(ins, outs, sems):
        c, out, _, fwd = copies(ins, outs, sems)

        @pl.when(c == layer)
        def _():
            for cp in out + fwd:
                cp.wait_send()

        @pl.when(c != layer)
        def _():
            for cp in fwd:
                cp.wait_recv()

    return first, middle, last


def _scatter_phases(n, layer):
    def copies(ins, outs, sems):
        send, recv = sems
        x, y, c = _mesh_pos()
        return c, [_remote(ins[i].at[2 * px + py], outs[i].at[j], send.at[3 * i + j], recv.at[3 * i + j], (px, py, c))
                   for i in range(n) for j, (px, py) in enumerate(_other_chips(x, y))]

    def first(ins, outs, sems):
        c, cps = copies(ins, outs, sems)

        @pl.when(c == layer)
        def _():
            for cp in cps:
                cp.start()

    def middle(ins, outs, sems):
        pass

    def last(ins, outs, sems):
        c, cps = copies(ins, outs, sems)

        @pl.when(c == layer)
        def _():
            for cp in cps:
                cp.wait()

    return first, middle, last


def _exchange(blocks, out_shapes, n_sems, phases, name, rider):
    sems = [pltpu.SemaphoreType.DMA((3 * len(blocks),))] * n_sems
    if rider:
        return _Rider(blocks, out_shapes, sems, phases)
    n = len(blocks)

    def body(*refs):
        for phase in phases:
            phase(refs[:n], refs[n:2 * n], refs[2 * n:])

    return list(_hbm_call(body, name, blocks, out_shapes, sems))


def _gather_layer(blocks, layer, name=None, rider=False):
    outs = [jax.ShapeDtypeStruct((N_CHIPS,) + b.shape, b.dtype) for b in blocks]
    return _exchange(blocks, outs, 4, _gather_phases(len(blocks), layer), name, rider)


def _scatter_layer(sums, layer, name=None, rider=False):
    outs = [jax.ShapeDtypeStruct((3,) + s.shape[1:], s.dtype) for s in sums]
    return _exchange(sums, outs, 2, _scatter_phases(len(sums), layer), name, rider)


def _to_sibling(gs, layer, name):
    n = len(gs)

    def body(*refs):
        ins, outs = refs[:n], refs[n:2 * n]
        send, recv = refs[2 * n:]
        x, y, c = _mesh_pos()
        cps = [_remote(ins[i], outs[i], send.at[i], recv.at[i], (x, y, 1 - c)) for i in range(n)]

        @pl.when(c != layer)
        def _():
            for cp in cps:
                cp.start()
            for cp in cps:
                cp.wait_send()

        @pl.when(c == layer)
        def _():
            for cp in cps:
                cp.wait_recv()

    sem = pltpu.SemaphoreType.DMA((n,))
    return list(_hbm_call(body, name, gs, [jax.ShapeDtypeStruct(g.shape, g.dtype) for g in gs], [sem, sem]))


def _sibling_swap(rs, name):
    n = len(rs)

    def body(*refs):
        ins, outs = refs[:n], refs[n:2 * n]
        send, recv = refs[2 * n:]
        x, y, c = _mesh_pos()
        cps = [_remote(ins[i], outs[i], send.at[i], recv.at[i], (x, y, 1 - c)) for i in range(n)]
        for cp in cps:
            cp.start()
        for cp in cps:
            cp.wait()

    sem = pltpu.SemaphoreType.DMA((n,))
    return _hbm_call(body, name, rs, [jax.ShapeDtypeStruct(r.shape, r.dtype) for r in rs], [sem, sem])


def _small_all_reduce(vec, name):
    r = vec.shape[0]
    ndev = 8

    def body(v_ref, o_ref, buf, send, recv):
        x, y, c = _mesh_pos()
        me = 4 * x + 2 * y + c
        buf[me] = v_ref[...]
        cps = []
        for rel in range(1, ndev):
            px = 1 - x if rel & 4 else x
            py = 1 - y if rel & 2 else y
            pc = 1 - c if rel & 1 else c
            cps.append((_remote(v_ref, buf.at[me], send.at[rel - 1], recv.at[rel - 1], (px, py, pc)),
                        4 * px + 2 * py + pc))
        for cp, _ in cps:
            cp.start()
        for k, (cp, peer) in enumerate(cps):
            slot = buf.at[peer]
            _remote(slot, slot, send.at[k], recv.at[k], (x, y, c)).wait_recv()
        for cp, _ in cps:
            cp.wait_send()
        acc = buf[0]
        for k in range(1, ndev):
            acc = acc + buf[k]
        o_ref[...] = acc

    vm = pl.BlockSpec(memory_space=pltpu.VMEM)
    return pl.pallas_call(
        body, name=name, out_shape=jax.ShapeDtypeStruct(vec.shape, F32), in_specs=[vm], out_specs=vm,
        scratch_shapes=[pltpu.VMEM((ndev, r, LANES), F32), pltpu.SemaphoreType.DMA((ndev - 1,)),
                        pltpu.SemaphoreType.DMA((ndev - 1,))],
        compiler_params=pltpu.CompilerParams(has_side_effects=True),
    )(vec)


def _row_tile(rows, cap=512):
    for t in range(min(rows, cap), 0, -1):
        if rows % t == 0 and (t % 16 == 0 or t == rows):
            return t
    raise ValueError(rows)


def _add_pairs(a, b, name):
    k, r, c = a.shape
    tr = _row_tile(r)

    def body(a_ref, b_ref, o_ref):
        o_ref[...] = (a_ref[...].astype(F32) + b_ref[...].astype(F32)).astype(o_ref.dtype)

    spec = pl.BlockSpec((None, tr, c), lambda i, j: (i, j, 0))
    return pl.pallas_call(body, name=name, grid=(k, r // tr), in_specs=[spec, spec], out_specs=spec,
                          out_shape=jax.ShapeDtypeStruct(a.shape, a.dtype),
                          compiler_params=_params(("parallel", "parallel")))(a, b)


def _final_sum(own, sib, others, name):
    r, c = own.shape
    tr = _row_tile(r)

    def body(a_ref, b_ref, o_ref_in, out_ref):
        acc = a_ref[...].astype(F32) + b_ref[...].astype(F32)
        for k in range(3):
            acc = acc + o_ref_in[k].astype(F32)
        out_ref[...] = acc

    spec = pl.BlockSpec((tr, c), lambda i: (i, 0))
    return pl.pallas_call(body, name=name, grid=(r // tr,),
                          in_specs=[spec, spec, pl.BlockSpec((3, tr, c), lambda i: (0, i, 0))], out_specs=spec,
                          out_shape=jax.ShapeDtypeStruct((r, c), F32),
                          compiler_params=_params(("parallel",)))(own, sib, others)


def _adamw(g, w, m, v, name):
    r, c = g.shape
    tr = _row_tile(r, 256)

    def body(g_ref, w_ref, m_ref, v_ref, d_ref, mo_ref, vo_ref):
        gv = g_ref[...]
        mn = ADAM_B1 * m_ref[...] + (1.0 - ADAM_B1) * gv
        vn = ADAM_B2 * v_ref[...] + (1.0 - ADAM_B2) * (gv * gv)
        m_hat = mn / (1.0 - ADAM_B1 ** ADAM_STEP)
        v_hat = vn / (1.0 - ADAM_B2 ** ADAM_STEP)
        d_ref[...] = -ADAM_LR * (m_hat / (jnp.sqrt(v_hat) + ADAM_EPS) + ADAM_WD * w_ref[...])
        mo_ref[...] = mn
        vo_ref[...] = vn

    spec = pl.BlockSpec((tr, c), lambda i: (i, 0))
    shp = jax.ShapeDtypeStruct((r, c), F32)
    return pl.pallas_call(body, name=name, grid=(r // tr,), in_specs=[spec] * 4, out_specs=[spec] * 3,
                          out_shape=[shp] * 3, compiler_params=_params(("parallel",)))(g, w, m, v)


def _pack(arrays):
    flat = jnp.concatenate([a.reshape(-1).astype(F32) for a in arrays])
    pad = (-flat.shape[0]) % (8 * LANES)
    return jnp.concatenate([flat, jnp.zeros((pad,), F32)]).reshape(-1, LANES)


def _unpack(packed, shapes):
    flat = packed.reshape(-1)
    out, off = [], 0
    for s in shapes:
        size = 1
        for dim in s:
            size *= dim
        out.append(flat[off:off + size].reshape(s))
        off += size
    return out


BIG = ("ffn1_w_in", "ffn1_w_out", "w_in", "w_out", "ffn2_w_in", "ffn2_w_out")
SMALL = ("ffn1_norm", "mix_norm", "fox_q_norm", "fox_k_norm", "fox_f_bias", "gdn_a_log", "gdn_dt_bias",
         "gdn_out_norm", "ffn2_norm", "gdn_conv")
WEIGHTS = ("ffn1_norm", "ffn1_w_in", "ffn1_w_out", "mix_norm", "w_in", "fox_q_norm", "fox_k_norm", "fox_f_bias",
           "gdn_conv", "gdn_a_log", "gdn_dt_bias", "gdn_out_norm", "w_out", "ffn2_norm", "ffn2_w_in", "ffn2_w_out")


def _step(x, target, w, m, v):
    xi, yi, ci = _mesh_pos()
    me = 2 * xi + yi
    depth = DEPTH
    d = x.shape[-1]

    nb, seq, _ = x.shape
    assert depth == 2

    p = {k: w[k] for k in SMALL if k != "gdn_conv"}
    for k in ("ffn1_w_in", "ffn1_w_out", "ffn2_w_in", "ffn2_w_out", "w_mix", "w_out", "gdn_conv"):
        p[k] = [None] * depth

    first, rest = BIG[:2], BIG[2:] + ("gdn_conv",)

    def shards(l, names):
        return [w[k][l] if k == "gdn_conv" else w[k][l].astype(BF) for k in names]

    def place(l, names, gathered):
        blocks = dict(zip(names, [lax.dynamic_update_index_in_dim(g, s, me, 0)
                                  for g, s in zip(gathered, shards(l, names))]))
        for k in ("ffn1_w_in", "ffn1_w_out", "ffn2_w_in", "ffn2_w_out"):
            if k in blocks:
                p[k][l] = blocks[k]
        if "w_in" in blocks:
            p["w_mix"][l] = _mix_to_padded(blocks["w_in"].transpose(1, 0, 2).reshape(d, N_IN))
            p["w_out"][l] = blocks["w_out"].reshape(2 * FOX_WIDTH, d)
            p["gdn_conv"][l] = blocks["gdn_conv"].transpose(1, 0, 2).reshape(CONV_WIDTH, -1)

    place(0, first, _gather_layer(shards(0, first), 0, "gather_first_ffn0"))
    xt = x.reshape(nb * seq, d)
    xt, saved0, gathered1 = _layer_fwd(
        xt, p, 0, nb, seq, _gather_layer(shards(1, first + rest), 1, rider=True),
        _gather_layer(shards(0, rest), 0, rider=True), lambda got: place(0, rest, got))
    place(1, first + rest, gathered1)
    xt, saved1, _ = _layer_fwd(xt, p, 1, nb, seq)
    loss, dx = _loss_grad(xt, target.reshape(nb * seq, d), "loss")

    def transport(g, names):
        out = []
        for k in names:
            if k == "w_in":
                out.append(g["w_in"].reshape(d, N_CHIPS, N_IN // N_CHIPS).transpose(1, 0, 2).astype(BF))
            elif k == "w_out":
                out.append(g["w_out"].reshape(N_CHIPS, -1, d))
            else:
                out.append(g[k])
        return out

    def chip_sums(g, l, names, tag):
        own = transport(g, names)
        sib = _to_sibling(own, l, f"grad{l}{tag}_to_sibling")
        return own, sib, [_add_pairs(a, b, f"grad{l}{tag}_chip_sum_{k}") for a, b, k in zip(own, sib, names)]

    dx, grads1, _, _ = _layer_bwd(dx, p, 1, saved1, nb, seq)
    own1, sib1, sums1 = chip_sums(grads1, 1, BIG, "")
    before = {}

    def before_ffn1(g):
        before["own"], before["sib"], sums = chip_sums(g, 0, BIG[2:], "_rest")
        return _scatter_layer(sums, 0, rider=True)

    dx, grads0, chips1, chips0_rest = _layer_bwd(dx, p, 0, saved0, nb, seq,
                                                 _scatter_layer(sums1, 1, rider=True), before_ffn1)
    own0, sib0, sums0 = chip_sums(grads0, 0, first, "_first")
    chips0 = _scatter_layer(sums0, 0, "grad0_first_to_chips") + chips0_rest
    own0, sib0 = own0 + before["own"], sib0 + before["sib"]
    grads = [grads0, grads1]
    dx = dx.reshape(nb, seq, d)

    mine = lambda a0, a1: jnp.where(ci == 0, a0, a1)
    at_me = lambda a: lax.dynamic_index_in_dim(a, me, 0, keepdims=False)
    reduced = [_final_sum(mine(at_me(own0[i]), at_me(own1[i])), mine(at_me(sib0[i]), at_me(sib1[i])),
                          mine(chips0[i], chips1[i]), f"grad_final_sum_{k}") for i, k in enumerate(BIG)]
    from_sib_final = _sibling_swap(reduced, "grad_swap_layers")
    full = {k: jnp.stack([jnp.where(ci == 0, a, b), jnp.where(ci == 0, b, a)])
            for k, a, b in zip(BIG, reduced, from_sib_final)}

    out_g, out_d, out_m, out_v = {}, {}, {}, {}
    for k in BIG:
        shp = w[k].shape
        two_d = lambda a: a.reshape(shp[0] * shp[1], shp[2])
        dl, mn, vn = _adamw(two_d(full[k]), two_d(w[k]), two_d(m[k]), two_d(v[k]), f"adamw_{k}")
        out_g[k], out_d[k], out_m[k], out_v[k] = full[k], dl.reshape(shp), mn.reshape(shp), vn.reshape(shp)

    small_local = [jnp.stack([grads[l][k] for l in range(depth)]) for k in SMALL]
    summed = _unpack(_small_all_reduce(_pack(small_local), "small_all_reduce"), [a.shape for a in small_local])
    sg = dict(zip(SMALL, summed))
    cs = w["gdn_conv"].shape[-1]
    sg["gdn_conv"] = lax.dynamic_slice_in_dim(sg["gdn_conv"], me * cs, cs, axis=2)
    shapes = [w[k].shape for k in SMALL]
    packs = [_pack([src[k] for k in SMALL]) for src in (sg, w, m, v)]
    dl, mn, vn = _adamw(*packs, "adamw_small")
    for k, a, b, c2 in zip(SMALL, _unpack(dl, shapes), _unpack(mn, shapes), _unpack(vn, shapes)):
        out_g[k], out_d[k], out_m[k], out_v[k] = sg[k], a, b, c2

    total = lax.psum(loss[0, 0], ("x", "y", "c"))
    return (total, dx, *[out_g[k] for k in WEIGHTS], *[out_d[k] for k in WEIGHTS],
            *[out_m[k] for k in WEIGHTS], *[out_v[k] for k in WEIGHTS])


def kernel(x, ffn1_norm, ffn1_w_in, ffn1_w_out, mix_norm, w_in, fox_q_norm, fox_k_norm, fox_f_bias, gdn_conv, gdn_a_log, gdn_dt_bias, gdn_out_norm, w_out, ffn2_norm, ffn2_w_in, ffn2_w_out, loss_target, m_ffn1_norm, m_ffn1_w_in, m_ffn1_w_out, m_mix_norm, m_w_in, m_fox_q_norm, m_fox_k_norm, m_fox_f_bias, m_gdn_conv, m_gdn_a_log, m_gdn_dt_bias, m_gdn_out_norm, m_w_out, m_ffn2_norm, m_ffn2_w_in, m_ffn2_w_out, v_ffn1_norm, v_ffn1_w_in, v_ffn1_w_out, v_mix_norm, v_w_in, v_fox_q_norm, v_fox_k_norm, v_fox_f_bias, v_gdn_conv, v_gdn_a_log, v_gdn_dt_bias, v_gdn_out_norm, v_w_out, v_ffn2_norm, v_ffn2_w_in, v_ffn2_w_out):
    w = dict(ffn1_norm=ffn1_norm, ffn1_w_in=ffn1_w_in, ffn1_w_out=ffn1_w_out, mix_norm=mix_norm, w_in=w_in,
             fox_q_norm=fox_q_norm, fox_k_norm=fox_k_norm, fox_f_bias=fox_f_bias, gdn_conv=gdn_conv,
             gdn_a_log=gdn_a_log, gdn_dt_bias=gdn_dt_bias, gdn_out_norm=gdn_out_norm, w_out=w_out,
             ffn2_norm=ffn2_norm, ffn2_w_in=ffn2_w_in, ffn2_w_out=ffn2_w_out)
    m = dict(ffn1_norm=m_ffn1_norm, ffn1_w_in=m_ffn1_w_in, ffn1_w_out=m_ffn1_w_out, mix_norm=m_mix_norm, w_in=m_w_in,
             fox_q_norm=m_fox_q_norm, fox_k_norm=m_fox_k_norm, fox_f_bias=m_fox_f_bias, gdn_conv=m_gdn_conv,
             gdn_a_log=m_gdn_a_log, gdn_dt_bias=m_gdn_dt_bias, gdn_out_norm=m_gdn_out_norm, w_out=m_w_out,
             ffn2_norm=m_ffn2_norm, ffn2_w_in=m_ffn2_w_in, ffn2_w_out=m_ffn2_w_out)
    v = dict(ffn1_norm=v_ffn1_norm, ffn1_w_in=v_ffn1_w_in, ffn1_w_out=v_ffn1_w_out, mix_norm=v_mix_norm, w_in=v_w_in,
             fox_q_norm=v_fox_q_norm, fox_k_norm=v_fox_k_norm, fox_f_bias=v_fox_f_bias, gdn_conv=v_gdn_conv,
             gdn_a_log=v_gdn_a_log, gdn_dt_bias=v_gdn_dt_bias, gdn_out_norm=v_gdn_out_norm, w_out=v_w_out,
             ffn2_norm=v_ffn2_norm, ffn2_w_in=v_ffn2_w_in, ffn2_w_out=v_ffn2_w_out)
    return _step(x, loss_target, w, m, v)
```

```python
import jax
import jax.numpy as jnp
from jax import lax
from jax.experimental import pallas as pl
from jax.experimental.pallas import tpu as pltpu

F32 = jnp.float32
BF = jnp.bfloat16
HI = lax.Precision.HIGHEST
MESH = pl.DeviceIdType.MESH

DEPTH = 2
FOX_HEADS = 8
FOX_HEAD_DIM = 64
FOX_WIDTH = 512
GDN_HEADS = 4
GDN_HEAD_DIM = 128
GDN_WIDTH = 512
CONV_WIDTH = 4
CHUNK = 64
EPS = 1e-6
N_IN = 3600
N_PAD = 3712
GATE_COL = 3584
LANES = 128
NEG = -1e30

ADAM_LR = 0.001
ADAM_B1 = 0.9
ADAM_B2 = 0.999
ADAM_EPS = 1e-08
ADAM_WD = 0.01
ADAM_STEP = 10

VMEM_LIMIT = 56 * 1024 * 1024


def _params(sem=None, **kw):
    return pltpu.CompilerParams(dimension_semantics=sem, vmem_limit_bytes=VMEM_LIMIT, **kw)


def _dot(a, b, precision=None):
    return jnp.dot(a, b, preferred_element_type=F32, precision=precision)


def _dot_nt(a, b, precision=None):
    return lax.dot_general(a, b, (((1,), (1,)), ((), ())), preferred_element_type=F32, precision=precision)


def _dot_tn(a, b, precision=None):
    return lax.dot_general(a, b, (((0,), (0,)), ((), ())), preferred_element_type=F32, precision=precision)


def _sigmoid(x):
    return 0.5 * jnp.tanh(0.5 * x) + 0.5


def _softplus(x):
    return jnp.maximum(x, 0.0) + jnp.log(1.0 + jnp.exp(-jnp.abs(x)))


def _log_sigmoid(x):
    return jnp.minimum(x, 0.0) - jnp.log(1.0 + jnp.exp(-jnp.abs(x)))


def _tile(n, t):
    t = min(n, t)
    assert n % t == 0, (n, t)
    return t


def _rms_fwd(x, gain):
    rstd = lax.rsqrt(jnp.mean(x * x, axis=-1, keepdims=True) + EPS)
    xhat = x * rstd
    return xhat * gain, xhat, rstd


def _rms_bwd(dy, xhat, rstd, gain):
    dxhat = dy * gain
    dx = rstd * (dxhat - xhat * jnp.mean(dxhat * xhat, axis=-1, keepdims=True))
    return dx, dy * xhat


def _full(shape):
    nd = len(shape)
    return pl.BlockSpec(shape, lambda *_: (0,) * nd)


HBM = pl.BlockSpec(memory_space=pltpu.HBM)


def _load_ffn_weights(win_hbm, wout_hbm, win_v, wout_v, sem):
    fr = wout_hbm.shape[1]
    copies = [pltpu.make_async_copy(win_hbm.at[s], win_v.at[s], sem.at[s]) for s in range(4)]
    copies += [pltpu.make_async_copy(wout_hbm.at[s], wout_v.at[pl.ds(s * fr, fr)], sem.at[4 + s])
               for s in range(4)]
    for c in copies:
        c.start()
    for c in copies:
        c.wait()


def _ffn_fwd(x, gain, win_g, wout_g, name, rider=None):
    t, d = x.shape
    _, _, fs = win_g.shape
    fr = wout_g.shape[1]
    tm = _tile(t, 512)
    r_in, r_out, r_sem = _rider_parts(rider)
    steps = t // tm

    def body(x_ref, g_ref, win_hbm, wout_hbm, *rest):
        rin, (xo_ref, h_ref) = rest[:len(r_in)], rest[len(r_in):len(r_in) + 2]
        rout = rest[len(r_in) + 2:len(r_in) + 2 + len(r_out)]
        win_v, wout_v, sem = rest[len(r_in) + 2 + len(r_out):len(r_in) + 5 + len(r_out)]
        riding = (rin, rout, rest[len(r_in) + 5 + len(r_out):])
        step = pl.program_id(0)
        _ride(rider, 0, step == 0, riding)
        _ride(rider, 1, step == (3 * steps) // 4, riding)

        @pl.when(step == 0)
        def _():
            _load_ffn_weights(win_hbm, wout_hbm, win_v, wout_v, sem)

        xv = x_ref[...]
        hn, _, _ = _rms_fwd(xv, g_ref[...])
        hn = hn.astype(BF)
        acc = jnp.zeros((tm, d), F32)
        for s in range(2):
            g = _dot(hn, win_v[s])
            u = _dot(hn, win_v[s + 2])
            h_ref[:, s * fs:(s + 1) * fs] = g.astype(BF)
            h_ref[:, (s + 2) * fs:(s + 3) * fs] = u.astype(BF)
            a = (g * _sigmoid(g) * u).astype(BF)
            acc = acc + _dot(a, wout_v[s * fs:(s + 1) * fs, :])
        xo_ref[...] = xv + 0.5 * acc
        _ride(rider, 2, step == steps - 1, riding)

    return pl.pallas_call(
        body, name=name, grid=(steps,),
        in_specs=[pl.BlockSpec((tm, d), lambda i: (i, 0)), _full((1, d)), HBM, HBM] + [HBM] * len(r_in),
        out_specs=[pl.BlockSpec((tm, d), lambda i: (i, 0)), pl.BlockSpec((tm, 4 * fs), lambda i: (i, 0))]
        + [HBM] * len(r_out),
        out_shape=[jax.ShapeDtypeStruct((t, d), F32), jax.ShapeDtypeStruct((t, 4 * fs), BF)] + r_out,
        scratch_shapes=[pltpu.VMEM((4, d, fs), BF), pltpu.VMEM((4 * fr, d), BF), pltpu.SemaphoreType.DMA((8,))]
        + r_sem,
        compiler_params=_params(("arbitrary",), has_side_effects=rider is not None),
    )(x, gain, win_g, wout_g, *r_in)


def _ffn_bwd(dy, x, h, gain, win_g, wout_g, name, rider=None):
    t, d = x.shape
    _, _, fs = win_g.shape
    fr = wout_g.shape[1]
    tm = _tile(t, 256)
    r_in, r_out, r_sem = _rider_parts(rider)
    steps = t // tm

    def body(dy_ref, x_ref, h_ref, g_ref, win_hbm, wout_hbm, *rest):
        rin, (dx_ref, dh_ref, a_ref, hn_ref, dyh_ref, dg_ref) = rest[:len(r_in)], rest[len(r_in):len(r_in) + 6]
        rout = rest[len(r_in) + 6:len(r_in) + 6 + len(r_out)]
        win_v, wout_v, sem = rest[len(r_in) + 6 + len(r_out):len(r_in) + 9 + len(r_out)]
        riding = (rin, rout, rest[len(r_in) + 9 + len(r_out):])
        step = pl.program_id(0)
        _ride(rider, 0, step == 0, riding)
        _ride(rider, 1, step == (3 * steps) // 4, riding)

        @pl.when(step == 0)
        def _():
            _load_ffn_weights(win_hbm, wout_hbm, win_v, wout_v, sem)
            dg_ref[...] = jnp.zeros_like(dg_ref)

        dyv = dy_ref[...]
        dyh = (0.5 * dyv).astype(BF)
        dyh_ref[...] = dyh
        dhn = jnp.zeros((tm, d), F32)
        for s in range(2):
            da = _dot_nt(dyh, wout_v[s * fs:(s + 1) * fs, :])
            g = h_ref[:, s * fs:(s + 1) * fs].astype(F32)
            u = h_ref[:, (s + 2) * fs:(s + 3) * fs].astype(F32)
            sg = _sigmoid(g)
            si = g * sg
            a_ref[:, s * fs:(s + 1) * fs] = (si * u).astype(BF)
            dgate = (da * u * (sg * (1.0 + g * (1.0 - sg)))).astype(BF)
            dup = (da * si).astype(BF)
            dh_ref[:, s * fs:(s + 1) * fs] = dgate
            dh_ref[:, (s + 2) * fs:(s + 3) * fs] = dup
            dhn = dhn + _dot_nt(dgate, win_v[s]) + _dot_nt(dup, win_v[s + 2])
        xv = x_ref[...]
        gain_v = g_ref[...]
        hn, xhat, rstd = _rms_fwd(xv, gain_v)
        hn_ref[...] = hn.astype(BF)
        dx, dgr = _rms_bwd(dhn, xhat, rstd, gain_v)
        dx_ref[...] = dyv + dx
        dg_ref[...] += jnp.sum(dgr, axis=0, keepdims=True)
        _ride(rider, 2, step == steps - 1, riding)

    row = lambda w: pl.BlockSpec((tm, w), lambda i: (i, 0))
    return pl.pallas_call(
        body, name=name, grid=(steps,),
        in_specs=[row(d), row(d), row(4 * fs), _full((1, d)), HBM, HBM] + [HBM] * len(r_in),
        out_specs=[row(d), row(4 * fs), row(2 * fs), row(d), row(d), _full((1, d))] + [HBM] * len(r_out),
        out_shape=[jax.ShapeDtypeStruct((t, d), F32), jax.ShapeDtypeStruct((t, 4 * fs), BF),
                   jax.ShapeDtypeStruct((t, 2 * fs), BF), jax.ShapeDtypeStruct((t, d), BF),
                   jax.ShapeDtypeStruct((t, d), BF), jax.ShapeDtypeStruct((1, d), F32)] + r_out,
        scratch_shapes=[pltpu.VMEM((4, d, fs), BF), pltpu.VMEM((4 * fr, d), BF), pltpu.SemaphoreType.DMA((8,))]
        + r_sem,
        compiler_params=_params(("arbitrary",), has_side_effects=rider is not None),
    )(dy, x, h, gain, win_g, wout_g, *r_in)


def _wgrad(a, b, out_shape, out_spec, tm, tn, name, tk=512):
    t, m = a.shape
    _, n = b.shape
    tk = _tile(t, tk)
    nk = t // tk

    def body(a_ref, b_ref, o_ref, acc):
        k = pl.program_id(2)

        @pl.when(k == 0)
        def _():
            acc[...] = jnp.zeros_like(acc)

        acc[...] += _dot_tn(a_ref[...], b_ref[...])

        @pl.when(k == nk - 1)
        def _():
            o_ref[...] = acc[...].astype(o_ref.dtype)

    return pl.pallas_call(
        body, name=name, grid=(m // tm, n // tn, nk),
        in_specs=[pl.BlockSpec((tk, tm), lambda i, j, k: (k, i)), pl.BlockSpec((tk, tn), lambda i, j, k: (k, j))],
        out_specs=out_spec, out_shape=out_shape,
        scratch_shapes=[pltpu.VMEM((tm, tn), F32)],
        compiler_params=_params(("parallel", "parallel", "arbitrary")),
    )(a, b)


def _norm_matmul(x, gain, w, name):
    t, d = x.shape
    n = w.shape[1]
    tm = _tile(t, 256)

    def body(x_ref, g_ref, w_ref, o_ref):
        hn, _, _ = _rms_fwd(x_ref[...], g_ref[...])
        o_ref[...] = _dot(hn.astype(BF), w_ref[...])

    return pl.pallas_call(
        body, name=name, grid=(t // tm,),
        in_specs=[pl.BlockSpec((tm, d), lambda i: (i, 0)), _full((1, d)), _full((d, n))],
        out_specs=pl.BlockSpec((tm, n), lambda i: (i, 0)),
        out_shape=jax.ShapeDtypeStruct((t, n), F32),
        compiler_params=_params(("parallel",)),
    )(x, gain, w)


def _norm_matmul_bwd(dres, dparts, x, gain, w, name):
    t, d = x.shape
    n = w.shape[1]
    tm = _tile(t, 256)
    widths = [a.shape[1] for a in dparts]
    assert sum(widths) == n
    k = len(dparts)

    def body(dr_ref, *rest):
        dp_refs, (x_ref, g_ref, w_ref, dx_ref, hn_ref, dg_ref) = rest[:k], rest[k:]

        @pl.when(pl.program_id(0) == 0)
        def _():
            dg_ref[...] = jnp.zeros_like(dg_ref)

        dhn, off = jnp.zeros((tm, d), F32), 0
        for dp_ref, wd in zip(dp_refs, widths):
            dhn = dhn + _dot_nt(dp_ref[...], w_ref[:, off:off + wd])
            off += wd
        gain_v = g_ref[...]
        hn, xhat, rstd = _rms_fwd(x_ref[...], gain_v)
        hn_ref[...] = hn.astype(BF)
        dx, dgr = _rms_bwd(dhn, xhat, rstd, gain_v)
        dx_ref[...] = dr_ref[...] + dx
        dg_ref[...] += jnp.sum(dgr, axis=0, keepdims=True)

    row = lambda wd: pl.BlockSpec((tm, wd), lambda i: (i, 0))
    return pl.pallas_call(
        body, name=name, grid=(t // tm,),
        in_specs=[row(d)] + [row(wd) for wd in widths] + [row(d), _full((1, d)), _full((d, n))],
        out_specs=[row(d), row(d), _full((1, d))],
        out_shape=[jax.ShapeDtypeStruct((t, d), F32), jax.ShapeDtypeStruct((t, d), BF),
                   jax.ShapeDtypeStruct((1, d), F32)],
        compiler_params=_params(("arbitrary",)),
    )(dres, *dparts, x, gain, w)


def _mix_out(x, yf, yg, w, name):
    t, d = x.shape
    kf = yf.shape[1]
    tm = _tile(t, 512)

    def body(x_ref, yf_ref, yg_ref, w_ref, o_ref):
        o_ref[...] = x_ref[...] + _dot(yf_ref[...], w_ref[0:kf, :]) + _dot(yg_ref[...], w_ref[kf:2 * kf, :])

    row = lambda wd: pl.BlockSpec((tm, wd), lambda i: (i, 0))
    return pl.pallas_call(
        body, name=name, grid=(t // tm,),
        in_specs=[row(d), row(kf), row(kf), _full((2 * kf, d))],
        out_specs=row(d), out_shape=jax.ShapeDtypeStruct((t, d), F32),
        compiler_params=_params(("parallel",)),
    )(x, yf, yg, w)


def _mix_out_bwd(dx, w, name):
    t, d = dx.shape
    kf = w.shape[0] // 2
    tm = _tile(t, 512)

    def body(dx_ref, w_ref, df_ref, dg_ref, dxb_ref):
        dxb = dx_ref[...].astype(BF)
        dxb_ref[...] = dxb
        df_ref[...] = _dot_nt(dxb, w_ref[0:kf, :]).astype(BF)
        dg_ref[...] = _dot_nt(dxb, w_ref[kf:2 * kf, :]).astype(BF)

    row = lambda wd: pl.BlockSpec((tm, wd), lambda i: (i, 0))
    return pl.pallas_call(
        body, name=name, grid=(t // tm,),
        in_specs=[row(d), _full((2 * kf, d))],
        out_specs=[row(kf), row(kf), row(d)],
        out_shape=[jax.ShapeDtypeStruct((t, kf), BF), jax.ShapeDtypeStruct((t, kf), BF),
                   jax.ShapeDtypeStruct((t, d), BF)],
        compiler_params=_params(("parallel",)),
    )(dx, w)


def _loss_grad(y, target, name):
    t, d = y.shape
    tm = _tile(t, 512)

    def body(y_ref, t_ref, l_ref, dy_ref):
        @pl.when(pl.program_id(0) == 0)
        def _():
            l_ref[...] = jnp.zeros_like(l_ref)

        diff = y_ref[...] - t_ref[...]
        dy_ref[...] = diff * (1.0 / d)
        part = jnp.sum(jnp.sum(diff * diff, axis=1, keepdims=True), axis=0, keepdims=True)
        l_ref[...] += part * (0.5 / d)

    row = pl.BlockSpec((tm, d), lambda i: (i, 0))
    return pl.pallas_call(
        body, name=name, grid=(t // tm,),
        in_specs=[row, row], out_specs=[_full((1, 1)), row],
        out_shape=[jax.ShapeDtypeStruct((1, 1), F32), jax.ShapeDtypeStruct((t, d), F32)],
        compiler_params=_params(("arbitrary",)),
    )(y, target)


def _head_sum_matrix(width, head):
    r = lax.broadcasted_iota(jnp.int32, (width, width), 0) // head
    c = lax.broadcasted_iota(jnp.int32, (width, width), 1) // head
    return (r == c).astype(BF)


def _head_mean(x, bd):
    return _dot(x.astype(BF), bd) * (1.0 / FOX_HEAD_DIM)


def _mask_dot(mask01, x):
    mb = mask01.astype(BF)
    hi = x.astype(BF)
    r1 = x - hi.astype(F32)
    mid = r1.astype(BF)
    lo = (r1 - mid.astype(F32)).astype(BF)
    return _dot(mb, hi) + _dot(mb, mid) + _dot(mb, lo)


def _fox_prep(proj, wq_t, wk_t, bias_pad, seq, name):
    t = proj.shape[0]
    ts = _tile(seq, 512)
    tpe = seq // ts
    scale = FOX_HEAD_DIM ** -0.5

    def body(q_ref, k_ref, v_ref, gt_ref, wq_ref, wk_ref, b_ref, qo_ref, ko_ref, vo_ref, cum_ref, carry):
        i = pl.program_id(0)
        bd = _head_sum_matrix(FOX_WIDTH, FOX_HEAD_DIM)

        def norm(xv, wv):
            ms = _head_mean(xv * xv, bd)
            return xv * lax.rsqrt(ms + EPS) * wv

        qo_ref[...] = (norm(q_ref[...], wq_ref[...]) * scale).astype(BF)
        ko_ref[...] = norm(k_ref[...], wk_ref[...]).astype(BF)
        vo_ref[...] = v_ref[...].astype(BF)

        @pl.when(i % tpe == 0)
        def _():
            carry[...] = jnp.zeros_like(carry)

        ls = _log_sigmoid(gt_ref[...] + b_ref[...])
        r = lax.broadcasted_iota(jnp.int32, (ts, ts), 0)
        c = lax.broadcasted_iota(jnp.int32, (ts, ts), 1)
        cum = _mask_dot(r >= c, ls) + carry[...]
        cum_ref[...] = cum
        carry[...] = cum[ts - 1:ts, :]

    blk = lambda j: pl.BlockSpec((ts, FOX_WIDTH), lambda i: (i, j))
    gate = pl.BlockSpec((ts, LANES), lambda i: (i, GATE_COL // LANES))
    out = pl.BlockSpec((ts, FOX_WIDTH), lambda i: (i, 0))
    return pl.pallas_call(
        body, name=name, grid=(t // ts,),
        in_specs=[blk(0), blk(1), blk(2), gate, _full((1, FOX_WIDTH)), _full((1, FOX_WIDTH)), _full((1, LANES))],
        out_specs=[out, out, out, pl.BlockSpec((ts, LANES), lambda i: (i, 0))],
        out_shape=[jax.ShapeDtypeStruct((t, FOX_WIDTH), BF)] * 3 + [jax.ShapeDtypeStruct((t, LANES), F32)],
        scratch_shapes=[pltpu.VMEM((1, LANES), F32)],
        compiler_params=_params(("arbitrary",)),
    )(proj, proj, proj, proj, wq_t, wk_t, bias_pad)


def _pick_lanes(x, lane_in_block, first_out_lane):
    r = lax.broadcasted_iota(jnp.int32, (FOX_WIDTH, LANES), 0)
    c = lax.broadcasted_iota(jnp.int32, (FOX_WIDTH, LANES), 1)
    sel = ((r % LANES == lane_in_block) & (c == first_out_lane + 2 * (r // LANES))).astype(BF)
    hi = x.astype(BF)
    r1 = x - hi.astype(F32)
    mid = r1.astype(BF)
    lo = (r1 - mid.astype(F32)).astype(BF)
    return _dot(hi, sel) + _dot(mid, sel) + _dot(lo, sel)


def _fox_prep_bwd(proj, dqa, dqb, dk, dv, dkx, wq_t, wk_t, bias_pad, seq, name):
    t = proj.shape[0]
    ts = _tile(seq, 512)
    tpe = seq // ts
    nt = t // ts
    scale = FOX_HEAD_DIM ** -0.5

    def body(q_ref, k_ref, gt_ref, dqa_ref, dqb_ref, dk_ref, dv_ref, dc_ref, wq_ref, wk_ref, b_ref,
             dp_ref, dff_ref, dwq_ref, dwk_ref, db_ref, carry):
        i = pl.program_id(0)
        first = (lax.broadcasted_iota(jnp.int32, (ts, FOX_WIDTH), 1) % LANES) < FOX_HEAD_DIM
        dq_all = jnp.where(first, dqa_ref[...], dqb_ref[...])
        ti = nt - 1 - i
        bd = _head_sum_matrix(FOX_WIDTH, FOX_HEAD_DIM)

        @pl.when(i == 0)
        def _():
            dwq_ref[...] = jnp.zeros_like(dwq_ref)
            dwk_ref[...] = jnp.zeros_like(dwk_ref)
            db_ref[...] = jnp.zeros_like(db_ref)

        def norm_bwd(xv, wv, dyv):
            ms = _head_mean(xv * xv, bd)
            rstd = lax.rsqrt(ms + EPS)
            xhat = xv * rstd
            dxhat = dyv * wv
            mean = _head_mean(dxhat * xhat, bd)
            return rstd * (dxhat - xhat * mean), jnp.sum(dyv * xhat, axis=0, keepdims=True)

        dxq, dwq = norm_bwd(q_ref[...], wq_ref[...], dq_all * scale)
        dxk, dwk = norm_bwd(k_ref[...], wk_ref[...], dk_ref[...])
        dp_ref[:, 0:FOX_WIDTH] = dxq.astype(BF)
        dp_ref[:, FOX_WIDTH:2 * FOX_WIDTH] = dxk.astype(BF)
        dp_ref[:, 2 * FOX_WIDTH:3 * FOX_WIDTH] = dv_ref[...].astype(BF)
        dwq_ref[...] += dwq
        dwk_ref[...] += dwk

        @pl.when(ti % tpe == tpe - 1)
        def _():
            carry[...] = jnp.zeros_like(carry)

        r = lax.broadcasted_iota(jnp.int32, (ts, ts), 0)
        c = lax.broadcasted_iota(jnp.int32, (ts, ts), 1)
        dkx = dc_ref[...]
        hd = FOX_HEAD_DIM
        dcum = (_pick_lanes(dqa_ref[...], hd, 0) + _pick_lanes(dqb_ref[...], 0, 1)
                - _pick_lanes(dkx, hd, 0) - _pick_lanes(dkx, 0, 1))
        dls = _mask_dot(c >= r, dcum) + carry[...]
        carry[...] = dls[0:1, :]
        z = gt_ref[...] + b_ref[...]
        lane = lax.broadcasted_iota(jnp.int32, (ts, LANES), 1)
        dff = jnp.where(lane < FOX_HEADS, dls * _sigmoid(-z), 0.0)
        dff_ref[...] = dff
        db_ref[...] += jnp.sum(dff, axis=0, keepdims=True)

        @pl.when(i == nt - 1)
        def _():
            fr = lax.broadcasted_iota(jnp.int32, (FOX_WIDTH, FOX_WIDTH), 0) % FOX_HEAD_DIM
            fc = lax.broadcasted_iota(jnp.int32, (FOX_WIDTH, FOX_WIDTH), 1) % FOX_HEAD_DIM
            fold = (fr == fc).astype(F32)
            dwq_ref[...] = _dot(dwq_ref[...], fold, HI)
            dwk_ref[...] = _dot(dwk_ref[...], fold, HI)

    rev = lambda w, j: pl.BlockSpec((ts, w), lambda i: (nt - 1 - i, j))
    return pl.pallas_call(
        body, name=name, grid=(nt,),
        in_specs=[rev(FOX_WIDTH, 0), rev(FOX_WIDTH, 1), rev(LANES, GATE_COL // LANES),
                  rev(FOX_WIDTH, 0), rev(FOX_WIDTH, 0), rev(FOX_WIDTH, 0), rev(FOX_WIDTH, 0), rev(FOX_WIDTH, 0),
                  _full((1, FOX_WIDTH)), _full((1, FOX_WIDTH)), _full((1, LANES))],
        out_specs=[rev(3 * FOX_WIDTH, 0), rev(LANES, 0), _full((1, FOX_WIDTH)), _full((1, FOX_WIDTH)),
                   _full((1, LANES))],
        out_shape=[jax.ShapeDtypeStruct((t, 3 * FOX_WIDTH), BF), jax.ShapeDtypeStruct((t, LANES), F32),
                   jax.ShapeDtypeStruct((1, FOX_WIDTH), F32), jax.ShapeDtypeStruct((1, FOX_WIDTH), F32),
                   jax.ShapeDtypeStruct((1, LANES), F32)],
        scratch_shapes=[pltpu.VMEM((1, LANES), F32)],
        compiler_params=_params(("arbitrary",)),
    )(proj, proj, proj, dqa, dqb, dk, dv, dkx, wq_t, wk_t, bias_pad)


class _Rider:
    def __init__(self, inputs, out_shapes, sems, phases):
        self.inputs, self.out_shapes, self.sems, self.phases = list(inputs), list(out_shapes), list(sems), phases


def _rider_parts(rider):
    if rider is None:
        return [], [], []
    return rider.inputs, rider.out_shapes, rider.sems


def _ride(rider, which, when, refs):
    if rider is not None:
        @pl.when(when)
        def _():
            rider.phases[which](*refs)


def _fox_attn(q, k, v, ck, nb, seq, name, rider=None):
    t = q.shape[0]
    tq = _tile(seq, 2048)
    nq = seq // tq
    npair = FOX_HEADS // 2
    hd = FOX_HEAD_DIM
    r_in, r_out, r_sem = _rider_parts(rider)
    steps = nb * npair * nq

    def body(q_ref, k_ref, v_ref, ck_ref, *rest):
        rin, (o_ref, lse_ref) = rest[:len(r_in)], rest[len(r_in):len(r_in) + 2]
        rout = rest[len(r_in) + 2:len(r_in) + 2 + len(r_out)]
        m_s, acc_s = rest[len(r_in) + 2 + len(r_out):len(r_in) + 4 + len(r_out)]
        riding = (rin, rout, rest[len(r_in) + 4 + len(r_out):])
        step = (pl.program_id(0) * npair + pl.program_id(1)) * nq + pl.program_id(2)
        _ride(rider, 0, step == 0, riding)
        _ride(rider, 1, step == (3 * steps) // 4, riding)
        qi = pl.program_id(2)
        lane = lax.broadcasted_iota(jnp.int32, (tq, LANES), 1)
        m_s[...] = jnp.full(m_s.shape, NEG, F32)
        acc_s[...] = jnp.zeros_like(acc_s)
        qv = q_ref[...]

        def tile(kj, on_diagonal):
            cols = pl.ds(pl.multiple_of(kj * tq, tq), tq)
            kv = k_ref[cols, :]
            vv = v_ref[cols, :]
            if on_diagonal:
                causal = (lax.broadcasted_iota(jnp.int32, (tq, tq), 0)
                          >= lax.broadcasted_iota(jnp.int32, (tq, tq), 1))
            ck = [ck_ref[hh, :, cols] for hh in range(2)]
            m_old = [m_s[hh] for hh in range(2)]
            acc_old = [acc_s[hh] for hh in range(2)]
            m_out, acc_out = [], []
            for hh in range(2):
                hm = (lane >= hd) if hh else (lane < hd)
                qh = jnp.where(hm, qv, jnp.zeros_like(qv))
                s = _dot_nt(qh, kv) - ck[hh]
                if on_diagonal:
                    s = jnp.where(causal, s, NEG)
                m_new = jnp.maximum(m_old[hh], jnp.max(s, axis=-1, keepdims=True))
                p = jnp.exp(s - m_new)
                alpha = jnp.exp(m_old[hh] - m_new)
                m_out.append(m_new)
                acc_out.append(alpha * acc_old[hh] + _dot(p.astype(BF), jnp.where(hm, vv, jnp.ones_like(vv))))
            for hh in range(2):
                m_s[hh] = m_out[hh]
                acc_s[hh] = acc_out[hh]

        def off_diagonal(kj, carry):
            tile(kj, False)
            return carry

        lax.fori_loop(0, qi, off_diagonal, 0)
        tile(qi, True)
        a0 = acc_s[0]
        a1 = acc_s[1]
        den = jnp.where(lane < hd, pltpu.roll(a0, hd, axis=1), pltpu.roll(a1, hd, axis=1))
        o_ref[...] = (jnp.where(lane < hd, a0, a1) / den).astype(o_ref.dtype)
        l0 = jnp.sum(jnp.where(lane == hd, a0, 0.0), axis=1, keepdims=True)
        l1 = jnp.sum(jnp.where(lane == 0, a1, 0.0), axis=1, keepdims=True)
        lse_ref[0] = m_s[0] + jnp.log(l0)
        lse_ref[1] = m_s[1] + jnp.log(l1)
        _ride(rider, 2, step == steps - 1, riding)

    qspec = pl.BlockSpec((tq, LANES), lambda b, p, i: (b * nq + i, p))
    kspec = pl.BlockSpec((seq, LANES), lambda b, p, i: (b, p))
    colspec = pl.BlockSpec((None, 2, tq, 1), lambda b, p, i: (b * npair + p, 0, i, 0))
    rowspec = pl.BlockSpec((None, 2, 1, seq), lambda b, p, i: (b * npair + p, 0, 0, 0))
    sem = ("arbitrary",) * 3 if rider else ("parallel",) * 3
    return pl.pallas_call(
        body, name=name, grid=(nb, npair, nq),
        in_specs=[qspec, kspec, kspec, rowspec] + [HBM] * len(r_in),
        out_specs=[qspec, colspec] + [HBM] * len(r_out),
        out_shape=[jax.ShapeDtypeStruct((t, FOX_WIDTH), BF), jax.ShapeDtypeStruct((nb * npair, 2, seq, 1), F32)]
        + r_out,
        scratch_shapes=[pltpu.VMEM((2, tq, 1), F32), pltpu.VMEM((2, tq, LANES), F32)] + r_sem,
        compiler_params=_params(sem, has_side_effects=rider is not None),
    )(q, k, v, ck, *r_in)


def _fox_attn_bwd(q, k, v, o, do, lse, ck, nb, seq, name, rider=None):
    t = q.shape[0]
    tq = _tile(seq, 1024)
    nq = seq // tq
    npair = FOX_HEADS // 2
    hd = FOX_HEAD_DIM
    r_in, r_out, r_sem = _rider_parts(rider)
    steps = nb * npair * nq

    def body(q_ref, k_ref, v_ref, o_ref, do_ref, lse_ref, ck_ref, *rest):
        rin, (dqa_ref, dqb_ref, dk_ref, dv_ref, dkx_ref) = rest[:len(r_in)], rest[len(r_in):len(r_in) + 5]
        rout = rest[len(r_in) + 5:len(r_in) + 5 + len(r_out)]
        dk_s, dv_s = rest[len(r_in) + 5 + len(r_out):len(r_in) + 7 + len(r_out)]
        riding = (rin, rout, rest[len(r_in) + 7 + len(r_out):])
        step = (pl.program_id(0) * npair + pl.program_id(1)) * nq + pl.program_id(2)
        _ride(rider, 0, step == 0, riding)
        _ride(rider, 1, step == (3 * steps) // 4, riding)
        kj = pl.program_id(2)
        lane = lax.broadcasted_iota(jnp.int32, (tq, LANES), 1)

        @pl.when(kj == 0)
        def _():
            dqa_ref[...] = jnp.zeros_like(dqa_ref)
            dqb_ref[...] = jnp.zeros_like(dqb_ref)

        dk_s[...] = jnp.zeros_like(dk_s)
        dv_s[...] = jnp.zeros_like(dv_s)
        kv = k_ref[...]
        vv = v_ref[...]

        def tile(qi, on_diagonal):
            rows = pl.ds(pl.multiple_of(qi * tq, tq), tq)
            qv = q_ref[rows, :]
            dov = do_ref[rows, :]
            prod = dov.astype(F32) * o_ref[rows, :].astype(F32)
            if on_diagonal:
                causal = (lax.broadcasted_iota(jnp.int32, (tq, tq), 0)
                          >= lax.broadcasted_iota(jnp.int32, (tq, tq), 1))
            for hh, dq_ref in ((0, dqa_ref), (1, dqb_ref)):
                hm = (lane >= hd) if hh else (lane < hd)
                zero = jnp.zeros_like(qv)
                one = jnp.ones_like(qv)
                doh = jnp.where(hm, dov, zero)
                delta = jnp.sum(jnp.where(hm, prod, 0.0), axis=-1, keepdims=True)
                s = _dot_nt(jnp.where(hm, qv, zero), kv) - ck_ref[hh]
                if on_diagonal:
                    s = jnp.where(causal, s, NEG)
                p = jnp.exp(s - lse_ref[hh, rows, :])
                dp = _dot_nt(doh, vv)
                dsb = (p * (dp - delta)).astype(BF)
                dv_s[...] += _dot_tn(p.astype(BF), doh)
                dk_s[hh] += _dot_tn(dsb, jnp.where(hm, qv, one))
                dq_ref[rows, :] += _dot(dsb, jnp.where(hm, kv, one))

        def off_diagonal(qi, carry):
            tile(qi, False)
            return carry

        tile(kj, True)
        lax.fori_loop(kj + 1, nq, off_diagonal, 0)
        dk_ref[...] = jnp.where(lane < hd, dk_s[0], dk_s[1])
        dkx_ref[...] = jnp.where(lane < hd, dk_s[1], dk_s[0])
        dv_ref[...] = dv_s[...]
        _ride(rider, 2, step == steps - 1, riding)

    kspec = pl.BlockSpec((tq, LANES), lambda b, p, j: (b * nq + j, p))
    full_q = pl.BlockSpec((seq, LANES), lambda b, p, j: (b, p))
    colspec = pl.BlockSpec((None, 2, seq, 1), lambda b, p, j: (b * npair + p, 0, 0, 0))
    rowspec = pl.BlockSpec((None, 2, 1, tq), lambda b, p, j: (b * npair + p, 0, 0, j))
    sem = ("arbitrary",) * 3 if rider else ("parallel", "parallel", "arbitrary")
    return pl.pallas_call(
        body, name=name, grid=(nb, npair, nq),
        in_specs=[full_q, kspec, kspec, full_q, full_q, colspec, rowspec] + [HBM] * len(r_in),
        out_specs=[full_q, full_q, kspec, kspec, kspec] + [HBM] * len(r_out),
        out_shape=[jax.ShapeDtypeStruct((t, FOX_WIDTH), F32)] * 5 + r_out,
        scratch_shapes=[pltpu.VMEM((2, tq, LANES), F32), pltpu.VMEM((tq, LANES), F32)] + r_sem,
        compiler_params=_params(sem, has_side_effects=rider is not None),
    )(q, k, v, o, do, lse, ck, *r_in)


GDN_QKV = 3 * GDN_WIDTH
GDN_COL = 3 * FOX_WIDTH
GG_COL = GDN_COL + GDN_QKV
A_LANE = FOX_HEADS
B_LANE = FOX_HEADS + GDN_HEADS
HALO = 8


def _gate_lanes(ts):
    lane = lax.broadcasted_iota(jnp.int32, (ts, LANES), 1)
    return (lane >= A_LANE) & (lane < B_LANE), (lane >= B_LANE) & (lane < B_LANE + GDN_HEADS)


def _chunk_tri(ts, upper):
    r = lax.broadcasted_iota(jnp.int32, (ts, ts), 0)
    c = lax.broadcasted_iota(jnp.int32, (ts, ts), 1)
    same = (r // CHUNK) == (c // CHUNK)
    return (same & ((c >= r) if upper else (r >= c))).astype(F32)


def _conv_silu_l2(xp_ref, w, ts):
    c = w[0:1, :] * xp_ref[pl.ds(HALO - 3, ts), :]
    for kk in range(1, CONV_WIDTH):
        c = c + w[kk:kk + 1, :] * xp_ref[pl.ds(HALO - 3 + kk, ts), :]
    return c, c * _sigmoid(c)


def _gdn_prep(proj, conv_w, a_pad, dt_pad, seq, name):
    t = proj.shape[0]
    ts = _tile(seq, 256)
    tpe = seq // ts
    qscale = GDN_HEAD_DIM ** -0.5

    def body(x_ref, gt_ref, w_ref, a_ref, dt_ref, qo_ref, ko_ref, vo_ref, go_ref, xp):
        i = pl.program_id(0)
        tail = xp[pl.ds(ts, HALO), :]
        xp[pl.ds(0, HALO), :] = jnp.where(i % tpe == 0, jnp.zeros_like(tail), tail)
        xp[pl.ds(HALO, ts), :] = x_ref[...]
        _, s = _conv_silu_l2(xp, w_ref[...], ts)
        for h in range(GDN_HEADS):
            for base, ref, sc in ((0, qo_ref, qscale), (GDN_WIDTH, ko_ref, 1.0)):
                xh = s[:, base + h * LANES: base + (h + 1) * LANES]
                r = lax.rsqrt(jnp.sum(xh * xh, axis=-1, keepdims=True) + EPS)
                ref[:, h * LANES:(h + 1) * LANES] = (xh * (r * sc)).astype(BF)
        vo_ref[...] = s[:, 2 * GDN_WIDTH:].astype(BF)
        gate = gt_ref[...]
        g_raw = -jnp.exp(a_ref[...]) * _softplus(gate + dt_ref[...])
        gc = _mask_dot(_chunk_tri(ts, False), g_raw)
        is_a, is_b = _gate_lanes(ts)
        go_ref[...] = jnp.where(is_a, gc, jnp.where(is_b, _sigmoid(gate), 0.0))

    out = pl.BlockSpec((ts, GDN_WIDTH), lambda i: (i, 0))
    lanes = pl.BlockSpec((ts, LANES), lambda i: (i, 0))
    return pl.pallas_call(
        body, name=name, grid=(t // ts,),
        in_specs=[pl.BlockSpec((ts, GDN_QKV), lambda i: (i, GDN_COL // GDN_QKV)),
                  pl.BlockSpec((ts, LANES), lambda i: (i, GATE_COL // LANES)),
                  _full((CONV_WIDTH, GDN_QKV)), _full((1, LANES)), _full((1, LANES))],
        out_specs=[out, out, out, lanes],
        out_shape=[jax.ShapeDtypeStruct((t, GDN_WIDTH), BF)] * 3 + [jax.ShapeDtypeStruct((t, LANES), F32)],
        scratch_shapes=[pltpu.VMEM((ts + HALO, GDN_QKV), F32)],
        compiler_params=_params(("arbitrary",)),
    )(proj, proj, conv_w, a_pad, dt_pad)


def _gdn_prep_bwd(proj, dq, dk, dv, dgates, dff, conv_w, a_pad, dt_pad, seq, name):
    t = proj.shape[0]
    ts = _tile(seq, 256)
    tpe = seq // ts
    nt = t // ts
    qscale = GDN_HEAD_DIM ** -0.5
    hb = ts // HALO

    def body(x_ref, halo_ref, gt_ref, dq_ref, dk_ref, dv_ref, dgt_ref, dff_ref, w_ref, a_ref, dt_ref,
             dx_ref, dgo_ref, dw_ref, da_ref, ddt_ref, xp, dcp, carry):
        i = pl.program_id(0)
        ti = nt - 1 - i

        @pl.when(i == 0)
        def _():
            dw_ref[...] = jnp.zeros_like(dw_ref)
            da_ref[...] = jnp.zeros_like(da_ref)
            ddt_ref[...] = jnp.zeros_like(ddt_ref)

        halo = halo_ref[...]
        xp[pl.ds(0, HALO), :] = jnp.where(ti % tpe == 0, jnp.zeros_like(halo), halo)
        xp[pl.ds(HALO, ts), :] = x_ref[...]
        w = w_ref[...]
        c, s = _conv_silu_l2(xp, w, ts)
        for h in range(GDN_HEADS):
            for base, ref, sc in ((0, dq_ref, qscale), (GDN_WIDTH, dk_ref, 1.0)):
                lo = base + h * LANES
                xh = s[:, lo:lo + LANES]
                r = lax.rsqrt(jnp.sum(xh * xh, axis=-1, keepdims=True) + EPS)
                y = xh * r
                dy = ref[:, h * LANES:(h + 1) * LANES] * sc
                dcp[pl.ds(0, ts), lo:lo + LANES] = r * (dy - y * jnp.sum(dy * y, axis=-1, keepdims=True))
        dcp[pl.ds(0, ts), 2 * GDN_WIDTH:] = dv_ref[...]
        sg = _sigmoid(c)
        dc = dcp[pl.ds(0, ts), :] * (sg * (1.0 + c * (1.0 - sg)))
        dcp[pl.ds(0, ts), :] = dc
        nxt = carry[...]
        dcp[pl.ds(ts, HALO), :] = jnp.where(ti % tpe == tpe - 1, jnp.zeros_like(nxt), nxt)
        carry[...] = dc[0:HALO, :]
        dx = w[CONV_WIDTH - 1:CONV_WIDTH, :] * dc
        for kk in range(CONV_WIDTH - 1):
            dx = dx + w[kk:kk + 1, :] * dcp[pl.ds(CONV_WIDTH - 1 - kk, ts), :]
        dx_ref[...] = dx.astype(BF)
        for kk in range(CONV_WIDTH):
            dw_ref[kk:kk + 1, :] += jnp.sum(dc * xp[pl.ds(HALO - 3 + kk, ts), :], axis=0, keepdims=True)
        gate = gt_ref[...]
        dgt = dgt_ref[...]
        is_a, is_b = _gate_lanes(ts)
        dg_raw = _mask_dot(_chunk_tri(ts, True), jnp.where(is_a, dgt, 0.0))
        z = gate + dt_ref[...]
        na = -jnp.exp(a_ref[...])
        dga = dg_raw * na * _sigmoid(z)
        beta = _sigmoid(gate)
        dgb = jnp.where(is_b, dgt * beta * (1.0 - beta), 0.0)
        dgo_ref[...] = (dff_ref[...] + dga + dgb).astype(BF)
        ddt_ref[...] += jnp.sum(dga, axis=0, keepdims=True)
        da_ref[...] += jnp.sum(dg_raw * na * _softplus(z), axis=0, keepdims=True)

    rev = lambda wd, j: pl.BlockSpec((ts, wd), lambda i: (nt - 1 - i, j))
    halo_spec = pl.BlockSpec((HALO, GDN_QKV), lambda i: (jnp.maximum((nt - 1 - i) * hb - 1, 0), GDN_COL // GDN_QKV))
    return pl.pallas_call(
        body, name=name, grid=(nt,),
        in_specs=[rev(GDN_QKV, GDN_COL // GDN_QKV), halo_spec, rev(LANES, GATE_COL // LANES),
                  rev(GDN_WIDTH, 0), rev(GDN_WIDTH, 0), rev(GDN_WIDTH, 0), rev(LANES, 0), rev(LANES, 0),
                  _full((CONV_WIDTH, GDN_QKV)), _full((1, LANES)), _full((1, LANES))],
        out_specs=[rev(GDN_QKV, 0), rev(LANES, 0), _full((CONV_WIDTH, GDN_QKV)), _full((1, LANES)),
                   _full((1, LANES))],
        out_shape=[jax.ShapeDtypeStruct((t, GDN_QKV), BF), jax.ShapeDtypeStruct((t, LANES), BF),
                   jax.ShapeDtypeStruct((CONV_WIDTH, GDN_QKV), F32), jax.ShapeDtypeStruct((1, LANES), F32),
                   jax.ShapeDtypeStruct((1, LANES), F32)],
        scratch_shapes=[pltpu.VMEM((ts + HALO, GDN_QKV), F32), pltpu.VMEM((ts + HALO, GDN_QKV), F32),
                        pltpu.VMEM((HALO, GDN_QKV), F32)],
        compiler_params=_params(("arbitrary",)),
    )(proj, proj, proj, dq, dk, dv, dgates, dff, conv_w, a_pad, dt_pad)


PAIR = 2 * CHUNK


def _split_bf16(a):
    hi = a.astype(BF)
    return hi, (a - hi.astype(F32)).astype(BF)


def _dot3(a, b, dims=(((1,), (0,)), ((), ()))):
    ah, al = _split_bf16(a)
    bh, bl = _split_bf16(b)
    (ca,), (cb,) = dims[0]
    return lax.dot_general(jnp.concatenate([ah, al, ah], axis=ca), jnp.concatenate([bh, bh, bl], axis=cb), dims,
                           preferred_element_type=F32)


def _inv_unit_lower(a):
    r = lax.broadcasted_iota(jnp.int32, (PAIR, PAIR), 0)
    c = lax.broadcasted_iota(jnp.int32, (PAIR, PAIR), 1)
    tm = (r == c).astype(F32) - a
    pw = _dot3(a, a)
    for _ in range(4):
        x = _dot3(jnp.concatenate([tm, pw], axis=0), pw)
        tm = tm + x[:PAIR]
        pw = x[PAIR:]
    return tm + _dot3(tm, pw)


def _gdn_pair_local(q, k, v, gc, gr, b):
    r = lax.broadcasted_iota(jnp.int32, (PAIR, PAIR), 0)
    c = lax.broadcasted_iota(jnp.int32, (PAIR, PAIR), 1)
    same = (r // CHUNK) == (c // CHUNK)
    incl = same & (r >= c)
    strict = same & (r > c)
    dm = jnp.exp(jnp.where(incl, gc - gr, NEG))
    e = jnp.exp(gc)
    kb = k * b
    vb = v * b
    kbe = kb * e
    kq = _dot_nt(jnp.concatenate([kb, q], axis=0).astype(BF), k.astype(BF))
    amat = jnp.where(strict, kq[:PAIR] * dm, 0.0)
    pmat = jnp.where(incl, kq[PAIR:] * dm, 0.0)
    lane = lax.broadcasted_iota(jnp.int32, (1, PAIR), 1)
    gl_a = jnp.sum(jnp.where(lane == CHUNK - 1, gr, 0.0), axis=1, keepdims=True)
    gl_b = jnp.sum(jnp.where(lane == PAIR - 1, gr, 0.0), axis=1, keepdims=True)
    ridx = lax.broadcasted_iota(jnp.int32, (PAIR, 1), 0)
    edec = jnp.exp(jnp.where(ridx < CHUNK, gl_a, gl_b) - gc)
    return dict(dm=dm, e=e, kb=kb, vb=vb, kbe=kbe, amat=amat, pmat=pmat, gl_a=gl_a, gl_b=gl_b, edec=edec,
                kd=k * edec, qd=q * e, incl=incl, strict=strict, ridx=ridx)


def _gdn_pair_states(loc, tb, s_a):
    uw = _dot(tb, jnp.concatenate([loc["vb"], loc["kbe"]], axis=1).astype(BF))
    u, w = uw[:, :LANES], uw[:, LANES:]
    qd, kd, c = loc["qd"], loc["kd"], CHUNK
    xa = _dot(jnp.concatenate([qd[:c], w[:c]], axis=0).astype(BF), s_a.astype(BF))
    vn_a = u[:c] - xa[c:]
    s_b = s_a * jnp.exp(loc["gl_a"]) + _dot_tn(kd[:c].astype(BF), vn_a.astype(BF))
    xb = _dot(jnp.concatenate([qd[c:], w[c:]], axis=0).astype(BF), s_b.astype(BF))
    vn_b = u[c:] - xb[c:]
    s_c = s_b * jnp.exp(loc["gl_b"]) + _dot_tn(kd[c:].astype(BF), vn_b.astype(BF))
    vn = jnp.concatenate([vn_a, vn_b], axis=0)
    o = jnp.concatenate([xa[:c], xb[:c]], axis=0) + _dot(loc["pmat"].astype(BF), vn.astype(BF))
    return w, vn, o, s_b, s_c


GDN_SEG = 512


def _gdn_specs(nb, seq, reverse):
    n = seq // CHUNK
    seg = _tile(seq, GDN_SEG)
    nseg = seq // seg
    sp = seg // PAIR
    at = (lambda s: nseg - 1 - s) if reverse else (lambda s: s)
    blk = pl.BlockSpec((nb, seg, GDN_WIDTH), lambda s: (0, at(s), 0))
    gg = pl.BlockSpec((nb, seg, GDN_WIDTH), lambda s: (0, at(s), GG_COL // GDN_WIDTH))
    gates = pl.BlockSpec((nb, seg, LANES), lambda s: (0, at(s), 0))
    rowb = pl.BlockSpec((nb, GDN_HEADS, sp, HALO, PAIR), lambda s: (0, 0, at(s), 0, 0))
    per_pair = pl.BlockSpec((nb, GDN_HEADS, sp, PAIR, PAIR), lambda s: (0, 0, at(s), 0, 0))
    return n, seg, nseg, sp, blk, gg, gates, rowb, per_pair


def _head_column(gt, lane, index):
    return jnp.sum(jnp.where(lane == index, gt, 0.0), axis=1, keepdims=True)


def _gdn_head_inputs(qkv_refs, gt_ref, gr_ref, rows, pi, lane, chains):
    per_chain = []
    for b, hh in chains:
        gt = gt_ref[b, rows, :]
        cols = slice(hh * LANES, (hh + 1) * LANES)
        per_chain.append([r[b, rows, cols].astype(F32) for r in qkv_refs]
                         + [_head_column(gt, lane, A_LANE + hh), gr_ref[b, hh, pi][0:1, :],
                            _head_column(gt, lane, B_LANE + hh)])
    return [jnp.stack(xs) for xs in zip(*per_chain)]


def _gdn_pair_fwd(qv, kv, vv, gcv, gr, bv, s_a):
    loc = _gdn_pair_local(qv, kv, vv, gcv, gr, bv)
    tf = _inv_unit_lower(loc["amat"])
    _, _, o, _, s_c = _gdn_pair_states(loc, tf.astype(BF), s_a)
    return tf, o, s_c


def _gdn_fwd(q, k, v, proj, gates, grow, wn, nb, seq, name):
    n, seg, nseg, sp, blk, gg, gates_spec, rowb, per_pair = _gdn_specs(nb, seq, False)
    chains = [(b, hh) for b in range(nb) for hh in range(GDN_HEADS)]

    def body(q_ref, k_ref, v_ref, gg_ref, gt_ref, gr_ref, wn_ref, y_ref, tn_ref, sn_ref, s_ref):
        @pl.when(pl.program_id(0) == 0)
        def _():
            s_ref[...] = jnp.zeros_like(s_ref)

        wnv = wn_ref[...]
        lane = lax.broadcasted_iota(jnp.int32, (PAIR, LANES), 1)

        def step(pi, carry):
            rows = pl.ds(pl.multiple_of(pi * PAIR, PAIR), PAIR)
            ins = _gdn_head_inputs((q_ref, k_ref, v_ref), gt_ref, gr_ref, rows, pi, lane, chains)
            s_a = s_ref[...]
            tf, o, s_c = jax.vmap(_gdn_pair_fwd)(*ins, s_a)
            s_ref[...] = s_c
            for c, (b, hh) in enumerate(chains):
                cols = slice(hh * LANES, (hh + 1) * LANES)
                tn_ref[b, hh, pi] = tf[c]
                sn_ref[b, hh, pi] = s_a[c]
                g = gg_ref[b, rows, cols]
                oh = o[c]
                rstd = lax.rsqrt(jnp.mean(oh * oh, axis=-1, keepdims=True) + EPS)
                y_ref[b, rows, cols] = (oh * rstd * wnv * (g * _sigmoid(g))).astype(BF)
            return carry

        lax.fori_loop(0, sp, step, 0)

    saved = jax.ShapeDtypeStruct((nb, GDN_HEADS, n // 2, PAIR, PAIR), F32)
    return pl.pallas_call(
        body, name=name, grid=(nseg,),
        in_specs=[blk, blk, blk, gg, gates_spec, rowb, _full((1, LANES))],
        out_specs=[blk, per_pair, per_pair],
        out_shape=[jax.ShapeDtypeStruct((nb, seq, GDN_WIDTH), BF), saved, saved],
        scratch_shapes=[pltpu.VMEM((len(chains), GDN_HEAD_DIM, GDN_HEAD_DIM), F32)],
        compiler_params=_params(("arbitrary",)),
    )(q, k, v, proj, gates, grow, wn)


def _gdn_pair_bwd(qv, kv, vv, gcv, gr, bv, tf, s_a, dsp, g, dyv, wnv):
    c = CHUNK
    loc = _gdn_pair_local(qv, kv, vv, gcv, gr, bv)
    tm = tf.astype(BF)
    kb, vb, kbe, e, dm = loc["kb"], loc["vb"], loc["kbe"], loc["e"], loc["dm"]
    kd, qd, pmat, amat = loc["kd"], loc["qd"], loc["pmat"], loc["amat"]
    w, vn, o, s_b, _ = _gdn_pair_states(loc, tm, s_a)
    sg = _sigmoid(g)
    silu = g * sg
    rstd = lax.rsqrt(jnp.mean(o * o, axis=-1, keepdims=True) + EPS)
    xhat = o * rstd
    dwn = jnp.sum(dyv * xhat * silu, axis=0, keepdims=True)
    dgg = dyv * xhat * wnv * (sg * (1.0 + g * (1.0 - sg)))
    dxhat = dyv * wnv * silu
    do = rstd * (dxhat - xhat * jnp.mean(dxhat * xhat, axis=-1, keepdims=True))
    dob = do.astype(BF)
    tot = lambda x: jnp.sum(jnp.sum(x, axis=1, keepdims=True), axis=0, keepdims=True)
    rsum = lambda x: jnp.sum(x, axis=1, keepdims=True)
    cat = lambda xs, ax=0: jnp.concatenate(xs, axis=ax)
    wb = w.astype(BF)
    qdb = qd.astype(BF)
    kdb = kd.astype(BF)
    vnb = vn.astype(BF)
    egl_a = jnp.exp(loc["gl_a"])
    egl_b = jnp.exp(loc["gl_b"])
    ptdo = _dot_tn(pmat.astype(BF), dob)
    dspb = dsp.astype(BF)
    dvn_b = ptdo[c:] + _dot(kdb[c:], dspb)
    dkd_b = _dot_nt(vnb[c:], dspb)
    dgl_b = egl_b * tot(s_b * dsp) + tot(dkd_b * kd[c:])
    dsm = egl_b * dsp + _dot_tn(cat([qdb[c:], -wb[c:]]), cat([dob[c:], dvn_b.astype(BF)]))
    dsmb = dsm.astype(BF)
    dvn_a = ptdo[:c] + _dot(kdb[:c], dsmb)
    dkd_a = _dot_nt(vnb[:c], dsmb)
    dgl_a = egl_a * tot(s_a * dsm) + tot(dkd_a * kd[:c])
    ds_new = egl_a * dsm + _dot_tn(cat([qdb[:c], -wb[:c]]), cat([dob[:c], dvn_a.astype(BF)]))
    ya = _dot_nt(cat([dob[:c], dvn_a.astype(BF)]), s_a.astype(BF))
    yb = _dot_nt(cat([dob[c:], dvn_b.astype(BF)]), s_b.astype(BF))
    dqd = cat([ya[:c], yb[:c]])
    dw = -cat([ya[c:], yb[c:]])
    dvn = cat([dvn_a, dvn_b])
    dkd = cat([dkd_a, dkd_b])
    dq = dqd * e
    dgc = rsum(dqd * qd) - rsum(dkd * kd)
    dk = dkd * loc["edec"]
    dpm = jnp.where(loc["incl"], _dot_nt(dob, vnb), 0.0)
    duw = cat([dvn, dw], 1).astype(BF)
    dt = _dot_nt(duw, cat([vb, kbe], 1).astype(BF))
    tt = _dot_tn(tm, duw)
    dvb, dkbe = tt[:, :LANES], tt[:, LANES:]
    tn_dims = (((0,), (0,)), ((), ()))
    nt_dims = (((1,), (1,)), ((), ()))
    da = jnp.where(loc["strict"], -_dot3(_dot3(tf, dt, tn_dims), tf, nt_dims), 0.0)
    st = cat([da * dm, dpm * dm]).astype(BF)
    z = _dot(st, kv.astype(BF))
    dkb = z[:PAIR] + dkbe * e
    dq = dq + z[PAIR:]
    dk = dk + _dot_tn(st, cat([kb, qv]).astype(BF))
    gmat = dpm * pmat + da * amat
    dgc = dgc + rsum(dkbe * kbe) + rsum(gmat)
    ridx = loc["ridx"]
    dgc = dgc + jnp.where(ridx == c - 1, dgl_a, 0.0) + jnp.where(ridx == PAIR - 1, dgl_b, 0.0)
    dgc_row = jnp.sum(gmat, axis=0, keepdims=True)
    db = rsum(dvb * vv) + rsum(dkb * kv)
    return dq, dk + dkb * bv, dvb * bv, dgg, dgc, dgc_row, db, dwn, ds_new


def _gdn_bwd(q, k, v, proj, gates, grow, wn, tinv_all, states_all, dy, nb, seq, name):
    n, seg, nseg, sp, blk, gg, gates_spec, rowb, per_pair = _gdn_specs(nb, seq, True)
    dh = GDN_HEAD_DIM
    chains = [(b, hh) for b in range(nb) for hh in range(GDN_HEADS)]

    def body(q_ref, k_ref, v_ref, gg_ref, gt_ref, gr_ref, wn_ref, tn_ref, sn_ref, dy_ref,
             dq_ref, dk_ref, dv_ref, dgg_ref, dgt_ref, dwn_ref, ds_ref):
        @pl.when(pl.program_id(0) == 0)
        def _():
            dwn_ref[...] = jnp.zeros_like(dwn_ref)
            ds_ref[...] = jnp.zeros_like(ds_ref)

        wnv = wn_ref[...]
        lane = lax.broadcasted_iota(jnp.int32, (PAIR, LANES), 1)

        def bwd_step(j, carry):
            pi = sp - 1 - j
            rows = pl.ds(pl.multiple_of(pi * PAIR, PAIR), PAIR)
            ins = _gdn_head_inputs((q_ref, k_ref, v_ref), gt_ref, gr_ref, rows, pi, lane, chains)
            lanes_of = lambda hh: slice(hh * LANES, (hh + 1) * LANES)
            saved = [jnp.stack([r[b, hh, pi] for b, hh in chains]) for r in (tn_ref, sn_ref)]
            g2 = jnp.stack([gg_ref[b, rows, lanes_of(hh)] for b, hh in chains])
            dy2 = jnp.stack([dy_ref[b, rows, lanes_of(hh)].astype(F32) for b, hh in chains])
            dq, dk, dv, dgg, dgc, dgc_row, db, dwn, ds_new = jax.vmap(
                _gdn_pair_bwd, in_axes=(0,) * 11 + (None,))(*ins, *saved, ds_ref[...], g2, dy2, wnv)
            ds_ref[...] = ds_new
            dgt = [jnp.zeros((PAIR, LANES), F32) for _ in range(nb)]
            for c, (b, hh) in enumerate(chains):
                cols = lanes_of(hh)
                dq_ref[b, rows, cols] = dq[c]
                dk_ref[b, rows, cols] = dk[c]
                dv_ref[b, rows, cols] = dv[c]
                dgg_ref[b, rows, cols] = dgg[c].astype(BF)
                dwn_ref[...] += dwn[c]
                row_as_col = jnp.transpose(jnp.broadcast_to(dgc_row[c], (PAIR, LANES)))
                dgt[b] = (dgt[b] + jnp.where(lane == A_LANE + hh, dgc[c] - row_as_col, 0.0)
                          + jnp.where(lane == B_LANE + hh, db[c], 0.0))
            for b in range(nb):
                dgt_ref[b, rows, :] = dgt[b]
            return carry

        lax.fori_loop(0, sp, bwd_step, 0)

    f32_out = jax.ShapeDtypeStruct((nb, seq, GDN_WIDTH), F32)
    return pl.pallas_call(
        body, name=name, grid=(nseg,),
        in_specs=[blk, blk, blk, gg, gates_spec, rowb, _full((1, LANES)), per_pair, per_pair, blk],
        out_specs=[blk, blk, blk, blk, gates_spec, _full((1, LANES))],
        out_shape=[f32_out, f32_out, f32_out, jax.ShapeDtypeStruct((nb, seq, GDN_WIDTH), BF),
                   jax.ShapeDtypeStruct((nb, seq, LANES), F32), jax.ShapeDtypeStruct((1, LANES), F32)],
        scratch_shapes=[pltpu.VMEM((len(chains), dh, dh), F32)],
        compiler_params=_params(("arbitrary",)),
    )(q, k, v, proj, gates, grow, wn, tinv_all, states_all, dy)


def _mix_to_padded(w):
    pad = jnp.zeros(w.shape[:-1] + (N_PAD - N_IN,), w.dtype)
    return jnp.concatenate([w[..., 0:1536], w[..., 1544:3080], w[..., 3088:3600], w[..., 1536:1544],
                            w[..., 3080:3088], pad], axis=-1)


def _pad_lanes(vec, start):
    return jnp.pad(vec[None, :], ((0, 0), (start, LANES - start - vec.shape[0])))


def _heads_to_rows(block, lane0, nheads, nb, seq):
    return block[:, lane0:lane0 + nheads].reshape(nb, seq, nheads).transpose(0, 2, 1).reshape(nb * nheads, seq)


def _mixer_small(p, l):
    wq_t = jnp.tile(p["fox_q_norm"][l], FOX_HEADS)[None, :]
    wk_t = jnp.tile(p["fox_k_norm"][l], FOX_HEADS)[None, :]
    bias = _pad_lanes(p["fox_f_bias"][l], 0)
    a_pad = _pad_lanes(p["gdn_a_log"][l], A_LANE)
    dt_pad = _pad_lanes(p["gdn_dt_bias"][l], A_LANE)
    wn = p["gdn_out_norm"][l][None, :]
    return wq_t, wk_t, bias, a_pad, dt_pad, wn


def _layer_fwd(x, p, l, nb, seq, rider=None, ffn1_rider=None, after_ffn1=None):
    npair = FOX_HEADS // 2
    n = seq // CHUNK
    x1, h1, *rode1 = _ffn_fwd(x, p["ffn1_norm"][l][None, :], p["ffn1_w_in"][l], p["ffn1_w_out"][l],
                              f"ffn1_fwd_{l}", ffn1_rider)
    if after_ffn1 is not None:
        after_ffn1(rode1)
    wq_t, wk_t, bias, a_pad, dt_pad, wn = _mixer_small(p, l)
    proj = _norm_matmul(x1, p["mix_norm"][l][None, :], p["w_mix"][l], f"mix_in_{l}")
    fq, fk, fv, cum = _fox_prep(proj, wq_t, wk_t, bias, seq, f"fox_prep_{l}")
    c8 = _heads_to_rows(cum, 0, FOX_HEADS, nb, seq)
    ck = c8.reshape(nb * npair, 2, 1, seq)
    o, lse, *rode = _fox_attn(fq, fk, fv, ck, nb, seq, f"fox_attn_{l}", rider)
    gq, gk, gv, gates = _gdn_prep(proj, p["gdn_conv"][l], a_pad, dt_pad, seq, f"gdn_prep_{l}")
    gc4 = _heads_to_rows(gates, A_LANE, GDN_HEADS, nb, seq)
    grow = jnp.broadcast_to(gc4.reshape(nb, GDN_HEADS, n // 2, 1, PAIR), (nb, GDN_HEADS, n // 2, HALO, PAIR))
    per_example = lambda a: a.reshape(nb, seq, a.shape[-1])
    gq, gk, gv, gates = per_example(gq), per_example(gk), per_example(gv), per_example(gates)
    y, tinv, states = _gdn_fwd(gq, gk, gv, per_example(proj), gates, grow, wn, nb, seq, f"gdn_fwd_{l}")
    y = y.reshape(nb * seq, GDN_WIDTH)
    x2 = _mix_out(x1, o, y, p["w_out"][l], f"mix_out_{l}")
    x3, h2 = _ffn_fwd(x2, p["ffn2_norm"][l][None, :], p["ffn2_w_in"][l], p["ffn2_w_out"][l], f"ffn2_fwd_{l}")
    saved = dict(x=x, h1=h1, x1=x1, proj=proj, fq=fq, fk=fk, fv=fv, ck=ck, o=o, lse=lse,
                 gq=gq, gk=gk, gv=gv, gates=gates, grow=grow, tinv=tinv, states=states, y=y, x2=x2, h2=h2)
    return x3, saved, rode


def _ffn_grads(dy, x, h, gain, win, wout, l, tag, rider=None):
    t, d = x.shape
    fs = win.shape[2]
    dx, dh, a, hn, dyh, dgain, *rode = _ffn_bwd(dy, x, h, gain, win, wout, f"{tag}_bwd_{l}", rider)
    g_in = _wgrad(hn, dh, jax.ShapeDtypeStruct((4, d, fs), BF),
                  pl.BlockSpec((None, d, fs), lambda i, j, k: (j, i, 0)), d, fs, f"{tag}_gw_in_{l}")
    g_out = _wgrad(a, dyh, jax.ShapeDtypeStruct((2 * fs, d), BF),
                   pl.BlockSpec((fs, d), lambda i, j, k: (i, j)), fs, d, f"{tag}_gw_out_{l}")
    return dx, dgain[0], g_in, g_out.reshape(4, fs // 2, d), rode


def _layer_bwd(dx3, p, l, sv, nb, seq, rider=None, before_ffn1=None, ffn2_rider=None, after_ffn2=None):
    npair = FOX_HEADS // 2
    d = dx3.shape[1]
    wq_t, wk_t, bias, a_pad, dt_pad, wn = _mixer_small(p, l)
    g = {}
    dx2, g["ffn2_norm"], g["ffn2_w_in"], g["ffn2_w_out"], rode2 = _ffn_grads(
        dx3, sv["x2"], sv["h2"], p["ffn2_norm"][l][None, :], p["ffn2_w_in"][l], p["ffn2_w_out"][l], l, "ffn2",
        ffn2_rider)
    if after_ffn2 is not None:
        rider = after_ffn2(rode2)
    dyf, dyg, dxb = _mix_out_bwd(dx2, p["w_out"][l], f"mix_out_bwd_{l}")
    half = lambda a, nm: _wgrad(a, dxb, jax.ShapeDtypeStruct((FOX_WIDTH, d), BF),
                                pl.BlockSpec((FOX_WIDTH, d), lambda i, j, k: (i, j)), FOX_WIDTH, d, nm)
    g["w_out"] = jnp.concatenate([half(sv["o"], f"gw_out_fox_{l}"), half(sv["y"], f"gw_out_gdn_{l}")], axis=0)
    dqa, dqb, dk, dv, dkx, *rode = _fox_attn_bwd(sv["fq"], sv["fk"], sv["fv"], sv["o"], dyf, sv["lse"], sv["ck"],
                                                 nb, seq, f"fox_attn_bwd_{l}", rider)

    dpf, dff, dwq, dwk, dbias = _fox_prep_bwd(sv["proj"], dqa, dqb, dk, dv, dkx, wq_t, wk_t, bias, seq,
                                              f"fox_prep_bwd_{l}")
    g["fox_q_norm"] = dwq[0, :FOX_HEAD_DIM]
    g["fox_k_norm"] = dwk[0, :FOX_HEAD_DIM]
    g["fox_f_bias"] = dbias[0, :FOX_HEADS]
    per_example = lambda a: a.reshape(nb, seq, a.shape[-1])
    flat = lambda a: a.reshape(nb * seq, a.shape[-1])
    dgq, dgk, dgv, dgg, dgates, dwn = _gdn_bwd(
        sv["gq"], sv["gk"], sv["gv"], per_example(sv["proj"]), sv["gates"], sv["grow"], wn, sv["tinv"],
        sv["states"], per_example(dyg), nb, seq, f"gdn_bwd_{l}")
    dgq, dgk, dgv, dgg, dgates = flat(dgq), flat(dgk), flat(dgv), flat(dgg), flat(dgates)
    dpg, dgate_blk, dconv, da, ddt = _gdn_prep_bwd(sv["proj"], dgq, dgk, dgv, dgates, dff, p["gdn_conv"][l],
                                                   a_pad, dt_pad, seq, f"gdn_prep_bwd_{l}")
    g["gdn_conv"] = dconv
    g["gdn_a_log"] = da[0, A_LANE:B_LANE]
    g["gdn_dt_bias"] = ddt[0, A_LANE:B_LANE]
    g["gdn_out_norm"] = dwn[0]
    dparts = [dpf, dpg, dgg, dgate_blk]
    dx1, hnm, dgm = _norm_matmul_bwd(dx2, dparts, sv["x1"], p["mix_norm"][l][None, :], p["w_mix"][l],
                                     f"mix_in_bwd_{l}")
    g["mix_norm"] = dgm[0]
    gf, gg_, go, gt = [_wgrad(hnm, a, jax.ShapeDtypeStruct((d, a.shape[1]), F32),
                              pl.BlockSpec((d // 2, a.shape[1]), lambda i, j, k: (i, j)), d // 2, a.shape[1],
                              f"gw_mix_{l}_{i}") for i, a in enumerate(dparts)]
    g["w_in"] = jnp.concatenate([gf, gt[:, 0:FOX_HEADS], gg_, gt[:, A_LANE:B_LANE + GDN_HEADS], go], axis=1)
    ffn1_rider = before_ffn1(g) if before_ffn1 is not None else None
    dx0, g["ffn1_norm"], g["ffn1_w_in"], g["ffn1_w_out"], rode1 = _ffn_grads(
        dx1, sv["x"], sv["h1"], p["ffn1_norm"][l][None, :], p["ffn1_w_in"][l], p["ffn1_w_out"][l], l, "ffn1",
        ffn1_rider)
    return dx0, g, rode, rode1


def _local_step(x, target, p):
    nb, seq, d = x.shape
    xt = x.reshape(nb * seq, d)
    saved = []
    for l in range(DEPTH):
        xt, sv, _ = _layer_fwd(xt, p, l, nb, seq)
        saved.append(sv)
    loss, dx = _loss_grad(xt, target.reshape(nb * seq, d), "loss")
    grads = [None] * DEPTH
    for l in reversed(range(DEPTH)):
        dx, grads[l], _, _ = _layer_bwd(dx, p, l, saved[l], nb, seq)
    return loss, dx.reshape(nb, seq, d), grads


N_CHIPS = 4


def _mesh_pos():
    return lax.axis_index("x"), lax.axis_index("y"), lax.axis_index("c")


def _other_chips(x, y):
    return [(1 - x, y), (x, 1 - y), (1 - x, 1 - y)]


def _remote(src, dst, send_sem, recv_sem, to):
    return pltpu.make_async_remote_copy(src_ref=src, dst_ref=dst, send_sem=send_sem, recv_sem=recv_sem,
                                        device_id=to, device_id_type=MESH)


def _hbm_call(body, name, ins, out_shape, scratch):
    return pl.pallas_call(
        body, name=name, out_shape=out_shape, in_specs=[HBM] * len(ins),
        out_specs=jax.tree.map(lambda _: HBM, out_shape), scratch_shapes=scratch,
        compiler_params=pltpu.CompilerParams(has_side_effects=True),
    )(*ins)


def _gather_phases(n, layer):
    def copies(ins, outs, sems):
        send1, recv1, send2, recv2 = sems
        x, y, c = _mesh_pos()
        out, back, fwd = [], [], []
        for i in range(n):
            for j, (px, py) in enumerate(_other_chips(x, y)):
                k = 3 * i + j
                blk = outs[i].at[2 * px + py]
                out.append(_remote(ins[i], outs[i].at[2 * x + y], send1.at[k], recv1.at[k], (px, py, c)))
                back.append(_remote(blk, blk, send1.at[k], recv1.at[k], (px, py, c)))
                fwd.append(_remote(blk, blk, send2.at[k], recv2.at[k], (x, y, 1 - c)))
        return c, out, back, fwd

    def first(ins, outs, sems):
        c, out, _, _ = copies(ins, outs, sems)

        @pl.when(c == layer)
        def _():
            for cp in out:
                cp.start()

    def middle(ins, outs, sems):
        c, _, back, fwd = copies(ins, outs, sems)

        @pl.when(c == layer)
        def _():
            for arrived, onward in zip(back, fwd):
                arrived.wait_recv()
                onward.start()

    def last(ins, outs, sems):
        c, out, _, fwd = copies(ins, outs, sems)

        @pl.when(c == layer)
        def _():
            for cp in out + fwd:
                cp.wait_send()

        @pl.when(c != layer)
        def _():
            for cp in fwd:
                cp.wait_recv()

    return first, middle, last


def _scatter_phases(n, layer):
    def copies(ins, outs, sems):
        send, recv = sems
        x, y, c = _mesh_pos()
        return c, [_remote(ins[i].at[2 * px + py], outs[i].at[j], send.at[3 * i + j], recv.at[3 * i + j], (px, py, c))
                   for i in range(n) for j, (px, py) in enumerate(_other_chips(x, y))]

    def first(ins, outs, sems):
        c, cps = copies(ins, outs, sems)

        @pl.when(c == layer)
        def _():
            for cp in cps:
                cp.start()

    def middle(ins, outs, sems):
        pass

    def last(ins, outs, sems):
        c, cps = copies(ins, outs, sems)

        @pl.when(c == layer)
        def _():
            for cp in cps:
                cp.wait()

    return first, middle, last


def _exchange(blocks, out_shapes, n_sems, phases, name, rider):
    sems = [pltpu.SemaphoreType.DMA((3 * len(blocks),))] * n_sems
    if rider:
        return _Rider(blocks, out_shapes, sems, phases)
    n = len(blocks)

    def body(*refs):
        for phase in phases:
            phase(refs[:n], refs[n:2 * n], refs[2 * n:])

    return list(_hbm_call(body, name, blocks, out_shapes, sems))


def _gather_layer(blocks, layer, name=None, rider=False):
    outs = [jax.ShapeDtypeStruct((N_CHIPS,) + b.shape, b.dtype) for b in blocks]
    return _exchange(blocks, outs, 4, _gather_phases(len(blocks), layer), name, rider)


def _scatter_layer(sums, layer, name=None, rider=False):
    outs = [jax.ShapeDtypeStruct((3,) + s.shape[1:], s.dtype) for s in sums]
    return _exchange(sums, outs, 2, _scatter_phases(len(sums), layer), name, rider)


def _to_sibling(gs, layer, name=None, rider=False):
    n = len(gs)

    def copies(ins, outs, sems):
        send, recv = sems
        x, y, c = _mesh_pos()
        return c, [_remote(ins[i], outs[i], send.at[i], recv.at[i], (x, y, 1 - c)) for i in range(n)]

    def first(ins, outs, sems):
        c, cps = copies(ins, outs, sems)

        @pl.when(c != layer)
        def _():
            for cp in cps:
                cp.start()

    def middle(ins, outs, sems):
        pass

    def last(ins, outs, sems):
        c, cps = copies(ins, outs, sems)

        @pl.when(c != layer)
        def _():
            for cp in cps:
                cp.wait_send()

        @pl.when(c == layer)
        def _():
            for cp in cps:
                cp.wait_recv()

    sems = [pltpu.SemaphoreType.DMA((n,))] * 2
    outs = [jax.ShapeDtypeStruct(g.shape, g.dtype) for g in gs]
    if rider:
        return _Rider(gs, outs, sems, (first, middle, last))

    def body(*refs):
        for phase in (first, middle, last):
            phase(refs[:n], refs[n:2 * n], refs[2 * n:])

    return list(_hbm_call(body, name, gs, outs, sems))


def _sibling_swap(rs, name):
    n = len(rs)

    def body(*refs):
        ins, outs = refs[:n], refs[n:2 * n]
        send, recv = refs[2 * n:]
        x, y, c = _mesh_pos()
        cps = [_remote(ins[i], outs[i], send.at[i], recv.at[i], (x, y, 1 - c)) for i in range(n)]
        for cp in cps:
            cp.start()
        for cp in cps:
            cp.wait()

    sem = pltpu.SemaphoreType.DMA((n,))
    return _hbm_call(body, name, rs, [jax.ShapeDtypeStruct(r.shape, r.dtype) for r in rs], [sem, sem])


def _small_all_reduce(vec, name):
    r = vec.shape[0]
    ndev = 8

    def body(v_ref, o_ref, buf, send, recv):
        x, y, c = _mesh_pos()
        me = 4 * x + 2 * y + c
        buf[me] = v_ref[...]
        cps = []
        for rel in range(1, ndev):
            px = 1 - x if rel & 4 else x
            py = 1 - y if rel & 2 else y
            pc = 1 - c if rel & 1 else c
            cps.append((_remote(v_ref, buf.at[me], send.at[rel - 1], recv.at[rel - 1], (px, py, pc)),
                        4 * px + 2 * py + pc))
        for cp, _ in cps:
            cp.start()
        for k, (cp, peer) in enumerate(cps):
            slot = buf.at[peer]
            _remote(slot, slot, send.at[k], recv.at[k], (x, y, c)).wait_recv()
        for cp, _ in cps:
            cp.wait_send()
        acc = buf[0]
        for k in range(1, ndev):
            acc = acc + buf[k]
        o_ref[...] = acc

    vm = pl.BlockSpec(memory_space=pltpu.VMEM)
    return pl.pallas_call(
        body, name=name, out_shape=jax.ShapeDtypeStruct(vec.shape, F32), in_specs=[vm], out_specs=vm,
        scratch_shapes=[pltpu.VMEM((ndev, r, LANES), F32), pltpu.SemaphoreType.DMA((ndev - 1,)),
                        pltpu.SemaphoreType.DMA((ndev - 1,))],
        compiler_params=pltpu.CompilerParams(has_side_effects=True),
    )(vec)


def _row_tile(rows, cap=512):
    for t in range(min(rows, cap), 0, -1):
        if rows % t == 0 and (t % 16 == 0 or t == rows):
            return t
    raise ValueError(rows)


def _add_pairs(a, b, name):
    k, r, c = a.shape
    tr = _row_tile(r)

    def body(a_ref, b_ref, o_ref):
        o_ref[...] = (a_ref[...].astype(F32) + b_ref[...].astype(F32)).astype(o_ref.dtype)

    spec = pl.BlockSpec((None, tr, c), lambda i, j: (i, j, 0))
    return pl.pallas_call(body, name=name, grid=(k, r // tr), in_specs=[spec, spec], out_specs=spec,
                          out_shape=jax.ShapeDtypeStruct(a.shape, a.dtype),
                          compiler_params=_params(("parallel", "parallel")))(a, b)


def _final_sum(own, sib, others, name):
    r, c = own.shape
    tr = _row_tile(r)

    def body(a_ref, b_ref, o_ref_in, out_ref):
        acc = a_ref[...].astype(F32) + b_ref[...].astype(F32)
        for k in range(3):
            acc = acc + o_ref_in[k].astype(F32)
        out_ref[...] = acc

    spec = pl.BlockSpec((tr, c), lambda i: (i, 0))
    return pl.pallas_call(body, name=name, grid=(r // tr,),
                          in_specs=[spec, spec, pl.BlockSpec((3, tr, c), lambda i: (0, i, 0))], out_specs=spec,
                          out_shape=jax.ShapeDtypeStruct((r, c), F32),
                          compiler_params=_params(("parallel",)))(own, sib, others)


def _adamw(g, w, m, v, name):
    r, c = g.shape
    tr = _row_tile(r, 256)

    def body(g_ref, w_ref, m_ref, v_ref, d_ref, mo_ref, vo_ref):
        gv = g_ref[...]
        mn = ADAM_B1 * m_ref[...] + (1.0 - ADAM_B1) * gv
        vn = ADAM_B2 * v_ref[...] + (1.0 - ADAM_B2) * (gv * gv)
        m_hat = mn / (1.0 - ADAM_B1 ** ADAM_STEP)
        v_hat = vn / (1.0 - ADAM_B2 ** ADAM_STEP)
        d_ref[...] = -ADAM_LR * (m_hat / (jnp.sqrt(v_hat) + ADAM_EPS) + ADAM_WD * w_ref[...])
        mo_ref[...] = mn
        vo_ref[...] = vn

    spec = pl.BlockSpec((tr, c), lambda i: (i, 0))
    shp = jax.ShapeDtypeStruct((r, c), F32)
    return pl.pallas_call(body, name=name, grid=(r // tr,), in_specs=[spec] * 4, out_specs=[spec] * 3,
                          out_shape=[shp] * 3, compiler_params=_params(("parallel",)))(g, w, m, v)


def _pack(arrays):
    flat = jnp.concatenate([a.reshape(-1).astype(F32) for a in arrays])
    pad = (-flat.shape[0]) % (8 * LANES)
    return jnp.concatenate([flat, jnp.zeros((pad,), F32)]).reshape(-1, LANES)


def _unpack(packed, shapes):
    flat = packed.reshape(-1)
    out, off = [], 0
    for s in shapes:
        size = 1
        for dim in s:
            size *= dim
        out.append(flat[off:off + size].reshape(s))
        off += size
    return out


BIG = ("ffn1_w_in", "ffn1_w_out", "w_in", "w_out", "ffn2_w_in", "ffn2_w_out")
SMALL = ("ffn1_norm", "mix_norm", "fox_q_norm", "fox_k_norm", "fox_f_bias", "gdn_a_log", "gdn_dt_bias",
         "gdn_out_norm", "ffn2_norm", "gdn_conv")
WEIGHTS = ("ffn1_norm", "ffn1_w_in", "ffn1_w_out", "mix_norm", "w_in", "fox_q_norm", "fox_k_norm", "fox_f_bias",
           "gdn_conv", "gdn_a_log", "gdn_dt_bias", "gdn_out_norm", "w_out", "ffn2_norm", "ffn2_w_in", "ffn2_w_out")


def _step(x, target, w, m, v):
    xi, yi, ci = _mesh_pos()
    me = 2 * xi + yi
    depth = DEPTH
    d = x.shape[-1]

    nb, seq, _ = x.shape
    assert depth == 2

    p = {k: w[k] for k in SMALL if k != "gdn_conv"}
    for k in ("ffn1_w_in", "ffn1_w_out", "ffn2_w_in", "ffn2_w_out", "w_mix", "w_out", "gdn_conv"):
        p[k] = [None] * depth

    first, rest = BIG[:2], BIG[2:] + ("gdn_conv",)

    def shards(l, names):
        return [w[k][l] if k == "gdn_conv" else w[k][l].astype(BF) for k in names]

    def place(l, names, gathered):
        blocks = dict(zip(names, [lax.dynamic_update_index_in_dim(g, s, me, 0)
                                  for g, s in zip(gathered, shards(l, names))]))
        for k in ("ffn1_w_in", "ffn1_w_out", "ffn2_w_in", "ffn2_w_out"):
            if k in blocks:
                p[k][l] = blocks[k]
        if "w_in" in blocks:
            p["w_mix"][l] = _mix_to_padded(blocks["w_in"].transpose(1, 0, 2).reshape(d, N_IN))
            p["w_out"][l] = blocks["w_out"].reshape(2 * FOX_WIDTH, d)
            p["gdn_conv"][l] = blocks["gdn_conv"].transpose(1, 0, 2).reshape(CONV_WIDTH, -1)

    place(0, first, _gather_layer(shards(0, first), 0, "gather_first_ffn0"))
    xt = x.reshape(nb * seq, d)
    xt, saved0, gathered1 = _layer_fwd(
        xt, p, 0, nb, seq, _gather_layer(shards(1, first + rest), 1, rider=True),
        _gather_layer(shards(0, rest), 0, rider=True), lambda got: place(0, rest, got))
    place(1, first + rest, gathered1)
    xt, saved1, _ = _layer_fwd(xt, p, 1, nb, seq)
    loss, dx = _loss_grad(xt, target.reshape(nb * seq, d), "loss")

    def transport(g, names):
        out = []
        for k in names:
            if k == "w_in":
                out.append(g["w_in"].reshape(d, N_CHIPS, N_IN // N_CHIPS).transpose(1, 0, 2).astype(BF))
            elif k == "w_out":
                out.append(g["w_out"].reshape(N_CHIPS, -1, d))
            else:
                out.append(g[k])
        return out

    def chip_sums(g, l, names, tag):
        own = transport(g, names)
        sib = _to_sibling(own, l, f"grad{l}{tag}_to_sibling")
        return own, sib, [_add_pairs(a, b, f"grad{l}{tag}_chip_sum_{k}") for a, b, k in zip(own, sib, names)]

    dx, grads1, _, _ = _layer_bwd(dx, p, 1, saved1, nb, seq)
    own1 = transport(grads1, BIG)
    before = {}

    def after_ffn2(from_sibling):
        before["sib1"] = from_sibling
        sums1 = [_add_pairs(a, b, f"grad1_chip_sum_{k}") for a, b, k in zip(own1, from_sibling, BIG)]
        return _scatter_layer(sums1, 1, rider=True)

    def before_ffn1(g):
        before["own"], before["sib"], sums = chip_sums(g, 0, BIG[2:], "_rest")
        return _scatter_layer(sums, 0, rider=True)

    dx, grads0, chips1, chips0_rest = _layer_bwd(dx, p, 0, saved0, nb, seq, None, before_ffn1,
                                                 _to_sibling(own1, 1, rider=True), after_ffn2)
    sib1 = before["sib1"]
    own0, sib0, sums0 = chip_sums(grads0, 0, first, "_first")
    chips0 = _scatter_layer(sums0, 0, "grad0_first_to_chips") + chips0_rest
    own0, sib0 = own0 + before["own"], sib0 + before["sib"]
    grads = [grads0, grads1]
    dx = dx.reshape(nb, seq, d)

    mine = lambda a0, a1: jnp.where(ci == 0, a0, a1)
    at_me = lambda a: lax.dynamic_index_in_dim(a, me, 0, keepdims=False)
    reduced = [_final_sum(mine(at_me(own0[i]), at_me(own1[i])), mine(at_me(sib0[i]), at_me(sib1[i])),
                          mine(chips0[i], chips1[i]), f"grad_final_sum_{k}") for i, k in enumerate(BIG)]
    from_sib_final = _sibling_swap(reduced, "grad_swap_layers")
    full = {k: jnp.stack([jnp.where(ci == 0, a, b), jnp.where(ci == 0, b, a)])
            for k, a, b in zip(BIG, reduced, from_sib_final)}

    out_g, out_d, out_m, out_v = {}, {}, {}, {}
    for k in BIG:
        shp = w[k].shape
        two_d = lambda a: a.reshape(shp[0] * shp[1], shp[2])
        dl, mn, vn = _adamw(two_d(full[k]), two_d(w[k]), two_d(m[k]), two_d(v[k]), f"adamw_{k}")
        out_g[k], out_d[k], out_m[k], out_v[k] = full[k], dl.reshape(shp), mn.reshape(shp), vn.reshape(shp)

    small_local = [jnp.stack([grads[l][k] for l in range(depth)]) for k in SMALL]
    summed = _unpack(_small_all_reduce(_pack(small_local), "small_all_reduce"), [a.shape for a in small_local])
    sg = dict(zip(SMALL, summed))
    cs = w["gdn_conv"].shape[-1]
    sg["gdn_conv"] = lax.dynamic_slice_in_dim(sg["gdn_conv"], me * cs, cs, axis=2)
    shapes = [w[k].shape for k in SMALL]
    packs = [_pack([src[k] for k in SMALL]) for src in (sg, w, m, v)]
    dl, mn, vn = _adamw(*packs, "adamw_small")
    for k, a, b, c2 in zip(SMALL, _unpack(dl, shapes), _unpack(mn, shapes), _unpack(vn, shapes)):
        out_g[k], out_d[k], out_m[k], out_v[k] = sg[k], a, b, c2

    total = lax.psum(loss[0, 0], ("x", "y", "c"))
    return (total, dx, *[out_g[k] for k in WEIGHTS], *[out_d[k] for k in WEIGHTS],
            *[out_m[k] for k in WEIGHTS], *[out_v[k] for k in WEIGHTS])


def kernel(x, ffn1_norm, ffn1_w_in, ffn1_w_out, mix_norm, w_in, fox_q_norm, fox_k_norm, fox_f_bias, gdn_conv, gdn_a_log, gdn_dt_bias, gdn_out_norm, w_out, ffn2_norm, ffn2_w_in, ffn2_w_out, loss_target, m_ffn1_norm, m_ffn1_w_in, m_ffn1_w_out, m_mix_norm, m_w_in, m_fox_q_norm, m_fox_k_norm, m_fox_f_bias, m_gdn_conv, m_gdn_a_log, m_gdn_dt_bias, m_gdn_out_norm, m_w_out, m_ffn2_norm, m_ffn2_w_in, m_ffn2_w_out, v_ffn1_norm, v_ffn1_w_in, v_ffn1_w_out, v_mix_norm, v_w_in, v_fox_q_norm, v_fox_k_norm, v_fox_f_bias, v_gdn_conv, v_gdn_a_log, v_gdn_dt_bias, v_gdn_out_norm, v_w_out, v_ffn2_norm, v_ffn2_w_in, v_ffn2_w_out):
    w = dict(ffn1_norm=ffn1_norm, ffn1_w_in=ffn1_w_in, ffn1_w_out=ffn1_w_out, mix_norm=mix_norm, w_in=w_in,
             fox_q_norm=fox_q_norm, fox_k_norm=fox_k_norm, fox_f_bias=fox_f_bias, gdn_conv=gdn_conv,
             gdn_a_log=gdn_a_log, gdn_dt_bias=gdn_dt_bias, gdn_out_norm=gdn_out_norm, w_out=w_out,
             ffn2_norm=ffn2_norm, ffn2_w_in=ffn2_w_in, ffn2_w_out=ffn2_w_out)
    m = dict(ffn1_norm=m_ffn1_norm, ffn1_w_in=m_ffn1_w_in, ffn1_w_out=m_ffn1_w_out, mix_norm=m_mix_norm, w_in=m_w_in,
             fox_q_norm=m_fox_q_norm, fox_k_norm=m_fox_k_norm, fox_f_bias=m_fox_f_bias, gdn_conv=m_gdn_conv,
             gdn_a_log=m_gdn_a_log, gdn_dt_bias=m_gdn_dt_bias, gdn_out_norm=m_gdn_out_norm, w_out=m_w_out,
             ffn2_norm=m_ffn2_norm, ffn2_w_in=m_ffn2_w_in, ffn2_w_out=m_ffn2_w_out)
    v = dict(ffn1_norm=v_ffn1_norm, ffn1_w_in=v_ffn1_w_in, ffn1_w_out=v_ffn1_w_out, mix_norm=v_mix_norm, w_in=v_w_in,
             fox_q_norm=v_fox_q_norm, fox_k_norm=v_fox_k_norm, fox_f_bias=v_fox_f_bias, gdn_conv=v_gdn_conv,
             gdn_a_log=v_gdn_a_log, gdn_dt_bias=v_gdn_dt_bias, gdn_out_norm=v_gdn_out_norm, w_out=v_w_out,
             ffn2_norm=v_ffn2_norm, ffn2_w_in=v_ffn2_w_in, ffn2_w_out=v_ffn2_w_out)
    return _step(x, loss_target, w, m, v)
```

```python
import jax
import jax.numpy as jnp
from jax import lax
from jax.experimental import pallas as pl
from jax.experimental.pallas import tpu as pltpu

F32 = jnp.float32
BF = jnp.bfloat16
HI = lax.Precision.HIGHEST
MESH = pl.DeviceIdType.MESH

DEPTH = 2
FOX_HEADS = 8
FOX_HEAD_DIM = 64
FOX_WIDTH = 512
GDN_HEADS = 4
GDN_HEAD_DIM = 128
GDN_WIDTH = 512
CONV_WIDTH = 4
CHUNK = 64
EPS = 1e-6
N_IN = 3600
N_PAD = 3712
GATE_COL = 3584
LANES = 128
NEG = -1e30

ADAM_LR = 0.001
ADAM_B1 = 0.9
ADAM_B2 = 0.999
ADAM_EPS = 1e-08
ADAM_WD = 0.01
ADAM_STEP = 10

VMEM_LIMIT = 56 * 1024 * 1024


def _params(sem=None, **kw):
    return pltpu.CompilerParams(dimension_semantics=sem, vmem_limit_bytes=VMEM_LIMIT, **kw)


def _dot(a, b, precision=None):
    return jnp.dot(a, b, preferred_element_type=F32, precision=precision)


def _dot_nt(a, b, precision=None):
    return lax.dot_general(a, b, (((1,), (1,)), ((), ())), preferred_element_type=F32, precision=precision)


def _dot_tn(a, b, precision=None):
    return lax.dot_general(a, b, (((0,), (0,)), ((), ())), preferred_element_type=F32, precision=precision)


def _sigmoid(x):
    return 0.5 * jnp.tanh(0.5 * x) + 0.5


def _softplus(x):
    return jnp.maximum(x, 0.0) + jnp.log(1.0 + jnp.exp(-jnp.abs(x)))


def _log_sigmoid(x):
    return jnp.minimum(x, 0.0) - jnp.log(1.0 + jnp.exp(-jnp.abs(x)))


def _tile(n, t):
    t = min(n, t)
    assert n % t == 0, (n, t)
    return t


def _rms_fwd(x, gain):
    rstd = lax.rsqrt(jnp.mean(x * x, axis=-1, keepdims=True) + EPS)
    xhat = x * rstd
    return xhat * gain, xhat, rstd


def _rms_bwd(dy, xhat, rstd, gain):
    dxhat = dy * gain
    dx = rstd * (dxhat - xhat * jnp.mean(dxhat * xhat, axis=-1, keepdims=True))
    return dx, dy * xhat


def _full(shape):
    nd = len(shape)
    return pl.BlockSpec(shape, lambda *_: (0,) * nd)


HBM = pl.BlockSpec(memory_space=pltpu.HBM)


def _load_ffn_weights(win_hbm, wout_hbm, win_v, wout_v, sem):
    fr = wout_hbm.shape[1]
    copies = [pltpu.make_async_copy(win_hbm.at[s], win_v.at[s], sem.at[s]) for s in range(4)]
    copies += [pltpu.make_async_copy(wout_hbm.at[s], wout_v.at[pl.ds(s * fr, fr)], sem.at[4 + s])
               for s in range(4)]
    for c in copies:
        c.start()
    for c in copies:
        c.wait()


def _ffn_fwd(x, gain, win_g, wout_g, name, rider=None):
    t, d = x.shape
    _, _, fs = win_g.shape
    fr = wout_g.shape[1]
    tm = _tile(t, 512)
    r_in, r_out, r_sem = _rider_parts(rider)
    steps = t // tm

    def body(x_ref, g_ref, win_hbm, wout_hbm, *rest):
        rin, (xo_ref, h_ref) = rest[:len(r_in)], rest[len(r_in):len(r_in) + 2]
        rout = rest[len(r_in) + 2:len(r_in) + 2 + len(r_out)]
        win_v, wout_v, sem = rest[len(r_in) + 2 + len(r_out):len(r_in) + 5 + len(r_out)]
        riding = (rin, rout, rest[len(r_in) + 5 + len(r_out):])
        step = pl.program_id(0)
        _ride(rider, 0, step == 0, riding)
        _ride(rider, 1, step == (3 * steps) // 4, riding)

        @pl.when(step == 0)
        def _():
            _load_ffn_weights(win_hbm, wout_hbm, win_v, wout_v, sem)

        xv = x_ref[...]
        hn, _, _ = _rms_fwd(xv, g_ref[...])
        hn = hn.astype(BF)
        acc = jnp.zeros((tm, d), F32)
        for s in range(2):
            g = _dot(hn, win_v[s])
            u = _dot(hn, win_v[s + 2])
            h_ref[:, s * fs:(s + 1) * fs] = g.astype(BF)
            h_ref[:, (s + 2) * fs:(s + 3) * fs] = u.astype(BF)
            a = (g * _sigmoid(g) * u).astype(BF)
            acc = acc + _dot(a, wout_v[s * fs:(s + 1) * fs, :])
        xo_ref[...] = xv + 0.5 * acc
        _ride(rider, 2, step == steps - 1, riding)

    return pl.pallas_call(
        body, name=name, grid=(steps,),
        in_specs=[pl.BlockSpec((tm, d), lambda i: (i, 0)), _full((1, d)), HBM, HBM] + [HBM] * len(r_in),
        out_specs=[pl.BlockSpec((tm, d), lambda i: (i, 0)), pl.BlockSpec((tm, 4 * fs), lambda i: (i, 0))]
        + [HBM] * len(r_out),
        out_shape=[jax.ShapeDtypeStruct((t, d), F32), jax.ShapeDtypeStruct((t, 4 * fs), BF)] + r_out,
        scratch_shapes=[pltpu.VMEM((4, d, fs), BF), pltpu.VMEM((4 * fr, d), BF), pltpu.SemaphoreType.DMA((8,))]
        + r_sem,
        compiler_params=_params(("arbitrary",), has_side_effects=rider is not None),
    )(x, gain, win_g, wout_g, *r_in)


def _ffn_bwd(dy, x, h, gain, win_g, wout_g, name, rider=None):
    t, d = x.shape
    _, _, fs = win_g.shape
    fr = wout_g.shape[1]
    tm = _tile(t, 256)
    r_in, r_out, r_sem = _rider_parts(rider)
    steps = t // tm

    def body(dy_ref, x_ref, h_ref, g_ref, win_hbm, wout_hbm, *rest):
        rin, (dx_ref, dh_ref, a_ref, hn_ref, dyh_ref, dg_ref) = rest[:len(r_in)], rest[len(r_in):len(r_in) + 6]
        rout = rest[len(r_in) + 6:len(r_in) + 6 + len(r_out)]
        win_v, wout_v, sem = rest[len(r_in) + 6 + len(r_out):len(r_in) + 9 + len(r_out)]
        riding = (rin, rout, rest[len(r_in) + 9 + len(r_out):])
        step = pl.program_id(0)
        _ride(rider, 0, step == 0, riding)
        _ride(rider, 1, step == (3 * steps) // 4, riding)

        @pl.when(step == 0)
        def _():
            _load_ffn_weights(win_hbm, wout_hbm, win_v, wout_v, sem)
            dg_ref[...] = jnp.zeros_like(dg_ref)

        dyv = dy_ref[...]
        dyh = (0.5 * dyv).astype(BF)
        dyh_ref[...] = dyh
        dhn = jnp.zeros((tm, d), F32)
        for s in range(2):
            da = _dot_nt(dyh, wout_v[s * fs:(s + 1) * fs, :])
            g = h_ref[:, s * fs:(s + 1) * fs].astype(F32)
            u = h_ref[:, (s + 2) * fs:(s + 3) * fs].astype(F32)
            sg = _sigmoid(g)
            si = g * sg
            a_ref[:, s * fs:(s + 1) * fs] = (si * u).astype(BF)
            dgate = (da * u * (sg * (1.0 + g * (1.0 - sg)))).astype(BF)
            dup = (da * si).astype(BF)
            dh_ref[:, s * fs:(s + 1) * fs] = dgate
            dh_ref[:, (s + 2) * fs:(s + 3) * fs] = dup
            dhn = dhn + _dot_nt(dgate, win_v[s]) + _dot_nt(dup, win_v[s + 2])
        xv = x_ref[...]
        gain_v = g_ref[...]
        hn, xhat, rstd = _rms_fwd(xv, gain_v)
        hn_ref[...] = hn.astype(BF)
        dx, dgr = _rms_bwd(dhn, xhat, rstd, gain_v)
        dx_ref[...] = dyv + dx
        dg_ref[...] += jnp.sum(dgr, axis=0, keepdims=True)
        _ride(rider, 2, step == steps - 1, riding)

    row = lambda w: pl.BlockSpec((tm, w), lambda i: (i, 0))
    return pl.pallas_call(
        body, name=name, grid=(steps,),
        in_specs=[row(d), row(d), row(4 * fs), _full((1, d)), HBM, HBM] + [HBM] * len(r_in),
        out_specs=[row(d), row(4 * fs), row(2 * fs), row(d), row(d), _full((1, d))] + [HBM] * len(r_out),
        out_shape=[jax.ShapeDtypeStruct((t, d), F32), jax.ShapeDtypeStruct((t, 4 * fs), BF),
                   jax.ShapeDtypeStruct((t, 2 * fs), BF), jax.ShapeDtypeStruct((t, d), BF),
                   jax.ShapeDtypeStruct((t, d), BF), jax.ShapeDtypeStruct((1, d), F32)] + r_out,
        scratch_shapes=[pltpu.VMEM((4, d, fs), BF), pltpu.VMEM((4 * fr, d), BF), pltpu.SemaphoreType.DMA((8,))]
        + r_sem,
        compiler_params=_params(("arbitrary",), has_side_effects=rider is not None),
    )(dy, x, h, gain, win_g, wout_g, *r_in)


def _wgrad(a, b, out_shape, out_spec, tm, tn, name, tk=512):
    t, m = a.shape
    _, n = b.shape
    tk = _tile(t, tk)
    nk = t // tk

    def body(a_ref, b_ref, o_ref, acc):
        k = pl.program_id(2)

        @pl.when(k == 0)
        def _():
            acc[...] = jnp.zeros_like(acc)

        acc[...] += _dot_tn(a_ref[...], b_ref[...])

        @pl.when(k == nk - 1)
        def _():
            o_ref[...] = acc[...].astype(o_ref.dtype)

    return pl.pallas_call(
        body, name=name, grid=(m // tm, n // tn, nk),
        in_specs=[pl.BlockSpec((tk, tm), lambda i, j, k: (k, i)), pl.BlockSpec((tk, tn), lambda i, j, k: (k, j))],
        out_specs=out_spec, out_shape=out_shape,
        scratch_shapes=[pltpu.VMEM((tm, tn), F32)],
        compiler_params=_params(("parallel", "parallel", "arbitrary")),
    )(a, b)


def _wgrad_parts(a, parts, tm, name, tk=512):
    t, m = a.shape
    widths = [p.shape[1] for p in parts]
    n = sum(widths)
    tk = _tile(t, tk)
    nk = t // tk
    np_ = len(parts)

    def body(a_ref, *rest):
        b_refs, o_ref, acc = rest[:np_], rest[np_], rest[np_ + 1]
        k = pl.program_id(1)

        @pl.when(k == 0)
        def _():
            acc[...] = jnp.zeros_like(acc)

        av, off = a_ref[...], 0
        for b_ref, wd in zip(b_refs, widths):
            acc[:, off:off + wd] += _dot_tn(av, b_ref[...])
            off += wd

        @pl.when(k == nk - 1)
        def _():
            o_ref[...] = acc[...]

    return pl.pallas_call(
        body, name=name, grid=(m // tm, nk),
        in_specs=[pl.BlockSpec((tk, tm), lambda i, k: (k, i))]
        + [pl.BlockSpec((tk, wd), lambda i, k: (k, 0)) for wd in widths],
        out_specs=pl.BlockSpec((tm, n), lambda i, k: (i, 0)), out_shape=jax.ShapeDtypeStruct((m, n), F32),
        scratch_shapes=[pltpu.VMEM((tm, n), F32)],
        compiler_params=_params(("parallel", "arbitrary")),
    )(a, *parts)


def _norm_matmul(x, gain, w, name):
    t, d = x.shape
    n = w.shape[1]
    tm = _tile(t, 256)

    def body(x_ref, g_ref, w_ref, o_ref):
        hn, _, _ = _rms_fwd(x_ref[...], g_ref[...])
        o_ref[...] = _dot(hn.astype(BF), w_ref[...])

    return pl.pallas_call(
        body, name=name, grid=(t // tm,),
        in_specs=[pl.BlockSpec((tm, d), lambda i: (i, 0)), _full((1, d)), _full((d, n))],
        out_specs=pl.BlockSpec((tm, n), lambda i: (i, 0)),
        out_shape=jax.ShapeDtypeStruct((t, n), F32),
        compiler_params=_params(("parallel",)),
    )(x, gain, w)


def _norm_matmul_bwd(dres, dparts, x, gain, w, name):
    t, d = x.shape
    n = w.shape[1]
    tm = _tile(t, 256)
    widths = [a.shape[1] for a in dparts]
    assert sum(widths) == n
    k = len(dparts)

    def body(dr_ref, *rest):
        dp_refs, (x_ref, g_ref, w_ref, dx_ref, hn_ref, dg_ref) = rest[:k], rest[k:]

        @pl.when(pl.program_id(0) == 0)
        def _():
            dg_ref[...] = jnp.zeros_like(dg_ref)

        dhn, off = jnp.zeros((tm, d), F32), 0
        for dp_ref, wd in zip(dp_refs, widths):
            dhn = dhn + _dot_nt(dp_ref[...], w_ref[:, off:off + wd])
            off += wd
        gain_v = g_ref[...]
        hn, xhat, rstd = _rms_fwd(x_ref[...], gain_v)
        hn_ref[...] = hn.astype(BF)
        dx, dgr = _rms_bwd(dhn, xhat, rstd, gain_v)
        dx_ref[...] = dr_ref[...] + dx
        dg_ref[...] += jnp.sum(dgr, axis=0, keepdims=True)

    row = lambda wd: pl.BlockSpec((tm, wd), lambda i: (i, 0))
    return pl.pallas_call(
        body, name=name, grid=(t // tm,),
        in_specs=[row(d)] + [row(wd) for wd in widths] + [row(d), _full((1, d)), _full((d, n))],
        out_specs=[row(d), row(d), _full((1, d))],
        out_shape=[jax.ShapeDtypeStruct((t, d), F32), jax.ShapeDtypeStruct((t, d), BF),
                   jax.ShapeDtypeStruct((1, d), F32)],
        compiler_params=_params(("arbitrary",)),
    )(dres, *dparts, x, gain, w)


def _mix_out(x, yf, yg, w, name):
    t, d = x.shape
    kf = yf.shape[1]
    tm = _tile(t, 512)

    def body(x_ref, yf_ref, yg_ref, w_ref, o_ref):
        o_ref[...] = x_ref[...] + _dot(yf_ref[...], w_ref[0:kf, :]) + _dot(yg_ref[...], w_ref[kf:2 * kf, :])

    row = lambda wd: pl.BlockSpec((tm, wd), lambda i: (i, 0))
    return pl.pallas_call(
        body, name=name, grid=(t // tm,),
        in_specs=[row(d), row(kf), row(kf), _full((2 * kf, d))],
        out_specs=row(d), out_shape=jax.ShapeDtypeStruct((t, d), F32),
        compiler_params=_params(("parallel",)),
    )(x, yf, yg, w)


def _mix_out_bwd(dx, w, name):
    t, d = dx.shape
    kf = w.shape[0] // 2
    tm = _tile(t, 512)

    def body(dx_ref, w_ref, df_ref, dg_ref, dxb_ref):
        dxb = dx_ref[...].astype(BF)
        dxb_ref[...] = dxb
        df_ref[...] = _dot_nt(dxb, w_ref[0:kf, :]).astype(BF)
        dg_ref[...] = _dot_nt(dxb, w_ref[kf:2 * kf, :]).astype(BF)

    row = lambda wd: pl.BlockSpec((tm, wd), lambda i: (i, 0))
    return pl.pallas_call(
        body, name=name, grid=(t // tm,),
        in_specs=[row(d), _full((2 * kf, d))],
        out_specs=[row(kf), row(kf), row(d)],
        out_shape=[jax.ShapeDtypeStruct((t, kf), BF), jax.ShapeDtypeStruct((t, kf), BF),
                   jax.ShapeDtypeStruct((t, d), BF)],
        compiler_params=_params(("parallel",)),
    )(dx, w)


def _loss_grad(y, target, name):
    t, d = y.shape
    tm = _tile(t, 512)

    def body(y_ref, t_ref, l_ref, dy_ref):
        @pl.when(pl.program_id(0) == 0)
        def _():
            l_ref[...] = jnp.zeros_like(l_ref)

        diff = y_ref[...] - t_ref[...]
        dy_ref[...] = diff * (1.0 / d)
        part = jnp.sum(jnp.sum(diff * diff, axis=1, keepdims=True), axis=0, keepdims=True)
        l_ref[...] += part * (0.5 / d)

    row = pl.BlockSpec((tm, d), lambda i: (i, 0))
    return pl.pallas_call(
        body, name=name, grid=(t // tm,),
        in_specs=[row, row], out_specs=[_full((1, 1)), row],
        out_shape=[jax.ShapeDtypeStruct((1, 1), F32), jax.ShapeDtypeStruct((t, d), F32)],
        compiler_params=_params(("arbitrary",)),
    )(y, target)


def _head_sum_matrix(width, head):
    r = lax.broadcasted_iota(jnp.int32, (width, width), 0) // head
    c = lax.broadcasted_iota(jnp.int32, (width, width), 1) // head
    return (r == c).astype(BF)


def _head_mean(x, bd):
    return _dot(x.astype(BF), bd) * (1.0 / FOX_HEAD_DIM)


def _mask_dot(mask01, x):
    mb = mask01.astype(BF)
    hi = x.astype(BF)
    r1 = x - hi.astype(F32)
    mid = r1.astype(BF)
    lo = (r1 - mid.astype(F32)).astype(BF)
    return _dot(mb, hi) + _dot(mb, mid) + _dot(mb, lo)


def _fox_prep(proj, wq_t, wk_t, bias_pad, seq, name):
    t = proj.shape[0]
    ts = _tile(seq, 512)
    tpe = seq // ts
    scale = FOX_HEAD_DIM ** -0.5

    def body(q_ref, k_ref, v_ref, gt_ref, wq_ref, wk_ref, b_ref, qo_ref, ko_ref, vo_ref, cum_ref, carry):
        i = pl.program_id(0)
        bd = _head_sum_matrix(FOX_WIDTH, FOX_HEAD_DIM)

        def norm(xv, wv):
            ms = _head_mean(xv * xv, bd)
            return xv * lax.rsqrt(ms + EPS) * wv

        qo_ref[...] = (norm(q_ref[...], wq_ref[...]) * scale).astype(BF)
        ko_ref[...] = norm(k_ref[...], wk_ref[...]).astype(BF)
        vo_ref[...] = v_ref[...].astype(BF)

        @pl.when(i % tpe == 0)
        def _():
            carry[...] = jnp.zeros_like(carry)

        ls = _log_sigmoid(gt_ref[...] + b_ref[...])
        r = lax.broadcasted_iota(jnp.int32, (ts, ts), 0)
        c = lax.broadcasted_iota(jnp.int32, (ts, ts), 1)
        cum = _mask_dot(r >= c, ls) + carry[...]
        cum_ref[...] = cum
        carry[...] = cum[ts - 1:ts, :]

    blk = lambda j: pl.BlockSpec((ts, FOX_WIDTH), lambda i: (i, j))
    gate = pl.BlockSpec((ts, LANES), lambda i: (i, GATE_COL // LANES))
    out = pl.BlockSpec((ts, FOX_WIDTH), lambda i: (i, 0))
    return pl.pallas_call(
        body, name=name, grid=(t // ts,),
        in_specs=[blk(0), blk(1), blk(2), gate, _full((1, FOX_WIDTH)), _full((1, FOX_WIDTH)), _full((1, LANES))],
        out_specs=[out, out, out, pl.BlockSpec((ts, LANES), lambda i: (i, 0))],
        out_shape=[jax.ShapeDtypeStruct((t, FOX_WIDTH), BF)] * 3 + [jax.ShapeDtypeStruct((t, LANES), F32)],
        scratch_shapes=[pltpu.VMEM((1, LANES), F32)],
        compiler_params=_params(("arbitrary",)),
    )(proj, proj, proj, proj, wq_t, wk_t, bias_pad)


def _pick_lanes(x, lane_in_block, first_out_lane):
    r = lax.broadcasted_iota(jnp.int32, (FOX_WIDTH, LANES), 0)
    c = lax.broadcasted_iota(jnp.int32, (FOX_WIDTH, LANES), 1)
    sel = ((r % LANES == lane_in_block) & (c == first_out_lane + 2 * (r // LANES))).astype(BF)
    hi = x.astype(BF)
    r1 = x - hi.astype(F32)
    mid = r1.astype(BF)
    lo = (r1 - mid.astype(F32)).astype(BF)
    return _dot(hi, sel) + _dot(mid, sel) + _dot(lo, sel)


def _fox_prep_bwd(proj, dqa, dqb, dk, dv, dkx, wq_t, wk_t, bias_pad, seq, name):
    t = proj.shape[0]
    ts = _tile(seq, 512)
    tpe = seq // ts
    nt = t // ts
    scale = FOX_HEAD_DIM ** -0.5

    def body(q_ref, k_ref, gt_ref, dqa_ref, dqb_ref, dk_ref, dv_ref, dc_ref, wq_ref, wk_ref, b_ref,
             dp_ref, dff_ref, dwq_ref, dwk_ref, db_ref, carry):
        i = pl.program_id(0)
        first = (lax.broadcasted_iota(jnp.int32, (ts, FOX_WIDTH), 1) % LANES) < FOX_HEAD_DIM
        dq_all = jnp.where(first, dqa_ref[...], dqb_ref[...])
        ti = nt - 1 - i
        bd = _head_sum_matrix(FOX_WIDTH, FOX_HEAD_DIM)

        @pl.when(i == 0)
        def _():
            dwq_ref[...] = jnp.zeros_like(dwq_ref)
            dwk_ref[...] = jnp.zeros_like(dwk_ref)
            db_ref[...] = jnp.zeros_like(db_ref)

        def norm_bwd(xv, wv, dyv):
            ms = _head_mean(xv * xv, bd)
            rstd = lax.rsqrt(ms + EPS)
            xhat = xv * rstd
            dxhat = dyv * wv
            mean = _head_mean(dxhat * xhat, bd)
            return rstd * (dxhat - xhat * mean), jnp.sum(dyv * xhat, axis=0, keepdims=True)

        dxq, dwq = norm_bwd(q_ref[...], wq_ref[...], dq_all * scale)
        dxk, dwk = norm_bwd(k_ref[...], wk_ref[...], dk_ref[...])
        dp_ref[:, 0:FOX_WIDTH] = dxq.astype(BF)
        dp_ref[:, FOX_WIDTH:2 * FOX_WIDTH] = dxk.astype(BF)
        dp_ref[:, 2 * FOX_WIDTH:3 * FOX_WIDTH] = dv_ref[...].astype(BF)
        dwq_ref[...] += dwq
        dwk_ref[...] += dwk

        @pl.when(ti % tpe == tpe - 1)
        def _():
            carry[...] = jnp.zeros_like(carry)

        r = lax.broadcasted_iota(jnp.int32, (ts, ts), 0)
        c = lax.broadcasted_iota(jnp.int32, (ts, ts), 1)
        dkx = dc_ref[...]
        hd = FOX_HEAD_DIM
        dcum = (_pick_lanes(dqa_ref[...], hd, 0) + _pick_lanes(dqb_ref[...], 0, 1)
                - _pick_lanes(dkx, hd, 0) - _pick_lanes(dkx, 0, 1))
        dls = _mask_dot(c >= r, dcum) + carry[...]
        carry[...] = dls[0:1, :]
        z = gt_ref[...] + b_ref[...]
        lane = lax.broadcasted_iota(jnp.int32, (ts, LANES), 1)
        dff = jnp.where(lane < FOX_HEADS, dls * _sigmoid(-z), 0.0)
        dff_ref[...] = dff
        db_ref[...] += jnp.sum(dff, axis=0, keepdims=True)

        @pl.when(i == nt - 1)
        def _():
            fr = lax.broadcasted_iota(jnp.int32, (FOX_WIDTH, FOX_WIDTH), 0) % FOX_HEAD_DIM
            fc = lax.broadcasted_iota(jnp.int32, (FOX_WIDTH, FOX_WIDTH), 1) % FOX_HEAD_DIM
            fold = (fr == fc).astype(F32)
            dwq_ref[...] = _dot(dwq_ref[...], fold, HI)
            dwk_ref[...] = _dot(dwk_ref[...], fold, HI)

    rev = lambda w, j: pl.BlockSpec((ts, w), lambda i: (nt - 1 - i, j))
    return pl.pallas_call(
        body, name=name, grid=(nt,),
        in_specs=[rev(FOX_WIDTH, 0), rev(FOX_WIDTH, 1), rev(LANES, GATE_COL // LANES),
                  rev(FOX_WIDTH, 0), rev(FOX_WIDTH, 0), rev(FOX_WIDTH, 0), rev(FOX_WIDTH, 0), rev(FOX_WIDTH, 0),
                  _full((1, FOX_WIDTH)), _full((1, FOX_WIDTH)), _full((1, LANES))],
        out_specs=[rev(3 * FOX_WIDTH, 0), rev(LANES, 0), _full((1, FOX_WIDTH)), _full((1, FOX_WIDTH)),
                   _full((1, LANES))],
        out_shape=[jax.ShapeDtypeStruct((t, 3 * FOX_WIDTH), BF), jax.ShapeDtypeStruct((t, LANES), F32),
                   jax.ShapeDtypeStruct((1, FOX_WIDTH), F32), jax.ShapeDtypeStruct((1, FOX_WIDTH), F32),
                   jax.ShapeDtypeStruct((1, LANES), F32)],
        scratch_shapes=[pltpu.VMEM((1, LANES), F32)],
        compiler_params=_params(("arbitrary",)),
    )(proj, proj, proj, dqa, dqb, dk, dv, dkx, wq_t, wk_t, bias_pad)


class _Rider:
    def __init__(self, inputs, out_shapes, sems, phases):
        self.inputs, self.out_shapes, self.sems, self.phases = list(inputs), list(out_shapes), list(sems), phases


def _rider_parts(rider):
    if rider is None:
        return [], [], []
    return rider.inputs, rider.out_shapes, rider.sems


def _ride(rider, which, when, refs):
    if rider is not None:
        @pl.when(when)
        def _():
            rider.phases[which](*refs)


def _fox_attn(q, k, v, ck, nb, seq, name, rider=None):
    t = q.shape[0]
    tq = _tile(seq, 2048)
    nq = seq // tq
    npair = FOX_HEADS // 2
    hd = FOX_HEAD_DIM
    r_in, r_out, r_sem = _rider_parts(rider)
    steps = nb * npair * nq

    def body(q_ref, k_ref, v_ref, ck_ref, *rest):
        rin, (o_ref, lse_ref) = rest[:len(r_in)], rest[len(r_in):len(r_in) + 2]
        rout = rest[len(r_in) + 2:len(r_in) + 2 + len(r_out)]
        m_s, acc_s = rest[len(r_in) + 2 + len(r_out):len(r_in) + 4 + len(r_out)]
        riding = (rin, rout, rest[len(r_in) + 4 + len(r_out):])
        step = (pl.program_id(0) * npair + pl.program_id(1)) * nq + pl.program_id(2)
        _ride(rider, 0, step == 0, riding)
        _ride(rider, 1, step == (3 * steps) // 4, riding)
        qi = pl.program_id(2)
        lane = lax.broadcasted_iota(jnp.int32, (tq, LANES), 1)
        m_s[...] = jnp.full(m_s.shape, NEG, F32)
        acc_s[...] = jnp.zeros_like(acc_s)
        qv = q_ref[...]

        def tile(kj, on_diagonal):
            cols = pl.ds(pl.multiple_of(kj * tq, tq), tq)
            kv = k_ref[cols, :]
            vv = v_ref[cols, :]
            if on_diagonal:
                causal = (lax.broadcasted_iota(jnp.int32, (tq, tq), 0)
                          >= lax.broadcasted_iota(jnp.int32, (tq, tq), 1))
            ck = [ck_ref[hh, :, cols] for hh in range(2)]
            m_old = [m_s[hh] for hh in range(2)]
            acc_old = [acc_s[hh] for hh in range(2)]
            m_out, acc_out = [], []
            for hh in range(2):
                hm = (lane >= hd) if hh else (lane < hd)
                qh = jnp.where(hm, qv, jnp.zeros_like(qv))
                s = _dot_nt(qh, kv) - ck[hh]
                if on_diagonal:
                    s = jnp.where(causal, s, NEG)
                m_new = jnp.maximum(m_old[hh], jnp.max(s, axis=-1, keepdims=True))
                p = jnp.exp(s - m_new)
                alpha = jnp.exp(m_old[hh] - m_new)
                m_out.append(m_new)
                acc_out.append(alpha * acc_old[hh] + _dot(p.astype(BF), jnp.where(hm, vv, jnp.ones_like(vv))))
            for hh in range(2):
                m_s[hh] = m_out[hh]
                acc_s[hh] = acc_out[hh]

        def off_diagonal(kj, carry):
            tile(kj, False)
            return carry

        lax.fori_loop(0, qi, off_diagonal, 0)
        tile(qi, True)
        a0 = acc_s[0]
        a1 = acc_s[1]
        den = jnp.where(lane < hd, pltpu.roll(a0, hd, axis=1), pltpu.roll(a1, hd, axis=1))
        o_ref[...] = (jnp.where(lane < hd, a0, a1) / den).astype(o_ref.dtype)
        l0 = jnp.sum(jnp.where(lane == hd, a0, 0.0), axis=1, keepdims=True)
        l1 = jnp.sum(jnp.where(lane == 0, a1, 0.0), axis=1, keepdims=True)
        lse_ref[0] = m_s[0] + jnp.log(l0)
        lse_ref[1] = m_s[1] + jnp.log(l1)
        _ride(rider, 2, step == steps - 1, riding)

    qspec = pl.BlockSpec((tq, LANES), lambda b, p, i: (b * nq + i, p))
    kspec = pl.BlockSpec((seq, LANES), lambda b, p, i: (b, p))
    colspec = pl.BlockSpec((None, 2, tq, 1), lambda b, p, i: (b * npair + p, 0, i, 0))
    rowspec = pl.BlockSpec((None, 2, 1, seq), lambda b, p, i: (b * npair + p, 0, 0, 0))
    sem = ("arbitrary",) * 3 if rider else ("parallel",) * 3
    return pl.pallas_call(
        body, name=name, grid=(nb, npair, nq),
        in_specs=[qspec, kspec, kspec, rowspec] + [HBM] * len(r_in),
        out_specs=[qspec, colspec] + [HBM] * len(r_out),
        out_shape=[jax.ShapeDtypeStruct((t, FOX_WIDTH), BF), jax.ShapeDtypeStruct((nb * npair, 2, seq, 1), F32)]
        + r_out,
        scratch_shapes=[pltpu.VMEM((2, tq, 1), F32), pltpu.VMEM((2, tq, LANES), F32)] + r_sem,
        compiler_params=_params(sem, has_side_effects=rider is not None),
    )(q, k, v, ck, *r_in)


def _fox_attn_bwd(q, k, v, o, do, lse, ck, nb, seq, name, rider=None):
    t = q.shape[0]
    tq = _tile(seq, 1024)
    nq = seq // tq
    npair = FOX_HEADS // 2
    hd = FOX_HEAD_DIM
    r_in, r_out, r_sem = _rider_parts(rider)
    steps = nb * npair * nq

    def body(q_ref, k_ref, v_ref, o_ref, do_ref, lse_ref, ck_ref, *rest):
        rin, (dqa_ref, dqb_ref, dk_ref, dv_ref, dkx_ref) = rest[:len(r_in)], rest[len(r_in):len(r_in) + 5]
        rout = rest[len(r_in) + 5:len(r_in) + 5 + len(r_out)]
        dk_s, dv_s = rest[len(r_in) + 5 + len(r_out):len(r_in) + 7 + len(r_out)]
        riding = (rin, rout, rest[len(r_in) + 7 + len(r_out):])
        step = (pl.program_id(0) * npair + pl.program_id(1)) * nq + pl.program_id(2)
        _ride(rider, 0, step == 0, riding)
        _ride(rider, 1, step == (3 * steps) // 4, riding)
        kj = pl.program_id(2)
        lane = lax.broadcasted_iota(jnp.int32, (tq, LANES), 1)

        @pl.when(kj == 0)
        def _():
            dqa_ref[...] = jnp.zeros_like(dqa_ref)
            dqb_ref[...] = jnp.zeros_like(dqb_ref)

        dk_s[...] = jnp.zeros_like(dk_s)
        dv_s[...] = jnp.zeros_like(dv_s)
        kv = k_ref[...]
        vv = v_ref[...]

        def tile(qi, on_diagonal):
            rows = pl.ds(pl.multiple_of(qi * tq, tq), tq)
            qv = q_ref[rows, :]
            dov = do_ref[rows, :]
            prod = dov.astype(F32) * o_ref[rows, :].astype(F32)
            if on_diagonal:
                causal = (lax.broadcasted_iota(jnp.int32, (tq, tq), 0)
                          >= lax.broadcasted_iota(jnp.int32, (tq, tq), 1))
            for hh, dq_ref in ((0, dqa_ref), (1, dqb_ref)):
                hm = (lane >= hd) if hh else (lane < hd)
                zero = jnp.zeros_like(qv)
                one = jnp.ones_like(qv)
                doh = jnp.where(hm, dov, zero)
                delta = jnp.sum(jnp.where(hm, prod, 0.0), axis=-1, keepdims=True)
                s = _dot_nt(jnp.where(hm, qv, zero), kv) - ck_ref[hh]
                if on_diagonal:
                    s = jnp.where(causal, s, NEG)
                p = jnp.exp(s - lse_ref[hh, rows, :])
                dp = _dot_nt(doh, vv)
                dsb = (p * (dp - delta)).astype(BF)
                dv_s[...] += _dot_tn(p.astype(BF), doh)
                dk_s[hh] += _dot_tn(dsb, jnp.where(hm, qv, one))
                dq_ref[rows, :] += _dot(dsb, jnp.where(hm, kv, one))

        def off_diagonal(qi, carry):
            tile(qi, False)
            return carry

        tile(kj, True)
        lax.fori_loop(kj + 1, nq, off_diagonal, 0)
        dk_ref[...] = jnp.where(lane < hd, dk_s[0], dk_s[1])
        dkx_ref[...] = jnp.where(lane < hd, dk_s[1], dk_s[0])
        dv_ref[...] = dv_s[...]
        _ride(rider, 2, step == steps - 1, riding)

    kspec = pl.BlockSpec((tq, LANES), lambda b, p, j: (b * nq + j, p))
    full_q = pl.BlockSpec((seq, LANES), lambda b, p, j: (b, p))
    colspec = pl.BlockSpec((None, 2, seq, 1), lambda b, p, j: (b * npair + p, 0, 0, 0))
    rowspec = pl.BlockSpec((None, 2, 1, tq), lambda b, p, j: (b * npair + p, 0, 0, j))
    sem = ("arbitrary",) * 3 if rider else ("parallel", "parallel", "arbitrary")
    return pl.pallas_call(
        body, name=name, grid=(nb, npair, nq),
        in_specs=[full_q, kspec, kspec, full_q, full_q, colspec, rowspec] + [HBM] * len(r_in),
        out_specs=[full_q, full_q, kspec, kspec, kspec] + [HBM] * len(r_out),
        out_shape=[jax.ShapeDtypeStruct((t, FOX_WIDTH), F32)] * 5 + r_out,
        scratch_shapes=[pltpu.VMEM((2, tq, LANES), F32), pltpu.VMEM((tq, LANES), F32)] + r_sem,
        compiler_params=_params(sem, has_side_effects=rider is not None),
    )(q, k, v, o, do, lse, ck, *r_in)


GDN_QKV = 3 * GDN_WIDTH
GDN_COL = 3 * FOX_WIDTH
GG_COL = GDN_COL + GDN_QKV
A_LANE = FOX_HEADS
B_LANE = FOX_HEADS + GDN_HEADS
HALO = 8


def _gate_lanes(ts):
    lane = lax.broadcasted_iota(jnp.int32, (ts, LANES), 1)
    return (lane >= A_LANE) & (lane < B_LANE), (lane >= B_LANE) & (lane < B_LANE + GDN_HEADS)


def _chunk_tri(ts, upper):
    r = lax.broadcasted_iota(jnp.int32, (ts, ts), 0)
    c = lax.broadcasted_iota(jnp.int32, (ts, ts), 1)
    same = (r // CHUNK) == (c // CHUNK)
    return (same & ((c >= r) if upper else (r >= c))).astype(F32)


def _shift_rows(x, edge, k, down):
    ts = x.shape[0]
    row = lax.broadcasted_iota(jnp.int32, (HALO, x.shape[1]), 0)
    if down:
        rolled = pltpu.roll(x, k, axis=0)
        patch = jnp.where(row < k, pltpu.roll(edge, k, axis=0), rolled[:HALO])
        return jnp.concatenate([patch, rolled[HALO:]], axis=0)
    rolled = pltpu.roll(x, ts - k, axis=0)
    patch = jnp.where(row >= HALO - k, pltpu.roll(edge, HALO - k, axis=0), rolled[ts - HALO:])
    return jnp.concatenate([rolled[:ts - HALO], patch], axis=0)


def _conv_silu(x, before, w):
    taps = [_shift_rows(x, before, CONV_WIDTH - 1 - kk, True) for kk in range(CONV_WIDTH - 1)] + [x]
    c = w[0:1, :] * taps[0]
    for kk in range(1, CONV_WIDTH):
        c = c + w[kk:kk + 1, :] * taps[kk]
    return taps, c, c * _sigmoid(c)


def _gdn_prep(proj, conv_w, a_pad, dt_pad, seq, name):
    t = proj.shape[0]
    ts = _tile(seq, 256)
    tpe = seq // ts
    qscale = GDN_HEAD_DIM ** -0.5

    def body(x_ref, gt_ref, w_ref, a_ref, dt_ref, qo_ref, ko_ref, vo_ref, go_ref, tail):
        i = pl.program_id(0)
        xv = x_ref[...]
        before = jnp.where(i % tpe == 0, jnp.zeros((HALO, GDN_QKV), F32), tail[...])
        tail[...] = xv[ts - HALO:]
        _, _, s = _conv_silu(xv, before, w_ref[...])
        for h in range(GDN_HEADS):
            for base, ref, sc in ((0, qo_ref, qscale), (GDN_WIDTH, ko_ref, 1.0)):
                xh = s[:, base + h * LANES: base + (h + 1) * LANES]
                r = lax.rsqrt(jnp.sum(xh * xh, axis=-1, keepdims=True) + EPS)
                ref[:, h * LANES:(h + 1) * LANES] = (xh * (r * sc)).astype(BF)
        vo_ref[...] = s[:, 2 * GDN_WIDTH:].astype(BF)
        gate = gt_ref[...]
        g_raw = -jnp.exp(a_ref[...]) * _softplus(gate + dt_ref[...])
        gc = _mask_dot(_chunk_tri(ts, False), g_raw)
        is_a, is_b = _gate_lanes(ts)
        go_ref[...] = jnp.where(is_a, gc, jnp.where(is_b, _sigmoid(gate), 0.0))

    out = pl.BlockSpec((ts, GDN_WIDTH), lambda i: (i, 0))
    lanes = pl.BlockSpec((ts, LANES), lambda i: (i, 0))
    return pl.pallas_call(
        body, name=name, grid=(t // ts,),
        in_specs=[pl.BlockSpec((ts, GDN_QKV), lambda i: (i, GDN_COL // GDN_QKV)),
                  pl.BlockSpec((ts, LANES), lambda i: (i, GATE_COL // LANES)),
                  _full((CONV_WIDTH, GDN_QKV)), _full((1, LANES)), _full((1, LANES))],
        out_specs=[out, out, out, lanes],
        out_shape=[jax.ShapeDtypeStruct((t, GDN_WIDTH), BF)] * 3 + [jax.ShapeDtypeStruct((t, LANES), F32)],
        scratch_shapes=[pltpu.VMEM((HALO, GDN_QKV), F32)],
        compiler_params=_params(("arbitrary",)),
    )(proj, proj, conv_w, a_pad, dt_pad)


def _gdn_prep_bwd(proj, dq, dk, dv, dgates, dff, conv_w, a_pad, dt_pad, seq, name):
    t = proj.shape[0]
    ts = _tile(seq, 256)
    tpe = seq // ts
    nt = t // ts
    qscale = GDN_HEAD_DIM ** -0.5
    hb = ts // HALO

    def body(x_ref, halo_ref, gt_ref, dq_ref, dk_ref, dv_ref, dgt_ref, dff_ref, w_ref, a_ref, dt_ref,
             dx_ref, dgo_ref, dw_ref, da_ref, ddt_ref, dsl, carry):
        i = pl.program_id(0)
        ti = nt - 1 - i

        @pl.when(i == 0)
        def _():
            dw_ref[...] = jnp.zeros_like(dw_ref)
            da_ref[...] = jnp.zeros_like(da_ref)
            ddt_ref[...] = jnp.zeros_like(ddt_ref)

        halo = halo_ref[...]
        before = jnp.where(ti % tpe == 0, jnp.zeros_like(halo), halo)
        w = w_ref[...]
        taps, c, s = _conv_silu(x_ref[...], before, w)
        for h in range(GDN_HEADS):
            for base, ref, sc in ((0, dq_ref, qscale), (GDN_WIDTH, dk_ref, 1.0)):
                lo = base + h * LANES
                xh = s[:, lo:lo + LANES]
                r = lax.rsqrt(jnp.sum(xh * xh, axis=-1, keepdims=True) + EPS)
                y = xh * r
                dy = ref[:, h * LANES:(h + 1) * LANES] * sc
                dsl[:, lo:lo + LANES] = r * (dy - y * jnp.sum(dy * y, axis=-1, keepdims=True))
        dsl[:, 2 * GDN_WIDTH:] = dv_ref[...]
        sg = _sigmoid(c)
        dc = dsl[...] * (sg * (1.0 + c * (1.0 - sg)))
        nxt = carry[...]
        after = jnp.where(ti % tpe == tpe - 1, jnp.zeros_like(nxt), nxt)
        carry[...] = dc[0:HALO, :]
        dx = w[CONV_WIDTH - 1:CONV_WIDTH, :] * dc
        for kk in range(CONV_WIDTH - 1):
            dx = dx + w[kk:kk + 1, :] * _shift_rows(dc, after, CONV_WIDTH - 1 - kk, False)
        dx_ref[...] = dx.astype(BF)
        for kk in range(CONV_WIDTH):
            dw_ref[kk:kk + 1, :] += jnp.sum(dc * taps[kk], axis=0, keepdims=True)
        gate = gt_ref[...]
        dgt = dgt_ref[...]
        is_a, is_b = _gate_lanes(ts)
        dg_raw = _mask_dot(_chunk_tri(ts, True), jnp.where(is_a, dgt, 0.0))
        z = gate + dt_ref[...]
        na = -jnp.exp(a_ref[...])
        dga = dg_raw * na * _sigmoid(z)
        beta = _sigmoid(gate)
        dgb = jnp.where(is_b, dgt * beta * (1.0 - beta), 0.0)
        dgo_ref[...] = (dff_ref[...] + dga + dgb).astype(BF)
        ddt_ref[...] += jnp.sum(dga, axis=0, keepdims=True)
        da_ref[...] += jnp.sum(dg_raw * na * _softplus(z), axis=0, keepdims=True)

    rev = lambda wd, j: pl.BlockSpec((ts, wd), lambda i: (nt - 1 - i, j))
    halo_spec = pl.BlockSpec((HALO, GDN_QKV), lambda i: (jnp.maximum((nt - 1 - i) * hb - 1, 0), GDN_COL // GDN_QKV))
    return pl.pallas_call(
        body, name=name, grid=(nt,),
        in_specs=[rev(GDN_QKV, GDN_COL // GDN_QKV), halo_spec, rev(LANES, GATE_COL // LANES),
                  rev(GDN_WIDTH, 0), rev(GDN_WIDTH, 0), rev(GDN_WIDTH, 0), rev(LANES, 0), rev(LANES, 0),
                  _full((CONV_WIDTH, GDN_QKV)), _full((1, LANES)), _full((1, LANES))],
        out_specs=[rev(GDN_QKV, 0), rev(LANES, 0), _full((CONV_WIDTH, GDN_QKV)), _full((1, LANES)),
                   _full((1, LANES))],
        out_shape=[jax.ShapeDtypeStruct((t, GDN_QKV), BF), jax.ShapeDtypeStruct((t, LANES), BF),
                   jax.ShapeDtypeStruct((CONV_WIDTH, GDN_QKV), F32), jax.ShapeDtypeStruct((1, LANES), F32),
                   jax.ShapeDtypeStruct((1, LANES), F32)],
        scratch_shapes=[pltpu.VMEM((ts, GDN_QKV), F32), pltpu.VMEM((HALO, GDN_QKV), F32)],
        compiler_params=_params(("arbitrary",)),
    )(proj, proj, proj, dq, dk, dv, dgates, dff, conv_w, a_pad, dt_pad)


PAIR = 2 * CHUNK


def _split_bf16(a):
    hi = a.astype(BF)
    return hi, (a - hi.astype(F32)).astype(BF)


def _dot3(a, b, dims=(((1,), (0,)), ((), ()))):
    ah, al = _split_bf16(a)
    bh, bl = _split_bf16(b)
    (ca,), (cb,) = dims[0]
    return lax.dot_general(jnp.concatenate([ah, al, ah], axis=ca), jnp.concatenate([bh, bh, bl], axis=cb), dims,
                           preferred_element_type=F32)


def _inv_unit_lower(a):
    r = lax.broadcasted_iota(jnp.int32, (PAIR, PAIR), 0)
    c = lax.broadcasted_iota(jnp.int32, (PAIR, PAIR), 1)
    tm = (r == c).astype(F32) - a
    pw = _dot3(a, a)
    for _ in range(4):
        x = _dot3(jnp.concatenate([tm, pw], axis=0), pw)
        tm = tm + x[:PAIR]
        pw = x[PAIR:]
    return tm + _dot3(tm, pw)


def _gdn_pair_local(q, k, v, gc, gr, b):
    r = lax.broadcasted_iota(jnp.int32, (PAIR, PAIR), 0)
    c = lax.broadcasted_iota(jnp.int32, (PAIR, PAIR), 1)
    same = (r // CHUNK) == (c // CHUNK)
    incl = same & (r >= c)
    strict = same & (r > c)
    dm = jnp.exp(jnp.where(incl, gc - gr, NEG))
    e = jnp.exp(gc)
    kb = k * b
    vb = v * b
    kbe = kb * e
    kq = _dot_nt(jnp.concatenate([kb, q], axis=0).astype(BF), k.astype(BF))
    amat = jnp.where(strict, kq[:PAIR] * dm, 0.0)
    pmat = jnp.where(incl, kq[PAIR:] * dm, 0.0)
    lane = lax.broadcasted_iota(jnp.int32, (1, PAIR), 1)
    gl_a = jnp.sum(jnp.where(lane == CHUNK - 1, gr, 0.0), axis=1, keepdims=True)
    gl_b = jnp.sum(jnp.where(lane == PAIR - 1, gr, 0.0), axis=1, keepdims=True)
    ridx = lax.broadcasted_iota(jnp.int32, (PAIR, 1), 0)
    edec = jnp.exp(jnp.where(ridx < CHUNK, gl_a, gl_b) - gc)
    return dict(dm=dm, e=e, kb=kb, vb=vb, kbe=kbe, amat=amat, pmat=pmat, gl_a=gl_a, gl_b=gl_b, edec=edec,
                kd=k * edec, qd=q * e, incl=incl, strict=strict, ridx=ridx)


def _gdn_pair_states(loc, tb, s_a):
    uw = _dot(tb, jnp.concatenate([loc["vb"], loc["kbe"]], axis=1).astype(BF))
    u, w = uw[:, :LANES], uw[:, LANES:]
    qd, kd, c = loc["qd"], loc["kd"], CHUNK
    xa = _dot(jnp.concatenate([qd[:c], w[:c]], axis=0).astype(BF), s_a.astype(BF))
    vn_a = u[:c] - xa[c:]
    s_b = s_a * jnp.exp(loc["gl_a"]) + _dot_tn(kd[:c].astype(BF), vn_a.astype(BF))
    xb = _dot(jnp.concatenate([qd[c:], w[c:]], axis=0).astype(BF), s_b.astype(BF))
    vn_b = u[c:] - xb[c:]
    s_c = s_b * jnp.exp(loc["gl_b"]) + _dot_tn(kd[c:].astype(BF), vn_b.astype(BF))
    vn = jnp.concatenate([vn_a, vn_b], axis=0)
    o = jnp.concatenate([xa[:c], xb[:c]], axis=0) + _dot(loc["pmat"].astype(BF), vn.astype(BF))
    return w, vn, o, s_b, s_c


GDN_SEG = 512


def _gdn_specs(nb, seq, reverse):
    n = seq // CHUNK
    seg = _tile(seq, GDN_SEG)
    nseg = seq // seg
    sp = seg // PAIR
    at = (lambda s: nseg - 1 - s) if reverse else (lambda s: s)
    blk = pl.BlockSpec((nb, seg, GDN_WIDTH), lambda s: (0, at(s), 0))
    gg = pl.BlockSpec((nb, seg, GDN_WIDTH), lambda s: (0, at(s), GG_COL // GDN_WIDTH))
    gates = pl.BlockSpec((nb, seg, LANES), lambda s: (0, at(s), 0))
    rowb = pl.BlockSpec((nb, GDN_HEADS, sp, HALO, PAIR), lambda s: (0, 0, at(s), 0, 0))
    per_pair = pl.BlockSpec((nb, GDN_HEADS, sp, PAIR, PAIR), lambda s: (0, 0, at(s), 0, 0))
    return n, seg, nseg, sp, blk, gg, gates, rowb, per_pair


def _head_column(gt, lane, index):
    return jnp.sum(jnp.where(lane == index, gt, 0.0), axis=1, keepdims=True)


def _gdn_head_inputs(qkv_refs, gt_ref, gr_ref, rows, pi, lane, chains):
    per_chain = []
    for b, hh in chains:
        gt = gt_ref[b, rows, :]
        cols = slice(hh * LANES, (hh + 1) * LANES)
        per_chain.append([r[b, rows, cols].astype(F32) for r in qkv_refs]
                         + [_head_column(gt, lane, A_LANE + hh), gr_ref[b, hh, pi][0:1, :],
                            _head_column(gt, lane, B_LANE + hh)])
    return [jnp.stack(xs) for xs in zip(*per_chain)]


def _gdn_pair_fwd(qv, kv, vv, gcv, gr, bv, s_a):
    loc = _gdn_pair_local(qv, kv, vv, gcv, gr, bv)
    tf = _inv_unit_lower(loc["amat"])
    _, _, o, _, s_c = _gdn_pair_states(loc, tf.astype(BF), s_a)
    return tf, o, s_c


def _gdn_fwd(q, k, v, proj, gates, grow, wn, nb, seq, name):
    n, seg, nseg, sp, blk, gg, gates_spec, rowb, per_pair = _gdn_specs(nb, seq, False)
    chains = [(b, hh) for b in range(nb) for hh in range(GDN_HEADS)]

    def body(q_ref, k_ref, v_ref, gg_ref, gt_ref, gr_ref, wn_ref, y_ref, tn_ref, sn_ref, s_ref):
        @pl.when(pl.program_id(0) == 0)
        def _():
            s_ref[...] = jnp.zeros_like(s_ref)

        wnv = wn_ref[...]
        lane = lax.broadcasted_iota(jnp.int32, (PAIR, LANES), 1)

        def step(pi, carry):
            rows = pl.ds(pl.multiple_of(pi * PAIR, PAIR), PAIR)
            ins = _gdn_head_inputs((q_ref, k_ref, v_ref), gt_ref, gr_ref, rows, pi, lane, chains)
            s_a = s_ref[...]
            tf, o, s_c = jax.vmap(_gdn_pair_fwd)(*ins, s_a)
            s_ref[...] = s_c
            for c, (b, hh) in enumerate(chains):
                cols = slice(hh * LANES, (hh + 1) * LANES)
                tn_ref[b, hh, pi] = tf[c]
                sn_ref[b, hh, pi] = s_a[c]
                g = gg_ref[b, rows, cols]
                oh = o[c]
                rstd = lax.rsqrt(jnp.mean(oh * oh, axis=-1, keepdims=True) + EPS)
                y_ref[b, rows, cols] = (oh * rstd * wnv * (g * _sigmoid(g))).astype(BF)
            return carry

        lax.fori_loop(0, sp, step, 0)

    saved = jax.ShapeDtypeStruct((nb, GDN_HEADS, n // 2, PAIR, PAIR), F32)
    return pl.pallas_call(
        body, name=name, grid=(nseg,),
        in_specs=[blk, blk, blk, gg, gates_spec, rowb, _full((1, LANES))],
        out_specs=[blk, per_pair, per_pair],
        out_shape=[jax.ShapeDtypeStruct((nb, seq, GDN_WIDTH), BF), saved, saved],
        scratch_shapes=[pltpu.VMEM((len(chains), GDN_HEAD_DIM, GDN_HEAD_DIM), F32)],
        compiler_params=_params(("arbitrary",)),
    )(q, k, v, proj, gates, grow, wn)


def _gdn_pair_bwd(qv, kv, vv, gcv, gr, bv, tf, s_a, dsp, g, dyv, wnv):
    c = CHUNK
    loc = _gdn_pair_local(qv, kv, vv, gcv, gr, bv)
    tm = tf.astype(BF)
    kb, vb, kbe, e, dm = loc["kb"], loc["vb"], loc["kbe"], loc["e"], loc["dm"]
    kd, qd, pmat, amat = loc["kd"], loc["qd"], loc["pmat"], loc["amat"]
    w, vn, o, s_b, _ = _gdn_pair_states(loc, tm, s_a)
    sg = _sigmoid(g)
    silu = g * sg
    rstd = lax.rsqrt(jnp.mean(o * o, axis=-1, keepdims=True) + EPS)
    xhat = o * rstd
    dwn = jnp.sum(dyv * xhat * silu, axis=0, keepdims=True)
    dgg = dyv * xhat * wnv * (sg * (1.0 + g * (1.0 - sg)))
    dxhat = dyv * wnv * silu
    do = rstd * (dxhat - xhat * jnp.mean(dxhat * xhat, axis=-1, keepdims=True))
    dob = do.astype(BF)
    tot = lambda x: jnp.sum(jnp.sum(x, axis=1, keepdims=True), axis=0, keepdims=True)
    rsum = lambda x: jnp.sum(x, axis=1, keepdims=True)
    cat = lambda xs, ax=0: jnp.concatenate(xs, axis=ax)
    wb = w.astype(BF)
    qdb = qd.astype(BF)
    kdb = kd.astype(BF)
    vnb = vn.astype(BF)
    egl_a = jnp.exp(loc["gl_a"])
    egl_b = jnp.exp(loc["gl_b"])
    ptdo = _dot_tn(pmat.astype(BF), dob)
    dspb = dsp.astype(BF)
    dvn_b = ptdo[c:] + _dot(kdb[c:], dspb)
    dkd_b = _dot_nt(vnb[c:], dspb)
    dgl_b = egl_b * tot(s_b * dsp) + tot(dkd_b * kd[c:])
    dsm = egl_b * dsp + _dot_tn(cat([qdb[c:], -wb[c:]]), cat([dob[c:], dvn_b.astype(BF)]))
    dsmb = dsm.astype(BF)
    dvn_a = ptdo[:c] + _dot(kdb[:c], dsmb)
    dkd_a = _dot_nt(vnb[:c], dsmb)
    dgl_a = egl_a * tot(s_a * dsm) + tot(dkd_a * kd[:c])
    ds_new = egl_a * dsm + _dot_tn(cat([qdb[:c], -wb[:c]]), cat([dob[:c], dvn_a.astype(BF)]))
    ya = _dot_nt(cat([dob[:c], dvn_a.astype(BF)]), s_a.astype(BF))
    yb = _dot_nt(cat([dob[c:], dvn_b.astype(BF)]), s_b.astype(BF))
    dqd = cat([ya[:c], yb[:c]])
    dw = -cat([ya[c:], yb[c:]])
    dvn = cat([dvn_a, dvn_b])
    dkd = cat([dkd_a, dkd_b])
    dq = dqd * e
    dgc = rsum(dqd * qd) - rsum(dkd * kd)
    dk = dkd * loc["edec"]
    dpm = jnp.where(loc["incl"], _dot_nt(dob, vnb), 0.0)
    duw = cat([dvn, dw], 1).astype(BF)
    dt = _dot_nt(duw, cat([vb, kbe], 1).astype(BF))
    tt = _dot_tn(tm, duw)
    dvb, dkbe = tt[:, :LANES], tt[:, LANES:]
    tn_dims = (((0,), (0,)), ((), ()))
    nt_dims = (((1,), (1,)), ((), ()))
    da = jnp.where(loc["strict"], -_dot3(_dot3(tf, dt, tn_dims), tf, nt_dims), 0.0)
    st = cat([da * dm, dpm * dm]).astype(BF)
    z = _dot(st, kv.astype(BF))
    dkb = z[:PAIR] + dkbe * e
    dq = dq + z[PAIR:]
    dk = dk + _dot_tn(st, cat([kb, qv]).astype(BF))
    gmat = dpm * pmat + da * amat
    dgc = dgc + rsum(dkbe * kbe) + rsum(gmat)
    ridx = loc["ridx"]
    dgc = dgc + jnp.where(ridx == c - 1, dgl_a, 0.0) + jnp.where(ridx == PAIR - 1, dgl_b, 0.0)
    dgc_row = jnp.sum(gmat, axis=0, keepdims=True)
    db = rsum(dvb * vv) + rsum(dkb * kv)
    return dq, dk + dkb * bv, dvb * bv, dgg, dgc, dgc_row, db, dwn, ds_new


def _gdn_bwd(q, k, v, proj, gates, grow, wn, tinv_all, states_all, dy, nb, seq, name):
    n, seg, nseg, sp, blk, gg, gates_spec, rowb, per_pair = _gdn_specs(nb, seq, True)
    dh = GDN_HEAD_DIM
    chains = [(b, hh) for b in range(nb) for hh in range(GDN_HEADS)]

    def body(q_ref, k_ref, v_ref, gg_ref, gt_ref, gr_ref, wn_ref, tn_ref, sn_ref, dy_ref,
             dq_ref, dk_ref, dv_ref, dgg_ref, dgt_ref, dwn_ref, ds_ref):
        @pl.when(pl.program_id(0) == 0)
        def _():
            dwn_ref[...] = jnp.zeros_like(dwn_ref)
            ds_ref[...] = jnp.zeros_like(ds_ref)

        wnv = wn_ref[...]
        lane = lax.broadcasted_iota(jnp.int32, (PAIR, LANES), 1)

        def bwd_step(j, carry):
            pi = sp - 1 - j
            rows = pl.ds(pl.multiple_of(pi * PAIR, PAIR), PAIR)
            ins = _gdn_head_inputs((q_ref, k_ref, v_ref), gt_ref, gr_ref, rows, pi, lane, chains)
            lanes_of = lambda hh: slice(hh * LANES, (hh + 1) * LANES)
            saved = [jnp.stack([r[b, hh, pi] for b, hh in chains]) for r in (tn_ref, sn_ref)]
            g2 = jnp.stack([gg_ref[b, rows, lanes_of(hh)] for b, hh in chains])
            dy2 = jnp.stack([dy_ref[b, rows, lanes_of(hh)].astype(F32) for b, hh in chains])
            dq, dk, dv, dgg, dgc, dgc_row, db, dwn, ds_new = jax.vmap(
                _gdn_pair_bwd, in_axes=(0,) * 11 + (None,))(*ins, *saved, ds_ref[...], g2, dy2, wnv)
            ds_ref[...] = ds_new
            dgt = [jnp.zeros((PAIR, LANES), F32) for _ in range(nb)]
            for c, (b, hh) in enumerate(chains):
                cols = lanes_of(hh)
                dq_ref[b, rows, cols] = dq[c]
                dk_ref[b, rows, cols] = dk[c]
                dv_ref[b, rows, cols] = dv[c]
                dgg_ref[b, rows, cols] = dgg[c].astype(BF)
                dwn_ref[...] += dwn[c]
                row_as_col = jnp.transpose(jnp.broadcast_to(dgc_row[c], (PAIR, LANES)))
                dgt[b] = (dgt[b] + jnp.where(lane == A_LANE + hh, dgc[c] - row_as_col, 0.0)
                          + jnp.where(lane == B_LANE + hh, db[c], 0.0))
            for b in range(nb):
                dgt_ref[b, rows, :] = dgt[b]
            return carry

        lax.fori_loop(0, sp, bwd_step, 0)

    f32_out = jax.ShapeDtypeStruct((nb, seq, GDN_WIDTH), F32)
    return pl.pallas_call(
        body, name=name, grid=(nseg,),
        in_specs=[blk, blk, blk, gg, gates_spec, rowb, _full((1, LANES)), per_pair, per_pair, blk],
        out_specs=[blk, blk, blk, blk, gates_spec, _full((1, LANES))],
        out_shape=[f32_out, f32_out, f32_out, jax.ShapeDtypeStruct((nb, seq, GDN_WIDTH), BF),
                   jax.ShapeDtypeStruct((nb, seq, LANES), F32), jax.ShapeDtypeStruct((1, LANES), F32)],
        scratch_shapes=[pltpu.VMEM((len(chains), dh, dh), F32)],
        compiler_params=_params(("arbitrary",)),
    )(q, k, v, proj, gates, grow, wn, tinv_all, states_all, dy)


def _mix_to_padded(w):
    pad = jnp.zeros(w.shape[:-1] + (N_PAD - N_IN,), w.dtype)
    return jnp.concatenate([w[..., 0:1536], w[..., 1544:3080], w[..., 3088:3600], w[..., 1536:1544],
                            w[..., 3080:3088], pad], axis=-1)


def _pad_lanes(vec, start):
    return jnp.pad(vec[None, :], ((0, 0), (start, LANES - start - vec.shape[0])))


def _heads_to_rows(block, lane0, nheads, nb, seq):
    return block[:, lane0:lane0 + nheads].reshape(nb, seq, nheads).transpose(0, 2, 1).reshape(nb * nheads, seq)


def _mixer_small(p, l):
    wq_t = jnp.tile(p["fox_q_norm"][l], FOX_HEADS)[None, :]
    wk_t = jnp.tile(p["fox_k_norm"][l], FOX_HEADS)[None, :]
    bias = _pad_lanes(p["fox_f_bias"][l], 0)
    a_pad = _pad_lanes(p["gdn_a_log"][l], A_LANE)
    dt_pad = _pad_lanes(p["gdn_dt_bias"][l], A_LANE)
    wn = p["gdn_out_norm"][l][None, :]
    return wq_t, wk_t, bias, a_pad, dt_pad, wn


def _layer_fwd(x, p, l, nb, seq, rider=None, ffn1_rider=None, after_ffn1=None):
    npair = FOX_HEADS // 2
    n = seq // CHUNK
    x1, h1, *rode1 = _ffn_fwd(x, p["ffn1_norm"][l][None, :], p["ffn1_w_in"][l], p["ffn1_w_out"][l],
                              f"ffn1_fwd_{l}", ffn1_rider)
    if after_ffn1 is not None:
        after_ffn1(rode1)
    wq_t, wk_t, bias, a_pad, dt_pad, wn = _mixer_small(p, l)
    proj = _norm_matmul(x1, p["mix_norm"][l][None, :], p["w_mix"][l], f"mix_in_{l}")
    fq, fk, fv, cum = _fox_prep(proj, wq_t, wk_t, bias, seq, f"fox_prep_{l}")
    c8 = _heads_to_rows(cum, 0, FOX_HEADS, nb, seq)
    ck = c8.reshape(nb * npair, 2, 1, seq)
    o, lse, *rode = _fox_attn(fq, fk, fv, ck, nb, seq, f"fox_attn_{l}", rider)
    gq, gk, gv, gates = _gdn_prep(proj, p["gdn_conv"][l], a_pad, dt_pad, seq, f"gdn_prep_{l}")
    gc4 = _heads_to_rows(gates, A_LANE, GDN_HEADS, nb, seq)
    grow = jnp.broadcast_to(gc4.reshape(nb, GDN_HEADS, n // 2, 1, PAIR), (nb, GDN_HEADS, n // 2, HALO, PAIR))
    per_example = lambda a: a.reshape(nb, seq, a.shape[-1])
    gq, gk, gv, gates = per_example(gq), per_example(gk), per_example(gv), per_example(gates)
    y, tinv, states = _gdn_fwd(gq, gk, gv, per_example(proj), gates, grow, wn, nb, seq, f"gdn_fwd_{l}")
    y = y.reshape(nb * seq, GDN_WIDTH)
    x2 = _mix_out(x1, o, y, p["w_out"][l], f"mix_out_{l}")
    x3, h2 = _ffn_fwd(x2, p["ffn2_norm"][l][None, :], p["ffn2_w_in"][l], p["ffn2_w_out"][l], f"ffn2_fwd_{l}")
    saved = dict(x=x, h1=h1, x1=x1, proj=proj, fq=fq, fk=fk, fv=fv, ck=ck, o=o, lse=lse,
                 gq=gq, gk=gk, gv=gv, gates=gates, grow=grow, tinv=tinv, states=states, y=y, x2=x2, h2=h2)
    return x3, saved, rode


def _ffn_grads(dy, x, h, gain, win, wout, l, tag, rider=None):
    t, d = x.shape
    fs = win.shape[2]
    dx, dh, a, hn, dyh, dgain, *rode = _ffn_bwd(dy, x, h, gain, win, wout, f"{tag}_bwd_{l}", rider)
    g_in = _wgrad(hn, dh, jax.ShapeDtypeStruct((4, d, fs), BF),
                  pl.BlockSpec((None, d, fs), lambda i, j, k: (j, i, 0)), d, fs, f"{tag}_gw_in_{l}")
    g_out = _wgrad(a, dyh, jax.ShapeDtypeStruct((2 * fs, d), BF),
                   pl.BlockSpec((fs, d), lambda i, j, k: (i, j)), fs, d, f"{tag}_gw_out_{l}")
    return dx, dgain[0], g_in, g_out.reshape(4, fs // 2, d), rode


def _layer_bwd(dx3, p, l, sv, nb, seq, rider=None, before_ffn1=None, ffn2_rider=None, after_ffn2=None):
    npair = FOX_HEADS // 2
    d = dx3.shape[1]
    wq_t, wk_t, bias, a_pad, dt_pad, wn = _mixer_small(p, l)
    g = {}
    dx2, g["ffn2_norm"], g["ffn2_w_in"], g["ffn2_w_out"], rode2 = _ffn_grads(
        dx3, sv["x2"], sv["h2"], p["ffn2_norm"][l][None, :], p["ffn2_w_in"][l], p["ffn2_w_out"][l], l, "ffn2",
        ffn2_rider)
    if after_ffn2 is not None:
        rider = after_ffn2(rode2)
    dyf, dyg, dxb = _mix_out_bwd(dx2, p["w_out"][l], f"mix_out_bwd_{l}")
    half = lambda a, nm: _wgrad(a, dxb, jax.ShapeDtypeStruct((FOX_WIDTH, d), BF),
                                pl.BlockSpec((FOX_WIDTH, d), lambda i, j, k: (i, j)), FOX_WIDTH, d, nm)
    g["w_out"] = jnp.concatenate([half(sv["o"], f"gw_out_fox_{l}"), half(sv["y"], f"gw_out_gdn_{l}")], axis=0)
    dqa, dqb, dk, dv, dkx, *rode = _fox_attn_bwd(sv["fq"], sv["fk"], sv["fv"], sv["o"], dyf, sv["lse"], sv["ck"],
                                                 nb, seq, f"fox_attn_bwd_{l}", rider)

    dpf, dff, dwq, dwk, dbias = _fox_prep_bwd(sv["proj"], dqa, dqb, dk, dv, dkx, wq_t, wk_t, bias, seq,
                                              f"fox_prep_bwd_{l}")
    g["fox_q_norm"] = dwq[0, :FOX_HEAD_DIM]
    g["fox_k_norm"] = dwk[0, :FOX_HEAD_DIM]
    g["fox_f_bias"] = dbias[0, :FOX_HEADS]
    per_example = lambda a: a.reshape(nb, seq, a.shape[-1])
    flat = lambda a: a.reshape(nb * seq, a.shape[-1])
    dgq, dgk, dgv, dgg, dgates, dwn = _gdn_bwd(
        sv["gq"], sv["gk"], sv["gv"], per_example(sv["proj"]), sv["gates"], sv["grow"], wn, sv["tinv"],
        sv["states"], per_example(dyg), nb, seq, f"gdn_bwd_{l}")
    dgq, dgk, dgv, dgg, dgates = flat(dgq), flat(dgk), flat(dgv), flat(dgg), flat(dgates)
    dpg, dgate_blk, dconv, da, ddt = _gdn_prep_bwd(sv["proj"], dgq, dgk, dgv, dgates, dff, p["gdn_conv"][l],
                                                   a_pad, dt_pad, seq, f"gdn_prep_bwd_{l}")
    g["gdn_conv"] = dconv
    g["gdn_a_log"] = da[0, A_LANE:B_LANE]
    g["gdn_dt_bias"] = ddt[0, A_LANE:B_LANE]
    g["gdn_out_norm"] = dwn[0]
    dparts = [dpf, dpg, dgg, dgate_blk]
    dx1, hnm, dgm = _norm_matmul_bwd(dx2, dparts, sv["x1"], p["mix_norm"][l][None, :], p["w_mix"][l],
                                     f"mix_in_bwd_{l}")
    g["mix_norm"] = dgm[0]
    gp = _wgrad_parts(hnm, dparts, d // 2, f"gw_mix_{l}")
    gate = GATE_COL
    g["w_in"] = jnp.concatenate([gp[:, :GDN_COL], gp[:, gate:gate + FOX_HEADS], gp[:, GDN_COL:GG_COL],
                                 gp[:, gate + A_LANE:gate + B_LANE + GDN_HEADS], gp[:, GG_COL:gate]], axis=1)
    ffn1_rider = before_ffn1(g) if before_ffn1 is not None else None
    dx0, g["ffn1_norm"], g["ffn1_w_in"], g["ffn1_w_out"], rode1 = _ffn_grads(
        dx1, sv["x"], sv["h1"], p["ffn1_norm"][l][None, :], p["ffn1_w_in"][l], p["ffn1_w_out"][l], l, "ffn1",
        ffn1_rider)
    return dx0, g, rode, rode1


def _local_step(x, target, p):
    nb, seq, d = x.shape
    xt = x.reshape(nb * seq, d)
    saved = []
    for l in range(DEPTH):
        xt, sv, _ = _layer_fwd(xt, p, l, nb, seq)
        saved.append(sv)
    loss, dx = _loss_grad(xt, target.reshape(nb * seq, d), "loss")
    grads = [None] * DEPTH
    for l in reversed(range(DEPTH)):
        dx, grads[l], _, _ = _layer_bwd(dx, p, l, saved[l], nb, seq)
    return loss, dx.reshape(nb, seq, d), grads


N_CHIPS = 4


def _mesh_pos():
    return lax.axis_index("x"), lax.axis_index("y"), lax.axis_index("c")


def _other_chips(x, y):
    return [(1 - x, y), (x, 1 - y), (1 - x, 1 - y)]


def _remote(src, dst, send_sem, recv_sem, to):
    return pltpu.make_async_remote_copy(src_ref=src, dst_ref=dst, send_sem=send_sem, recv_sem=recv_sem,
                                        device_id=to, device_id_type=MESH)


def _hbm_call(body, name, ins, out_shape, scratch):
    return pl.pallas_call(
        body, name=name, out_shape=out_shape, in_specs=[HBM] * len(ins),
        out_specs=jax.tree.map(lambda _: HBM, out_shape), scratch_shapes=scratch,
        compiler_params=pltpu.CompilerParams(has_side_effects=True),
    )(*ins)


def _gather_phases(n, layer):
    def copies(ins, outs, sems):
        send1, recv1, send2, recv2 = sems
        x, y, c = _mesh_pos()
        out, back, fwd = [], [], []
        for i in range(n):
            for j, (px, py) in enumerate(_other_chips(x, y)):
                k = 3 * i + j
                blk = outs[i].at[2 * px + py]
                out.append(_remote(ins[i], outs[i].at[2 * x + y], send1.at[k], recv1.at[k], (px, py, c)))
                back.append(_remote(blk, blk, send1.at[k], recv1.at[k], (px, py, c)))
                fwd.append(_remote(blk, blk, send2.at[k], recv2.at[k], (x, y, 1 - c)))
        return c, out, back, fwd

    def first(ins, outs, sems):
        c, out, _, _ = copies(ins, outs, sems)

        @pl.when(c == layer)
        def _():
            for cp in out:
                cp.start()

    def middle(ins, outs, sems):
        c, _, back, fwd = copies(ins, outs, sems)

        @pl.when(c == layer)
        def _():
            for arrived, onward in zip(back, fwd):
                arrived.wait_recv()
                onward.start()

    def last(ins, outs, sems):
        c, out, _, fwd = copies(ins, outs, sems)

        @pl.when(c == layer)
        def _():
            for cp in out + fwd:
                cp.wait_send()

        @pl.when(c != layer)
        def _():
            for cp in fwd:
                cp.wait_recv()

    return first, middle, last


def _scatter_phases(n, layer):
    def copies(ins, outs, sems):
        send, recv = sems
        x, y, c = _mesh_pos()
        return c, [_remote(ins[i].at[2 * px + py], outs[i].at[j], send.at[3 * i + j], recv.at[3 * i + j], (px, py, c))
                   for i in range(n) for j, (px, py) in enumerate(_other_chips(x, y))]

    def first(ins, outs, sems):
        c, cps = copies(ins, outs, sems)

        @pl.when(c == layer)
        def _():
            for cp in cps:
                cp.start()

    def middle(ins, outs, sems):
        pass

    def last(ins, outs, sems):
        c, cps = copies(ins, outs, sems)

        @pl.when(c == layer)
        def _():
            for cp in cps:
                cp.wait()

    return first, middle, last


def _exchange(blocks, out_shapes, n_sems, phases, name, rider):
    sems = [pltpu.SemaphoreType.DMA((3 * len(blocks),))] * n_sems
    if rider:
        return _Rider(blocks, out_shapes, sems, phases)
    n = len(blocks)

    def body(*refs):
        for phase in phases:
            phase(refs[:n], refs[n:2 * n], refs[2 * n:])

    return list(_hbm_call(body, name, blocks, out_shapes, sems))


def _gather_layer(blocks, layer, name=None, rider=False):
    outs = [jax.ShapeDtypeStruct((N_CHIPS,) + b.shape, b.dtype) for b in blocks]
    return _exchange(blocks, outs, 4, _gather_phases(len(blocks), layer), name, rider)


def _scatter_layer(sums, layer, name=None, rider=False):
    outs = [jax.ShapeDtypeStruct((3,) + s.shape[1:], s.dtype) for s in sums]
    return _exchange(sums, outs, 2, _scatter_phases(len(sums), layer), name, rider)


def _to_sibling(gs, layer, name=None, rider=False):
    n = len(gs)

    def copies(ins, outs, sems):
        send, recv = sems
        x, y, c = _mesh_pos()
        return c, [_remote(ins[i], outs[i], send.at[i], recv.at[i], (x, y, 1 - c)) for i in range(n)]

    def first(ins, outs, sems):
        c, cps = copies(ins, outs, sems)

        @pl.when(c != layer)
        def _():
            for cp in cps:
                cp.start()

    def middle(ins, outs, sems):
        pass

    def last(ins, outs, sems):
        c, cps = copies(ins, outs, sems)

        @pl.when(c != layer)
        def _():
            for cp in cps:
                cp.wait_send()

        @pl.when(c == layer)
        def _():
            for cp in cps:
                cp.wait_recv()

    sems = [pltpu.SemaphoreType.DMA((n,))] * 2
    outs = [jax.ShapeDtypeStruct(g.shape, g.dtype) for g in gs]
    if rider:
        return _Rider(gs, outs, sems, (first, middle, last))

    def body(*refs):
        for phase in (first, middle, last):
            phase(refs[:n], refs[n:2 * n], refs[2 * n:])

    return list(_hbm_call(body, name, gs, outs, sems))


def _sibling_swap(rs, name):
    n = len(rs)

    def body(*refs):
        ins, outs = refs[:n], refs[n:2 * n]
        send, recv = refs[2 * n:]
        x, y, c = _mesh_pos()
        cps = [_remote(ins[i], outs[i], send.at[i], recv.at[i], (x, y, 1 - c)) for i in range(n)]
        for cp in cps:
            cp.start()
        for cp in cps:
            cp.wait()

    sem = pltpu.SemaphoreType.DMA((n,))
    return _hbm_call(body, name, rs, [jax.ShapeDtypeStruct(r.shape, r.dtype) for r in rs], [sem, sem])


def _small_all_reduce(vec, name):
    r = vec.shape[0]
    ndev = 8

    def body(v_ref, o_ref, buf, send, recv):
        x, y, c = _mesh_pos()
        me = 4 * x + 2 * y + c
        buf[me] = v_ref[...]
        cps = []
        for rel in range(1, ndev):
            px = 1 - x if rel & 4 else x
            py = 1 - y if rel & 2 else y
            pc = 1 - c if rel & 1 else c
            cps.append((_remote(v_ref, buf.at[me], send.at[rel - 1], recv.at[rel - 1], (px, py, pc)),
                        4 * px + 2 * py + pc))
        for cp, _ in cps:
            cp.start()
        for k, (cp, peer) in enumerate(cps):
            slot = buf.at[peer]
            _remote(slot, slot, send.at[k], recv.at[k], (x, y, c)).wait_recv()
        for cp, _ in cps:
            cp.wait_send()
        acc = buf[0]
        for k in range(1, ndev):
            acc = acc + buf[k]
        o_ref[...] = acc

    vm = pl.BlockSpec(memory_space=pltpu.VMEM)
    return pl.pallas_call(
        body, name=name, out_shape=jax.ShapeDtypeStruct(vec.shape, F32), in_specs=[vm], out_specs=vm,
        scratch_shapes=[pltpu.VMEM((ndev, r, LANES), F32), pltpu.SemaphoreType.DMA((ndev - 1,)),
                        pltpu.SemaphoreType.DMA((ndev - 1,))],
        compiler_params=pltpu.CompilerParams(has_side_effects=True),
    )(vec)


def _row_tile(rows, cap=512):
    for t in range(min(rows, cap), 0, -1):
        if rows % t == 0 and (t % 16 == 0 or t == rows):
            return t
    raise ValueError(rows)


def _add_pairs(a, b, name):
    k, r, c = a.shape
    tr = _row_tile(r)

    def body(a_ref, b_ref, o_ref):
        o_ref[...] = (a_ref[...].astype(F32) + b_ref[...].astype(F32)).astype(o_ref.dtype)

    spec = pl.BlockSpec((None, tr, c), lambda i, j: (i, j, 0))
    return pl.pallas_call(body, name=name, grid=(k, r // tr), in_specs=[spec, spec], out_specs=spec,
                          out_shape=jax.ShapeDtypeStruct(a.shape, a.dtype),
                          compiler_params=_params(("parallel", "parallel")))(a, b)


def _final_sum(own, sib, others, name):
    r, c = own.shape
    tr = _row_tile(r)

    def body(a_ref, b_ref, o_ref_in, out_ref):
        acc = a_ref[...].astype(F32) + b_ref[...].astype(F32)
        for k in range(3):
            acc = acc + o_ref_in[k].astype(F32)
        out_ref[...] = acc

    spec = pl.BlockSpec((tr, c), lambda i: (i, 0))
    return pl.pallas_call(body, name=name, grid=(r // tr,),
                          in_specs=[spec, spec, pl.BlockSpec((3, tr, c), lambda i: (0, i, 0))], out_specs=spec,
                          out_shape=jax.ShapeDtypeStruct((r, c), F32),
                          compiler_params=_params(("parallel",)))(own, sib, others)


def _adamw(g, w, m, v, name):
    r, c = g.shape
    tr = _row_tile(r, 256)

    def body(g_ref, w_ref, m_ref, v_ref, d_ref, mo_ref, vo_ref):
        gv = g_ref[...]
        mn = ADAM_B1 * m_ref[...] + (1.0 - ADAM_B1) * gv
        vn = ADAM_B2 * v_ref[...] + (1.0 - ADAM_B2) * (gv * gv)
        m_hat = mn / (1.0 - ADAM_B1 ** ADAM_STEP)
        v_hat = vn / (1.0 - ADAM_B2 ** ADAM_STEP)
        d_ref[...] = -ADAM_LR * (m_hat / (jnp.sqrt(v_hat) + ADAM_EPS) + ADAM_WD * w_ref[...])
        mo_ref[...] = mn
        vo_ref[...] = vn

    spec = pl.BlockSpec((tr, c), lambda i: (i, 0))
    shp = jax.ShapeDtypeStruct((r, c), F32)
    return pl.pallas_call(body, name=name, grid=(r // tr,), in_specs=[spec] * 4, out_specs=[spec] * 3,
                          out_shape=[shp] * 3, compiler_params=_params(("parallel",)))(g, w, m, v)


def _pack(arrays):
    flat = jnp.concatenate([a.reshape(-1).astype(F32) for a in arrays])
    pad = (-flat.shape[0]) % (8 * LANES)
    return jnp.concatenate([flat, jnp.zeros((pad,), F32)]).reshape(-1, LANES)


def _unpack(packed, shapes):
    flat = packed.reshape(-1)
    out, off = [], 0
    for s in shapes:
        size = 1
        for dim in s:
            size *= dim
        out.append(flat[off:off + size].reshape(s))
        off += size
    return out


BIG = ("ffn1_w_in", "ffn1_w_out", "w_in", "w_out", "ffn2_w_in", "ffn2_w_out")
SMALL = ("ffn1_norm", "mix_norm", "fox_q_norm", "fox_k_norm", "fox_f_bias", "gdn_a_log", "gdn_dt_bias",
         "gdn_out_norm", "ffn2_norm", "gdn_conv")
WEIGHTS = ("ffn1_norm", "ffn1_w_in", "ffn1_w_out", "mix_norm", "w_in", "fox_q_norm", "fox_k_norm", "fox_f_bias",
           "gdn_conv", "gdn_a_log", "gdn_dt_bias", "gdn_out_norm", "w_out", "ffn2_norm", "ffn2_w_in", "ffn2_w_out")


def _step(x, target, w, m, v):
    xi, yi, ci = _mesh_pos()
    me = 2 * xi + yi
    depth = DEPTH
    d = x.shape[-1]

    nb, seq, _ = x.shape
    assert depth == 2

    p = {k: w[k] for k in SMALL if k != "gdn_conv"}
    for k in ("ffn1_w_in", "ffn1_w_out", "ffn2_w_in", "ffn2_w_out", "w_mix", "w_out", "gdn_conv"):
        p[k] = [None] * depth

    first, rest = BIG[:2], BIG[2:] + ("gdn_conv",)

    def shards(l, names):
        return [w[k][l] if k == "gdn_conv" else w[k][l].astype(BF) for k in names]

    def place(l, names, gathered):
        blocks = dict(zip(names, [lax.dynamic_update_index_in_dim(g, s, me, 0)
                                  for g, s in zip(gathered, shards(l, names))]))
        for k in ("ffn1_w_in", "ffn1_w_out", "ffn2_w_in", "ffn2_w_out"):
            if k in blocks:
                p[k][l] = blocks[k]
        if "w_in" in blocks:
            p["w_mix"][l] = _mix_to_padded(blocks["w_in"].transpose(1, 0, 2).reshape(d, N_IN))
            p["w_out"][l] = blocks["w_out"].reshape(2 * FOX_WIDTH, d)
            p["gdn_conv"][l] = blocks["gdn_conv"].transpose(1, 0, 2).reshape(CONV_WIDTH, -1)

    place(0, first, _gather_layer(shards(0, first), 0, "gather_first_ffn0"))
    xt = x.reshape(nb * seq, d)
    xt, saved0, gathered1 = _layer_fwd(
        xt, p, 0, nb, seq, _gather_layer(shards(1, first + rest), 1, rider=True),
        _gather_layer(shards(0, rest), 0, rider=True), lambda got: place(0, rest, got))
    place(1, first + rest, gathered1)
    xt, saved1, _ = _layer_fwd(xt, p, 1, nb, seq)
    loss, dx = _loss_grad(xt, target.reshape(nb * seq, d), "loss")

    def transport(g, names):
        out = []
        for k in names:
            if k == "w_in":
                out.append(g["w_in"].reshape(d, N_CHIPS, N_IN // N_CHIPS).transpose(1, 0, 2).astype(BF))
            elif k == "w_out":
                out.append(g["w_out"].reshape(N_CHIPS, -1, d))
            else:
                out.append(g[k])
        return out

    def chip_sums(g, l, names, tag):
        own = transport(g, names)
        sib = _to_sibling(own, l, f"grad{l}{tag}_to_sibling")
        return own, sib, [_add_pairs(a, b, f"grad{l}{tag}_chip_sum_{k}") for a, b, k in zip(own, sib, names)]

    dx, grads1, _, _ = _layer_bwd(dx, p, 1, saved1, nb, seq)
    own1 = transport(grads1, BIG)
    before = {}

    def after_ffn2(from_sibling):
        before["sib1"] = from_sibling
        sums1 = [_add_pairs(a, b, f"grad1_chip_sum_{k}") for a, b, k in zip(own1, from_sibling, BIG)]
        return _scatter_layer(sums1, 1, rider=True)

    def before_ffn1(g):
        before["own"], before["sib"], sums = chip_sums(g, 0, BIG[2:], "_rest")
        return _scatter_layer(sums, 0, rider=True)

    dx, grads0, chips1, chips0_rest = _layer_bwd(dx, p, 0, saved0, nb, seq, None, before_ffn1,
                                                 _to_sibling(own1, 1, rider=True), after_ffn2)
    sib1 = before["sib1"]
    own0, sib0, sums0 = chip_sums(grads0, 0, first, "_first")
    chips0 = _scatter_layer(sums0, 0, "grad0_first_to_chips") + chips0_rest
    own0, sib0 = own0 + before["own"], sib0 + before["sib"]
    grads = [grads0, grads1]
    dx = dx.reshape(nb, seq, d)

    mine = lambda a0, a1: jnp.where(ci == 0, a0, a1)
    at_me = lambda a: lax.dynamic_index_in_dim(a, me, 0, keepdims=False)
    reduced = [_final_sum(mine(at_me(own0[i]), at_me(own1[i])), mine(at_me(sib0[i]), at_me(sib1[i])),
                          mine(chips0[i], chips1[i]), f"grad_final_sum_{k}") for i, k in enumerate(BIG)]
    from_sib_final = _sibling_swap(reduced, "grad_swap_layers")
    full = {k: jnp.stack([jnp.where(ci == 0, a, b), jnp.where(ci == 0, b, a)])
            for k, a, b in zip(BIG, reduced, from_sib_final)}

    out_g, out_d, out_m, out_v = {}, {}, {}, {}
    for k in BIG:
        shp = w[k].shape
        two_d = lambda a: a.reshape(shp[0] * shp[1], shp[2])
        dl, mn, vn = _adamw(two_d(full[k]), two_d(w[k]), two_d(m[k]), two_d(v[k]), f"adamw_{k}")
        out_g[k], out_d[k], out_m[k], out_v[k] = full[k], dl.reshape(shp), mn.reshape(shp), vn.reshape(shp)

    small_local = [jnp.stack([grads[l][k] for l in range(depth)]) for k in SMALL]
    summed = _unpack(_small_all_reduce(_pack(small_local), "small_all_reduce"), [a.shape for a in small_local])
    sg = dict(zip(SMALL, summed))
    cs = w["gdn_conv"].shape[-1]
    sg["gdn_conv"] = lax.dynamic_slice_in_dim(sg["gdn_conv"], me * cs, cs, axis=2)
    shapes = [w[k].shape for k in SMALL]
    packs = [_pack([src[k] for k in SMALL]) for src in (sg, w, m, v)]
    dl, mn, vn = _adamw(*packs, "adamw_small")
    for k, a, b, c2 in zip(SMALL, _unpack(dl, shapes), _unpack(mn, shapes), _unpack(vn, shapes)):
        out_g[k], out_d[k], out_m[k], out_v[k] = sg[k], a, b, c2

    total = lax.psum(loss[0, 0], ("x", "y", "c"))
    return (total, dx, *[out_g[k] for k in WEIGHTS], *[out_d[k] for k in WEIGHTS],
            *[out_m[k] for k in WEIGHTS], *[out_v[k] for k in WEIGHTS])


def kernel(x, ffn1_norm, ffn1_w_in, ffn1_w_out, mix_norm, w_in, fox_q_norm, fox_k_norm, fox_f_bias, gdn_conv, gdn_a_log, gdn_dt_bias, gdn_out_norm, w_out, ffn2_norm, ffn2_w_in, ffn2_w_out, loss_target, m_ffn1_norm, m_ffn1_w_in, m_ffn1_w_out, m_mix_norm, m_w_in, m_fox_q_norm, m_fox_k_norm, m_fox_f_bias, m_gdn_conv, m_gdn_a_log, m_gdn_dt_bias, m_gdn_out_norm, m_w_out, m_ffn2_norm, m_ffn2_w_in, m_ffn2_w_out, v_ffn1_norm, v_ffn1_w_in, v_ffn1_w_out, v_mix_norm, v_w_in, v_fox_q_norm, v_fox_k_norm, v_fox_f_bias, v_gdn_conv, v_gdn_a_log, v_gdn_dt_bias, v_gdn_out_norm, v_w_out, v_ffn2_norm, v_ffn2_w_in, v_ffn2_w_out):
    w = dict(ffn1_norm=ffn1_norm, ffn1_w_in=ffn1_w_in, ffn1_w_out=ffn1_w_out, mix_norm=mix_norm, w_in=w_in,
             fox_q_norm=fox_q_norm, fox_k_norm=fox_k_norm, fox_f_bias=fox_f_bias, gdn_conv=gdn_conv,
             gdn_a_log=gdn_a_log, gdn_dt_bias=gdn_dt_bias, gdn_out_norm=gdn_out_norm, w_out=w_out,
             ffn2_norm=ffn2_norm, ffn2_w_in=ffn2_w_in, ffn2_w_out=ffn2_w_out)
    m = dict(ffn1_norm=m_ffn1_norm, ffn1_w_in=m_ffn1_w_in, ffn1_w_out=m_ffn1_w_out, mix_norm=m_mix_norm, w_in=m_w_in,
             fox_q_norm=m_fox_q_norm, fox_k_norm=m_fox_k_norm, fox_f_bias=m_fox_f_bias, gdn_conv=m_gdn_conv,
             gdn_a_log=m_gdn_a_log, gdn_dt_bias=m_gdn_dt_bias, gdn_out_norm=m_gdn_out_norm, w_out=m_w_out,
             ffn2_norm=m_ffn2_norm, ffn2_w_in=m_ffn2_w_in, ffn2_w_out=m_ffn2_w_out)
    v = dict(ffn1_norm=v_ffn1_norm, ffn1_w_in=v_ffn1_w_in, ffn1_w_out=v_ffn1_w_out, mix_norm=v_mix_norm, w_in=v_w_in,
             fox_q_norm=v_fox_q_norm, fox_k_norm=v_fox_k_norm, fox_f_bias=v_fox_f_bias, gdn_conv=v_gdn_conv,
             gdn_a_log=v_gdn_a_log, gdn_dt_bias=v_gdn_dt_bias, gdn_out_norm=v_gdn_out_norm, w_out=v_w_out,
             ffn2_norm=v_ffn2_norm, ffn2_w_in=v_ffn2_w_in, ffn2_w_out=v_ffn2_w_out)
    return _step(x, loss_target, w, m, v)
```

```python
import jax
import jax.numpy as jnp
from jax import lax
from jax.experimental import pallas as pl
from jax.experimental.pallas import tpu as pltpu

F32 = jnp.float32
BF = jnp.bfloat16
HI = lax.Precision.HIGHEST
MESH = pl.DeviceIdType.MESH

DEPTH = 2
FOX_HEADS = 8
FOX_HEAD_DIM = 64
FOX_WIDTH = 512
GDN_HEADS = 4
GDN_HEAD_DIM = 128
GDN_WIDTH = 512
CONV_WIDTH = 4
CHUNK = 64
EPS = 1e-6
N_IN = 3600
N_PAD = 3712
GATE_COL = 3584
LANES = 128
NEG = -1e30

ADAM_LR = 0.001
ADAM_B1 = 0.9
ADAM_B2 = 0.999
ADAM_EPS = 1e-08
ADAM_WD = 0.01
ADAM_STEP = 10

VMEM_LIMIT = 56 * 1024 * 1024


def _params(sem=None, **kw):
    return pltpu.CompilerParams(dimension_semantics=sem, vmem_limit_bytes=VMEM_LIMIT, **kw)


def _dot(a, b, precision=None):
    return jnp.dot(a, b, preferred_element_type=F32, precision=precision)


def _dot_nt(a, b, precision=None):
    return lax.dot_general(a, b, (((1,), (1,)), ((), ())), preferred_element_type=F32, precision=precision)


def _dot_tn(a, b, precision=None):
    return lax.dot_general(a, b, (((0,), (0,)), ((), ())), preferred_element_type=F32, precision=precision)


def _sigmoid(x):
    return 0.5 * jnp.tanh(0.5 * x) + 0.5


def _softplus(x):
    return jnp.maximum(x, 0.0) + jnp.log(1.0 + jnp.exp(-jnp.abs(x)))


def _log_sigmoid(x):
    return jnp.minimum(x, 0.0) - jnp.log(1.0 + jnp.exp(-jnp.abs(x)))


def _tile(n, t):
    t = min(n, t)
    assert n % t == 0, (n, t)
    return t


def _rms_fwd(x, gain):
    rstd = lax.rsqrt(jnp.mean(x * x, axis=-1, keepdims=True) + EPS)
    xhat = x * rstd
    return xhat * gain, xhat, rstd


def _rms_bwd(dy, xhat, rstd, gain):
    dxhat = dy * gain
    dx = rstd * (dxhat - xhat * jnp.mean(dxhat * xhat, axis=-1, keepdims=True))
    return dx, dy * xhat


def _full(shape):
    nd = len(shape)
    return pl.BlockSpec(shape, lambda *_: (0,) * nd)


HBM = pl.BlockSpec(memory_space=pltpu.HBM)


def _load_ffn_weights(win_hbm, wout_hbm, win_v, wout_v, sem):
    fr = wout_hbm.shape[1]
    copies = [pltpu.make_async_copy(win_hbm.at[s], win_v.at[s], sem.at[s]) for s in range(4)]
    copies += [pltpu.make_async_copy(wout_hbm.at[s], wout_v.at[pl.ds(s * fr, fr)], sem.at[4 + s])
               for s in range(4)]
    for c in copies:
        c.start()
    for c in copies:
        c.wait()


def _ffn_fwd(x, gain, win_g, wout_g, name, rider=None):
    t, d = x.shape
    _, _, fs = win_g.shape
    fr = wout_g.shape[1]
    tm = _tile(t, 512)
    r_in, r_out, r_sem = _rider_parts(rider)
    steps = t // tm

    def body(x_ref, g_ref, win_hbm, wout_hbm, *rest):
        rin, (xo_ref, h_ref) = rest[:len(r_in)], rest[len(r_in):len(r_in) + 2]
        rout = rest[len(r_in) + 2:len(r_in) + 2 + len(r_out)]
        win_v, wout_v, sem = rest[len(r_in) + 2 + len(r_out):len(r_in) + 5 + len(r_out)]
        riding = (rin, rout, rest[len(r_in) + 5 + len(r_out):])
        step = pl.program_id(0)
        _ride(rider, 0, step == 0, riding)
        _ride(rider, 1, step == (3 * steps) // 4, riding)

        @pl.when(step == 0)
        def _():
            _load_ffn_weights(win_hbm, wout_hbm, win_v, wout_v, sem)

        xv = x_ref[...]
        hn, _, _ = _rms_fwd(xv, g_ref[...])
        hn = hn.astype(BF)
        acc = jnp.zeros((tm, d), F32)
        for s in range(2):
            g = _dot(hn, win_v[s])
            u = _dot(hn, win_v[s + 2])
            h_ref[:, s * fs:(s + 1) * fs] = g.astype(BF)
            h_ref[:, (s + 2) * fs:(s + 3) * fs] = u.astype(BF)
            a = (g * _sigmoid(g) * u).astype(BF)
            acc = acc + _dot(a, wout_v[s * fs:(s + 1) * fs, :])
        xo_ref[...] = xv + 0.5 * acc
        _ride(rider, 2, step == steps - 1, riding)

    return pl.pallas_call(
        body, name=name, grid=(steps,),
        in_specs=[pl.BlockSpec((tm, d), lambda i: (i, 0)), _full((1, d)), HBM, HBM] + [HBM] * len(r_in),
        out_specs=[pl.BlockSpec((tm, d), lambda i: (i, 0)), pl.BlockSpec((tm, 4 * fs), lambda i: (i, 0))]
        + [HBM] * len(r_out),
        out_shape=[jax.ShapeDtypeStruct((t, d), F32), jax.ShapeDtypeStruct((t, 4 * fs), BF)] + r_out,
        scratch_shapes=[pltpu.VMEM((4, d, fs), BF), pltpu.VMEM((4 * fr, d), BF), pltpu.SemaphoreType.DMA((8,))]
        + r_sem,
        compiler_params=_params(("arbitrary",), has_side_effects=rider is not None),
    )(x, gain, win_g, wout_g, *r_in)


def _ffn_bwd(dy, x, h, gain, win_g, wout_g, name, rider=None):
    t, d = x.shape
    _, _, fs = win_g.shape
    fr = wout_g.shape[1]
    tm = _tile(t, 256)
    r_in, r_out, r_sem = _rider_parts(rider)
    steps = t // tm

    def body(dy_ref, x_ref, h_ref, g_ref, win_hbm, wout_hbm, *rest):
        rin, (dx_ref, dh_ref, a_ref, hn_ref, dyh_ref, dg_ref) = rest[:len(r_in)], rest[len(r_in):len(r_in) + 6]
        rout = rest[len(r_in) + 6:len(r_in) + 6 + len(r_out)]
        win_v, wout_v, sem = rest[len(r_in) + 6 + len(r_out):len(r_in) + 9 + len(r_out)]
        riding = (rin, rout, rest[len(r_in) + 9 + len(r_out):])
        step = pl.program_id(0)
        _ride(rider, 0, step == 0, riding)
        _ride(rider, 1, step == (3 * steps) // 4, riding)

        @pl.when(step == 0)
        def _():
            _load_ffn_weights(win_hbm, wout_hbm, win_v, wout_v, sem)
            dg_ref[...] = jnp.zeros_like(dg_ref)

        dyv = dy_ref[...]
        dyh = (0.5 * dyv).astype(BF)
        dyh_ref[...] = dyh
        dhn = jnp.zeros((tm, d), F32)
        for s in range(2):
            da = _dot_nt(dyh, wout_v[s * fs:(s + 1) * fs, :])
            g = h_ref[:, s * fs:(s + 1) * fs].astype(F32)
            u = h_ref[:, (s + 2) * fs:(s + 3) * fs].astype(F32)
            sg = _sigmoid(g)
            si = g * sg
            a_ref[:, s * fs:(s + 1) * fs] = (si * u).astype(BF)
            dgate = (da * u * (sg * (1.0 + g * (1.0 - sg)))).astype(BF)
            dup = (da * si).astype(BF)
            dh_ref[:, s * fs:(s + 1) * fs] = dgate
            dh_ref[:, (s + 2) * fs:(s + 3) * fs] = dup
            dhn = dhn + _dot_nt(dgate, win_v[s]) + _dot_nt(dup, win_v[s + 2])
        xv = x_ref[...]
        gain_v = g_ref[...]
        hn, xhat, rstd = _rms_fwd(xv, gain_v)
        hn_ref[...] = hn.astype(BF)
        dx, dgr = _rms_bwd(dhn, xhat, rstd, gain_v)
        dx_ref[...] = dyv + dx
        dg_ref[...] += jnp.sum(dgr, axis=0, keepdims=True)
        _ride(rider, 2, step == steps - 1, riding)

    row = lambda w: pl.BlockSpec((tm, w), lambda i: (i, 0))
    return pl.pallas_call(
        body, name=name, grid=(steps,),
        in_specs=[row(d), row(d), row(4 * fs), _full((1, d)), HBM, HBM] + [HBM] * len(r_in),
        out_specs=[row(d), row(4 * fs), row(2 * fs), row(d), row(d), _full((1, d))] + [HBM] * len(r_out),
        out_shape=[jax.ShapeDtypeStruct((t, d), F32), jax.ShapeDtypeStruct((t, 4 * fs), BF),
                   jax.ShapeDtypeStruct((t, 2 * fs), BF), jax.ShapeDtypeStruct((t, d), BF),
                   jax.ShapeDtypeStruct((t, d), BF), jax.ShapeDtypeStruct((1, d), F32)] + r_out,
        scratch_shapes=[pltpu.VMEM((4, d, fs), BF), pltpu.VMEM((4 * fr, d), BF), pltpu.SemaphoreType.DMA((8,))]
        + r_sem,
        compiler_params=_params(("arbitrary",), has_side_effects=rider is not None),
    )(dy, x, h, gain, win_g, wout_g, *r_in)


def _wgrad(a, b, out_shape, out_spec, tm, tn, name, tk=512):
    t, m = a.shape
    _, n = b.shape
    tk = _tile(t, tk)
    nk = t // tk

    def body(a_ref, b_ref, o_ref, acc):
        k = pl.program_id(2)

        @pl.when(k == 0)
        def _():
            acc[...] = jnp.zeros_like(acc)

        acc[...] += _dot_tn(a_ref[...], b_ref[...])

        @pl.when(k == nk - 1)
        def _():
            o_ref[...] = acc[...].astype(o_ref.dtype)

    return pl.pallas_call(
        body, name=name, grid=(m // tm, n // tn, nk),
        in_specs=[pl.BlockSpec((tk, tm), lambda i, j, k: (k, i)), pl.BlockSpec((tk, tn), lambda i, j, k: (k, j))],
        out_specs=out_spec, out_shape=out_shape,
        scratch_shapes=[pltpu.VMEM((tm, tn), F32)],
        compiler_params=_params(("parallel", "parallel", "arbitrary")),
    )(a, b)


def _wgrad_parts(a, parts, tm, name, tk=512):
    t, m = a.shape
    widths = [p.shape[1] for p in parts]
    n = sum(widths)
    tk = _tile(t, tk)
    nk = t // tk
    np_ = len(parts)

    def body(a_ref, *rest):
        b_refs, o_ref, acc = rest[:np_], rest[np_], rest[np_ + 1]
        k = pl.program_id(1)

        @pl.when(k == 0)
        def _():
            acc[...] = jnp.zeros_like(acc)

        av, off = a_ref[...], 0
        for b_ref, wd in zip(b_refs, widths):
            acc[:, off:off + wd] += _dot_tn(av, b_ref[...])
            off += wd

        @pl.when(k == nk - 1)
        def _():
            o_ref[...] = acc[...]

    return pl.pallas_call(
        body, name=name, grid=(m // tm, nk),
        in_specs=[pl.BlockSpec((tk, tm), lambda i, k: (k, i))]
        + [pl.BlockSpec((tk, wd), lambda i, k: (k, 0)) for wd in widths],
        out_specs=pl.BlockSpec((tm, n), lambda i, k: (i, 0)), out_shape=jax.ShapeDtypeStruct((m, n), F32),
        scratch_shapes=[pltpu.VMEM((tm, n), F32)],
        compiler_params=_params(("parallel", "arbitrary")),
    )(a, *parts)


def _norm_matmul(x, gain, w, name):
    t, d = x.shape
    n = w.shape[1]
    tm = _tile(t, 512)

    def body(x_ref, g_ref, w_ref, o_ref):
        hn, _, _ = _rms_fwd(x_ref[...], g_ref[...])
        o_ref[...] = _dot(hn.astype(BF), w_ref[...])

    return pl.pallas_call(
        body, name=name, grid=(t // tm,),
        in_specs=[pl.BlockSpec((tm, d), lambda i: (i, 0)), _full((1, d)), _full((d, n))],
        out_specs=pl.BlockSpec((tm, n), lambda i: (i, 0)),
        out_shape=jax.ShapeDtypeStruct((t, n), F32),
        compiler_params=_params(("parallel",)),
    )(x, gain, w)


def _norm_matmul_bwd(dres, dparts, x, gain, w, name):
    t, d = x.shape
    n = w.shape[1]
    tm = _tile(t, 256)
    widths = [a.shape[1] for a in dparts]
    assert sum(widths) == n
    k = len(dparts)

    def body(dr_ref, *rest):
        dp_refs, (x_ref, g_ref, w_ref, dx_ref, hn_ref, dg_ref) = rest[:k], rest[k:]

        @pl.when(pl.program_id(0) == 0)
        def _():
            dg_ref[...] = jnp.zeros_like(dg_ref)

        dhn, off = jnp.zeros((tm, d), F32), 0
        for dp_ref, wd in zip(dp_refs, widths):
            dhn = dhn + _dot_nt(dp_ref[...], w_ref[:, off:off + wd])
            off += wd
        gain_v = g_ref[...]
        hn, xhat, rstd = _rms_fwd(x_ref[...], gain_v)
        hn_ref[...] = hn.astype(BF)
        dx, dgr = _rms_bwd(dhn, xhat, rstd, gain_v)
        dx_ref[...] = dr_ref[...] + dx
        dg_ref[...] += jnp.sum(dgr, axis=0, keepdims=True)

    row = lambda wd: pl.BlockSpec((tm, wd), lambda i: (i, 0))
    return pl.pallas_call(
        body, name=name, grid=(t // tm,),
        in_specs=[row(d)] + [row(wd) for wd in widths] + [row(d), _full((1, d)), _full((d, n))],
        out_specs=[row(d), row(d), _full((1, d))],
        out_shape=[jax.ShapeDtypeStruct((t, d), F32), jax.ShapeDtypeStruct((t, d), BF),
                   jax.ShapeDtypeStruct((1, d), F32)],
        compiler_params=_params(("arbitrary",)),
    )(dres, *dparts, x, gain, w)


def _mix_out(x, yf, yg, w, name):
    t, d = x.shape
    kf = yf.shape[1]
    tm = _tile(t, 512)

    def body(x_ref, yf_ref, yg_ref, w_ref, o_ref):
        o_ref[...] = x_ref[...] + _dot(yf_ref[...], w_ref[0:kf, :]) + _dot(yg_ref[...], w_ref[kf:2 * kf, :])

    row = lambda wd: pl.BlockSpec((tm, wd), lambda i: (i, 0))
    return pl.pallas_call(
        body, name=name, grid=(t // tm,),
        in_specs=[row(d), row(kf), row(kf), _full((2 * kf, d))],
        out_specs=row(d), out_shape=jax.ShapeDtypeStruct((t, d), F32),
        compiler_params=_params(("parallel",)),
    )(x, yf, yg, w)


def _mix_out_bwd(dx, w, name):
    t, d = dx.shape
    kf = w.shape[0] // 2
    tm = _tile(t, 512)

    def body(dx_ref, w_ref, df_ref, dg_ref, dxb_ref):
        dxb = dx_ref[...].astype(BF)
        dxb_ref[...] = dxb
        df_ref[...] = _dot_nt(dxb, w_ref[0:kf, :]).astype(BF)
        dg_ref[...] = _dot_nt(dxb, w_ref[kf:2 * kf, :]).astype(BF)

    row = lambda wd: pl.BlockSpec((tm, wd), lambda i: (i, 0))
    return pl.pallas_call(
        body, name=name, grid=(t // tm,),
        in_specs=[row(d), _full((2 * kf, d))],
        out_specs=[row(kf), row(kf), row(d)],
        out_shape=[jax.ShapeDtypeStruct((t, kf), BF), jax.ShapeDtypeStruct((t, kf), BF),
                   jax.ShapeDtypeStruct((t, d), BF)],
        compiler_params=_params(("parallel",)),
    )(dx, w)


def _loss_grad(y, target, name):
    t, d = y.shape
    tm = _tile(t, 512)

    def body(y_ref, t_ref, l_ref, dy_ref):
        @pl.when(pl.program_id(0) == 0)
        def _():
            l_ref[...] = jnp.zeros_like(l_ref)

        diff = y_ref[...] - t_ref[...]
        dy_ref[...] = diff * (1.0 / d)
        part = jnp.sum(jnp.sum(diff * diff, axis=1, keepdims=True), axis=0, keepdims=True)
        l_ref[...] += part * (0.5 / d)

    row = pl.BlockSpec((tm, d), lambda i: (i, 0))
    return pl.pallas_call(
        body, name=name, grid=(t // tm,),
        in_specs=[row, row], out_specs=[_full((1, 1)), row],
        out_shape=[jax.ShapeDtypeStruct((1, 1), F32), jax.ShapeDtypeStruct((t, d), F32)],
        compiler_params=_params(("arbitrary",)),
    )(y, target)


def _head_sum_matrix(width, head):
    r = lax.broadcasted_iota(jnp.int32, (width, width), 0) // head
    c = lax.broadcasted_iota(jnp.int32, (width, width), 1) // head
    return (r == c).astype(BF)


def _head_mean(x, bd):
    return _dot(x.astype(BF), bd) * (1.0 / FOX_HEAD_DIM)


def _mask_dot(mask01, x):
    mb = mask01.astype(BF)
    hi = x.astype(BF)
    r1 = x - hi.astype(F32)
    mid = r1.astype(BF)
    lo = (r1 - mid.astype(F32)).astype(BF)
    return _dot(mb, hi) + _dot(mb, mid) + _dot(mb, lo)


def _fox_prep(proj, wq_t, wk_t, bias_pad, seq, name):
    t = proj.shape[0]
    ts = _tile(seq, 512)
    tpe = seq // ts
    scale = FOX_HEAD_DIM ** -0.5

    def body(q_ref, k_ref, v_ref, gt_ref, wq_ref, wk_ref, b_ref, qo_ref, ko_ref, vo_ref, cum_ref, carry):
        i = pl.program_id(0)
        bd = _head_sum_matrix(FOX_WIDTH, FOX_HEAD_DIM)

        def norm(xv, wv):
            ms = _head_mean(xv * xv, bd)
            return xv * lax.rsqrt(ms + EPS) * wv

        qo_ref[...] = (norm(q_ref[...], wq_ref[...]) * scale).astype(BF)
        ko_ref[...] = norm(k_ref[...], wk_ref[...]).astype(BF)
        vo_ref[...] = v_ref[...].astype(BF)

        @pl.when(i % tpe == 0)
        def _():
            carry[...] = jnp.zeros_like(carry)

        ls = _log_sigmoid(gt_ref[...] + b_ref[...])
        r = lax.broadcasted_iota(jnp.int32, (ts, ts), 0)
        c = lax.broadcasted_iota(jnp.int32, (ts, ts), 1)
        cum = _mask_dot(r >= c, ls) + carry[...]
        cum_ref[...] = cum
        carry[...] = cum[ts - 1:ts, :]

    blk = lambda j: pl.BlockSpec((ts, FOX_WIDTH), lambda i: (i, j))
    gate = pl.BlockSpec((ts, LANES), lambda i: (i, GATE_COL // LANES))
    out = pl.BlockSpec((ts, FOX_WIDTH), lambda i: (i, 0))
    return pl.pallas_call(
        body, name=name, grid=(t // ts,),
        in_specs=[blk(0), blk(1), blk(2), gate, _full((1, FOX_WIDTH)), _full((1, FOX_WIDTH)), _full((1, LANES))],
        out_specs=[out, out, out, pl.BlockSpec((ts, LANES), lambda i: (i, 0))],
        out_shape=[jax.ShapeDtypeStruct((t, FOX_WIDTH), BF)] * 3 + [jax.ShapeDtypeStruct((t, LANES), F32)],
        scratch_shapes=[pltpu.VMEM((1, LANES), F32)],
        compiler_params=_params(("arbitrary",)),
    )(proj, proj, proj, proj, wq_t, wk_t, bias_pad)


def _pick_lanes(x, lane_in_block, first_out_lane):
    r = lax.broadcasted_iota(jnp.int32, (FOX_WIDTH, LANES), 0)
    c = lax.broadcasted_iota(jnp.int32, (FOX_WIDTH, LANES), 1)
    sel = ((r % LANES == lane_in_block) & (c == first_out_lane + 2 * (r // LANES))).astype(BF)
    hi = x.astype(BF)
    r1 = x - hi.astype(F32)
    mid = r1.astype(BF)
    lo = (r1 - mid.astype(F32)).astype(BF)
    return _dot(hi, sel) + _dot(mid, sel) + _dot(lo, sel)


def _fox_prep_bwd(proj, dqa, dqb, dk, dv, dkx, wq_t, wk_t, bias_pad, seq, name):
    t = proj.shape[0]
    ts = _tile(seq, 512)
    tpe = seq // ts
    nt = t // ts
    scale = FOX_HEAD_DIM ** -0.5

    def body(q_ref, k_ref, gt_ref, dqa_ref, dqb_ref, dk_ref, dv_ref, dc_ref, wq_ref, wk_ref, b_ref,
             dp_ref, dff_ref, dwq_ref, dwk_ref, db_ref, carry):
        i = pl.program_id(0)
        first = (lax.broadcasted_iota(jnp.int32, (ts, FOX_WIDTH), 1) % LANES) < FOX_HEAD_DIM
        dq_all = jnp.where(first, dqa_ref[...], dqb_ref[...])
        ti = nt - 1 - i
        bd = _head_sum_matrix(FOX_WIDTH, FOX_HEAD_DIM)

        @pl.when(i == 0)
        def _():
            dwq_ref[...] = jnp.zeros_like(dwq_ref)
            dwk_ref[...] = jnp.zeros_like(dwk_ref)
            db_ref[...] = jnp.zeros_like(db_ref)

        def norm_bwd(xv, wv, dyv):
            ms = _head_mean(xv * xv, bd)
            rstd = lax.rsqrt(ms + EPS)
            xhat = xv * rstd
            dxhat = dyv * wv
            mean = _head_mean(dxhat * xhat, bd)
            return rstd * (dxhat - xhat * mean), jnp.sum(dyv * xhat, axis=0, keepdims=True)

        dxq, dwq = norm_bwd(q_ref[...], wq_ref[...], dq_all * scale)
        dxk, dwk = norm_bwd(k_ref[...], wk_ref[...], dk_ref[...])
        dp_ref[:, 0:FOX_WIDTH] = dxq.astype(BF)
        dp_ref[:, FOX_WIDTH:2 * FOX_WIDTH] = dxk.astype(BF)
        dp_ref[:, 2 * FOX_WIDTH:3 * FOX_WIDTH] = dv_ref[...].astype(BF)
        dwq_ref[...] += dwq
        dwk_ref[...] += dwk

        @pl.when(ti % tpe == tpe - 1)
        def _():
            carry[...] = jnp.zeros_like(carry)

        r = lax.broadcasted_iota(jnp.int32, (ts, ts), 0)
        c = lax.broadcasted_iota(jnp.int32, (ts, ts), 1)
        dkx = dc_ref[...]
        hd = FOX_HEAD_DIM
        dcum = (_pick_lanes(dqa_ref[...], hd, 0) + _pick_lanes(dqb_ref[...], 0, 1)
                - _pick_lanes(dkx, hd, 0) - _pick_lanes(dkx, 0, 1))
        dls = _mask_dot(c >= r, dcum) + carry[...]
        carry[...] = dls[0:1, :]
        z = gt_ref[...] + b_ref[...]
        lane = lax.broadcasted_iota(jnp.int32, (ts, LANES), 1)
        dff = jnp.where(lane < FOX_HEADS, dls * _sigmoid(-z), 0.0)
        dff_ref[...] = dff
        db_ref[...] += jnp.sum(dff, axis=0, keepdims=True)

        @pl.when(i == nt - 1)
        def _():
            fr = lax.broadcasted_iota(jnp.int32, (FOX_WIDTH, FOX_WIDTH), 0) % FOX_HEAD_DIM
            fc = lax.broadcasted_iota(jnp.int32, (FOX_WIDTH, FOX_WIDTH), 1) % FOX_HEAD_DIM
            fold = (fr == fc).astype(F32)
            dwq_ref[...] = _dot(dwq_ref[...], fold, HI)
            dwk_ref[...] = _dot(dwk_ref[...], fold, HI)

    rev = lambda w, j: pl.BlockSpec((ts, w), lambda i: (nt - 1 - i, j))
    return pl.pallas_call(
        body, name=name, grid=(nt,),
        in_specs=[rev(FOX_WIDTH, 0), rev(FOX_WIDTH, 1), rev(LANES, GATE_COL // LANES),
                  rev(FOX_WIDTH, 0), rev(FOX_WIDTH, 0), rev(FOX_WIDTH, 0), rev(FOX_WIDTH, 0), rev(FOX_WIDTH, 0),
                  _full((1, FOX_WIDTH)), _full((1, FOX_WIDTH)), _full((1, LANES))],
        out_specs=[rev(3 * FOX_WIDTH, 0), rev(LANES, 0), _full((1, FOX_WIDTH)), _full((1, FOX_WIDTH)),
                   _full((1, LANES))],
        out_shape=[jax.ShapeDtypeStruct((t, 3 * FOX_WIDTH), BF), jax.ShapeDtypeStruct((t, LANES), F32),
                   jax.ShapeDtypeStruct((1, FOX_WIDTH), F32), jax.ShapeDtypeStruct((1, FOX_WIDTH), F32),
                   jax.ShapeDtypeStruct((1, LANES), F32)],
        scratch_shapes=[pltpu.VMEM((1, LANES), F32)],
        compiler_params=_params(("arbitrary",)),
    )(proj, proj, proj, dqa, dqb, dk, dv, dkx, wq_t, wk_t, bias_pad)


class _Rider:
    def __init__(self, inputs, out_shapes, sems, phases):
        self.inputs, self.out_shapes, self.sems, self.phases = list(inputs), list(out_shapes), list(sems), phases


def _rider_parts(rider):
    if rider is None:
        return [], [], []
    return rider.inputs, rider.out_shapes, rider.sems


def _ride(rider, which, when, refs):
    if rider is not None:
        @pl.when(when)
        def _():
            rider.phases[which](*refs)


def _fox_attn(q, k, v, ck, nb, seq, name, rider=None):
    t = q.shape[0]
    tq = _tile(seq, 2048)
    nq = seq // tq
    npair = FOX_HEADS // 2
    hd = FOX_HEAD_DIM
    r_in, r_out, r_sem = _rider_parts(rider)
    steps = nb * npair * nq

    def body(q_ref, k_ref, v_ref, ck_ref, *rest):
        rin, (o_ref, lse_ref) = rest[:len(r_in)], rest[len(r_in):len(r_in) + 2]
        rout = rest[len(r_in) + 2:len(r_in) + 2 + len(r_out)]
        m_s, acc_s = rest[len(r_in) + 2 + len(r_out):len(r_in) + 4 + len(r_out)]
        riding = (rin, rout, rest[len(r_in) + 4 + len(r_out):])
        step = (pl.program_id(0) * npair + pl.program_id(1)) * nq + pl.program_id(2)
        _ride(rider, 0, step == 0, riding)
        _ride(rider, 1, step == (3 * steps) // 4, riding)
        qi = pl.program_id(2)
        lane = lax.broadcasted_iota(jnp.int32, (tq, LANES), 1)
        m_s[...] = jnp.full(m_s.shape, NEG, F32)
        acc_s[...] = jnp.zeros_like(acc_s)
        qv = q_ref[...]

        def tile(kj, on_diagonal):
            cols = pl.ds(pl.multiple_of(kj * tq, tq), tq)
            kv = k_ref[cols, :]
            vv = v_ref[cols, :]
            if on_diagonal:
                causal = (lax.broadcasted_iota(jnp.int32, (tq, tq), 0)
                          >= lax.broadcasted_iota(jnp.int32, (tq, tq), 1))
            ck = [ck_ref[hh, :, cols] for hh in range(2)]
            m_old = [m_s[hh] for hh in range(2)]
            acc_old = [acc_s[hh] for hh in range(2)]
            m_out, acc_out = [], []
            for hh in range(2):
                hm = (lane >= hd) if hh else (lane < hd)
                qh = jnp.where(hm, qv, jnp.zeros_like(qv))
                s = _dot_nt(qh, kv) - ck[hh]
                if on_diagonal:
                    s = jnp.where(causal, s, NEG)
                m_new = jnp.maximum(m_old[hh], jnp.max(s, axis=-1, keepdims=True))
                p = jnp.exp(s - m_new)
                alpha = jnp.exp(m_old[hh] - m_new)
                m_out.append(m_new)
                acc_out.append(alpha * acc_old[hh] + _dot(p.astype(BF), jnp.where(hm, vv, jnp.ones_like(vv))))
            for hh in range(2):
                m_s[hh] = m_out[hh]
                acc_s[hh] = acc_out[hh]

        def off_diagonal(kj, carry):
            tile(kj, False)
            return carry

        lax.fori_loop(0, qi, off_diagonal, 0)
        tile(qi, True)
        a0 = acc_s[0]
        a1 = acc_s[1]
        den = jnp.where(lane < hd, pltpu.roll(a0, hd, axis=1), pltpu.roll(a1, hd, axis=1))
        o_ref[...] = (jnp.where(lane < hd, a0, a1) / den).astype(o_ref.dtype)
        l0 = jnp.sum(jnp.where(lane == hd, a0, 0.0), axis=1, keepdims=True)
        l1 = jnp.sum(jnp.where(lane == 0, a1, 0.0), axis=1, keepdims=True)
        lse_ref[0] = m_s[0] + jnp.log(l0)
        lse_ref[1] = m_s[1] + jnp.log(l1)
        _ride(rider, 2, step == steps - 1, riding)

    qspec = pl.BlockSpec((tq, LANES), lambda b, p, i: (b * nq + i, p))
    kspec = pl.BlockSpec((seq, LANES), lambda b, p, i: (b, p))
    colspec = pl.BlockSpec((None, 2, tq, 1), lambda b, p, i: (b * npair + p, 0, i, 0))
    rowspec = pl.BlockSpec((None, 2, 1, seq), lambda b, p, i: (b * npair + p, 0, 0, 0))
    sem = ("arbitrary",) * 3 if rider else ("parallel",) * 3
    return pl.pallas_call(
        body, name=name, grid=(nb, npair, nq),
        in_specs=[qspec, kspec, kspec, rowspec] + [HBM] * len(r_in),
        out_specs=[qspec, colspec] + [HBM] * len(r_out),
        out_shape=[jax.ShapeDtypeStruct((t, FOX_WIDTH), BF), jax.ShapeDtypeStruct((nb * npair, 2, seq, 1), F32)]
        + r_out,
        scratch_shapes=[pltpu.VMEM((2, tq, 1), F32), pltpu.VMEM((2, tq, LANES), F32)] + r_sem,
        compiler_params=_params(sem, has_side_effects=rider is not None),
    )(q, k, v, ck, *r_in)


def _fox_attn_bwd(q, k, v, o, do, lse, ck, nb, seq, name, rider=None):
    t = q.shape[0]
    tq = _tile(seq, 1024)
    nq = seq // tq
    npair = FOX_HEADS // 2
    hd = FOX_HEAD_DIM
    r_in, r_out, r_sem = _rider_parts(rider)
    steps = nb * npair * nq

    def body(q_ref, k_ref, v_ref, o_ref, do_ref, lse_ref, ck_ref, *rest):
        rin, (dqa_ref, dqb_ref, dk_ref, dv_ref, dkx_ref) = rest[:len(r_in)], rest[len(r_in):len(r_in) + 5]
        rout = rest[len(r_in) + 5:len(r_in) + 5 + len(r_out)]
        dk_s, dv_s = rest[len(r_in) + 5 + len(r_out):len(r_in) + 7 + len(r_out)]
        riding = (rin, rout, rest[len(r_in) + 7 + len(r_out):])
        step = (pl.program_id(0) * npair + pl.program_id(1)) * nq + pl.program_id(2)
        _ride(rider, 0, step == 0, riding)
        _ride(rider, 1, step == (3 * steps) // 4, riding)
        kj = pl.program_id(2)
        lane = lax.broadcasted_iota(jnp.int32, (tq, LANES), 1)

        @pl.when(kj == 0)
        def _():
            dqa_ref[...] = jnp.zeros_like(dqa_ref)
            dqb_ref[...] = jnp.zeros_like(dqb_ref)

        dk_s[...] = jnp.zeros_like(dk_s)
        dv_s[...] = jnp.zeros_like(dv_s)
        kv = k_ref[...]
        vv = v_ref[...]

        def tile(qi, on_diagonal):
            rows = pl.ds(pl.multiple_of(qi * tq, tq), tq)
            qv = q_ref[rows, :]
            dov = do_ref[rows, :]
            prod = dov.astype(F32) * o_ref[rows, :].astype(F32)
            if on_diagonal:
                causal = (lax.broadcasted_iota(jnp.int32, (tq, tq), 0)
                          >= lax.broadcasted_iota(jnp.int32, (tq, tq), 1))
            for hh, dq_ref in ((0, dqa_ref), (1, dqb_ref)):
                hm = (lane >= hd) if hh else (lane < hd)
                zero = jnp.zeros_like(qv)
                one = jnp.ones_like(qv)
                doh = jnp.where(hm, dov, zero)
                delta = jnp.sum(jnp.where(hm, prod, 0.0), axis=-1, keepdims=True)
                s = _dot_nt(jnp.where(hm, qv, zero), kv) - ck_ref[hh]
                if on_diagonal:
                    s = jnp.where(causal, s, NEG)
                p = jnp.exp(s - lse_ref[hh, rows, :])
                dp = _dot_nt(doh, vv)
                dsb = (p * (dp - delta)).astype(BF)
                dv_s[...] += _dot_tn(p.astype(BF), doh)
                dk_s[hh] += _dot_tn(dsb, jnp.where(hm, qv, one))
                dq_ref[rows, :] += _dot(dsb, jnp.where(hm, kv, one))

        def off_diagonal(qi, carry):
            tile(qi, False)
            return carry

        tile(kj, True)
        lax.fori_loop(kj + 1, nq, off_diagonal, 0)
        dk_ref[...] = jnp.where(lane < hd, dk_s[0], dk_s[1])
        dkx_ref[...] = jnp.where(lane < hd, dk_s[1], dk_s[0])
        dv_ref[...] = dv_s[...]
        _ride(rider, 2, step == steps - 1, riding)

    kspec = pl.BlockSpec((tq, LANES), lambda b, p, j: (b * nq + j, p))
    full_q = pl.BlockSpec((seq, LANES), lambda b, p, j: (b, p))
    colspec = pl.BlockSpec((None, 2, seq, 1), lambda b, p, j: (b * npair + p, 0, 0, 0))
    rowspec = pl.BlockSpec((None, 2, 1, tq), lambda b, p, j: (b * npair + p, 0, 0, j))
    sem = ("arbitrary",) * 3 if rider else ("parallel", "parallel", "arbitrary")
    return pl.pallas_call(
        body, name=name, grid=(nb, npair, nq),
        in_specs=[full_q, kspec, kspec, full_q, full_q, colspec, rowspec] + [HBM] * len(r_in),
        out_specs=[full_q, full_q, kspec, kspec, kspec] + [HBM] * len(r_out),
        out_shape=[jax.ShapeDtypeStruct((t, FOX_WIDTH), F32)] * 5 + r_out,
        scratch_shapes=[pltpu.VMEM((2, tq, LANES), F32), pltpu.VMEM((tq, LANES), F32)] + r_sem,
        compiler_params=_params(sem, has_side_effects=rider is not None),
    )(q, k, v, o, do, lse, ck, *r_in)


GDN_QKV = 3 * GDN_WIDTH
GDN_COL = 3 * FOX_WIDTH
GG_COL = GDN_COL + GDN_QKV
A_LANE = FOX_HEADS
B_LANE = FOX_HEADS + GDN_HEADS
HALO = 8


def _gate_lanes(ts):
    lane = lax.broadcasted_iota(jnp.int32, (ts, LANES), 1)
    return (lane >= A_LANE) & (lane < B_LANE), (lane >= B_LANE) & (lane < B_LANE + GDN_HEADS)


def _chunk_tri(ts, upper):
    r = lax.broadcasted_iota(jnp.int32, (ts, ts), 0)
    c = lax.broadcasted_iota(jnp.int32, (ts, ts), 1)
    same = (r // CHUNK) == (c // CHUNK)
    return (same & ((c >= r) if upper else (r >= c))).astype(F32)


def _shift_rows(x, edge, k, down):
    ts = x.shape[0]
    row = lax.broadcasted_iota(jnp.int32, (HALO, x.shape[1]), 0)
    if down:
        rolled = pltpu.roll(x, k, axis=0)
        patch = jnp.where(row < k, pltpu.roll(edge, k, axis=0), rolled[:HALO])
        return jnp.concatenate([patch, rolled[HALO:]], axis=0)
    rolled = pltpu.roll(x, ts - k, axis=0)
    patch = jnp.where(row >= HALO - k, pltpu.roll(edge, HALO - k, axis=0), rolled[ts - HALO:])
    return jnp.concatenate([rolled[:ts - HALO], patch], axis=0)


def _conv_silu(x, before, w):
    taps = [_shift_rows(x, before, CONV_WIDTH - 1 - kk, True) for kk in range(CONV_WIDTH - 1)] + [x]
    c = w[0:1, :] * taps[0]
    for kk in range(1, CONV_WIDTH):
        c = c + w[kk:kk + 1, :] * taps[kk]
    return taps, c, c * _sigmoid(c)


def _gdn_prep(proj, conv_w, a_pad, dt_pad, seq, name):
    t = proj.shape[0]
    ts = _tile(seq, 256)
    tpe = seq // ts
    qscale = GDN_HEAD_DIM ** -0.5

    def body(x_ref, gt_ref, w_ref, a_ref, dt_ref, qo_ref, ko_ref, vo_ref, go_ref, tail):
        i = pl.program_id(0)
        xv = x_ref[...]
        before = jnp.where(i % tpe == 0, jnp.zeros((HALO, GDN_QKV), F32), tail[...])
        tail[...] = xv[ts - HALO:]
        _, _, s = _conv_silu(xv, before, w_ref[...])
        for h in range(GDN_HEADS):
            for base, ref, sc in ((0, qo_ref, qscale), (GDN_WIDTH, ko_ref, 1.0)):
                xh = s[:, base + h * LANES: base + (h + 1) * LANES]
                r = lax.rsqrt(jnp.sum(xh * xh, axis=-1, keepdims=True) + EPS)
                ref[:, h * LANES:(h + 1) * LANES] = (xh * (r * sc)).astype(BF)
        vo_ref[...] = s[:, 2 * GDN_WIDTH:].astype(BF)
        gate = gt_ref[...]
        g_raw = -jnp.exp(a_ref[...]) * _softplus(gate + dt_ref[...])
        gc = _mask_dot(_chunk_tri(ts, False), g_raw)
        is_a, is_b = _gate_lanes(ts)
        go_ref[...] = jnp.where(is_a, gc, jnp.where(is_b, _sigmoid(gate), 0.0))

    out = pl.BlockSpec((ts, GDN_WIDTH), lambda i: (i, 0))
    lanes = pl.BlockSpec((ts, LANES), lambda i: (i, 0))
    return pl.pallas_call(
        body, name=name, grid=(t // ts,),
        in_specs=[pl.BlockSpec((ts, GDN_QKV), lambda i: (i, GDN_COL // GDN_QKV)),
                  pl.BlockSpec((ts, LANES), lambda i: (i, GATE_COL // LANES)),
                  _full((CONV_WIDTH, GDN_QKV)), _full((1, LANES)), _full((1, LANES))],
        out_specs=[out, out, out, lanes],
        out_shape=[jax.ShapeDtypeStruct((t, GDN_WIDTH), BF)] * 3 + [jax.ShapeDtypeStruct((t, LANES), F32)],
        scratch_shapes=[pltpu.VMEM((HALO, GDN_QKV), F32)],
        compiler_params=_params(("arbitrary",)),
    )(proj, proj, conv_w, a_pad, dt_pad)


def _gdn_prep_bwd(proj, dq, dk, dv, dgates, dff, conv_w, a_pad, dt_pad, seq, name):
    t = proj.shape[0]
    ts = _tile(seq, 256)
    tpe = seq // ts
    nt = t // ts
    qscale = GDN_HEAD_DIM ** -0.5
    hb = ts // HALO

    def body(x_ref, halo_ref, gt_ref, dq_ref, dk_ref, dv_ref, dgt_ref, dff_ref, w_ref, a_ref, dt_ref,
             dx_ref, dgo_ref, dw_ref, da_ref, ddt_ref, dsl, carry):
        i = pl.program_id(0)
        ti = nt - 1 - i

        @pl.when(i == 0)
        def _():
            dw_ref[...] = jnp.zeros_like(dw_ref)
            da_ref[...] = jnp.zeros_like(da_ref)
            ddt_ref[...] = jnp.zeros_like(ddt_ref)

        halo = halo_ref[...]
        before = jnp.where(ti % tpe == 0, jnp.zeros_like(halo), halo)
        w = w_ref[...]
        taps, c, s = _conv_silu(x_ref[...], before, w)
        for h in range(GDN_HEADS):
            for base, ref, sc in ((0, dq_ref, qscale), (GDN_WIDTH, dk_ref, 1.0)):
                lo = base + h * LANES
                xh = s[:, lo:lo + LANES]
                r = lax.rsqrt(jnp.sum(xh * xh, axis=-1, keepdims=True) + EPS)
                y = xh * r
                dy = ref[:, h * LANES:(h + 1) * LANES] * sc
                dsl[:, lo:lo + LANES] = r * (dy - y * jnp.sum(dy * y, axis=-1, keepdims=True))
        dsl[:, 2 * GDN_WIDTH:] = dv_ref[...]
        sg = _sigmoid(c)
        dc = dsl[...] * (sg * (1.0 + c * (1.0 - sg)))
        nxt = carry[...]
        after = jnp.where(ti % tpe == tpe - 1, jnp.zeros_like(nxt), nxt)
        carry[...] = dc[0:HALO, :]
        dx = w[CONV_WIDTH - 1:CONV_WIDTH, :] * dc
        for kk in range(CONV_WIDTH - 1):
            dx = dx + w[kk:kk + 1, :] * _shift_rows(dc, after, CONV_WIDTH - 1 - kk, False)
        dx_ref[...] = dx.astype(BF)
        for kk in range(CONV_WIDTH):
            dw_ref[kk:kk + 1, :] += jnp.sum(dc * taps[kk], axis=0, keepdims=True)
        gate = gt_ref[...]
        dgt = dgt_ref[...]
        is_a, is_b = _gate_lanes(ts)
        dg_raw = _mask_dot(_chunk_tri(ts, True), jnp.where(is_a, dgt, 0.0))
        z = gate + dt_ref[...]
        na = -jnp.exp(a_ref[...])
        dga = dg_raw * na * _sigmoid(z)
        beta = _sigmoid(gate)
        dgb = jnp.where(is_b, dgt * beta * (1.0 - beta), 0.0)
        dgo_ref[...] = (dff_ref[...] + dga + dgb).astype(BF)
        ddt_ref[...] += jnp.sum(dga, axis=0, keepdims=True)
        da_ref[...] += jnp.sum(dg_raw * na * _softplus(z), axis=0, keepdims=True)

    rev = lambda wd, j: pl.BlockSpec((ts, wd), lambda i: (nt - 1 - i, j))
    halo_spec = pl.BlockSpec((HALO, GDN_QKV), lambda i: (jnp.maximum((nt - 1 - i) * hb - 1, 0), GDN_COL // GDN_QKV))
    return pl.pallas_call(
        body, name=name, grid=(nt,),
        in_specs=[rev(GDN_QKV, GDN_COL // GDN_QKV), halo_spec, rev(LANES, GATE_COL // LANES),
                  rev(GDN_WIDTH, 0), rev(GDN_WIDTH, 0), rev(GDN_WIDTH, 0), rev(LANES, 0), rev(LANES, 0),
                  _full((CONV_WIDTH, GDN_QKV)), _full((1, LANES)), _full((1, LANES))],
        out_specs=[rev(GDN_QKV, 0), rev(LANES, 0), _full((CONV_WIDTH, GDN_QKV)), _full((1, LANES)),
                   _full((1, LANES))],
        out_shape=[jax.ShapeDtypeStruct((t, GDN_QKV), BF), jax.ShapeDtypeStruct((t, LANES), BF),
                   jax.ShapeDtypeStruct((CONV_WIDTH, GDN_QKV), F32), jax.ShapeDtypeStruct((1, LANES), F32),
                   jax.ShapeDtypeStruct((1, LANES), F32)],
        scratch_shapes=[pltpu.VMEM((ts, GDN_QKV), F32), pltpu.VMEM((HALO, GDN_QKV), F32)],
        compiler_params=_params(("arbitrary",)),
    )(proj, proj, proj, dq, dk, dv, dgates, dff, conv_w, a_pad, dt_pad)


PAIR = 2 * CHUNK


def _split_bf16(a):
    hi = a.astype(BF)
    return hi, (a - hi.astype(F32)).astype(BF)


def _dot3(a, b, dims=(((1,), (0,)), ((), ()))):
    ah, al = _split_bf16(a)
    bh, bl = _split_bf16(b)
    (ca,), (cb,) = dims[0]
    return lax.dot_general(jnp.concatenate([ah, al, ah], axis=ca), jnp.concatenate([bh, bh, bl], axis=cb), dims,
                           preferred_element_type=F32)


def _inv_unit_lower(a):
    r = lax.broadcasted_iota(jnp.int32, (PAIR, PAIR), 0)
    c = lax.broadcasted_iota(jnp.int32, (PAIR, PAIR), 1)
    tm = (r == c).astype(F32) - a
    pw = _dot3(a, a)
    for _ in range(4):
        x = _dot3(jnp.concatenate([tm, pw], axis=0), pw)
        tm = tm + x[:PAIR]
        pw = x[PAIR:]
    return tm + _dot3(tm, pw)


def _gdn_pair_local(q, k, v, gc, gr, b):
    r = lax.broadcasted_iota(jnp.int32, (PAIR, PAIR), 0)
    c = lax.broadcasted_iota(jnp.int32, (PAIR, PAIR), 1)
    same = (r // CHUNK) == (c // CHUNK)
    incl = same & (r >= c)
    strict = same & (r > c)
    dm = jnp.exp(jnp.where(incl, gc - gr, NEG))
    e = jnp.exp(gc)
    kb = k * b
    vb = v * b
    kbe = kb * e
    kq = _dot_nt(jnp.concatenate([kb, q], axis=0).astype(BF), k.astype(BF))
    amat = jnp.where(strict, kq[:PAIR] * dm, 0.0)
    pmat = jnp.where(incl, kq[PAIR:] * dm, 0.0)
    lane = lax.broadcasted_iota(jnp.int32, (1, PAIR), 1)
    gl_a = jnp.sum(jnp.where(lane == CHUNK - 1, gr, 0.0), axis=1, keepdims=True)
    gl_b = jnp.sum(jnp.where(lane == PAIR - 1, gr, 0.0), axis=1, keepdims=True)
    ridx = lax.broadcasted_iota(jnp.int32, (PAIR, 1), 0)
    edec = jnp.exp(jnp.where(ridx < CHUNK, gl_a, gl_b) - gc)
    return dict(dm=dm, e=e, kb=kb, vb=vb, kbe=kbe, amat=amat, pmat=pmat, gl_a=gl_a, gl_b=gl_b, edec=edec,
                kd=k * edec, qd=q * e, incl=incl, strict=strict, ridx=ridx)


def _gdn_pair_states(loc, tb, s_a):
    uw = _dot(tb, jnp.concatenate([loc["vb"], loc["kbe"]], axis=1).astype(BF))
    u, w = uw[:, :LANES], uw[:, LANES:]
    qd, kd, c = loc["qd"], loc["kd"], CHUNK
    xa = _dot(jnp.concatenate([qd[:c], w[:c]], axis=0).astype(BF), s_a.astype(BF))
    vn_a = u[:c] - xa[c:]
    s_b = s_a * jnp.exp(loc["gl_a"]) + _dot_tn(kd[:c].astype(BF), vn_a.astype(BF))
    xb = _dot(jnp.concatenate([qd[c:], w[c:]], axis=0).astype(BF), s_b.astype(BF))
    vn_b = u[c:] - xb[c:]
    s_c = s_b * jnp.exp(loc["gl_b"]) + _dot_tn(kd[c:].astype(BF), vn_b.astype(BF))
    vn = jnp.concatenate([vn_a, vn_b], axis=0)
    o = jnp.concatenate([xa[:c], xb[:c]], axis=0) + _dot(loc["pmat"].astype(BF), vn.astype(BF))
    return w, vn, o, s_b, s_c


GDN_SEG = 512


def _gdn_specs(nb, seq, reverse):
    n = seq // CHUNK
    seg = _tile(seq, GDN_SEG)
    nseg = seq // seg
    sp = seg // PAIR
    at = (lambda s: nseg - 1 - s) if reverse else (lambda s: s)
    blk = pl.BlockSpec((nb, seg, GDN_WIDTH), lambda s: (0, at(s), 0))
    gg = pl.BlockSpec((nb, seg, GDN_WIDTH), lambda s: (0, at(s), GG_COL // GDN_WIDTH))
    gates = pl.BlockSpec((nb, seg, LANES), lambda s: (0, at(s), 0))
    rowb = pl.BlockSpec((nb, GDN_HEADS, sp, HALO, PAIR), lambda s: (0, 0, at(s), 0, 0))
    per_pair = pl.BlockSpec((nb, GDN_HEADS, sp, PAIR, PAIR), lambda s: (0, 0, at(s), 0, 0))
    return n, seg, nseg, sp, blk, gg, gates, rowb, per_pair


def _head_column(gt, lane, index):
    return jnp.sum(jnp.where(lane == index, gt, 0.0), axis=1, keepdims=True)


def _gdn_head_inputs(qkv_refs, gt_ref, gr_ref, rows, pi, lane, chains):
    per_chain = []
    for b, hh in chains:
        gt = gt_ref[b, rows, :]
        cols = slice(hh * LANES, (hh + 1) * LANES)
        per_chain.append([r[b, rows, cols].astype(F32) for r in qkv_refs]
                         + [_head_column(gt, lane, A_LANE + hh), gr_ref[b, hh, pi][0:1, :],
                            _head_column(gt, lane, B_LANE + hh)])
    return [jnp.stack(xs) for xs in zip(*per_chain)]


def _gdn_pair_fwd(qv, kv, vv, gcv, gr, bv, s_a):
    loc = _gdn_pair_local(qv, kv, vv, gcv, gr, bv)
    tf = _inv_unit_lower(loc["amat"])
    _, _, o, _, s_c = _gdn_pair_states(loc, tf.astype(BF), s_a)
    return tf, o, s_c


def _gdn_fwd(q, k, v, proj, gates, grow, wn, nb, seq, name):
    n, seg, nseg, sp, blk, gg, gates_spec, rowb, per_pair = _gdn_specs(nb, seq, False)
    chains = [(b, hh) for b in range(nb) for hh in range(GDN_HEADS)]

    def body(q_ref, k_ref, v_ref, gg_ref, gt_ref, gr_ref, wn_ref, y_ref, tn_ref, sn_ref, s_ref):
        @pl.when(pl.program_id(0) == 0)
        def _():
            s_ref[...] = jnp.zeros_like(s_ref)

        wnv = wn_ref[...]
        lane = lax.broadcasted_iota(jnp.int32, (PAIR, LANES), 1)

        def step(pi, carry):
            rows = pl.ds(pl.multiple_of(pi * PAIR, PAIR), PAIR)
            ins = _gdn_head_inputs((q_ref, k_ref, v_ref), gt_ref, gr_ref, rows, pi, lane, chains)
            s_a = s_ref[...]
            tf, o, s_c = jax.vmap(_gdn_pair_fwd)(*ins, s_a)
            s_ref[...] = s_c
            for c, (b, hh) in enumerate(chains):
                cols = slice(hh * LANES, (hh + 1) * LANES)
                tn_ref[b, hh, pi] = tf[c]
                sn_ref[b, hh, pi] = s_a[c]
                g = gg_ref[b, rows, cols]
                oh = o[c]
                rstd = lax.rsqrt(jnp.mean(oh * oh, axis=-1, keepdims=True) + EPS)
                y_ref[b, rows, cols] = (oh * rstd * wnv * (g * _sigmoid(g))).astype(BF)
            return carry

        lax.fori_loop(0, sp, step, 0)

    saved = jax.ShapeDtypeStruct((nb, GDN_HEADS, n // 2, PAIR, PAIR), F32)
    return pl.pallas_call(
        body, name=name, grid=(nseg,),
        in_specs=[blk, blk, blk, gg, gates_spec, rowb, _full((1, LANES))],
        out_specs=[blk, per_pair, per_pair],
        out_shape=[jax.ShapeDtypeStruct((nb, seq, GDN_WIDTH), BF), saved, saved],
        scratch_shapes=[pltpu.VMEM((len(chains), GDN_HEAD_DIM, GDN_HEAD_DIM), F32)],
        compiler_params=_params(("arbitrary",)),
    )(q, k, v, proj, gates, grow, wn)


def _gdn_pair_bwd(qv, kv, vv, gcv, gr, bv, tf, s_a, dsp, g, dyv, wnv):
    c = CHUNK
    loc = _gdn_pair_local(qv, kv, vv, gcv, gr, bv)
    tm = tf.astype(BF)
    kb, vb, kbe, e, dm = loc["kb"], loc["vb"], loc["kbe"], loc["e"], loc["dm"]
    kd, qd, pmat, amat = loc["kd"], loc["qd"], loc["pmat"], loc["amat"]
    w, vn, o, s_b, _ = _gdn_pair_states(loc, tm, s_a)
    sg = _sigmoid(g)
    silu = g * sg
    rstd = lax.rsqrt(jnp.mean(o * o, axis=-1, keepdims=True) + EPS)
    xhat = o * rstd
    dwn = jnp.sum(dyv * xhat * silu, axis=0, keepdims=True)
    dgg = dyv * xhat * wnv * (sg * (1.0 + g * (1.0 - sg)))
    dxhat = dyv * wnv * silu
    do = rstd * (dxhat - xhat * jnp.mean(dxhat * xhat, axis=-1, keepdims=True))
    dob = do.astype(BF)
    tot = lambda x: jnp.sum(jnp.sum(x, axis=1, keepdims=True), axis=0, keepdims=True)
    rsum = lambda x: jnp.sum(x, axis=1, keepdims=True)
    cat = lambda xs, ax=0: jnp.concatenate(xs, axis=ax)
    wb = w.astype(BF)
    qdb = qd.astype(BF)
    kdb = kd.astype(BF)
    vnb = vn.astype(BF)
    egl_a = jnp.exp(loc["gl_a"])
    egl_b = jnp.exp(loc["gl_b"])
    ptdo = _dot_tn(pmat.astype(BF), dob)
    dspb = dsp.astype(BF)
    dvn_b = ptdo[c:] + _dot(kdb[c:], dspb)
    dkd_b = _dot_nt(vnb[c:], dspb)
    dgl_b = egl_b * tot(s_b * dsp) + tot(dkd_b * kd[c:])
    dsm = egl_b * dsp + _dot_tn(cat([qdb[c:], -wb[c:]]), cat([dob[c:], dvn_b.astype(BF)]))
    dsmb = dsm.astype(BF)
    dvn_a = ptdo[:c] + _dot(kdb[:c], dsmb)
    dkd_a = _dot_nt(vnb[:c], dsmb)
    dgl_a = egl_a * tot(s_a * dsm) + tot(dkd_a * kd[:c])
    ds_new = egl_a * dsm + _dot_tn(cat([qdb[:c], -wb[:c]]), cat([dob[:c], dvn_a.astype(BF)]))
    ya = _dot_nt(cat([dob[:c], dvn_a.astype(BF)]), s_a.astype(BF))
    yb = _dot_nt(cat([dob[c:], dvn_b.astype(BF)]), s_b.astype(BF))
    dqd = cat([ya[:c], yb[:c]])
    dw = -cat([ya[c:], yb[c:]])
    dvn = cat([dvn_a, dvn_b])
    dkd = cat([dkd_a, dkd_b])
    dq = dqd * e
    dgc = rsum(dqd * qd) - rsum(dkd * kd)
    dk = dkd * loc["edec"]
    dpm = jnp.where(loc["incl"], _dot_nt(dob, vnb), 0.0)
    duw = cat([dvn, dw], 1).astype(BF)
    dt = _dot_nt(duw, cat([vb, kbe], 1).astype(BF))
    tt = _dot_tn(tm, duw)
    dvb, dkbe = tt[:, :LANES], tt[:, LANES:]
    tn_dims = (((0,), (0,)), ((), ()))
    nt_dims = (((1,), (1,)), ((), ()))
    da = jnp.where(loc["strict"], -_dot3(_dot3(tf, dt, tn_dims), tf, nt_dims), 0.0)
    st = cat([da * dm, dpm * dm]).astype(BF)
    z = _dot(st, kv.astype(BF))
    dkb = z[:PAIR] + dkbe * e
    dq = dq + z[PAIR:]
    dk = dk + _dot_tn(st, cat([kb, qv]).astype(BF))
    gmat = dpm * pmat + da * amat
    dgc = dgc + rsum(dkbe * kbe) + rsum(gmat)
    ridx = loc["ridx"]
    dgc = dgc + jnp.where(ridx == c - 1, dgl_a, 0.0) + jnp.where(ridx == PAIR - 1, dgl_b, 0.0)
    dgc_row = jnp.sum(gmat, axis=0, keepdims=True)
    db = rsum(dvb * vv) + rsum(dkb * kv)
    return dq, dk + dkb * bv, dvb * bv, dgg, dgc, dgc_row, db, dwn, ds_new


def _gdn_bwd(q, k, v, proj, gates, grow, wn, tinv_all, states_all, dy, nb, seq, name):
    n, seg, nseg, sp, blk, gg, gates_spec, rowb, per_pair = _gdn_specs(nb, seq, True)
    dh = GDN_HEAD_DIM
    chains = [(b, hh) for b in range(nb) for hh in range(GDN_HEADS)]

    def body(q_ref, k_ref, v_ref, gg_ref, gt_ref, gr_ref, wn_ref, tn_ref, sn_ref, dy_ref,
             dq_ref, dk_ref, dv_ref, dgg_ref, dgt_ref, dwn_ref, ds_ref):
        @pl.when(pl.program_id(0) == 0)
        def _():
            dwn_ref[...] = jnp.zeros_like(dwn_ref)
            ds_ref[...] = jnp.zeros_like(ds_ref)

        wnv = wn_ref[...]
        lane = lax.broadcasted_iota(jnp.int32, (PAIR, LANES), 1)

        def bwd_step(j, carry):
            pi = sp - 1 - j
            rows = pl.ds(pl.multiple_of(pi * PAIR, PAIR), PAIR)
            ins = _gdn_head_inputs((q_ref, k_ref, v_ref), gt_ref, gr_ref, rows, pi, lane, chains)
            lanes_of = lambda hh: slice(hh * LANES, (hh + 1) * LANES)
            saved = [jnp.stack([r[b, hh, pi] for b, hh in chains]) for r in (tn_ref, sn_ref)]
            g2 = jnp.stack([gg_ref[b, rows, lanes_of(hh)] for b, hh in chains])
            dy2 = jnp.stack([dy_ref[b, rows, lanes_of(hh)].astype(F32) for b, hh in chains])
            dq, dk, dv, dgg, dgc, dgc_row, db, dwn, ds_new = jax.vmap(
                _gdn_pair_bwd, in_axes=(0,) * 11 + (None,))(*ins, *saved, ds_ref[...], g2, dy2, wnv)
            ds_ref[...] = ds_new
            dgt = [jnp.zeros((PAIR, LANES), F32) for _ in range(nb)]
            for c, (b, hh) in enumerate(chains):
                cols = lanes_of(hh)
                dq_ref[b, rows, cols] = dq[c]
                dk_ref[b, rows, cols] = dk[c]
                dv_ref[b, rows, cols] = dv[c]
                dgg_ref[b, rows, cols] = dgg[c].astype(BF)
                dwn_ref[...] += dwn[c]
                row_as_col = jnp.transpose(jnp.broadcast_to(dgc_row[c], (PAIR, LANES)))
                dgt[b] = (dgt[b] + jnp.where(lane == A_LANE + hh, dgc[c] - row_as_col, 0.0)
                          + jnp.where(lane == B_LANE + hh, db[c], 0.0))
            for b in range(nb):
                dgt_ref[b, rows, :] = dgt[b]
            return carry

        lax.fori_loop(0, sp, bwd_step, 0)

    f32_out = jax.ShapeDtypeStruct((nb, seq, GDN_WIDTH), F32)
    return pl.pallas_call(
        body, name=name, grid=(nseg,),
        in_specs=[blk, blk, blk, gg, gates_spec, rowb, _full((1, LANES)), per_pair, per_pair, blk],
        out_specs=[blk, blk, blk, blk, gates_spec, _full((1, LANES))],
        out_shape=[f32_out, f32_out, f32_out, jax.ShapeDtypeStruct((nb, seq, GDN_WIDTH), BF),
                   jax.ShapeDtypeStruct((nb, seq, LANES), F32), jax.ShapeDtypeStruct((1, LANES), F32)],
        scratch_shapes=[pltpu.VMEM((len(chains), dh, dh), F32)],
        compiler_params=_params(("arbitrary",)),
    )(q, k, v, proj, gates, grow, wn, tinv_all, states_all, dy)


def _mix_to_padded(w):
    pad = jnp.zeros(w.shape[:-1] + (N_PAD - N_IN,), w.dtype)
    return jnp.concatenate([w[..., 0:1536], w[..., 1544:3080], w[..., 3088:3600], w[..., 1536:1544],
                            w[..., 3080:3088], pad], axis=-1)


def _pad_lanes(vec, start):
    return jnp.pad(vec[None, :], ((0, 0), (start, LANES - start - vec.shape[0])))


def _heads_to_rows(block, lane0, nheads, nb, seq):
    return block[:, lane0:lane0 + nheads].reshape(nb, seq, nheads).transpose(0, 2, 1).reshape(nb * nheads, seq)


def _mixer_small(p, l):
    wq_t = jnp.tile(p["fox_q_norm"][l], FOX_HEADS)[None, :]
    wk_t = jnp.tile(p["fox_k_norm"][l], FOX_HEADS)[None, :]
    bias = _pad_lanes(p["fox_f_bias"][l], 0)
    a_pad = _pad_lanes(p["gdn_a_log"][l], A_LANE)
    dt_pad = _pad_lanes(p["gdn_dt_bias"][l], A_LANE)
    wn = p["gdn_out_norm"][l][None, :]
    return wq_t, wk_t, bias, a_pad, dt_pad, wn


def _layer_fwd(x, p, l, nb, seq, rider=None, ffn1_rider=None, after_ffn1=None):
    npair = FOX_HEADS // 2
    n = seq // CHUNK
    x1, h1, *rode1 = _ffn_fwd(x, p["ffn1_norm"][l][None, :], p["ffn1_w_in"][l], p["ffn1_w_out"][l],
                              f"ffn1_fwd_{l}", ffn1_rider)
    if after_ffn1 is not None:
        after_ffn1(rode1)
    wq_t, wk_t, bias, a_pad, dt_pad, wn = _mixer_small(p, l)
    proj = _norm_matmul(x1, p["mix_norm"][l][None, :], p["w_mix"][l], f"mix_in_{l}")
    fq, fk, fv, cum = _fox_prep(proj, wq_t, wk_t, bias, seq, f"fox_prep_{l}")
    c8 = _heads_to_rows(cum, 0, FOX_HEADS, nb, seq)
    ck = c8.reshape(nb * npair, 2, 1, seq)
    o, lse, *rode = _fox_attn(fq, fk, fv, ck, nb, seq, f"fox_attn_{l}", rider)
    gq, gk, gv, gates = _gdn_prep(proj, p["gdn_conv"][l], a_pad, dt_pad, seq, f"gdn_prep_{l}")
    gc4 = _heads_to_rows(gates, A_LANE, GDN_HEADS, nb, seq)
    grow = jnp.broadcast_to(gc4.reshape(nb, GDN_HEADS, n // 2, 1, PAIR), (nb, GDN_HEADS, n // 2, HALO, PAIR))
    per_example = lambda a: a.reshape(nb, seq, a.shape[-1])
    gq, gk, gv, gates = per_example(gq), per_example(gk), per_example(gv), per_example(gates)
    y, tinv, states = _gdn_fwd(gq, gk, gv, per_example(proj), gates, grow, wn, nb, seq, f"gdn_fwd_{l}")
    y = y.reshape(nb * seq, GDN_WIDTH)
    x2 = _mix_out(x1, o, y, p["w_out"][l], f"mix_out_{l}")
    x3, h2 = _ffn_fwd(x2, p["ffn2_norm"][l][None, :], p["ffn2_w_in"][l], p["ffn2_w_out"][l], f"ffn2_fwd_{l}")
    saved = dict(x=x, h1=h1, x1=x1, proj=proj, fq=fq, fk=fk, fv=fv, ck=ck, o=o, lse=lse,
                 gq=gq, gk=gk, gv=gv, gates=gates, grow=grow, tinv=tinv, states=states, y=y, x2=x2, h2=h2)
    return x3, saved, rode


def _ffn_grads(dy, x, h, gain, win, wout, l, tag, rider=None):
    t, d = x.shape
    fs = win.shape[2]
    dx, dh, a, hn, dyh, dgain, *rode = _ffn_bwd(dy, x, h, gain, win, wout, f"{tag}_bwd_{l}", rider)
    g_in = _wgrad(hn, dh, jax.ShapeDtypeStruct((4, d, fs), BF),
                  pl.BlockSpec((None, d, fs), lambda i, j, k: (j, i, 0)), d, fs, f"{tag}_gw_in_{l}")
    g_out = _wgrad(a, dyh, jax.ShapeDtypeStruct((2 * fs, d), BF),
                   pl.BlockSpec((fs, d), lambda i, j, k: (i, j)), fs, d, f"{tag}_gw_out_{l}")
    return dx, dgain[0], g_in, g_out.reshape(4, fs // 2, d), rode


def _layer_bwd(dx3, p, l, sv, nb, seq, rider=None, before_ffn1=None, ffn2_rider=None, after_ffn2=None):
    npair = FOX_HEADS // 2
    d = dx3.shape[1]
    wq_t, wk_t, bias, a_pad, dt_pad, wn = _mixer_small(p, l)
    g = {}
    dx2, g["ffn2_norm"], g["ffn2_w_in"], g["ffn2_w_out"], rode2 = _ffn_grads(
        dx3, sv["x2"], sv["h2"], p["ffn2_norm"][l][None, :], p["ffn2_w_in"][l], p["ffn2_w_out"][l], l, "ffn2",
        ffn2_rider)
    if after_ffn2 is not None:
        rider = after_ffn2(rode2)
    dyf, dyg, dxb = _mix_out_bwd(dx2, p["w_out"][l], f"mix_out_bwd_{l}")
    half = lambda a, nm: _wgrad(a, dxb, jax.ShapeDtypeStruct((FOX_WIDTH, d), BF),
                                pl.BlockSpec((FOX_WIDTH, d), lambda i, j, k: (i, j)), FOX_WIDTH, d, nm)
    g["w_out"] = jnp.concatenate([half(sv["o"], f"gw_out_fox_{l}"), half(sv["y"], f"gw_out_gdn_{l}")], axis=0)
    dqa, dqb, dk, dv, dkx, *rode = _fox_attn_bwd(sv["fq"], sv["fk"], sv["fv"], sv["o"], dyf, sv["lse"], sv["ck"],
                                                 nb, seq, f"fox_attn_bwd_{l}", rider)

    dpf, dff, dwq, dwk, dbias = _fox_prep_bwd(sv["proj"], dqa, dqb, dk, dv, dkx, wq_t, wk_t, bias, seq,
                                              f"fox_prep_bwd_{l}")
    g["fox_q_norm"] = dwq[0, :FOX_HEAD_DIM]
    g["fox_k_norm"] = dwk[0, :FOX_HEAD_DIM]
    g["fox_f_bias"] = dbias[0, :FOX_HEADS]
    per_example = lambda a: a.reshape(nb, seq, a.shape[-1])
    flat = lambda a: a.reshape(nb * seq, a.shape[-1])
    dgq, dgk, dgv, dgg, dgates, dwn = _gdn_bwd(
        sv["gq"], sv["gk"], sv["gv"], per_example(sv["proj"]), sv["gates"], sv["grow"], wn, sv["tinv"],
        sv["states"], per_example(dyg), nb, seq, f"gdn_bwd_{l}")
    dgq, dgk, dgv, dgg, dgates = flat(dgq), flat(dgk), flat(dgv), flat(dgg), flat(dgates)
    dpg, dgate_blk, dconv, da, ddt = _gdn_prep_bwd(sv["proj"], dgq, dgk, dgv, dgates, dff, p["gdn_conv"][l],
                                                   a_pad, dt_pad, seq, f"gdn_prep_bwd_{l}")
    g["gdn_conv"] = dconv
    g["gdn_a_log"] = da[0, A_LANE:B_LANE]
    g["gdn_dt_bias"] = ddt[0, A_LANE:B_LANE]
    g["gdn_out_norm"] = dwn[0]
    dparts = [dpf, dpg, dgg, dgate_blk]
    dx1, hnm, dgm = _norm_matmul_bwd(dx2, dparts, sv["x1"], p["mix_norm"][l][None, :], p["w_mix"][l],
                                     f"mix_in_bwd_{l}")
    g["mix_norm"] = dgm[0]
    gp = _wgrad_parts(hnm, dparts, d // 2, f"gw_mix_{l}")
    gate = GATE_COL
    g["w_in"] = jnp.concatenate([gp[:, :GDN_COL], gp[:, gate:gate + FOX_HEADS], gp[:, GDN_COL:GG_COL],
                                 gp[:, gate + A_LANE:gate + B_LANE + GDN_HEADS], gp[:, GG_COL:gate]], axis=1)
    ffn1_rider = before_ffn1(g) if before_ffn1 is not None else None
    dx0, g["ffn1_norm"], g["ffn1_w_in"], g["ffn1_w_out"], rode1 = _ffn_grads(
        dx1, sv["x"], sv["h1"], p["ffn1_norm"][l][None, :], p["ffn1_w_in"][l], p["ffn1_w_out"][l], l, "ffn1",
        ffn1_rider)
    return dx0, g, rode, rode1


def _local_step(x, target, p):
    nb, seq, d = x.shape
    xt = x.reshape(nb * seq, d)
    saved = []
    for l in range(DEPTH):
        xt, sv, _ = _layer_fwd(xt, p, l, nb, seq)
        saved.append(sv)
    loss, dx = _loss_grad(xt, target.reshape(nb * seq, d), "loss")
    grads = [None] * DEPTH
    for l in reversed(range(DEPTH)):
        dx, grads[l], _, _ = _layer_bwd(dx, p, l, saved[l], nb, seq)
    return loss, dx.reshape(nb, seq, d), grads


N_CHIPS = 4


def _mesh_pos():
    return lax.axis_index("x"), lax.axis_index("y"), lax.axis_index("c")


def _other_chips(x, y):
    return [(1 - x, y), (x, 1 - y), (1 - x, 1 - y)]


def _remote(src, dst, send_sem, recv_sem, to):
    return pltpu.make_async_remote_copy(src_ref=src, dst_ref=dst, send_sem=send_sem, recv_sem=recv_sem,
                                        device_id=to, device_id_type=MESH)


def _hbm_call(body, name, ins, out_shape, scratch):
    return pl.pallas_call(
        body, name=name, out_shape=out_shape, in_specs=[HBM] * len(ins),
        out_specs=jax.tree.map(lambda _: HBM, out_shape), scratch_shapes=scratch,
        compiler_params=pltpu.CompilerParams(has_side_effects=True),
    )(*ins)


def _gather_phases(n, layer):
    def copies(ins, outs, sems):
        send1, recv1, send2, recv2 = sems
        x, y, c = _mesh_pos()
        out, back, fwd = [], [], []
        for i in range(n):
            for j, (px, py) in enumerate(_other_chips(x, y)):
                k = 3 * i + j
                blk = outs[i].at[2 * px + py]
                out.append(_remote(ins[i], outs[i].at[2 * x + y], send1.at[k], recv1.at[k], (px, py, c)))
                back.append(_remote(blk, blk, send1.at[k], recv1.at[k], (px, py, c)))
                fwd.append(_remote(blk, blk, send2.at[k], recv2.at[k], (x, y, 1 - c)))
        return c, out, back, fwd

    def first(ins, outs, sems):
        c, out, _, _ = copies(ins, outs, sems)

        @pl.when(c == layer)
        def _():
            for cp in out:
                cp.start()

    def middle(ins, outs, sems):
        c, _, back, fwd = copies(ins, outs, sems)

        @pl.when(c == layer)
        def _():
            for arrived, onward in zip(back, fwd):
                arrived.wait_recv()
                onward.start()

    def last(ins, outs, sems):
        c, out, _, fwd = copies(ins, outs, sems)

        @pl.when(c == layer)
        def _():
            for cp in out + fwd:
                cp.wait_send()

        @pl.when(c != layer)
        def _():
            for cp in fwd:
                cp.wait_recv()

    return first, middle, last


def _scatter_phases(n, layer):
    def copies(ins, outs, sems):
        send, recv = sems
        x, y, c = _mesh_pos()
        return c, [_remote(ins[i].at[2 * px + py], outs[i].at[j], send.at[3 * i + j], recv.at[3 * i + j], (px, py, c))
                   for i in range(n) for j, (px, py) in enumerate(_other_chips(x, y))]

    def first(ins, outs, sems):
        c, cps = copies(ins, outs, sems)

        @pl.when(c == layer)
        def _():
            for cp in cps:
                cp.start()

    def middle(ins, outs, sems):
        pass

    def last(ins, outs, sems):
        c, cps = copies(ins, outs, sems)

        @pl.when(c == layer)
        def _():
            for cp in cps:
                cp.wait()

    return first, middle, last


def _exchange(blocks, out_shapes, n_sems, phases, name, rider):
    sems = [pltpu.SemaphoreType.DMA((3 * len(blocks),))] * n_sems
    if rider:
        return _Rider(blocks, out_shapes, sems, phases)
    n = len(blocks)

    def body(*refs):
        for phase in phases:
            phase(refs[:n], refs[n:2 * n], refs[2 * n:])

    return list(_hbm_call(body, name, blocks, out_shapes, sems))


def _gather_layer(blocks, layer, name=None, rider=False):
    outs = [jax.ShapeDtypeStruct((N_CHIPS,) + b.shape, b.dtype) for b in blocks]
    return _exchange(blocks, outs, 4, _gather_phases(len(blocks), layer), name, rider)


def _scatter_layer(sums, layer, name=None, rider=False):
    outs = [jax.ShapeDtypeStruct((3,) + s.shape[1:], s.dtype) for s in sums]
    return _exchange(sums, outs, 2, _scatter_phases(len(sums), layer), name, rider)


def _to_sibling(gs, layer, name=None, rider=False):
    n = len(gs)

    def copies(ins, outs, sems):
        send, recv = sems
        x, y, c = _mesh_pos()
        return c, [_remote(ins[i], outs[i], send.at[i], recv.at[i], (x, y, 1 - c)) for i in range(n)]

    def first(ins, outs, sems):
        c, cps = copies(ins, outs, sems)

        @pl.when(c != layer)
        def _():
            for cp in cps:
                cp.start()

    def middle(ins, outs, sems):
        pass

    def last(ins, outs, sems):
        c, cps = copies(ins, outs, sems)

        @pl.when(c != layer)
        def _():
            for cp in cps:
                cp.wait_send()

        @pl.when(c == layer)
        def _():
            for cp in cps:
                cp.wait_recv()

    sems = [pltpu.SemaphoreType.DMA((n,))] * 2
    outs = [jax.ShapeDtypeStruct(g.shape, g.dtype) for g in gs]
    if rider:
        return _Rider(gs, outs, sems, (first, middle, last))

    def body(*refs):
        for phase in (first, middle, last):
            phase(refs[:n], refs[n:2 * n], refs[2 * n:])

    return list(_hbm_call(body, name, gs, outs, sems))


def _sibling_swap(rs, name):
    n = len(rs)

    def body(*refs):
        ins, outs = refs[:n], refs[n:2 * n]
        send, recv = refs[2 * n:]
        x, y, c = _mesh_pos()
        cps = [_remote(ins[i], outs[i], send.at[i], recv.at[i], (x, y, 1 - c)) for i in range(n)]
        for cp in cps:
            cp.start()
        for cp in cps:
            cp.wait()

    sem = pltpu.SemaphoreType.DMA((n,))
    return _hbm_call(body, name, rs, [jax.ShapeDtypeStruct(r.shape, r.dtype) for r in rs], [sem, sem])


def _small_all_reduce(vec, name):
    r = vec.shape[0]
    ndev = 8

    def body(v_ref, o_ref, buf, send, recv):
        x, y, c = _mesh_pos()
        me = 4 * x + 2 * y + c
        buf[me] = v_ref[...]
        cps = []
        for rel in range(1, ndev):
            px = 1 - x if rel & 4 else x
            py = 1 - y if rel & 2 else y
            pc = 1 - c if rel & 1 else c
            cps.append((_remote(v_ref, buf.at[me], send.at[rel - 1], recv.at[rel - 1], (px, py, pc)),
                        4 * px + 2 * py + pc))
        for cp, _ in cps:
            cp.start()
        for k, (cp, peer) in enumerate(cps):
            slot = buf.at[peer]
            _remote(slot, slot, send.at[k], recv.at[k], (x, y, c)).wait_recv()
        for cp, _ in cps:
            cp.wait_send()
        acc = buf[0]
        for k in range(1, ndev):
            acc = acc + buf[k]
        o_ref[...] = acc

    vm = pl.BlockSpec(memory_space=pltpu.VMEM)
    return pl.pallas_call(
        body, name=name, out_shape=jax.ShapeDtypeStruct(vec.shape, F32), in_specs=[vm], out_specs=vm,
        scratch_shapes=[pltpu.VMEM((ndev, r, LANES), F32), pltpu.SemaphoreType.DMA((ndev - 1,)),
                        pltpu.SemaphoreType.DMA((ndev - 1,))],
        compiler_params=pltpu.CompilerParams(has_side_effects=True),
    )(vec)


def _row_tile(rows, cap=512):
    for t in range(min(rows, cap), 0, -1):
        if rows % t == 0 and (t % 16 == 0 or t == rows):
            return t
    raise ValueError(rows)


def _add_pairs(a, b, name):
    k, r, c = a.shape
    tr = _row_tile(r)

    def body(a_ref, b_ref, o_ref):
        o_ref[...] = (a_ref[...].astype(F32) + b_ref[...].astype(F32)).astype(o_ref.dtype)

    spec = pl.BlockSpec((None, tr, c), lambda i, j: (i, j, 0))
    return pl.pallas_call(body, name=name, grid=(k, r // tr), in_specs=[spec, spec], out_specs=spec,
                          out_shape=jax.ShapeDtypeStruct(a.shape, a.dtype),
                          compiler_params=_params(("parallel", "parallel")))(a, b)


def _final_sum(own, sib, others, name):
    r, c = own.shape
    tr = _row_tile(r)

    def body(a_ref, b_ref, o_ref_in, out_ref):
        acc = a_ref[...].astype(F32) + b_ref[...].astype(F32)
        for k in range(3):
            acc = acc + o_ref_in[k].astype(F32)
        out_ref[...] = acc

    spec = pl.BlockSpec((tr, c), lambda i: (i, 0))
    return pl.pallas_call(body, name=name, grid=(r // tr,),
                          in_specs=[spec, spec, pl.BlockSpec((3, tr, c), lambda i: (0, i, 0))], out_specs=spec,
                          out_shape=jax.ShapeDtypeStruct((r, c), F32),
                          compiler_params=_params(("parallel",)))(own, sib, others)


def _adamw(g, w, m, v, name):
    r, c = g.shape
    tr = _row_tile(r, 256)

    def body(g_ref, w_ref, m_ref, v_ref, d_ref, mo_ref, vo_ref):
        gv = g_ref[...]
        mn = ADAM_B1 * m_ref[...] + (1.0 - ADAM_B1) * gv
        vn = ADAM_B2 * v_ref[...] + (1.0 - ADAM_B2) * (gv * gv)
        m_hat = mn / (1.0 - ADAM_B1 ** ADAM_STEP)
        v_hat = vn / (1.0 - ADAM_B2 ** ADAM_STEP)
        d_ref[...] = -ADAM_LR * (m_hat / (jnp.sqrt(v_hat) + ADAM_EPS) + ADAM_WD * w_ref[...])
        mo_ref[...] = mn
        vo_ref[...] = vn

    spec = pl.BlockSpec((tr, c), lambda i: (i, 0))
    shp = jax.ShapeDtypeStruct((r, c), F32)
    return pl.pallas_call(body, name=name, grid=(r // tr,), in_specs=[spec] * 4, out_specs=[spec] * 3,
                          out_shape=[shp] * 3, compiler_params=_params(("parallel",)))(g, w, m, v)


def _pack(arrays):
    flat = jnp.concatenate([a.reshape(-1).astype(F32) for a in arrays])
    pad = (-flat.shape[0]) % (8 * LANES)
    return jnp.concatenate([flat, jnp.zeros((pad,), F32)]).reshape(-1, LANES)


def _unpack(packed, shapes):
    flat = packed.reshape(-1)
    out, off = [], 0
    for s in shapes:
        size = 1
        for dim in s:
            size *= dim
        out.append(flat[off:off + size].reshape(s))
        off += size
    return out


BIG = ("ffn1_w_in", "ffn1_w_out", "w_in", "w_out", "ffn2_w_in", "ffn2_w_out")
SMALL = ("ffn1_norm", "mix_norm", "fox_q_norm", "fox_k_norm", "fox_f_bias", "gdn_a_log", "gdn_dt_bias",
         "gdn_out_norm", "ffn2_norm", "gdn_conv")
WEIGHTS = ("ffn1_norm", "ffn1_w_in", "ffn1_w_out", "mix_norm", "w_in", "fox_q_norm", "fox_k_norm", "fox_f_bias",
           "gdn_conv", "gdn_a_log", "gdn_dt_bias", "gdn_out_norm", "w_out", "ffn2_norm", "ffn2_w_in", "ffn2_w_out")


def _step(x, target, w, m, v):
    xi, yi, ci = _mesh_pos()
    me = 2 * xi + yi
    depth = DEPTH
    d = x.shape[-1]

    nb, seq, _ = x.shape
    assert depth == 2

    p = {k: w[k] for k in SMALL if k != "gdn_conv"}
    for k in ("ffn1_w_in", "ffn1_w_out", "ffn2_w_in", "ffn2_w_out", "w_mix", "w_out", "gdn_conv"):
        p[k] = [None] * depth

    first, rest = BIG[:2], BIG[2:] + ("gdn_conv",)

    def shards(l, names):
        return [w[k][l] if k == "gdn_conv" else w[k][l].astype(BF) for k in names]

    def place(l, names, gathered):
        blocks = dict(zip(names, [lax.dynamic_update_index_in_dim(g, s, me, 0)
                                  for g, s in zip(gathered, shards(l, names))]))
        for k in ("ffn1_w_in", "ffn1_w_out", "ffn2_w_in", "ffn2_w_out"):
            if k in blocks:
                p[k][l] = blocks[k]
        if "w_in" in blocks:
            p["w_mix"][l] = _mix_to_padded(blocks["w_in"].transpose(1, 0, 2).reshape(d, N_IN))
            p["w_out"][l] = blocks["w_out"].reshape(2 * FOX_WIDTH, d)
            p["gdn_conv"][l] = blocks["gdn_conv"].transpose(1, 0, 2).reshape(CONV_WIDTH, -1)

    place(0, first, _gather_layer(shards(0, first), 0, "gather_first_ffn0"))
    xt = x.reshape(nb * seq, d)
    xt, saved0, gathered1 = _layer_fwd(
        xt, p, 0, nb, seq, _gather_layer(shards(1, first + rest), 1, rider=True),
        _gather_layer(shards(0, rest), 0, rider=True), lambda got: place(0, rest, got))
    place(1, first + rest, gathered1)
    xt, saved1, _ = _layer_fwd(xt, p, 1, nb, seq)
    loss, dx = _loss_grad(xt, target.reshape(nb * seq, d), "loss")

    def transport(g, names):
        out = []
        for k in names:
            if k == "w_in":
                out.append(g["w_in"].reshape(d, N_CHIPS, N_IN // N_CHIPS).transpose(1, 0, 2).astype(BF))
            elif k == "w_out":
                out.append(g["w_out"].reshape(N_CHIPS, -1, d))
            else:
                out.append(g[k])
        return out

    def chip_sums(g, l, names, tag):
        own = transport(g, names)
        sib = _to_sibling(own, l, f"grad{l}{tag}_to_sibling")
        return own, sib, [_add_pairs(a, b, f"grad{l}{tag}_chip_sum_{k}") for a, b, k in zip(own, sib, names)]

    dx, grads1, _, _ = _layer_bwd(dx, p, 1, saved1, nb, seq)
    own1 = transport(grads1, BIG)
    before = {}

    def after_ffn2(from_sibling):
        before["sib1"] = from_sibling
        sums1 = [_add_pairs(a, b, f"grad1_chip_sum_{k}") for a, b, k in zip(own1, from_sibling, BIG)]
        return _scatter_layer(sums1, 1, rider=True)

    def before_ffn1(g):
        before["own"], before["sib"], sums = chip_sums(g, 0, BIG[2:], "_rest")
        return _scatter_layer(sums, 0, rider=True)

    dx, grads0, chips1, chips0_rest = _layer_bwd(dx, p, 0, saved0, nb, seq, None, before_ffn1,
                                                 _to_sibling(own1, 1, rider=True), after_ffn2)
    sib1 = before["sib1"]
    own0, sib0, sums0 = chip_sums(grads0, 0, first, "_first")
    chips0 = _scatter_layer(sums0, 0, "grad0_first_to_chips") + chips0_rest
    own0, sib0 = own0 + before["own"], sib0 + before["sib"]
    grads = [grads0, grads1]
    dx = dx.reshape(nb, seq, d)

    mine = lambda a0, a1: jnp.where(ci == 0, a0, a1)
    at_me = lambda a: lax.dynamic_index_in_dim(a, me, 0, keepdims=False)
    reduced = [_final_sum(mine(at_me(own0[i]), at_me(own1[i])), mine(at_me(sib0[i]), at_me(sib1[i])),
                          mine(chips0[i], chips1[i]), f"grad_final_sum_{k}") for i, k in enumerate(BIG)]
    from_sib_final = _sibling_swap(reduced, "grad_swap_layers")
    full = {k: jnp.stack([jnp.where(ci == 0, a, b), jnp.where(ci == 0, b, a)])
            for k, a, b in zip(BIG, reduced, from_sib_final)}

    out_g, out_d, out_m, out_v = {}, {}, {}, {}
    for k in BIG:
        shp = w[k].shape
        two_d = lambda a: a.reshape(shp[0] * shp[1], shp[2])
        dl, mn, vn = _adamw(two_d(full[k]), two_d(w[k]), two_d(m[k]), two_d(v[k]), f"adamw_{k}")
        out_g[k], out_d[k], out_m[k], out_v[k] = full[k], dl.reshape(shp), mn.reshape(shp), vn.reshape(shp)

    small_local = [jnp.stack([grads[l][k] for l in range(depth)]) for k in SMALL] + [loss.reshape(1)]
    summed = _unpack(_small_all_reduce(_pack(small_local), "small_all_reduce"), [a.shape for a in small_local])
    total = summed.pop()[0]
    sg = dict(zip(SMALL, summed))
    cs = w["gdn_conv"].shape[-1]
    sg["gdn_conv"] = lax.dynamic_slice_in_dim(sg["gdn_conv"], me * cs, cs, axis=2)
    shapes = [w[k].shape for k in SMALL]
    packs = [_pack([src[k] for k in SMALL]) for src in (sg, w, m, v)]
    dl, mn, vn = _adamw(*packs, "adamw_small")
    for k, a, b, c2 in zip(SMALL, _unpack(dl, shapes), _unpack(mn, shapes), _unpack(vn, shapes)):
        out_g[k], out_d[k], out_m[k], out_v[k] = sg[k], a, b, c2

    return (total, dx, *[out_g[k] for k in WEIGHTS], *[out_d[k] for k in WEIGHTS],
            *[out_m[k] for k in WEIGHTS], *[out_v[k] for k in WEIGHTS])


def kernel(x, ffn1_norm, ffn1_w_in, ffn1_w_out, mix_norm, w_in, fox_q_norm, fox_k_norm, fox_f_bias, gdn_conv, gdn_a_log, gdn_dt_bias, gdn_out_norm, w_out, ffn2_norm, ffn2_w_in, ffn2_w_out, loss_target, m_ffn1_norm, m_ffn1_w_in, m_ffn1_w_out, m_mix_norm, m_w_in, m_fox_q_norm, m_fox_k_norm, m_fox_f_bias, m_gdn_conv, m_gdn_a_log, m_gdn_dt_bias, m_gdn_out_norm, m_w_out, m_ffn2_norm, m_ffn2_w_in, m_ffn2_w_out, v_ffn1_norm, v_ffn1_w_in, v_ffn1_w_out, v_mix_norm, v_w_in, v_fox_q_norm, v_fox_k_norm, v_fox_f_bias, v_gdn_conv, v_gdn_a_log, v_gdn_dt_bias, v_gdn_out_norm, v_w_out, v_ffn2_norm, v_ffn2_w_in, v_ffn2_w_out):
    w = dict(ffn1_norm=ffn1_norm, ffn1_w_in=ffn1_w_in, ffn1_w_out=ffn1_w_out, mix_norm=mix_norm, w_in=w_in,
             fox_q_norm=fox_q_norm, fox_k_norm=fox_k_norm, fox_f_bias=fox_f_bias, gdn_conv=gdn_conv,
             gdn_a_log=gdn_a_log, gdn_dt_bias=gdn_dt_bias, gdn_out_norm=gdn_out_norm, w_out=w_out,
             ffn2_norm=ffn2_norm, ffn2_w_in=ffn2_w_in, ffn2_w_out=ffn2_w_out)
    m = dict(ffn1_norm=m_ffn1_norm, ffn1_w_in=m_ffn1_w_in, ffn1_w_out=m_ffn1_w_out, mix_norm=m_mix_norm, w_in=m_w_in,
             fox_q_norm=m_fox_q_norm, fox_k_norm=m_fox_k_norm, fox_f_bias=m_fox_f_bias, gdn_conv=m_gdn_conv,
             gdn_a_log=m_gdn_a_log, gdn_dt_bias=m_gdn_dt_bias, gdn_out_norm=m_gdn_out_norm, w_out=m_w_out,
             ffn2_norm=m_ffn2_norm, ffn2_w_in=m_ffn2_w_in, ffn2_w_out=m_ffn2_w_out)
    v = dict(ffn1_norm=v_ffn1_norm, ffn1_w_in=v_ffn1_w_in, ffn1_w_out=v_ffn1_w_out, mix_norm=v_mix_norm, w_in=v_w_in,
             fox_q_norm=v_fox_q_norm, fox_k_norm=v_fox_k_norm, fox_f_bias=v_fox_f_bias, gdn_conv=v_gdn_conv,
             gdn_a_log=v_gdn_a_log, gdn_dt_bias=v_gdn_dt_bias, gdn_out_norm=v_gdn_out_norm, w_out=v_w_out,
             ffn2_norm=v_ffn2_norm, ffn2_w_in=v_ffn2_w_in, ffn2_w_out=v_ffn2_w_out)
    return _step(x, loss_target, w, m, v)
```

```python
import jax
import jax.numpy as jnp
from jax import lax
from jax.experimental import pallas as pl
from jax.experimental.pallas import tpu as pltpu

F32 = jnp.float32
BF = jnp.bfloat16
HI = lax.Precision.HIGHEST
MESH = pl.DeviceIdType.MESH

DEPTH = 2
FOX_HEADS = 8
FOX_HEAD_DIM = 64
FOX_WIDTH = 512
GDN_HEADS = 4
GDN_HEAD_DIM = 128
GDN_WIDTH = 512
CONV_WIDTH = 4
CHUNK = 64
EPS = 1e-6
N_IN = 3600
N_PAD = 3712
GATE_COL = 3584
LANES = 128
NEG = -1e30

ADAM_LR = 0.001
ADAM_B1 = 0.9
ADAM_B2 = 0.999
ADAM_EPS = 1e-08
ADAM_WD = 0.01
ADAM_STEP = 10

VMEM_LIMIT = 56 * 1024 * 1024

TOKEN_TILE = 512
TOKEN_TILE_BWD = 256
WGRAD_TOKENS = 512
FOX_PREP_TILE = 512
GDN_PREP_TILE = 256
ATTN_FWD_TILE = 2048
ATTN_BWD_TILE = 1024
SUM_ROWS = 512
ADAM_ROWS = 256


def _params(sem=None, **kw):
    return pltpu.CompilerParams(dimension_semantics=sem, vmem_limit_bytes=VMEM_LIMIT, **kw)


def _dot(a, b, precision=None):
    return jnp.dot(a, b, preferred_element_type=F32, precision=precision)


def _dot_nt(a, b, precision=None):
    return lax.dot_general(a, b, (((1,), (1,)), ((), ())), preferred_element_type=F32, precision=precision)


def _dot_tn(a, b, precision=None):
    return lax.dot_general(a, b, (((0,), (0,)), ((), ())), preferred_element_type=F32, precision=precision)


def _sigmoid(x):
    return 0.5 * jnp.tanh(0.5 * x) + 0.5


def _softplus(x):
    return jnp.maximum(x, 0.0) + jnp.log(1.0 + jnp.exp(-jnp.abs(x)))


def _log_sigmoid(x):
    return jnp.minimum(x, 0.0) - jnp.log(1.0 + jnp.exp(-jnp.abs(x)))


def _tile(n, t):
    t = min(n, t)
    assert n % t == 0, (n, t)
    return t


def _rms_fwd(x, gain):
    rstd = lax.rsqrt(jnp.mean(x * x, axis=-1, keepdims=True) + EPS)
    xhat = x * rstd
    return xhat * gain, xhat, rstd


def _rms_bwd(dy, xhat, rstd, gain):
    dxhat = dy * gain
    dx = rstd * (dxhat - xhat * jnp.mean(dxhat * xhat, axis=-1, keepdims=True))
    return dx, dy * xhat


def _full(shape):
    nd = len(shape)
    return pl.BlockSpec(shape, lambda *_: (0,) * nd)


HBM = pl.BlockSpec(memory_space=pltpu.HBM)


def _load_ffn_weights(win_hbm, wout_hbm, win_v, wout_v, sem):
    fr = wout_hbm.shape[1]
    copies = [pltpu.make_async_copy(win_hbm.at[s], win_v.at[s], sem.at[s]) for s in range(4)]
    copies += [pltpu.make_async_copy(wout_hbm.at[s], wout_v.at[pl.ds(s * fr, fr)], sem.at[4 + s])
               for s in range(4)]
    for c in copies:
        c.start()
    for c in copies:
        c.wait()


def _ffn_fwd(x, gain, win_g, wout_g, name, rider=None):
    t, d = x.shape
    _, _, fs = win_g.shape
    fr = wout_g.shape[1]
    tm = _tile(t, TOKEN_TILE)
    r_in, r_out, r_sem = _rider_parts(rider)
    steps = t // tm

    def body(x_ref, g_ref, win_hbm, wout_hbm, *rest):
        rin, (xo_ref, h_ref) = rest[:len(r_in)], rest[len(r_in):len(r_in) + 2]
        rout = rest[len(r_in) + 2:len(r_in) + 2 + len(r_out)]
        win_v, wout_v, sem = rest[len(r_in) + 2 + len(r_out):len(r_in) + 5 + len(r_out)]
        riding = (rin, rout, rest[len(r_in) + 5 + len(r_out):])
        step = pl.program_id(0)
        _ride(rider, 0, step == 0, riding)
        _ride(rider, 1, step == (3 * steps) // 4, riding)

        @pl.when(step == 0)
        def _():
            _load_ffn_weights(win_hbm, wout_hbm, win_v, wout_v, sem)

        xv = x_ref[...]
        hn, _, _ = _rms_fwd(xv, g_ref[...])
        hn = hn.astype(BF)
        acc = jnp.zeros((tm, d), F32)
        for s in range(2):
            g = _dot(hn, win_v[s])
            u = _dot(hn, win_v[s + 2])
            h_ref[:, s * fs:(s + 1) * fs] = g.astype(BF)
            h_ref[:, (s + 2) * fs:(s + 3) * fs] = u.astype(BF)
            a = (g * _sigmoid(g) * u).astype(BF)
            acc = acc + _dot(a, wout_v[s * fs:(s + 1) * fs, :])
        xo_ref[...] = xv + 0.5 * acc
        _ride(rider, 2, step == steps - 1, riding)

    return pl.pallas_call(
        body, name=name, grid=(steps,),
        in_specs=[pl.BlockSpec((tm, d), lambda i: (i, 0)), _full((1, d)), HBM, HBM] + [HBM] * len(r_in),
        out_specs=[pl.BlockSpec((tm, d), lambda i: (i, 0)), pl.BlockSpec((tm, 4 * fs), lambda i: (i, 0))]
        + [HBM] * len(r_out),
        out_shape=[jax.ShapeDtypeStruct((t, d), F32), jax.ShapeDtypeStruct((t, 4 * fs), BF)] + r_out,
        scratch_shapes=[pltpu.VMEM((4, d, fs), BF), pltpu.VMEM((4 * fr, d), BF), pltpu.SemaphoreType.DMA((8,))]
        + r_sem,
        compiler_params=_params(("arbitrary",), has_side_effects=rider is not None),
    )(x, gain, win_g, wout_g, *r_in)


def _ffn_bwd(dy, x, h, gain, win_g, wout_g, name, rider=None):
    t, d = x.shape
    _, _, fs = win_g.shape
    fr = wout_g.shape[1]
    tm = _tile(t, TOKEN_TILE_BWD)
    r_in, r_out, r_sem = _rider_parts(rider)
    steps = t // tm

    def body(dy_ref, x_ref, h_ref, g_ref, win_hbm, wout_hbm, *rest):
        rin, (dx_ref, dh_ref, a_ref, hn_ref, dyh_ref, dg_ref) = rest[:len(r_in)], rest[len(r_in):len(r_in) + 6]
        rout = rest[len(r_in) + 6:len(r_in) + 6 + len(r_out)]
        win_v, wout_v, sem = rest[len(r_in) + 6 + len(r_out):len(r_in) + 9 + len(r_out)]
        riding = (rin, rout, rest[len(r_in) + 9 + len(r_out):])
        step = pl.program_id(0)
        _ride(rider, 0, step == 0, riding)
        _ride(rider, 1, step == (3 * steps) // 4, riding)

        @pl.when(step == 0)
        def _():
            _load_ffn_weights(win_hbm, wout_hbm, win_v, wout_v, sem)
            dg_ref[...] = jnp.zeros_like(dg_ref)

        dyv = dy_ref[...]
        dyh = (0.5 * dyv).astype(BF)
        dyh_ref[...] = dyh
        dhn = jnp.zeros((tm, d), F32)
        for s in range(2):
            da = _dot_nt(dyh, wout_v[s * fs:(s + 1) * fs, :])
            g = h_ref[:, s * fs:(s + 1) * fs].astype(F32)
            u = h_ref[:, (s + 2) * fs:(s + 3) * fs].astype(F32)
            sg = _sigmoid(g)
            si = g * sg
            a_ref[:, s * fs:(s + 1) * fs] = (si * u).astype(BF)
            dgate = (da * u * (sg * (1.0 + g * (1.0 - sg)))).astype(BF)
            dup = (da * si).astype(BF)
            dh_ref[:, s * fs:(s + 1) * fs] = dgate
            dh_ref[:, (s + 2) * fs:(s + 3) * fs] = dup
            dhn = dhn + _dot_nt(dgate, win_v[s]) + _dot_nt(dup, win_v[s + 2])
        xv = x_ref[...]
        gain_v = g_ref[...]
        hn, xhat, rstd = _rms_fwd(xv, gain_v)
        hn_ref[...] = hn.astype(BF)
        dx, dgr = _rms_bwd(dhn, xhat, rstd, gain_v)
        dx_ref[...] = dyv + dx
        dg_ref[...] += jnp.sum(dgr, axis=0, keepdims=True)
        _ride(rider, 2, step == steps - 1, riding)

    row = lambda w: pl.BlockSpec((tm, w), lambda i: (i, 0))
    return pl.pallas_call(
        body, name=name, grid=(steps,),
        in_specs=[row(d), row(d), row(4 * fs), _full((1, d)), HBM, HBM] + [HBM] * len(r_in),
        out_specs=[row(d), row(4 * fs), row(2 * fs), row(d), row(d), _full((1, d))] + [HBM] * len(r_out),
        out_shape=[jax.ShapeDtypeStruct((t, d), F32), jax.ShapeDtypeStruct((t, 4 * fs), BF),
                   jax.ShapeDtypeStruct((t, 2 * fs), BF), jax.ShapeDtypeStruct((t, d), BF),
                   jax.ShapeDtypeStruct((t, d), BF), jax.ShapeDtypeStruct((1, d), F32)] + r_out,
        scratch_shapes=[pltpu.VMEM((4, d, fs), BF), pltpu.VMEM((4 * fr, d), BF), pltpu.SemaphoreType.DMA((8,))]
        + r_sem,
        compiler_params=_params(("arbitrary",), has_side_effects=rider is not None),
    )(dy, x, h, gain, win_g, wout_g, *r_in)


def _wgrad(a, b, out_shape, out_spec, tm, tn, name, tk=WGRAD_TOKENS):
    t, m = a.shape
    _, n = b.shape
    tk = _tile(t, tk)
    nk = t // tk

    def body(a_ref, b_ref, o_ref, acc):
        k = pl.program_id(2)

        @pl.when(k == 0)
        def _():
            acc[...] = jnp.zeros_like(acc)

        acc[...] += _dot_tn(a_ref[...], b_ref[...])

        @pl.when(k == nk - 1)
        def _():
            o_ref[...] = acc[...].astype(o_ref.dtype)

    return pl.pallas_call(
        body, name=name, grid=(m // tm, n // tn, nk),
        in_specs=[pl.BlockSpec((tk, tm), lambda i, j, k: (k, i)), pl.BlockSpec((tk, tn), lambda i, j, k: (k, j))],
        out_specs=out_spec, out_shape=out_shape,
        scratch_shapes=[pltpu.VMEM((tm, tn), F32)],
        compiler_params=_params(("parallel", "parallel", "arbitrary")),
    )(a, b)


def _wgrad_parts(a, parts, tm, name, tk=WGRAD_TOKENS):
    t, m = a.shape
    widths = [p.shape[1] for p in parts]
    n = sum(widths)
    tk = _tile(t, tk)
    nk = t // tk
    np_ = len(parts)

    def body(a_ref, *rest):
        b_refs, o_ref, acc = rest[:np_], rest[np_], rest[np_ + 1]
        k = pl.program_id(1)

        @pl.when(k == 0)
        def _():
            acc[...] = jnp.zeros_like(acc)

        av, off = a_ref[...], 0
        for b_ref, wd in zip(b_refs, widths):
            acc[:, off:off + wd] += _dot_tn(av, b_ref[...])
            off += wd

        @pl.when(k == nk - 1)
        def _():
            o_ref[...] = acc[...]

    return pl.pallas_call(
        body, name=name, grid=(m // tm, nk),
        in_specs=[pl.BlockSpec((tk, tm), lambda i, k: (k, i))]
        + [pl.BlockSpec((tk, wd), lambda i, k: (k, 0)) for wd in widths],
        out_specs=pl.BlockSpec((tm, n), lambda i, k: (i, 0)), out_shape=jax.ShapeDtypeStruct((m, n), F32),
        scratch_shapes=[pltpu.VMEM((tm, n), F32)],
        compiler_params=_params(("parallel", "arbitrary")),
    )(a, *parts)


def _norm_matmul(x, gain, w, name):
    t, d = x.shape
    n = w.shape[1]
    tm = _tile(t, TOKEN_TILE)

    def body(x_ref, g_ref, w_ref, o_ref):
        hn, _, _ = _rms_fwd(x_ref[...], g_ref[...])
        o_ref[...] = _dot(hn.astype(BF), w_ref[...])

    return pl.pallas_call(
        body, name=name, grid=(t // tm,),
        in_specs=[pl.BlockSpec((tm, d), lambda i: (i, 0)), _full((1, d)), _full((d, n))],
        out_specs=pl.BlockSpec((tm, n), lambda i: (i, 0)),
        out_shape=jax.ShapeDtypeStruct((t, n), F32),
        compiler_params=_params(("parallel",)),
    )(x, gain, w)


def _norm_matmul_bwd(dres, dparts, x, gain, w, name):
    t, d = x.shape
    n = w.shape[1]
    tm = _tile(t, TOKEN_TILE_BWD)
    widths = [a.shape[1] for a in dparts]
    assert sum(widths) == n
    k = len(dparts)

    def body(dr_ref, *rest):
        dp_refs, (x_ref, g_ref, w_ref, dx_ref, hn_ref, dg_ref) = rest[:k], rest[k:]

        @pl.when(pl.program_id(0) == 0)
        def _():
            dg_ref[...] = jnp.zeros_like(dg_ref)

        dhn, off = jnp.zeros((tm, d), F32), 0
        for dp_ref, wd in zip(dp_refs, widths):
            dhn = dhn + _dot_nt(dp_ref[...], w_ref[:, off:off + wd])
            off += wd
        gain_v = g_ref[...]
        hn, xhat, rstd = _rms_fwd(x_ref[...], gain_v)
        hn_ref[...] = hn.astype(BF)
        dx, dgr = _rms_bwd(dhn, xhat, rstd, gain_v)
        dx_ref[...] = dr_ref[...] + dx
        dg_ref[...] += jnp.sum(dgr, axis=0, keepdims=True)

    row = lambda wd: pl.BlockSpec((tm, wd), lambda i: (i, 0))
    return pl.pallas_call(
        body, name=name, grid=(t // tm,),
        in_specs=[row(d)] + [row(wd) for wd in widths] + [row(d), _full((1, d)), _full((d, n))],
        out_specs=[row(d), row(d), _full((1, d))],
        out_shape=[jax.ShapeDtypeStruct((t, d), F32), jax.ShapeDtypeStruct((t, d), BF),
                   jax.ShapeDtypeStruct((1, d), F32)],
        compiler_params=_params(("arbitrary",)),
    )(dres, *dparts, x, gain, w)


def _mix_out(x, yf, yg, w, name):
    t, d = x.shape
    kf = yf.shape[1]
    tm = _tile(t, TOKEN_TILE)

    def body(x_ref, yf_ref, yg_ref, w_ref, o_ref):
        o_ref[...] = x_ref[...] + _dot(yf_ref[...], w_ref[0:kf, :]) + _dot(yg_ref[...], w_ref[kf:2 * kf, :])

    row = lambda wd: pl.BlockSpec((tm, wd), lambda i: (i, 0))
    return pl.pallas_call(
        body, name=name, grid=(t // tm,),
        in_specs=[row(d), row(kf), row(kf), _full((2 * kf, d))],
        out_specs=row(d), out_shape=jax.ShapeDtypeStruct((t, d), F32),
        compiler_params=_params(("parallel",)),
    )(x, yf, yg, w)


def _mix_out_bwd(dx, w, name):
    t, d = dx.shape
    kf = w.shape[0] // 2
    tm = _tile(t, TOKEN_TILE)

    def body(dx_ref, w_ref, df_ref, dg_ref, dxb_ref):
        dxb = dx_ref[...].astype(BF)
        dxb_ref[...] = dxb
        df_ref[...] = _dot_nt(dxb, w_ref[0:kf, :]).astype(BF)
        dg_ref[...] = _dot_nt(dxb, w_ref[kf:2 * kf, :]).astype(BF)

    row = lambda wd: pl.BlockSpec((tm, wd), lambda i: (i, 0))
    return pl.pallas_call(
        body, name=name, grid=(t // tm,),
        in_specs=[row(d), _full((2 * kf, d))],
        out_specs=[row(kf), row(kf), row(d)],
        out_shape=[jax.ShapeDtypeStruct((t, kf), BF), jax.ShapeDtypeStruct((t, kf), BF),
                   jax.ShapeDtypeStruct((t, d), BF)],
        compiler_params=_params(("parallel",)),
    )(dx, w)


def _loss_grad(y, target, name):
    t, d = y.shape
    tm = _tile(t, TOKEN_TILE)

    def body(y_ref, t_ref, l_ref, dy_ref):
        @pl.when(pl.program_id(0) == 0)
        def _():
            l_ref[...] = jnp.zeros_like(l_ref)

        diff = y_ref[...] - t_ref[...]
        dy_ref[...] = diff * (1.0 / d)
        part = jnp.sum(jnp.sum(diff * diff, axis=1, keepdims=True), axis=0, keepdims=True)
        l_ref[...] += part * (0.5 / d)

    row = pl.BlockSpec((tm, d), lambda i: (i, 0))
    return pl.pallas_call(
        body, name=name, grid=(t // tm,),
        in_specs=[row, row], out_specs=[_full((1, 1)), row],
        out_shape=[jax.ShapeDtypeStruct((1, 1), F32), jax.ShapeDtypeStruct((t, d), F32)],
        compiler_params=_params(("arbitrary",)),
    )(y, target)


def _head_sum_matrix(width, head):
    r = lax.broadcasted_iota(jnp.int32, (width, width), 0) // head
    c = lax.broadcasted_iota(jnp.int32, (width, width), 1) // head
    return (r == c).astype(BF)


def _head_mean(x, bd):
    return _dot(x.astype(BF), bd) * (1.0 / FOX_HEAD_DIM)


def _mask_dot(mask01, x):
    mb = mask01.astype(BF)
    hi = x.astype(BF)
    r1 = x - hi.astype(F32)
    mid = r1.astype(BF)
    lo = (r1 - mid.astype(F32)).astype(BF)
    return _dot(mb, hi) + _dot(mb, mid) + _dot(mb, lo)


def _fox_prep(proj, wq_t, wk_t, bias_pad, seq, name):
    t = proj.shape[0]
    ts = _tile(seq, FOX_PREP_TILE)
    tpe = seq // ts
    scale = FOX_HEAD_DIM ** -0.5

    def body(q_ref, k_ref, v_ref, gt_ref, wq_ref, wk_ref, b_ref, qo_ref, ko_ref, vo_ref, cum_ref, carry):
        i = pl.program_id(0)
        bd = _head_sum_matrix(FOX_WIDTH, FOX_HEAD_DIM)

        def norm(xv, wv):
            ms = _head_mean(xv * xv, bd)
            return xv * lax.rsqrt(ms + EPS) * wv

        qo_ref[...] = (norm(q_ref[...], wq_ref[...]) * scale).astype(BF)
        ko_ref[...] = norm(k_ref[...], wk_ref[...]).astype(BF)
        vo_ref[...] = v_ref[...].astype(BF)

        @pl.when(i % tpe == 0)
        def _():
            carry[...] = jnp.zeros_like(carry)

        ls = _log_sigmoid(gt_ref[...] + b_ref[...])
        r = lax.broadcasted_iota(jnp.int32, (ts, ts), 0)
        c = lax.broadcasted_iota(jnp.int32, (ts, ts), 1)
        cum = _mask_dot(r >= c, ls) + carry[...]
        cum_ref[...] = cum
        carry[...] = cum[ts - 1:ts, :]

    blk = lambda j: pl.BlockSpec((ts, FOX_WIDTH), lambda i: (i, j))
    gate = pl.BlockSpec((ts, LANES), lambda i: (i, GATE_COL // LANES))
    out = pl.BlockSpec((ts, FOX_WIDTH), lambda i: (i, 0))
    return pl.pallas_call(
        body, name=name, grid=(t // ts,),
        in_specs=[blk(0), blk(1), blk(2), gate, _full((1, FOX_WIDTH)), _full((1, FOX_WIDTH)), _full((1, LANES))],
        out_specs=[out, out, out, pl.BlockSpec((ts, LANES), lambda i: (i, 0))],
        out_shape=[jax.ShapeDtypeStruct((t, FOX_WIDTH), BF)] * 3 + [jax.ShapeDtypeStruct((t, LANES), F32)],
        scratch_shapes=[pltpu.VMEM((1, LANES), F32)],
        compiler_params=_params(("arbitrary",)),
    )(proj, proj, proj, proj, wq_t, wk_t, bias_pad)


def _pick_head_sums(x):
    r = lax.broadcasted_iota(jnp.int32, (FOX_WIDTH, LANES), 0)
    c = lax.broadcasted_iota(jnp.int32, (FOX_WIDTH, LANES), 1)
    sel = (((r % LANES == FOX_HEAD_DIM) & (c == 2 * (r // LANES)))
           | ((r % LANES == 0) & (c == 2 * (r // LANES) + 1))).astype(BF)
    hi = x.astype(BF)
    r1 = x - hi.astype(F32)
    mid = r1.astype(BF)
    lo = (r1 - mid.astype(F32)).astype(BF)
    return _dot(hi, sel) + _dot(mid, sel) + _dot(lo, sel)


def _fox_prep_bwd(proj, dqa, dqb, dk, dv, dkx, wq_t, wk_t, bias_pad, seq, name):
    t = proj.shape[0]
    ts = _tile(seq, FOX_PREP_TILE)
    tpe = seq // ts
    nt = t // ts
    scale = FOX_HEAD_DIM ** -0.5

    def body(q_ref, k_ref, gt_ref, dqa_ref, dqb_ref, dk_ref, dv_ref, dc_ref, wq_ref, wk_ref, b_ref,
             dp_ref, dff_ref, dwq_ref, dwk_ref, db_ref, carry):
        i = pl.program_id(0)
        first = (lax.broadcasted_iota(jnp.int32, (ts, FOX_WIDTH), 1) % LANES) < FOX_HEAD_DIM
        dq_all = jnp.where(first, dqa_ref[...], dqb_ref[...])
        ti = nt - 1 - i
        bd = _head_sum_matrix(FOX_WIDTH, FOX_HEAD_DIM)

        @pl.when(i == 0)
        def _():
            dwq_ref[...] = jnp.zeros_like(dwq_ref)
            dwk_ref[...] = jnp.zeros_like(dwk_ref)
            db_ref[...] = jnp.zeros_like(db_ref)

        def norm_bwd(xv, wv, dyv):
            ms = _head_mean(xv * xv, bd)
            rstd = lax.rsqrt(ms + EPS)
            xhat = xv * rstd
            dxhat = dyv * wv
            mean = _head_mean(dxhat * xhat, bd)
            return rstd * (dxhat - xhat * mean), jnp.sum(dyv * xhat, axis=0, keepdims=True)

        dxq, dwq = norm_bwd(q_ref[...], wq_ref[...], dq_all * scale)
        dxk, dwk = norm_bwd(k_ref[...], wk_ref[...], dk_ref[...])
        dp_ref[:, 0:FOX_WIDTH] = dxq.astype(BF)
        dp_ref[:, FOX_WIDTH:2 * FOX_WIDTH] = dxk.astype(BF)
        dp_ref[:, 2 * FOX_WIDTH:3 * FOX_WIDTH] = dv_ref[...].astype(BF)
        dwq_ref[...] += dwq
        dwk_ref[...] += dwk

        @pl.when(ti % tpe == tpe - 1)
        def _():
            carry[...] = jnp.zeros_like(carry)

        r = lax.broadcasted_iota(jnp.int32, (ts, ts), 0)
        c = lax.broadcasted_iota(jnp.int32, (ts, ts), 1)
        dcum = _pick_head_sums(jnp.where(first, dqb_ref[...], dqa_ref[...]) - dc_ref[...])
        dls = _mask_dot(c >= r, dcum) + carry[...]
        carry[...] = dls[0:1, :]
        z = gt_ref[...] + b_ref[...]
        lane = lax.broadcasted_iota(jnp.int32, (ts, LANES), 1)
        dff = jnp.where(lane < FOX_HEADS, dls * _sigmoid(-z), 0.0)
        dff_ref[...] = dff
        db_ref[...] += jnp.sum(dff, axis=0, keepdims=True)

        @pl.when(i == nt - 1)
        def _():
            fr = lax.broadcasted_iota(jnp.int32, (FOX_WIDTH, FOX_WIDTH), 0) % FOX_HEAD_DIM
            fc = lax.broadcasted_iota(jnp.int32, (FOX_WIDTH, FOX_WIDTH), 1) % FOX_HEAD_DIM
            fold = (fr == fc).astype(F32)
            dwq_ref[...] = _dot(dwq_ref[...], fold, HI)
            dwk_ref[...] = _dot(dwk_ref[...], fold, HI)

    rev = lambda w, j: pl.BlockSpec((ts, w), lambda i: (nt - 1 - i, j))
    return pl.pallas_call(
        body, name=name, grid=(nt,),
        in_specs=[rev(FOX_WIDTH, 0), rev(FOX_WIDTH, 1), rev(LANES, GATE_COL // LANES),
                  rev(FOX_WIDTH, 0), rev(FOX_WIDTH, 0), rev(FOX_WIDTH, 0), rev(FOX_WIDTH, 0), rev(FOX_WIDTH, 0),
                  _full((1, FOX_WIDTH)), _full((1, FOX_WIDTH)), _full((1, LANES))],
        out_specs=[rev(3 * FOX_WIDTH, 0), rev(LANES, 0), _full((1, FOX_WIDTH)), _full((1, FOX_WIDTH)),
                   _full((1, LANES))],
        out_shape=[jax.ShapeDtypeStruct((t, 3 * FOX_WIDTH), BF), jax.ShapeDtypeStruct((t, LANES), F32),
                   jax.ShapeDtypeStruct((1, FOX_WIDTH), F32), jax.ShapeDtypeStruct((1, FOX_WIDTH), F32),
                   jax.ShapeDtypeStruct((1, LANES), F32)],
        scratch_shapes=[pltpu.VMEM((1, LANES), F32)],
        compiler_params=_params(("arbitrary",)),
    )(proj, proj, proj, dqa, dqb, dk, dv, dkx, wq_t, wk_t, bias_pad)


class _Rider:
    def __init__(self, inputs, out_shapes, sems, phases):
        self.inputs, self.out_shapes, self.sems, self.phases = list(inputs), list(out_shapes), list(sems), phases


def _rider_parts(rider):
    if rider is None:
        return [], [], []
    return rider.inputs, rider.out_shapes, rider.sems


def _ride(rider, which, when, refs):
    if rider is not None:
        @pl.when(when)
        def _():
            rider.phases[which](*refs)


def _fox_attn(q, k, v, ck, nb, seq, name, rider=None):
    t = q.shape[0]
    tq = _tile(seq, ATTN_FWD_TILE)
    nq = seq // tq
    npair = FOX_HEADS // 2
    hd = FOX_HEAD_DIM
    r_in, r_out, r_sem = _rider_parts(rider)
    steps = nb * npair * nq

    def body(q_ref, k_ref, v_ref, ck_ref, *rest):
        rin, (o_ref, lse_ref) = rest[:len(r_in)], rest[len(r_in):len(r_in) + 2]
        rout = rest[len(r_in) + 2:len(r_in) + 2 + len(r_out)]
        m_s, acc_s = rest[len(r_in) + 2 + len(r_out):len(r_in) + 4 + len(r_out)]
        riding = (rin, rout, rest[len(r_in) + 4 + len(r_out):])
        step = (pl.program_id(0) * npair + pl.program_id(1)) * nq + pl.program_id(2)
        _ride(rider, 0, step == 0, riding)
        _ride(rider, 1, step == (3 * steps) // 4, riding)
        qi = pl.program_id(2)
        lane = lax.broadcasted_iota(jnp.int32, (tq, LANES), 1)
        m_s[...] = jnp.full(m_s.shape, NEG, F32)
        acc_s[...] = jnp.zeros_like(acc_s)
        qv = q_ref[...]

        def tile(kj, on_diagonal):
            cols = pl.ds(pl.multiple_of(kj * tq, tq), tq)
            kv = k_ref[cols, :]
            vv = v_ref[cols, :]
            if on_diagonal:
                causal = (lax.broadcasted_iota(jnp.int32, (tq, tq), 0)
                          >= lax.broadcasted_iota(jnp.int32, (tq, tq), 1))
            for hh in range(2):
                hm = (lane >= hd) if hh else (lane < hd)
                qh = jnp.where(hm, qv, jnp.zeros_like(qv))
                s = _dot_nt(qh, kv) - ck_ref[hh, :, cols]
                if on_diagonal:
                    s = jnp.where(causal, s, NEG)
                m_old = m_s[hh]
                m_new = jnp.maximum(m_old, jnp.max(s, axis=-1, keepdims=True))
                p = jnp.exp(s - m_new)
                alpha = jnp.exp(m_old - m_new)
                m_s[hh] = m_new
                acc_s[hh] = alpha * acc_s[hh] + _dot(p.astype(BF), jnp.where(hm, vv, jnp.ones_like(vv)))

        def off_diagonal(kj, carry):
            tile(kj, False)
            return carry

        lax.fori_loop(0, qi, off_diagonal, 0)
        tile(qi, True)
        a0 = acc_s[0]
        a1 = acc_s[1]
        den = jnp.where(lane < hd, pltpu.roll(a0, hd, axis=1), pltpu.roll(a1, hd, axis=1))
        o_ref[...] = (jnp.where(lane < hd, a0, a1) / den).astype(o_ref.dtype)
        l0 = jnp.sum(jnp.where(lane == hd, a0, 0.0), axis=1, keepdims=True)
        l1 = jnp.sum(jnp.where(lane == 0, a1, 0.0), axis=1, keepdims=True)
        lse_ref[0] = m_s[0] + jnp.log(l0)
        lse_ref[1] = m_s[1] + jnp.log(l1)
        _ride(rider, 2, step == steps - 1, riding)

    qspec = pl.BlockSpec((tq, LANES), lambda b, p, i: (b * nq + i, p))
    kspec = pl.BlockSpec((seq, LANES), lambda b, p, i: (b, p))
    colspec = pl.BlockSpec((None, 2, tq, 1), lambda b, p, i: (b * npair + p, 0, i, 0))
    rowspec = pl.BlockSpec((None, 2, 1, seq), lambda b, p, i: (b * npair + p, 0, 0, 0))
    sem = ("arbitrary",) * 3 if rider else ("parallel",) * 3
    return pl.pallas_call(
        body, name=name, grid=(nb, npair, nq),
        in_specs=[qspec, kspec, kspec, rowspec] + [HBM] * len(r_in),
        out_specs=[qspec, colspec] + [HBM] * len(r_out),
        out_shape=[jax.ShapeDtypeStruct((t, FOX_WIDTH), BF), jax.ShapeDtypeStruct((nb * npair, 2, seq, 1), F32)]
        + r_out,
        scratch_shapes=[pltpu.VMEM((2, tq, 1), F32), pltpu.VMEM((2, tq, LANES), F32)] + r_sem,
        compiler_params=_params(sem, has_side_effects=rider is not None),
    )(q, k, v, ck, *r_in)


def _fox_attn_bwd(q, k, v, o, do, lse, ck, nb, seq, name, rider=None):
    t = q.shape[0]
    tq = _tile(seq, ATTN_BWD_TILE)
    nq = seq // tq
    npair = FOX_HEADS // 2
    hd = FOX_HEAD_DIM
    r_in, r_out, r_sem = _rider_parts(rider)
    steps = nb * npair * nq

    def body(q_ref, k_ref, v_ref, o_ref, do_ref, lse_ref, ck_ref, *rest):
        rin, (dqa_ref, dqb_ref, dk_ref, dv_ref, dkx_ref) = rest[:len(r_in)], rest[len(r_in):len(r_in) + 5]
        rout = rest[len(r_in) + 5:len(r_in) + 5 + len(r_out)]
        dk_s, dv_s = rest[len(r_in) + 5 + len(r_out):len(r_in) + 7 + len(r_out)]
        riding = (rin, rout, rest[len(r_in) + 7 + len(r_out):])
        step = (pl.program_id(0) * npair + pl.program_id(1)) * nq + pl.program_id(2)
        _ride(rider, 0, step == 0, riding)
        _ride(rider, 1, step == (3 * steps) // 4, riding)
        kj = pl.program_id(2)
        lane = lax.broadcasted_iota(jnp.int32, (tq, LANES), 1)

        @pl.when(kj == 0)
        def _():
            dqa_ref[...] = jnp.zeros_like(dqa_ref)
            dqb_ref[...] = jnp.zeros_like(dqb_ref)

        dk_s[...] = jnp.zeros_like(dk_s)
        dv_s[...] = jnp.zeros_like(dv_s)
        kv = k_ref[...]
        vv = v_ref[...]

        def tile(qi, on_diagonal):
            rows = pl.ds(pl.multiple_of(qi * tq, tq), tq)
            qv = q_ref[rows, :]
            dov = do_ref[rows, :]
            prod = dov.astype(F32) * o_ref[rows, :].astype(F32)
            if on_diagonal:
                causal = (lax.broadcasted_iota(jnp.int32, (tq, tq), 0)
                          >= lax.broadcasted_iota(jnp.int32, (tq, tq), 1))
            for hh, dq_ref in ((0, dqa_ref), (1, dqb_ref)):
                hm = (lane >= hd) if hh else (lane < hd)
                zero = jnp.zeros_like(qv)
                one = jnp.ones_like(qv)
                doh = jnp.where(hm, dov, zero)
                delta = jnp.sum(jnp.where(hm, prod, 0.0), axis=-1, keepdims=True)
                s = _dot_nt(jnp.where(hm, qv, zero), kv) - ck_ref[hh]
                if on_diagonal:
                    s = jnp.where(causal, s, NEG)
                p = jnp.exp(s - lse_ref[hh, rows, :])
                dp = _dot_nt(doh, vv)
                dsb = (p * (dp - delta)).astype(BF)
                dv_s[...] += _dot_tn(p.astype(BF), doh)
                dk_s[hh] += _dot_tn(dsb, jnp.where(hm, qv, one))
                dq_ref[rows, :] += _dot(dsb, jnp.where(hm, kv, one))

        def off_diagonal(qi, carry):
            tile(qi, False)
            return carry

        tile(kj, True)
        lax.fori_loop(kj + 1, nq, off_diagonal, 0)
        dk_ref[...] = jnp.where(lane < hd, dk_s[0], dk_s[1])
        dkx_ref[...] = jnp.where(lane < hd, dk_s[1], dk_s[0])
        dv_ref[...] = dv_s[...]
        _ride(rider, 2, step == steps - 1, riding)

    kspec = pl.BlockSpec((tq, LANES), lambda b, p, j: (b * nq + j, p))
    full_q = pl.BlockSpec((seq, LANES), lambda b, p, j: (b, p))
    colspec = pl.BlockSpec((None, 2, seq, 1), lambda b, p, j: (b * npair + p, 0, 0, 0))
    rowspec = pl.BlockSpec((None, 2, 1, tq), lambda b, p, j: (b * npair + p, 0, 0, j))
    sem = ("arbitrary",) * 3 if rider else ("parallel", "parallel", "arbitrary")
    return pl.pallas_call(
        body, name=name, grid=(nb, npair, nq),
        in_specs=[full_q, kspec, kspec, full_q, full_q, colspec, rowspec] + [HBM] * len(r_in),
        out_specs=[full_q, full_q, kspec, kspec, kspec] + [HBM] * len(r_out),
        out_shape=[jax.ShapeDtypeStruct((t, FOX_WIDTH), F32)] * 5 + r_out,
        scratch_shapes=[pltpu.VMEM((2, tq, LANES), F32), pltpu.VMEM((tq, LANES), F32)] + r_sem,
        compiler_params=_params(sem, has_side_effects=rider is not None),
    )(q, k, v, o, do, lse, ck, *r_in)


GDN_QKV = 3 * GDN_WIDTH
GDN_COL = 3 * FOX_WIDTH
GG_COL = GDN_COL + GDN_QKV
A_LANE = FOX_HEADS
B_LANE = FOX_HEADS + GDN_HEADS
HALO = 8


def _gate_lanes(ts):
    lane = lax.broadcasted_iota(jnp.int32, (ts, LANES), 1)
    return (lane >= A_LANE) & (lane < B_LANE), (lane >= B_LANE) & (lane < B_LANE + GDN_HEADS)


def _chunk_tri(ts, upper):
    r = lax.broadcasted_iota(jnp.int32, (ts, ts), 0)
    c = lax.broadcasted_iota(jnp.int32, (ts, ts), 1)
    same = (r // CHUNK) == (c // CHUNK)
    return (same & ((c >= r) if upper else (r >= c))).astype(F32)


def _shift_rows(x, edge, k, down):
    ts = x.shape[0]
    row = lax.broadcasted_iota(jnp.int32, (HALO, x.shape[1]), 0)
    if down:
        rolled = pltpu.roll(x, k, axis=0)
        patch = jnp.where(row < k, pltpu.roll(edge, k, axis=0), rolled[:HALO])
        return jnp.concatenate([patch, rolled[HALO:]], axis=0)
    rolled = pltpu.roll(x, ts - k, axis=0)
    patch = jnp.where(row >= HALO - k, pltpu.roll(edge, HALO - k, axis=0), rolled[ts - HALO:])
    return jnp.concatenate([rolled[:ts - HALO], patch], axis=0)


def _conv_silu(x, before, w):
    taps = [_shift_rows(x, before, CONV_WIDTH - 1 - kk, True) for kk in range(CONV_WIDTH - 1)] + [x]
    c = w[0:1, :] * taps[0]
    for kk in range(1, CONV_WIDTH):
        c = c + w[kk:kk + 1, :] * taps[kk]
    return taps, c, c * _sigmoid(c)


def _gdn_prep(proj, conv_w, a_pad, dt_pad, seq, name):
    t = proj.shape[0]
    ts = _tile(seq, GDN_PREP_TILE)
    tpe = seq // ts
    qscale = GDN_HEAD_DIM ** -0.5

    def body(x_ref, gt_ref, w_ref, a_ref, dt_ref, qo_ref, ko_ref, vo_ref, go_ref, tail):
        i = pl.program_id(0)
        xv = x_ref[...]
        before = jnp.where(i % tpe == 0, jnp.zeros((HALO, GDN_QKV), F32), tail[...])
        tail[...] = xv[ts - HALO:]
        _, _, s = _conv_silu(xv, before, w_ref[...])
        for h in range(GDN_HEADS):
            for base, ref, sc in ((0, qo_ref, qscale), (GDN_WIDTH, ko_ref, 1.0)):
                xh = s[:, base + h * LANES: base + (h + 1) * LANES]
                r = lax.rsqrt(jnp.sum(xh * xh, axis=-1, keepdims=True) + EPS)
                ref[:, h * LANES:(h + 1) * LANES] = (xh * (r * sc)).astype(BF)
        vo_ref[...] = s[:, 2 * GDN_WIDTH:].astype(BF)
        gate = gt_ref[...]
        g_raw = -jnp.exp(a_ref[...]) * _softplus(gate + dt_ref[...])
        gc = _mask_dot(_chunk_tri(ts, False), g_raw)
        is_a, is_b = _gate_lanes(ts)
        go_ref[...] = jnp.where(is_a, gc, jnp.where(is_b, _sigmoid(gate), 0.0))

    out = pl.BlockSpec((ts, GDN_WIDTH), lambda i: (i, 0))
    lanes = pl.BlockSpec((ts, LANES), lambda i: (i, 0))
    return pl.pallas_call(
        body, name=name, grid=(t // ts,),
        in_specs=[pl.BlockSpec((ts, GDN_QKV), lambda i: (i, GDN_COL // GDN_QKV)),
                  pl.BlockSpec((ts, LANES), lambda i: (i, GATE_COL // LANES)),
                  _full((CONV_WIDTH, GDN_QKV)), _full((1, LANES)), _full((1, LANES))],
        out_specs=[out, out, out, lanes],
        out_shape=[jax.ShapeDtypeStruct((t, GDN_WIDTH), BF)] * 3 + [jax.ShapeDtypeStruct((t, LANES), F32)],
        scratch_shapes=[pltpu.VMEM((HALO, GDN_QKV), F32)],
        compiler_params=_params(("arbitrary",)),
    )(proj, proj, conv_w, a_pad, dt_pad)


def _gdn_prep_bwd(proj, dq, dk, dv, dgates, dff, conv_w, a_pad, dt_pad, seq, name):
    t = proj.shape[0]
    ts = _tile(seq, GDN_PREP_TILE)
    tpe = seq // ts
    nt = t // ts
    qscale = GDN_HEAD_DIM ** -0.5
    hb = ts // HALO

    def body(x_ref, halo_ref, gt_ref, dq_ref, dk_ref, dv_ref, dgt_ref, dff_ref, w_ref, a_ref, dt_ref,
             dx_ref, dgo_ref, dw_ref, da_ref, ddt_ref, dsl, carry):
        i = pl.program_id(0)
        ti = nt - 1 - i

        @pl.when(i == 0)
        def _():
            dw_ref[...] = jnp.zeros_like(dw_ref)
            da_ref[...] = jnp.zeros_like(da_ref)
            ddt_ref[...] = jnp.zeros_like(ddt_ref)

        halo = halo_ref[...]
        before = jnp.where(ti % tpe == 0, jnp.zeros_like(halo), halo)
        w = w_ref[...]
        taps, c, s = _conv_silu(x_ref[...], before, w)
        for h in range(GDN_HEADS):
            for base, ref, sc in ((0, dq_ref, qscale), (GDN_WIDTH, dk_ref, 1.0)):
                lo = base + h * LANES
                xh = s[:, lo:lo + LANES]
                r = lax.rsqrt(jnp.sum(xh * xh, axis=-1, keepdims=True) + EPS)
                y = xh * r
                dy = ref[:, h * LANES:(h + 1) * LANES] * sc
                dsl[:, lo:lo + LANES] = r * (dy - y * jnp.sum(dy * y, axis=-1, keepdims=True))
        dsl[:, 2 * GDN_WIDTH:] = dv_ref[...]
        sg = _sigmoid(c)
        dc = dsl[...] * (sg * (1.0 + c * (1.0 - sg)))
        nxt = carry[...]
        after = jnp.where(ti % tpe == tpe - 1, jnp.zeros_like(nxt), nxt)
        carry[...] = dc[0:HALO, :]
        dx = w[CONV_WIDTH - 1:CONV_WIDTH, :] * dc
        for kk in range(CONV_WIDTH - 1):
            dx = dx + w[kk:kk + 1, :] * _shift_rows(dc, after, CONV_WIDTH - 1 - kk, False)
        dx_ref[...] = dx.astype(BF)
        for kk in range(CONV_WIDTH):
            dw_ref[kk:kk + 1, :] += jnp.sum(dc * taps[kk], axis=0, keepdims=True)
        gate = gt_ref[...]
        dgt = dgt_ref[...]
        is_a, is_b = _gate_lanes(ts)
        dg_raw = _mask_dot(_chunk_tri(ts, True), jnp.where(is_a, dgt, 0.0))
        z = gate + dt_ref[...]
        na = -jnp.exp(a_ref[...])
        dga = dg_raw * na * _sigmoid(z)
        beta = _sigmoid(gate)
        dgb = jnp.where(is_b, dgt * beta * (1.0 - beta), 0.0)
        dgo_ref[...] = (dff_ref[...] + dga + dgb).astype(BF)
        ddt_ref[...] += jnp.sum(dga, axis=0, keepdims=True)
        da_ref[...] += jnp.sum(dg_raw * na * _softplus(z), axis=0, keepdims=True)

    rev = lambda wd, j: pl.BlockSpec((ts, wd), lambda i: (nt - 1 - i, j))
    halo_spec = pl.BlockSpec((HALO, GDN_QKV), lambda i: (jnp.maximum((nt - 1 - i) * hb - 1, 0), GDN_COL // GDN_QKV))
    return pl.pallas_call(
        body, name=name, grid=(nt,),
        in_specs=[rev(GDN_QKV, GDN_COL // GDN_QKV), halo_spec, rev(LANES, GATE_COL // LANES),
                  rev(GDN_WIDTH, 0), rev(GDN_WIDTH, 0), rev(GDN_WIDTH, 0), rev(LANES, 0), rev(LANES, 0),
                  _full((CONV_WIDTH, GDN_QKV)), _full((1, LANES)), _full((1, LANES))],
        out_specs=[rev(GDN_QKV, 0), rev(LANES, 0), _full((CONV_WIDTH, GDN_QKV)), _full((1, LANES)),
                   _full((1, LANES))],
        out_shape=[jax.ShapeDtypeStruct((t, GDN_QKV), BF), jax.ShapeDtypeStruct((t, LANES), BF),
                   jax.ShapeDtypeStruct((CONV_WIDTH, GDN_QKV), F32), jax.ShapeDtypeStruct((1, LANES), F32),
                   jax.ShapeDtypeStruct((1, LANES), F32)],
        scratch_shapes=[pltpu.VMEM((ts, GDN_QKV), F32), pltpu.VMEM((HALO, GDN_QKV), F32)],
        compiler_params=_params(("arbitrary",)),
    )(proj, proj, proj, dq, dk, dv, dgates, dff, conv_w, a_pad, dt_pad)


PAIR = 2 * CHUNK


def _split_bf16(a):
    hi = a.astype(BF)
    return hi, (a - hi.astype(F32)).astype(BF)


def _dot3(a, b, dims=(((1,), (0,)), ((), ()))):
    ah, al = _split_bf16(a)
    bh, bl = _split_bf16(b)
    (ca,), (cb,) = dims[0]
    return lax.dot_general(jnp.concatenate([ah, al, ah], axis=ca), jnp.concatenate([bh, bh, bl], axis=cb), dims,
                           preferred_element_type=F32)


def _inv_unit_lower(a):
    r = lax.broadcasted_iota(jnp.int32, (PAIR, PAIR), 0)
    c = lax.broadcasted_iota(jnp.int32, (PAIR, PAIR), 1)
    tm = (r == c).astype(F32) - a
    pw = _dot3(a, a)
    for _ in range(4):
        x = _dot3(jnp.concatenate([tm, pw], axis=0), pw)
        tm = tm + x[:PAIR]
        pw = x[PAIR:]
    return tm + _dot3(tm, pw)


def _gdn_pair_local(q, k, v, gc, gr, b):
    r = lax.broadcasted_iota(jnp.int32, (PAIR, PAIR), 0)
    c = lax.broadcasted_iota(jnp.int32, (PAIR, PAIR), 1)
    same = (r // CHUNK) == (c // CHUNK)
    incl = same & (r >= c)
    strict = same & (r > c)
    dm = jnp.exp(jnp.where(incl, gc - gr, NEG))
    e = jnp.exp(gc)
    kb = k * b
    vb = v * b
    kbe = kb * e
    kq = _dot_nt(jnp.concatenate([kb, q], axis=0).astype(BF), k.astype(BF))
    amat = jnp.where(strict, kq[:PAIR] * dm, 0.0)
    pmat = jnp.where(incl, kq[PAIR:] * dm, 0.0)
    lane = lax.broadcasted_iota(jnp.int32, (1, PAIR), 1)
    gl_a = jnp.sum(jnp.where(lane == CHUNK - 1, gr, 0.0), axis=1, keepdims=True)
    gl_b = jnp.sum(jnp.where(lane == PAIR - 1, gr, 0.0), axis=1, keepdims=True)
    ridx = lax.broadcasted_iota(jnp.int32, (PAIR, 1), 0)
    edec = jnp.exp(jnp.where(ridx < CHUNK, gl_a, gl_b) - gc)
    return dict(dm=dm, e=e, kb=kb, vb=vb, kbe=kbe, amat=amat, pmat=pmat, gl_a=gl_a, gl_b=gl_b, edec=edec,
                kd=k * edec, qd=q * e, incl=incl, strict=strict, ridx=ridx)


def _gdn_pair_states(loc, tb, s_a):
    uw = _dot(tb, jnp.concatenate([loc["vb"], loc["kbe"]], axis=1).astype(BF))
    u, w = uw[:, :LANES], uw[:, LANES:]
    qd, kd, c = loc["qd"], loc["kd"], CHUNK
    xa = _dot(jnp.concatenate([qd[:c], w[:c]], axis=0).astype(BF), s_a.astype(BF))
    vn_a = u[:c] - xa[c:]
    s_b = s_a * jnp.exp(loc["gl_a"]) + _dot_tn(kd[:c].astype(BF), vn_a.astype(BF))
    xb = _dot(jnp.concatenate([qd[c:], w[c:]], axis=0).astype(BF), s_b.astype(BF))
    vn_b = u[c:] - xb[c:]
    s_c = s_b * jnp.exp(loc["gl_b"]) + _dot_tn(kd[c:].astype(BF), vn_b.astype(BF))
    vn = jnp.concatenate([vn_a, vn_b], axis=0)
    o = jnp.concatenate([xa[:c], xb[:c]], axis=0) + _dot(loc["pmat"].astype(BF), vn.astype(BF))
    return w, vn, o, s_b, s_c


GDN_SEG = 512


def _gdn_specs(nb, seq, reverse):
    n = seq // CHUNK
    seg = _tile(seq, GDN_SEG)
    nseg = seq // seg
    sp = seg // PAIR
    at = (lambda s: nseg - 1 - s) if reverse else (lambda s: s)
    blk = pl.BlockSpec((nb, seg, GDN_WIDTH), lambda s: (0, at(s), 0))
    gg = pl.BlockSpec((nb, seg, GDN_WIDTH), lambda s: (0, at(s), GG_COL // GDN_WIDTH))
    gates = pl.BlockSpec((nb, seg, LANES), lambda s: (0, at(s), 0))
    rowb = pl.BlockSpec((nb, GDN_HEADS, sp, HALO, PAIR), lambda s: (0, 0, at(s), 0, 0))
    per_pair = pl.BlockSpec((nb, GDN_HEADS, sp, PAIR, PAIR), lambda s: (0, 0, at(s), 0, 0))
    return n, seg, nseg, sp, blk, gg, gates, rowb, per_pair


def _head_column(gt, lane, index):
    return jnp.sum(jnp.where(lane == index, gt, 0.0), axis=1, keepdims=True)


def _gdn_head_inputs(qkv_refs, gt_ref, gr_ref, rows, pi, lane, chains):
    per_chain = []
    for b, hh in chains:
        gt = gt_ref[b, rows, :]
        cols = slice(hh * LANES, (hh + 1) * LANES)
        per_chain.append([r[b, rows, cols].astype(F32) for r in qkv_refs]
                         + [_head_column(gt, lane, A_LANE + hh), gr_ref[b, hh, pi][0:1, :],
                            _head_column(gt, lane, B_LANE + hh)])
    return [jnp.stack(xs) for xs in zip(*per_chain)]


def _gdn_pair_fwd(qv, kv, vv, gcv, gr, bv, s_a):
    loc = _gdn_pair_local(qv, kv, vv, gcv, gr, bv)
    tf = _inv_unit_lower(loc["amat"])
    _, _, o, _, s_c = _gdn_pair_states(loc, tf.astype(BF), s_a)
    return tf, o, s_c


def _gdn_fwd(q, k, v, proj, gates, grow, wn, nb, seq, name):
    n, seg, nseg, sp, blk, gg, gates_spec, rowb, per_pair = _gdn_specs(nb, seq, False)
    chains = [(b, hh) for b in range(nb) for hh in range(GDN_HEADS)]

    def body(q_ref, k_ref, v_ref, gg_ref, gt_ref, gr_ref, wn_ref, y_ref, tn_ref, sn_ref, s_ref):
        @pl.when(pl.program_id(0) == 0)
        def _():
            s_ref[...] = jnp.zeros_like(s_ref)

        wnv = wn_ref[...]
        lane = lax.broadcasted_iota(jnp.int32, (PAIR, LANES), 1)

        def step(pi, carry):
            rows = pl.ds(pl.multiple_of(pi * PAIR, PAIR), PAIR)
            ins = _gdn_head_inputs((q_ref, k_ref, v_ref), gt_ref, gr_ref, rows, pi, lane, chains)
            s_a = s_ref[...]
            tf, o, s_c = jax.vmap(_gdn_pair_fwd)(*ins, s_a)
            s_ref[...] = s_c
            for c, (b, hh) in enumerate(chains):
                cols = slice(hh * LANES, (hh + 1) * LANES)
                tn_ref[b, hh, pi] = tf[c]
                sn_ref[b, hh, pi] = s_a[c]
                g = gg_ref[b, rows, cols]
                oh = o[c]
                rstd = lax.rsqrt(jnp.mean(oh * oh, axis=-1, keepdims=True) + EPS)
                y_ref[b, rows, cols] = (oh * rstd * wnv * (g * _sigmoid(g))).astype(BF)
            return carry

        lax.fori_loop(0, sp, step, 0)

    saved = jax.ShapeDtypeStruct((nb, GDN_HEADS, n // 2, PAIR, PAIR), F32)
    return pl.pallas_call(
        body, name=name, grid=(nseg,),
        in_specs=[blk, blk, blk, gg, gates_spec, rowb, _full((1, LANES))],
        out_specs=[blk, per_pair, per_pair],
        out_shape=[jax.ShapeDtypeStruct((nb, seq, GDN_WIDTH), BF), saved, saved],
        scratch_shapes=[pltpu.VMEM((len(chains), GDN_HEAD_DIM, GDN_HEAD_DIM), F32)],
        compiler_params=_params(("arbitrary",)),
    )(q, k, v, proj, gates, grow, wn)


def _gdn_pair_bwd(qv, kv, vv, gcv, gr, bv, tf, s_a, dsp, g, dyv, wnv):
    c = CHUNK
    loc = _gdn_pair_local(qv, kv, vv, gcv, gr, bv)
    tm = tf.astype(BF)
    kb, vb, kbe, e, dm = loc["kb"], loc["vb"], loc["kbe"], loc["e"], loc["dm"]
    kd, qd, pmat, amat = loc["kd"], loc["qd"], loc["pmat"], loc["amat"]
    w, vn, o, s_b, _ = _gdn_pair_states(loc, tm, s_a)
    sg = _sigmoid(g)
    silu = g * sg
    rstd = lax.rsqrt(jnp.mean(o * o, axis=-1, keepdims=True) + EPS)
    xhat = o * rstd
    dwn = jnp.sum(dyv * xhat * silu, axis=0, keepdims=True)
    dgg = dyv * xhat * wnv * (sg * (1.0 + g * (1.0 - sg)))
    dxhat = dyv * wnv * silu
    do = rstd * (dxhat - xhat * jnp.mean(dxhat * xhat, axis=-1, keepdims=True))
    dob = do.astype(BF)
    tot = lambda x: jnp.sum(jnp.sum(x, axis=1, keepdims=True), axis=0, keepdims=True)
    rsum = lambda x: jnp.sum(x, axis=1, keepdims=True)
    cat = lambda xs, ax=0: jnp.concatenate(xs, axis=ax)
    wb = w.astype(BF)
    qdb = qd.astype(BF)
    kdb = kd.astype(BF)
    vnb = vn.astype(BF)
    egl_a = jnp.exp(loc["gl_a"])
    egl_b = jnp.exp(loc["gl_b"])
    ptdo = _dot_tn(pmat.astype(BF), dob)
    dspb = dsp.astype(BF)
    dvn_b = ptdo[c:] + _dot(kdb[c:], dspb)
    dkd_b = _dot_nt(vnb[c:], dspb)
    dgl_b = egl_b * tot(s_b * dsp) + tot(dkd_b * kd[c:])
    dsm = egl_b * dsp + _dot_tn(cat([qdb[c:], -wb[c:]]), cat([dob[c:], dvn_b.astype(BF)]))
    dsmb = dsm.astype(BF)
    dvn_a = ptdo[:c] + _dot(kdb[:c], dsmb)
    dkd_a = _dot_nt(vnb[:c], dsmb)
    dgl_a = egl_a * tot(s_a * dsm) + tot(dkd_a * kd[:c])
    ds_new = egl_a * dsm + _dot_tn(cat([qdb[:c], -wb[:c]]), cat([dob[:c], dvn_a.astype(BF)]))
    ya = _dot_nt(cat([dob[:c], dvn_a.astype(BF)]), s_a.astype(BF))
    yb = _dot_nt(cat([dob[c:], dvn_b.astype(BF)]), s_b.astype(BF))
    dqd = cat([ya[:c], yb[:c]])
    dw = -cat([ya[c:], yb[c:]])
    dvn = cat([dvn_a, dvn_b])
    dkd = cat([dkd_a, dkd_b])
    dq = dqd * e
    dgc = rsum(dqd * qd) - rsum(dkd * kd)
    dk = dkd * loc["edec"]
    dpm = jnp.where(loc["incl"], _dot_nt(dob, vnb), 0.0)
    duw = cat([dvn, dw], 1).astype(BF)
    dt = _dot_nt(duw, cat([vb, kbe], 1).astype(BF))
    tt = _dot_tn(tm, duw)
    dvb, dkbe = tt[:, :LANES], tt[:, LANES:]
    tn_dims = (((0,), (0,)), ((), ()))
    nt_dims = (((1,), (1,)), ((), ()))
    da = jnp.where(loc["strict"], -_dot3(_dot3(tf, dt, tn_dims), tf, nt_dims), 0.0)
    st = cat([da * dm, dpm * dm]).astype(BF)
    z = _dot(st, kv.astype(BF))
    dkb = z[:PAIR] + dkbe * e
    dq = dq + z[PAIR:]
    dk = dk + _dot_tn(st, cat([kb, qv]).astype(BF))
    gmat = dpm * pmat + da * amat
    dgc = dgc + rsum(dkbe * kbe) + rsum(gmat)
    ridx = loc["ridx"]
    dgc = dgc + jnp.where(ridx == c - 1, dgl_a, 0.0) + jnp.where(ridx == PAIR - 1, dgl_b, 0.0)
    dgc_row = jnp.sum(gmat, axis=0, keepdims=True)
    db = rsum(dvb * vv) + rsum(dkb * kv)
    return dq, dk + dkb * bv, dvb * bv, dgg, dgc, dgc_row, db, dwn, ds_new


def _gdn_bwd(q, k, v, proj, gates, grow, wn, tinv_all, states_all, dy, nb, seq, name):
    n, seg, nseg, sp, blk, gg, gates_spec, rowb, per_pair = _gdn_specs(nb, seq, True)
    dh = GDN_HEAD_DIM
    chains = [(b, hh) for b in range(nb) for hh in range(GDN_HEADS)]

    def body(q_ref, k_ref, v_ref, gg_ref, gt_ref, gr_ref, wn_ref, tn_ref, sn_ref, dy_ref,
             dq_ref, dk_ref, dv_ref, dgg_ref, dgt_ref, dwn_ref, ds_ref):
        @pl.when(pl.program_id(0) == 0)
        def _():
            dwn_ref[...] = jnp.zeros_like(dwn_ref)
            ds_ref[...] = jnp.zeros_like(ds_ref)

        wnv = wn_ref[...]
        lane = lax.broadcasted_iota(jnp.int32, (PAIR, LANES), 1)

        def bwd_step(j, carry):
            pi = sp - 1 - j
            rows = pl.ds(pl.multiple_of(pi * PAIR, PAIR), PAIR)
            ins = _gdn_head_inputs((q_ref, k_ref, v_ref), gt_ref, gr_ref, rows, pi, lane, chains)
            lanes_of = lambda hh: slice(hh * LANES, (hh + 1) * LANES)
            saved = [jnp.stack([r[b, hh, pi] for b, hh in chains]) for r in (tn_ref, sn_ref)]
            g2 = jnp.stack([gg_ref[b, rows, lanes_of(hh)] for b, hh in chains])
            dy2 = jnp.stack([dy_ref[b, rows, lanes_of(hh)].astype(F32) for b, hh in chains])
            dq, dk, dv, dgg, dgc, dgc_row, db, dwn, ds_new = jax.vmap(
                _gdn_pair_bwd, in_axes=(0,) * 11 + (None,))(*ins, *saved, ds_ref[...], g2, dy2, wnv)
            ds_ref[...] = ds_new
            dgt = [jnp.zeros((PAIR, LANES), F32) for _ in range(nb)]
            for c, (b, hh) in enumerate(chains):
                cols = lanes_of(hh)
                dq_ref[b, rows, cols] = dq[c]
                dk_ref[b, rows, cols] = dk[c]
                dv_ref[b, rows, cols] = dv[c]
                dgg_ref[b, rows, cols] = dgg[c].astype(BF)
                dwn_ref[...] += dwn[c]
                row_as_col = jnp.transpose(jnp.broadcast_to(dgc_row[c], (PAIR, LANES)))
                dgt[b] = (dgt[b] + jnp.where(lane == A_LANE + hh, dgc[c] - row_as_col, 0.0)
                          + jnp.where(lane == B_LANE + hh, db[c], 0.0))
            for b in range(nb):
                dgt_ref[b, rows, :] = dgt[b]
            return carry

        lax.fori_loop(0, sp, bwd_step, 0)

    f32_out = jax.ShapeDtypeStruct((nb, seq, GDN_WIDTH), F32)
    return pl.pallas_call(
        body, name=name, grid=(nseg,),
        in_specs=[blk, blk, blk, gg, gates_spec, rowb, _full((1, LANES)), per_pair, per_pair, blk],
        out_specs=[blk, blk, blk, blk, gates_spec, _full((1, LANES))],
        out_shape=[f32_out, f32_out, f32_out, jax.ShapeDtypeStruct((nb, seq, GDN_WIDTH), BF),
                   jax.ShapeDtypeStruct((nb, seq, LANES), F32), jax.ShapeDtypeStruct((1, LANES), F32)],
        scratch_shapes=[pltpu.VMEM((len(chains), dh, dh), F32)],
        compiler_params=_params(("arbitrary",)),
    )(q, k, v, proj, gates, grow, wn, tinv_all, states_all, dy)


def _mix_to_padded(w):
    pad = jnp.zeros(w.shape[:-1] + (N_PAD - N_IN,), w.dtype)
    return jnp.concatenate([w[..., 0:1536], w[..., 1544:3080], w[..., 3088:3600], w[..., 1536:1544],
                            w[..., 3080:3088], pad], axis=-1)


def _pad_lanes(vec, start):
    return jnp.pad(vec[None, :], ((0, 0), (start, LANES - start - vec.shape[0])))


def _heads_to_rows(block, lane0, nheads, nb, seq):
    return block[:, lane0:lane0 + nheads].reshape(nb, seq, nheads).transpose(0, 2, 1).reshape(nb * nheads, seq)


def _mixer_small(p, l):
    wq_t = jnp.tile(p["fox_q_norm"][l], FOX_HEADS)[None, :]
    wk_t = jnp.tile(p["fox_k_norm"][l], FOX_HEADS)[None, :]
    bias = _pad_lanes(p["fox_f_bias"][l], 0)
    a_pad = _pad_lanes(p["gdn_a_log"][l], A_LANE)
    dt_pad = _pad_lanes(p["gdn_dt_bias"][l], A_LANE)
    wn = p["gdn_out_norm"][l][None, :]
    return wq_t, wk_t, bias, a_pad, dt_pad, wn


def _layer_fwd(x, p, l, nb, seq, rider=None, ffn1_rider=None, after_ffn1=None):
    npair = FOX_HEADS // 2
    n = seq // CHUNK
    x1, h1, *rode1 = _ffn_fwd(x, p["ffn1_norm"][l][None, :], p["ffn1_w_in"][l], p["ffn1_w_out"][l],
                              f"ffn1_fwd_{l}", ffn1_rider)
    if after_ffn1 is not None:
        after_ffn1(rode1)
    wq_t, wk_t, bias, a_pad, dt_pad, wn = _mixer_small(p, l)
    proj = _norm_matmul(x1, p["mix_norm"][l][None, :], p["w_mix"][l], f"mix_in_{l}")
    fq, fk, fv, cum = _fox_prep(proj, wq_t, wk_t, bias, seq, f"fox_prep_{l}")
    c8 = _heads_to_rows(cum, 0, FOX_HEADS, nb, seq)
    ck = c8.reshape(nb * npair, 2, 1, seq)
    o, lse, *rode = _fox_attn(fq, fk, fv, ck, nb, seq, f"fox_attn_{l}", rider)
    gq, gk, gv, gates = _gdn_prep(proj, p["gdn_conv"][l], a_pad, dt_pad, seq, f"gdn_prep_{l}")
    gc4 = _heads_to_rows(gates, A_LANE, GDN_HEADS, nb, seq)
    grow = jnp.broadcast_to(gc4.reshape(nb, GDN_HEADS, n // 2, 1, PAIR), (nb, GDN_HEADS, n // 2, HALO, PAIR))
    per_example = lambda a: a.reshape(nb, seq, a.shape[-1])
    gq, gk, gv, gates = per_example(gq), per_example(gk), per_example(gv), per_example(gates)
    y, tinv, states = _gdn_fwd(gq, gk, gv, per_example(proj), gates, grow, wn, nb, seq, f"gdn_fwd_{l}")
    y = y.reshape(nb * seq, GDN_WIDTH)
    x2 = _mix_out(x1, o, y, p["w_out"][l], f"mix_out_{l}")
    x3, h2 = _ffn_fwd(x2, p["ffn2_norm"][l][None, :], p["ffn2_w_in"][l], p["ffn2_w_out"][l], f"ffn2_fwd_{l}")
    saved = dict(x=x, h1=h1, x1=x1, proj=proj, fq=fq, fk=fk, fv=fv, ck=ck, o=o, lse=lse,
                 gq=gq, gk=gk, gv=gv, gates=gates, grow=grow, tinv=tinv, states=states, y=y, x2=x2, h2=h2)
    return x3, saved, rode


def _ffn_grads(dy, x, h, gain, win, wout, l, tag, rider=None):
    t, d = x.shape
    fs = win.shape[2]
    dx, dh, a, hn, dyh, dgain, *rode = _ffn_bwd(dy, x, h, gain, win, wout, f"{tag}_bwd_{l}", rider)
    g_in = _wgrad(hn, dh, jax.ShapeDtypeStruct((4, d, fs), BF),
                  pl.BlockSpec((None, d, fs), lambda i, j, k: (j, i, 0)), d, fs, f"{tag}_gw_in_{l}")
    g_out = _wgrad(a, dyh, jax.ShapeDtypeStruct((2 * fs, d), BF),
                   pl.BlockSpec((fs, d), lambda i, j, k: (i, j)), fs, d, f"{tag}_gw_out_{l}")
    return dx, dgain[0], g_in, g_out.reshape(4, fs // 2, d), rode


def _layer_bwd(dx3, p, l, sv, nb, seq, rider=None, before_ffn1=None, ffn2_rider=None, after_ffn2=None):
    npair = FOX_HEADS // 2
    d = dx3.shape[1]
    wq_t, wk_t, bias, a_pad, dt_pad, wn = _mixer_small(p, l)
    g = {}
    dx2, g["ffn2_norm"], g["ffn2_w_in"], g["ffn2_w_out"], rode2 = _ffn_grads(
        dx3, sv["x2"], sv["h2"], p["ffn2_norm"][l][None, :], p["ffn2_w_in"][l], p["ffn2_w_out"][l], l, "ffn2",
        ffn2_rider)
    if after_ffn2 is not None:
        rider = after_ffn2(rode2)
    dyf, dyg, dxb = _mix_out_bwd(dx2, p["w_out"][l], f"mix_out_bwd_{l}")
    half = lambda a, nm: _wgrad(a, dxb, jax.ShapeDtypeStruct((FOX_WIDTH, d), BF),
                                pl.BlockSpec((FOX_WIDTH, d), lambda i, j, k: (i, j)), FOX_WIDTH, d, nm)
    g["w_out"] = jnp.concatenate([half(sv["o"], f"gw_out_fox_{l}"), half(sv["y"], f"gw_out_gdn_{l}")], axis=0)
    dqa, dqb, dk, dv, dkx, *rode = _fox_attn_bwd(sv["fq"], sv["fk"], sv["fv"], sv["o"], dyf, sv["lse"], sv["ck"],
                                                 nb, seq, f"fox_attn_bwd_{l}", rider)

    dpf, dff, dwq, dwk, dbias = _fox_prep_bwd(sv["proj"], dqa, dqb, dk, dv, dkx, wq_t, wk_t, bias, seq,
                                              f"fox_prep_bwd_{l}")
    g["fox_q_norm"] = dwq[0, :FOX_HEAD_DIM]
    g["fox_k_norm"] = dwk[0, :FOX_HEAD_DIM]
    g["fox_f_bias"] = dbias[0, :FOX_HEADS]
    per_example = lambda a: a.reshape(nb, seq, a.shape[-1])
    flat = lambda a: a.reshape(nb * seq, a.shape[-1])
    dgq, dgk, dgv, dgg, dgates, dwn = _gdn_bwd(
        sv["gq"], sv["gk"], sv["gv"], per_example(sv["proj"]), sv["gates"], sv["grow"], wn, sv["tinv"],
        sv["states"], per_example(dyg), nb, seq, f"gdn_bwd_{l}")
    dgq, dgk, dgv, dgg, dgates = flat(dgq), flat(dgk), flat(dgv), flat(dgg), flat(dgates)
    dpg, dgate_blk, dconv, da, ddt = _gdn_prep_bwd(sv["proj"], dgq, dgk, dgv, dgates, dff, p["gdn_conv"][l],
                                                   a_pad, dt_pad, seq, f"gdn_prep_bwd_{l}")
    g["gdn_conv"] = dconv
    g["gdn_a_log"] = da[0, A_LANE:B_LANE]
    g["gdn_dt_bias"] = ddt[0, A_LANE:B_LANE]
    g["gdn_out_norm"] = dwn[0]
    dparts = [dpf, dpg, dgg, dgate_blk]
    dx1, hnm, dgm = _norm_matmul_bwd(dx2, dparts, sv["x1"], p["mix_norm"][l][None, :], p["w_mix"][l],
                                     f"mix_in_bwd_{l}")
    g["mix_norm"] = dgm[0]
    gp = _wgrad_parts(hnm, dparts, d // 2, f"gw_mix_{l}")
    gate = GATE_COL
    g["w_in"] = jnp.concatenate([gp[:, :GDN_COL], gp[:, gate:gate + FOX_HEADS], gp[:, GDN_COL:GG_COL],
                                 gp[:, gate + A_LANE:gate + B_LANE + GDN_HEADS], gp[:, GG_COL:gate]], axis=1)
    ffn1_rider = before_ffn1(g) if before_ffn1 is not None else None
    dx0, g["ffn1_norm"], g["ffn1_w_in"], g["ffn1_w_out"], rode1 = _ffn_grads(
        dx1, sv["x"], sv["h1"], p["ffn1_norm"][l][None, :], p["ffn1_w_in"][l], p["ffn1_w_out"][l], l, "ffn1",
        ffn1_rider)
    return dx0, g, rode, rode1


def _local_step(x, target, p):
    nb, seq, d = x.shape
    xt = x.reshape(nb * seq, d)
    saved = []
    for l in range(DEPTH):
        xt, sv, _ = _layer_fwd(xt, p, l, nb, seq)
        saved.append(sv)
    loss, dx = _loss_grad(xt, target.reshape(nb * seq, d), "loss")
    grads = [None] * DEPTH
    for l in reversed(range(DEPTH)):
        dx, grads[l], _, _ = _layer_bwd(dx, p, l, saved[l], nb, seq)
    return loss, dx.reshape(nb, seq, d), grads


N_CHIPS = 4


def _mesh_pos():
    return lax.axis_index("x"), lax.axis_index("y"), lax.axis_index("c")


def _other_chips(x, y):
    return [(1 - x, y), (x, 1 - y), (1 - x, 1 - y)]


def _remote(src, dst, send_sem, recv_sem, to):
    return pltpu.make_async_remote_copy(src_ref=src, dst_ref=dst, send_sem=send_sem, recv_sem=recv_sem,
                                        device_id=to, device_id_type=MESH)


def _hbm_call(body, name, ins, out_shape, scratch):
    return pl.pallas_call(
        body, name=name, out_shape=out_shape, in_specs=[HBM] * len(ins),
        out_specs=jax.tree.map(lambda _: HBM, out_shape), scratch_shapes=scratch,
        compiler_params=pltpu.CompilerParams(has_side_effects=True),
    )(*ins)


def _gather_phases(n, layer):
    def copies(ins, outs, sems):
        send1, recv1, send2, recv2 = sems
        x, y, c = _mesh_pos()
        out, back, fwd = [], [], []
        for i in range(n):
            for j, (px, py) in enumerate(_other_chips(x, y)):
                k = 3 * i + j
                blk = outs[i].at[2 * px + py]
                out.append(_remote(ins[i], outs[i].at[2 * x + y], send1.at[k], recv1.at[k], (px, py, c)))
                back.append(_remote(blk, blk, send1.at[k], recv1.at[k], (px, py, c)))
                fwd.append(_remote(blk, blk, send2.at[k], recv2.at[k], (x, y, 1 - c)))
        return c, out, back, fwd

    def first(ins, outs, sems):
        c, out, _, _ = copies(ins, outs, sems)

        @pl.when(c == layer)
        def _():
            for cp in out:
                cp.start()

    def middle(ins, outs, sems):
        c, _, back, fwd = copies(ins, outs, sems)

        @pl.when(c == layer)
        def _():
            for arrived, onward in zip(back, fwd):
                arrived.wait_recv()
                onward.start()

    def last(ins, outs, sems):
        c, out, _, fwd = copies(ins, outs, sems)

        @pl.when(c == layer)
        def _():
            for cp in out + fwd:
                cp.wait_send()

        @pl.when(c != layer)
        def _():
            for cp in fwd:
                cp.wait_recv()

    return first, middle, last


def _scatter_phases(n, layer):
    def copies(ins, outs, sems):
        send, recv = sems
        x, y, c = _mesh_pos()
        return c, [_remote(ins[i].at[2 * px + py], outs[i].at[j], send.at[3 * i + j], recv.at[3 * i + j], (px, py, c))
                   for i in range(n) for j, (px, py) in enumerate(_other_chips(x, y))]

    def first(ins, outs, sems):
        c, cps = copies(ins, outs, sems)

        @pl.when(c == layer)
        def _():
            for cp in cps:
                cp.start()

    def middle(ins, outs, sems):
        pass

    def last(ins, outs, sems):
        c, cps = copies(ins, outs, sems)

        @pl.when(c == layer)
        def _():
            for cp in cps:
                cp.wait()

    return first, middle, last


def _exchange(blocks, out_shapes, n_sems, phases, name, rider):
    sems = [pltpu.SemaphoreType.DMA((3 * len(blocks),))] * n_sems
    if rider:
        return _Rider(blocks, out_shapes, sems, phases)
    n = len(blocks)

    def body(*refs):
        for phase in phases:
            phase(refs[:n], refs[n:2 * n], refs[2 * n:])

    return list(_hbm_call(body, name, blocks, out_shapes, sems))


def _gather_layer(blocks, layer, name=None, rider=False):
    outs = [jax.ShapeDtypeStruct((N_CHIPS,) + b.shape, b.dtype) for b in blocks]
    return _exchange(blocks, outs, 4, _gather_phases(len(blocks), layer), name, rider)


def _scatter_layer(sums, layer, name=None, rider=False):
    outs = [jax.ShapeDtypeStruct((3,) + s.shape[1:], s.dtype) for s in sums]
    return _exchange(sums, outs, 2, _scatter_phases(len(sums), layer), name, rider)


def _to_sibling(gs, layer, name=None, rider=False):
    n = len(gs)

    def copies(ins, outs, sems):
        send, recv = sems
        x, y, c = _mesh_pos()
        return c, [_remote(ins[i], outs[i], send.at[i], recv.at[i], (x, y, 1 - c)) for i in range(n)]

    def first(ins, outs, sems):
        c, cps = copies(ins, outs, sems)

        @pl.when(c != layer)
        def _():
            for cp in cps:
                cp.start()

    def middle(ins, outs, sems):
        pass

    def last(ins, outs, sems):
        c, cps = copies(ins, outs, sems)

        @pl.when(c != layer)
        def _():
            for cp in cps:
                cp.wait_send()

        @pl.when(c == layer)
        def _():
            for cp in cps:
                cp.wait_recv()

    sems = [pltpu.SemaphoreType.DMA((n,))] * 2
    outs = [jax.ShapeDtypeStruct(g.shape, g.dtype) for g in gs]
    if rider:
        return _Rider(gs, outs, sems, (first, middle, last))

    def body(*refs):
        for phase in (first, middle, last):
            phase(refs[:n], refs[n:2 * n], refs[2 * n:])

    return list(_hbm_call(body, name, gs, outs, sems))


def _sibling_swap(rs, name):
    n = len(rs)

    def body(*refs):
        ins, outs = refs[:n], refs[n:2 * n]
        send, recv = refs[2 * n:]
        x, y, c = _mesh_pos()
        cps = [_remote(ins[i], outs[i], send.at[i], recv.at[i], (x, y, 1 - c)) for i in range(n)]
        for cp in cps:
            cp.start()
        for cp in cps:
            cp.wait()

    sem = pltpu.SemaphoreType.DMA((n,))
    return _hbm_call(body, name, rs, [jax.ShapeDtypeStruct(r.shape, r.dtype) for r in rs], [sem, sem])


def _small_all_reduce(vec, name):
    r = vec.shape[0]
    ndev = 8

    def body(v_ref, o_ref, buf, send, recv):
        x, y, c = _mesh_pos()
        me = 4 * x + 2 * y + c
        buf[me] = v_ref[...]
        cps = []
        for rel in range(1, ndev):
            px = 1 - x if rel & 4 else x
            py = 1 - y if rel & 2 else y
            pc = 1 - c if rel & 1 else c
            cps.append((_remote(v_ref, buf.at[me], send.at[rel - 1], recv.at[rel - 1], (px, py, pc)),
                        4 * px + 2 * py + pc))
        for cp, _ in cps:
            cp.start()
        for k, (cp, peer) in enumerate(cps):
            slot = buf.at[peer]
            _remote(slot, slot, send.at[k], recv.at[k], (x, y, c)).wait_recv()
        for cp, _ in cps:
            cp.wait_send()
        acc = buf[0]
        for k in range(1, ndev):
            acc = acc + buf[k]
        o_ref[...] = acc

    vm = pl.BlockSpec(memory_space=pltpu.VMEM)
    return pl.pallas_call(
        body, name=name, out_shape=jax.ShapeDtypeStruct(vec.shape, F32), in_specs=[vm], out_specs=vm,
        scratch_shapes=[pltpu.VMEM((ndev, r, LANES), F32), pltpu.SemaphoreType.DMA((ndev - 1,)),
                        pltpu.SemaphoreType.DMA((ndev - 1,))],
        compiler_params=pltpu.CompilerParams(has_side_effects=True),
    )(vec)


def _row_tile(rows, cap=SUM_ROWS):
    for t in range(min(rows, cap), 0, -1):
        if rows % t == 0 and (t % 16 == 0 or t == rows):
            return t
    raise ValueError(rows)


def _add_pairs(a, b, name):
    k, r, c = a.shape
    tr = _row_tile(r)

    def body(a_ref, b_ref, o_ref):
        o_ref[...] = (a_ref[...].astype(F32) + b_ref[...].astype(F32)).astype(o_ref.dtype)

    spec = pl.BlockSpec((None, tr, c), lambda i, j: (i, j, 0))
    return pl.pallas_call(body, name=name, grid=(k, r // tr), in_specs=[spec, spec], out_specs=spec,
                          out_shape=jax.ShapeDtypeStruct(a.shape, a.dtype),
                          compiler_params=_params(("parallel", "parallel")))(a, b)


def _final_sum(own, sib, others, name):
    r, c = own.shape
    tr = _row_tile(r)

    def body(a_ref, b_ref, o_ref_in, out_ref):
        acc = a_ref[...].astype(F32) + b_ref[...].astype(F32)
        for k in range(3):
            acc = acc + o_ref_in[k].astype(F32)
        out_ref[...] = acc

    spec = pl.BlockSpec((tr, c), lambda i: (i, 0))
    return pl.pallas_call(body, name=name, grid=(r // tr,),
                          in_specs=[spec, spec, pl.BlockSpec((3, tr, c), lambda i: (0, i, 0))], out_specs=spec,
                          out_shape=jax.ShapeDtypeStruct((r, c), F32),
                          compiler_params=_params(("parallel",)))(own, sib, others)


def _adamw(g, w, m, v, name):
    r, c = g.shape
    tr = _row_tile(r, ADAM_ROWS)

    def body(g_ref, w_ref, m_ref, v_ref, d_ref, mo_ref, vo_ref):
        gv = g_ref[...]
        mn = ADAM_B1 * m_ref[...] + (1.0 - ADAM_B1) * gv
        vn = ADAM_B2 * v_ref[...] + (1.0 - ADAM_B2) * (gv * gv)
        m_hat = mn / (1.0 - ADAM_B1 ** ADAM_STEP)
        v_hat = vn / (1.0 - ADAM_B2 ** ADAM_STEP)
        d_ref[...] = -ADAM_LR * (m_hat / (jnp.sqrt(v_hat) + ADAM_EPS) + ADAM_WD * w_ref[...])
        mo_ref[...] = mn
        vo_ref[...] = vn

    spec = pl.BlockSpec((tr, c), lambda i: (i, 0))
    shp = jax.ShapeDtypeStruct((r, c), F32)
    return pl.pallas_call(body, name=name, grid=(r // tr,), in_specs=[spec] * 4, out_specs=[spec] * 3,
                          out_shape=[shp] * 3, compiler_params=_params(("parallel",)))(g, w, m, v)


def _pack(arrays):
    flat = jnp.concatenate([a.reshape(-1).astype(F32) for a in arrays])
    pad = (-flat.shape[0]) % (8 * LANES)
    return jnp.concatenate([flat, jnp.zeros((pad,), F32)]).reshape(-1, LANES)


def _unpack(packed, shapes):
    flat = packed.reshape(-1)
    out, off = [], 0
    for s in shapes:
        size = 1
        for dim in s:
            size *= dim
        out.append(flat[off:off + size].reshape(s))
        off += size
    return out


BIG = ("ffn1_w_in", "ffn1_w_out", "w_in", "w_out", "ffn2_w_in", "ffn2_w_out")
SMALL = ("ffn1_norm", "mix_norm", "fox_q_norm", "fox_k_norm", "fox_f_bias", "gdn_a_log", "gdn_dt_bias",
         "gdn_out_norm", "ffn2_norm", "gdn_conv")
WEIGHTS = ("ffn1_norm", "ffn1_w_in", "ffn1_w_out", "mix_norm", "w_in", "fox_q_norm", "fox_k_norm", "fox_f_bias",
           "gdn_conv", "gdn_a_log", "gdn_dt_bias", "gdn_out_norm", "w_out", "ffn2_norm", "ffn2_w_in", "ffn2_w_out")


def _step(x, target, w, m, v):
    xi, yi, ci = _mesh_pos()
    me = 2 * xi + yi
    depth = DEPTH
    d = x.shape[-1]

    nb, seq, _ = x.shape
    assert depth == 2

    p ={k: w[k] for k in SMALL if k != "gdn_conv"}
    for k in ("ffn1_w_in", "ffn1_w_out", "ffn2_w_in", "ffn2_w_out", "w_mix", "w_out", "gdn_conv"):
        p[k] = [None] * depth

    first, rest = BIG[:2], BIG[2:] + ("gdn_conv",)

    def shards(l, names):
        return [w[k][l] if k == "gdn_conv" else w[k][l].astype(BF) for k in names]

    def place(l, names, gathered):
        blocks = dict(zip(names, [lax.dynamic_update_index_in_dim(g, s, me, 0)
                                  for g, s in zip(gathered, shards(l, names))]))
        for k in ("ffn1_w_in", "ffn1_w_out", "ffn2_w_in", "ffn2_w_out"):
            if k in blocks:
                p[k][l] = blocks[k]
        if "w_in" in blocks:
            p["w_mix"][l] = _mix_to_padded(blocks["w_in"].transpose(1, 0, 2).reshape(d, N_IN))
            p["w_out"][l] = blocks["w_out"].reshape(2 * FOX_WIDTH, d)
            p["gdn_conv"][l] = blocks["gdn_conv"].transpose(1, 0, 2).reshape(CONV_WIDTH, -1)

    place(0, first, _gather_layer(shards(0, first), 0, "gather_first_ffn0"))
    xt = x.reshape(nb * seq, d)
    xt, saved0, gathered1 = _layer_fwd(
        xt, p, 0, nb, seq, _gather_layer(shards(1, first + rest), 1, rider=True),
        _gather_layer(shards(0, rest), 0, rider=True), lambda got: place(0, rest, got))
    place(1, first + rest, gathered1)
    xt, saved1, _ = _layer_fwd(xt, p, 1, nb, seq)
    loss, dx = _loss_grad(xt, target.reshape(nb * seq, d), "loss")

    def transport(g, names):
        out = []
        for k in names:
            if k == "w_in":
                out.append(g["w_in"].reshape(d, N_CHIPS, N_IN // N_CHIPS).transpose(1, 0, 2).astype(BF))
            elif k == "w_out":
                out.append(g["w_out"].reshape(N_CHIPS, -1, d))
            else:
                out.append(g[k])
        return out

    def chip_sums(g, l, names, tag):
        own = transport(g, names)
        sib = _to_sibling(own, l, f"grad{l}{tag}_to_sibling")
        return own, sib, [_add_pairs(a, b, f"grad{l}{tag}_chip_sum_{k}") for a, b, k in zip(own, sib, names)]

    dx, grads1, _, _ = _layer_bwd(dx, p, 1, saved1, nb, seq)
    own1 = transport(grads1, BIG)
    before = {}

    def after_ffn2(from_sibling):
        before["sib1"] = from_sibling
        sums1 = [_add_pairs(a, b, f"grad1_chip_sum_{k}") for a, b, k in zip(own1, from_sibling, BIG)]
        return _scatter_layer(sums1, 1, rider=True)

    def before_ffn1(g):
        before["own"], before["sib"], sums = chip_sums(g, 0, BIG[2:], "_rest")
        return _scatter_layer(sums, 0, rider=True)

    dx, grads0, chips1, chips0_rest = _layer_bwd(dx, p, 0, saved0, nb, seq, None, before_ffn1,
                                                 _to_sibling(own1, 1, rider=True), after_ffn2)
    sib1 = before["sib1"]
    own0, sib0, sums0 = chip_sums(grads0, 0, first, "_first")
    chips0 = _scatter_layer(sums0, 0, "grad0_first_to_chips") + chips0_rest
    own0, sib0 = own0 + before["own"], sib0 + before["sib"]
    grads = [grads0, grads1]
    dx = dx.reshape(nb, seq, d)

    mine = lambda a0, a1: jnp.where(ci == 0, a0, a1)
    at_me = lambda a: lax.dynamic_index_in_dim(a, me, 0, keepdims=False)
    reduced = [_final_sum(mine(at_me(own0[i]), at_me(own1[i])), mine(at_me(sib0[i]), at_me(sib1[i])),
                          mine(chips0[i], chips1[i]), f"grad_final_sum_{k}") for i, k in enumerate(BIG)]
    from_sib_final = _sibling_swap(reduced, "grad_swap_layers")
    full = {k: jnp.stack([jnp.where(ci == 0, a, b), jnp.where(ci == 0, b, a)])
            for k, a, b in zip(BIG, reduced, from_sib_final)}

    out_g, out_d, out_m, out_v = {}, {}, {}, {}
    for k in BIG:
        shp = w[k].shape
        two_d = lambda a: a.reshape(shp[0] * shp[1], shp[2])
        dl, mn, vn = _adamw(two_d(full[k]), two_d(w[k]), two_d(m[k]), two_d(v[k]), f"adamw_{k}")
        out_g[k], out_d[k], out_m[k], out_v[k] = full[k], dl.reshape(shp), mn.reshape(shp), vn.reshape(shp)

    small_local = [jnp.stack([grads[l][k] for l in range(depth)]) for k in SMALL] + [loss.reshape(1)]
    summed = _unpack(_small_all_reduce(_pack(small_local), "small_all_reduce"), [a.shape for a in small_local])
    total = summed.pop()[0]
    sg = dict(zip(SMALL, summed))
    cs = w["gdn_conv"].shape[-1]
    sg["gdn_conv"] = lax.dynamic_slice_in_dim(sg["gdn_conv"], me * cs, cs, axis=2)
    shapes = [w[k].shape for k in SMALL]
    packs = [_pack([src[k] for k in SMALL]) for src in (sg, w, m, v)]
    dl, mn, vn = _adamw(*packs, "adamw_small")
    for k, a, b, c2 in zip(SMALL, _unpack(dl, shapes), _unpack(mn, shapes), _unpack(vn, shapes)):
        out_g[k], out_d[k], out_m[k], out_v[k] = sg[k], a, b, c2

    return (total, dx, *[out_g[k] for k in WEIGHTS], *[out_d[k] for k in WEIGHTS],
            *[out_m[k] for k in WEIGHTS], *[out_v[k] for k in WEIGHTS])


def kernel(x, ffn1_norm, ffn1_w_in, ffn1_w_out, mix_norm, w_in, fox_q_norm, fox_k_norm, fox_f_bias, gdn_conv, gdn_a_log, gdn_dt_bias, gdn_out_norm, w_out, ffn2_norm, ffn2_w_in, ffn2_w_out, loss_target, m_ffn1_norm, m_ffn1_w_in, m_ffn1_w_out, m_mix_norm, m_w_in, m_fox_q_norm, m_fox_k_norm, m_fox_f_bias, m_gdn_conv, m_gdn_a_log, m_gdn_dt_bias, m_gdn_out_norm, m_w_out, m_ffn2_norm, m_ffn2_w_in, m_ffn2_w_out, v_ffn1_norm, v_ffn1_w_in, v_ffn1_w_out, v_mix_norm, v_w_in, v_fox_q_norm, v_fox_k_norm, v_fox_f_bias, v_gdn_conv, v_gdn_a_log, v_gdn_dt_bias, v_gdn_out_norm, v_w_out, v_ffn2_norm, v_ffn2_w_in, v_ffn2_w_out):
    w = dict(ffn1_norm=ffn1_norm, ffn1_w_in=ffn1_w_in, ffn1_w_out=ffn1_w_out, mix_norm=mix_norm, w_in=w_in,
             fox_q_norm=fox_q_norm, fox_k_norm=fox_k_norm, fox_f_bias=fox_f_bias, gdn_conv=gdn_conv,
             gdn_a_log=gdn_a_log, gdn_dt_bias=gdn_dt_bias, gdn_out_norm=gdn_out_norm, w_out=w_out,
             ffn2_norm=ffn2_norm, ffn2_w_in=ffn2_w_in, ffn2_w_out=ffn2_w_out)
    m = dict(ffn1_norm=m_ffn1_norm, ffn1_w_in=m_ffn1_w_in, ffn1_w_out=m_ffn1_w_out, mix_norm=m_mix_norm, w_in=m_w_in,
             fox_q_norm=m_fox_q_norm, fox_k_norm=m_fox_k_norm, fox_f_bias=m_fox_f_bias, gdn_conv=m_gdn_conv,
             gdn_a_log=m_gdn_a_log, gdn_dt_bias=m_gdn_dt_bias, gdn_out_norm=m_gdn_out_norm, w_out=m_w_out,
             ffn2_norm=m_ffn2_norm, ffn2_w_in=m_ffn2_w_in, ffn2_w_out=m_ffn2_w_out)
    v = dict(ffn1_norm=v_ffn1_norm, ffn1_w_in=v_ffn1_w_in, ffn1_w_out=v_ffn1_w_out, mix_norm=v_mix_norm, w_in=v_w_in,
             fox_q_norm=v_fox_q_norm, fox_k_norm=v_fox_k_norm, fox_f_bias=v_fox_f_bias, gdn_conv=v_gdn_conv,
             gdn_a_log=v_gdn_a_log, gdn_dt_bias=v_gdn_dt_bias, gdn_out_norm=v_gdn_out_norm, w_out=v_w_out,
             ffn2_norm=v_ffn2_norm, ffn2_w_in=v_ffn2_w_in, ffn2_w_out=v_ffn2_w_out)
    return _step(x, loss_target, w, m, v)
```

```python
import jax
import jax.numpy as jnp
from jax import lax
from jax.experimental import pallas as pl
from jax.experimental.pallas import tpu as pltpu

F32 = jnp.float32
BF = jnp.bfloat16
HI = lax.Precision.HIGHEST
MESH = pl.DeviceIdType.MESH

DEPTH = 2
FOX_HEADS = 8
FOX_HEAD_DIM = 64
FOX_WIDTH = 512
GDN_HEADS = 4
GDN_HEAD_DIM = 128
GDN_WIDTH = 512
CONV_WIDTH = 4
CHUNK = 64
EPS = 1e-6
N_IN = 3600
N_PAD = 3712
GATE_COL = 3584
LANES = 128
NEG = -1e30

ADAM_LR = 0.001
ADAM_B1 = 0.9
ADAM_B2 = 0.999
ADAM_EPS = 1e-08
ADAM_WD = 0.01
ADAM_STEP = 10

VMEM_LIMIT = 56 * 1024 * 1024

TOKEN_TILE = 512
TOKEN_TILE_BWD = 256
WGRAD_TOKENS = 512
FOX_PREP_TILE = 512
GDN_PREP_TILE = 256
ATTN_FWD_TILE = 2048
ATTN_BWD_TILE = 1024
DIAGONAL_STRIPS = 2
SUM_ROWS = 512
ADAM_ROWS = 256


def _params(sem=None, **kw):
    return pltpu.CompilerParams(dimension_semantics=sem, vmem_limit_bytes=VMEM_LIMIT, **kw)


def _dot(a, b, precision=None):
    return jnp.dot(a, b, preferred_element_type=F32, precision=precision)


def _dot_nt(a, b, precision=None):
    return lax.dot_general(a, b, (((1,), (1,)), ((), ())), preferred_element_type=F32, precision=precision)


def _dot_tn(a, b, precision=None):
    return lax.dot_general(a, b, (((0,), (0,)), ((), ())), preferred_element_type=F32, precision=precision)


def _sigmoid(x):
    return 0.5 * jnp.tanh(0.5 * x) + 0.5


def _softplus(x):
    return jnp.maximum(x, 0.0) + jnp.log(1.0 + jnp.exp(-jnp.abs(x)))


def _log_sigmoid(x):
    return jnp.minimum(x, 0.0) - jnp.log(1.0 + jnp.exp(-jnp.abs(x)))


def _tile(n, t):
    t = min(n, t)
    assert n % t == 0, (n, t)
    return t


def _rms_fwd(x, gain):
    rstd = lax.rsqrt(jnp.mean(x * x, axis=-1, keepdims=True) + EPS)
    xhat = x * rstd
    return xhat * gain, xhat, rstd


def _rms_bwd(dy, xhat, rstd, gain):
    dxhat = dy * gain
    dx = rstd * (dxhat - xhat * jnp.mean(dxhat * xhat, axis=-1, keepdims=True))
    return dx, dy * xhat


def _full(shape):
    nd = len(shape)
    return pl.BlockSpec(shape, lambda *_: (0,) * nd)


HBM = pl.BlockSpec(memory_space=pltpu.HBM)


def _load_ffn_weights(win_hbm, wout_hbm, win_v, wout_v, sem):
    fr = wout_hbm.shape[1]
    copies = [pltpu.make_async_copy(win_hbm.at[s], win_v.at[s], sem.at[s]) for s in range(4)]
    copies += [pltpu.make_async_copy(wout_hbm.at[s], wout_v.at[pl.ds(s * fr, fr)], sem.at[4 + s])
               for s in range(4)]
    for c in copies:
        c.start()
    for c in copies:
        c.wait()


def _ffn_fwd(x, gain, win_g, wout_g, name, rider=None):
    t, d = x.shape
    _, _, fs = win_g.shape
    fr = wout_g.shape[1]
    tm = _tile(t, TOKEN_TILE)
    r_in, r_out, r_sem = _rider_parts(rider)
    steps = t // tm

    def body(x_ref, g_ref, win_hbm, wout_hbm, *rest):
        rin, (xo_ref, h_ref) = rest[:len(r_in)], rest[len(r_in):len(r_in) + 2]
        rout = rest[len(r_in) + 2:len(r_in) + 2 + len(r_out)]
        win_v, wout_v, sem = rest[len(r_in) + 2 + len(r_out):len(r_in) + 5 + len(r_out)]
        riding = (rin, rout, rest[len(r_in) + 5 + len(r_out):])
        step = pl.program_id(0)
        _ride(rider, 0, step == 0, riding)
        _ride(rider, 1, step == (3 * steps) // 4, riding)

        @pl.when(step == 0)
        def _():
            _load_ffn_weights(win_hbm, wout_hbm, win_v, wout_v, sem)

        xv = x_ref[...]
        hn, _, _ = _rms_fwd(xv, g_ref[...])
        hn = hn.astype(BF)
        acc = jnp.zeros((tm, d), F32)
        for s in range(2):
            g = _dot(hn, win_v[s])
            u = _dot(hn, win_v[s + 2])
            h_ref[:, s * fs:(s + 1) * fs] = g.astype(BF)
            h_ref[:, (s + 2) * fs:(s + 3) * fs] = u.astype(BF)
            a = (g * _sigmoid(g) * u).astype(BF)
            acc = acc + _dot(a, wout_v[s * fs:(s + 1) * fs, :])
        xo_ref[...] = xv + 0.5 * acc
        _ride(rider, 2, step == steps - 1, riding)

    return pl.pallas_call(
        body, name=name, grid=(steps,),
        in_specs=[pl.BlockSpec((tm, d), lambda i: (i, 0)), _full((1, d)), HBM, HBM] + [HBM] * len(r_in),
        out_specs=[pl.BlockSpec((tm, d), lambda i: (i, 0)), pl.BlockSpec((tm, 4 * fs), lambda i: (i, 0))]
        + [HBM] * len(r_out),
        out_shape=[jax.ShapeDtypeStruct((t, d), F32), jax.ShapeDtypeStruct((t, 4 * fs), BF)] + r_out,
        scratch_shapes=[pltpu.VMEM((4, d, fs), BF), pltpu.VMEM((4 * fr, d), BF), pltpu.SemaphoreType.DMA((8,))]
        + r_sem,
        compiler_params=_params(("arbitrary",), has_side_effects=rider is not None),
    )(x, gain, win_g, wout_g, *r_in)


def _ffn_bwd(dy, x, h, gain, win_g, wout_g, name, rider=None):
    t, d = x.shape
    _, _, fs = win_g.shape
    fr = wout_g.shape[1]
    tm = _tile(t, TOKEN_TILE_BWD)
    r_in, r_out, r_sem = _rider_parts(rider)
    steps = t // tm

    def body(dy_ref, x_ref, h_ref, g_ref, win_hbm, wout_hbm, *rest):
        rin, (dx_ref, dh_ref, a_ref, hn_ref, dyh_ref, dg_ref) = rest[:len(r_in)], rest[len(r_in):len(r_in) + 6]
        rout = rest[len(r_in) + 6:len(r_in) + 6 + len(r_out)]
        win_v, wout_v, sem = rest[len(r_in) + 6 + len(r_out):len(r_in) + 9 + len(r_out)]
        riding = (rin, rout, rest[len(r_in) + 9 + len(r_out):])
        step = pl.program_id(0)
        _ride(rider, 0, step == 0, riding)
        _ride(rider, 1, step == (3 * steps) // 4, riding)

        @pl.when(step == 0)
        def _():
            _load_ffn_weights(win_hbm, wout_hbm, win_v, wout_v, sem)
            dg_ref[...] = jnp.zeros_like(dg_ref)

        dyv = dy_ref[...]
        dyh = (0.5 * dyv).astype(BF)
        dyh_ref[...] = dyh
        dhn = jnp.zeros((tm, d), F32)
        for s in range(2):
            da = _dot_nt(dyh, wout_v[s * fs:(s + 1) * fs, :])
            g = h_ref[:, s * fs:(s + 1) * fs].astype(F32)
            u = h_ref[:, (s + 2) * fs:(s + 3) * fs].astype(F32)
            sg = _sigmoid(g)
            si = g * sg
            a_ref[:, s * fs:(s + 1) * fs] = (si * u).astype(BF)
            dgate = (da * u * (sg * (1.0 + g * (1.0 - sg)))).astype(BF)
            dup = (da * si).astype(BF)
            dh_ref[:, s * fs:(s + 1) * fs] = dgate
            dh_ref[:, (s + 2) * fs:(s + 3) * fs] = dup
            dhn = dhn + _dot_nt(dgate, win_v[s]) + _dot_nt(dup, win_v[s + 2])
        xv = x_ref[...]
        gain_v = g_ref[...]
        hn, xhat, rstd = _rms_fwd(xv, gain_v)
        hn_ref[...] = hn.astype(BF)
        dx, dgr = _rms_bwd(dhn, xhat, rstd, gain_v)
        dx_ref[...] = dyv + dx
        dg_ref[...] += jnp.sum(dgr, axis=0, keepdims=True)
        _ride(rider, 2, step == steps - 1, riding)

    row = lambda w: pl.BlockSpec((tm, w), lambda i: (i, 0))
    return pl.pallas_call(
        body, name=name, grid=(steps,),
        in_specs=[row(d), row(d), row(4 * fs), _full((1, d)), HBM, HBM] + [HBM] * len(r_in),
        out_specs=[row(d), row(4 * fs), row(2 * fs), row(d), row(d), _full((1, d))] + [HBM] * len(r_out),
        out_shape=[jax.ShapeDtypeStruct((t, d), F32), jax.ShapeDtypeStruct((t, 4 * fs), BF),
                   jax.ShapeDtypeStruct((t, 2 * fs), BF), jax.ShapeDtypeStruct((t, d), BF),
                   jax.ShapeDtypeStruct((t, d), BF), jax.ShapeDtypeStruct((1, d), F32)] + r_out,
        scratch_shapes=[pltpu.VMEM((4, d, fs), BF), pltpu.VMEM((4 * fr, d), BF), pltpu.SemaphoreType.DMA((8,))]
        + r_sem,
        compiler_params=_params(("arbitrary",), has_side_effects=rider is not None),
    )(dy, x, h, gain, win_g, wout_g, *r_in)


def _wgrad(a, b, out_shape, out_spec, tm, tn, name, tk=WGRAD_TOKENS):
    t, m = a.shape
    _, n = b.shape
    tk = _tile(t, tk)
    nk = t // tk

    def body(a_ref, b_ref, o_ref, acc):
        k = pl.program_id(2)

        @pl.when(k == 0)
        def _():
            acc[...] = jnp.zeros_like(acc)

        acc[...] += _dot_tn(a_ref[...], b_ref[...])

        @pl.when(k == nk - 1)
        def _():
            o_ref[...] = acc[...].astype(o_ref.dtype)

    return pl.pallas_call(
        body, name=name, grid=(m // tm, n // tn, nk),
        in_specs=[pl.BlockSpec((tk, tm), lambda i, j, k: (k, i)), pl.BlockSpec((tk, tn), lambda i, j, k: (k, j))],
        out_specs=out_spec, out_shape=out_shape,
        scratch_shapes=[pltpu.VMEM((tm, tn), F32)],
        compiler_params=_params(("parallel", "parallel", "arbitrary")),
    )(a, b)


def _wgrad_parts(a, parts, tm, name, tk=WGRAD_TOKENS):
    t, m = a.shape
    widths = [p.shape[1] for p in parts]
    n = sum(widths)
    tk = _tile(t, tk)
    nk = t // tk
    np_ = len(parts)

    def body(a_ref, *rest):
        b_refs, o_ref, acc = rest[:np_], rest[np_], rest[np_ + 1]
        k = pl.program_id(1)

        @pl.when(k == 0)
        def _():
            acc[...] = jnp.zeros_like(acc)

        av, off = a_ref[...], 0
        for b_ref, wd in zip(b_refs, widths):
            acc[:, off:off + wd] += _dot_tn(av, b_ref[...])
            off += wd

        @pl.when(k == nk - 1)
        def _():
            o_ref[...] = acc[...]

    return pl.pallas_call(
        body, name=name, grid=(m // tm, nk),
        in_specs=[pl.BlockSpec((tk, tm), lambda i, k: (k, i))]
        + [pl.BlockSpec((tk, wd), lambda i, k: (k, 0)) for wd in widths],
        out_specs=pl.BlockSpec((tm, n), lambda i, k: (i, 0)), out_shape=jax.ShapeDtypeStruct((m, n), F32),
        scratch_shapes=[pltpu.VMEM((tm, n), F32)],
        compiler_params=_params(("parallel", "arbitrary")),
    )(a, *parts)


def _norm_matmul(x, gain, w, name):
    t, d = x.shape
    n = w.shape[1]
    tm = _tile(t, TOKEN_TILE)

    def body(x_ref, g_ref, w_ref, o_ref):
        hn, _, _ = _rms_fwd(x_ref[...], g_ref[...])
        o_ref[...] = _dot(hn.astype(BF), w_ref[...])

    return pl.pallas_call(
        body, name=name, grid=(t // tm,),
        in_specs=[pl.BlockSpec((tm, d), lambda i: (i, 0)), _full((1, d)), _full((d, n))],
        out_specs=pl.BlockSpec((tm, n), lambda i: (i, 0)),
        out_shape=jax.ShapeDtypeStruct((t, n), F32),
        compiler_params=_params(("parallel",)),
    )(x, gain, w)


def _norm_matmul_bwd(dres, dparts, x, gain, w, name):
    t, d = x.shape
    n = w.shape[1]
    tm = _tile(t, TOKEN_TILE_BWD)
    widths = [a.shape[1] for a in dparts]
    assert sum(widths) == n
    k = len(dparts)

    def body(dr_ref, *rest):
        dp_refs, (x_ref, g_ref, w_ref, dx_ref, hn_ref, dg_ref) = rest[:k], rest[k:]

        @pl.when(pl.program_id(0) == 0)
        def _():
            dg_ref[...] = jnp.zeros_like(dg_ref)

        dhn, off = jnp.zeros((tm, d), F32), 0
        for dp_ref, wd in zip(dp_refs, widths):
            dhn = dhn + _dot_nt(dp_ref[...], w_ref[:, off:off + wd])
            off += wd
        gain_v = g_ref[...]
        hn, xhat, rstd = _rms_fwd(x_ref[...], gain_v)
        hn_ref[...] = hn.astype(BF)
        dx, dgr = _rms_bwd(dhn, xhat, rstd, gain_v)
        dx_ref[...] = dr_ref[...] + dx
        dg_ref[...] += jnp.sum(dgr, axis=0, keepdims=True)

    row = lambda wd: pl.BlockSpec((tm, wd), lambda i: (i, 0))
    return pl.pallas_call(
        body, name=name, grid=(t // tm,),
        in_specs=[row(d)] + [row(wd) for wd in widths] + [row(d), _full((1, d)), _full((d, n))],
        out_specs=[row(d), row(d), _full((1, d))],
        out_shape=[jax.ShapeDtypeStruct((t, d), F32), jax.ShapeDtypeStruct((t, d), BF),
                   jax.ShapeDtypeStruct((1, d), F32)],
        compiler_params=_params(("arbitrary",)),
    )(dres, *dparts, x, gain, w)


def _mix_out(x, yf, yg, w, name):
    t, d = x.shape
    kf = yf.shape[1]
    tm = _tile(t, TOKEN_TILE)

    def body(x_ref, yf_ref, yg_ref, w_ref, o_ref):
        o_ref[...] = x_ref[...] + _dot(yf_ref[...], w_ref[0:kf, :]) + _dot(yg_ref[...], w_ref[kf:2 * kf, :])

    row = lambda wd: pl.BlockSpec((tm, wd), lambda i: (i, 0))
    return pl.pallas_call(
        body, name=name, grid=(t // tm,),
        in_specs=[row(d), row(kf), row(kf), _full((2 * kf, d))],
        out_specs=row(d), out_shape=jax.ShapeDtypeStruct((t, d), F32),
        compiler_params=_params(("parallel",)),
    )(x, yf, yg, w)


def _mix_out_bwd(dx, w, name):
    t, d = dx.shape
    kf = w.shape[0] // 2
    tm = _tile(t, TOKEN_TILE)

    def body(dx_ref, w_ref, df_ref, dg_ref, dxb_ref):
        dxb = dx_ref[...].astype(BF)
        dxb_ref[...] = dxb
        df_ref[...] = _dot_nt(dxb, w_ref[0:kf, :]).astype(BF)
        dg_ref[...] = _dot_nt(dxb, w_ref[kf:2 * kf, :]).astype(BF)

    row = lambda wd: pl.BlockSpec((tm, wd), lambda i: (i, 0))
    return pl.pallas_call(
        body, name=name, grid=(t // tm,),
        in_specs=[row(d), _full((2 * kf, d))],
        out_specs=[row(kf), row(kf), row(d)],
        out_shape=[jax.ShapeDtypeStruct((t, kf), BF), jax.ShapeDtypeStruct((t, kf), BF),
                   jax.ShapeDtypeStruct((t, d), BF)],
        compiler_params=_params(("parallel",)),
    )(dx, w)


def _loss_grad(y, target, name):
    t, d = y.shape
    tm = _tile(t, TOKEN_TILE)

    def body(y_ref, t_ref, l_ref, dy_ref):
        @pl.when(pl.program_id(0) == 0)
        def _():
            l_ref[...] = jnp.zeros_like(l_ref)

        diff = y_ref[...] - t_ref[...]
        dy_ref[...] = diff * (1.0 / d)
        part = jnp.sum(jnp.sum(diff * diff, axis=1, keepdims=True), axis=0, keepdims=True)
        l_ref[...] += part * (0.5 / d)

    row = pl.BlockSpec((tm, d), lambda i: (i, 0))
    return pl.pallas_call(
        body, name=name, grid=(t // tm,),
        in_specs=[row, row], out_specs=[_full((1, 1)), row],
        out_shape=[jax.ShapeDtypeStruct((1, 1), F32), jax.ShapeDtypeStruct((t, d), F32)],
        compiler_params=_params(("arbitrary",)),
    )(y, target)


def _head_sum_matrix(width, head):
    r = lax.broadcasted_iota(jnp.int32, (width, width), 0) // head
    c = lax.broadcasted_iota(jnp.int32, (width, width), 1) // head
    return (r == c).astype(BF)


def _head_mean(x, bd):
    return _dot(x.astype(BF), bd) * (1.0 / FOX_HEAD_DIM)


def _mask_dot(mask01, x):
    mb = mask01.astype(BF)
    hi = x.astype(BF)
    r1 = x - hi.astype(F32)
    mid = r1.astype(BF)
    lo = (r1 - mid.astype(F32)).astype(BF)
    return _dot(mb, hi) + _dot(mb, mid) + _dot(mb, lo)


def _fox_prep(proj, wq_t, wk_t, bias_pad, seq, name):
    t = proj.shape[0]
    ts = _tile(seq, FOX_PREP_TILE)
    tpe = seq // ts
    scale = FOX_HEAD_DIM ** -0.5

    def body(q_ref, k_ref, v_ref, gt_ref, wq_ref, wk_ref, b_ref, qo_ref, ko_ref, vo_ref, cum_ref, carry):
        i = pl.program_id(0)
        bd = _head_sum_matrix(FOX_WIDTH, FOX_HEAD_DIM)

        def norm(xv, wv):
            ms = _head_mean(xv * xv, bd)
            return xv * lax.rsqrt(ms + EPS) * wv

        qo_ref[...] = (norm(q_ref[...], wq_ref[...]) * scale).astype(BF)
        ko_ref[...] = norm(k_ref[...], wk_ref[...]).astype(BF)
        vo_ref[...] = v_ref[...].astype(BF)

        @pl.when(i % tpe == 0)
        def _():
            carry[...] = jnp.zeros_like(carry)

        ls = _log_sigmoid(gt_ref[...] + b_ref[...])
        r = lax.broadcasted_iota(jnp.int32, (ts, ts), 0)
        c = lax.broadcasted_iota(jnp.int32, (ts, ts), 1)
        cum = _mask_dot(r >= c, ls) + carry[...]
        cum_ref[...] = cum
        carry[...] = cum[ts - 1:ts, :]

    blk = lambda j: pl.BlockSpec((ts, FOX_WIDTH), lambda i: (i, j))
    gate = pl.BlockSpec((ts, LANES), lambda i: (i, GATE_COL // LANES))
    out = pl.BlockSpec((ts, FOX_WIDTH), lambda i: (i, 0))
    return pl.pallas_call(
        body, name=name, grid=(t // ts,),
        in_specs=[blk(0), blk(1), blk(2), gate, _full((1, FOX_WIDTH)), _full((1, FOX_WIDTH)), _full((1, LANES))],
        out_specs=[out, out, out, pl.BlockSpec((ts, LANES), lambda i: (i, 0))],
        out_shape=[jax.ShapeDtypeStruct((t, FOX_WIDTH), BF)] * 3 + [jax.ShapeDtypeStruct((t, LANES), F32)],
        scratch_shapes=[pltpu.VMEM((1, LANES), F32)],
        compiler_params=_params(("arbitrary",)),
    )(proj, proj, proj, proj, wq_t, wk_t, bias_pad)


def _pick_head_sums(x):
    r = lax.broadcasted_iota(jnp.int32, (FOX_WIDTH, LANES), 0)
    c = lax.broadcasted_iota(jnp.int32, (FOX_WIDTH, LANES), 1)
    sel = (((r % LANES == FOX_HEAD_DIM) & (c == 2 * (r // LANES)))
           | ((r % LANES == 0) & (c == 2 * (r // LANES) + 1))).astype(BF)
    hi = x.astype(BF)
    r1 = x - hi.astype(F32)
    mid = r1.astype(BF)
    lo = (r1 - mid.astype(F32)).astype(BF)
    return _dot(hi, sel) + _dot(mid, sel) + _dot(lo, sel)


def _fox_prep_bwd(proj, dqa, dqb, dk, dv, dkx, wq_t, wk_t, bias_pad, seq, name):
    t = proj.shape[0]
    ts = _tile(seq, FOX_PREP_TILE)
    tpe = seq // ts
    nt = t // ts
    scale = FOX_HEAD_DIM ** -0.5

    def body(q_ref, k_ref, gt_ref, dqa_ref, dqb_ref, dk_ref, dv_ref, dc_ref, wq_ref, wk_ref, b_ref,
             dp_ref, dff_ref, dwq_ref, dwk_ref, db_ref, carry):
        i = pl.program_id(0)
        first = (lax.broadcasted_iota(jnp.int32, (ts, FOX_WIDTH), 1) % LANES) < FOX_HEAD_DIM
        dq_all = jnp.where(first, dqa_ref[...], dqb_ref[...])
        ti = nt - 1 - i
        bd = _head_sum_matrix(FOX_WIDTH, FOX_HEAD_DIM)

        @pl.when(i == 0)
        def _():
            dwq_ref[...] = jnp.zeros_like(dwq_ref)
            dwk_ref[...] = jnp.zeros_like(dwk_ref)
            db_ref[...] = jnp.zeros_like(db_ref)

        def norm_bwd(xv, wv, dyv):
            ms = _head_mean(xv * xv, bd)
            rstd = lax.rsqrt(ms + EPS)
            xhat = xv * rstd
            dxhat = dyv * wv
            mean = _head_mean(dxhat * xhat, bd)
            return rstd * (dxhat - xhat * mean), jnp.sum(dyv * xhat, axis=0, keepdims=True)

        dxq, dwq = norm_bwd(q_ref[...], wq_ref[...], dq_all * scale)
        dxk, dwk = norm_bwd(k_ref[...], wk_ref[...], dk_ref[...])
        dp_ref[:, 0:FOX_WIDTH] = dxq.astype(BF)
        dp_ref[:, FOX_WIDTH:2 * FOX_WIDTH] = dxk.astype(BF)
        dp_ref[:, 2 * FOX_WIDTH:3 * FOX_WIDTH] = dv_ref[...].astype(BF)
        dwq_ref[...] += dwq
        dwk_ref[...] += dwk

        @pl.when(ti % tpe == tpe - 1)
        def _():
            carry[...] = jnp.zeros_like(carry)

        r = lax.broadcasted_iota(jnp.int32, (ts, ts), 0)
        c = lax.broadcasted_iota(jnp.int32, (ts, ts), 1)
        dcum = _pick_head_sums(jnp.where(first, dqb_ref[...], dqa_ref[...]) - dc_ref[...])
        dls = _mask_dot(c >= r, dcum) + carry[...]
        carry[...] = dls[0:1, :]
        z = gt_ref[...] + b_ref[...]
        lane = lax.broadcasted_iota(jnp.int32, (ts, LANES), 1)
        dff = jnp.where(lane < FOX_HEADS, dls * _sigmoid(-z), 0.0)
        dff_ref[...] = dff
        db_ref[...] += jnp.sum(dff, axis=0, keepdims=True)

        @pl.when(i == nt - 1)
        def _():
            fr = lax.broadcasted_iota(jnp.int32, (FOX_WIDTH, FOX_WIDTH), 0) % FOX_HEAD_DIM
            fc = lax.broadcasted_iota(jnp.int32, (FOX_WIDTH, FOX_WIDTH), 1) % FOX_HEAD_DIM
            fold = (fr == fc).astype(F32)
            dwq_ref[...] = _dot(dwq_ref[...], fold, HI)
            dwk_ref[...] = _dot(dwk_ref[...], fold, HI)

    rev = lambda w, j: pl.BlockSpec((ts, w), lambda i: (nt - 1 - i, j))
    return pl.pallas_call(
        body, name=name, grid=(nt,),
        in_specs=[rev(FOX_WIDTH, 0), rev(FOX_WIDTH, 1), rev(LANES, GATE_COL // LANES),
                  rev(FOX_WIDTH, 0), rev(FOX_WIDTH, 0), rev(FOX_WIDTH, 0), rev(FOX_WIDTH, 0), rev(FOX_WIDTH, 0),
                  _full((1, FOX_WIDTH)), _full((1, FOX_WIDTH)), _full((1, LANES))],
        out_specs=[rev(3 * FOX_WIDTH, 0), rev(LANES, 0), _full((1, FOX_WIDTH)), _full((1, FOX_WIDTH)),
                   _full((1, LANES))],
        out_shape=[jax.ShapeDtypeStruct((t, 3 * FOX_WIDTH), BF), jax.ShapeDtypeStruct((t, LANES), F32),
                   jax.ShapeDtypeStruct((1, FOX_WIDTH), F32), jax.ShapeDtypeStruct((1, FOX_WIDTH), F32),
                   jax.ShapeDtypeStruct((1, LANES), F32)],
        scratch_shapes=[pltpu.VMEM((1, LANES), F32)],
        compiler_params=_params(("arbitrary",)),
    )(proj, proj, proj, dqa, dqb, dk, dv, dkx, wq_t, wk_t, bias_pad)


class _Rider:
    def __init__(self, inputs, out_shapes, sems, phases):
        self.inputs, self.out_shapes, self.sems, self.phases = list(inputs), list(out_shapes), list(sems), phases


def _rider_parts(rider):
    if rider is None:
        return [], [], []
    return rider.inputs, rider.out_shapes, rider.sems


def _ride(rider, which, when, refs):
    if rider is not None:
        @pl.when(when)
        def _():
            rider.phases[which](*refs)


def _fox_attn(q, k, v, ck, nb, seq, name, rider=None):
    t = q.shape[0]
    tq = _tile(seq, ATTN_FWD_TILE)
    nq = seq // tq
    npair = FOX_HEADS // 2
    hd = FOX_HEAD_DIM
    r_in, r_out, r_sem = _rider_parts(rider)
    steps = nb * npair * nq

    def body(q_ref, k_ref, v_ref, ck_ref, *rest):
        rin, (o_ref, lse_ref) = rest[:len(r_in)], rest[len(r_in):len(r_in) + 2]
        rout = rest[len(r_in) + 2:len(r_in) + 2 + len(r_out)]
        m_s, acc_s = rest[len(r_in) + 2 + len(r_out):len(r_in) + 4 + len(r_out)]
        riding = (rin, rout, rest[len(r_in) + 4 + len(r_out):])
        step = (pl.program_id(0) * npair + pl.program_id(1)) * nq + pl.program_id(2)
        _ride(rider, 0, step == 0, riding)
        _ride(rider, 1, step == (3 * steps) // 4, riding)
        qi = pl.program_id(2)
        lane = lax.broadcasted_iota(jnp.int32, (tq, LANES), 1)
        m_s[...] = jnp.full(m_s.shape, NEG, F32)
        acc_s[...] = jnp.zeros_like(acc_s)
        qv = q_ref[...]

        def block(kj, r0, nr, nc, on_diagonal):
            cols = pl.ds(pl.multiple_of(kj * tq, tq), nc)
            rows = slice(r0, r0 + nr)
            kv = k_ref[cols, :]
            vv = v_ref[cols, :]
            qr = qv[rows]
            lanes = lane[rows]
            if on_diagonal:
                causal = (r0 + lax.broadcasted_iota(jnp.int32, (nr, nc), 0)
                          >= lax.broadcasted_iota(jnp.int32, (nr, nc), 1))
            for hh in range(2):
                hm = (lanes >= hd) if hh else (lanes < hd)
                qh = jnp.where(hm, qr, jnp.zeros_like(qr))
                s = _dot_nt(qh, kv) - ck_ref[hh, :, cols]
                if on_diagonal:
                    s = jnp.where(causal, s, NEG)
                m_old = m_s[hh, rows]
                m_new = jnp.maximum(m_old, jnp.max(s, axis=-1, keepdims=True))
                p = jnp.exp(s - m_new)
                alpha = jnp.exp(m_old - m_new)
                m_s[hh, rows] = m_new
                vh = jnp.where(lane[:nc] >= hd if hh else lane[:nc] < hd, vv, jnp.ones_like(vv))
                acc_s[hh, rows] = alpha * acc_s[hh, rows] + _dot(p.astype(BF), vh)

        def off_diagonal(kj, carry):
            block(kj, 0, tq, tq, False)
            return carry

        lax.fori_loop(0, qi, off_diagonal, 0)
        strip = tq // DIAGONAL_STRIPS
        for i in range(DIAGONAL_STRIPS):
            block(qi, i * strip, strip, (i + 1) * strip, True)
        a0 = acc_s[0]
        a1 = acc_s[1]
        den = jnp.where(lane < hd, pltpu.roll(a0, hd, axis=1), pltpu.roll(a1, hd, axis=1))
        o_ref[...] = (jnp.where(lane < hd, a0, a1) / den).astype(o_ref.dtype)
        l0 = jnp.sum(jnp.where(lane == hd, a0, 0.0), axis=1, keepdims=True)
        l1 = jnp.sum(jnp.where(lane == 0, a1, 0.0), axis=1, keepdims=True)
        lse_ref[0] = m_s[0] + jnp.log(l0)
        lse_ref[1] = m_s[1] + jnp.log(l1)
        _ride(rider, 2, step == steps - 1, riding)

    qspec = pl.BlockSpec((tq, LANES), lambda b, p, i: (b * nq + i, p))
    kspec = pl.BlockSpec((seq, LANES), lambda b, p, i: (b, p))
    colspec = pl.BlockSpec((None, 2, tq, 1), lambda b, p, i: (b * npair + p, 0, i, 0))
    rowspec = pl.BlockSpec((None, 2, 1, seq), lambda b, p, i: (b * npair + p, 0, 0, 0))
    sem = ("arbitrary",) * 3 if rider else ("parallel",) * 3
    return pl.pallas_call(
        body, name=name, grid=(nb, npair, nq),
        in_specs=[qspec, kspec, kspec, rowspec] + [HBM] * len(r_in),
        out_specs=[qspec, colspec] + [HBM] * len(r_out),
        out_shape=[jax.ShapeDtypeStruct((t, FOX_WIDTH), BF), jax.ShapeDtypeStruct((nb * npair, 2, seq, 1), F32)]
        + r_out,
        scratch_shapes=[pltpu.VMEM((2, tq, 1), F32), pltpu.VMEM((2, tq, LANES), F32)] + r_sem,
        compiler_params=_params(sem, has_side_effects=rider is not None),
    )(q, k, v, ck, *r_in)


def _fox_attn_bwd(q, k, v, o, do, lse, ck, nb, seq, name, rider=None):
    t = q.shape[0]
    tq = _tile(seq, ATTN_BWD_TILE)
    nq = seq // tq
    npair = FOX_HEADS // 2
    hd = FOX_HEAD_DIM
    r_in, r_out, r_sem = _rider_parts(rider)
    steps = nb * npair * nq

    def body(q_ref, k_ref, v_ref, o_ref, do_ref, lse_ref, ck_ref, *rest):
        rin, (dqa_ref, dqb_ref, dk_ref, dv_ref, dkx_ref) = rest[:len(r_in)], rest[len(r_in):len(r_in) + 5]
        rout = rest[len(r_in) + 5:len(r_in) + 5 + len(r_out)]
        dk_s, dv_s = rest[len(r_in) + 5 + len(r_out):len(r_in) + 7 + len(r_out)]
        riding = (rin, rout, rest[len(r_in) + 7 + len(r_out):])
        step = (pl.program_id(0) * npair + pl.program_id(1)) * nq + pl.program_id(2)
        _ride(rider, 0, step == 0, riding)
        _ride(rider, 1, step == (3 * steps) // 4, riding)
        kj = pl.program_id(2)
        lane = lax.broadcasted_iota(jnp.int32, (tq, LANES), 1)

        @pl.when(kj == 0)
        def _():
            dqa_ref[...] = jnp.zeros_like(dqa_ref)
            dqb_ref[...] = jnp.zeros_like(dqb_ref)

        dk_s[...] = jnp.zeros_like(dk_s)
        dv_s[...] = jnp.zeros_like(dv_s)
        kv = k_ref[...]
        vv = v_ref[...]

        def block(qi, r0, nr, nc, on_diagonal):
            rows = pl.ds(pl.multiple_of(qi * tq, tq) + r0, nr)
            keys = slice(0, nc)
            qv = q_ref[rows, :]
            dov = do_ref[rows, :]
            kc, vc = kv[keys], vv[keys]
            prod = dov.astype(F32) * o_ref[rows, :].astype(F32)
            lq, lk = lane[:nr], lane[:nc]
            if on_diagonal:
                causal = (r0 + lax.broadcasted_iota(jnp.int32, (nr, nc), 0)
                          >= lax.broadcasted_iota(jnp.int32, (nr, nc), 1))
            for hh, dq_ref in ((0, dqa_ref), (1, dqb_ref)):
                hm = (lq >= hd) if hh else (lq < hd)
                hk = (lk >= hd) if hh else (lk < hd)
                zero = jnp.zeros_like(qv)
                doh = jnp.where(hm, dov, zero)
                delta = jnp.sum(jnp.where(hm, prod, 0.0), axis=-1, keepdims=True)
                s = _dot_nt(jnp.where(hm, qv, zero), kc) - ck_ref[hh, :, keys]
                if on_diagonal:
                    s = jnp.where(causal, s, NEG)
                p = jnp.exp(s - lse_ref[hh, rows, :])
                dp = _dot_nt(doh, vc)
                dsb = (p * (dp - delta)).astype(BF)
                dv_s[keys] += _dot_tn(p.astype(BF), doh)
                dk_s[hh, keys] += _dot_tn(dsb, jnp.where(hm, qv, jnp.ones_like(qv)))
                dq_ref[rows, :] += _dot(dsb, jnp.where(hk, kc, jnp.ones_like(kc)))

        def off_diagonal(qi, carry):
            block(qi, 0, tq, tq, False)
            return carry

        strip = tq // DIAGONAL_STRIPS
        for i in range(DIAGONAL_STRIPS):
            block(kj, i * strip, strip, (i + 1) * strip, True)
        lax.fori_loop(kj + 1, nq, off_diagonal, 0)
        dk_ref[...] = jnp.where(lane < hd, dk_s[0], dk_s[1])
        dkx_ref[...] = jnp.where(lane < hd, dk_s[1], dk_s[0])
        dv_ref[...] = dv_s[...]
        _ride(rider, 2, step == steps - 1, riding)

    kspec = pl.BlockSpec((tq, LANES), lambda b, p, j: (b * nq + j, p))
    full_q = pl.BlockSpec((seq, LANES), lambda b, p, j: (b, p))
    colspec = pl.BlockSpec((None, 2, seq, 1), lambda b, p, j: (b * npair + p, 0, 0, 0))
    rowspec = pl.BlockSpec((None, 2, 1, tq), lambda b, p, j: (b * npair + p, 0, 0, j))
    sem = ("arbitrary",) * 3 if rider else ("parallel", "parallel", "arbitrary")
    return pl.pallas_call(
        body, name=name, grid=(nb, npair, nq),
        in_specs=[full_q, kspec, kspec, full_q, full_q, colspec, rowspec] + [HBM] * len(r_in),
        out_specs=[full_q, full_q, kspec, kspec, kspec] + [HBM] * len(r_out),
        out_shape=[jax.ShapeDtypeStruct((t, FOX_WIDTH), F32)] * 5 + r_out,
        scratch_shapes=[pltpu.VMEM((2, tq, LANES), F32), pltpu.VMEM((tq, LANES), F32)] + r_sem,
        compiler_params=_params(sem, has_side_effects=rider is not None),
    )(q, k, v, o, do, lse, ck, *r_in)


GDN_QKV = 3 * GDN_WIDTH
GDN_COL = 3 * FOX_WIDTH
GG_COL = GDN_COL + GDN_QKV
A_LANE = FOX_HEADS
B_LANE = FOX_HEADS + GDN_HEADS
HALO = 8


def _gate_lanes(ts):
    lane = lax.broadcasted_iota(jnp.int32, (ts, LANES), 1)
    return (lane >= A_LANE) & (lane < B_LANE), (lane >= B_LANE) & (lane < B_LANE + GDN_HEADS)


def _chunk_tri(ts, upper):
    r = lax.broadcasted_iota(jnp.int32, (ts, ts), 0)
    c = lax.broadcasted_iota(jnp.int32, (ts, ts), 1)
    same = (r // CHUNK) == (c // CHUNK)
    return (same & ((c >= r) if upper else (r >= c))).astype(F32)


def _shift_rows(x, edge, k, down):
    ts = x.shape[0]
    row = lax.broadcasted_iota(jnp.int32, (HALO, x.shape[1]), 0)
    if down:
        rolled = pltpu.roll(x, k, axis=0)
        patch = jnp.where(row < k, pltpu.roll(edge, k, axis=0), rolled[:HALO])
        return jnp.concatenate([patch, rolled[HALO:]], axis=0)
    rolled = pltpu.roll(x, ts - k, axis=0)
    patch = jnp.where(row >= HALO - k, pltpu.roll(edge, HALO - k, axis=0), rolled[ts - HALO:])
    return jnp.concatenate([rolled[:ts - HALO], patch], axis=0)


def _conv_silu(x, before, w):
    taps = [_shift_rows(x, before, CONV_WIDTH - 1 - kk, True) for kk in range(CONV_WIDTH - 1)] + [x]
    c = w[0:1, :] * taps[0]
    for kk in range(1, CONV_WIDTH):
        c = c + w[kk:kk + 1, :] * taps[kk]
    return taps, c, c * _sigmoid(c)


def _gdn_prep(proj, conv_w, a_pad, dt_pad, seq, name):
    t = proj.shape[0]
    ts = _tile(seq, GDN_PREP_TILE)
    tpe = seq // ts
    qscale = GDN_HEAD_DIM ** -0.5

    def body(x_ref, gt_ref, w_ref, a_ref, dt_ref, qo_ref, ko_ref, vo_ref, go_ref, tail):
        i = pl.program_id(0)
        xv = x_ref[...]
        before = jnp.where(i % tpe == 0, jnp.zeros((HALO, GDN_QKV), F32), tail[...])
        tail[...] = xv[ts - HALO:]
        _, _, s = _conv_silu(xv, before, w_ref[...])
        for h in range(GDN_HEADS):
            for base, ref, sc in ((0, qo_ref, qscale), (GDN_WIDTH, ko_ref, 1.0)):
                xh = s[:, base + h * LANES: base + (h + 1) * LANES]
                r = lax.rsqrt(jnp.sum(xh * xh, axis=-1, keepdims=True) + EPS)
                ref[:, h * LANES:(h + 1) * LANES] = (xh * (r * sc)).astype(BF)
        vo_ref[...] = s[:, 2 * GDN_WIDTH:].astype(BF)
        gate = gt_ref[...]
        g_raw = -jnp.exp(a_ref[...]) * _softplus(gate + dt_ref[...])
        gc = _mask_dot(_chunk_tri(ts, False), g_raw)
        is_a, is_b = _gate_lanes(ts)
        go_ref[...] = jnp.where(is_a, gc, jnp.where(is_b, _sigmoid(gate), 0.0))

    out = pl.BlockSpec((ts, GDN_WIDTH), lambda i: (i, 0))
    lanes = pl.BlockSpec((ts, LANES), lambda i: (i, 0))
    return pl.pallas_call(
        body, name=name, grid=(t // ts,),
        in_specs=[pl.BlockSpec((ts, GDN_QKV), lambda i: (i, GDN_COL // GDN_QKV)),
                  pl.BlockSpec((ts, LANES), lambda i: (i, GATE_COL // LANES)),
                  _full((CONV_WIDTH, GDN_QKV)), _full((1, LANES)), _full((1, LANES))],
        out_specs=[out, out, out, lanes],
        out_shape=[jax.ShapeDtypeStruct((t, GDN_WIDTH), BF)] * 3 + [jax.ShapeDtypeStruct((t, LANES), F32)],
        scratch_shapes=[pltpu.VMEM((HALO, GDN_QKV), F32)],
        compiler_params=_params(("arbitrary",)),
    )(proj, proj, conv_w, a_pad, dt_pad)


def _gdn_prep_bwd(proj, dq, dk, dv, dgates, dff, conv_w, a_pad, dt_pad, seq, name):
    t = proj.shape[0]
    ts = _tile(seq, GDN_PREP_TILE)
    tpe = seq // ts
    nt = t // ts
    qscale = GDN_HEAD_DIM ** -0.5
    hb = ts // HALO

    def body(x_ref, halo_ref, gt_ref, dq_ref, dk_ref, dv_ref, dgt_ref, dff_ref, w_ref, a_ref, dt_ref,
             dx_ref, dgo_ref, dw_ref, da_ref, ddt_ref, dsl, carry):
        i = pl.program_id(0)
        ti = nt - 1 - i

        @pl.when(i == 0)
        def _():
            dw_ref[...] = jnp.zeros_like(dw_ref)
            da_ref[...] = jnp.zeros_like(da_ref)
            ddt_ref[...] = jnp.zeros_like(ddt_ref)

        halo = halo_ref[...]
        before = jnp.where(ti % tpe == 0, jnp.zeros_like(halo), halo)
        w = w_ref[...]
        taps, c, s = _conv_silu(x_ref[...], before, w)
        for h in range(GDN_HEADS):
            for base, ref, sc in ((0, dq_ref, qscale), (GDN_WIDTH, dk_ref, 1.0)):
                lo = base + h * LANES
                xh = s[:, lo:lo + LANES]
                r = lax.rsqrt(jnp.sum(xh * xh, axis=-1, keepdims=True) + EPS)
                y = xh * r
                dy = ref[:, h * LANES:(h + 1) * LANES] * sc
                dsl[:, lo:lo + LANES] = r * (dy - y * jnp.sum(dy * y, axis=-1, keepdims=True))
        dsl[:, 2 * GDN_WIDTH:] = dv_ref[...]
        sg = _sigmoid(c)
        dc = dsl[...] * (sg * (1.0 + c * (1.0 - sg)))
        nxt = carry[...]
        after = jnp.where(ti % tpe == tpe - 1, jnp.zeros_like(nxt), nxt)
        carry[...] = dc[0:HALO, :]
        dx = w[CONV_WIDTH - 1:CONV_WIDTH, :] * dc
        for kk in range(CONV_WIDTH - 1):
            dx = dx + w[kk:kk + 1, :] * _shift_rows(dc, after, CONV_WIDTH - 1 - kk, False)
        dx_ref[...] = dx.astype(BF)
        for kk in range(CONV_WIDTH):
            dw_ref[kk:kk + 1, :] += jnp.sum(dc * taps[kk], axis=0, keepdims=True)
        gate = gt_ref[...]
        dgt = dgt_ref[...]
        is_a, is_b = _gate_lanes(ts)
        dg_raw = _mask_dot(_chunk_tri(ts, True), jnp.where(is_a, dgt, 0.0))
        z = gate + dt_ref[...]
        na = -jnp.exp(a_ref[...])
        dga = dg_raw * na * _sigmoid(z)
        beta = _sigmoid(gate)
        dgb = jnp.where(is_b, dgt * beta * (1.0 - beta), 0.0)
        dgo_ref[...] = (dff_ref[...] + dga + dgb).astype(BF)
        ddt_ref[...] += jnp.sum(dga, axis=0, keepdims=True)
        da_ref[...] += jnp.sum(dg_raw * na * _softplus(z), axis=0, keepdims=True)

    rev = lambda wd, j: pl.BlockSpec((ts, wd), lambda i: (nt - 1 - i, j))
    halo_spec = pl.BlockSpec((HALO, GDN_QKV), lambda i: (jnp.maximum((nt - 1 - i) * hb - 1, 0), GDN_COL // GDN_QKV))
    return pl.pallas_call(
        body, name=name, grid=(nt,),
        in_specs=[rev(GDN_QKV, GDN_COL // GDN_QKV), halo_spec, rev(LANES, GATE_COL // LANES),
                  rev(GDN_WIDTH, 0), rev(GDN_WIDTH, 0), rev(GDN_WIDTH, 0), rev(LANES, 0), rev(LANES, 0),
                  _full((CONV_WIDTH, GDN_QKV)), _full((1, LANES)), _full((1, LANES))],
        out_specs=[rev(GDN_QKV, 0), rev(LANES, 0), _full((CONV_WIDTH, GDN_QKV)), _full((1, LANES)),
                   _full((1, LANES))],
        out_shape=[jax.ShapeDtypeStruct((t, GDN_QKV), BF), jax.ShapeDtypeStruct((t, LANES), BF),
                   jax.ShapeDtypeStruct((CONV_WIDTH, GDN_QKV), F32), jax.ShapeDtypeStruct((1, LANES), F32),
                   jax.ShapeDtypeStruct((1, LANES), F32)],
        scratch_shapes=[pltpu.VMEM((ts, GDN_QKV), F32), pltpu.VMEM((HALO, GDN_QKV), F32)],
        compiler_params=_params(("arbitrary",)),
    )(proj, proj, proj, dq, dk, dv, dgates, dff, conv_w, a_pad, dt_pad)


PAIR = 2 * CHUNK


def _split_bf16(a):
    hi = a.astype(BF)
    return hi, (a - hi.astype(F32)).astype(BF)


def _dot3(a, b, dims=(((1,), (0,)), ((), ()))):
    ah, al = _split_bf16(a)
    bh, bl = _split_bf16(b)
    (ca,), (cb,) = dims[0]
    return lax.dot_general(jnp.concatenate([ah, al, ah], axis=ca), jnp.concatenate([bh, bh, bl], axis=cb), dims,
                           preferred_element_type=F32)


def _inv_unit_lower(a):
    r = lax.broadcasted_iota(jnp.int32, (PAIR, PAIR), 0)
    c = lax.broadcasted_iota(jnp.int32, (PAIR, PAIR), 1)
    tm = (r == c).astype(F32) - a
    pw = _dot3(a, a)
    for _ in range(4):
        x = _dot3(jnp.concatenate([tm, pw], axis=0), pw)
        tm = tm + x[:PAIR]
        pw = x[PAIR:]
    return tm + _dot3(tm, pw)


def _gdn_pair_local(q, k, v, gc, gr, b):
    r = lax.broadcasted_iota(jnp.int32, (PAIR, PAIR), 0)
    c = lax.broadcasted_iota(jnp.int32, (PAIR, PAIR), 1)
    same = (r // CHUNK) == (c // CHUNK)
    incl = same & (r >= c)
    strict = same & (r > c)
    dm = jnp.exp(jnp.where(incl, gc - gr, NEG))
    e = jnp.exp(gc)
    kb = k * b
    vb = v * b
    kbe = kb * e
    kq = _dot_nt(jnp.concatenate([kb, q], axis=0).astype(BF), k.astype(BF))
    amat = jnp.where(strict, kq[:PAIR] * dm, 0.0)
    pmat = jnp.where(incl, kq[PAIR:] * dm, 0.0)
    lane = lax.broadcasted_iota(jnp.int32, (1, PAIR), 1)
    gl_a = jnp.sum(jnp.where(lane == CHUNK - 1, gr, 0.0), axis=1, keepdims=True)
    gl_b = jnp.sum(jnp.where(lane == PAIR - 1, gr, 0.0), axis=1, keepdims=True)
    ridx = lax.broadcasted_iota(jnp.int32, (PAIR, 1), 0)
    edec = jnp.exp(jnp.where(ridx < CHUNK, gl_a, gl_b) - gc)
    return dict(dm=dm, e=e, kb=kb, vb=vb, kbe=kbe, amat=amat, pmat=pmat, gl_a=gl_a, gl_b=gl_b, edec=edec,
                kd=k * edec, qd=q * e, incl=incl, strict=strict, ridx=ridx)


def _gdn_pair_states(loc, tb, s_a):
    uw = _dot(tb, jnp.concatenate([loc["vb"], loc["kbe"]], axis=1).astype(BF))
    u, w = uw[:, :LANES], uw[:, LANES:]
    qd, kd, c = loc["qd"], loc["kd"], CHUNK
    xa = _dot(jnp.concatenate([qd[:c], w[:c]], axis=0).astype(BF), s_a.astype(BF))
    vn_a = u[:c] - xa[c:]
    s_b = s_a * jnp.exp(loc["gl_a"]) + _dot_tn(kd[:c].astype(BF), vn_a.astype(BF))
    xb = _dot(jnp.concatenate([qd[c:], w[c:]], axis=0).astype(BF), s_b.astype(BF))
    vn_b = u[c:] - xb[c:]
    s_c = s_b * jnp.exp(loc["gl_b"]) + _dot_tn(kd[c:].astype(BF), vn_b.astype(BF))
    vn = jnp.concatenate([vn_a, vn_b], axis=0)
    o = jnp.concatenate([xa[:c], xb[:c]], axis=0) + _dot(loc["pmat"].astype(BF), vn.astype(BF))
    return w, vn, o, s_b, s_c


GDN_SEG = 512


def _gdn_specs(nb, seq, reverse):
    n = seq // CHUNK
    seg = _tile(seq, GDN_SEG)
    nseg = seq // seg
    sp = seg // PAIR
    at = (lambda s: nseg - 1 - s) if reverse else (lambda s: s)
    blk = pl.BlockSpec((nb, seg, GDN_WIDTH), lambda s: (0, at(s), 0))
    gg = pl.BlockSpec((nb, seg, GDN_WIDTH), lambda s: (0, at(s), GG_COL // GDN_WIDTH))
    gates = pl.BlockSpec((nb, seg, LANES), lambda s: (0, at(s), 0))
    rowb = pl.BlockSpec((nb, GDN_HEADS, sp, HALO, PAIR), lambda s: (0, 0, at(s), 0, 0))
    per_pair = pl.BlockSpec((nb, GDN_HEADS, sp, PAIR, PAIR), lambda s: (0, 0, at(s), 0, 0))
    return n, seg, nseg, sp, blk, gg, gates, rowb, per_pair


def _head_column(gt, lane, index):
    return jnp.sum(jnp.where(lane == index, gt, 0.0), axis=1, keepdims=True)


def _gdn_head_inputs(qkv_refs, gt_ref, gr_ref, rows, pi, lane, chains):
    per_chain = []
    for b, hh in chains:
        gt = gt_ref[b, rows, :]
        cols = slice(hh * LANES, (hh + 1) * LANES)
        per_chain.append([r[b, rows, cols].astype(F32) for r in qkv_refs]
                         + [_head_column(gt, lane, A_LANE + hh), gr_ref[b, hh, pi][0:1, :],
                            _head_column(gt, lane, B_LANE + hh)])
    return [jnp.stack(xs) for xs in zip(*per_chain)]


def _gdn_pair_fwd(qv, kv, vv, gcv, gr, bv, s_a):
    loc = _gdn_pair_local(qv, kv, vv, gcv, gr, bv)
    tf = _inv_unit_lower(loc["amat"])
    _, _, o, _, s_c = _gdn_pair_states(loc, tf.astype(BF), s_a)
    return tf, o, s_c


def _gdn_fwd(q, k, v, proj, gates, grow, wn, nb, seq, name):
    n, seg, nseg, sp, blk, gg, gates_spec, rowb, per_pair = _gdn_specs(nb, seq, False)
    chains = [(b, hh) for b in range(nb) for hh in range(GDN_HEADS)]

    def body(q_ref, k_ref, v_ref, gg_ref, gt_ref, gr_ref, wn_ref, y_ref, tn_ref, sn_ref, s_ref):
        @pl.when(pl.program_id(0) == 0)
        def _():
            s_ref[...] = jnp.zeros_like(s_ref)

        wnv = wn_ref[...]
        lane = lax.broadcasted_iota(jnp.int32, (PAIR, LANES), 1)

        def step(pi, carry):
            rows = pl.ds(pl.multiple_of(pi * PAIR, PAIR), PAIR)
            ins = _gdn_head_inputs((q_ref, k_ref, v_ref), gt_ref, gr_ref, rows, pi, lane, chains)
            s_a = s_ref[...]
            tf, o, s_c = jax.vmap(_gdn_pair_fwd)(*ins, s_a)
            s_ref[...] = s_c
            for c, (b, hh) in enumerate(chains):
                cols = slice(hh * LANES, (hh + 1) * LANES)
                tn_ref[b, hh, pi] = tf[c]
                sn_ref[b, hh, pi] = s_a[c]
                g = gg_ref[b, rows, cols]
                oh = o[c]
                rstd = lax.rsqrt(jnp.mean(oh * oh, axis=-1, keepdims=True) + EPS)
                y_ref[b, rows, cols] = (oh * rstd * wnv * (g * _sigmoid(g))).astype(BF)
            return carry

        lax.fori_loop(0, sp, step, 0)

    saved = jax.ShapeDtypeStruct((nb, GDN_HEADS, n // 2, PAIR, PAIR), F32)
    return pl.pallas_call(
        body, name=name, grid=(nseg,),
        in_specs=[blk, blk, blk, gg, gates_spec, rowb, _full((1, LANES))],
        out_specs=[blk, per_pair, per_pair],
        out_shape=[jax.ShapeDtypeStruct((nb, seq, GDN_WIDTH), BF), saved, saved],
        scratch_shapes=[pltpu.VMEM((len(chains), GDN_HEAD_DIM, GDN_HEAD_DIM), F32)],
        compiler_params=_params(("arbitrary",)),
    )(q, k, v, proj, gates, grow, wn)


def _gdn_pair_bwd(qv, kv, vv, gcv, gr, bv, tf, s_a, dsp, g, dyv, wnv):
    c = CHUNK
    loc = _gdn_pair_local(qv, kv, vv, gcv, gr, bv)
    tm = tf.astype(BF)
    kb, vb, kbe, e, dm = loc["kb"], loc["vb"], loc["kbe"], loc["e"], loc["dm"]
    kd, qd, pmat, amat = loc["kd"], loc["qd"], loc["pmat"], loc["amat"]
    w, vn, o, s_b, _ = _gdn_pair_states(loc, tm, s_a)
    sg = _sigmoid(g)
    silu = g * sg
    rstd = lax.rsqrt(jnp.mean(o * o, axis=-1, keepdims=True) + EPS)
    xhat = o * rstd
    dwn = jnp.sum(dyv * xhat * silu, axis=0, keepdims=True)
    dgg = dyv * xhat * wnv * (sg * (1.0 + g * (1.0 - sg)))
    dxhat = dyv * wnv * silu
    do = rstd * (dxhat - xhat * jnp.mean(dxhat * xhat, axis=-1, keepdims=True))
    dob = do.astype(BF)
    tot = lambda x: jnp.sum(jnp.sum(x, axis=1, keepdims=True), axis=0, keepdims=True)
    rsum = lambda x: jnp.sum(x, axis=1, keepdims=True)
    cat = lambda xs, ax=0: jnp.concatenate(xs, axis=ax)
    wb = w.astype(BF)
    qdb = qd.astype(BF)
    kdb = kd.astype(BF)
    vnb = vn.astype(BF)
    egl_a = jnp.exp(loc["gl_a"])
    egl_b = jnp.exp(loc["gl_b"])
    ptdo = _dot_tn(pmat.astype(BF), dob)
    dspb = dsp.astype(BF)
    dvn_b = ptdo[c:] + _dot(kdb[c:], dspb)
    dkd_b = _dot_nt(vnb[c:], dspb)
    dgl_b = egl_b * tot(s_b * dsp) + tot(dkd_b * kd[c:])
    dsm = egl_b * dsp + _dot_tn(cat([qdb[c:], -wb[c:]]), cat([dob[c:], dvn_b.astype(BF)]))
    dsmb = dsm.astype(BF)
    dvn_a = ptdo[:c] + _dot(kdb[:c], dsmb)
    dkd_a = _dot_nt(vnb[:c], dsmb)
    dgl_a = egl_a * tot(s_a * dsm) + tot(dkd_a * kd[:c])
    ds_new = egl_a * dsm + _dot_tn(cat([qdb[:c], -wb[:c]]), cat([dob[:c], dvn_a.astype(BF)]))
    ya = _dot_nt(cat([dob[:c], dvn_a.astype(BF)]), s_a.astype(BF))
    yb = _dot_nt(cat([dob[c:], dvn_b.astype(BF)]), s_b.astype(BF))
    dqd = cat([ya[:c], yb[:c]])
    dw = -cat([ya[c:], yb[c:]])
    dvn = cat([dvn_a, dvn_b])
    dkd = cat([dkd_a, dkd_b])
    dq = dqd * e
    dgc = rsum(dqd * qd) - rsum(dkd * kd)
    dk = dkd * loc["edec"]
    dpm = jnp.where(loc["incl"], _dot_nt(dob, vnb), 0.0)
    duw = cat([dvn, dw], 1).astype(BF)
    dt = _dot_nt(duw, cat([vb, kbe], 1).astype(BF))
    tt = _dot_tn(tm, duw)
    dvb, dkbe = tt[:, :LANES], tt[:, LANES:]
    tn_dims = (((0,), (0,)), ((), ()))
    nt_dims = (((1,), (1,)), ((), ()))
    da = jnp.where(loc["strict"], -_dot3(_dot3(tf, dt, tn_dims), tf, nt_dims), 0.0)
    st = cat([da * dm, dpm * dm]).astype(BF)
    z = _dot(st, kv.astype(BF))
    dkb = z[:PAIR] + dkbe * e
    dq = dq + z[PAIR:]
    dk = dk + _dot_tn(st, cat([kb, qv]).astype(BF))
    gmat = dpm * pmat + da * amat
    dgc = dgc + rsum(dkbe * kbe) + rsum(gmat)
    ridx = loc["ridx"]
    dgc = dgc + jnp.where(ridx == c - 1, dgl_a, 0.0) + jnp.where(ridx == PAIR - 1, dgl_b, 0.0)
    dgc_row = jnp.sum(gmat, axis=0, keepdims=True)
    db = rsum(dvb * vv) + rsum(dkb * kv)
    return dq, dk + dkb * bv, dvb * bv, dgg, dgc, dgc_row, db, dwn, ds_new


def _gdn_bwd(q, k, v, proj, gates, grow, wn, tinv_all, states_all, dy, nb, seq, name):
    n, seg, nseg, sp, blk, gg, gates_spec, rowb, per_pair = _gdn_specs(nb, seq, True)
    dh = GDN_HEAD_DIM
    chains = [(b, hh) for b in range(nb) for hh in range(GDN_HEADS)]

    def body(q_ref, k_ref, v_ref, gg_ref, gt_ref, gr_ref, wn_ref, tn_ref, sn_ref, dy_ref,
             dq_ref, dk_ref, dv_ref, dgg_ref, dgt_ref, dwn_ref, ds_ref):
        @pl.when(pl.program_id(0) == 0)
        def _():
            dwn_ref[...] = jnp.zeros_like(dwn_ref)
            ds_ref[...] = jnp.zeros_like(ds_ref)

        wnv = wn_ref[...]
        lane = lax.broadcasted_iota(jnp.int32, (PAIR, LANES), 1)

        def bwd_step(j, carry):
            pi = sp - 1 - j
            rows = pl.ds(pl.multiple_of(pi * PAIR, PAIR), PAIR)
            ins = _gdn_head_inputs((q_ref, k_ref, v_ref), gt_ref, gr_ref, rows, pi, lane, chains)
            lanes_of = lambda hh: slice(hh * LANES, (hh + 1) * LANES)
            saved = [jnp.stack([r[b, hh, pi] for b, hh in chains]) for r in (tn_ref, sn_ref)]
            g2 = jnp.stack([gg_ref[b, rows, lanes_of(hh)] for b, hh in chains])
            dy2 = jnp.stack([dy_ref[b, rows, lanes_of(hh)].astype(F32) for b, hh in chains])
            dq, dk, dv, dgg, dgc, dgc_row, db, dwn, ds_new = jax.vmap(
                _gdn_pair_bwd, in_axes=(0,) * 11 + (None,))(*ins, *saved, ds_ref[...], g2, dy2, wnv)
            ds_ref[...] = ds_new
            dgt = [jnp.zeros((PAIR, LANES), F32) for _ in range(nb)]
            for c, (b, hh) in enumerate(chains):
                cols = lanes_of(hh)
                dq_ref[b, rows, cols] = dq[c]
                dk_ref[b, rows, cols] = dk[c]
                dv_ref[b, rows, cols] = dv[c]
                dgg_ref[b, rows, cols] = dgg[c].astype(BF)
                dwn_ref[...] += dwn[c]
                row_as_col = jnp.transpose(jnp.broadcast_to(dgc_row[c], (PAIR, LANES)))
                dgt[b] = (dgt[b] + jnp.where(lane == A_LANE + hh, dgc[c] - row_as_col, 0.0)
                          + jnp.where(lane == B_LANE + hh, db[c], 0.0))
            for b in range(nb):
                dgt_ref[b, rows, :] = dgt[b]
            return carry

        lax.fori_loop(0, sp, bwd_step, 0)

    f32_out = jax.ShapeDtypeStruct((nb, seq, GDN_WIDTH), F32)
    return pl.pallas_call(
        body, name=name, grid=(nseg,),
        in_specs=[blk, blk, blk, gg, gates_spec, rowb, _full((1, LANES)), per_pair, per_pair, blk],
        out_specs=[blk, blk, blk, blk, gates_spec, _full((1, LANES))],
        out_shape=[f32_out, f32_out, f32_out, jax.ShapeDtypeStruct((nb, seq, GDN_WIDTH), BF),
                   jax.ShapeDtypeStruct((nb, seq, LANES), F32), jax.ShapeDtypeStruct((1, LANES), F32)],
        scratch_shapes=[pltpu.VMEM((len(chains), dh, dh), F32)],
        compiler_params=_params(("arbitrary",)),
    )(q, k, v, proj, gates, grow, wn, tinv_all, states_all, dy)


def _mix_to_padded(w):
    pad = jnp.zeros(w.shape[:-1] + (N_PAD - N_IN,), w.dtype)
    return jnp.concatenate([w[..., 0:1536], w[..., 1544:3080], w[..., 3088:3600], w[..., 1536:1544],
                            w[..., 3080:3088], pad], axis=-1)


def _pad_lanes(vec, start):
    return jnp.pad(vec[None, :], ((0, 0), (start, LANES - start - vec.shape[0])))


def _heads_to_rows(block, lane0, nheads, nb, seq):
    return block[:, lane0:lane0 + nheads].reshape(nb, seq, nheads).transpose(0, 2, 1).reshape(nb * nheads, seq)


def _mixer_small(p, l):
    wq_t = jnp.tile(p["fox_q_norm"][l], FOX_HEADS)[None, :]
    wk_t = jnp.tile(p["fox_k_norm"][l], FOX_HEADS)[None, :]
    bias = _pad_lanes(p["fox_f_bias"][l], 0)
    a_pad = _pad_lanes(p["gdn_a_log"][l], A_LANE)
    dt_pad = _pad_lanes(p["gdn_dt_bias"][l], A_LANE)
    wn = p["gdn_out_norm"][l][None, :]
    return wq_t, wk_t, bias, a_pad, dt_pad, wn


def _layer_fwd(x, p, l, nb, seq, rider=None, ffn1_rider=None, after_ffn1=None):
    npair = FOX_HEADS // 2
    n = seq // CHUNK
    x1, h1, *rode1 = _ffn_fwd(x, p["ffn1_norm"][l][None, :], p["ffn1_w_in"][l], p["ffn1_w_out"][l],
                              f"ffn1_fwd_{l}", ffn1_rider)
    if after_ffn1 is not None:
        after_ffn1(rode1)
    wq_t, wk_t, bias, a_pad, dt_pad, wn = _mixer_small(p, l)
    proj = _norm_matmul(x1, p["mix_norm"][l][None, :], p["w_mix"][l], f"mix_in_{l}")
    fq, fk, fv, cum = _fox_prep(proj, wq_t, wk_t, bias, seq, f"fox_prep_{l}")
    c8 = _heads_to_rows(cum, 0, FOX_HEADS, nb, seq)
    ck = c8.reshape(nb * npair, 2, 1, seq)
    o, lse, *rode = _fox_attn(fq, fk, fv, ck, nb, seq, f"fox_attn_{l}", rider)
    gq, gk, gv, gates = _gdn_prep(proj, p["gdn_conv"][l], a_pad, dt_pad, seq, f"gdn_prep_{l}")
    gc4 = _heads_to_rows(gates, A_LANE, GDN_HEADS, nb, seq)
    grow = jnp.broadcast_to(gc4.reshape(nb, GDN_HEADS, n // 2, 1, PAIR), (nb, GDN_HEADS, n // 2, HALO, PAIR))
    per_example = lambda a: a.reshape(nb, seq, a.shape[-1])
    gq, gk, gv, gates = per_example(gq), per_example(gk), per_example(gv), per_example(gates)
    y, tinv, states = _gdn_fwd(gq, gk, gv, per_example(proj), gates, grow, wn, nb, seq, f"gdn_fwd_{l}")
    y = y.reshape(nb * seq, GDN_WIDTH)
    x2 = _mix_out(x1, o, y, p["w_out"][l], f"mix_out_{l}")
    x3, h2 = _ffn_fwd(x2, p["ffn2_norm"][l][None, :], p["ffn2_w_in"][l], p["ffn2_w_out"][l], f"ffn2_fwd_{l}")
    saved = dict(x=x, h1=h1, x1=x1, proj=proj, fq=fq, fk=fk, fv=fv, ck=ck, o=o, lse=lse,
                 gq=gq, gk=gk, gv=gv, gates=gates, grow=grow, tinv=tinv, states=states, y=y, x2=x2, h2=h2)
    return x3, saved, rode


def _ffn_grads(dy, x, h, gain, win, wout, l, tag, rider=None):
    t, d = x.shape
    fs = win.shape[2]
    dx, dh, a, hn, dyh, dgain, *rode = _ffn_bwd(dy, x, h, gain, win, wout, f"{tag}_bwd_{l}", rider)
    g_in = _wgrad(hn, dh, jax.ShapeDtypeStruct((4, d, fs), BF),
                  pl.BlockSpec((None, d, fs), lambda i, j, k: (j, i, 0)), d, fs, f"{tag}_gw_in_{l}")
    g_out = _wgrad(a, dyh, jax.ShapeDtypeStruct((2 * fs, d), BF),
                   pl.BlockSpec((fs, d), lambda i, j, k: (i, j)), fs, d, f"{tag}_gw_out_{l}")
    return dx, dgain[0], g_in, g_out.reshape(4, fs // 2, d), rode


def _layer_bwd(dx3, p, l, sv, nb, seq, rider=None, before_ffn1=None, ffn2_rider=None, after_ffn2=None):
    npair = FOX_HEADS // 2
    d = dx3.shape[1]
    wq_t, wk_t, bias, a_pad, dt_pad, wn = _mixer_small(p, l)
    g = {}
    dx2, g["ffn2_norm"], g["ffn2_w_in"], g["ffn2_w_out"], rode2 = _ffn_grads(
        dx3, sv["x2"], sv["h2"], p["ffn2_norm"][l][None, :], p["ffn2_w_in"][l], p["ffn2_w_out"][l], l, "ffn2",
        ffn2_rider)
    if after_ffn2 is not None:
        rider = after_ffn2(rode2)
    dyf, dyg, dxb = _mix_out_bwd(dx2, p["w_out"][l], f"mix_out_bwd_{l}")
    half = lambda a, nm: _wgrad(a, dxb, jax.ShapeDtypeStruct((FOX_WIDTH, d), BF),
                                pl.BlockSpec((FOX_WIDTH, d), lambda i, j, k: (i, j)), FOX_WIDTH, d, nm)
    g["w_out"] = jnp.concatenate([half(sv["o"], f"gw_out_fox_{l}"), half(sv["y"], f"gw_out_gdn_{l}")], axis=0)
    dqa, dqb, dk, dv, dkx, *rode = _fox_attn_bwd(sv["fq"], sv["fk"], sv["fv"], sv["o"], dyf, sv["lse"], sv["ck"],
                                                 nb, seq, f"fox_attn_bwd_{l}", rider)

    dpf, dff, dwq, dwk, dbias = _fox_prep_bwd(sv["proj"], dqa, dqb, dk, dv, dkx, wq_t, wk_t, bias, seq,
                                              f"fox_prep_bwd_{l}")
    g["fox_q_norm"] = dwq[0, :FOX_HEAD_DIM]
    g["fox_k_norm"] = dwk[0, :FOX_HEAD_DIM]
    g["fox_f_bias"] = dbias[0, :FOX_HEADS]
    per_example = lambda a: a.reshape(nb, seq, a.shape[-1])
    flat = lambda a: a.reshape(nb * seq, a.shape[-1])
    dgq, dgk, dgv, dgg, dgates, dwn = _gdn_bwd(
        sv["gq"], sv["gk"], sv["gv"], per_example(sv["proj"]), sv["gates"], sv["grow"], wn, sv["tinv"],
        sv["states"], per_example(dyg), nb, seq, f"gdn_bwd_{l}")
    dgq, dgk, dgv, dgg, dgates = flat(dgq), flat(dgk), flat(dgv), flat(dgg), flat(dgates)
    dpg, dgate_blk, dconv, da, ddt = _gdn_prep_bwd(sv["proj"], dgq, dgk, dgv, dgates, dff, p["gdn_conv"][l],
                                                   a_pad, dt_pad, seq, f"gdn_prep_bwd_{l}")
    g["gdn_conv"] = dconv
    g["gdn_a_log"] = da[0, A_LANE:B_LANE]
    g["gdn_dt_bias"] = ddt[0, A_LANE:B_LANE]
    g["gdn_out_norm"] = dwn[0]
    dparts = [dpf, dpg, dgg, dgate_blk]
    dx1, hnm, dgm = _norm_matmul_bwd(dx2, dparts, sv["x1"], p["mix_norm"][l][None, :], p["w_mix"][l],
                                     f"mix_in_bwd_{l}")
    g["mix_norm"] = dgm[0]
    gp = _wgrad_parts(hnm, dparts, d // 2, f"gw_mix_{l}")
    gate = GATE_COL
    g["w_in"] = jnp.concatenate([gp[:, :GDN_COL], gp[:, gate:gate + FOX_HEADS], gp[:, GDN_COL:GG_COL],
                                 gp[:, gate + A_LANE:gate + B_LANE + GDN_HEADS], gp[:, GG_COL:gate]], axis=1)
    ffn1_rider = before_ffn1(g) if before_ffn1 is not None else None
    dx0, g["ffn1_norm"], g["ffn1_w_in"], g["ffn1_w_out"], rode1 = _ffn_grads(
        dx1, sv["x"], sv["h1"], p["ffn1_norm"][l][None, :], p["ffn1_w_in"][l], p["ffn1_w_out"][l], l, "ffn1",
        ffn1_rider)
    return dx0, g, rode, rode1


def _local_step(x, target, p):
    nb, seq, d = x.shape
    xt = x.reshape(nb * seq, d)
    saved = []
    for l in range(DEPTH):
        xt, sv, _ = _layer_fwd(xt, p, l, nb, seq)
        saved.append(sv)
    loss, dx = _loss_grad(xt, target.reshape(nb * seq, d), "loss")
    grads = [None] * DEPTH
    for l in reversed(range(DEPTH)):
        dx, grads[l], _, _ = _layer_bwd(dx, p, l, saved[l], nb, seq)
    return loss, dx.reshape(nb, seq, d), grads


N_CHIPS = 4


def _mesh_pos():
    return lax.axis_index("x"), lax.axis_index("y"), lax.axis_index("c")


def _other_chips(x, y):
    return [(1 - x, y), (x, 1 - y), (1 - x, 1 - y)]


def _remote(src, dst, send_sem, recv_sem, to):
    return pltpu.make_async_remote_copy(src_ref=src, dst_ref=dst, send_sem=send_sem, recv_sem=recv_sem,
                                        device_id=to, device_id_type=MESH)


def _hbm_call(body, name, ins, out_shape, scratch):
    return pl.pallas_call(
        body, name=name, out_shape=out_shape, in_specs=[HBM] * len(ins),
        out_specs=jax.tree.map(lambda _: HBM, out_shape), scratch_shapes=scratch,
        compiler_params=pltpu.CompilerParams(has_side_effects=True),
    )(*ins)


def _gather_phases(n, layer):
    def copies(ins, outs, sems):
        send1, recv1, send2, recv2 = sems
        x, y, c = _mesh_pos()
        out, back, fwd = [], [], []
        for i in range(n):
            for j, (px, py) in enumerate(_other_chips(x, y)):
                k = 3 * i + j
                blk = outs[i].at[2 * px + py]
                out.append(_remote(ins[i], outs[i].at[2 * x + y], send1.at[k], recv1.at[k], (px, py, c)))
                back.append(_remote(blk, blk, send1.at[k], recv1.at[k], (px, py, c)))
                fwd.append(_remote(blk, blk, send2.at[k], recv2.at[k], (x, y, 1 - c)))
        return c, out, back, fwd

    def first(ins, outs, sems):
        c, out, _, _ = copies(ins, outs, sems)

        @pl.when(c == layer)
        def _():
            for cp in out:
                cp.start()

    def middle(ins, outs, sems):
        c, _, back, fwd = copies(ins, outs, sems)

        @pl.when(c == layer)
        def _():
            for arrived, onward in zip(back, fwd):
                arrived.wait_recv()
                onward.start()

    def last(ins, outs, sems):
        c, out, _, fwd = copies(ins, outs, sems)

        @pl.when(c == layer)
        def _():
            for cp in out + fwd:
                cp.wait_send()

        @pl.when(c != layer)
        def _():
            for cp in fwd:
                cp.wait_recv()

    return first, middle, last


def _scatter_phases(n, layer):
    def copies(ins, outs, sems):
        send, recv = sems
        x, y, c = _mesh_pos()
        return c, [_remote(ins[i].at[2 * px + py], outs[i].at[j], send.at[3 * i + j], recv.at[3 * i + j], (px, py, c))
                   for i in range(n) for j, (px, py) in enumerate(_other_chips(x, y))]

    def first(ins, outs, sems):
        c, cps = copies(ins, outs, sems)

        @pl.when(c == layer)
        def _():
            for cp in cps:
                cp.start()

    def middle(ins, outs, sems):
        pass

    def last(ins, outs, sems):
        c, cps = copies(ins, outs, sems)

        @pl.when(c == layer)
        def _():
            for cp in cps:
                cp.wait()

    return first, middle, last


def _exchange(blocks, out_shapes, n_sems, phases, name, rider):
    sems = [pltpu.SemaphoreType.DMA((3 * len(blocks),))] * n_sems
    if rider:
        return _Rider(blocks, out_shapes, sems, phases)
    n = len(blocks)

    def body(*refs):
        for phase in phases:
            phase(refs[:n], refs[n:2 * n], refs[2 * n:])

    return list(_hbm_call(body, name, blocks, out_shapes, sems))


def _gather_layer(blocks, layer, name=None, rider=False):
    outs = [jax.ShapeDtypeStruct((N_CHIPS,) + b.shape, b.dtype) for b in blocks]
    return _exchange(blocks, outs, 4, _gather_phases(len(blocks), layer), name, rider)


def _scatter_layer(sums, layer, name=None, rider=False):
    outs = [jax.ShapeDtypeStruct((3,) + s.shape[1:], s.dtype) for s in sums]
    return _exchange(sums, outs, 2, _scatter_phases(len(sums), layer), name, rider)


def _to_sibling(gs, layer, name=None, rider=False):
    n = len(gs)

    def copies(ins, outs, sems):
        send, recv = sems
        x, y, c = _mesh_pos()
        return c, [_remote(ins[i], outs[i], send.at[i], recv.at[i], (x, y, 1 - c)) for i in range(n)]

    def first(ins, outs, sems):
        c, cps = copies(ins, outs, sems)

        @pl.when(c != layer)
        def _():
            for cp in cps:
                cp.start()

    def middle(ins, outs, sems):
        pass

    def last(ins, outs, sems):
        c, cps = copies(ins, outs, sems)

        @pl.when(c != layer)
        def _():
            for cp in cps:
                cp.wait_send()

        @pl.when(c == layer)
        def _():
            for cp in cps:
                cp.wait_recv()

    sems = [pltpu.SemaphoreType.DMA((n,))] * 2
    outs = [jax.ShapeDtypeStruct(g.shape, g.dtype) for g in gs]
    if rider:
        return _Rider(gs, outs, sems, (first, middle, last))

    def body(*refs):
        for phase in (first, middle, last):
            phase(refs[:n], refs[n:2 * n], refs[2 * n:])

    return list(_hbm_call(body, name, gs, outs, sems))


def _sibling_swap(rs, name):
    n = len(rs)

    def body(*refs):
        ins, outs = refs[:n], refs[n:2 * n]
        send, recv = refs[2 * n:]
        x, y, c = _mesh_pos()
        cps = [_remote(ins[i], outs[i], send.at[i], recv.at[i], (x, y, 1 - c)) for i in range(n)]
        for cp in cps:
            cp.start()
        for cp in cps:
            cp.wait()

    sem = pltpu.SemaphoreType.DMA((n,))
    return _hbm_call(body, name, rs, [jax.ShapeDtypeStruct(r.shape, r.dtype) for r in rs], [sem, sem])


def _small_all_reduce(vec, name):
    r = vec.shape[0]
    ndev = 8

    def body(v_ref, o_ref, buf, send, recv):
        x, y, c = _mesh_pos()
        me = 4 * x + 2 * y + c
        buf[me] = v_ref[...]
        cps = []
        for rel in range(1, ndev):
            px = 1 - x if rel & 4 else x
            py = 1 - y if rel & 2 else y
            pc = 1 - c if rel & 1 else c
            cps.append((_remote(v_ref, buf.at[me], send.at[rel - 1], recv.at[rel - 1], (px, py, pc)),
                        4 * px + 2 * py + pc))
        for cp, _ in cps:
            cp.start()
        for k, (cp, peer) in enumerate(cps):
            slot = buf.at[peer]
            _remote(slot, slot, send.at[k], recv.at[k], (x, y, c)).wait_recv()
        for cp, _ in cps:
            cp.wait_send()
        acc = buf[0]
        for k in range(1, ndev):
            acc = acc + buf[k]
        o_ref[...] = acc

    vm = pl.BlockSpec(memory_space=pltpu.VMEM)
    return pl.pallas_call(
        body, name=name, out_shape=jax.ShapeDtypeStruct(vec.shape, F32), in_specs=[vm], out_specs=vm,
        scratch_shapes=[pltpu.VMEM((ndev, r, LANES), F32), pltpu.SemaphoreType.DMA((ndev - 1,)),
                        pltpu.SemaphoreType.DMA((ndev - 1,))],
        compiler_params=pltpu.CompilerParams(has_side_effects=True),
    )(vec)


def _row_tile(rows, cap=SUM_ROWS):
    for t in range(min(rows, cap), 0, -1):
        if rows % t == 0 and (t % 16 == 0 or t == rows):
            return t
    raise ValueError(rows)


def _add_pairs(a, b, name):
    k, r, c = a.shape
    tr = _row_tile(r)

    def body(a_ref, b_ref, o_ref):
        o_ref[...] = (a_ref[...].astype(F32) + b_ref[...].astype(F32)).astype(o_ref.dtype)

    spec = pl.BlockSpec((None, tr, c), lambda i, j: (i, j, 0))
    return pl.pallas_call(body, name=name, grid=(k, r // tr), in_specs=[spec, spec], out_specs=spec,
                          out_shape=jax.ShapeDtypeStruct(a.shape, a.dtype),
                          compiler_params=_params(("parallel", "parallel")))(a, b)


def _final_sum(own, sib, others, name):
    r, c = own.shape
    tr = _row_tile(r)

    def body(a_ref, b_ref, o_ref_in, out_ref):
        acc = a_ref[...].astype(F32) + b_ref[...].astype(F32)
        for k in range(3):
            acc = acc + o_ref_in[k].astype(F32)
        out_ref[...] = acc

    spec = pl.BlockSpec((tr, c), lambda i: (i, 0))
    return pl.pallas_call(body, name=name, grid=(r // tr,),
                          in_specs=[spec, spec, pl.BlockSpec((3, tr, c), lambda i: (0, i, 0))], out_specs=spec,
                          out_shape=jax.ShapeDtypeStruct((r, c), F32),
                          compiler_params=_params(("parallel",)))(own, sib, others)


def _adamw(g, w, m, v, name):
    r, c = g.shape
    tr = _row_tile(r, ADAM_ROWS)

    def body(g_ref, w_ref, m_ref, v_ref, d_ref, mo_ref, vo_ref):
        gv = g_ref[...]
        mn = ADAM_B1 * m_ref[...] + (1.0 - ADAM_B1) * gv
        vn = ADAM_B2 * v_ref[...] + (1.0 - ADAM_B2) * (gv * gv)
        m_hat = mn / (1.0 - ADAM_B1 ** ADAM_STEP)
        v_hat = vn / (1.0 - ADAM_B2 ** ADAM_STEP)
        d_ref[...] = -ADAM_LR * (m_hat / (jnp.sqrt(v_hat) + ADAM_EPS) + ADAM_WD * w_ref[...])
        mo_ref[...] = mn
        vo_ref[...] = vn

    spec = pl.BlockSpec((tr, c), lambda i: (i, 0))
    shp = jax.ShapeDtypeStruct((r, c), F32)
    return pl.pallas_call(body, name=name, grid=(r // tr,), in_specs=[spec] * 4, out_specs=[spec] * 3,
                          out_shape=[shp] * 3, compiler_params=_params(("parallel",)))(g, w, m, v)


def _pack(arrays):
    flat = jnp.concatenate([a.reshape(-1).astype(F32) for a in arrays])
    pad = (-flat.shape[0]) % (8 * LANES)
    return jnp.concatenate([flat, jnp.zeros((pad,), F32)]).reshape(-1, LANES)


def _unpack(packed, shapes):
    flat = packed.reshape(-1)
    out, off = [], 0
    for s in shapes:
        size = 1
        for dim in s:
            size *= dim
        out.append(flat[off:off + size].reshape(s))
        off += size
    return out


BIG = ("ffn1_w_in", "ffn1_w_out", "w_in", "w_out", "ffn2_w_in", "ffn2_w_out")
SMALL = ("ffn1_norm", "mix_norm", "fox_q_norm", "fox_k_norm", "fox_f_bias", "gdn_a_log", "gdn_dt_bias",
         "gdn_out_norm", "ffn2_norm", "gdn_conv")
WEIGHTS = ("ffn1_norm", "ffn1_w_in", "ffn1_w_out", "mix_norm", "w_in", "fox_q_norm", "fox_k_norm", "fox_f_bias",
           "gdn_conv", "gdn_a_log", "gdn_dt_bias", "gdn_out_norm", "w_out", "ffn2_norm", "ffn2_w_in", "ffn2_w_out")


def _step(x, target, w, m, v):
    xi, yi, ci = _mesh_pos()
    me = 2 * xi + yi
    depth = DEPTH
    d = x.shape[-1]

    nb, seq, _ = x.shape
    assert depth == 2

    p ={k: w[k] for k in SMALL if k != "gdn_conv"}
    for k in ("ffn1_w_in", "ffn1_w_out", "ffn2_w_in", "ffn2_w_out", "w_mix", "w_out", "gdn_conv"):
        p[k] = [None] * depth

    first, rest = BIG[:2], BIG[2:] + ("gdn_conv",)

    def shards(l, names):
        return [w[k][l] if k == "gdn_conv" else w[k][l].astype(BF) for k in names]

    def place(l, names, gathered):
        blocks = dict(zip(names, [lax.dynamic_update_index_in_dim(g, s, me, 0)
                                  for g, s in zip(gathered, shards(l, names))]))
        for k in ("ffn1_w_in", "ffn1_w_out", "ffn2_w_in", "ffn2_w_out"):
            if k in blocks:
                p[k][l] = blocks[k]
        if "w_in" in blocks:
            p["w_mix"][l] = _mix_to_padded(blocks["w_in"].transpose(1, 0, 2).reshape(d, N_IN))
            p["w_out"][l] = blocks["w_out"].reshape(2 * FOX_WIDTH, d)
            p["gdn_conv"][l] = blocks["gdn_conv"].transpose(1, 0, 2).reshape(CONV_WIDTH, -1)

    place(0, first, _gather_layer(shards(0, first), 0, "gather_first_ffn0"))
    xt = x.reshape(nb * seq, d)
    xt, saved0, gathered1 = _layer_fwd(
        xt, p, 0, nb, seq, _gather_layer(shards(1, first + rest), 1, rider=True),
        _gather_layer(shards(0, rest), 0, rider=True), lambda got: place(0, rest, got))
    place(1, first + rest, gathered1)
    xt, saved1, _ = _layer_fwd(xt, p, 1, nb, seq)
    loss, dx = _loss_grad(xt, target.reshape(nb * seq, d), "loss")

    def transport(g, names):
        out = []
        for k in names:
            if k == "w_in":
                out.append(g["w_in"].reshape(d, N_CHIPS, N_IN // N_CHIPS).transpose(1, 0, 2).astype(BF))
            elif k == "w_out":
                out.append(g["w_out"].reshape(N_CHIPS, -1, d))
            else:
                out.append(g[k])
        return out

    def chip_sums(g, l, names, tag):
        own = transport(g, names)
        sib = _to_sibling(own, l, f"grad{l}{tag}_to_sibling")
        return own, sib, [_add_pairs(a, b, f"grad{l}{tag}_chip_sum_{k}") for a, b, k in zip(own, sib, names)]

    dx, grads1, _, _ = _layer_bwd(dx, p, 1, saved1, nb, seq)
    own1 = transport(grads1, BIG)
    before = {}

    def after_ffn2(from_sibling):
        before["sib1"] = from_sibling
        sums1 = [_add_pairs(a, b, f"grad1_chip_sum_{k}") for a, b, k in zip(own1, from_sibling, BIG)]
        return _scatter_layer(sums1, 1, rider=True)

    def before_ffn1(g):
        before["own"], before["sib"], sums = chip_sums(g, 0, BIG[2:], "_rest")
        return _scatter_layer(sums, 0, rider=True)

    dx, grads0, chips1, chips0_rest = _layer_bwd(dx, p, 0, saved0, nb, seq, None, before_ffn1,
                                                 _to_sibling(own1, 1, rider=True), after_ffn2)
    sib1 = before["sib1"]
    own0, sib0, sums0 = chip_sums(grads0, 0, first, "_first")
    chips0 = _scatter_layer(sums0, 0, "grad0_first_to_chips") + chips0_rest
    own0, sib0 = own0 + before["own"], sib0 + before["sib"]
    grads = [grads0, grads1]
    dx = dx.reshape(nb, seq, d)

    mine = lambda a0, a1: jnp.where(ci == 0, a0, a1)
    at_me = lambda a: lax.dynamic_index_in_dim(a, me, 0, keepdims=False)
    reduced = [_final_sum(mine(at_me(own0[i]), at_me(own1[i])), mine(at_me(sib0[i]), at_me(sib1[i])),
                          mine(chips0[i], chips1[i]), f"grad_final_sum_{k}") for i, k in enumerate(BIG)]
    from_sib_final = _sibling_swap(reduced, "grad_swap_layers")
    full = {k: jnp.stack([jnp.where(ci == 0, a, b), jnp.where(ci == 0, b, a)])
            for k, a, b in zip(BIG, reduced, from_sib_final)}

    out_g, out_d, out_m, out_v = {}, {}, {}, {}
    for k in BIG:
        shp = w[k].shape
        two_d = lambda a: a.reshape(shp[0] * shp[1], shp[2])
        dl, mn, vn = _adamw(two_d(full[k]), two_d(w[k]), two_d(m[k]), two_d(v[k]), f"adamw_{k}")
        out_g[k], out_d[k], out_m[k], out_v[k] = full[k], dl.reshape(shp), mn.reshape(shp), vn.reshape(shp)

    small_local = [jnp.stack([grads[l][k] for l in range(depth)]) for k in SMALL] + [loss.reshape(1)]
    summed = _unpack(_small_all_reduce(_pack(small_local), "small_all_reduce"), [a.shape for a in small_local])
    total = summed.pop()[0]
    sg = dict(zip(SMALL, summed))
    cs = w["gdn_conv"].shape[-1]
    sg["gdn_conv"] = lax.dynamic_slice_in_dim(sg["gdn_conv"], me * cs, cs, axis=2)
    shapes = [w[k].shape for k in SMALL]
    packs = [_pack([src[k] for k in SMALL]) for src in (sg, w, m, v)]
    dl, mn, vn = _adamw(*packs, "adamw_small")
    for k, a, b, c2 in zip(SMALL, _unpack(dl, shapes), _unpack(mn, shapes), _unpack(vn, shapes)):
        out_g[k], out_d[k], out_m[k], out_v[k] = sg[k], a, b, c2

    return (total, dx, *[out_g[k] for k in WEIGHTS], *[out_d[k] for k in WEIGHTS],
            *[out_m[k] for k in WEIGHTS], *[out_v[k] for k in WEIGHTS])


def kernel(x, ffn1_norm, ffn1_w_in, ffn1_w_out, mix_norm, w_in, fox_q_norm, fox_k_norm, fox_f_bias, gdn_conv, gdn_a_log, gdn_dt_bias, gdn_out_norm, w_out, ffn2_norm, ffn2_w_in, ffn2_w_out, loss_target, m_ffn1_norm, m_ffn1_w_in, m_ffn1_w_out, m_mix_norm, m_w_in, m_fox_q_norm, m_fox_k_norm, m_fox_f_bias, m_gdn_conv, m_gdn_a_log, m_gdn_dt_bias, m_gdn_out_norm, m_w_out, m_ffn2_norm, m_ffn2_w_in, m_ffn2_w_out, v_ffn1_norm, v_ffn1_w_in, v_ffn1_w_out, v_mix_norm, v_w_in, v_fox_q_norm, v_fox_k_norm, v_fox_f_bias, v_gdn_conv, v_gdn_a_log, v_gdn_dt_bias, v_gdn_out_norm, v_w_out, v_ffn2_norm, v_ffn2_w_in, v_ffn2_w_out):
    w = dict(ffn1_norm=ffn1_norm, ffn1_w_in=ffn1_w_in, ffn1_w_out=ffn1_w_out, mix_norm=mix_norm, w_in=w_in,
             fox_q_norm=fox_q_norm, fox_k_norm=fox_k_norm, fox_f_bias=fox_f_bias, gdn_conv=gdn_conv,
             gdn_a_log=gdn_a_log, gdn_dt_bias=gdn_dt_bias, gdn_out_norm=gdn_out_norm, w_out=w_out,
             ffn2_norm=ffn2_norm, ffn2_w_in=ffn2_w_in, ffn2_w_out=ffn2_w_out)
    m = dict(ffn1_norm=m_ffn1_norm, ffn1_w_in=m_ffn1_w_in, ffn1_w_out=m_ffn1_w_out, mix_norm=m_mix_norm, w_in=m_w_in,
             fox_q_norm=m_fox_q_norm, fox_k_norm=m_fox_k_norm, fox_f_bias=m_fox_f_bias, gdn_conv=m_gdn_conv,
             gdn_a_log=m_gdn_a_log, gdn_dt_bias=m_gdn_dt_bias, gdn_out_norm=m_gdn_out_norm, w_out=m_w_out,
             ffn2_norm=m_ffn2_norm, ffn2_w_in=m_ffn2_w_in, ffn2_w_out=m_ffn2_w_out)
    v = dict(ffn1_norm=v_ffn1_norm, ffn1_w_in=v_ffn1_w_in, ffn1_w_out=v_ffn1_w_out, mix_norm=v_mix_norm, w_in=v_w_in,
             fox_q_norm=v_fox_q_norm, fox_k_norm=v_fox_k_norm, fox_f_bias=v_fox_f_bias, gdn_conv=v_gdn_conv,
             gdn_a_log=v_gdn_a_log, gdn_dt_bias=v_gdn_dt_bias, gdn_out_norm=v_gdn_out_norm, w_out=v_w_out,
             ffn2_norm=v_ffn2_norm, ffn2_w_in=v_ffn2_w_in, ffn2_w_out=v_ffn2_w_out)
    return _step(x, loss_target, w, m, v)
```

```python
import jax
import jax.numpy as jnp
from jax import lax
from jax.experimental import pallas as pl
from jax.experimental.pallas import tpu as pltpu

F32 = jnp.float32
BF = jnp.bfloat16
HI = lax.Precision.HIGHEST
MESH = pl.DeviceIdType.MESH

DEPTH = 2
FOX_HEADS = 8
FOX_HEAD_DIM = 64
FOX_WIDTH = 512
GDN_HEADS = 4
GDN_HEAD_DIM = 128
GDN_WIDTH = 512
CONV_WIDTH = 4
CHUNK = 64
EPS = 1e-6
N_IN = 3600
N_PAD = 3712
GATE_COL = 3584
LANES = 128
NEG = -1e30

ADAM_LR = 0.001
ADAM_B1 = 0.9
ADAM_B2 = 0.999
ADAM_EPS = 1e-08
ADAM_WD = 0.01
ADAM_STEP = 10

VMEM_LIMIT = 56 * 1024 * 1024

TOKEN_TILE = 512
TOKEN_TILE_BWD = 256
WGRAD_TOKENS = 512
FOX_PREP_TILE = 512
GDN_PREP_TILE = 256
ATTN_FWD_TILE = 2048
ATTN_BWD_TILE = 1024
DIAGONAL_STRIPS = 2
SUM_ROWS = 512
ADAM_ROWS = 256


def _params(sem=None, **kw):
    return pltpu.CompilerParams(dimension_semantics=sem, vmem_limit_bytes=VMEM_LIMIT, **kw)


def _dot(a, b, precision=None):
    return jnp.dot(a, b, preferred_element_type=F32, precision=precision)


def _dot_nt(a, b, precision=None):
    return lax.dot_general(a, b, (((1,), (1,)), ((), ())), preferred_element_type=F32, precision=precision)


def _dot_tn(a, b, precision=None):
    return lax.dot_general(a, b, (((0,), (0,)), ((), ())), preferred_element_type=F32, precision=precision)


def _sigmoid(x):
    return 0.5 * jnp.tanh(0.5 * x) + 0.5


def _softplus(x):
    return jnp.maximum(x, 0.0) + jnp.log(1.0 + jnp.exp(-jnp.abs(x)))


def _log_sigmoid(x):
    return jnp.minimum(x, 0.0) - jnp.log(1.0 + jnp.exp(-jnp.abs(x)))


def _tile(n, t):
    t = min(n, t)
    assert n % t == 0, (n, t)
    return t


def _rms_fwd(x, gain):
    rstd = lax.rsqrt(jnp.mean(x * x, axis=-1, keepdims=True) + EPS)
    xhat = x * rstd
    return xhat * gain, xhat, rstd


def _rms_bwd(dy, xhat, rstd, gain):
    dxhat = dy * gain
    dx = rstd * (dxhat - xhat * jnp.mean(dxhat * xhat, axis=-1, keepdims=True))
    return dx, dy * xhat


def _full(shape):
    nd = len(shape)
    return pl.BlockSpec(shape, lambda *_: (0,) * nd)


HBM = pl.BlockSpec(memory_space=pltpu.HBM)


def _load_ffn_weights(win_hbm, wout_hbm, win_v, wout_v, sem):
    fr = wout_hbm.shape[1]
    copies = [pltpu.make_async_copy(win_hbm.at[s], win_v.at[s], sem.at[s]) for s in range(4)]
    copies += [pltpu.make_async_copy(wout_hbm.at[s], wout_v.at[pl.ds(s * fr, fr)], sem.at[4 + s])
               for s in range(4)]
    for c in copies:
        c.start()
    for c in copies:
        c.wait()


def _ffn_fwd(x, gain, win_g, wout_g, name, rider=None):
    t, d = x.shape
    _, _, fs = win_g.shape
    fr = wout_g.shape[1]
    tm = _tile(t, TOKEN_TILE)
    r_in, r_out, r_sem = _rider_parts(rider)
    steps = t // tm

    def body(x_ref, g_ref, win_hbm, wout_hbm, *rest):
        rin, (xo_ref, h_ref) = rest[:len(r_in)], rest[len(r_in):len(r_in) + 2]
        rout = rest[len(r_in) + 2:len(r_in) + 2 + len(r_out)]
        win_v, wout_v, sem = rest[len(r_in) + 2 + len(r_out):len(r_in) + 5 + len(r_out)]
        riding = (rin, rout, rest[len(r_in) + 5 + len(r_out):])
        step = pl.program_id(0)
        _ride(rider, 0, step == 0, riding)
        _ride(rider, 1, step == (3 * steps) // 4, riding)

        @pl.when(step == 0)
        def _():
            _load_ffn_weights(win_hbm, wout_hbm, win_v, wout_v, sem)

        xv = x_ref[...]
        hn, _, _ = _rms_fwd(xv, g_ref[...])
        hn = hn.astype(BF)
        acc = jnp.zeros((tm, d), F32)
        for s in range(2):
            g = _dot(hn, win_v[s])
            u = _dot(hn, win_v[s + 2])
            h_ref[:, s * fs:(s + 1) * fs] = g.astype(BF)
            h_ref[:, (s + 2) * fs:(s + 3) * fs] = u.astype(BF)
            a = (g * _sigmoid(g) * u).astype(BF)
            acc = acc + _dot(a, wout_v[s * fs:(s + 1) * fs, :])
        xo_ref[...] = xv + 0.5 * acc
        _ride(rider, 2, step == steps - 1, riding)

    return pl.pallas_call(
        body, name=name, grid=(steps,),
        in_specs=[pl.BlockSpec((tm, d), lambda i: (i, 0)), _full((1, d)), HBM, HBM] + [HBM] * len(r_in),
        out_specs=[pl.BlockSpec((tm, d), lambda i: (i, 0)), pl.BlockSpec((tm, 4 * fs), lambda i: (i, 0))]
        + [HBM] * len(r_out),
        out_shape=[jax.ShapeDtypeStruct((t, d), F32), jax.ShapeDtypeStruct((t, 4 * fs), BF)] + r_out,
        scratch_shapes=[pltpu.VMEM((4, d, fs), BF), pltpu.VMEM((4 * fr, d), BF), pltpu.SemaphoreType.DMA((8,))]
        + r_sem,
        compiler_params=_params(("arbitrary",), has_side_effects=rider is not None),
    )(x, gain, win_g, wout_g, *r_in)


def _ffn_bwd(dy, x, h, gain, win_g, wout_g, name, rider=None):
    t, d = x.shape
    _, _, fs = win_g.shape
    fr = wout_g.shape[1]
    tm = _tile(t, TOKEN_TILE_BWD)
    r_in, r_out, r_sem = _rider_parts(rider)
    steps = t // tm

    def body(dy_ref, x_ref, h_ref, g_ref, win_hbm, wout_hbm, *rest):
        rin, (dx_ref, dh_ref, a_ref, hn_ref, dyh_ref, dg_ref) = rest[:len(r_in)], rest[len(r_in):len(r_in) + 6]
        rout = rest[len(r_in) + 6:len(r_in) + 6 + len(r_out)]
        win_v, wout_v, sem = rest[len(r_in) + 6 + len(r_out):len(r_in) + 9 + len(r_out)]
        riding = (rin, rout, rest[len(r_in) + 9 + len(r_out):])
        step = pl.program_id(0)
        _ride(rider, 0, step == 0, riding)
        _ride(rider, 1, step == (3 * steps) // 4, riding)

        @pl.when(step == 0)
        def _():
            _load_ffn_weights(win_hbm, wout_hbm, win_v, wout_v, sem)
            dg_ref[...] = jnp.zeros_like(dg_ref)

        dyv = dy_ref[...]
        dyh = (0.5 * dyv).astype(BF)
        dyh_ref[...] = dyh
        dhn = jnp.zeros((tm, d), F32)
        for s in range(2):
            da = _dot_nt(dyh, wout_v[s * fs:(s + 1) * fs, :])
            g = h_ref[:, s * fs:(s + 1) * fs].astype(F32)
            u = h_ref[:, (s + 2) * fs:(s + 3) * fs].astype(F32)
            sg = _sigmoid(g)
            si = g * sg
            a_ref[:, s * fs:(s + 1) * fs] = (si * u).astype(BF)
            dgate = (da * u * (sg * (1.0 + g * (1.0 - sg)))).astype(BF)
            dup = (da * si).astype(BF)
            dh_ref[:, s * fs:(s + 1) * fs] = dgate
            dh_ref[:, (s + 2) * fs:(s + 3) * fs] = dup
            dhn = dhn + _dot_nt(dgate, win_v[s]) + _dot_nt(dup, win_v[s + 2])
        xv = x_ref[...]
        gain_v = g_ref[...]
        hn, xhat, rstd = _rms_fwd(xv, gain_v)
        hn_ref[...] = hn.astype(BF)
        dx, dgr = _rms_bwd(dhn, xhat, rstd, gain_v)
        dx_ref[...] = dyv + dx
        dg_ref[...] += jnp.sum(dgr, axis=0, keepdims=True)
        _ride(rider, 2, step == steps - 1, riding)

    row = lambda w: pl.BlockSpec((tm, w), lambda i: (i, 0))
    return pl.pallas_call(
        body, name=name, grid=(steps,),
        in_specs=[row(d), row(d), row(4 * fs), _full((1, d)), HBM, HBM] + [HBM] * len(r_in),
        out_specs=[row(d), row(4 * fs), row(2 * fs), row(d), row(d), _full((1, d))] + [HBM] * len(r_out),
        out_shape=[jax.ShapeDtypeStruct((t, d), F32), jax.ShapeDtypeStruct((t, 4 * fs), BF),
                   jax.ShapeDtypeStruct((t, 2 * fs), BF), jax.ShapeDtypeStruct((t, d), BF),
                   jax.ShapeDtypeStruct((t, d), BF), jax.ShapeDtypeStruct((1, d), F32)] + r_out,
        scratch_shapes=[pltpu.VMEM((4, d, fs), BF), pltpu.VMEM((4 * fr, d), BF), pltpu.SemaphoreType.DMA((8,))]
        + r_sem,
        compiler_params=_params(("arbitrary",), has_side_effects=rider is not None),
    )(dy, x, h, gain, win_g, wout_g, *r_in)


def _wgrad(a, b, out_shape, out_spec, tm, tn, name, tk=WGRAD_TOKENS):
    t, m = a.shape
    _, n = b.shape
    tk = _tile(t, tk)
    nk = t // tk

    def body(a_ref, b_ref, o_ref, acc):
        k = pl.program_id(2)

        @pl.when(k == 0)
        def _():
            acc[...] = jnp.zeros_like(acc)

        acc[...] += _dot_tn(a_ref[...], b_ref[...])

        @pl.when(k == nk - 1)
        def _():
            o_ref[...] = acc[...].astype(o_ref.dtype)

    return pl.pallas_call(
        body, name=name, grid=(m // tm, n // tn, nk),
        in_specs=[pl.BlockSpec((tk, tm), lambda i, j, k: (k, i)), pl.BlockSpec((tk, tn), lambda i, j, k: (k, j))],
        out_specs=out_spec, out_shape=out_shape,
        scratch_shapes=[pltpu.VMEM((tm, tn), F32)],
        compiler_params=_params(("parallel", "parallel", "arbitrary")),
    )(a, b)


def _wgrad_parts(a, parts, tm, name, tk=WGRAD_TOKENS):
    t, m = a.shape
    widths = [p.shape[1] for p in parts]
    n = sum(widths)
    tk = _tile(t, tk)
    nk = t // tk
    np_ = len(parts)

    def body(a_ref, *rest):
        b_refs, o_ref, acc = rest[:np_], rest[np_], rest[np_ + 1]
        k = pl.program_id(1)

        @pl.when(k == 0)
        def _():
            acc[...] = jnp.zeros_like(acc)

        av, off = a_ref[...], 0
        for b_ref, wd in zip(b_refs, widths):
            acc[:, off:off + wd] += _dot_tn(av, b_ref[...])
            off += wd

        @pl.when(k == nk - 1)
        def _():
            o_ref[...] = acc[...]

    return pl.pallas_call(
        body, name=name, grid=(m // tm, nk),
        in_specs=[pl.BlockSpec((tk, tm), lambda i, k: (k, i))]
        + [pl.BlockSpec((tk, wd), lambda i, k: (k, 0)) for wd in widths],
        out_specs=pl.BlockSpec((tm, n), lambda i, k: (i, 0)), out_shape=jax.ShapeDtypeStruct((m, n), F32),
        scratch_shapes=[pltpu.VMEM((tm, n), F32)],
        compiler_params=_params(("parallel", "arbitrary")),
    )(a, *parts)


def _norm_matmul(x, gain, w, name):
    t, d = x.shape
    n = w.shape[1]
    tm = _tile(t, TOKEN_TILE)

    def body(x_ref, g_ref, w_ref, o_ref):
        hn, _, _ = _rms_fwd(x_ref[...], g_ref[...])
        o_ref[...] = _dot(hn.astype(BF), w_ref[...])

    return pl.pallas_call(
        body, name=name, grid=(t // tm,),
        in_specs=[pl.BlockSpec((tm, d), lambda i: (i, 0)), _full((1, d)), _full((d, n))],
        out_specs=pl.BlockSpec((tm, n), lambda i: (i, 0)),
        out_shape=jax.ShapeDtypeStruct((t, n), F32),
        compiler_params=_params(("parallel",)),
    )(x, gain, w)


def _norm_matmul_bwd(dres, dparts, x, gain, w, name):
    t, d = x.shape
    n = w.shape[1]
    tm = _tile(t, TOKEN_TILE)
    widths = [a.shape[1] for a in dparts]
    assert sum(widths) == n
    k = len(dparts)

    def body(dr_ref, *rest):
        dp_refs, (x_ref, g_ref, w_ref, dx_ref, hn_ref, dg_ref) = rest[:k], rest[k:]

        @pl.when(pl.program_id(0) == 0)
        def _():
            dg_ref[...] = jnp.zeros_like(dg_ref)

        dhn, off = jnp.zeros((tm, d), F32), 0
        for dp_ref, wd in zip(dp_refs, widths):
            dhn = dhn + _dot_nt(dp_ref[...], w_ref[:, off:off + wd])
            off += wd
        gain_v = g_ref[...]
        hn, xhat, rstd = _rms_fwd(x_ref[...], gain_v)
        hn_ref[...] = hn.astype(BF)
        dx, dgr = _rms_bwd(dhn, xhat, rstd, gain_v)
        dx_ref[...] = dr_ref[...] + dx
        dg_ref[...] += jnp.sum(dgr, axis=0, keepdims=True)

    row = lambda wd: pl.BlockSpec((tm, wd), lambda i: (i, 0))
    return pl.pallas_call(
        body, name=name, grid=(t // tm,),
        in_specs=[row(d)] + [row(wd) for wd in widths] + [row(d), _full((1, d)), _full((d, n))],
        out_specs=[row(d), row(d), _full((1, d))],
        out_shape=[jax.ShapeDtypeStruct((t, d), F32), jax.ShapeDtypeStruct((t, d), BF),
                   jax.ShapeDtypeStruct((1, d), F32)],
        compiler_params=_params(("arbitrary",)),
    )(dres, *dparts, x, gain, w)


def _mix_out(x, yf, yg, w, name):
    t, d = x.shape
    kf = yf.shape[1]
    tm = _tile(t, TOKEN_TILE)

    def body(x_ref, yf_ref, yg_ref, w_ref, o_ref):
        o_ref[...] = x_ref[...] + _dot(yf_ref[...], w_ref[0:kf, :]) + _dot(yg_ref[...], w_ref[kf:2 * kf, :])

    row = lambda wd: pl.BlockSpec((tm, wd), lambda i: (i, 0))
    return pl.pallas_call(
        body, name=name, grid=(t // tm,),
        in_specs=[row(d), row(kf), row(kf), _full((2 * kf, d))],
        out_specs=row(d), out_shape=jax.ShapeDtypeStruct((t, d), F32),
        compiler_params=_params(("parallel",)),
    )(x, yf, yg, w)


def _mix_out_bwd(dx, w, name):
    t, d = dx.shape
    kf = w.shape[0] // 2
    tm = _tile(t, TOKEN_TILE)

    def body(dx_ref, w_ref, df_ref, dg_ref, dxb_ref):
        dxb = dx_ref[...].astype(BF)
        dxb_ref[...] = dxb
        df_ref[...] = _dot_nt(dxb, w_ref[0:kf, :]).astype(BF)
        dg_ref[...] = _dot_nt(dxb, w_ref[kf:2 * kf, :]).astype(BF)

    row = lambda wd: pl.BlockSpec((tm, wd), lambda i: (i, 0))
    return pl.pallas_call(
        body, name=name, grid=(t // tm,),
        in_specs=[row(d), _full((2 * kf, d))],
        out_specs=[row(kf), row(kf), row(d)],
        out_shape=[jax.ShapeDtypeStruct((t, kf), BF), jax.ShapeDtypeStruct((t, kf), BF),
                   jax.ShapeDtypeStruct((t, d), BF)],
        compiler_params=_params(("parallel",)),
    )(dx, w)


def _loss_grad(y, target, name):
    t, d = y.shape
    tm = _tile(t, TOKEN_TILE)

    def body(y_ref, t_ref, l_ref, dy_ref):
        @pl.when(pl.program_id(0) == 0)
        def _():
            l_ref[...] = jnp.zeros_like(l_ref)

        diff = y_ref[...] - t_ref[...]
        dy_ref[...] = diff * (1.0 / d)
        part = jnp.sum(jnp.sum(diff * diff, axis=1, keepdims=True), axis=0, keepdims=True)
        l_ref[...] += part * (0.5 / d)

    row = pl.BlockSpec((tm, d), lambda i: (i, 0))
    return pl.pallas_call(
        body, name=name, grid=(t // tm,),
        in_specs=[row, row], out_specs=[_full((1, 1)), row],
        out_shape=[jax.ShapeDtypeStruct((1, 1), F32), jax.ShapeDtypeStruct((t, d), F32)],
        compiler_params=_params(("arbitrary",)),
    )(y, target)


def _head_sum_matrix(width, head):
    r = lax.broadcasted_iota(jnp.int32, (width, width), 0) // head
    c = lax.broadcasted_iota(jnp.int32, (width, width), 1) // head
    return (r == c).astype(BF)


def _head_mean(x, bd):
    return _dot(x.astype(BF), bd) * (1.0 / FOX_HEAD_DIM)


def _mask_dot(mask01, x):
    mb = mask01.astype(BF)
    hi = x.astype(BF)
    r1 = x - hi.astype(F32)
    mid = r1.astype(BF)
    lo = (r1 - mid.astype(F32)).astype(BF)
    return _dot(mb, hi) + _dot(mb, mid) + _dot(mb, lo)


def _fox_prep(proj, wq_t, wk_t, bias_pad, seq, name):
    t = proj.shape[0]
    ts = _tile(seq, FOX_PREP_TILE)
    tpe = seq // ts
    scale = FOX_HEAD_DIM ** -0.5

    def body(q_ref, k_ref, v_ref, gt_ref, wq_ref, wk_ref, b_ref, qo_ref, ko_ref, vo_ref, cum_ref, carry):
        i = pl.program_id(0)
        bd = _head_sum_matrix(FOX_WIDTH, FOX_HEAD_DIM)

        def norm(xv, wv):
            ms = _head_mean(xv * xv, bd)
            return xv * lax.rsqrt(ms + EPS) * wv

        qo_ref[...] = (norm(q_ref[...], wq_ref[...]) * scale).astype(BF)
        ko_ref[...] = norm(k_ref[...], wk_ref[...]).astype(BF)
        vo_ref[...] = v_ref[...].astype(BF)

        @pl.when(i % tpe == 0)
        def _():
            carry[...] = jnp.zeros_like(carry)

        ls = _log_sigmoid(gt_ref[...] + b_ref[...])
        r = lax.broadcasted_iota(jnp.int32, (ts, ts), 0)
        c = lax.broadcasted_iota(jnp.int32, (ts, ts), 1)
        cum = _mask_dot(r >= c, ls) + carry[...]
        cum_ref[...] = cum
        carry[...] = cum[ts - 1:ts, :]

    blk = lambda j: pl.BlockSpec((ts, FOX_WIDTH), lambda i: (i, j))
    gate = pl.BlockSpec((ts, LANES), lambda i: (i, GATE_COL // LANES))
    out = pl.BlockSpec((ts, FOX_WIDTH), lambda i: (i, 0))
    return pl.pallas_call(
        body, name=name, grid=(t // ts,),
        in_specs=[blk(0), blk(1), blk(2), gate, _full((1, FOX_WIDTH)), _full((1, FOX_WIDTH)), _full((1, LANES))],
        out_specs=[out, out, out, pl.BlockSpec((ts, LANES), lambda i: (i, 0))],
        out_shape=[jax.ShapeDtypeStruct((t, FOX_WIDTH), BF)] * 3 + [jax.ShapeDtypeStruct((t, LANES), F32)],
        scratch_shapes=[pltpu.VMEM((1, LANES), F32)],
        compiler_params=_params(("arbitrary",)),
    )(proj, proj, proj, proj, wq_t, wk_t, bias_pad)


def _pick_head_sums(x):
    r = lax.broadcasted_iota(jnp.int32, (FOX_WIDTH, LANES), 0)
    c = lax.broadcasted_iota(jnp.int32, (FOX_WIDTH, LANES), 1)
    sel = (((r % LANES == FOX_HEAD_DIM) & (c == 2 * (r // LANES)))
           | ((r % LANES == 0) & (c == 2 * (r // LANES) + 1))).astype(BF)
    hi = x.astype(BF)
    r1 = x - hi.astype(F32)
    mid = r1.astype(BF)
    lo = (r1 - mid.astype(F32)).astype(BF)
    return _dot(hi, sel) + _dot(mid, sel) + _dot(lo, sel)


def _fox_prep_bwd(proj, dqa, dqb, dk, dv, dkx, wq_t, wk_t, bias_pad, seq, name):
    t = proj.shape[0]
    ts = _tile(seq, FOX_PREP_TILE)
    tpe = seq // ts
    nt = t // ts
    scale = FOX_HEAD_DIM ** -0.5

    def body(q_ref, k_ref, gt_ref, dqa_ref, dqb_ref, dk_ref, dv_ref, dc_ref, wq_ref, wk_ref, b_ref,
             dp_ref, dff_ref, dwq_ref, dwk_ref, db_ref, carry):
        i = pl.program_id(0)
        first = (lax.broadcasted_iota(jnp.int32, (ts, FOX_WIDTH), 1) % LANES) < FOX_HEAD_DIM
        dq_all = jnp.where(first, dqa_ref[...], dqb_ref[...])
        ti = nt - 1 - i
        bd = _head_sum_matrix(FOX_WIDTH, FOX_HEAD_DIM)

        @pl.when(i == 0)
        def _():
            dwq_ref[...] = jnp.zeros_like(dwq_ref)
            dwk_ref[...] = jnp.zeros_like(dwk_ref)
            db_ref[...] = jnp.zeros_like(db_ref)

        def norm_bwd(xv, wv, dyv):
            ms = _head_mean(xv * xv, bd)
            rstd = lax.rsqrt(ms + EPS)
            xhat = xv * rstd
            dxhat = dyv * wv
            mean = _head_mean(dxhat * xhat, bd)
            return rstd * (dxhat - xhat * mean), jnp.sum(dyv * xhat, axis=0, keepdims=True)

        dxq, dwq = norm_bwd(q_ref[...], wq_ref[...], dq_all * scale)
        dxk, dwk = norm_bwd(k_ref[...], wk_ref[...], dk_ref[...])
        dp_ref[:, 0:FOX_WIDTH] = dxq.astype(BF)
        dp_ref[:, FOX_WIDTH:2 * FOX_WIDTH] = dxk.astype(BF)
        dp_ref[:, 2 * FOX_WIDTH:3 * FOX_WIDTH] = dv_ref[...].astype(BF)
        dwq_ref[...] += dwq
        dwk_ref[...] += dwk

        @pl.when(ti % tpe == tpe - 1)
        def _():
            carry[...] = jnp.zeros_like(carry)

        r = lax.broadcasted_iota(jnp.int32, (ts, ts), 0)
        c = lax.broadcasted_iota(jnp.int32, (ts, ts), 1)
        dcum = _pick_head_sums(jnp.where(first, dqb_ref[...], dqa_ref[...]) - dc_ref[...])
        dls = _mask_dot(c >= r, dcum) + carry[...]
        carry[...] = dls[0:1, :]
        z = gt_ref[...] + b_ref[...]
        lane = lax.broadcasted_iota(jnp.int32, (ts, LANES), 1)
        dff = jnp.where(lane < FOX_HEADS, dls * _sigmoid(-z), 0.0)
        dff_ref[...] = dff
        db_ref[...] += jnp.sum(dff, axis=0, keepdims=True)

        @pl.when(i == nt - 1)
        def _():
            fr = lax.broadcasted_iota(jnp.int32, (FOX_WIDTH, FOX_WIDTH), 0) % FOX_HEAD_DIM
            fc = lax.broadcasted_iota(jnp.int32, (FOX_WIDTH, FOX_WIDTH), 1) % FOX_HEAD_DIM
            fold = (fr == fc).astype(F32)
            dwq_ref[...] = _dot(dwq_ref[...], fold, HI)
            dwk_ref[...] = _dot(dwk_ref[...], fold, HI)

    rev = lambda w, j: pl.BlockSpec((ts, w), lambda i: (nt - 1 - i, j))
    return pl.pallas_call(
        body, name=name, grid=(nt,),
        in_specs=[rev(FOX_WIDTH, 0), rev(FOX_WIDTH, 1), rev(LANES, GATE_COL // LANES),
                  rev(FOX_WIDTH, 0), rev(FOX_WIDTH, 0), rev(FOX_WIDTH, 0), rev(FOX_WIDTH, 0), rev(FOX_WIDTH, 0),
                  _full((1, FOX_WIDTH)), _full((1, FOX_WIDTH)), _full((1, LANES))],
        out_specs=[rev(3 * FOX_WIDTH, 0), rev(LANES, 0), _full((1, FOX_WIDTH)), _full((1, FOX_WIDTH)),
                   _full((1, LANES))],
        out_shape=[jax.ShapeDtypeStruct((t, 3 * FOX_WIDTH), BF), jax.ShapeDtypeStruct((t, LANES), F32),
                   jax.ShapeDtypeStruct((1, FOX_WIDTH), F32), jax.ShapeDtypeStruct((1, FOX_WIDTH), F32),
                   jax.ShapeDtypeStruct((1, LANES), F32)],
        scratch_shapes=[pltpu.VMEM((1, LANES), F32)],
        compiler_params=_params(("arbitrary",)),
    )(proj, proj, proj, dqa, dqb, dk, dv, dkx, wq_t, wk_t, bias_pad)


class _Rider:
    def __init__(self, inputs, out_shapes, sems, phases):
        self.inputs, self.out_shapes, self.sems, self.phases = list(inputs), list(out_shapes), list(sems), phases


def _rider_parts(rider):
    if rider is None:
        return [], [], []
    return rider.inputs, rider.out_shapes, rider.sems


def _ride(rider, which, when, refs):
    if rider is not None:
        @pl.when(when)
        def _():
            rider.phases[which](*refs)


def _fox_attn(q, k, v, ck, nb, seq, name, rider=None):
    t = q.shape[0]
    tq = _tile(seq, ATTN_FWD_TILE)
    nq = seq // tq
    npair = FOX_HEADS // 2
    hd = FOX_HEAD_DIM
    r_in, r_out, r_sem = _rider_parts(rider)
    steps = nb * npair * nq

    def body(q_ref, k_ref, v_ref, ck_ref, *rest):
        rin, (o_ref, lse_ref) = rest[:len(r_in)], rest[len(r_in):len(r_in) + 2]
        rout = rest[len(r_in) + 2:len(r_in) + 2 + len(r_out)]
        m_s, acc_s = rest[len(r_in) + 2 + len(r_out):len(r_in) + 4 + len(r_out)]
        riding = (rin, rout, rest[len(r_in) + 4 + len(r_out):])
        step = (pl.program_id(0) * npair + pl.program_id(1)) * nq + pl.program_id(2)
        _ride(rider, 0, step == 0, riding)
        _ride(rider, 1, step == (3 * steps) // 4, riding)
        qi = pl.program_id(2)
        lane = lax.broadcasted_iota(jnp.int32, (tq, LANES), 1)
        m_s[...] = jnp.full(m_s.shape, NEG, F32)
        acc_s[...] = jnp.zeros_like(acc_s)
        qv = q_ref[...]

        def block(kj, r0, nr, nc, on_diagonal):
            cols = pl.ds(pl.multiple_of(kj * tq, tq), nc)
            rows = slice(r0, r0 + nr)
            kv = k_ref[cols, :]
            vv = v_ref[cols, :]
            qr = qv[rows]
            lanes = lane[rows]
            if on_diagonal:
                causal = (r0 + lax.broadcasted_iota(jnp.int32, (nr, nc), 0)
                          >= lax.broadcasted_iota(jnp.int32, (nr, nc), 1))
            for hh in range(2):
                hm = (lanes >= hd) if hh else (lanes < hd)
                qh = jnp.where(hm, qr, jnp.zeros_like(qr))
                s = _dot_nt(qh, kv) - ck_ref[hh, :, cols]
                if on_diagonal:
                    s = jnp.where(causal, s, NEG)
                m_old = m_s[hh, rows]
                m_new = jnp.maximum(m_old, jnp.max(s, axis=-1, keepdims=True))
                p = jnp.exp(s - m_new)
                alpha = jnp.exp(m_old - m_new)
                m_s[hh, rows] = m_new
                vh = jnp.where(lane[:nc] >= hd if hh else lane[:nc] < hd, vv, jnp.ones_like(vv))
                acc_s[hh, rows] = alpha * acc_s[hh, rows] + _dot(p.astype(BF), vh)

        def off_diagonal(kj, carry):
            block(kj, 0, tq, tq, False)
            return carry

        lax.fori_loop(0, qi, off_diagonal, 0)
        strip = tq // DIAGONAL_STRIPS
        for i in range(DIAGONAL_STRIPS):
            block(qi, i * strip, strip, (i + 1) * strip, True)
        a0 = acc_s[0]
        a1 = acc_s[1]
        den = jnp.where(lane < hd, pltpu.roll(a0, hd, axis=1), pltpu.roll(a1, hd, axis=1))
        o_ref[...] = (jnp.where(lane < hd, a0, a1) / den).astype(o_ref.dtype)
        l0 = jnp.sum(jnp.where(lane == hd, a0, 0.0), axis=1, keepdims=True)
        l1 = jnp.sum(jnp.where(lane == 0, a1, 0.0), axis=1, keepdims=True)
        lse_ref[0] = m_s[0] + jnp.log(l0)
        lse_ref[1] = m_s[1] + jnp.log(l1)
        _ride(rider, 2, step == steps - 1, riding)

    qspec = pl.BlockSpec((tq, LANES), lambda b, p, i: (b * nq + i, p))
    kspec = pl.BlockSpec((seq, LANES), lambda b, p, i: (b, p))
    colspec = pl.BlockSpec((None, 2, tq, 1), lambda b, p, i: (b * npair + p, 0, i, 0))
    rowspec = pl.BlockSpec((None, 2, 1, seq), lambda b, p, i: (b * npair + p, 0, 0, 0))
    sem = ("arbitrary",) * 3 if rider else ("parallel",) * 3
    return pl.pallas_call(
        body, name=name, grid=(nb, npair, nq),
        in_specs=[qspec, kspec, kspec, rowspec] + [HBM] * len(r_in),
        out_specs=[qspec, colspec] + [HBM] * len(r_out),
        out_shape=[jax.ShapeDtypeStruct((t, FOX_WIDTH), BF), jax.ShapeDtypeStruct((nb * npair, 2, seq, 1), F32)]
        + r_out,
        scratch_shapes=[pltpu.VMEM((2, tq, 1), F32), pltpu.VMEM((2, tq, LANES), F32)] + r_sem,
        compiler_params=_params(sem, has_side_effects=rider is not None),
    )(q, k, v, ck, *r_in)


def _fox_attn_bwd(q, k, v, o, do, lse, ck, nb, seq, name, rider=None):
    t = q.shape[0]
    tq = _tile(seq, ATTN_BWD_TILE)
    nq = seq // tq
    npair = FOX_HEADS // 2
    hd = FOX_HEAD_DIM
    r_in, r_out, r_sem = _rider_parts(rider)
    steps = nb * npair * nq

    def body(q_ref, k_ref, v_ref, o_ref, do_ref, lse_ref, ck_ref, *rest):
        rin, (dqa_ref, dqb_ref, dk_ref, dv_ref, dkx_ref) = rest[:len(r_in)], rest[len(r_in):len(r_in) + 5]
        rout = rest[len(r_in) + 5:len(r_in) + 5 + len(r_out)]
        dk_s, dv_s = rest[len(r_in) + 5 + len(r_out):len(r_in) + 7 + len(r_out)]
        riding = (rin, rout, rest[len(r_in) + 7 + len(r_out):])
        step = (pl.program_id(0) * npair + pl.program_id(1)) * nq + pl.program_id(2)
        _ride(rider, 0, step == 0, riding)
        _ride(rider, 1, step == (3 * steps) // 4, riding)
        kj = pl.program_id(2)
        lane = lax.broadcasted_iota(jnp.int32, (tq, LANES), 1)

        @pl.when(kj == 0)
        def _():
            dqa_ref[...] = jnp.zeros_like(dqa_ref)
            dqb_ref[...] = jnp.zeros_like(dqb_ref)

        dk_s[...] = jnp.zeros_like(dk_s)
        dv_s[...] = jnp.zeros_like(dv_s)
        kv = k_ref[...]
        vv = v_ref[...]

        def block(qi, r0, nr, nc, on_diagonal):
            rows = pl.ds(pl.multiple_of(qi * tq, tq) + r0, nr)
            keys = slice(0, nc)
            qv = q_ref[rows, :]
            dov = do_ref[rows, :]
            kc, vc = kv[keys], vv[keys]
            prod = dov.astype(F32) * o_ref[rows, :].astype(F32)
            lq, lk = lane[:nr], lane[:nc]
            if on_diagonal:
                causal = (r0 + lax.broadcasted_iota(jnp.int32, (nr, nc), 0)
                          >= lax.broadcasted_iota(jnp.int32, (nr, nc), 1))
            for hh, dq_ref in ((0, dqa_ref), (1, dqb_ref)):
                hm = (lq >= hd) if hh else (lq < hd)
                hk = (lk >= hd) if hh else (lk < hd)
                zero = jnp.zeros_like(qv)
                doh = jnp.where(hm, dov, zero)
                delta = jnp.sum(jnp.where(hm, prod, 0.0), axis=-1, keepdims=True)
                s = _dot_nt(jnp.where(hm, qv, zero), kc) - ck_ref[hh, :, keys]
                if on_diagonal:
                    s = jnp.where(causal, s, NEG)
                p = jnp.exp(s - lse_ref[hh, rows, :])
                dp = _dot_nt(doh, vc)
                dsb = (p * (dp - delta)).astype(BF)
                dv_s[keys] += _dot_tn(p.astype(BF), doh)
                dk_s[hh, keys] += _dot_tn(dsb, jnp.where(hm, qv, jnp.ones_like(qv)))
                dq_ref[rows, :] += _dot(dsb, jnp.where(hk, kc, jnp.ones_like(kc)))

        def off_diagonal(qi, carry):
            block(qi, 0, tq, tq, False)
            return carry

        strip = tq // DIAGONAL_STRIPS
        for i in range(DIAGONAL_STRIPS):
            block(kj, i * strip, strip, (i + 1) * strip, True)
        lax.fori_loop(kj + 1, nq, off_diagonal, 0)
        dk_ref[...] = jnp.where(lane < hd, dk_s[0], dk_s[1])
        dkx_ref[...] = jnp.where(lane < hd, dk_s[1], dk_s[0])
        dv_ref[...] = dv_s[...]
        _ride(rider, 2, step == steps - 1, riding)

    kspec = pl.BlockSpec((tq, LANES), lambda b, p, j: (b * nq + j, p))
    full_q = pl.BlockSpec((seq, LANES), lambda b, p, j: (b, p))
    colspec = pl.BlockSpec((None, 2, seq, 1), lambda b, p, j: (b * npair + p, 0, 0, 0))
    rowspec = pl.BlockSpec((None, 2, 1, tq), lambda b, p, j: (b * npair + p, 0, 0, j))
    sem = ("arbitrary",) * 3 if rider else ("parallel", "parallel", "arbitrary")
    return pl.pallas_call(
        body, name=name, grid=(nb, npair, nq),
        in_specs=[full_q, kspec, kspec, full_q, full_q, colspec, rowspec] + [HBM] * len(r_in),
        out_specs=[full_q, full_q, kspec, kspec, kspec] + [HBM] * len(r_out),
        out_shape=[jax.ShapeDtypeStruct((t, FOX_WIDTH), F32)] * 5 + r_out,
        scratch_shapes=[pltpu.VMEM((2, tq, LANES), F32), pltpu.VMEM((tq, LANES), F32)] + r_sem,
        compiler_params=_params(sem, has_side_effects=rider is not None),
    )(q, k, v, o, do, lse, ck, *r_in)


GDN_QKV = 3 * GDN_WIDTH
GDN_COL = 3 * FOX_WIDTH
GG_COL = GDN_COL + GDN_QKV
A_LANE = FOX_HEADS
B_LANE = FOX_HEADS + GDN_HEADS
HALO = 8


def _gate_lanes(ts):
    lane = lax.broadcasted_iota(jnp.int32, (ts, LANES), 1)
    return (lane >= A_LANE) & (lane < B_LANE), (lane >= B_LANE) & (lane < B_LANE + GDN_HEADS)


def _chunk_tri(ts, upper):
    r = lax.broadcasted_iota(jnp.int32, (ts, ts), 0)
    c = lax.broadcasted_iota(jnp.int32, (ts, ts), 1)
    same = (r // CHUNK) == (c // CHUNK)
    return (same & ((c >= r) if upper else (r >= c))).astype(F32)


def _shift_rows(x, edge, k, down):
    ts = x.shape[0]
    row = lax.broadcasted_iota(jnp.int32, (HALO, x.shape[1]), 0)
    if down:
        rolled = pltpu.roll(x, k, axis=0)
        patch = jnp.where(row < k, pltpu.roll(edge, k, axis=0), rolled[:HALO])
        return jnp.concatenate([patch, rolled[HALO:]], axis=0)
    rolled = pltpu.roll(x, ts - k, axis=0)
    patch = jnp.where(row >= HALO - k, pltpu.roll(edge, HALO - k, axis=0), rolled[ts - HALO:])
    return jnp.concatenate([rolled[:ts - HALO], patch], axis=0)


def _conv_silu(x, before, w):
    taps = [_shift_rows(x, before, CONV_WIDTH - 1 - kk, True) for kk in range(CONV_WIDTH - 1)] + [x]
    c = w[0:1, :] * taps[0]
    for kk in range(1, CONV_WIDTH):
        c = c + w[kk:kk + 1, :] * taps[kk]
    return taps, c, c * _sigmoid(c)


def _gdn_prep(proj, conv_w, a_pad, dt_pad, seq, name):
    t = proj.shape[0]
    ts = _tile(seq, GDN_PREP_TILE)
    tpe = seq // ts
    qscale = GDN_HEAD_DIM ** -0.5

    def body(x_ref, gt_ref, w_ref, a_ref, dt_ref, qo_ref, ko_ref, vo_ref, go_ref, tail):
        i = pl.program_id(0)
        xv = x_ref[...]
        before = jnp.where(i % tpe == 0, jnp.zeros((HALO, GDN_QKV), F32), tail[...])
        tail[...] = xv[ts - HALO:]
        _, _, s = _conv_silu(xv, before, w_ref[...])
        for h in range(GDN_HEADS):
            for base, ref, sc in ((0, qo_ref, qscale), (GDN_WIDTH, ko_ref, 1.0)):
                xh = s[:, base + h * LANES: base + (h + 1) * LANES]
                r = lax.rsqrt(jnp.sum(xh * xh, axis=-1, keepdims=True) + EPS)
                ref[:, h * LANES:(h + 1) * LANES] = (xh * (r * sc)).astype(BF)
        vo_ref[...] = s[:, 2 * GDN_WIDTH:].astype(BF)
        gate = gt_ref[...]
        g_raw = -jnp.exp(a_ref[...]) * _softplus(gate + dt_ref[...])
        gc = _mask_dot(_chunk_tri(ts, False), g_raw)
        is_a, is_b = _gate_lanes(ts)
        go_ref[...] = jnp.where(is_a, gc, jnp.where(is_b, _sigmoid(gate), 0.0))

    out = pl.BlockSpec((ts, GDN_WIDTH), lambda i: (i, 0))
    lanes = pl.BlockSpec((ts, LANES), lambda i: (i, 0))
    return pl.pallas_call(
        body, name=name, grid=(t // ts,),
        in_specs=[pl.BlockSpec((ts, GDN_QKV), lambda i: (i, GDN_COL // GDN_QKV)),
                  pl.BlockSpec((ts, LANES), lambda i: (i, GATE_COL // LANES)),
                  _full((CONV_WIDTH, GDN_QKV)), _full((1, LANES)), _full((1, LANES))],
        out_specs=[out, out, out, lanes],
        out_shape=[jax.ShapeDtypeStruct((t, GDN_WIDTH), BF)] * 3 + [jax.ShapeDtypeStruct((t, LANES), F32)],
        scratch_shapes=[pltpu.VMEM((HALO, GDN_QKV), F32)],
        compiler_params=_params(("arbitrary",)),
    )(proj, proj, conv_w, a_pad, dt_pad)


def _gdn_prep_bwd(proj, dq, dk, dv, dgates, dff, conv_w, a_pad, dt_pad, seq, name):
    t = proj.shape[0]
    ts = _tile(seq, GDN_PREP_TILE)
    tpe = seq // ts
    nt = t // ts
    qscale = GDN_HEAD_DIM ** -0.5
    hb = ts // HALO

    def body(x_ref, halo_ref, gt_ref, dq_ref, dk_ref, dv_ref, dgt_ref, dff_ref, w_ref, a_ref, dt_ref,
             dx_ref, dgo_ref, dw_ref, da_ref, ddt_ref, dsl, carry):
        i = pl.program_id(0)
        ti = nt - 1 - i

        @pl.when(i == 0)
        def _():
            dw_ref[...] = jnp.zeros_like(dw_ref)
            da_ref[...] = jnp.zeros_like(da_ref)
            ddt_ref[...] = jnp.zeros_like(ddt_ref)

        halo = halo_ref[...]
        before = jnp.where(ti % tpe == 0, jnp.zeros_like(halo), halo)
        w = w_ref[...]
        taps, c, s = _conv_silu(x_ref[...], before, w)
        for h in range(GDN_HEADS):
            for base, ref, sc in ((0, dq_ref, qscale), (GDN_WIDTH, dk_ref, 1.0)):
                lo = base + h * LANES
                xh = s[:, lo:lo + LANES]
                r = lax.rsqrt(jnp.sum(xh * xh, axis=-1, keepdims=True) + EPS)
                y = xh * r
                dy = ref[:, h * LANES:(h + 1) * LANES] * sc
                dsl[:, lo:lo + LANES] = r * (dy - y * jnp.sum(dy * y, axis=-1, keepdims=True))
        dsl[:, 2 * GDN_WIDTH:] = dv_ref[...]
        sg = _sigmoid(c)
        dc = dsl[...] * (sg * (1.0 + c * (1.0 - sg)))
        nxt = carry[...]
        after = jnp.where(ti % tpe == tpe - 1, jnp.zeros_like(nxt), nxt)
        carry[...] = dc[0:HALO, :]
        dx = w[CONV_WIDTH - 1:CONV_WIDTH, :] * dc
        for kk in range(CONV_WIDTH - 1):
            dx = dx + w[kk:kk + 1, :] * _shift_rows(dc, after, CONV_WIDTH - 1 - kk, False)
        dx_ref[...] = dx.astype(BF)
        for kk in range(CONV_WIDTH):
            dw_ref[kk:kk + 1, :] += jnp.sum(dc * taps[kk], axis=0, keepdims=True)
        gate = gt_ref[...]
        dgt = dgt_ref[...]
        is_a, is_b = _gate_lanes(ts)
        dg_raw = _mask_dot(_chunk_tri(ts, True), jnp.where(is_a, dgt, 0.0))
        z = gate + dt_ref[...]
        na = -jnp.exp(a_ref[...])
        dga = dg_raw * na * _sigmoid(z)
        beta = _sigmoid(gate)
        dgb = jnp.where(is_b, dgt * beta * (1.0 - beta), 0.0)
        dgo_ref[...] = (dff_ref[...] + dga + dgb).astype(BF)
        ddt_ref[...] += jnp.sum(dga, axis=0, keepdims=True)
        da_ref[...] += jnp.sum(dg_raw * na * _softplus(z), axis=0, keepdims=True)

    rev = lambda wd, j: pl.BlockSpec((ts, wd), lambda i: (nt - 1 - i, j))
    halo_spec = pl.BlockSpec((HALO, GDN_QKV), lambda i: (jnp.maximum((nt - 1 - i) * hb - 1, 0), GDN_COL // GDN_QKV))
    return pl.pallas_call(
        body, name=name, grid=(nt,),
        in_specs=[rev(GDN_QKV, GDN_COL // GDN_QKV), halo_spec, rev(LANES, GATE_COL // LANES),
                  rev(GDN_WIDTH, 0), rev(GDN_WIDTH, 0), rev(GDN_WIDTH, 0), rev(LANES, 0), rev(LANES, 0),
                  _full((CONV_WIDTH, GDN_QKV)), _full((1, LANES)), _full((1, LANES))],
        out_specs=[rev(GDN_QKV, 0), rev(LANES, 0), _full((CONV_WIDTH, GDN_QKV)), _full((1, LANES)),
                   _full((1, LANES))],
        out_shape=[jax.ShapeDtypeStruct((t, GDN_QKV), BF), jax.ShapeDtypeStruct((t, LANES), BF),
                   jax.ShapeDtypeStruct((CONV_WIDTH, GDN_QKV), F32), jax.ShapeDtypeStruct((1, LANES), F32),
                   jax.ShapeDtypeStruct((1, LANES), F32)],
        scratch_shapes=[pltpu.VMEM((ts, GDN_QKV), F32), pltpu.VMEM((HALO, GDN_QKV), F32)],
        compiler_params=_params(("arbitrary",)),
    )(proj, proj, proj, dq, dk, dv, dgates, dff, conv_w, a_pad, dt_pad)


PAIR = 2 * CHUNK


def _split_bf16(a):
    hi = a.astype(BF)
    return hi, (a - hi.astype(F32)).astype(BF)


def _dot3(a, b, dims=(((1,), (0,)), ((), ()))):
    ah, al = _split_bf16(a)
    bh, bl = _split_bf16(b)
    (ca,), (cb,) = dims[0]
    return lax.dot_general(jnp.concatenate([ah, al, ah], axis=ca), jnp.concatenate([bh, bh, bl], axis=cb), dims,
                           preferred_element_type=F32)


def _inv_unit_lower(a):
    r = lax.broadcasted_iota(jnp.int32, (PAIR, PAIR), 0)
    c = lax.broadcasted_iota(jnp.int32, (PAIR, PAIR), 1)
    tm = (r == c).astype(F32) - a
    pw = _dot3(a, a)
    for _ in range(4):
        x = _dot3(jnp.concatenate([tm, pw], axis=0), pw)
        tm = tm + x[:PAIR]
        pw = x[PAIR:]
    return tm + _dot3(tm, pw)


def _gdn_pair_local(q, k, v, gc, gr, b):
    r = lax.broadcasted_iota(jnp.int32, (PAIR, PAIR), 0)
    c = lax.broadcasted_iota(jnp.int32, (PAIR, PAIR), 1)
    same = (r // CHUNK) == (c // CHUNK)
    incl = same & (r >= c)
    strict = same & (r > c)
    dm = jnp.exp(jnp.where(incl, gc - gr, NEG))
    e = jnp.exp(gc)
    kb = k * b
    vb = v * b
    kbe = kb * e
    kq = _dot_nt(jnp.concatenate([kb, q], axis=0).astype(BF), k.astype(BF))
    amat = jnp.where(strict, kq[:PAIR] * dm, 0.0)
    pmat = jnp.where(incl, kq[PAIR:] * dm, 0.0)
    lane = lax.broadcasted_iota(jnp.int32, (1, PAIR), 1)
    gl_a = jnp.sum(jnp.where(lane == CHUNK - 1, gr, 0.0), axis=1, keepdims=True)
    gl_b = jnp.sum(jnp.where(lane == PAIR - 1, gr, 0.0), axis=1, keepdims=True)
    ridx = lax.broadcasted_iota(jnp.int32, (PAIR, 1), 0)
    edec = jnp.exp(jnp.where(ridx < CHUNK, gl_a, gl_b) - gc)
    return dict(dm=dm, e=e, kb=kb, vb=vb, kbe=kbe, amat=amat, pmat=pmat, gl_a=gl_a, gl_b=gl_b, edec=edec,
                kd=k * edec, qd=q * e, incl=incl, strict=strict, ridx=ridx)


def _gdn_pair_states(loc, tb, s_a):
    uw = _dot(tb, jnp.concatenate([loc["vb"], loc["kbe"]], axis=1).astype(BF))
    u, w = uw[:, :LANES], uw[:, LANES:]
    qd, kd, c = loc["qd"], loc["kd"], CHUNK
    xa = _dot(jnp.concatenate([qd[:c], w[:c]], axis=0).astype(BF), s_a.astype(BF))
    vn_a = u[:c] - xa[c:]
    s_b = s_a * jnp.exp(loc["gl_a"]) + _dot_tn(kd[:c].astype(BF), vn_a.astype(BF))
    xb = _dot(jnp.concatenate([qd[c:], w[c:]], axis=0).astype(BF), s_b.astype(BF))
    vn_b = u[c:] - xb[c:]
    s_c = s_b * jnp.exp(loc["gl_b"]) + _dot_tn(kd[c:].astype(BF), vn_b.astype(BF))
    vn = jnp.concatenate([vn_a, vn_b], axis=0)
    o = jnp.concatenate([xa[:c], xb[:c]], axis=0) + _dot(loc["pmat"].astype(BF), vn.astype(BF))
    return w, vn, o, s_b, s_c


GDN_SEG = 512


def _gdn_specs(nb, seq, reverse):
    n = seq // CHUNK
    seg = _tile(seq, GDN_SEG)
    nseg = seq // seg
    sp = seg // PAIR
    at = (lambda s: nseg - 1 - s) if reverse else (lambda s: s)
    blk = pl.BlockSpec((nb, seg, GDN_WIDTH), lambda s: (0, at(s), 0))
    gg = pl.BlockSpec((nb, seg, GDN_WIDTH), lambda s: (0, at(s), GG_COL // GDN_WIDTH))
    gates = pl.BlockSpec((nb, seg, LANES), lambda s: (0, at(s), 0))
    rowb = pl.BlockSpec((nb, GDN_HEADS, sp, HALO, PAIR), lambda s: (0, 0, at(s), 0, 0))
    per_pair = pl.BlockSpec((nb, GDN_HEADS, sp, PAIR, PAIR), lambda s: (0, 0, at(s), 0, 0))
    return n, seg, nseg, sp, blk, gg, gates, rowb, per_pair


def _head_column(gt, lane, index):
    return jnp.sum(jnp.where(lane == index, gt, 0.0), axis=1, keepdims=True)


def _gdn_head_inputs(qkv_refs, gt_ref, gr_ref, rows, pi, lane, chains):
    per_chain = []
    for b, hh in chains:
        gt = gt_ref[b, rows, :]
        cols = slice(hh * LANES, (hh + 1) * LANES)
        per_chain.append([r[b, rows, cols].astype(F32) for r in qkv_refs]
                         + [_head_column(gt, lane, A_LANE + hh), gr_ref[b, hh, pi][0:1, :],
                            _head_column(gt, lane, B_LANE + hh)])
    return [jnp.stack(xs) for xs in zip(*per_chain)]


def _gdn_pair_fwd(qv, kv, vv, gcv, gr, bv, s_a):
    loc = _gdn_pair_local(qv, kv, vv, gcv, gr, bv)
    tf = _inv_unit_lower(loc["amat"])
    _, _, o, _, s_c = _gdn_pair_states(loc, tf.astype(BF), s_a)
    return tf, o, s_c


def _gdn_fwd(q, k, v, proj, gates, grow, wn, nb, seq, name):
    n, seg, nseg, sp, blk, gg, gates_spec, rowb, per_pair = _gdn_specs(nb, seq, False)
    chains = [(b, hh) for b in range(nb) for hh in range(GDN_HEADS)]

    def body(q_ref, k_ref, v_ref, gg_ref, gt_ref, gr_ref, wn_ref, y_ref, tn_ref, sn_ref, s_ref):
        @pl.when(pl.program_id(0) == 0)
        def _():
            s_ref[...] = jnp.zeros_like(s_ref)

        wnv = wn_ref[...]
        lane = lax.broadcasted_iota(jnp.int32, (PAIR, LANES), 1)

        def step(pi, carry):
            rows = pl.ds(pl.multiple_of(pi * PAIR, PAIR), PAIR)
            ins = _gdn_head_inputs((q_ref, k_ref, v_ref), gt_ref, gr_ref, rows, pi, lane, chains)
            s_a = s_ref[...]
            tf, o, s_c = jax.vmap(_gdn_pair_fwd)(*ins, s_a)
            s_ref[...] = s_c
            for c, (b, hh) in enumerate(chains):
                cols = slice(hh * LANES, (hh + 1) * LANES)
                tn_ref[b, hh, pi] = tf[c]
                sn_ref[b, hh, pi] = s_a[c]
                g = gg_ref[b, rows, cols]
                oh = o[c]
                rstd = lax.rsqrt(jnp.mean(oh * oh, axis=-1, keepdims=True) + EPS)
                y_ref[b, rows, cols] = (oh * rstd * wnv * (g * _sigmoid(g))).astype(BF)
            return carry

        lax.fori_loop(0, sp, step, 0)

    saved = jax.ShapeDtypeStruct((nb, GDN_HEADS, n // 2, PAIR, PAIR), F32)
    return pl.pallas_call(
        body, name=name, grid=(nseg,),
        in_specs=[blk, blk, blk, gg, gates_spec, rowb, _full((1, LANES))],
        out_specs=[blk, per_pair, per_pair],
        out_shape=[jax.ShapeDtypeStruct((nb, seq, GDN_WIDTH), BF), saved, saved],
        scratch_shapes=[pltpu.VMEM((len(chains), GDN_HEAD_DIM, GDN_HEAD_DIM), F32)],
        compiler_params=_params(("arbitrary",)),
    )(q, k, v, proj, gates, grow, wn)


def _gdn_pair_bwd(qv, kv, vv, gcv, gr, bv, tf, s_a, dsp, g, dyv, wnv):
    c = CHUNK
    loc = _gdn_pair_local(qv, kv, vv, gcv, gr, bv)
    tm = tf.astype(BF)
    kb, vb, kbe, e, dm = loc["kb"], loc["vb"], loc["kbe"], loc["e"], loc["dm"]
    kd, qd, pmat, amat = loc["kd"], loc["qd"], loc["pmat"], loc["amat"]
    w, vn, o, s_b, _ = _gdn_pair_states(loc, tm, s_a)
    sg = _sigmoid(g)
    silu = g * sg
    rstd = lax.rsqrt(jnp.mean(o * o, axis=-1, keepdims=True) + EPS)
    xhat = o * rstd
    dwn = jnp.sum(dyv * xhat * silu, axis=0, keepdims=True)
    dgg = dyv * xhat * wnv * (sg * (1.0 + g * (1.0 - sg)))
    dxhat = dyv * wnv * silu
    do = rstd * (dxhat - xhat * jnp.mean(dxhat * xhat, axis=-1, keepdims=True))
    dob = do.astype(BF)
    tot = lambda x: jnp.sum(jnp.sum(x, axis=1, keepdims=True), axis=0, keepdims=True)
    rsum = lambda x: jnp.sum(x, axis=1, keepdims=True)
    cat = lambda xs, ax=0: jnp.concatenate(xs, axis=ax)
    wb = w.astype(BF)
    qdb = qd.astype(BF)
    kdb = kd.astype(BF)
    vnb = vn.astype(BF)
    egl_a = jnp.exp(loc["gl_a"])
    egl_b = jnp.exp(loc["gl_b"])
    ptdo = _dot_tn(pmat.astype(BF), dob)
    dspb = dsp.astype(BF)
    dvn_b = ptdo[c:] + _dot(kdb[c:], dspb)
    dkd_b = _dot_nt(vnb[c:], dspb)
    dgl_b = egl_b * tot(s_b * dsp) + tot(dkd_b * kd[c:])
    dsm = egl_b * dsp + _dot_tn(cat([qdb[c:], -wb[c:]]), cat([dob[c:], dvn_b.astype(BF)]))
    dsmb = dsm.astype(BF)
    dvn_a = ptdo[:c] + _dot(kdb[:c], dsmb)
    dkd_a = _dot_nt(vnb[:c], dsmb)
    dgl_a = egl_a * tot(s_a * dsm) + tot(dkd_a * kd[:c])
    ds_new = egl_a * dsm + _dot_tn(cat([qdb[:c], -wb[:c]]), cat([dob[:c], dvn_a.astype(BF)]))
    ya = _dot_nt(cat([dob[:c], dvn_a.astype(BF)]), s_a.astype(BF))
    yb = _dot_nt(cat([dob[c:], dvn_b.astype(BF)]), s_b.astype(BF))
    dqd = cat([ya[:c], yb[:c]])
    dw = -cat([ya[c:], yb[c:]])
    dvn = cat([dvn_a, dvn_b])
    dkd = cat([dkd_a, dkd_b])
    dq = dqd * e
    dgc = rsum(dqd * qd) - rsum(dkd * kd)
    dk = dkd * loc["edec"]
    dpm = jnp.where(loc["incl"], _dot_nt(dob, vnb), 0.0)
    duw = cat([dvn, dw], 1).astype(BF)
    dt = _dot_nt(duw, cat([vb, kbe], 1).astype(BF))
    tt = _dot_tn(tm, duw)
    dvb, dkbe = tt[:, :LANES], tt[:, LANES:]
    tn_dims = (((0,), (0,)), ((), ()))
    nt_dims = (((1,), (1,)), ((), ()))
    da = jnp.where(loc["strict"], -_dot3(_dot3(tf, dt, tn_dims), tf, nt_dims), 0.0)
    st = cat([da * dm, dpm * dm]).astype(BF)
    z = _dot(st, kv.astype(BF))
    dkb = z[:PAIR] + dkbe * e
    dq = dq + z[PAIR:]
    dk = dk + _dot_tn(st, cat([kb, qv]).astype(BF))
    gmat = dpm * pmat + da * amat
    dgc = dgc + rsum(dkbe * kbe) + rsum(gmat)
    ridx = loc["ridx"]
    dgc = dgc + jnp.where(ridx == c - 1, dgl_a, 0.0) + jnp.where(ridx == PAIR - 1, dgl_b, 0.0)
    dgc_row = jnp.sum(gmat, axis=0, keepdims=True)
    db = rsum(dvb * vv) + rsum(dkb * kv)
    return dq, dk + dkb * bv, dvb * bv, dgg, dgc, dgc_row, db, dwn, ds_new


def _gdn_bwd(q, k, v, proj, gates, grow, wn, tinv_all, states_all, dy, nb, seq, name):
    n, seg, nseg, sp, blk, gg, gates_spec, rowb, per_pair = _gdn_specs(nb, seq, True)
    dh = GDN_HEAD_DIM
    chains = [(b, hh) for b in range(nb) for hh in range(GDN_HEADS)]

    def body(q_ref, k_ref, v_ref, gg_ref, gt_ref, gr_ref, wn_ref, tn_ref, sn_ref, dy_ref,
             dq_ref, dk_ref, dv_ref, dgg_ref, dgt_ref, dwn_ref, ds_ref):
        @pl.when(pl.program_id(0) == 0)
        def _():
            dwn_ref[...] = jnp.zeros_like(dwn_ref)
            ds_ref[...] = jnp.zeros_like(ds_ref)

        wnv = wn_ref[...]
        lane = lax.broadcasted_iota(jnp.int32, (PAIR, LANES), 1)

        def bwd_step(j, carry):
            pi = sp - 1 - j
            rows = pl.ds(pl.multiple_of(pi * PAIR, PAIR), PAIR)
            ins = _gdn_head_inputs((q_ref, k_ref, v_ref), gt_ref, gr_ref, rows, pi, lane, chains)
            lanes_of = lambda hh: slice(hh * LANES, (hh + 1) * LANES)
            saved = [jnp.stack([r[b, hh, pi] for b, hh in chains]) for r in (tn_ref, sn_ref)]
            g2 = jnp.stack([gg_ref[b, rows, lanes_of(hh)] for b, hh in chains])
            dy2 = jnp.stack([dy_ref[b, rows, lanes_of(hh)].astype(F32) for b, hh in chains])
            dq, dk, dv, dgg, dgc, dgc_row, db, dwn, ds_new = jax.vmap(
                _gdn_pair_bwd, in_axes=(0,) * 11 + (None,))(*ins, *saved, ds_ref[...], g2, dy2, wnv)
            ds_ref[...] = ds_new
            dgt = [jnp.zeros((PAIR, LANES), F32) for _ in range(nb)]
            for c, (b, hh) in enumerate(chains):
                cols = lanes_of(hh)
                dq_ref[b, rows, cols] = dq[c]
                dk_ref[b, rows, cols] = dk[c]
                dv_ref[b, rows, cols] = dv[c]
                dgg_ref[b, rows, cols] = dgg[c].astype(BF)
                dwn_ref[...] += dwn[c]
                row_as_col = jnp.transpose(jnp.broadcast_to(dgc_row[c], (PAIR, LANES)))
                dgt[b] = (dgt[b] + jnp.where(lane == A_LANE + hh, dgc[c] - row_as_col, 0.0)
                          + jnp.where(lane == B_LANE + hh, db[c], 0.0))
            for b in range(nb):
                dgt_ref[b, rows, :] = dgt[b]
            return carry

        lax.fori_loop(0, sp, bwd_step, 0)

    f32_out = jax.ShapeDtypeStruct((nb, seq, GDN_WIDTH), F32)
    return pl.pallas_call(
        body, name=name, grid=(nseg,),
        in_specs=[blk, blk, blk, gg, gates_spec, rowb, _full((1, LANES)), per_pair, per_pair, blk],
        out_specs=[blk, blk, blk, blk, gates_spec, _full((1, LANES))],
        out_shape=[f32_out, f32_out, f32_out, jax.ShapeDtypeStruct((nb, seq, GDN_WIDTH), BF),
                   jax.ShapeDtypeStruct((nb, seq, LANES), F32), jax.ShapeDtypeStruct((1, LANES), F32)],
        scratch_shapes=[pltpu.VMEM((len(chains), dh, dh), F32)],
        compiler_params=_params(("arbitrary",)),
    )(q, k, v, proj, gates, grow, wn, tinv_all, states_all, dy)


def _mix_to_padded(w):
    pad = jnp.zeros(w.shape[:-1] + (N_PAD - N_IN,), w.dtype)
    return jnp.concatenate([w[..., 0:1536], w[..., 1544:3080], w[..., 3088:3600], w[..., 1536:1544],
                            w[..., 3080:3088], pad], axis=-1)


def _pad_lanes(vec, start):
    return jnp.pad(vec[None, :], ((0, 0), (start, LANES - start - vec.shape[0])))


def _heads_to_rows(block, lane0, nheads, nb, seq):
    return block[:, lane0:lane0 + nheads].reshape(nb, seq, nheads).transpose(0, 2, 1).reshape(nb * nheads, seq)


def _mixer_small(p, l):
    wq_t = jnp.tile(p["fox_q_norm"][l], FOX_HEADS)[None, :]
    wk_t = jnp.tile(p["fox_k_norm"][l], FOX_HEADS)[None, :]
    bias = _pad_lanes(p["fox_f_bias"][l], 0)
    a_pad = _pad_lanes(p["gdn_a_log"][l], A_LANE)
    dt_pad = _pad_lanes(p["gdn_dt_bias"][l], A_LANE)
    wn = p["gdn_out_norm"][l][None, :]
    return wq_t, wk_t, bias, a_pad, dt_pad, wn


def _layer_fwd(x, p, l, nb, seq, rider=None, ffn1_rider=None, after_ffn1=None):
    npair = FOX_HEADS // 2
    n = seq // CHUNK
    x1, h1, *rode1 = _ffn_fwd(x, p["ffn1_norm"][l][None, :], p["ffn1_w_in"][l], p["ffn1_w_out"][l],
                              f"ffn1_fwd_{l}", ffn1_rider)
    if after_ffn1 is not None:
        after_ffn1(rode1)
    wq_t, wk_t, bias, a_pad, dt_pad, wn = _mixer_small(p, l)
    proj = _norm_matmul(x1, p["mix_norm"][l][None, :], p["w_mix"][l], f"mix_in_{l}")
    fq, fk, fv, cum = _fox_prep(proj, wq_t, wk_t, bias, seq, f"fox_prep_{l}")
    c8 = _heads_to_rows(cum, 0, FOX_HEADS, nb, seq)
    ck = c8.reshape(nb * npair, 2, 1, seq)
    o, lse, *rode = _fox_attn(fq, fk, fv, ck, nb, seq, f"fox_attn_{l}", rider)
    gq, gk, gv, gates = _gdn_prep(proj, p["gdn_conv"][l], a_pad, dt_pad, seq, f"gdn_prep_{l}")
    gc4 = _heads_to_rows(gates, A_LANE, GDN_HEADS, nb, seq)
    grow = jnp.broadcast_to(gc4.reshape(nb, GDN_HEADS, n // 2, 1, PAIR), (nb, GDN_HEADS, n // 2, HALO, PAIR))
    per_example = lambda a: a.reshape(nb, seq, a.shape[-1])
    gq, gk, gv, gates = per_example(gq), per_example(gk), per_example(gv), per_example(gates)
    y, tinv, states = _gdn_fwd(gq, gk, gv, per_example(proj), gates, grow, wn, nb, seq, f"gdn_fwd_{l}")
    y = y.reshape(nb * seq, GDN_WIDTH)
    x2 = _mix_out(x1, o, y, p["w_out"][l], f"mix_out_{l}")
    x3, h2 = _ffn_fwd(x2, p["ffn2_norm"][l][None, :], p["ffn2_w_in"][l], p["ffn2_w_out"][l], f"ffn2_fwd_{l}")
    saved = dict(x=x, h1=h1, x1=x1, proj=proj, fq=fq, fk=fk, fv=fv, ck=ck, o=o, lse=lse,
                 gq=gq, gk=gk, gv=gv, gates=gates, grow=grow, tinv=tinv, states=states, y=y, x2=x2, h2=h2)
    return x3, saved, rode


def _ffn_grads(dy, x, h, gain, win, wout, l, tag, rider=None):
    t, d = x.shape
    fs = win.shape[2]
    dx, dh, a, hn, dyh, dgain, *rode = _ffn_bwd(dy, x, h, gain, win, wout, f"{tag}_bwd_{l}", rider)
    g_in = _wgrad(hn, dh, jax.ShapeDtypeStruct((4, d, fs), BF),
                  pl.BlockSpec((None, d, fs), lambda i, j, k: (j, i, 0)), d, fs, f"{tag}_gw_in_{l}")
    g_out = _wgrad(a, dyh, jax.ShapeDtypeStruct((2 * fs, d), BF),
                   pl.BlockSpec((fs, d), lambda i, j, k: (i, j)), fs, d, f"{tag}_gw_out_{l}")
    return dx, dgain[0], g_in, g_out.reshape(4, fs // 2, d), rode


def _layer_bwd(dx3, p, l, sv, nb, seq, rider=None, before_ffn1=None, ffn2_rider=None, after_ffn2=None):
    npair = FOX_HEADS // 2
    d = dx3.shape[1]
    wq_t, wk_t, bias, a_pad, dt_pad, wn = _mixer_small(p, l)
    g = {}
    dx2, g["ffn2_norm"], g["ffn2_w_in"], g["ffn2_w_out"], rode2 = _ffn_grads(
        dx3, sv["x2"], sv["h2"], p["ffn2_norm"][l][None, :], p["ffn2_w_in"][l], p["ffn2_w_out"][l], l, "ffn2",
        ffn2_rider)
    if after_ffn2 is not None:
        rider = after_ffn2(rode2)
    dyf, dyg, dxb = _mix_out_bwd(dx2, p["w_out"][l], f"mix_out_bwd_{l}")
    half = lambda a, nm: _wgrad(a, dxb, jax.ShapeDtypeStruct((FOX_WIDTH, d), BF),
                                pl.BlockSpec((FOX_WIDTH, d), lambda i, j, k: (i, j)), FOX_WIDTH, d, nm)
    g["w_out"] = jnp.concatenate([half(sv["o"], f"gw_out_fox_{l}"), half(sv["y"], f"gw_out_gdn_{l}")], axis=0)
    dqa, dqb, dk, dv, dkx, *rode = _fox_attn_bwd(sv["fq"], sv["fk"], sv["fv"], sv["o"], dyf, sv["lse"], sv["ck"],
                                                 nb, seq, f"fox_attn_bwd_{l}", rider)

    dpf, dff, dwq, dwk, dbias = _fox_prep_bwd(sv["proj"], dqa, dqb, dk, dv, dkx, wq_t, wk_t, bias, seq,
                                              f"fox_prep_bwd_{l}")
    g["fox_q_norm"] = dwq[0, :FOX_HEAD_DIM]
    g["fox_k_norm"] = dwk[0, :FOX_HEAD_DIM]
    g["fox_f_bias"] = dbias[0, :FOX_HEADS]
    per_example = lambda a: a.reshape(nb, seq, a.shape[-1])
    flat = lambda a: a.reshape(nb * seq, a.shape[-1])
    dgq, dgk, dgv, dgg, dgates, dwn = _gdn_bwd(
        sv["gq"], sv["gk"], sv["gv"], per_example(sv["proj"]), sv["gates"], sv["grow"], wn, sv["tinv"],
        sv["states"], per_example(dyg), nb, seq, f"gdn_bwd_{l}")
    dgq, dgk, dgv, dgg, dgates = flat(dgq), flat(dgk), flat(dgv), flat(dgg), flat(dgates)
    dpg, dgate_blk, dconv, da, ddt = _gdn_prep_bwd(sv["proj"], dgq, dgk, dgv, dgates, dff, p["gdn_conv"][l],
                                                   a_pad, dt_pad, seq, f"gdn_prep_bwd_{l}")
    g["gdn_conv"] = dconv
    g["gdn_a_log"] = da[0, A_LANE:B_LANE]
    g["gdn_dt_bias"] = ddt[0, A_LANE:B_LANE]
    g["gdn_out_norm"] = dwn[0]
    dparts = [dpf, dpg, dgg, dgate_blk]
    dx1, hnm, dgm = _norm_matmul_bwd(dx2, dparts, sv["x1"], p["mix_norm"][l][None, :], p["w_mix"][l],
                                     f"mix_in_bwd_{l}")
    g["mix_norm"] = dgm[0]
    gp = _wgrad_parts(hnm, dparts, d // 2, f"gw_mix_{l}")
    gate = GATE_COL
    g["w_in"] = jnp.concatenate([gp[:, :GDN_COL], gp[:, gate:gate + FOX_HEADS], gp[:, GDN_COL:GG_COL],
                                 gp[:, gate + A_LANE:gate + B_LANE + GDN_HEADS], gp[:, GG_COL:gate]], axis=1)
    ffn1_rider = before_ffn1(g) if before_ffn1 is not None else None
    dx0, g["ffn1_norm"], g["ffn1_w_in"], g["ffn1_w_out"], rode1 = _ffn_grads(
        dx1, sv["x"], sv["h1"], p["ffn1_norm"][l][None, :], p["ffn1_w_in"][l], p["ffn1_w_out"][l], l, "ffn1",
        ffn1_rider)
    return dx0, g, rode, rode1


N_CHIPS = 4


def _mesh_pos():
    return lax.axis_index("x"), lax.axis_index("y"), lax.axis_index("c")


def _other_chips(x, y):
    return [(1 - x, y), (x, 1 - y), (1 - x, 1 - y)]


def _remote(src, dst, send_sem, recv_sem, to):
    return pltpu.make_async_remote_copy(src_ref=src, dst_ref=dst, send_sem=send_sem, recv_sem=recv_sem,
                                        device_id=to, device_id_type=MESH)


def _hbm_call(body, name, ins, out_shape, scratch):
    return pl.pallas_call(
        body, name=name, out_shape=out_shape, in_specs=[HBM] * len(ins),
        out_specs=jax.tree.map(lambda _: HBM, out_shape), scratch_shapes=scratch,
        compiler_params=pltpu.CompilerParams(has_side_effects=True),
    )(*ins)


def _gather_phases(n, layer):
    def copies(ins, outs, sems):
        send1, recv1, send2, recv2 = sems
        x, y, c = _mesh_pos()
        out, back, fwd = [], [], []
        for i in range(n):
            for j, (px, py) in enumerate(_other_chips(x, y)):
                k = 3 * i + j
                blk = outs[i].at[2 * px + py]
                out.append(_remote(ins[i], outs[i].at[2 * x + y], send1.at[k], recv1.at[k], (px, py, c)))
                back.append(_remote(blk, blk, send1.at[k], recv1.at[k], (px, py, c)))
                fwd.append(_remote(blk, blk, send2.at[k], recv2.at[k], (x, y, 1 - c)))
        return c, out, back, fwd

    def first(ins, outs, sems):
        c, out, _, _ = copies(ins, outs, sems)

        @pl.when(c == layer)
        def _():
            for cp in out:
                cp.start()

    def middle(ins, outs, sems):
        c, _, back, fwd = copies(ins, outs, sems)

        @pl.when(c == layer)
        def _():
            for arrived, onward in zip(back, fwd):
                arrived.wait_recv()
                onward.start()

    def last(ins, outs, sems):
        c, out, _, fwd = copies(ins, outs, sems)

        @pl.when(c == layer)
        def _():
            for cp in out + fwd:
                cp.wait_send()

        @pl.when(c != layer)
        def _():
            for cp in fwd:
                cp.wait_recv()

    return first, middle, last


def _scatter_phases(n, layer):
    def copies(ins, outs, sems):
        send, recv = sems
        x, y, c = _mesh_pos()
        return c, [_remote(ins[i].at[2 * px + py], outs[i].at[j], send.at[3 * i + j], recv.at[3 * i + j], (px, py, c))
                   for i in range(n) for j, (px, py) in enumerate(_other_chips(x, y))]

    def first(ins, outs, sems):
        c, cps = copies(ins, outs, sems)

        @pl.when(c == layer)
        def _():
            for cp in cps:
                cp.start()

    def middle(ins, outs, sems):
        pass

    def last(ins, outs, sems):
        c, cps = copies(ins, outs, sems)

        @pl.when(c == layer)
        def _():
            for cp in cps:
                cp.wait()

    return first, middle, last


def _exchange(blocks, out_shapes, n_sems, phases, name, rider):
    sems = [pltpu.SemaphoreType.DMA((3 * len(blocks),))] * n_sems
    if rider:
        return _Rider(blocks, out_shapes, sems, phases)
    n = len(blocks)

    def body(*refs):
        for phase in phases:
            phase(refs[:n], refs[n:2 * n], refs[2 * n:])

    return list(_hbm_call(body, name, blocks, out_shapes, sems))


def _gather_layer(blocks, layer, name=None, rider=False):
    outs = [jax.ShapeDtypeStruct((N_CHIPS,) + b.shape, b.dtype) for b in blocks]
    return _exchange(blocks, outs, 4, _gather_phases(len(blocks), layer), name, rider)


def _scatter_layer(sums, layer, name=None, rider=False):
    outs = [jax.ShapeDtypeStruct((3,) + s.shape[1:], s.dtype) for s in sums]
    return _exchange(sums, outs, 2, _scatter_phases(len(sums), layer), name, rider)


def _to_sibling(gs, layer, name=None, rider=False):
    n = len(gs)

    def copies(ins, outs, sems):
        send, recv = sems
        x, y, c = _mesh_pos()
        return c, [_remote(ins[i], outs[i], send.at[i], recv.at[i], (x, y, 1 - c)) for i in range(n)]

    def first(ins, outs, sems):
        c, cps = copies(ins, outs, sems)

        @pl.when(c != layer)
        def _():
            for cp in cps:
                cp.start()

    def middle(ins, outs, sems):
        pass

    def last(ins, outs, sems):
        c, cps = copies(ins, outs, sems)

        @pl.when(c != layer)
        def _():
            for cp in cps:
                cp.wait_send()

        @pl.when(c == layer)
        def _():
            for cp in cps:
                cp.wait_recv()

    sems = [pltpu.SemaphoreType.DMA((n,))] * 2
    outs = [jax.ShapeDtypeStruct(g.shape, g.dtype) for g in gs]
    if rider:
        return _Rider(gs, outs, sems, (first, middle, last))

    def body(*refs):
        for phase in (first, middle, last):
            phase(refs[:n], refs[n:2 * n], refs[2 * n:])

    return list(_hbm_call(body, name, gs, outs, sems))


def _sibling_swap(rs, name):
    n = len(rs)

    def body(*refs):
        ins, outs = refs[:n], refs[n:2 * n]
        send, recv = refs[2 * n:]
        x, y, c = _mesh_pos()
        cps = [_remote(ins[i], outs[i], send.at[i], recv.at[i], (x, y, 1 - c)) for i in range(n)]
        for cp in cps:
            cp.start()
        for cp in cps:
            cp.wait()

    sem = pltpu.SemaphoreType.DMA((n,))
    return _hbm_call(body, name, rs, [jax.ShapeDtypeStruct(r.shape, r.dtype) for r in rs], [sem, sem])


def _small_all_reduce(vec, name):
    r = vec.shape[0]
    ndev = 8

    def body(v_ref, o_ref, buf, send, recv):
        x, y, c = _mesh_pos()
        me = 4 * x + 2 * y + c
        buf[me] = v_ref[...]
        cps = []
        for rel in range(1, ndev):
            px = 1 - x if rel & 4 else x
            py = 1 - y if rel & 2 else y
            pc = 1 - c if rel & 1 else c
            cps.append((_remote(v_ref, buf.at[me], send.at[rel - 1], recv.at[rel - 1], (px, py, pc)),
                        4 * px + 2 * py + pc))
        for cp, _ in cps:
            cp.start()
        for k, (cp, peer) in enumerate(cps):
            slot = buf.at[peer]
            _remote(slot, slot, send.at[k], recv.at[k], (x, y, c)).wait_recv()
        for cp, _ in cps:
            cp.wait_send()
        acc = buf[0]
        for k in range(1, ndev):
            acc = acc + buf[k]
        o_ref[...] = acc

    vm = pl.BlockSpec(memory_space=pltpu.VMEM)
    return pl.pallas_call(
        body, name=name, out_shape=jax.ShapeDtypeStruct(vec.shape, F32), in_specs=[vm], out_specs=vm,
        scratch_shapes=[pltpu.VMEM((ndev, r, LANES), F32), pltpu.SemaphoreType.DMA((ndev - 1,)),
                        pltpu.SemaphoreType.DMA((ndev - 1,))],
        compiler_params=pltpu.CompilerParams(has_side_effects=True),
    )(vec)


def _row_tile(rows, cap=SUM_ROWS):
    for t in range(min(rows, cap), 0, -1):
        if rows % t == 0 and (t % 16 == 0 or t == rows):
            return t
    raise ValueError(rows)


def _add_pairs(a, b, name):
    k, r, c = a.shape
    tr = _row_tile(r)

    def body(a_ref, b_ref, o_ref):
        o_ref[...] = (a_ref[...].astype(F32) + b_ref[...].astype(F32)).astype(o_ref.dtype)

    spec = pl.BlockSpec((None, tr, c), lambda i, j: (i, j, 0))
    return pl.pallas_call(body, name=name, grid=(k, r // tr), in_specs=[spec, spec], out_specs=spec,
                          out_shape=jax.ShapeDtypeStruct(a.shape, a.dtype),
                          compiler_params=_params(("parallel", "parallel")))(a, b)


def _final_sum(own, sib, others, name):
    r, c = own.shape
    tr = _row_tile(r)

    def body(a_ref, b_ref, o_ref_in, out_ref):
        acc = a_ref[...].astype(F32) + b_ref[...].astype(F32)
        for k in range(3):
            acc = acc + o_ref_in[k].astype(F32)
        out_ref[...] = acc

    spec = pl.BlockSpec((tr, c), lambda i: (i, 0))
    return pl.pallas_call(body, name=name, grid=(r // tr,),
                          in_specs=[spec, spec, pl.BlockSpec((3, tr, c), lambda i: (0, i, 0))], out_specs=spec,
                          out_shape=jax.ShapeDtypeStruct((r, c), F32),
                          compiler_params=_params(("parallel",)))(own, sib, others)


def _adamw(g, w, m, v, name):
    r, c = g.shape
    tr = _row_tile(r, ADAM_ROWS)

    def body(g_ref, w_ref, m_ref, v_ref, d_ref, mo_ref, vo_ref):
        gv = g_ref[...]
        mn = ADAM_B1 * m_ref[...] + (1.0 - ADAM_B1) * gv
        vn = ADAM_B2 * v_ref[...] + (1.0 - ADAM_B2) * (gv * gv)
        m_hat = mn / (1.0 - ADAM_B1 ** ADAM_STEP)
        v_hat = vn / (1.0 - ADAM_B2 ** ADAM_STEP)
        d_ref[...] = -ADAM_LR * (m_hat / (jnp.sqrt(v_hat) + ADAM_EPS) + ADAM_WD * w_ref[...])
        mo_ref[...] = mn
        vo_ref[...] = vn

    spec = pl.BlockSpec((tr, c), lambda i: (i, 0))
    shp = jax.ShapeDtypeStruct((r, c), F32)
    return pl.pallas_call(body, name=name, grid=(r // tr,), in_specs=[spec] * 4, out_specs=[spec] * 3,
                          out_shape=[shp] * 3, compiler_params=_params(("parallel",)))(g, w, m, v)


def _pack(arrays):
    flat = jnp.concatenate([a.reshape(-1).astype(F32) for a in arrays])
    pad = (-flat.shape[0]) % (8 * LANES)
    return jnp.concatenate([flat, jnp.zeros((pad,), F32)]).reshape(-1, LANES)


def _unpack(packed, shapes):
    flat = packed.reshape(-1)
    out, off = [], 0
    for s in shapes:
        size = 1
        for dim in s:
            size *= dim
        out.append(flat[off:off + size].reshape(s))
        off += size
    return out


BIG = ("ffn1_w_in", "ffn1_w_out", "w_in", "w_out", "ffn2_w_in", "ffn2_w_out")
SMALL = ("ffn1_norm", "mix_norm", "fox_q_norm", "fox_k_norm", "fox_f_bias", "gdn_a_log", "gdn_dt_bias",
         "gdn_out_norm", "ffn2_norm", "gdn_conv")
WEIGHTS = ("ffn1_norm", "ffn1_w_in", "ffn1_w_out", "mix_norm", "w_in", "fox_q_norm", "fox_k_norm", "fox_f_bias",
           "gdn_conv", "gdn_a_log", "gdn_dt_bias", "gdn_out_norm", "w_out", "ffn2_norm", "ffn2_w_in", "ffn2_w_out")


def _step(x, target, w, m, v):
    xi, yi, ci = _mesh_pos()
    me = 2 * xi + yi
    depth = DEPTH
    d = x.shape[-1]

    nb, seq, _ = x.shape
    assert depth == 2

    p ={k: w[k] for k in SMALL if k != "gdn_conv"}
    for k in ("ffn1_w_in", "ffn1_w_out", "ffn2_w_in", "ffn2_w_out", "w_mix", "w_out", "gdn_conv"):
        p[k] = [None] * depth

    first, rest = BIG[:2], BIG[2:] + ("gdn_conv",)

    def shards(l, names):
        return [w[k][l] if k == "gdn_conv" else w[k][l].astype(BF) for k in names]

    def place(l, names, gathered):
        blocks = dict(zip(names, [lax.dynamic_update_index_in_dim(g, s, me, 0)
                                  for g, s in zip(gathered, shards(l, names))]))
        for k in ("ffn1_w_in", "ffn1_w_out", "ffn2_w_in", "ffn2_w_out"):
            if k in blocks:
                p[k][l] = blocks[k]
        if "w_in" in blocks:
            p["w_mix"][l] = _mix_to_padded(blocks["w_in"].transpose(1, 0, 2).reshape(d, N_IN))
            p["w_out"][l] = blocks["w_out"].reshape(2 * FOX_WIDTH, d)
            p["gdn_conv"][l] = blocks["gdn_conv"].transpose(1, 0, 2).reshape(CONV_WIDTH, -1)

    place(0, first, _gather_layer(shards(0, first), 0, "gather_first_ffn0"))
    xt = x.reshape(nb * seq, d)
    xt, saved0, gathered1 = _layer_fwd(
        xt, p, 0, nb, seq, _gather_layer(shards(1, first + rest), 1, rider=True),
        _gather_layer(shards(0, rest), 0, rider=True), lambda got: place(0, rest, got))
    place(1, first + rest, gathered1)
    xt, saved1, _ = _layer_fwd(xt, p, 1, nb, seq)
    loss, dx = _loss_grad(xt, target.reshape(nb * seq, d), "loss")

    def transport(g, names):
        out = []
        for k in names:
            if k == "w_in":
                out.append(g["w_in"].reshape(d, N_CHIPS, N_IN // N_CHIPS).transpose(1, 0, 2).astype(BF))
            elif k == "w_out":
                out.append(g["w_out"].reshape(N_CHIPS, -1, d))
            else:
                out.append(g[k])
        return out

    def chip_sums(g, l, names, tag):
        own = transport(g, names)
        sib = _to_sibling(own, l, f"grad{l}{tag}_to_sibling")
        return own, sib, [_add_pairs(a, b, f"grad{l}{tag}_chip_sum_{k}") for a, b, k in zip(own, sib, names)]

    dx, grads1, _, _ = _layer_bwd(dx, p, 1, saved1, nb, seq)
    own1 = transport(grads1, BIG)
    before = {}

    def after_ffn2(from_sibling):
        before["sib1"] = from_sibling
        sums1 = [_add_pairs(a, b, f"grad1_chip_sum_{k}") for a, b, k in zip(own1, from_sibling, BIG)]
        return _scatter_layer(sums1, 1, rider=True)

    def before_ffn1(g):
        before["own"], before["sib"], sums = chip_sums(g, 0, BIG[2:], "_rest")
        return _scatter_layer(sums, 0, rider=True)

    dx, grads0, chips1, chips0_rest = _layer_bwd(dx, p, 0, saved0, nb, seq, None, before_ffn1,
                                                 _to_sibling(own1, 1, rider=True), after_ffn2)
    sib1 = before["sib1"]
    own0, sib0, sums0 = chip_sums(grads0, 0, first, "_first")
    chips0 = _scatter_layer(sums0, 0, "grad0_first_to_chips") + chips0_rest
    own0, sib0 = own0 + before["own"], sib0 + before["sib"]
    grads = [grads0, grads1]
    dx = dx.reshape(nb, seq, d)

    mine = lambda a0, a1: jnp.where(ci == 0, a0, a1)
    at_me = lambda a: lax.dynamic_index_in_dim(a, me, 0, keepdims=False)
    reduced = [_final_sum(mine(at_me(own0[i]), at_me(own1[i])), mine(at_me(sib0[i]), at_me(sib1[i])),
                          mine(chips0[i], chips1[i]), f"grad_final_sum_{k}") for i, k in enumerate(BIG)]
    from_sib_final = _sibling_swap(reduced, "grad_swap_layers")
    full = {k: jnp.stack([jnp.where(ci == 0, a, b), jnp.where(ci == 0, b, a)])
            for k, a, b in zip(BIG, reduced, from_sib_final)}

    out_g, out_d, out_m, out_v = {}, {}, {}, {}
    for k in BIG:
        shp = w[k].shape
        two_d = lambda a: a.reshape(shp[0] * shp[1], shp[2])
        dl, mn, vn = _adamw(two_d(full[k]), two_d(w[k]), two_d(m[k]), two_d(v[k]), f"adamw_{k}")
        out_g[k], out_d[k], out_m[k], out_v[k] = full[k], dl.reshape(shp), mn.reshape(shp), vn.reshape(shp)

    small_local = [jnp.stack([grads[l][k] for l in range(depth)]) for k in SMALL] + [loss.reshape(1)]
    summed = _unpack(_small_all_reduce(_pack(small_local), "small_all_reduce"), [a.shape for a in small_local])
    total = summed.pop()[0]
    sg = dict(zip(SMALL, summed))
    cs = w["gdn_conv"].shape[-1]
    sg["gdn_conv"] = lax.dynamic_slice_in_dim(sg["gdn_conv"], me * cs, cs, axis=2)
    shapes = [w[k].shape for k in SMALL]
    packs = [_pack([src[k] for k in SMALL]) for src in (sg, w, m, v)]
    dl, mn, vn = _adamw(*packs, "adamw_small")
    for k, a, b, c2 in zip(SMALL, _unpack(dl, shapes), _unpack(mn, shapes), _unpack(vn, shapes)):
        out_g[k], out_d[k], out_m[k], out_v[k] = sg[k], a, b, c2

    return (total, dx, *[out_g[k] for k in WEIGHTS], *[out_d[k] for k in WEIGHTS],
            *[out_m[k] for k in WEIGHTS], *[out_v[k] for k in WEIGHTS])


def kernel(x, ffn1_norm, ffn1_w_in, ffn1_w_out, mix_norm, w_in, fox_q_norm, fox_k_norm, fox_f_bias, gdn_conv, gdn_a_log, gdn_dt_bias, gdn_out_norm, w_out, ffn2_norm, ffn2_w_in, ffn2_w_out, loss_target, m_ffn1_norm, m_ffn1_w_in, m_ffn1_w_out, m_mix_norm, m_w_in, m_fox_q_norm, m_fox_k_norm, m_fox_f_bias, m_gdn_conv, m_gdn_a_log, m_gdn_dt_bias, m_gdn_out_norm, m_w_out, m_ffn2_norm, m_ffn2_w_in, m_ffn2_w_out, v_ffn1_norm, v_ffn1_w_in, v_ffn1_w_out, v_mix_norm, v_w_in, v_fox_q_norm, v_fox_k_norm, v_fox_f_bias, v_gdn_conv, v_gdn_a_log, v_gdn_dt_bias, v_gdn_out_norm, v_w_out, v_ffn2_norm, v_ffn2_w_in, v_ffn2_w_out):
    w = dict(ffn1_norm=ffn1_norm, ffn1_w_in=ffn1_w_in, ffn1_w_out=ffn1_w_out, mix_norm=mix_norm, w_in=w_in,
             fox_q_norm=fox_q_norm, fox_k_norm=fox_k_norm, fox_f_bias=fox_f_bias, gdn_conv=gdn_conv,
             gdn_a_log=gdn_a_log, gdn_dt_bias=gdn_dt_bias, gdn_out_norm=gdn_out_norm, w_out=w_out,
             ffn2_norm=ffn2_norm, ffn2_w_in=ffn2_w_in, ffn2_w_out=ffn2_w_out)
    m = dict(ffn1_norm=m_ffn1_norm, ffn1_w_in=m_ffn1_w_in, ffn1_w_out=m_ffn1_w_out, mix_norm=m_mix_norm, w_in=m_w_in,
             fox_q_norm=m_fox_q_norm, fox_k_norm=m_fox_k_norm, fox_f_bias=m_fox_f_bias, gdn_conv=m_gdn_conv,
             gdn_a_log=m_gdn_a_log, gdn_dt_bias=m_gdn_dt_bias, gdn_out_norm=m_gdn_out_norm, w_out=m_w_out,
             ffn2_norm=m_ffn2_norm, ffn2_w_in=m_ffn2_w_in, ffn2_w_out=m_ffn2_w_out)
    v = dict(ffn1_norm=v_ffn1_norm, ffn1_w_in=v_ffn1_w_in, ffn1_w_out=v_ffn1_w_out, mix_norm=v_mix_norm, w_in=v_w_in,
             fox_q_norm=v_fox_q_norm, fox_k_norm=v_fox_k_norm, fox_f_bias=v_fox_f_bias, gdn_conv=v_gdn_conv,
             gdn_a_log=v_gdn_a_log, gdn_dt_bias=v_gdn_dt_bias, gdn_out_norm=v_gdn_out_norm, w_out=v_w_out,
             ffn2_norm=v_ffn2_norm, ffn2_w_in=v_ffn2_w_in, ffn2_w_out=v_ffn2_w_out)
    return _step(x, loss_target, w, m, v)
```

```python
import jax
import jax.numpy as jnp
from jax import lax
from jax.experimental import pallas as pl
from jax.experimental.pallas import tpu as pltpu

F32 = jnp.float32
BF = jnp.bfloat16
HI = lax.Precision.HIGHEST
MESH = pl.DeviceIdType.MESH

DEPTH = 2
FOX_HEADS = 8
FOX_HEAD_DIM = 64
FOX_WIDTH = 512
GDN_HEADS = 4
GDN_HEAD_DIM = 128
GDN_WIDTH = 512
CONV_WIDTH = 4
CHUNK = 64
EPS = 1e-6
N_IN = 3600
N_PAD = 3712
GATE_COL = 3584
LANES = 128
NEG = -1e30

ADAM_LR = 0.001
ADAM_B1 = 0.9
ADAM_B2 = 0.999
ADAM_EPS = 1e-08
ADAM_WD = 0.01
ADAM_STEP = 10

VMEM_LIMIT = 56 * 1024 * 1024

TOKEN_TILE = 512
TOKEN_TILE_BWD = 256
WGRAD_TOKENS = 512
FOX_PREP_TILE = 512
GDN_PREP_TILE = 256
ATTN_FWD_TILE = 2048
ATTN_BWD_TILE = 1024
DIAGONAL_STRIPS = 2
SUM_ROWS = 512
ADAM_ROWS = 256


def _params(sem=None, **kw):
    return pltpu.CompilerParams(dimension_semantics=sem, vmem_limit_bytes=VMEM_LIMIT, **kw)


def _dot(a, b, precision=None):
    return jnp.dot(a, b, preferred_element_type=F32, precision=precision)


def _dot_nt(a, b, precision=None):
    return lax.dot_general(a, b, (((1,), (1,)), ((), ())), preferred_element_type=F32, precision=precision)


def _dot_tn(a, b, precision=None):
    return lax.dot_general(a, b, (((0,), (0,)), ((), ())), preferred_element_type=F32, precision=precision)


def _sigmoid(x):
    return 0.5 * jnp.tanh(0.5 * x) + 0.5


def _softplus(x):
    return jnp.maximum(x, 0.0) + jnp.log(1.0 + jnp.exp(-jnp.abs(x)))


def _log_sigmoid(x):
    return jnp.minimum(x, 0.0) - jnp.log(1.0 + jnp.exp(-jnp.abs(x)))


def _tile(n, t):
    t = min(n, t)
    assert n % t == 0, (n, t)
    return t


def _rms_fwd(x, gain):
    rstd = lax.rsqrt(jnp.mean(x * x, axis=-1, keepdims=True) + EPS)
    xhat = x * rstd
    return xhat * gain, xhat, rstd


def _rms_bwd(dy, xhat, rstd, gain):
    dxhat = dy * gain
    dx = rstd * (dxhat - xhat * jnp.mean(dxhat * xhat, axis=-1, keepdims=True))
    return dx, dy * xhat


def _full(shape):
    nd = len(shape)
    return pl.BlockSpec(shape, lambda *_: (0,) * nd)


HBM = pl.BlockSpec(memory_space=pltpu.HBM)


def _load_ffn_weights(win_hbm, wout_hbm, win_v, wout_v, sem):
    fr = wout_hbm.shape[1]
    copies = [pltpu.make_async_copy(win_hbm.at[s], win_v.at[s], sem.at[s]) for s in range(4)]
    copies += [pltpu.make_async_copy(wout_hbm.at[s], wout_v.at[pl.ds(s * fr, fr)], sem.at[4 + s])
               for s in range(4)]
    for c in copies:
        c.start()
    for c in copies:
        c.wait()


def _ffn_fwd(x, gain, win_g, wout_g, name, rider=None):
    t, d = x.shape
    _, _, fs = win_g.shape
    fr = wout_g.shape[1]
    tm = _tile(t, TOKEN_TILE)
    r_in, r_out, r_sem = _rider_parts(rider)
    steps = t // tm

    def body(x_ref, g_ref, win_hbm, wout_hbm, *rest):
        rin, (xo_ref, h_ref) = rest[:len(r_in)], rest[len(r_in):len(r_in) + 2]
        rout = rest[len(r_in) + 2:len(r_in) + 2 + len(r_out)]
        win_v, wout_v, sem = rest[len(r_in) + 2 + len(r_out):len(r_in) + 5 + len(r_out)]
        riding = (rin, rout, rest[len(r_in) + 5 + len(r_out):])
        step = pl.program_id(0)
        _ride(rider, 0, step == 0, riding)
        _ride(rider, 1, step == (13 * steps) // 16, riding)

        @pl.when(step == 0)
        def _():
            _load_ffn_weights(win_hbm, wout_hbm, win_v, wout_v, sem)

        xv = x_ref[...]
        hn, _, _ = _rms_fwd(xv, g_ref[...])
        hn = hn.astype(BF)
        acc = jnp.zeros((tm, d), F32)
        for s in range(2):
            g = _dot(hn, win_v[s])
            u = _dot(hn, win_v[s + 2])
            h_ref[:, s * fs:(s + 1) * fs] = g.astype(BF)
            h_ref[:, (s + 2) * fs:(s + 3) * fs] = u.astype(BF)
            a = (g * _sigmoid(g) * u).astype(BF)
            acc = acc + _dot(a, wout_v[s * fs:(s + 1) * fs, :])
        xo_ref[...] = xv + 0.5 * acc
        _ride(rider, 2, step == steps - 1, riding)

    return pl.pallas_call(
        body, name=name, grid=(steps,),
        in_specs=[pl.BlockSpec((tm, d), lambda i: (i, 0)), _full((1, d)), HBM, HBM] + [HBM] * len(r_in),
        out_specs=[pl.BlockSpec((tm, d), lambda i: (i, 0)), pl.BlockSpec((tm, 4 * fs), lambda i: (i, 0))]
        + [HBM] * len(r_out),
        out_shape=[jax.ShapeDtypeStruct((t, d), F32), jax.ShapeDtypeStruct((t, 4 * fs), BF)] + r_out,
        scratch_shapes=[pltpu.VMEM((4, d, fs), BF), pltpu.VMEM((4 * fr, d), BF), pltpu.SemaphoreType.DMA((8,))]
        + r_sem,
        compiler_params=_params(("arbitrary",), has_side_effects=rider is not None),
    )(x, gain, win_g, wout_g, *r_in)


def _ffn_bwd(dy, x, h, gain, win_g, wout_g, name, rider=None):
    t, d = x.shape
    _, _, fs = win_g.shape
    fr = wout_g.shape[1]
    tm = _tile(t, TOKEN_TILE_BWD)
    r_in, r_out, r_sem = _rider_parts(rider)
    steps = t // tm

    def body(dy_ref, x_ref, h_ref, g_ref, win_hbm, wout_hbm, *rest):
        rin, (dx_ref, dh_ref, a_ref, hn_ref, dyh_ref, dg_ref) = rest[:len(r_in)], rest[len(r_in):len(r_in) + 6]
        rout = rest[len(r_in) + 6:len(r_in) + 6 + len(r_out)]
        win_v, wout_v, sem = rest[len(r_in) + 6 + len(r_out):len(r_in) + 9 + len(r_out)]
        riding = (rin, rout, rest[len(r_in) + 9 + len(r_out):])
        step = pl.program_id(0)
        _ride(rider, 0, step == 0, riding)
        _ride(rider, 1, step == (13 * steps) // 16, riding)

        @pl.when(step == 0)
        def _():
            _load_ffn_weights(win_hbm, wout_hbm, win_v, wout_v, sem)
            dg_ref[...] = jnp.zeros_like(dg_ref)

        dyv = dy_ref[...]
        dyh = (0.5 * dyv).astype(BF)
        dyh_ref[...] = dyh
        dhn = jnp.zeros((tm, d), F32)
        for s in range(2):
            da = _dot_nt(dyh, wout_v[s * fs:(s + 1) * fs, :])
            g = h_ref[:, s * fs:(s + 1) * fs].astype(F32)
            u = h_ref[:, (s + 2) * fs:(s + 3) * fs].astype(F32)
            sg = _sigmoid(g)
            si = g * sg
            a_ref[:, s * fs:(s + 1) * fs] = (si * u).astype(BF)
            dgate = (da * u * (sg * (1.0 + g * (1.0 - sg)))).astype(BF)
            dup = (da * si).astype(BF)
            dh_ref[:, s * fs:(s + 1) * fs] = dgate
            dh_ref[:, (s + 2) * fs:(s + 3) * fs] = dup
            dhn = dhn + _dot_nt(dgate, win_v[s]) + _dot_nt(dup, win_v[s + 2])
        xv = x_ref[...]
        gain_v = g_ref[...]
        hn, xhat, rstd = _rms_fwd(xv, gain_v)
        hn_ref[...] = hn.astype(BF)
        dx, dgr = _rms_bwd(dhn, xhat, rstd, gain_v)
        dx_ref[...] = dyv + dx
        dg_ref[...] += jnp.sum(dgr, axis=0, keepdims=True)
        _ride(rider, 2, step == steps - 1, riding)

    row = lambda w: pl.BlockSpec((tm, w), lambda i: (i, 0))
    return pl.pallas_call(
        body, name=name, grid=(steps,),
        in_specs=[row(d), row(d), row(4 * fs), _full((1, d)), HBM, HBM] + [HBM] * len(r_in),
        out_specs=[row(d), row(4 * fs), row(2 * fs), row(d), row(d), _full((1, d))] + [HBM] * len(r_out),
        out_shape=[jax.ShapeDtypeStruct((t, d), F32), jax.ShapeDtypeStruct((t, 4 * fs), BF),
                   jax.ShapeDtypeStruct((t, 2 * fs), BF), jax.ShapeDtypeStruct((t, d), BF),
                   jax.ShapeDtypeStruct((t, d), BF), jax.ShapeDtypeStruct((1, d), F32)] + r_out,
        scratch_shapes=[pltpu.VMEM((4, d, fs), BF), pltpu.VMEM((4 * fr, d), BF), pltpu.SemaphoreType.DMA((8,))]
        + r_sem,
        compiler_params=_params(("arbitrary",), has_side_effects=rider is not None),
    )(dy, x, h, gain, win_g, wout_g, *r_in)


def _wgrad(a, b, out_shape, out_spec, tm, tn, name, tk=WGRAD_TOKENS):
    t, m = a.shape
    _, n = b.shape
    tk = _tile(t, tk)
    nk = t // tk

    def body(a_ref, b_ref, o_ref, acc):
        k = pl.program_id(2)

        @pl.when(k == 0)
        def _():
            acc[...] = jnp.zeros_like(acc)

        acc[...] += _dot_tn(a_ref[...], b_ref[...])

        @pl.when(k == nk - 1)
        def _():
            o_ref[...] = acc[...].astype(o_ref.dtype)

    return pl.pallas_call(
        body, name=name, grid=(m // tm, n // tn, nk),
        in_specs=[pl.BlockSpec((tk, tm), lambda i, j, k: (k, i)), pl.BlockSpec((tk, tn), lambda i, j, k: (k, j))],
        out_specs=out_spec, out_shape=out_shape,
        scratch_shapes=[pltpu.VMEM((tm, tn), F32)],
        compiler_params=_params(("parallel", "parallel", "arbitrary")),
    )(a, b)


def _wgrad_parts(a, parts, tm, name, tk=WGRAD_TOKENS):
    t, m = a.shape
    widths = [p.shape[1] for p in parts]
    n = sum(widths)
    tk = _tile(t, tk)
    nk = t // tk
    np_ = len(parts)

    def body(a_ref, *rest):
        b_refs, o_ref, acc = rest[:np_], rest[np_], rest[np_ + 1]
        k = pl.program_id(1)

        @pl.when(k == 0)
        def _():
            acc[...] = jnp.zeros_like(acc)

        av, off = a_ref[...], 0
        for b_ref, wd in zip(b_refs, widths):
            acc[:, off:off + wd] += _dot_tn(av, b_ref[...])
            off += wd

        @pl.when(k == nk - 1)
        def _():
            o_ref[...] = acc[...]

    return pl.pallas_call(
        body, name=name, grid=(m // tm, nk),
        in_specs=[pl.BlockSpec((tk, tm), lambda i, k: (k, i))]
        + [pl.BlockSpec((tk, wd), lambda i, k: (k, 0)) for wd in widths],
        out_specs=pl.BlockSpec((tm, n), lambda i, k: (i, 0)), out_shape=jax.ShapeDtypeStruct((m, n), F32),
        scratch_shapes=[pltpu.VMEM((tm, n), F32)],
        compiler_params=_params(("parallel", "arbitrary")),
    )(a, *parts)


def _norm_matmul(x, gain, w, name):
    t, d = x.shape
    n = w.shape[1]
    tm = _tile(t, TOKEN_TILE)

    def body(x_ref, g_ref, w_ref, o_ref):
        hn, _, _ = _rms_fwd(x_ref[...], g_ref[...])
        o_ref[...] = _dot(hn.astype(BF), w_ref[...])

    return pl.pallas_call(
        body, name=name, grid=(t // tm,),
        in_specs=[pl.BlockSpec((tm, d), lambda i: (i, 0)), _full((1, d)), _full((d, n))],
        out_specs=pl.BlockSpec((tm, n), lambda i: (i, 0)),
        out_shape=jax.ShapeDtypeStruct((t, n), F32),
        compiler_params=_params(("parallel",)),
    )(x, gain, w)


def _norm_matmul_bwd(dres, dparts, x, gain, w, name):
    t, d = x.shape
    n = w.shape[1]
    tm = _tile(t, TOKEN_TILE)
    widths = [a.shape[1] for a in dparts]
    assert sum(widths) == n
    k = len(dparts)

    def body(dr_ref, *rest):
        dp_refs, (x_ref, g_ref, w_ref, dx_ref, hn_ref, dg_ref) = rest[:k], rest[k:]

        @pl.when(pl.program_id(0) == 0)
        def _():
            dg_ref[...] = jnp.zeros_like(dg_ref)

        dhn, off = jnp.zeros((tm, d), F32), 0
        for dp_ref, wd in zip(dp_refs, widths):
            dhn = dhn + _dot_nt(dp_ref[...], w_ref[:, off:off + wd])
            off += wd
        gain_v = g_ref[...]
        hn, xhat, rstd = _rms_fwd(x_ref[...], gain_v)
        hn_ref[...] = hn.astype(BF)
        dx, dgr = _rms_bwd(dhn, xhat, rstd, gain_v)
        dx_ref[...] = dr_ref[...] + dx
        dg_ref[...] += jnp.sum(dgr, axis=0, keepdims=True)

    row = lambda wd: pl.BlockSpec((tm, wd), lambda i: (i, 0))
    return pl.pallas_call(
        body, name=name, grid=(t // tm,),
        in_specs=[row(d)] + [row(wd) for wd in widths] + [row(d), _full((1, d)), _full((d, n))],
        out_specs=[row(d), row(d), _full((1, d))],
        out_shape=[jax.ShapeDtypeStruct((t, d), F32), jax.ShapeDtypeStruct((t, d), BF),
                   jax.ShapeDtypeStruct((1, d), F32)],
        compiler_params=_params(("arbitrary",)),
    )(dres, *dparts, x, gain, w)


def _mix_out(x, yf, yg, w, name):
    t, d = x.shape
    kf = yf.shape[1]
    tm = _tile(t, TOKEN_TILE)

    def body(x_ref, yf_ref, yg_ref, w_ref, o_ref):
        o_ref[...] = x_ref[...] + _dot(yf_ref[...], w_ref[0:kf, :]) + _dot(yg_ref[...], w_ref[kf:2 * kf, :])

    row = lambda wd: pl.BlockSpec((tm, wd), lambda i: (i, 0))
    return pl.pallas_call(
        body, name=name, grid=(t // tm,),
        in_specs=[row(d), row(kf), row(kf), _full((2 * kf, d))],
        out_specs=row(d), out_shape=jax.ShapeDtypeStruct((t, d), F32),
        compiler_params=_params(("parallel",)),
    )(x, yf, yg, w)


def _mix_out_bwd(dx, w, name):
    t, d = dx.shape
    kf = w.shape[0] // 2
    tm = _tile(t, TOKEN_TILE)

    def body(dx_ref, w_ref, df_ref, dg_ref, dxb_ref):
        dxb = dx_ref[...].astype(BF)
        dxb_ref[...] = dxb
        df_ref[...] = _dot_nt(dxb, w_ref[0:kf, :]).astype(BF)
        dg_ref[...] = _dot_nt(dxb, w_ref[kf:2 * kf, :]).astype(BF)

    row = lambda wd: pl.BlockSpec((tm, wd), lambda i: (i, 0))
    return pl.pallas_call(
        body, name=name, grid=(t // tm,),
        in_specs=[row(d), _full((2 * kf, d))],
        out_specs=[row(kf), row(kf), row(d)],
        out_shape=[jax.ShapeDtypeStruct((t, kf), BF), jax.ShapeDtypeStruct((t, kf), BF),
                   jax.ShapeDtypeStruct((t, d), BF)],
        compiler_params=_params(("parallel",)),
    )(dx, w)


def _loss_grad(y, target, name):
    t, d = y.shape
    tm = _tile(t, TOKEN_TILE)

    def body(y_ref, t_ref, l_ref, dy_ref):
        @pl.when(pl.program_id(0) == 0)
        def _():
            l_ref[...] = jnp.zeros_like(l_ref)

        diff = y_ref[...] - t_ref[...]
        dy_ref[...] = diff * (1.0 / d)
        part = jnp.sum(jnp.sum(diff * diff, axis=1, keepdims=True), axis=0, keepdims=True)
        l_ref[...] += part * (0.5 / d)

    row = pl.BlockSpec((tm, d), lambda i: (i, 0))
    return pl.pallas_call(
        body, name=name, grid=(t // tm,),
        in_specs=[row, row], out_specs=[_full((1, 1)), row],
        out_shape=[jax.ShapeDtypeStruct((1, 1), F32), jax.ShapeDtypeStruct((t, d), F32)],
        compiler_params=_params(("arbitrary",)),
    )(y, target)


def _head_sum_matrix(width, head):
    r = lax.broadcasted_iota(jnp.int32, (width, width), 0) // head
    c = lax.broadcasted_iota(jnp.int32, (width, width), 1) // head
    return (r == c).astype(BF)


def _head_mean(x, bd):
    return _dot(x.astype(BF), bd) * (1.0 / FOX_HEAD_DIM)


def _mask_dot(mask01, x):
    mb = mask01.astype(BF)
    hi = x.astype(BF)
    r1 = x - hi.astype(F32)
    mid = r1.astype(BF)
    lo = (r1 - mid.astype(F32)).astype(BF)
    return _dot(mb, hi) + _dot(mb, mid) + _dot(mb, lo)


def _fox_prep(proj, wq_t, wk_t, bias_pad, seq, name):
    t = proj.shape[0]
    ts = _tile(seq, FOX_PREP_TILE)
    tpe = seq // ts
    scale = FOX_HEAD_DIM ** -0.5

    def body(q_ref, k_ref, v_ref, gt_ref, wq_ref, wk_ref, b_ref, qo_ref, ko_ref, vo_ref, cum_ref, carry):
        i = pl.program_id(0)
        bd = _head_sum_matrix(FOX_WIDTH, FOX_HEAD_DIM)

        def norm(xv, wv):
            ms = _head_mean(xv * xv, bd)
            return xv * lax.rsqrt(ms + EPS) * wv

        qo_ref[...] = (norm(q_ref[...], wq_ref[...]) * scale).astype(BF)
        ko_ref[...] = norm(k_ref[...], wk_ref[...]).astype(BF)
        vo_ref[...] = v_ref[...].astype(BF)

        @pl.when(i % tpe == 0)
        def _():
            carry[...] = jnp.zeros_like(carry)

        ls = _log_sigmoid(gt_ref[...] + b_ref[...])
        r = lax.broadcasted_iota(jnp.int32, (ts, ts), 0)
        c = lax.broadcasted_iota(jnp.int32, (ts, ts), 1)
        cum = _mask_dot(r >= c, ls) + carry[...]
        cum_ref[...] = cum
        carry[...] = cum[ts - 1:ts, :]

    blk = lambda j: pl.BlockSpec((ts, FOX_WIDTH), lambda i: (i, j))
    gate = pl.BlockSpec((ts, LANES), lambda i: (i, GATE_COL // LANES))
    out = pl.BlockSpec((ts, FOX_WIDTH), lambda i: (i, 0))
    return pl.pallas_call(
        body, name=name, grid=(t // ts,),
        in_specs=[blk(0), blk(1), blk(2), gate, _full((1, FOX_WIDTH)), _full((1, FOX_WIDTH)), _full((1, LANES))],
        out_specs=[out, out, out, pl.BlockSpec((ts, LANES), lambda i: (i, 0))],
        out_shape=[jax.ShapeDtypeStruct((t, FOX_WIDTH), BF)] * 3 + [jax.ShapeDtypeStruct((t, LANES), F32)],
        scratch_shapes=[pltpu.VMEM((1, LANES), F32)],
        compiler_params=_params(("arbitrary",)),
    )(proj, proj, proj, proj, wq_t, wk_t, bias_pad)


def _pick_head_sums(x):
    r = lax.broadcasted_iota(jnp.int32, (FOX_WIDTH, LANES), 0)
    c = lax.broadcasted_iota(jnp.int32, (FOX_WIDTH, LANES), 1)
    sel = (((r % LANES == FOX_HEAD_DIM) & (c == 2 * (r // LANES)))
           | ((r % LANES == 0) & (c == 2 * (r // LANES) + 1))).astype(BF)
    hi = x.astype(BF)
    r1 = x - hi.astype(F32)
    mid = r1.astype(BF)
    lo = (r1 - mid.astype(F32)).astype(BF)
    return _dot(hi, sel) + _dot(mid, sel) + _dot(lo, sel)


def _fox_prep_bwd(proj, dqa, dqb, dk, dv, dkx, wq_t, wk_t, bias_pad, seq, name):
    t = proj.shape[0]
    ts = _tile(seq, FOX_PREP_TILE)
    tpe = seq // ts
    nt = t // ts
    scale = FOX_HEAD_DIM ** -0.5

    def body(q_ref, k_ref, gt_ref, dqa_ref, dqb_ref, dk_ref, dv_ref, dc_ref, wq_ref, wk_ref, b_ref,
             dp_ref, dff_ref, dwq_ref, dwk_ref, db_ref, carry):
        i = pl.program_id(0)
        first = (lax.broadcasted_iota(jnp.int32, (ts, FOX_WIDTH), 1) % LANES) < FOX_HEAD_DIM
        dq_all = jnp.where(first, dqa_ref[...], dqb_ref[...])
        ti = nt - 1 - i
        bd = _head_sum_matrix(FOX_WIDTH, FOX_HEAD_DIM)

        @pl.when(i == 0)
        def _():
            dwq_ref[...] = jnp.zeros_like(dwq_ref)
            dwk_ref[...] = jnp.zeros_like(dwk_ref)
            db_ref[...] = jnp.zeros_like(db_ref)

        def norm_bwd(xv, wv, dyv):
            ms = _head_mean(xv * xv, bd)
            rstd = lax.rsqrt(ms + EPS)
            xhat = xv * rstd
            dxhat = dyv * wv
            mean = _head_mean(dxhat * xhat, bd)
            return rstd * (dxhat - xhat * mean), jnp.sum(dyv * xhat, axis=0, keepdims=True)

        dxq, dwq = norm_bwd(q_ref[...], wq_ref[...], dq_all * scale)
        dxk, dwk = norm_bwd(k_ref[...], wk_ref[...], dk_ref[...])
        dp_ref[:, 0:FOX_WIDTH] = dxq.astype(BF)
        dp_ref[:, FOX_WIDTH:2 * FOX_WIDTH] = dxk.astype(BF)
        dp_ref[:, 2 * FOX_WIDTH:3 * FOX_WIDTH] = dv_ref[...].astype(BF)
        dwq_ref[...] += dwq
        dwk_ref[...] += dwk

        @pl.when(ti % tpe == tpe - 1)
        def _():
            carry[...] = jnp.zeros_like(carry)

        r = lax.broadcasted_iota(jnp.int32, (ts, ts), 0)
        c = lax.broadcasted_iota(jnp.int32, (ts, ts), 1)
        dcum = _pick_head_sums(jnp.where(first, dqb_ref[...], dqa_ref[...]) - dc_ref[...])
        dls = _mask_dot(c >= r, dcum) + carry[...]
        carry[...] = dls[0:1, :]
        z = gt_ref[...] + b_ref[...]
        lane = lax.broadcasted_iota(jnp.int32, (ts, LANES), 1)
        dff = jnp.where(lane < FOX_HEADS, dls * _sigmoid(-z), 0.0)
        dff_ref[...] = dff
        db_ref[...] += jnp.sum(dff, axis=0, keepdims=True)

        @pl.when(i == nt - 1)
        def _():
            fr = lax.broadcasted_iota(jnp.int32, (FOX_WIDTH, FOX_WIDTH), 0) % FOX_HEAD_DIM
            fc = lax.broadcasted_iota(jnp.int32, (FOX_WIDTH, FOX_WIDTH), 1) % FOX_HEAD_DIM
            fold = (fr == fc).astype(F32)
            dwq_ref[...] = _dot(dwq_ref[...], fold, HI)
            dwk_ref[...] = _dot(dwk_ref[...], fold, HI)

    rev = lambda w, j: pl.BlockSpec((ts, w), lambda i: (nt - 1 - i, j))
    return pl.pallas_call(
        body, name=name, grid=(nt,),
        in_specs=[rev(FOX_WIDTH, 0), rev(FOX_WIDTH, 1), rev(LANES, GATE_COL // LANES),
                  rev(FOX_WIDTH, 0), rev(FOX_WIDTH, 0), rev(FOX_WIDTH, 0), rev(FOX_WIDTH, 0), rev(FOX_WIDTH, 0),
                  _full((1, FOX_WIDTH)), _full((1, FOX_WIDTH)), _full((1, LANES))],
        out_specs=[rev(3 * FOX_WIDTH, 0), rev(LANES, 0), _full((1, FOX_WIDTH)), _full((1, FOX_WIDTH)),
                   _full((1, LANES))],
        out_shape=[jax.ShapeDtypeStruct((t, 3 * FOX_WIDTH), BF), jax.ShapeDtypeStruct((t, LANES), F32),
                   jax.ShapeDtypeStruct((1, FOX_WIDTH), F32), jax.ShapeDtypeStruct((1, FOX_WIDTH), F32),
                   jax.ShapeDtypeStruct((1, LANES), F32)],
        scratch_shapes=[pltpu.VMEM((1, LANES), F32)],
        compiler_params=_params(("arbitrary",)),
    )(proj, proj, proj, dqa, dqb, dk, dv, dkx, wq_t, wk_t, bias_pad)


class _Rider:
    def __init__(self, inputs, out_shapes, sems, phases):
        self.inputs, self.out_shapes, self.sems, self.phases = list(inputs), list(out_shapes), list(sems), phases


def _rider_parts(rider):
    if rider is None:
        return [], [], []
    return rider.inputs, rider.out_shapes, rider.sems


def _ride(rider, which, when, refs):
    if rider is not None:
        @pl.when(when)
        def _():
            rider.phases[which](*refs)


def _fox_attn(q, k, v, ck, nb, seq, name, rider=None):
    t = q.shape[0]
    tq = _tile(seq, ATTN_FWD_TILE)
    nq = seq // tq
    npair = FOX_HEADS // 2
    hd = FOX_HEAD_DIM
    r_in, r_out, r_sem = _rider_parts(rider)
    steps = nb * npair * nq

    def body(q_ref, k_ref, v_ref, ck_ref, *rest):
        rin, (o_ref, lse_ref) = rest[:len(r_in)], rest[len(r_in):len(r_in) + 2]
        rout = rest[len(r_in) + 2:len(r_in) + 2 + len(r_out)]
        m_s, acc_s = rest[len(r_in) + 2 + len(r_out):len(r_in) + 4 + len(r_out)]
        riding = (rin, rout, rest[len(r_in) + 4 + len(r_out):])
        step = (pl.program_id(0) * npair + pl.program_id(1)) * nq + pl.program_id(2)
        _ride(rider, 0, step == 0, riding)
        _ride(rider, 1, step == (13 * steps) // 16, riding)
        qi = pl.program_id(2)
        lane = lax.broadcasted_iota(jnp.int32, (tq, LANES), 1)
        m_s[...] = jnp.full(m_s.shape, NEG, F32)
        acc_s[...] = jnp.zeros_like(acc_s)
        qv = q_ref[...]

        def block(kj, r0, nr, nc, on_diagonal):
            cols = pl.ds(pl.multiple_of(kj * tq, tq), nc)
            rows = slice(r0, r0 + nr)
            kv = k_ref[cols, :]
            vv = v_ref[cols, :]
            qr = qv[rows]
            lanes = lane[rows]
            if on_diagonal:
                causal = (r0 + lax.broadcasted_iota(jnp.int32, (nr, nc), 0)
                          >= lax.broadcasted_iota(jnp.int32, (nr, nc), 1))
            for hh in range(2):
                hm = (lanes >= hd) if hh else (lanes < hd)
                qh = jnp.where(hm, qr, jnp.zeros_like(qr))
                s = _dot_nt(qh, kv) - ck_ref[hh, :, cols]
                if on_diagonal:
                    s = jnp.where(causal, s, NEG)
                m_old = m_s[hh, rows]
                m_new = jnp.maximum(m_old, jnp.max(s, axis=-1, keepdims=True))
                p = jnp.exp(s - m_new)
                alpha = jnp.exp(m_old - m_new)
                m_s[hh, rows] = m_new
                vh = jnp.where(lane[:nc] >= hd if hh else lane[:nc] < hd, vv, jnp.ones_like(vv))
                acc_s[hh, rows] = alpha * acc_s[hh, rows] + _dot(p.astype(BF), vh)

        def off_diagonal(kj, carry):
            block(kj, 0, tq, tq, False)
            return carry

        lax.fori_loop(0, qi, off_diagonal, 0)
        strip = tq // DIAGONAL_STRIPS
        for i in range(DIAGONAL_STRIPS):
            block(qi, i * strip, strip, (i + 1) * strip, True)
        a0 = acc_s[0]
        a1 = acc_s[1]
        den = jnp.where(lane < hd, pltpu.roll(a0, hd, axis=1), pltpu.roll(a1, hd, axis=1))
        o_ref[...] = (jnp.where(lane < hd, a0, a1) / den).astype(o_ref.dtype)
        l0 = jnp.sum(jnp.where(lane == hd, a0, 0.0), axis=1, keepdims=True)
        l1 = jnp.sum(jnp.where(lane == 0, a1, 0.0), axis=1, keepdims=True)
        lse_ref[0] = m_s[0] + jnp.log(l0)
        lse_ref[1] = m_s[1] + jnp.log(l1)
        _ride(rider, 2, step == steps - 1, riding)

    qspec = pl.BlockSpec((tq, LANES), lambda b, p, i: (b * nq + i, p))
    kspec = pl.BlockSpec((seq, LANES), lambda b, p, i: (b, p))
    colspec = pl.BlockSpec((None, 2, tq, 1), lambda b, p, i: (b * npair + p, 0, i, 0))
    rowspec = pl.BlockSpec((None, 2, 1, seq), lambda b, p, i: (b * npair + p, 0, 0, 0))
    sem = ("arbitrary",) * 3 if rider else ("parallel",) * 3
    return pl.pallas_call(
        body, name=name, grid=(nb, npair, nq),
        in_specs=[qspec, kspec, kspec, rowspec] + [HBM] * len(r_in),
        out_specs=[qspec, colspec] + [HBM] * len(r_out),
        out_shape=[jax.ShapeDtypeStruct((t, FOX_WIDTH), BF), jax.ShapeDtypeStruct((nb * npair, 2, seq, 1), F32)]
        + r_out,
        scratch_shapes=[pltpu.VMEM((2, tq, 1), F32), pltpu.VMEM((2, tq, LANES), F32)] + r_sem,
        compiler_params=_params(sem, has_side_effects=rider is not None),
    )(q, k, v, ck, *r_in)


def _fox_attn_bwd(q, k, v, o, do, lse, ck, nb, seq, name, rider=None):
    t = q.shape[0]
    tq = _tile(seq, ATTN_BWD_TILE)
    nq = seq // tq
    npair = FOX_HEADS // 2
    hd = FOX_HEAD_DIM
    r_in, r_out, r_sem = _rider_parts(rider)
    steps = nb * npair * nq

    def body(q_ref, k_ref, v_ref, o_ref, do_ref, lse_ref, ck_ref, *rest):
        rin, (dqa_ref, dqb_ref, dk_ref, dv_ref, dkx_ref) = rest[:len(r_in)], rest[len(r_in):len(r_in) + 5]
        rout = rest[len(r_in) + 5:len(r_in) + 5 + len(r_out)]
        dk_s, dv_s = rest[len(r_in) + 5 + len(r_out):len(r_in) + 7 + len(r_out)]
        riding = (rin, rout, rest[len(r_in) + 7 + len(r_out):])
        step = (pl.program_id(0) * npair + pl.program_id(1)) * nq + pl.program_id(2)
        _ride(rider, 0, step == 0, riding)
        _ride(rider, 1, step == (13 * steps) // 16, riding)
        kj = pl.program_id(2)
        lane = lax.broadcasted_iota(jnp.int32, (tq, LANES), 1)

        @pl.when(kj == 0)
        def _():
            dqa_ref[...] = jnp.zeros_like(dqa_ref)
            dqb_ref[...] = jnp.zeros_like(dqb_ref)

        dk_s[...] = jnp.zeros_like(dk_s)
        dv_s[...] = jnp.zeros_like(dv_s)
        kv = k_ref[...]
        vv = v_ref[...]

        def block(qi, r0, nr, nc, on_diagonal):
            rows = pl.ds(pl.multiple_of(qi * tq, tq) + r0, nr)
            keys = slice(0, nc)
            qv = q_ref[rows, :]
            dov = do_ref[rows, :]
            kc, vc = kv[keys], vv[keys]
            prod = dov.astype(F32) * o_ref[rows, :].astype(F32)
            lq, lk = lane[:nr], lane[:nc]
            if on_diagonal:
                causal = (r0 + lax.broadcasted_iota(jnp.int32, (nr, nc), 0)
                          >= lax.broadcasted_iota(jnp.int32, (nr, nc), 1))
            for hh, dq_ref in ((0, dqa_ref), (1, dqb_ref)):
                hm = (lq >= hd) if hh else (lq < hd)
                hk = (lk >= hd) if hh else (lk < hd)
                zero = jnp.zeros_like(qv)
                doh = jnp.where(hm, dov, zero)
                delta = jnp.sum(jnp.where(hm, prod, 0.0), axis=-1, keepdims=True)
                s = _dot_nt(jnp.where(hm, qv, zero), kc) - ck_ref[hh, :, keys]
                if on_diagonal:
                    s = jnp.where(causal, s, NEG)
                p = jnp.exp(s - lse_ref[hh, rows, :])
                dp = _dot_nt(doh, vc)
                dsb = (p * (dp - delta)).astype(BF)
                dv_s[keys] += _dot_tn(p.astype(BF), doh)
                dk_s[hh, keys] += _dot_tn(dsb, jnp.where(hm, qv, jnp.ones_like(qv)))
                dq_ref[rows, :] += _dot(dsb, jnp.where(hk, kc, jnp.ones_like(kc)))

        def off_diagonal(qi, carry):
            block(qi, 0, tq, tq, False)
            return carry

        strip = tq // DIAGONAL_STRIPS
        for i in range(DIAGONAL_STRIPS):
            block(kj, i * strip, strip, (i + 1) * strip, True)
        lax.fori_loop(kj + 1, nq, off_diagonal, 0)
        dk_ref[...] = jnp.where(lane < hd, dk_s[0], dk_s[1])
        dkx_ref[...] = jnp.where(lane < hd, dk_s[1], dk_s[0])
        dv_ref[...] = dv_s[...]
        _ride(rider, 2, step == steps - 1, riding)

    kspec = pl.BlockSpec((tq, LANES), lambda b, p, j: (b * nq + j, p))
    full_q = pl.BlockSpec((seq, LANES), lambda b, p, j: (b, p))
    colspec = pl.BlockSpec((None, 2, seq, 1), lambda b, p, j: (b * npair + p, 0, 0, 0))
    rowspec = pl.BlockSpec((None, 2, 1, tq), lambda b, p, j: (b * npair + p, 0, 0, j))
    sem = ("arbitrary",) * 3 if rider else ("parallel", "parallel", "arbitrary")
    return pl.pallas_call(
        body, name=name, grid=(nb, npair, nq),
        in_specs=[full_q, kspec, kspec, full_q, full_q, colspec, rowspec] + [HBM] * len(r_in),
        out_specs=[full_q, full_q, kspec, kspec, kspec] + [HBM] * len(r_out),
        out_shape=[jax.ShapeDtypeStruct((t, FOX_WIDTH), F32)] * 5 + r_out,
        scratch_shapes=[pltpu.VMEM((2, tq, LANES), F32), pltpu.VMEM((tq, LANES), F32)] + r_sem,
        compiler_params=_params(sem, has_side_effects=rider is not None),
    )(q, k, v, o, do, lse, ck, *r_in)


GDN_QKV = 3 * GDN_WIDTH
GDN_COL = 3 * FOX_WIDTH
GG_COL = GDN_COL + GDN_QKV
A_LANE = FOX_HEADS
B_LANE = FOX_HEADS + GDN_HEADS
HALO = 8


def _gate_lanes(ts):
    lane = lax.broadcasted_iota(jnp.int32, (ts, LANES), 1)
    return (lane >= A_LANE) & (lane < B_LANE), (lane >= B_LANE) & (lane < B_LANE + GDN_HEADS)


def _chunk_tri(ts, upper):
    r = lax.broadcasted_iota(jnp.int32, (ts, ts), 0)
    c = lax.broadcasted_iota(jnp.int32, (ts, ts), 1)
    same = (r // CHUNK) == (c // CHUNK)
    return (same & ((c >= r) if upper else (r >= c))).astype(F32)


def _shift_rows(x, edge, k, down):
    ts = x.shape[0]
    row = lax.broadcasted_iota(jnp.int32, (HALO, x.shape[1]), 0)
    if down:
        rolled = pltpu.roll(x, k, axis=0)
        patch = jnp.where(row < k, pltpu.roll(edge, k, axis=0), rolled[:HALO])
        return jnp.concatenate([patch, rolled[HALO:]], axis=0)
    rolled = pltpu.roll(x, ts - k, axis=0)
    patch = jnp.where(row >= HALO - k, pltpu.roll(edge, HALO - k, axis=0), rolled[ts - HALO:])
    return jnp.concatenate([rolled[:ts - HALO], patch], axis=0)


def _conv_silu(x, before, w):
    taps = [_shift_rows(x, before, CONV_WIDTH - 1 - kk, True) for kk in range(CONV_WIDTH - 1)] + [x]
    c = w[0:1, :] * taps[0]
    for kk in range(1, CONV_WIDTH):
        c = c + w[kk:kk + 1, :] * taps[kk]
    return taps, c, c * _sigmoid(c)


def _gdn_prep(proj, conv_w, a_pad, dt_pad, seq, name):
    t = proj.shape[0]
    ts = _tile(seq, GDN_PREP_TILE)
    tpe = seq // ts
    qscale = GDN_HEAD_DIM ** -0.5

    def body(x_ref, gt_ref, w_ref, a_ref, dt_ref, qo_ref, ko_ref, vo_ref, go_ref, tail):
        i = pl.program_id(0)
        xv = x_ref[...]
        before = jnp.where(i % tpe == 0, jnp.zeros((HALO, GDN_QKV), F32), tail[...])
        tail[...] = xv[ts - HALO:]
        _, _, s = _conv_silu(xv, before, w_ref[...])
        for h in range(GDN_HEADS):
            for base, ref, sc in ((0, qo_ref, qscale), (GDN_WIDTH, ko_ref, 1.0)):
                xh = s[:, base + h * LANES: base + (h + 1) * LANES]
                r = lax.rsqrt(jnp.sum(xh * xh, axis=-1, keepdims=True) + EPS)
                ref[:, h * LANES:(h + 1) * LANES] = (xh * (r * sc)).astype(BF)
        vo_ref[...] = s[:, 2 * GDN_WIDTH:].astype(BF)
        gate = gt_ref[...]
        g_raw = -jnp.exp(a_ref[...]) * _softplus(gate + dt_ref[...])
        gc = _mask_dot(_chunk_tri(ts, False), g_raw)
        is_a, is_b = _gate_lanes(ts)
        go_ref[...] = jnp.where(is_a, gc, jnp.where(is_b, _sigmoid(gate), 0.0))

    out = pl.BlockSpec((ts, GDN_WIDTH), lambda i: (i, 0))
    lanes = pl.BlockSpec((ts, LANES), lambda i: (i, 0))
    return pl.pallas_call(
        body, name=name, grid=(t // ts,),
        in_specs=[pl.BlockSpec((ts, GDN_QKV), lambda i: (i, GDN_COL // GDN_QKV)),
                  pl.BlockSpec((ts, LANES), lambda i: (i, GATE_COL // LANES)),
                  _full((CONV_WIDTH, GDN_QKV)), _full((1, LANES)), _full((1, LANES))],
        out_specs=[out, out, out, lanes],
        out_shape=[jax.ShapeDtypeStruct((t, GDN_WIDTH), BF)] * 3 + [jax.ShapeDtypeStruct((t, LANES), F32)],
        scratch_shapes=[pltpu.VMEM((HALO, GDN_QKV), F32)],
        compiler_params=_params(("arbitrary",)),
    )(proj, proj, conv_w, a_pad, dt_pad)


def _gdn_prep_bwd(proj, dq, dk, dv, dgates, dff, conv_w, a_pad, dt_pad, seq, name):
    t = proj.shape[0]
    ts = _tile(seq, GDN_PREP_TILE)
    tpe = seq // ts
    nt = t // ts
    qscale = GDN_HEAD_DIM ** -0.5
    hb = ts // HALO

    def body(x_ref, halo_ref, gt_ref, dq_ref, dk_ref, dv_ref, dgt_ref, dff_ref, w_ref, a_ref, dt_ref,
             dx_ref, dgo_ref, dw_ref, da_ref, ddt_ref, dsl, carry):
        i = pl.program_id(0)
        ti = nt - 1 - i

        @pl.when(i == 0)
        def _():
            dw_ref[...] = jnp.zeros_like(dw_ref)
            da_ref[...] = jnp.zeros_like(da_ref)
            ddt_ref[...] = jnp.zeros_like(ddt_ref)

        halo = halo_ref[...]
        before = jnp.where(ti % tpe == 0, jnp.zeros_like(halo), halo)
        w = w_ref[...]
        taps, c, s = _conv_silu(x_ref[...], before, w)
        for h in range(GDN_HEADS):
            for base, ref, sc in ((0, dq_ref, qscale), (GDN_WIDTH, dk_ref, 1.0)):
                lo = base + h * LANES
                xh = s[:, lo:lo + LANES]
                r = lax.rsqrt(jnp.sum(xh * xh, axis=-1, keepdims=True) + EPS)
                y = xh * r
                dy = ref[:, h * LANES:(h + 1) * LANES] * sc
                dsl[:, lo:lo + LANES] = r * (dy - y * jnp.sum(dy * y, axis=-1, keepdims=True))
        dsl[:, 2 * GDN_WIDTH:] = dv_ref[...]
        sg = _sigmoid(c)
        dc = dsl[...] * (sg * (1.0 + c * (1.0 - sg)))
        nxt = carry[...]
        after = jnp.where(ti % tpe == tpe - 1, jnp.zeros_like(nxt), nxt)
        carry[...] = dc[0:HALO, :]
        dx = w[CONV_WIDTH - 1:CONV_WIDTH, :] * dc
        for kk in range(CONV_WIDTH - 1):
            dx = dx + w[kk:kk + 1, :] * _shift_rows(dc, after, CONV_WIDTH - 1 - kk, False)
        dx_ref[...] = dx.astype(BF)
        for kk in range(CONV_WIDTH):
            dw_ref[kk:kk + 1, :] += jnp.sum(dc * taps[kk], axis=0, keepdims=True)
        gate = gt_ref[...]
        dgt = dgt_ref[...]
        is_a, is_b = _gate_lanes(ts)
        dg_raw = _mask_dot(_chunk_tri(ts, True), jnp.where(is_a, dgt, 0.0))
        z = gate + dt_ref[...]
        na = -jnp.exp(a_ref[...])
        dga = dg_raw * na * _sigmoid(z)
        beta = _sigmoid(gate)
        dgb = jnp.where(is_b, dgt * beta * (1.0 - beta), 0.0)
        dgo_ref[...] = (dff_ref[...] + dga + dgb).astype(BF)
        ddt_ref[...] += jnp.sum(dga, axis=0, keepdims=True)
        da_ref[...] += jnp.sum(dg_raw * na * _softplus(z), axis=0, keepdims=True)

    rev = lambda wd, j: pl.BlockSpec((ts, wd), lambda i: (nt - 1 - i, j))
    halo_spec = pl.BlockSpec((HALO, GDN_QKV), lambda i: (jnp.maximum((nt - 1 - i) * hb - 1, 0), GDN_COL // GDN_QKV))
    return pl.pallas_call(
        body, name=name, grid=(nt,),
        in_specs=[rev(GDN_QKV, GDN_COL // GDN_QKV), halo_spec, rev(LANES, GATE_COL // LANES),
                  rev(GDN_WIDTH, 0), rev(GDN_WIDTH, 0), rev(GDN_WIDTH, 0), rev(LANES, 0), rev(LANES, 0),
                  _full((CONV_WIDTH, GDN_QKV)), _full((1, LANES)), _full((1, LANES))],
        out_specs=[rev(GDN_QKV, 0), rev(LANES, 0), _full((CONV_WIDTH, GDN_QKV)), _full((1, LANES)),
                   _full((1, LANES))],
        out_shape=[jax.ShapeDtypeStruct((t, GDN_QKV), BF), jax.ShapeDtypeStruct((t, LANES), BF),
                   jax.ShapeDtypeStruct((CONV_WIDTH, GDN_QKV), F32), jax.ShapeDtypeStruct((1, LANES), F32),
                   jax.ShapeDtypeStruct((1, LANES), F32)],
        scratch_shapes=[pltpu.VMEM((ts, GDN_QKV), F32), pltpu.VMEM((HALO, GDN_QKV), F32)],
        compiler_params=_params(("arbitrary",)),
    )(proj, proj, proj, dq, dk, dv, dgates, dff, conv_w, a_pad, dt_pad)


PAIR = 2 * CHUNK


def _split_bf16(a):
    hi = a.astype(BF)
    return hi, (a - hi.astype(F32)).astype(BF)


def _dot3(a, b, dims=(((1,), (0,)), ((), ()))):
    ah, al = _split_bf16(a)
    bh, bl = _split_bf16(b)
    (ca,), (cb,) = dims[0]
    return lax.dot_general(jnp.concatenate([ah, al, ah], axis=ca), jnp.concatenate([bh, bh, bl], axis=cb), dims,
                           preferred_element_type=F32)


def _inv_unit_lower(a):
    r = lax.broadcasted_iota(jnp.int32, (PAIR, PAIR), 0)
    c = lax.broadcasted_iota(jnp.int32, (PAIR, PAIR), 1)
    tm = (r == c).astype(F32) - a
    pw = _dot3(a, a)
    for _ in range(4):
        x = _dot3(jnp.concatenate([tm, pw], axis=0), pw)
        tm = tm + x[:PAIR]
        pw = x[PAIR:]
    return tm + _dot3(tm, pw)


def _gdn_pair_local(q, k, v, gc, gr, b):
    r = lax.broadcasted_iota(jnp.int32, (PAIR, PAIR), 0)
    c = lax.broadcasted_iota(jnp.int32, (PAIR, PAIR), 1)
    same = (r // CHUNK) == (c // CHUNK)
    incl = same & (r >= c)
    strict = same & (r > c)
    dm = jnp.exp(jnp.where(incl, gc - gr, NEG))
    e = jnp.exp(gc)
    kb = k * b
    vb = v * b
    kbe = kb * e
    kq = _dot_nt(jnp.concatenate([kb, q], axis=0).astype(BF), k.astype(BF))
    amat = jnp.where(strict, kq[:PAIR] * dm, 0.0)
    pmat = jnp.where(incl, kq[PAIR:] * dm, 0.0)
    lane = lax.broadcasted_iota(jnp.int32, (1, PAIR), 1)
    gl_a = jnp.sum(jnp.where(lane == CHUNK - 1, gr, 0.0), axis=1, keepdims=True)
    gl_b = jnp.sum(jnp.where(lane == PAIR - 1, gr, 0.0), axis=1, keepdims=True)
    ridx = lax.broadcasted_iota(jnp.int32, (PAIR, 1), 0)
    edec = jnp.exp(jnp.where(ridx < CHUNK, gl_a, gl_b) - gc)
    return dict(dm=dm, e=e, kb=kb, vb=vb, kbe=kbe, amat=amat, pmat=pmat, gl_a=gl_a, gl_b=gl_b, edec=edec,
                kd=k * edec, qd=q * e, incl=incl, strict=strict, ridx=ridx)


def _gdn_pair_states(loc, tb, s_a):
    uw = _dot(tb, jnp.concatenate([loc["vb"], loc["kbe"]], axis=1).astype(BF))
    u, w = uw[:, :LANES], uw[:, LANES:]
    qd, kd, c = loc["qd"], loc["kd"], CHUNK
    xa = _dot(jnp.concatenate([qd[:c], w[:c]], axis=0).astype(BF), s_a.astype(BF))
    vn_a = u[:c] - xa[c:]
    s_b = s_a * jnp.exp(loc["gl_a"]) + _dot_tn(kd[:c].astype(BF), vn_a.astype(BF))
    xb = _dot(jnp.concatenate([qd[c:], w[c:]], axis=0).astype(BF), s_b.astype(BF))
    vn_b = u[c:] - xb[c:]
    s_c = s_b * jnp.exp(loc["gl_b"]) + _dot_tn(kd[c:].astype(BF), vn_b.astype(BF))
    vn = jnp.concatenate([vn_a, vn_b], axis=0)
    o = jnp.concatenate([xa[:c], xb[:c]], axis=0) + _dot(loc["pmat"].astype(BF), vn.astype(BF))
    return w, vn, o, s_b, s_c


GDN_SEG = 512


def _gdn_specs(nb, seq, reverse):
    n = seq // CHUNK
    seg = _tile(seq, GDN_SEG)
    nseg = seq // seg
    sp = seg // PAIR
    at = (lambda s: nseg - 1 - s) if reverse else (lambda s: s)
    blk = pl.BlockSpec((nb, seg, GDN_WIDTH), lambda s: (0, at(s), 0))
    gg = pl.BlockSpec((nb, seg, GDN_WIDTH), lambda s: (0, at(s), GG_COL // GDN_WIDTH))
    gates = pl.BlockSpec((nb, seg, LANES), lambda s: (0, at(s), 0))
    rowb = pl.BlockSpec((nb, GDN_HEADS, sp, HALO, PAIR), lambda s: (0, 0, at(s), 0, 0))
    per_pair = pl.BlockSpec((nb, GDN_HEADS, sp, PAIR, PAIR), lambda s: (0, 0, at(s), 0, 0))
    return n, seg, nseg, sp, blk, gg, gates, rowb, per_pair


def _head_column(gt, lane, index):
    return jnp.sum(jnp.where(lane == index, gt, 0.0), axis=1, keepdims=True)


def _gdn_head_inputs(qkv_refs, gt_ref, gr_ref, rows, pi, lane, chains):
    per_chain = []
    for b, hh in chains:
        gt = gt_ref[b, rows, :]
        cols = slice(hh * LANES, (hh + 1) * LANES)
        per_chain.append([r[b, rows, cols].astype(F32) for r in qkv_refs]
                         + [_head_column(gt, lane, A_LANE + hh), gr_ref[b, hh, pi][0:1, :],
                            _head_column(gt, lane, B_LANE + hh)])
    return [jnp.stack(xs) for xs in zip(*per_chain)]


def _gdn_pair_fwd(qv, kv, vv, gcv, gr, bv, s_a):
    loc = _gdn_pair_local(qv, kv, vv, gcv, gr, bv)
    tf = _inv_unit_lower(loc["amat"])
    _, _, o, _, s_c = _gdn_pair_states(loc, tf.astype(BF), s_a)
    return tf, o, s_c


def _gdn_fwd(q, k, v, proj, gates, grow, wn, nb, seq, name):
    n, seg, nseg, sp, blk, gg, gates_spec, rowb, per_pair = _gdn_specs(nb, seq, False)
    chains = [(b, hh) for b in range(nb) for hh in range(GDN_HEADS)]

    def body(q_ref, k_ref, v_ref, gg_ref, gt_ref, gr_ref, wn_ref, y_ref, tn_ref, sn_ref, s_ref):
        @pl.when(pl.program_id(0) == 0)
        def _():
            s_ref[...] = jnp.zeros_like(s_ref)

        wnv = wn_ref[...]
        lane = lax.broadcasted_iota(jnp.int32, (PAIR, LANES), 1)

        def step(pi, carry):
            rows = pl.ds(pl.multiple_of(pi * PAIR, PAIR), PAIR)
            ins = _gdn_head_inputs((q_ref, k_ref, v_ref), gt_ref, gr_ref, rows, pi, lane, chains)
            s_a = s_ref[...]
            tf, o, s_c = jax.vmap(_gdn_pair_fwd)(*ins, s_a)
            s_ref[...] = s_c
            for c, (b, hh) in enumerate(chains):
                cols = slice(hh * LANES, (hh + 1) * LANES)
                tn_ref[b, hh, pi] = tf[c]
                sn_ref[b, hh, pi] = s_a[c]
                g = gg_ref[b, rows, cols]
                oh = o[c]
                rstd = lax.rsqrt(jnp.mean(oh * oh, axis=-1, keepdims=True) + EPS)
                y_ref[b, rows, cols] = (oh * rstd * wnv * (g * _sigmoid(g))).astype(BF)
            return carry

        lax.fori_loop(0, sp, step, 0)

    saved = jax.ShapeDtypeStruct((nb, GDN_HEADS, n // 2, PAIR, PAIR), F32)
    return pl.pallas_call(
        body, name=name, grid=(nseg,),
        in_specs=[blk, blk, blk, gg, gates_spec, rowb, _full((1, LANES))],
        out_specs=[blk, per_pair, per_pair],
        out_shape=[jax.ShapeDtypeStruct((nb, seq, GDN_WIDTH), BF), saved, saved],
        scratch_shapes=[pltpu.VMEM((len(chains), GDN_HEAD_DIM, GDN_HEAD_DIM), F32)],
        compiler_params=_params(("arbitrary",)),
    )(q, k, v, proj, gates, grow, wn)


def _gdn_pair_bwd(qv, kv, vv, gcv, gr, bv, tf, s_a, dsp, g, dyv, wnv):
    c = CHUNK
    loc = _gdn_pair_local(qv, kv, vv, gcv, gr, bv)
    tm = tf.astype(BF)
    kb, vb, kbe, e, dm = loc["kb"], loc["vb"], loc["kbe"], loc["e"], loc["dm"]
    kd, qd, pmat, amat = loc["kd"], loc["qd"], loc["pmat"], loc["amat"]
    w, vn, o, s_b, _ = _gdn_pair_states(loc, tm, s_a)
    sg = _sigmoid(g)
    silu = g * sg
    rstd = lax.rsqrt(jnp.mean(o * o, axis=-1, keepdims=True) + EPS)
    xhat = o * rstd
    dwn = jnp.sum(dyv * xhat * silu, axis=0, keepdims=True)
    dgg = dyv * xhat * wnv * (sg * (1.0 + g * (1.0 - sg)))
    dxhat = dyv * wnv * silu
    do = rstd * (dxhat - xhat * jnp.mean(dxhat * xhat, axis=-1, keepdims=True))
    dob = do.astype(BF)
    tot = lambda x: jnp.sum(jnp.sum(x, axis=1, keepdims=True), axis=0, keepdims=True)
    rsum = lambda x: jnp.sum(x, axis=1, keepdims=True)
    cat = lambda xs, ax=0: jnp.concatenate(xs, axis=ax)
    wb = w.astype(BF)
    qdb = qd.astype(BF)
    kdb = kd.astype(BF)
    vnb = vn.astype(BF)
    egl_a = jnp.exp(loc["gl_a"])
    egl_b = jnp.exp(loc["gl_b"])
    ptdo = _dot_tn(pmat.astype(BF), dob)
    dspb = dsp.astype(BF)
    dvn_b = ptdo[c:] + _dot(kdb[c:], dspb)
    dkd_b = _dot_nt(vnb[c:], dspb)
    dgl_b = egl_b * tot(s_b * dsp) + tot(dkd_b * kd[c:])
    dsm = egl_b * dsp + _dot_tn(cat([qdb[c:], -wb[c:]]), cat([dob[c:], dvn_b.astype(BF)]))
    dsmb = dsm.astype(BF)
    dvn_a = ptdo[:c] + _dot(kdb[:c], dsmb)
    dkd_a = _dot_nt(vnb[:c], dsmb)
    dgl_a = egl_a * tot(s_a * dsm) + tot(dkd_a * kd[:c])
    ds_new = egl_a * dsm + _dot_tn(cat([qdb[:c], -wb[:c]]), cat([dob[:c], dvn_a.astype(BF)]))
    ya = _dot_nt(cat([dob[:c], dvn_a.astype(BF)]), s_a.astype(BF))
    yb = _dot_nt(cat([dob[c:], dvn_b.astype(BF)]), s_b.astype(BF))
    dqd = cat([ya[:c], yb[:c]])
    dw = -cat([ya[c:], yb[c:]])
    dvn = cat([dvn_a, dvn_b])
    dkd = cat([dkd_a, dkd_b])
    dq = dqd * e
    dgc = rsum(dqd * qd) - rsum(dkd * kd)
    dk = dkd * loc["edec"]
    dpm = jnp.where(loc["incl"], _dot_nt(dob, vnb), 0.0)
    duw = cat([dvn, dw], 1).astype(BF)
    dt = _dot_nt(duw, cat([vb, kbe], 1).astype(BF))
    tt = _dot_tn(tm, duw)
    dvb, dkbe = tt[:, :LANES], tt[:, LANES:]
    tn_dims = (((0,), (0,)), ((), ()))
    nt_dims = (((1,), (1,)), ((), ()))
    da = jnp.where(loc["strict"], -_dot3(_dot3(tf, dt, tn_dims), tf, nt_dims), 0.0)
    st = cat([da * dm, dpm * dm]).astype(BF)
    z = _dot(st, kv.astype(BF))
    dkb = z[:PAIR] + dkbe * e
    dq = dq + z[PAIR:]
    dk = dk + _dot_tn(st, cat([kb, qv]).astype(BF))
    gmat = dpm * pmat + da * amat
    dgc = dgc + rsum(dkbe * kbe) + rsum(gmat)
    ridx = loc["ridx"]
    dgc = dgc + jnp.where(ridx == c - 1, dgl_a, 0.0) + jnp.where(ridx == PAIR - 1, dgl_b, 0.0)
    dgc_row = jnp.sum(gmat, axis=0, keepdims=True)
    db = rsum(dvb * vv) + rsum(dkb * kv)
    return dq, dk + dkb * bv, dvb * bv, dgg, dgc, dgc_row, db, dwn, ds_new


def _gdn_bwd(q, k, v, proj, gates, grow, wn, tinv_all, states_all, dy, nb, seq, name):
    n, seg, nseg, sp, blk, gg, gates_spec, rowb, per_pair = _gdn_specs(nb, seq, True)
    dh = GDN_HEAD_DIM
    chains = [(b, hh) for b in range(nb) for hh in range(GDN_HEADS)]

    def body(q_ref, k_ref, v_ref, gg_ref, gt_ref, gr_ref, wn_ref, tn_ref, sn_ref, dy_ref,
             dq_ref, dk_ref, dv_ref, dgg_ref, dgt_ref, dwn_ref, ds_ref):
        @pl.when(pl.program_id(0) == 0)
        def _():
            dwn_ref[...] = jnp.zeros_like(dwn_ref)
            ds_ref[...] = jnp.zeros_like(ds_ref)

        wnv = wn_ref[...]
        lane = lax.broadcasted_iota(jnp.int32, (PAIR, LANES), 1)

        def bwd_step(j, carry):
            pi = sp - 1 - j
            rows = pl.ds(pl.multiple_of(pi * PAIR, PAIR), PAIR)
            ins = _gdn_head_inputs((q_ref, k_ref, v_ref), gt_ref, gr_ref, rows, pi, lane, chains)
            lanes_of = lambda hh: slice(hh * LANES, (hh + 1) * LANES)
            saved = [jnp.stack([r[b, hh, pi] for b, hh in chains]) for r in (tn_ref, sn_ref)]
            g2 = jnp.stack([gg_ref[b, rows, lanes_of(hh)] for b, hh in chains])
            dy2 = jnp.stack([dy_ref[b, rows, lanes_of(hh)].astype(F32) for b, hh in chains])
            dq, dk, dv, dgg, dgc, dgc_row, db, dwn, ds_new = jax.vmap(
                _gdn_pair_bwd, in_axes=(0,) * 11 + (None,))(*ins, *saved, ds_ref[...], g2, dy2, wnv)
            ds_ref[...] = ds_new
            dgt = [jnp.zeros((PAIR, LANES), F32) for _ in range(nb)]
            for c, (b, hh) in enumerate(chains):
                cols = lanes_of(hh)
                dq_ref[b, rows, cols] = dq[c]
                dk_ref[b, rows, cols] = dk[c]
                dv_ref[b, rows, cols] = dv[c]
                dgg_ref[b, rows, cols] = dgg[c].astype(BF)
                dwn_ref[...] += dwn[c]
                row_as_col = jnp.transpose(jnp.broadcast_to(dgc_row[c], (PAIR, LANES)))
                dgt[b] = (dgt[b] + jnp.where(lane == A_LANE + hh, dgc[c] - row_as_col, 0.0)
                          + jnp.where(lane == B_LANE + hh, db[c], 0.0))
            for b in range(nb):
                dgt_ref[b, rows, :] = dgt[b]
            return carry

        lax.fori_loop(0, sp, bwd_step, 0)

    f32_out = jax.ShapeDtypeStruct((nb, seq, GDN_WIDTH), F32)
    return pl.pallas_call(
        body, name=name, grid=(nseg,),
        in_specs=[blk, blk, blk, gg, gates_spec, rowb, _full((1, LANES)), per_pair, per_pair, blk],
        out_specs=[blk, blk, blk, blk, gates_spec, _full((1, LANES))],
        out_shape=[f32_out, f32_out, f32_out, jax.ShapeDtypeStruct((nb, seq, GDN_WIDTH), BF),
                   jax.ShapeDtypeStruct((nb, seq, LANES), F32), jax.ShapeDtypeStruct((1, LANES), F32)],
        scratch_shapes=[pltpu.VMEM((len(chains), dh, dh), F32)],
        compiler_params=_params(("arbitrary",)),
    )(q, k, v, proj, gates, grow, wn, tinv_all, states_all, dy)


def _mix_to_padded(w):
    pad = jnp.zeros(w.shape[:-1] + (N_PAD - N_IN,), w.dtype)
    return jnp.concatenate([w[..., 0:1536], w[..., 1544:3080], w[..., 3088:3600], w[..., 1536:1544],
                            w[..., 3080:3088], pad], axis=-1)


def _pad_lanes(vec, start):
    return jnp.pad(vec[None, :], ((0, 0), (start, LANES - start - vec.shape[0])))


def _heads_to_rows(block, lane0, nheads, nb, seq):
    return block[:, lane0:lane0 + nheads].reshape(nb, seq, nheads).transpose(0, 2, 1).reshape(nb * nheads, seq)


def _mixer_small(p, l):
    wq_t = jnp.tile(p["fox_q_norm"][l], FOX_HEADS)[None, :]
    wk_t = jnp.tile(p["fox_k_norm"][l], FOX_HEADS)[None, :]
    bias = _pad_lanes(p["fox_f_bias"][l], 0)
    a_pad = _pad_lanes(p["gdn_a_log"][l], A_LANE)
    dt_pad = _pad_lanes(p["gdn_dt_bias"][l], A_LANE)
    wn = p["gdn_out_norm"][l][None, :]
    return wq_t, wk_t, bias, a_pad, dt_pad, wn


def _layer_fwd(x, p, l, nb, seq, rider=None, ffn1_rider=None, after_ffn1=None):
    npair = FOX_HEADS // 2
    n = seq // CHUNK
    x1, h1, *rode1 = _ffn_fwd(x, p["ffn1_norm"][l][None, :], p["ffn1_w_in"][l], p["ffn1_w_out"][l],
                              f"ffn1_fwd_{l}", ffn1_rider)
    if after_ffn1 is not None:
        after_ffn1(rode1)
    wq_t, wk_t, bias, a_pad, dt_pad, wn = _mixer_small(p, l)
    proj = _norm_matmul(x1, p["mix_norm"][l][None, :], p["w_mix"][l], f"mix_in_{l}")
    fq, fk, fv, cum = _fox_prep(proj, wq_t, wk_t, bias, seq, f"fox_prep_{l}")
    c8 = _heads_to_rows(cum, 0, FOX_HEADS, nb, seq)
    ck = c8.reshape(nb * npair, 2, 1, seq)
    o, lse, *rode = _fox_attn(fq, fk, fv, ck, nb, seq, f"fox_attn_{l}", rider)
    gq, gk, gv, gates = _gdn_prep(proj, p["gdn_conv"][l], a_pad, dt_pad, seq, f"gdn_prep_{l}")
    gc4 = _heads_to_rows(gates, A_LANE, GDN_HEADS, nb, seq)
    grow = jnp.broadcast_to(gc4.reshape(nb, GDN_HEADS, n // 2, 1, PAIR), (nb, GDN_HEADS, n // 2, HALO, PAIR))
    per_example = lambda a: a.reshape(nb, seq, a.shape[-1])
    gq, gk, gv, gates = per_example(gq), per_example(gk), per_example(gv), per_example(gates)
    y, tinv, states = _gdn_fwd(gq, gk, gv, per_example(proj), gates, grow, wn, nb, seq, f"gdn_fwd_{l}")
    y = y.reshape(nb * seq, GDN_WIDTH)
    x2 = _mix_out(x1, o, y, p["w_out"][l], f"mix_out_{l}")
    x3, h2 = _ffn_fwd(x2, p["ffn2_norm"][l][None, :], p["ffn2_w_in"][l], p["ffn2_w_out"][l], f"ffn2_fwd_{l}")
    saved = dict(x=x, h1=h1, x1=x1, proj=proj, fq=fq, fk=fk, fv=fv, ck=ck, o=o, lse=lse,
                 gq=gq, gk=gk, gv=gv, gates=gates, grow=grow, tinv=tinv, states=states, y=y, x2=x2, h2=h2)
    return x3, saved, rode


def _ffn_grads(dy, x, h, gain, win, wout, l, tag, rider=None):
    t, d = x.shape
    fs = win.shape[2]
    dx, dh, a, hn, dyh, dgain, *rode = _ffn_bwd(dy, x, h, gain, win, wout, f"{tag}_bwd_{l}", rider)
    g_in = _wgrad(hn, dh, jax.ShapeDtypeStruct((4, d, fs), BF),
                  pl.BlockSpec((None, d, fs), lambda i, j, k: (j, i, 0)), d, fs, f"{tag}_gw_in_{l}")
    g_out = _wgrad(a, dyh, jax.ShapeDtypeStruct((2 * fs, d), BF),
                   pl.BlockSpec((fs, d), lambda i, j, k: (i, j)), fs, d, f"{tag}_gw_out_{l}")
    return dx, dgain[0], g_in, g_out.reshape(4, fs // 2, d), rode


def _layer_bwd(dx3, p, l, sv, nb, seq, rider=None, before_ffn1=None, ffn2_rider=None, after_ffn2=None):
    npair = FOX_HEADS // 2
    d = dx3.shape[1]
    wq_t, wk_t, bias, a_pad, dt_pad, wn = _mixer_small(p, l)
    g = {}
    dx2, g["ffn2_norm"], g["ffn2_w_in"], g["ffn2_w_out"], rode2 = _ffn_grads(
        dx3, sv["x2"], sv["h2"], p["ffn2_norm"][l][None, :], p["ffn2_w_in"][l], p["ffn2_w_out"][l], l, "ffn2",
        ffn2_rider)
    if after_ffn2 is not None:
        rider = after_ffn2(rode2)
    dyf, dyg, dxb = _mix_out_bwd(dx2, p["w_out"][l], f"mix_out_bwd_{l}")
    half = lambda a, nm: _wgrad(a, dxb, jax.ShapeDtypeStruct((FOX_WIDTH, d), BF),
                                pl.BlockSpec((FOX_WIDTH, d), lambda i, j, k: (i, j)), FOX_WIDTH, d, nm)
    g["w_out"] = jnp.concatenate([half(sv["o"], f"gw_out_fox_{l}"), half(sv["y"], f"gw_out_gdn_{l}")], axis=0)
    dqa, dqb, dk, dv, dkx, *rode = _fox_attn_bwd(sv["fq"], sv["fk"], sv["fv"], sv["o"], dyf, sv["lse"], sv["ck"],
                                                 nb, seq, f"fox_attn_bwd_{l}", rider)

    dpf, dff, dwq, dwk, dbias = _fox_prep_bwd(sv["proj"], dqa, dqb, dk, dv, dkx, wq_t, wk_t, bias, seq,
                                              f"fox_prep_bwd_{l}")
    g["fox_q_norm"] = dwq[0, :FOX_HEAD_DIM]
    g["fox_k_norm"] = dwk[0, :FOX_HEAD_DIM]
    g["fox_f_bias"] = dbias[0, :FOX_HEADS]
    per_example = lambda a: a.reshape(nb, seq, a.shape[-1])
    flat = lambda a: a.reshape(nb * seq, a.shape[-1])
    dgq, dgk, dgv, dgg, dgates, dwn = _gdn_bwd(
        sv["gq"], sv["gk"], sv["gv"], per_example(sv["proj"]), sv["gates"], sv["grow"], wn, sv["tinv"],
        sv["states"], per_example(dyg), nb, seq, f"gdn_bwd_{l}")
    dgq, dgk, dgv, dgg, dgates = flat(dgq), flat(dgk), flat(dgv), flat(dgg), flat(dgates)
    dpg, dgate_blk, dconv, da, ddt = _gdn_prep_bwd(sv["proj"], dgq, dgk, dgv, dgates, dff, p["gdn_conv"][l],
                                                   a_pad, dt_pad, seq, f"gdn_prep_bwd_{l}")
    g["gdn_conv"] = dconv
    g["gdn_a_log"] = da[0, A_LANE:B_LANE]
    g["gdn_dt_bias"] = ddt[0, A_LANE:B_LANE]
    g["gdn_out_norm"] = dwn[0]
    dparts = [dpf, dpg, dgg, dgate_blk]
    dx1, hnm, dgm = _norm_matmul_bwd(dx2, dparts, sv["x1"], p["mix_norm"][l][None, :], p["w_mix"][l],
                                     f"mix_in_bwd_{l}")
    g["mix_norm"] = dgm[0]
    gp = _wgrad_parts(hnm, dparts, d // 2, f"gw_mix_{l}")
    gate = GATE_COL
    g["w_in"] = jnp.concatenate([gp[:, :GDN_COL], gp[:, gate:gate + FOX_HEADS], gp[:, GDN_COL:GG_COL],
                                 gp[:, gate + A_LANE:gate + B_LANE + GDN_HEADS], gp[:, GG_COL:gate]], axis=1)
    ffn1_rider = before_ffn1(g) if before_ffn1 is not None else None
    dx0, g["ffn1_norm"], g["ffn1_w_in"], g["ffn1_w_out"], rode1 = _ffn_grads(
        dx1, sv["x"], sv["h1"], p["ffn1_norm"][l][None, :], p["ffn1_w_in"][l], p["ffn1_w_out"][l], l, "ffn1",
        ffn1_rider)
    return dx0, g, rode, rode1


N_CHIPS = 4


def _mesh_pos():
    return lax.axis_index("x"), lax.axis_index("y"), lax.axis_index("c")


def _other_chips(x, y):
    return [(1 - x, y), (x, 1 - y), (1 - x, 1 - y)]


def _remote(src, dst, send_sem, recv_sem, to):
    return pltpu.make_async_remote_copy(src_ref=src, dst_ref=dst, send_sem=send_sem, recv_sem=recv_sem,
                                        device_id=to, device_id_type=MESH)


def _hbm_call(body, name, ins, out_shape, scratch):
    return pl.pallas_call(
        body, name=name, out_shape=out_shape, in_specs=[HBM] * len(ins),
        out_specs=jax.tree.map(lambda _: HBM, out_shape), scratch_shapes=scratch,
        compiler_params=pltpu.CompilerParams(has_side_effects=True),
    )(*ins)


def _gather_phases(n, layer):
    def copies(ins, outs, sems):
        send1, recv1, send2, recv2 = sems
        x, y, c = _mesh_pos()
        out, back, fwd = [], [], []
        for i in range(n):
            for j, (px, py) in enumerate(_other_chips(x, y)):
                k = 3 * i + j
                blk = outs[i].at[2 * px + py]
                out.append(_remote(ins[i], outs[i].at[2 * x + y], send1.at[k], recv1.at[k], (px, py, c)))
                back.append(_remote(blk, blk, send1.at[k], recv1.at[k], (px, py, c)))
                fwd.append(_remote(blk, blk, send2.at[k], recv2.at[k], (x, y, 1 - c)))
        return c, out, back, fwd

    def first(ins, outs, sems):
        c, out, _, _ = copies(ins, outs, sems)

        @pl.when(c == layer)
        def _():
            for cp in out:
                cp.start()

    def middle(ins, outs, sems):
        c, _, back, fwd = copies(ins, outs, sems)

        @pl.when(c == layer)
        def _():
            for arrived, onward in zip(back, fwd):
                arrived.wait_recv()
                onward.start()

    def last(ins, outs, sems):
        c, out, _, fwd = copies(ins, outs, sems)

        @pl.when(c == layer)
        def _():
            for cp in out + fwd:
                cp.wait_send()

        @pl.when(c != layer)
        def _():
            for cp in fwd:
                cp.wait_recv()

    return first, middle, last


def _scatter_phases(n, layer):
    def copies(ins, outs, sems):
        send, recv = sems
        x, y, c = _mesh_pos()
        return c, [_remote(ins[i].at[2 * px + py], outs[i].at[j], send.at[3 * i + j], recv.at[3 * i + j], (px, py, c))
                   for i in range(n) for j, (px, py) in enumerate(_other_chips(x, y))]

    def first(ins, outs, sems):
        c, cps = copies(ins, outs, sems)

        @pl.when(c == layer)
        def _():
            for cp in cps:
                cp.start()

    def middle(ins, outs, sems):
        pass

    def last(ins, outs, sems):
        c, cps = copies(ins, outs, sems)

        @pl.when(c == layer)
        def _():
            for cp in cps:
                cp.wait()

    return first, middle, last


def _exchange(blocks, out_shapes, n_sems, phases, name, rider):
    sems = [pltpu.SemaphoreType.DMA((3 * len(blocks),))] * n_sems
    if rider:
        return _Rider(blocks, out_shapes, sems, phases)
    n = len(blocks)

    def body(*refs):
        for phase in phases:
            phase(refs[:n], refs[n:2 * n], refs[2 * n:])

    return list(_hbm_call(body, name, blocks, out_shapes, sems))


def _gather_layer(blocks, layer, name=None, rider=False):
    outs = [jax.ShapeDtypeStruct((N_CHIPS,) + b.shape, b.dtype) for b in blocks]
    return _exchange(blocks, outs, 4, _gather_phases(len(blocks), layer), name, rider)


def _scatter_layer(sums, layer, name=None, rider=False):
    outs = [jax.ShapeDtypeStruct((3,) + s.shape[1:], s.dtype) for s in sums]
    return _exchange(sums, outs, 2, _scatter_phases(len(sums), layer), name, rider)


def _to_sibling(gs, layer, name=None, rider=False):
    n = len(gs)

    def copies(ins, outs, sems):
        send, recv = sems
        x, y, c = _mesh_pos()
        return c, [_remote(ins[i], outs[i], send.at[i], recv.at[i], (x, y, 1 - c)) for i in range(n)]

    def first(ins, outs, sems):
        c, cps = copies(ins, outs, sems)

        @pl.when(c != layer)
        def _():
            for cp in cps:
                cp.start()

    def middle(ins, outs, sems):
        pass

    def last(ins, outs, sems):
        c, cps = copies(ins, outs, sems)

        @pl.when(c != layer)
        def _():
            for cp in cps:
                cp.wait_send()

        @pl.when(c == layer)
        def _():
            for cp in cps:
                cp.wait_recv()

    sems = [pltpu.SemaphoreType.DMA((n,))] * 2
    outs = [jax.ShapeDtypeStruct(g.shape, g.dtype) for g in gs]
    if rider:
        return _Rider(gs, outs, sems, (first, middle, last))

    def body(*refs):
        for phase in (first, middle, last):
            phase(refs[:n], refs[n:2 * n], refs[2 * n:])

    return list(_hbm_call(body, name, gs, outs, sems))


def _sibling_swap(rs, name):
    n = len(rs)

    def body(*refs):
        ins, outs = refs[:n], refs[n:2 * n]
        send, recv = refs[2 * n:]
        x, y, c = _mesh_pos()
        cps = [_remote(ins[i], outs[i], send.at[i], recv.at[i], (x, y, 1 - c)) for i in range(n)]
        for cp in cps:
            cp.start()
        for cp in cps:
            cp.wait()

    sem = pltpu.SemaphoreType.DMA((n,))
    return _hbm_call(body, name, rs, [jax.ShapeDtypeStruct(r.shape, r.dtype) for r in rs], [sem, sem])


def _small_all_reduce(vec, name):
    r = vec.shape[0]
    ndev = 8

    def body(v_ref, o_ref, buf, send, recv):
        x, y, c = _mesh_pos()
        me = 4 * x + 2 * y + c
        buf[me] = v_ref[...]
        cps = []
        for rel in range(1, ndev):
            px = 1 - x if rel & 4 else x
            py = 1 - y if rel & 2 else y
            pc = 1 - c if rel & 1 else c
            cps.append((_remote(v_ref, buf.at[me], send.at[rel - 1], recv.at[rel - 1], (px, py, pc)),
                        4 * px + 2 * py + pc))
        for cp, _ in cps:
            cp.start()
        for k, (cp, peer) in enumerate(cps):
            slot = buf.at[peer]
            _remote(slot, slot, send.at[k], recv.at[k], (x, y, c)).wait_recv()
        for cp, _ in cps:
            cp.wait_send()
        acc = buf[0]
        for k in range(1, ndev):
            acc = acc + buf[k]
        o_ref[...] = acc

    vm = pl.BlockSpec(memory_space=pltpu.VMEM)
    return pl.pallas_call(
        body, name=name, out_shape=jax.ShapeDtypeStruct(vec.shape, F32), in_specs=[vm], out_specs=vm,
        scratch_shapes=[pltpu.VMEM((ndev, r, LANES), F32), pltpu.SemaphoreType.DMA((ndev - 1,)),
                        pltpu.SemaphoreType.DMA((ndev - 1,))],
        compiler_params=pltpu.CompilerParams(has_side_effects=True),
    )(vec)


def _row_tile(rows, cap=SUM_ROWS):
    for t in range(min(rows, cap), 0, -1):
        if rows % t == 0 and (t % 16 == 0 or t == rows):
            return t
    raise ValueError(rows)


def _add_pairs(a, b, name):
    k, r, c = a.shape
    tr = _row_tile(r)

    def body(a_ref, b_ref, o_ref):
        o_ref[...] = (a_ref[...].astype(F32) + b_ref[...].astype(F32)).astype(o_ref.dtype)

    spec = pl.BlockSpec((None, tr, c), lambda i, j: (i, j, 0))
    return pl.pallas_call(body, name=name, grid=(k, r // tr), in_specs=[spec, spec], out_specs=spec,
                          out_shape=jax.ShapeDtypeStruct(a.shape, a.dtype),
                          compiler_params=_params(("parallel", "parallel")))(a, b)


def _final_sum(own, sib, others, name):
    r, c = own.shape
    tr = _row_tile(r)

    def body(a_ref, b_ref, o_ref_in, out_ref):
        acc = a_ref[...].astype(F32) + b_ref[...].astype(F32)
        for k in range(3):
            acc = acc + o_ref_in[k].astype(F32)
        out_ref[...] = acc

    spec = pl.BlockSpec((tr, c), lambda i: (i, 0))
    return pl.pallas_call(body, name=name, grid=(r // tr,),
                          in_specs=[spec, spec, pl.BlockSpec((3, tr, c), lambda i: (0, i, 0))], out_specs=spec,
                          out_shape=jax.ShapeDtypeStruct((r, c), F32),
                          compiler_params=_params(("parallel",)))(own, sib, others)


def _adamw(g, w, m, v, name):
    r, c = g.shape
    tr = _row_tile(r, ADAM_ROWS)

    def body(g_ref, w_ref, m_ref, v_ref, d_ref, mo_ref, vo_ref):
        gv = g_ref[...]
        mn = ADAM_B1 * m_ref[...] + (1.0 - ADAM_B1) * gv
        vn = ADAM_B2 * v_ref[...] + (1.0 - ADAM_B2) * (gv * gv)
        m_hat = mn / (1.0 - ADAM_B1 ** ADAM_STEP)
        v_hat = vn / (1.0 - ADAM_B2 ** ADAM_STEP)
        d_ref[...] = -ADAM_LR * (m_hat / (jnp.sqrt(v_hat) + ADAM_EPS) + ADAM_WD * w_ref[...])
        mo_ref[...] = mn
        vo_ref[...] = vn

    spec = pl.BlockSpec((tr, c), lambda i: (i, 0))
    shp = jax.ShapeDtypeStruct((r, c), F32)
    return pl.pallas_call(body, name=name, grid=(r // tr,), in_specs=[spec] * 4, out_specs=[spec] * 3,
                          out_shape=[shp] * 3, compiler_params=_params(("parallel",)))(g, w, m, v)


def _pack(arrays):
    flat = jnp.concatenate([a.reshape(-1).astype(F32) for a in arrays])
    pad = (-flat.shape[0]) % (8 * LANES)
    return jnp.concatenate([flat, jnp.zeros((pad,), F32)]).reshape(-1, LANES)


def _unpack(packed, shapes):
    flat = packed.reshape(-1)
    out, off = [], 0
    for s in shapes:
        size = 1
        for dim in s:
            size *= dim
        out.append(flat[off:off + size].reshape(s))
        off += size
    return out


BIG = ("ffn1_w_in", "ffn1_w_out", "w_in", "w_out", "ffn2_w_in", "ffn2_w_out")
SMALL = ("ffn1_norm", "mix_norm", "fox_q_norm", "fox_k_norm", "fox_f_bias", "gdn_a_log", "gdn_dt_bias",
         "gdn_out_norm", "ffn2_norm", "gdn_conv")
WEIGHTS = ("ffn1_norm", "ffn1_w_in", "ffn1_w_out", "mix_norm", "w_in", "fox_q_norm", "fox_k_norm", "fox_f_bias",
           "gdn_conv", "gdn_a_log", "gdn_dt_bias", "gdn_out_norm", "w_out", "ffn2_norm", "ffn2_w_in", "ffn2_w_out")


def _step(x, target, w, m, v):
    xi, yi, ci = _mesh_pos()
    me = 2 * xi + yi
    depth = DEPTH
    d = x.shape[-1]

    nb, seq, _ = x.shape
    assert depth == 2

    p ={k: w[k] for k in SMALL if k != "gdn_conv"}
    for k in ("ffn1_w_in", "ffn1_w_out", "ffn2_w_in", "ffn2_w_out", "w_mix", "w_out", "gdn_conv"):
        p[k] = [None] * depth

    first, rest = BIG[:2], BIG[2:] + ("gdn_conv",)

    def shards(l, names):
        return [w[k][l] if k == "gdn_conv" else w[k][l].astype(BF) for k in names]

    def place(l, names, gathered):
        blocks = dict(zip(names, [lax.dynamic_update_index_in_dim(g, s, me, 0)
                                  for g, s in zip(gathered, shards(l, names))]))
        for k in ("ffn1_w_in", "ffn1_w_out", "ffn2_w_in", "ffn2_w_out"):
            if k in blocks:
                p[k][l] = blocks[k]
        if "w_in" in blocks:
            p["w_mix"][l] = _mix_to_padded(blocks["w_in"].transpose(1, 0, 2).reshape(d, N_IN))
            p["w_out"][l] = blocks["w_out"].reshape(2 * FOX_WIDTH, d)
            p["gdn_conv"][l] = blocks["gdn_conv"].transpose(1, 0, 2).reshape(CONV_WIDTH, -1)

    place(0, first, _gather_layer(shards(0, first), 0, "gather_first_ffn0"))
    xt = x.reshape(nb * seq, d)
    xt, saved0, gathered1 = _layer_fwd(
        xt, p, 0, nb, seq, _gather_layer(shards(1, first + rest), 1, rider=True),
        _gather_layer(shards(0, rest), 0, rider=True), lambda got: place(0, rest, got))
    place(1, first + rest, gathered1)
    xt, saved1, _ = _layer_fwd(xt, p, 1, nb, seq)
    loss, dx = _loss_grad(xt, target.reshape(nb * seq, d), "loss")

    def transport(g, names):
        out = []
        for k in names:
            if k == "w_in":
                out.append(g["w_in"].reshape(d, N_CHIPS, N_IN // N_CHIPS).transpose(1, 0, 2).astype(BF))
            elif k == "w_out":
                out.append(g["w_out"].reshape(N_CHIPS, -1, d))
            else:
                out.append(g[k])
        return out

    def chip_sums(g, l, names, tag):
        own = transport(g, names)
        sib = _to_sibling(own, l, f"grad{l}{tag}_to_sibling")
        return own, sib, [_add_pairs(a, b, f"grad{l}{tag}_chip_sum_{k}") for a, b, k in zip(own, sib, names)]

    dx, grads1, _, _ = _layer_bwd(dx, p, 1, saved1, nb, seq)
    own1 = transport(grads1, BIG)
    before = {}

    def after_ffn2(from_sibling):
        before["sib1"] = from_sibling
        sums1 = [_add_pairs(a, b, f"grad1_chip_sum_{k}") for a, b, k in zip(own1, from_sibling, BIG)]
        return _scatter_layer(sums1, 1, rider=True)

    def before_ffn1(g):
        before["own"], before["sib"], sums = chip_sums(g, 0, BIG[2:], "_rest")
        return _scatter_layer(sums, 0, rider=True)

    dx, grads0, chips1, chips0_rest = _layer_bwd(dx, p, 0, saved0, nb, seq, None, before_ffn1,
                                                 _to_sibling(own1, 1, rider=True), after_ffn2)
    sib1 = before["sib1"]
    own0, sib0, sums0 = chip_sums(grads0, 0, first, "_first")
    chips0 = _scatter_layer(sums0, 0, "grad0_first_to_chips") + chips0_rest
    own0, sib0 = own0 + before["own"], sib0 + before["sib"]
    grads = [grads0, grads1]
    dx = dx.reshape(nb, seq, d)

    mine = lambda a0, a1: jnp.where(ci == 0, a0, a1)
    at_me = lambda a: lax.dynamic_index_in_dim(a, me, 0, keepdims=False)
    reduced = [_final_sum(mine(at_me(own0[i]), at_me(own1[i])), mine(at_me(sib0[i]), at_me(sib1[i])),
                          mine(chips0[i], chips1[i]), f"grad_final_sum_{k}") for i, k in enumerate(BIG)]
    from_sib_final = _sibling_swap(reduced, "grad_swap_layers")
    full = {k: jnp.stack([jnp.where(ci == 0, a, b), jnp.where(ci == 0, b, a)])
            for k, a, b in zip(BIG, reduced, from_sib_final)}

    out_g, out_d, out_m, out_v = {}, {}, {}, {}
    for k in BIG:
        shp = w[k].shape
        two_d = lambda a: a.reshape(shp[0] * shp[1], shp[2])
        dl, mn, vn = _adamw(two_d(full[k]), two_d(w[k]), two_d(m[k]), two_d(v[k]), f"adamw_{k}")
        out_g[k], out_d[k], out_m[k], out_v[k] = full[k], dl.reshape(shp), mn.reshape(shp), vn.reshape(shp)

    small_local = [jnp.stack([grads[l][k] for l in range(depth)]) for k in SMALL] + [loss.reshape(1)]
    summed = _unpack(_small_all_reduce(_pack(small_local), "small_all_reduce"), [a.shape for a in small_local])
    total = summed.pop()[0]
    sg = dict(zip(SMALL, summed))
    cs = w["gdn_conv"].shape[-1]
    sg["gdn_conv"] = lax.dynamic_slice_in_dim(sg["gdn_conv"], me * cs, cs, axis=2)
    shapes = [w[k].shape for k in SMALL]
    packs = [_pack([src[k] for k in SMALL]) for src in (sg, w, m, v)]
    dl, mn, vn = _adamw(*packs, "adamw_small")
    for k, a, b, c2 in zip(SMALL, _unpack(dl, shapes), _unpack(mn, shapes), _unpack(vn, shapes)):
        out_g[k], out_d[k], out_m[k], out_v[k] = sg[k], a, b, c2

    return (total, dx, *[out_g[k] for k in WEIGHTS], *[out_d[k] for k in WEIGHTS],
            *[out_m[k] for k in WEIGHTS], *[out_v[k] for k in WEIGHTS])


def kernel(x, ffn1_norm, ffn1_w_in, ffn1_w_out, mix_norm, w_in, fox_q_norm, fox_k_norm, fox_f_bias, gdn_conv, gdn_a_log, gdn_dt_bias, gdn_out_norm, w_out, ffn2_norm, ffn2_w_in, ffn2_w_out, loss_target, m_ffn1_norm, m_ffn1_w_in, m_ffn1_w_out, m_mix_norm, m_w_in, m_fox_q_norm, m_fox_k_norm, m_fox_f_bias, m_gdn_conv, m_gdn_a_log, m_gdn_dt_bias, m_gdn_out_norm, m_w_out, m_ffn2_norm, m_ffn2_w_in, m_ffn2_w_out, v_ffn1_norm, v_ffn1_w_in, v_ffn1_w_out, v_mix_norm, v_w_in, v_fox_q_norm, v_fox_k_norm, v_fox_f_bias, v_gdn_conv, v_gdn_a_log, v_gdn_dt_bias, v_gdn_out_norm, v_w_out, v_ffn2_norm, v_ffn2_w_in, v_ffn2_w_out):
    w = dict(ffn1_norm=ffn1_norm, ffn1_w_in=ffn1_w_in, ffn1_w_out=ffn1_w_out, mix_norm=mix_norm, w_in=w_in,
             fox_q_norm=fox_q_norm, fox_k_norm=fox_k_norm, fox_f_bias=fox_f_bias, gdn_conv=gdn_conv,
             gdn_a_log=gdn_a_log, gdn_dt_bias=gdn_dt_bias, gdn_out_norm=gdn_out_norm, w_out=w_out,
             ffn2_norm=ffn2_norm, ffn2_w_in=ffn2_w_in, ffn2_w_out=ffn2_w_out)
    m = dict(ffn1_norm=m_ffn1_norm, ffn1_w_in=m_ffn1_w_in, ffn1_w_out=m_ffn1_w_out, mix_norm=m_mix_norm, w_in=m_w_in,
             fox_q_norm=m_fox_q_norm, fox_k_norm=m_fox_k_norm, fox_f_bias=m_fox_f_bias, gdn_conv=m_gdn_conv,
             gdn_a_log=m_gdn_a_log, gdn_dt_bias=m_gdn_dt_bias, gdn_out_norm=m_gdn_out_norm, w_out=m_w_out,
             ffn2_norm=m_ffn2_norm, ffn2_w_in=m_ffn2_w_in, ffn2_w_out=m_ffn2_w_out)
    v = dict(ffn1_norm=v_ffn1_norm, ffn1_w_in=v_ffn1_w_in, ffn1_w_out=v_ffn1_w_out, mix_norm=v_mix_norm, w_in=v_w_in,
             fox_q_norm=v_fox_q_norm, fox_k_norm=v_fox_k_norm, fox_f_bias=v_fox_f_bias, gdn_conv=v_gdn_conv,
             gdn_a_log=v_gdn_a_log, gdn_dt_bias=v_gdn_dt_bias, gdn_out_norm=v_gdn_out_norm, w_out=v_w_out,
             ffn2_norm=v_ffn2_norm, ffn2_w_in=v_ffn2_w_in, ffn2_w_out=v_ffn2_w_out)
    return _step(x, loss_target, w, m, v)
```

```python
import jax
import jax.numpy as jnp
from jax import lax
from jax.experimental import pallas as pl
from jax.experimental.pallas import tpu as pltpu

F32 = jnp.float32
BF = jnp.bfloat16
HI = lax.Precision.HIGHEST
MESH = pl.DeviceIdType.MESH

DEPTH = 2
FOX_HEADS = 8
FOX_HEAD_DIM = 64
FOX_WIDTH = 512
GDN_HEADS = 4
GDN_HEAD_DIM = 128
GDN_WIDTH = 512
CONV_WIDTH = 4
CHUNK = 64
EPS = 1e-6
N_IN = 3600
N_PAD = 3712
GATE_COL = 3584
LANES = 128
NEG = -1e30

ADAM_LR = 0.001
ADAM_B1 = 0.9
ADAM_B2 = 0.999
ADAM_EPS = 1e-08
ADAM_WD = 0.01
ADAM_STEP = 10

VMEM_LIMIT = 56 * 1024 * 1024

TOKEN_TILE = 512
TOKEN_TILE_BWD = 256
WGRAD_TOKENS = 512
FOX_PREP_TILE = 512
GDN_PREP_TILE = 256
ATTN_FWD_TILE = 2048
ATTN_BWD_TILE = 1024
DIAGONAL_STRIPS = 2
SUM_ROWS = 512
ADAM_ROWS = 256


def _params(sem=None, **kw):
    return pltpu.CompilerParams(dimension_semantics=sem, vmem_limit_bytes=VMEM_LIMIT, **kw)


def _dot(a, b, precision=None):
    return jnp.dot(a, b, preferred_element_type=F32, precision=precision)


def _dot_nt(a, b, precision=None):
    return lax.dot_general(a, b, (((1,), (1,)), ((), ())), preferred_element_type=F32, precision=precision)


def _dot_tn(a, b, precision=None):
    return lax.dot_general(a, b, (((0,), (0,)), ((), ())), preferred_element_type=F32, precision=precision)


def _sigmoid(x):
    return 0.5 * jnp.tanh(0.5 * x) + 0.5


def _softplus(x):
    return jnp.maximum(x, 0.0) + jnp.log(1.0 + jnp.exp(-jnp.abs(x)))


def _log_sigmoid(x):
    return jnp.minimum(x, 0.0) - jnp.log(1.0 + jnp.exp(-jnp.abs(x)))


def _tile(n, t):
    t = min(n, t)
    assert n % t == 0, (n, t)
    return t


def _rms_fwd(x, gain):
    rstd = lax.rsqrt(jnp.mean(x * x, axis=-1, keepdims=True) + EPS)
    xhat = x * rstd
    return xhat * gain, xhat, rstd


def _rms_bwd(dy, xhat, rstd, gain):
    dxhat = dy * gain
    dx = rstd * (dxhat - xhat * jnp.mean(dxhat * xhat, axis=-1, keepdims=True))
    return dx, dy * xhat


def _full(shape):
    nd = len(shape)
    return pl.BlockSpec(shape, lambda *_: (0,) * nd)


HBM = pl.BlockSpec(memory_space=pltpu.HBM)


def _load_ffn_weights(win_hbm, wout_hbm, win_v, wout_v, sem):
    fr = wout_hbm.shape[1]
    copies = [pltpu.make_async_copy(win_hbm.at[s], win_v.at[s], sem.at[s]) for s in range(4)]
    copies += [pltpu.make_async_copy(wout_hbm.at[s], wout_v.at[pl.ds(s * fr, fr)], sem.at[4 + s])
               for s in range(4)]
    for c in copies:
        c.start()
    for c in copies:
        c.wait()


def _ffn_fwd(x, gain, win_g, wout_g, name, rider=None):
    t, d = x.shape
    _, _, fs = win_g.shape
    fr = wout_g.shape[1]
    tm = _tile(t, TOKEN_TILE)
    r_in, r_out, r_sem = _rider_parts(rider)
    steps = t // tm

    def body(x_ref, g_ref, win_hbm, wout_hbm, *rest):
        rin, (xo_ref, h_ref) = rest[:len(r_in)], rest[len(r_in):len(r_in) + 2]
        rout = rest[len(r_in) + 2:len(r_in) + 2 + len(r_out)]
        win_v, wout_v, sem = rest[len(r_in) + 2 + len(r_out):len(r_in) + 5 + len(r_out)]
        riding = (rin, rout, rest[len(r_in) + 5 + len(r_out):])
        step = pl.program_id(0)
        _ride(rider, 0, step == 0, riding)
        _ride(rider, 1, step == (13 * steps) // 16, riding)

        @pl.when(step == 0)
        def _():
            _load_ffn_weights(win_hbm, wout_hbm, win_v, wout_v, sem)

        xv = x_ref[...]
        hn, _, _ = _rms_fwd(xv, g_ref[...])
        hn = hn.astype(BF)
        acc = jnp.zeros((tm, d), F32)
        for s in range(2):
            g = _dot(hn, win_v[s])
            u = _dot(hn, win_v[s + 2])
            h_ref[:, s * fs:(s + 1) * fs] = g.astype(BF)
            h_ref[:, (s + 2) * fs:(s + 3) * fs] = u.astype(BF)
            a = (g * _sigmoid(g) * u).astype(BF)
            acc = acc + _dot(a, wout_v[s * fs:(s + 1) * fs, :])
        xo_ref[...] = xv + 0.5 * acc
        _ride(rider, 2, step == steps - 1, riding)

    return pl.pallas_call(
        body, name=name, grid=(steps,),
        in_specs=[pl.BlockSpec((tm, d), lambda i: (i, 0)), _full((1, d)), HBM, HBM] + [HBM] * len(r_in),
        out_specs=[pl.BlockSpec((tm, d), lambda i: (i, 0)), pl.BlockSpec((tm, 4 * fs), lambda i: (i, 0))]
        + [HBM] * len(r_out),
        out_shape=[jax.ShapeDtypeStruct((t, d), F32), jax.ShapeDtypeStruct((t, 4 * fs), BF)] + r_out,
        scratch_shapes=[pltpu.VMEM((4, d, fs), BF), pltpu.VMEM((4 * fr, d), BF), pltpu.SemaphoreType.DMA((8,))]
        + r_sem,
        compiler_params=_params(("arbitrary",), has_side_effects=rider is not None),
    )(x, gain, win_g, wout_g, *r_in)


def _ffn_bwd(dy, x, h, gain, win_g, wout_g, name, rider=None):
    t, d = x.shape
    _, _, fs = win_g.shape
    fr = wout_g.shape[1]
    tm = _tile(t, TOKEN_TILE_BWD)
    r_in, r_out, r_sem = _rider_parts(rider)
    steps = t // tm

    def body(dy_ref, x_ref, h_ref, g_ref, win_hbm, wout_hbm, *rest):
        rin, (dx_ref, dh_ref, a_ref, hn_ref, dyh_ref, dg_ref) = rest[:len(r_in)], rest[len(r_in):len(r_in) + 6]
        rout = rest[len(r_in) + 6:len(r_in) + 6 + len(r_out)]
        win_v, wout_v, sem = rest[len(r_in) + 6 + len(r_out):len(r_in) + 9 + len(r_out)]
        riding = (rin, rout, rest[len(r_in) + 9 + len(r_out):])
        step = pl.program_id(0)
        _ride(rider, 0, step == 0, riding)
        _ride(rider, 1, step == (13 * steps) // 16, riding)

        @pl.when(step == 0)
        def _():
            _load_ffn_weights(win_hbm, wout_hbm, win_v, wout_v, sem)
            dg_ref[...] = jnp.zeros_like(dg_ref)

        dyv = dy_ref[...]
        dyh = (0.5 * dyv).astype(BF)
        dyh_ref[...] = dyh
        dhn = jnp.zeros((tm, d), F32)
        for s in range(2):
            da = _dot_nt(dyh, wout_v[s * fs:(s + 1) * fs, :])
            g = h_ref[:, s * fs:(s + 1) * fs].astype(F32)
            u = h_ref[:, (s + 2) * fs:(s + 3) * fs].astype(F32)
            sg = _sigmoid(g)
            si = g * sg
            a_ref[:, s * fs:(s + 1) * fs] = (si * u).astype(BF)
            dgate = (da * u * (sg * (1.0 + g * (1.0 - sg)))).astype(BF)
            dup = (da * si).astype(BF)
            dh_ref[:, s * fs:(s + 1) * fs] = dgate
            dh_ref[:, (s + 2) * fs:(s + 3) * fs] = dup
            dhn = dhn + _dot_nt(dgate, win_v[s]) + _dot_nt(dup, win_v[s + 2])
        xv = x_ref[...]
        gain_v = g_ref[...]
        hn, xhat, rstd = _rms_fwd(xv, gain_v)
        hn_ref[...] = hn.astype(BF)
        dx, dgr = _rms_bwd(dhn, xhat, rstd, gain_v)
        dx_ref[...] = dyv + dx
        dg_ref[...] += jnp.sum(dgr, axis=0, keepdims=True)
        _ride(rider, 2, step == steps - 1, riding)

    row = lambda w: pl.BlockSpec((tm, w), lambda i: (i, 0))
    return pl.pallas_call(
        body, name=name, grid=(steps,),
        in_specs=[row(d), row(d), row(4 * fs), _full((1, d)), HBM, HBM] + [HBM] * len(r_in),
        out_specs=[row(d), row(4 * fs), row(2 * fs), row(d), row(d), _full((1, d))] + [HBM] * len(r_out),
        out_shape=[jax.ShapeDtypeStruct((t, d), F32), jax.ShapeDtypeStruct((t, 4 * fs), BF),
                   jax.ShapeDtypeStruct((t, 2 * fs), BF), jax.ShapeDtypeStruct((t, d), BF),
                   jax.ShapeDtypeStruct((t, d), BF), jax.ShapeDtypeStruct((1, d), F32)] + r_out,
        scratch_shapes=[pltpu.VMEM((4, d, fs), BF), pltpu.VMEM((4 * fr, d), BF), pltpu.SemaphoreType.DMA((8,))]
        + r_sem,
        compiler_params=_params(("arbitrary",), has_side_effects=rider is not None),
    )(dy, x, h, gain, win_g, wout_g, *r_in)


def _wgrad(a, b, out_shape, out_spec, tm, tn, name, tk=WGRAD_TOKENS, rider=None):
    t, m = a.shape
    _, n = b.shape
    tk = _tile(t, tk)
    nk = t // tk
    r_in, r_out, r_sem = _rider_parts(rider)
    gm, gn = m // tm, n // tn
    steps = gm * gn * nk

    def body(a_ref, b_ref, *rest):
        rin, o_ref = rest[:len(r_in)], rest[len(r_in)]
        rout = rest[len(r_in) + 1:len(r_in) + 1 + len(r_out)]
        acc = rest[len(r_in) + 1 + len(r_out)]
        riding = (rin, rout, rest[len(r_in) + 2 + len(r_out):])
        k = pl.program_id(2)
        step = (pl.program_id(0) * gn + pl.program_id(1)) * nk + k
        _ride(rider, 0, step == 0, riding)
        _ride(rider, 1, step == (13 * steps) // 16, riding)

        @pl.when(k == 0)
        def _():
            acc[...] = jnp.zeros_like(acc)

        acc[...] += _dot_tn(a_ref[...], b_ref[...])

        @pl.when(k == nk - 1)
        def _():
            o_ref[...] = acc[...].astype(o_ref.dtype)

        _ride(rider, 2, step == steps - 1, riding)

    sem = ("arbitrary",) * 3 if rider else ("parallel", "parallel", "arbitrary")
    out = pl.pallas_call(
        body, name=name, grid=(gm, gn, nk),
        in_specs=[pl.BlockSpec((tk, tm), lambda i, j, k: (k, i)), pl.BlockSpec((tk, tn), lambda i, j, k: (k, j))]
        + [HBM] * len(r_in),
        out_specs=[out_spec] + [HBM] * len(r_out), out_shape=[out_shape] + r_out,
        scratch_shapes=[pltpu.VMEM((tm, tn), F32)] + r_sem,
        compiler_params=_params(sem, has_side_effects=rider is not None),
    )(a, b, *r_in)
    return out if rider else out[0]


def _wgrad_parts(a, parts, tm, name, tk=WGRAD_TOKENS):
    t, m = a.shape
    widths = [p.shape[1] for p in parts]
    n = sum(widths)
    tk = _tile(t, tk)
    nk = t // tk
    np_ = len(parts)

    def body(a_ref, *rest):
        b_refs, o_ref, acc = rest[:np_], rest[np_], rest[np_ + 1]
        k = pl.program_id(1)

        @pl.when(k == 0)
        def _():
            acc[...] = jnp.zeros_like(acc)

        av, off = a_ref[...], 0
        for b_ref, wd in zip(b_refs, widths):
            acc[:, off:off + wd] += _dot_tn(av, b_ref[...])
            off += wd

        @pl.when(k == nk - 1)
        def _():
            o_ref[...] = acc[...]

    return pl.pallas_call(
        body, name=name, grid=(m // tm, nk),
        in_specs=[pl.BlockSpec((tk, tm), lambda i, k: (k, i))]
        + [pl.BlockSpec((tk, wd), lambda i, k: (k, 0)) for wd in widths],
        out_specs=pl.BlockSpec((tm, n), lambda i, k: (i, 0)), out_shape=jax.ShapeDtypeStruct((m, n), F32),
        scratch_shapes=[pltpu.VMEM((tm, n), F32)],
        compiler_params=_params(("parallel", "arbitrary")),
    )(a, *parts)


def _norm_matmul(x, gain, w, name):
    t, d = x.shape
    n = w.shape[1]
    tm = _tile(t, TOKEN_TILE)

    def body(x_ref, g_ref, w_ref, o_ref):
        hn, _, _ = _rms_fwd(x_ref[...], g_ref[...])
        o_ref[...] = _dot(hn.astype(BF), w_ref[...])

    return pl.pallas_call(
        body, name=name, grid=(t // tm,),
        in_specs=[pl.BlockSpec((tm, d), lambda i: (i, 0)), _full((1, d)), _full((d, n))],
        out_specs=pl.BlockSpec((tm, n), lambda i: (i, 0)),
        out_shape=jax.ShapeDtypeStruct((t, n), F32),
        compiler_params=_params(("parallel",)),
    )(x, gain, w)


def _norm_matmul_bwd(dres, dparts, x, gain, w, name):
    t, d = x.shape
    n = w.shape[1]
    tm = _tile(t, TOKEN_TILE)
    widths = [a.shape[1] for a in dparts]
    assert sum(widths) == n
    k = len(dparts)

    def body(dr_ref, *rest):
        dp_refs, (x_ref, g_ref, w_ref, dx_ref, hn_ref, dg_ref) = rest[:k], rest[k:]

        @pl.when(pl.program_id(0) == 0)
        def _():
            dg_ref[...] = jnp.zeros_like(dg_ref)

        dhn, off = jnp.zeros((tm, d), F32), 0
        for dp_ref, wd in zip(dp_refs, widths):
            dhn = dhn + _dot_nt(dp_ref[...], w_ref[:, off:off + wd])
            off += wd
        gain_v = g_ref[...]
        hn, xhat, rstd = _rms_fwd(x_ref[...], gain_v)
        hn_ref[...] = hn.astype(BF)
        dx, dgr = _rms_bwd(dhn, xhat, rstd, gain_v)
        dx_ref[...] = dr_ref[...] + dx
        dg_ref[...] += jnp.sum(dgr, axis=0, keepdims=True)

    row = lambda wd: pl.BlockSpec((tm, wd), lambda i: (i, 0))
    return pl.pallas_call(
        body, name=name, grid=(t // tm,),
        in_specs=[row(d)] + [row(wd) for wd in widths] + [row(d), _full((1, d)), _full((d, n))],
        out_specs=[row(d), row(d), _full((1, d))],
        out_shape=[jax.ShapeDtypeStruct((t, d), F32), jax.ShapeDtypeStruct((t, d), BF),
                   jax.ShapeDtypeStruct((1, d), F32)],
        compiler_params=_params(("arbitrary",)),
    )(dres, *dparts, x, gain, w)


def _mix_out(x, yf, yg, w, name):
    t, d = x.shape
    kf = yf.shape[1]
    tm = _tile(t, TOKEN_TILE)

    def body(x_ref, yf_ref, yg_ref, w_ref, o_ref):
        o_ref[...] = x_ref[...] + _dot(yf_ref[...], w_ref[0:kf, :]) + _dot(yg_ref[...], w_ref[kf:2 * kf, :])

    row = lambda wd: pl.BlockSpec((tm, wd), lambda i: (i, 0))
    return pl.pallas_call(
        body, name=name, grid=(t // tm,),
        in_specs=[row(d), row(kf), row(kf), _full((2 * kf, d))],
        out_specs=row(d), out_shape=jax.ShapeDtypeStruct((t, d), F32),
        compiler_params=_params(("parallel",)),
    )(x, yf, yg, w)


def _mix_out_bwd(dx, w, name):
    t, d = dx.shape
    kf = w.shape[0] // 2
    tm = _tile(t, TOKEN_TILE)

    def body(dx_ref, w_ref, df_ref, dg_ref, dxb_ref):
        dxb = dx_ref[...].astype(BF)
        dxb_ref[...] = dxb
        df_ref[...] = _dot_nt(dxb, w_ref[0:kf, :]).astype(BF)
        dg_ref[...] = _dot_nt(dxb, w_ref[kf:2 * kf, :]).astype(BF)

    row = lambda wd: pl.BlockSpec((tm, wd), lambda i: (i, 0))
    return pl.pallas_call(
        body, name=name, grid=(t // tm,),
        in_specs=[row(d), _full((2 * kf, d))],
        out_specs=[row(kf), row(kf), row(d)],
        out_shape=[jax.ShapeDtypeStruct((t, kf), BF), jax.ShapeDtypeStruct((t, kf), BF),
                   jax.ShapeDtypeStruct((t, d), BF)],
        compiler_params=_params(("parallel",)),
    )(dx, w)


def _loss_grad(y, target, name):
    t, d = y.shape
    tm = _tile(t, TOKEN_TILE)

    def body(y_ref, t_ref, l_ref, dy_ref):
        @pl.when(pl.program_id(0) == 0)
        def _():
            l_ref[...] = jnp.zeros_like(l_ref)

        diff = y_ref[...] - t_ref[...]
        dy_ref[...] = diff * (1.0 / d)
        part = jnp.sum(jnp.sum(diff * diff, axis=1, keepdims=True), axis=0, keepdims=True)
        l_ref[...] += part * (0.5 / d)

    row = pl.BlockSpec((tm, d), lambda i: (i, 0))
    return pl.pallas_call(
        body, name=name, grid=(t // tm,),
        in_specs=[row, row], out_specs=[_full((1, 1)), row],
        out_shape=[jax.ShapeDtypeStruct((1, 1), F32), jax.ShapeDtypeStruct((t, d), F32)],
        compiler_params=_params(("arbitrary",)),
    )(y, target)


def _head_sum_matrix(width, head):
    r = lax.broadcasted_iota(jnp.int32, (width, width), 0) // head
    c = lax.broadcasted_iota(jnp.int32, (width, width), 1) // head
    return (r == c).astype(BF)


def _head_mean(x, bd):
    return _dot(x.astype(BF), bd) * (1.0 / FOX_HEAD_DIM)


def _mask_dot(mask01, x):
    mb = mask01.astype(BF)
    hi = x.astype(BF)
    r1 = x - hi.astype(F32)
    mid = r1.astype(BF)
    lo = (r1 - mid.astype(F32)).astype(BF)
    return _dot(mb, hi) + _dot(mb, mid) + _dot(mb, lo)


def _fox_prep(proj, wq_t, wk_t, bias_pad, seq, name):
    t = proj.shape[0]
    ts = _tile(seq, FOX_PREP_TILE)
    tpe = seq // ts
    scale = FOX_HEAD_DIM ** -0.5

    def body(q_ref, k_ref, v_ref, gt_ref, wq_ref, wk_ref, b_ref, qo_ref, ko_ref, vo_ref, cum_ref, carry):
        i = pl.program_id(0)
        bd = _head_sum_matrix(FOX_WIDTH, FOX_HEAD_DIM)

        def norm(xv, wv):
            ms = _head_mean(xv * xv, bd)
            return xv * lax.rsqrt(ms + EPS) * wv

        qo_ref[...] = (norm(q_ref[...], wq_ref[...]) * scale).astype(BF)
        ko_ref[...] = norm(k_ref[...], wk_ref[...]).astype(BF)
        vo_ref[...] = v_ref[...].astype(BF)

        @pl.when(i % tpe == 0)
        def _():
            carry[...] = jnp.zeros_like(carry)

        ls = _log_sigmoid(gt_ref[...] + b_ref[...])
        r = lax.broadcasted_iota(jnp.int32, (ts, ts), 0)
        c = lax.broadcasted_iota(jnp.int32, (ts, ts), 1)
        cum = _mask_dot(r >= c, ls) + carry[...]
        cum_ref[...] = cum
        carry[...] = cum[ts - 1:ts, :]

    blk = lambda j: pl.BlockSpec((ts, FOX_WIDTH), lambda i: (i, j))
    gate = pl.BlockSpec((ts, LANES), lambda i: (i, GATE_COL // LANES))
    out = pl.BlockSpec((ts, FOX_WIDTH), lambda i: (i, 0))
    return pl.pallas_call(
        body, name=name, grid=(t // ts,),
        in_specs=[blk(0), blk(1), blk(2), gate, _full((1, FOX_WIDTH)), _full((1, FOX_WIDTH)), _full((1, LANES))],
        out_specs=[out, out, out, pl.BlockSpec((ts, LANES), lambda i: (i, 0))],
        out_shape=[jax.ShapeDtypeStruct((t, FOX_WIDTH), BF)] * 3 + [jax.ShapeDtypeStruct((t, LANES), F32)],
        scratch_shapes=[pltpu.VMEM((1, LANES), F32)],
        compiler_params=_params(("arbitrary",)),
    )(proj, proj, proj, proj, wq_t, wk_t, bias_pad)


def _pick_head_sums(x):
    r = lax.broadcasted_iota(jnp.int32, (FOX_WIDTH, LANES), 0)
    c = lax.broadcasted_iota(jnp.int32, (FOX_WIDTH, LANES), 1)
    sel = (((r % LANES == FOX_HEAD_DIM) & (c == 2 * (r // LANES)))
           | ((r % LANES == 0) & (c == 2 * (r // LANES) + 1))).astype(BF)
    hi = x.astype(BF)
    r1 = x - hi.astype(F32)
    mid = r1.astype(BF)
    lo = (r1 - mid.astype(F32)).astype(BF)
    return _dot(hi, sel) + _dot(mid, sel) + _dot(lo, sel)


def _fox_prep_bwd(proj, dqa, dqb, dk, dv, dkx, wq_t, wk_t, bias_pad, seq, name):
    t = proj.shape[0]
    ts = _tile(seq, FOX_PREP_TILE)
    tpe = seq // ts
    nt = t // ts
    scale = FOX_HEAD_DIM ** -0.5

    def body(q_ref, k_ref, gt_ref, dqa_ref, dqb_ref, dk_ref, dv_ref, dc_ref, wq_ref, wk_ref, b_ref,
             dp_ref, dff_ref, dwq_ref, dwk_ref, db_ref, carry):
        i = pl.program_id(0)
        first = (lax.broadcasted_iota(jnp.int32, (ts, FOX_WIDTH), 1) % LANES) < FOX_HEAD_DIM
        dq_all = jnp.where(first, dqa_ref[...], dqb_ref[...])
        ti = nt - 1 - i
        bd = _head_sum_matrix(FOX_WIDTH, FOX_HEAD_DIM)

        @pl.when(i == 0)
        def _():
            dwq_ref[...] = jnp.zeros_like(dwq_ref)
            dwk_ref[...] = jnp.zeros_like(dwk_ref)
            db_ref[...] = jnp.zeros_like(db_ref)

        def norm_bwd(xv, wv, dyv):
            ms = _head_mean(xv * xv, bd)
            rstd = lax.rsqrt(ms + EPS)
            xhat = xv * rstd
            dxhat = dyv * wv
            mean = _head_mean(dxhat * xhat, bd)
            return rstd * (dxhat - xhat * mean), jnp.sum(dyv * xhat, axis=0, keepdims=True)

        dxq, dwq = norm_bwd(q_ref[...], wq_ref[...], dq_all * scale)
        dxk, dwk = norm_bwd(k_ref[...], wk_ref[...], dk_ref[...])
        dp_ref[:, 0:FOX_WIDTH] = dxq.astype(BF)
        dp_ref[:, FOX_WIDTH:2 * FOX_WIDTH] = dxk.astype(BF)
        dp_ref[:, 2 * FOX_WIDTH:3 * FOX_WIDTH] = dv_ref[...].astype(BF)
        dwq_ref[...] += dwq
        dwk_ref[...] += dwk

        @pl.when(ti % tpe == tpe - 1)
        def _():
            carry[...] = jnp.zeros_like(carry)

        r = lax.broadcasted_iota(jnp.int32, (ts, ts), 0)
        c = lax.broadcasted_iota(jnp.int32, (ts, ts), 1)
        dcum = _pick_head_sums(jnp.where(first, dqb_ref[...], dqa_ref[...]) - dc_ref[...])
        dls = _mask_dot(c >= r, dcum) + carry[...]
        carry[...] = dls[0:1, :]
        z = gt_ref[...] + b_ref[...]
        lane = lax.broadcasted_iota(jnp.int32, (ts, LANES), 1)
        dff = jnp.where(lane < FOX_HEADS, dls * _sigmoid(-z), 0.0)
        dff_ref[...] = dff
        db_ref[...] += jnp.sum(dff, axis=0, keepdims=True)

        @pl.when(i == nt - 1)
        def _():
            fr = lax.broadcasted_iota(jnp.int32, (FOX_WIDTH, FOX_WIDTH), 0) % FOX_HEAD_DIM
            fc = lax.broadcasted_iota(jnp.int32, (FOX_WIDTH, FOX_WIDTH), 1) % FOX_HEAD_DIM
            fold = (fr == fc).astype(F32)
            dwq_ref[...] = _dot(dwq_ref[...], fold, HI)
            dwk_ref[...] = _dot(dwk_ref[...], fold, HI)

    rev = lambda w, j: pl.BlockSpec((ts, w), lambda i: (nt - 1 - i, j))
    return pl.pallas_call(
        body, name=name, grid=(nt,),
        in_specs=[rev(FOX_WIDTH, 0), rev(FOX_WIDTH, 1), rev(LANES, GATE_COL // LANES),
                  rev(FOX_WIDTH, 0), rev(FOX_WIDTH, 0), rev(FOX_WIDTH, 0), rev(FOX_WIDTH, 0), rev(FOX_WIDTH, 0),
                  _full((1, FOX_WIDTH)), _full((1, FOX_WIDTH)), _full((1, LANES))],
        out_specs=[rev(3 * FOX_WIDTH, 0), rev(LANES, 0), _full((1, FOX_WIDTH)), _full((1, FOX_WIDTH)),
                   _full((1, LANES))],
        out_shape=[jax.ShapeDtypeStruct((t, 3 * FOX_WIDTH), BF), jax.ShapeDtypeStruct((t, LANES), F32),
                   jax.ShapeDtypeStruct((1, FOX_WIDTH), F32), jax.ShapeDtypeStruct((1, FOX_WIDTH), F32),
                   jax.ShapeDtypeStruct((1, LANES), F32)],
        scratch_shapes=[pltpu.VMEM((1, LANES), F32)],
        compiler_params=_params(("arbitrary",)),
    )(proj, proj, proj, dqa, dqb, dk, dv, dkx, wq_t, wk_t, bias_pad)


class _Rider:
    def __init__(self, inputs, out_shapes, sems, phases):
        self.inputs, self.out_shapes, self.sems, self.phases = list(inputs), list(out_shapes), list(sems), phases


def _rider_parts(rider):
    if rider is None:
        return [], [], []
    return rider.inputs, rider.out_shapes, rider.sems


def _ride(rider, which, when, refs):
    if rider is not None:
        @pl.when(when)
        def _():
            rider.phases[which](*refs)


def _fox_attn(q, k, v, ck, nb, seq, name, rider=None):
    t = q.shape[0]
    tq = _tile(seq, ATTN_FWD_TILE)
    nq = seq // tq
    npair = FOX_HEADS // 2
    hd = FOX_HEAD_DIM
    r_in, r_out, r_sem = _rider_parts(rider)
    steps = nb * npair * nq

    def body(q_ref, k_ref, v_ref, ck_ref, *rest):
        rin, (o_ref, lse_ref) = rest[:len(r_in)], rest[len(r_in):len(r_in) + 2]
        rout = rest[len(r_in) + 2:len(r_in) + 2 + len(r_out)]
        m_s, acc_s = rest[len(r_in) + 2 + len(r_out):len(r_in) + 4 + len(r_out)]
        riding = (rin, rout, rest[len(r_in) + 4 + len(r_out):])
        step = (pl.program_id(0) * npair + pl.program_id(1)) * nq + pl.program_id(2)
        _ride(rider, 0, step == 0, riding)
        _ride(rider, 1, step == (13 * steps) // 16, riding)
        qi = pl.program_id(2)
        lane = lax.broadcasted_iota(jnp.int32, (tq, LANES), 1)
        m_s[...] = jnp.full(m_s.shape, NEG, F32)
        acc_s[...] = jnp.zeros_like(acc_s)
        qv = q_ref[...]

        def block(kj, r0, nr, nc, on_diagonal):
            cols = pl.ds(pl.multiple_of(kj * tq, tq), nc)
            rows = slice(r0, r0 + nr)
            kv = k_ref[cols, :]
            vv = v_ref[cols, :]
            qr = qv[rows]
            lanes = lane[rows]
            if on_diagonal:
                causal = (r0 + lax.broadcasted_iota(jnp.int32, (nr, nc), 0)
                          >= lax.broadcasted_iota(jnp.int32, (nr, nc), 1))
            for hh in range(2):
                hm = (lanes >= hd) if hh else (lanes < hd)
                qh = jnp.where(hm, qr, jnp.zeros_like(qr))
                s = _dot_nt(qh, kv) - ck_ref[hh, :, cols]
                if on_diagonal:
                    s = jnp.where(causal, s, NEG)
                m_old = m_s[hh, rows]
                m_new = jnp.maximum(m_old, jnp.max(s, axis=-1, keepdims=True))
                p = jnp.exp(s - m_new)
                alpha = jnp.exp(m_old - m_new)
                m_s[hh, rows] = m_new
                vh = jnp.where(lane[:nc] >= hd if hh else lane[:nc] < hd, vv, jnp.ones_like(vv))
                acc_s[hh, rows] = alpha * acc_s[hh, rows] + _dot(p.astype(BF), vh)

        def off_diagonal(kj, carry):
            block(kj, 0, tq, tq, False)
            return carry

        lax.fori_loop(0, qi, off_diagonal, 0)
        strip = tq // DIAGONAL_STRIPS
        for i in range(DIAGONAL_STRIPS):
            block(qi, i * strip, strip, (i + 1) * strip, True)
        a0 = acc_s[0]
        a1 = acc_s[1]
        den = jnp.where(lane < hd, pltpu.roll(a0, hd, axis=1), pltpu.roll(a1, hd, axis=1))
        o_ref[...] = (jnp.where(lane < hd, a0, a1) / den).astype(o_ref.dtype)
        l0 = jnp.sum(jnp.where(lane == hd, a0, 0.0), axis=1, keepdims=True)
        l1 = jnp.sum(jnp.where(lane == 0, a1, 0.0), axis=1, keepdims=True)
        lse_ref[0] = m_s[0] + jnp.log(l0)
        lse_ref[1] = m_s[1] + jnp.log(l1)
        _ride(rider, 2, step == steps - 1, riding)

    qspec = pl.BlockSpec((tq, LANES), lambda b, p, i: (b * nq + i, p))
    kspec = pl.BlockSpec((seq, LANES), lambda b, p, i: (b, p))
    colspec = pl.BlockSpec((None, 2, tq, 1), lambda b, p, i: (b * npair + p, 0, i, 0))
    rowspec = pl.BlockSpec((None, 2, 1, seq), lambda b, p, i: (b * npair + p, 0, 0, 0))
    sem = ("arbitrary",) * 3 if rider else ("parallel",) * 3
    return pl.pallas_call(
        body, name=name, grid=(nb, npair, nq),
        in_specs=[qspec, kspec, kspec, rowspec] + [HBM] * len(r_in),
        out_specs=[qspec, colspec] + [HBM] * len(r_out),
        out_shape=[jax.ShapeDtypeStruct((t, FOX_WIDTH), BF), jax.ShapeDtypeStruct((nb * npair, 2, seq, 1), F32)]
        + r_out,
        scratch_shapes=[pltpu.VMEM((2, tq, 1), F32), pltpu.VMEM((2, tq, LANES), F32)] + r_sem,
        compiler_params=_params(sem, has_side_effects=rider is not None),
    )(q, k, v, ck, *r_in)


def _fox_attn_bwd(q, k, v, o, do, lse, ck, nb, seq, name, rider=None):
    t = q.shape[0]
    tq = _tile(seq, ATTN_BWD_TILE)
    nq = seq // tq
    npair = FOX_HEADS // 2
    hd = FOX_HEAD_DIM
    r_in, r_out, r_sem = _rider_parts(rider)
    steps = nb * npair * nq

    def body(q_ref, k_ref, v_ref, o_ref, do_ref, lse_ref, ck_ref, *rest):
        rin, (dqa_ref, dqb_ref, dk_ref, dv_ref, dkx_ref) = rest[:len(r_in)], rest[len(r_in):len(r_in) + 5]
        rout = rest[len(r_in) + 5:len(r_in) + 5 + len(r_out)]
        dk_s, dv_s = rest[len(r_in) + 5 + len(r_out):len(r_in) + 7 + len(r_out)]
        riding = (rin, rout, rest[len(r_in) + 7 + len(r_out):])
        step = (pl.program_id(0) * npair + pl.program_id(1)) * nq + pl.program_id(2)
        _ride(rider, 0, step == 0, riding)
        _ride(rider, 1, step == (13 * steps) // 16, riding)
        kj = pl.program_id(2)
        lane = lax.broadcasted_iota(jnp.int32, (tq, LANES), 1)

        @pl.when(kj == 0)
        def _():
            dqa_ref[...] = jnp.zeros_like(dqa_ref)
            dqb_ref[...] = jnp.zeros_like(dqb_ref)

        dk_s[...] = jnp.zeros_like(dk_s)
        dv_s[...] = jnp.zeros_like(dv_s)
        kv = k_ref[...]
        vv = v_ref[...]

        def block(qi, r0, nr, nc, on_diagonal):
            rows = pl.ds(pl.multiple_of(qi * tq, tq) + r0, nr)
            keys = slice(0, nc)
            qv = q_ref[rows, :]
            dov = do_ref[rows, :]
            kc, vc = kv[keys], vv[keys]
            prod = dov.astype(F32) * o_ref[rows, :].astype(F32)
            lq, lk = lane[:nr], lane[:nc]
            if on_diagonal:
                causal = (r0 + lax.broadcasted_iota(jnp.int32, (nr, nc), 0)
                          >= lax.broadcasted_iota(jnp.int32, (nr, nc), 1))
            for hh, dq_ref in ((0, dqa_ref), (1, dqb_ref)):
                hm = (lq >= hd) if hh else (lq < hd)
                hk = (lk >= hd) if hh else (lk < hd)
                zero = jnp.zeros_like(qv)
                doh = jnp.where(hm, dov, zero)
                delta = jnp.sum(jnp.where(hm, prod, 0.0), axis=-1, keepdims=True)
                s = _dot_nt(jnp.where(hm, qv, zero), kc) - ck_ref[hh, :, keys]
                if on_diagonal:
                    s = jnp.where(causal, s, NEG)
                p = jnp.exp(s - lse_ref[hh, rows, :])
                dp = _dot_nt(doh, vc)
                dsb = (p * (dp - delta)).astype(BF)
                dv_s[keys] += _dot_tn(p.astype(BF), doh)
                dk_s[hh, keys] += _dot_tn(dsb, jnp.where(hm, qv, jnp.ones_like(qv)))
                dq_ref[rows, :] += _dot(dsb, jnp.where(hk, kc, jnp.ones_like(kc)))

        def off_diagonal(qi, carry):
            block(qi, 0, tq, tq, False)
            return carry

        strip = tq // DIAGONAL_STRIPS
        for i in range(DIAGONAL_STRIPS):
            block(kj, i * strip, strip, (i + 1) * strip, True)
        lax.fori_loop(kj + 1, nq, off_diagonal, 0)
        dk_ref[...] = jnp.where(lane < hd, dk_s[0], dk_s[1])
        dkx_ref[...] = jnp.where(lane < hd, dk_s[1], dk_s[0])
        dv_ref[...] = dv_s[...]
        _ride(rider, 2, step == steps - 1, riding)

    kspec = pl.BlockSpec((tq, LANES), lambda b, p, j: (b * nq + j, p))
    full_q = pl.BlockSpec((seq, LANES), lambda b, p, j: (b, p))
    colspec = pl.BlockSpec((None, 2, seq, 1), lambda b, p, j: (b * npair + p, 0, 0, 0))
    rowspec = pl.BlockSpec((None, 2, 1, tq), lambda b, p, j: (b * npair + p, 0, 0, j))
    sem = ("arbitrary",) * 3 if rider else ("parallel", "parallel", "arbitrary")
    return pl.pallas_call(
        body, name=name, grid=(nb, npair, nq),
        in_specs=[full_q, kspec, kspec, full_q, full_q, colspec, rowspec] + [HBM] * len(r_in),
        out_specs=[full_q, full_q, kspec, kspec, kspec] + [HBM] * len(r_out),
        out_shape=[jax.ShapeDtypeStruct((t, FOX_WIDTH), F32)] * 5 + r_out,
        scratch_shapes=[pltpu.VMEM((2, tq, LANES), F32), pltpu.VMEM((tq, LANES), F32)] + r_sem,
        compiler_params=_params(sem, has_side_effects=rider is not None),
    )(q, k, v, o, do, lse, ck, *r_in)


GDN_QKV = 3 * GDN_WIDTH
GDN_COL = 3 * FOX_WIDTH
GG_COL = GDN_COL + GDN_QKV
A_LANE = FOX_HEADS
B_LANE = FOX_HEADS + GDN_HEADS
HALO = 8


def _gate_lanes(ts):
    lane = lax.broadcasted_iota(jnp.int32, (ts, LANES), 1)
    return (lane >= A_LANE) & (lane < B_LANE), (lane >= B_LANE) & (lane < B_LANE + GDN_HEADS)


def _chunk_tri(ts, upper):
    r = lax.broadcasted_iota(jnp.int32, (ts, ts), 0)
    c = lax.broadcasted_iota(jnp.int32, (ts, ts), 1)
    same = (r // CHUNK) == (c // CHUNK)
    return (same & ((c >= r) if upper else (r >= c))).astype(F32)


def _shift_rows(x, edge, k, down):
    ts = x.shape[0]
    row = lax.broadcasted_iota(jnp.int32, (HALO, x.shape[1]), 0)
    if down:
        rolled = pltpu.roll(x, k, axis=0)
        patch = jnp.where(row < k, pltpu.roll(edge, k, axis=0), rolled[:HALO])
        return jnp.concatenate([patch, rolled[HALO:]], axis=0)
    rolled = pltpu.roll(x, ts - k, axis=0)
    patch = jnp.where(row >= HALO - k, pltpu.roll(edge, HALO - k, axis=0), rolled[ts - HALO:])
    return jnp.concatenate([rolled[:ts - HALO], patch], axis=0)


def _conv_silu(x, before, w):
    taps = [_shift_rows(x, before, CONV_WIDTH - 1 - kk, True) for kk in range(CONV_WIDTH - 1)] + [x]
    c = w[0:1, :] * taps[0]
    for kk in range(1, CONV_WIDTH):
        c = c + w[kk:kk + 1, :] * taps[kk]
    return taps, c, c * _sigmoid(c)


def _gdn_prep(proj, conv_w, a_pad, dt_pad, seq, name):
    t = proj.shape[0]
    ts = _tile(seq, GDN_PREP_TILE)
    tpe = seq // ts
    qscale = GDN_HEAD_DIM ** -0.5

    def body(x_ref, gt_ref, w_ref, a_ref, dt_ref, qo_ref, ko_ref, vo_ref, go_ref, tail):
        i = pl.program_id(0)
        xv = x_ref[...]
        before = jnp.where(i % tpe == 0, jnp.zeros((HALO, GDN_QKV), F32), tail[...])
        tail[...] = xv[ts - HALO:]
        _, _, s = _conv_silu(xv, before, w_ref[...])
        for h in range(GDN_HEADS):
            for base, ref, sc in ((0, qo_ref, qscale), (GDN_WIDTH, ko_ref, 1.0)):
                xh = s[:, base + h * LANES: base + (h + 1) * LANES]
                r = lax.rsqrt(jnp.sum(xh * xh, axis=-1, keepdims=True) + EPS)
                ref[:, h * LANES:(h + 1) * LANES] = (xh * (r * sc)).astype(BF)
        vo_ref[...] = s[:, 2 * GDN_WIDTH:].astype(BF)
        gate = gt_ref[...]
        g_raw = -jnp.exp(a_ref[...]) * _softplus(gate + dt_ref[...])
        gc = _mask_dot(_chunk_tri(ts, False), g_raw)
        is_a, is_b = _gate_lanes(ts)
        go_ref[...] = jnp.where(is_a, gc, jnp.where(is_b, _sigmoid(gate), 0.0))

    out = pl.BlockSpec((ts, GDN_WIDTH), lambda i: (i, 0))
    lanes = pl.BlockSpec((ts, LANES), lambda i: (i, 0))
    return pl.pallas_call(
        body, name=name, grid=(t // ts,),
        in_specs=[pl.BlockSpec((ts, GDN_QKV), lambda i: (i, GDN_COL // GDN_QKV)),
                  pl.BlockSpec((ts, LANES), lambda i: (i, GATE_COL // LANES)),
                  _full((CONV_WIDTH, GDN_QKV)), _full((1, LANES)), _full((1, LANES))],
        out_specs=[out, out, out, lanes],
        out_shape=[jax.ShapeDtypeStruct((t, GDN_WIDTH), BF)] * 3 + [jax.ShapeDtypeStruct((t, LANES), F32)],
        scratch_shapes=[pltpu.VMEM((HALO, GDN_QKV), F32)],
        compiler_params=_params(("arbitrary",)),
    )(proj, proj, conv_w, a_pad, dt_pad)


def _gdn_prep_bwd(proj, dq, dk, dv, dgates, dff, conv_w, a_pad, dt_pad, seq, name):
    t = proj.shape[0]
    ts = _tile(seq, GDN_PREP_TILE)
    tpe = seq // ts
    nt = t // ts
    qscale = GDN_HEAD_DIM ** -0.5
    hb = ts // HALO

    def body(x_ref, halo_ref, gt_ref, dq_ref, dk_ref, dv_ref, dgt_ref, dff_ref, w_ref, a_ref, dt_ref,
             dx_ref, dgo_ref, dw_ref, da_ref, ddt_ref, dsl, carry):
        i = pl.program_id(0)
        ti = nt - 1 - i

        @pl.when(i == 0)
        def _():
            dw_ref[...] = jnp.zeros_like(dw_ref)
            da_ref[...] = jnp.zeros_like(da_ref)
            ddt_ref[...] = jnp.zeros_like(ddt_ref)

        halo = halo_ref[...]
        before = jnp.where(ti % tpe == 0, jnp.zeros_like(halo), halo)
        w = w_ref[...]
        taps, c, s = _conv_silu(x_ref[...], before, w)
        for h in range(GDN_HEADS):
            for base, ref, sc in ((0, dq_ref, qscale), (GDN_WIDTH, dk_ref, 1.0)):
                lo = base + h * LANES
                xh = s[:, lo:lo + LANES]
                r = lax.rsqrt(jnp.sum(xh * xh, axis=-1, keepdims=True) + EPS)
                y = xh * r
                dy = ref[:, h * LANES:(h + 1) * LANES] * sc
                dsl[:, lo:lo + LANES] = r * (dy - y * jnp.sum(dy * y, axis=-1, keepdims=True))
        dsl[:, 2 * GDN_WIDTH:] = dv_ref[...]
        sg = _sigmoid(c)
        dc = dsl[...] * (sg * (1.0 + c * (1.0 - sg)))
        nxt = carry[...]
        after = jnp.where(ti % tpe == tpe - 1, jnp.zeros_like(nxt), nxt)
        carry[...] = dc[0:HALO, :]
        dx = w[CONV_WIDTH - 1:CONV_WIDTH, :] * dc
        for kk in range(CONV_WIDTH - 1):
            dx = dx + w[kk:kk + 1, :] * _shift_rows(dc, after, CONV_WIDTH - 1 - kk, False)
        dx_ref[...] = dx.astype(BF)
        for kk in range(CONV_WIDTH):
            dw_ref[kk:kk + 1, :] += jnp.sum(dc * taps[kk], axis=0, keepdims=True)
        gate = gt_ref[...]
        dgt = dgt_ref[...]
        is_a, is_b = _gate_lanes(ts)
        dg_raw = _mask_dot(_chunk_tri(ts, True), jnp.where(is_a, dgt, 0.0))
        z = gate + dt_ref[...]
        na = -jnp.exp(a_ref[...])
        dga = dg_raw * na * _sigmoid(z)
        beta = _sigmoid(gate)
        dgb = jnp.where(is_b, dgt * beta * (1.0 - beta), 0.0)
        dgo_ref[...] = (dff_ref[...] + dga + dgb).astype(BF)
        ddt_ref[...] += jnp.sum(dga, axis=0, keepdims=True)
        da_ref[...] += jnp.sum(dg_raw * na * _softplus(z), axis=0, keepdims=True)

    rev = lambda wd, j: pl.BlockSpec((ts, wd), lambda i: (nt - 1 - i, j))
    halo_spec = pl.BlockSpec((HALO, GDN_QKV), lambda i: (jnp.maximum((nt - 1 - i) * hb - 1, 0), GDN_COL // GDN_QKV))
    return pl.pallas_call(
        body, name=name, grid=(nt,),
        in_specs=[rev(GDN_QKV, GDN_COL // GDN_QKV), halo_spec, rev(LANES, GATE_COL // LANES),
                  rev(GDN_WIDTH, 0), rev(GDN_WIDTH, 0), rev(GDN_WIDTH, 0), rev(LANES, 0), rev(LANES, 0),
                  _full((CONV_WIDTH, GDN_QKV)), _full((1, LANES)), _full((1, LANES))],
        out_specs=[rev(GDN_QKV, 0), rev(LANES, 0), _full((CONV_WIDTH, GDN_QKV)), _full((1, LANES)),
                   _full((1, LANES))],
        out_shape=[jax.ShapeDtypeStruct((t, GDN_QKV), BF), jax.ShapeDtypeStruct((t, LANES), BF),
                   jax.ShapeDtypeStruct((CONV_WIDTH, GDN_QKV), F32), jax.ShapeDtypeStruct((1, LANES), F32),
                   jax.ShapeDtypeStruct((1, LANES), F32)],
        scratch_shapes=[pltpu.VMEM((ts, GDN_QKV), F32), pltpu.VMEM((HALO, GDN_QKV), F32)],
        compiler_params=_params(("arbitrary",)),
    )(proj, proj, proj, dq, dk, dv, dgates, dff, conv_w, a_pad, dt_pad)


PAIR = 2 * CHUNK


def _split_bf16(a):
    hi = a.astype(BF)
    return hi, (a - hi.astype(F32)).astype(BF)


def _dot3(a, b, dims=(((1,), (0,)), ((), ()))):
    ah, al = _split_bf16(a)
    bh, bl = _split_bf16(b)
    (ca,), (cb,) = dims[0]
    return lax.dot_general(jnp.concatenate([ah, al, ah], axis=ca), jnp.concatenate([bh, bh, bl], axis=cb), dims,
                           preferred_element_type=F32)


def _inv_unit_lower(a):
    r = lax.broadcasted_iota(jnp.int32, (PAIR, PAIR), 0)
    c = lax.broadcasted_iota(jnp.int32, (PAIR, PAIR), 1)
    tm = (r == c).astype(F32) - a
    pw = _dot3(a, a)
    for _ in range(4):
        x = _dot3(jnp.concatenate([tm, pw], axis=0), pw)
        tm = tm + x[:PAIR]
        pw = x[PAIR:]
    return tm + _dot3(tm, pw)


def _gdn_pair_local(q, k, v, gc, gr, b):
    r = lax.broadcasted_iota(jnp.int32, (PAIR, PAIR), 0)
    c = lax.broadcasted_iota(jnp.int32, (PAIR, PAIR), 1)
    same = (r // CHUNK) == (c // CHUNK)
    incl = same & (r >= c)
    strict = same & (r > c)
    dm = jnp.exp(jnp.where(incl, gc - gr, NEG))
    e = jnp.exp(gc)
    kb = k * b
    vb = v * b
    kbe = kb * e
    kq = _dot_nt(jnp.concatenate([kb, q], axis=0).astype(BF), k.astype(BF))
    amat = jnp.where(strict, kq[:PAIR] * dm, 0.0)
    pmat = jnp.where(incl, kq[PAIR:] * dm, 0.0)
    lane = lax.broadcasted_iota(jnp.int32, (1, PAIR), 1)
    gl_a = jnp.sum(jnp.where(lane == CHUNK - 1, gr, 0.0), axis=1, keepdims=True)
    gl_b = jnp.sum(jnp.where(lane == PAIR - 1, gr, 0.0), axis=1, keepdims=True)
    ridx = lax.broadcasted_iota(jnp.int32, (PAIR, 1), 0)
    edec = jnp.exp(jnp.where(ridx < CHUNK, gl_a, gl_b) - gc)
    return dict(dm=dm, e=e, kb=kb, vb=vb, kbe=kbe, amat=amat, pmat=pmat, gl_a=gl_a, gl_b=gl_b, edec=edec,
                kd=k * edec, qd=q * e, incl=incl, strict=strict, ridx=ridx)


def _gdn_pair_states(loc, tb, s_a):
    uw = _dot(tb, jnp.concatenate([loc["vb"], loc["kbe"]], axis=1).astype(BF))
    u, w = uw[:, :LANES], uw[:, LANES:]
    qd, kd, c = loc["qd"], loc["kd"], CHUNK
    xa = _dot(jnp.concatenate([qd[:c], w[:c]], axis=0).astype(BF), s_a.astype(BF))
    vn_a = u[:c] - xa[c:]
    s_b = s_a * jnp.exp(loc["gl_a"]) + _dot_tn(kd[:c].astype(BF), vn_a.astype(BF))
    xb = _dot(jnp.concatenate([qd[c:], w[c:]], axis=0).astype(BF), s_b.astype(BF))
    vn_b = u[c:] - xb[c:]
    s_c = s_b * jnp.exp(loc["gl_b"]) + _dot_tn(kd[c:].astype(BF), vn_b.astype(BF))
    vn = jnp.concatenate([vn_a, vn_b], axis=0)
    o = jnp.concatenate([xa[:c], xb[:c]], axis=0) + _dot(loc["pmat"].astype(BF), vn.astype(BF))
    return w, vn, o, s_b, s_c


GDN_SEG = 512


def _gdn_specs(nb, seq, reverse):
    n = seq // CHUNK
    seg = _tile(seq, GDN_SEG)
    nseg = seq // seg
    sp = seg // PAIR
    at = (lambda s: nseg - 1 - s) if reverse else (lambda s: s)
    blk = pl.BlockSpec((nb, seg, GDN_WIDTH), lambda s: (0, at(s), 0))
    gg = pl.BlockSpec((nb, seg, GDN_WIDTH), lambda s: (0, at(s), GG_COL // GDN_WIDTH))
    gates = pl.BlockSpec((nb, seg, LANES), lambda s: (0, at(s), 0))
    rowb = pl.BlockSpec((nb, GDN_HEADS, sp, HALO, PAIR), lambda s: (0, 0, at(s), 0, 0))
    per_pair = pl.BlockSpec((nb, GDN_HEADS, sp, PAIR, PAIR), lambda s: (0, 0, at(s), 0, 0))
    return n, seg, nseg, sp, blk, gg, gates, rowb, per_pair


def _head_column(gt, lane, index):
    return jnp.sum(jnp.where(lane == index, gt, 0.0), axis=1, keepdims=True)


def _gdn_head_inputs(qkv_refs, gt_ref, gr_ref, rows, pi, lane, chains):
    per_chain = []
    for b, hh in chains:
        gt = gt_ref[b, rows, :]
        cols = slice(hh * LANES, (hh + 1) * LANES)
        per_chain.append([r[b, rows, cols].astype(F32) for r in qkv_refs]
                         + [_head_column(gt, lane, A_LANE + hh), gr_ref[b, hh, pi][0:1, :],
                            _head_column(gt, lane, B_LANE + hh)])
    return [jnp.stack(xs) for xs in zip(*per_chain)]


def _gdn_pair_fwd(qv, kv, vv, gcv, gr, bv, s_a):
    loc = _gdn_pair_local(qv, kv, vv, gcv, gr, bv)
    tf = _inv_unit_lower(loc["amat"])
    _, _, o, _, s_c = _gdn_pair_states(loc, tf.astype(BF), s_a)
    return tf, o, s_c


def _gdn_fwd(q, k, v, proj, gates, grow, wn, nb, seq, name):
    n, seg, nseg, sp, blk, gg, gates_spec, rowb, per_pair = _gdn_specs(nb, seq, False)
    chains = [(b, hh) for b in range(nb) for hh in range(GDN_HEADS)]

    def body(q_ref, k_ref, v_ref, gg_ref, gt_ref, gr_ref, wn_ref, y_ref, tn_ref, sn_ref, s_ref):
        @pl.when(pl.program_id(0) == 0)
        def _():
            s_ref[...] = jnp.zeros_like(s_ref)

        wnv = wn_ref[...]
        lane = lax.broadcasted_iota(jnp.int32, (PAIR, LANES), 1)

        def step(pi, carry):
            rows = pl.ds(pl.multiple_of(pi * PAIR, PAIR), PAIR)
            ins = _gdn_head_inputs((q_ref, k_ref, v_ref), gt_ref, gr_ref, rows, pi, lane, chains)
            s_a = s_ref[...]
            tf, o, s_c = jax.vmap(_gdn_pair_fwd)(*ins, s_a)
            s_ref[...] = s_c
            for c, (b, hh) in enumerate(chains):
                cols = slice(hh * LANES, (hh + 1) * LANES)
                tn_ref[b, hh, pi] = tf[c]
                sn_ref[b, hh, pi] = s_a[c]
                g = gg_ref[b, rows, cols]
                oh = o[c]
                rstd = lax.rsqrt(jnp.mean(oh * oh, axis=-1, keepdims=True) + EPS)
                y_ref[b, rows, cols] = (oh * rstd * wnv * (g * _sigmoid(g))).astype(BF)
            return carry

        lax.fori_loop(0, sp, step, 0)

    saved = jax.ShapeDtypeStruct((nb, GDN_HEADS, n // 2, PAIR, PAIR), F32)
    return pl.pallas_call(
        body, name=name, grid=(nseg,),
        in_specs=[blk, blk, blk, gg, gates_spec, rowb, _full((1, LANES))],
        out_specs=[blk, per_pair, per_pair],
        out_shape=[jax.ShapeDtypeStruct((nb, seq, GDN_WIDTH), BF), saved, saved],
        scratch_shapes=[pltpu.VMEM((len(chains), GDN_HEAD_DIM, GDN_HEAD_DIM), F32)],
        compiler_params=_params(("arbitrary",)),
    )(q, k, v, proj, gates, grow, wn)


def _gdn_pair_bwd(qv, kv, vv, gcv, gr, bv, tf, s_a, dsp, g, dyv, wnv):
    c = CHUNK
    loc = _gdn_pair_local(qv, kv, vv, gcv, gr, bv)
    tm = tf.astype(BF)
    kb, vb, kbe, e, dm = loc["kb"], loc["vb"], loc["kbe"], loc["e"], loc["dm"]
    kd, qd, pmat, amat = loc["kd"], loc["qd"], loc["pmat"], loc["amat"]
    w, vn, o, s_b, _ = _gdn_pair_states(loc, tm, s_a)
    sg = _sigmoid(g)
    silu = g * sg
    rstd = lax.rsqrt(jnp.mean(o * o, axis=-1, keepdims=True) + EPS)
    xhat = o * rstd
    dwn = jnp.sum(dyv * xhat * silu, axis=0, keepdims=True)
    dgg = dyv * xhat * wnv * (sg * (1.0 + g * (1.0 - sg)))
    dxhat = dyv * wnv * silu
    do = rstd * (dxhat - xhat * jnp.mean(dxhat * xhat, axis=-1, keepdims=True))
    dob = do.astype(BF)
    tot = lambda x: jnp.sum(jnp.sum(x, axis=1, keepdims=True), axis=0, keepdims=True)
    rsum = lambda x: jnp.sum(x, axis=1, keepdims=True)
    cat = lambda xs, ax=0: jnp.concatenate(xs, axis=ax)
    wb = w.astype(BF)
    qdb = qd.astype(BF)
    kdb = kd.astype(BF)
    vnb = vn.astype(BF)
    egl_a = jnp.exp(loc["gl_a"])
    egl_b = jnp.exp(loc["gl_b"])
    ptdo = _dot_tn(pmat.astype(BF), dob)
    dspb = dsp.astype(BF)
    dvn_b = ptdo[c:] + _dot(kdb[c:], dspb)
    dkd_b = _dot_nt(vnb[c:], dspb)
    dgl_b = egl_b * tot(s_b * dsp) + tot(dkd_b * kd[c:])
    dsm = egl_b * dsp + _dot_tn(cat([qdb[c:], -wb[c:]]), cat([dob[c:], dvn_b.astype(BF)]))
    dsmb = dsm.astype(BF)
    dvn_a = ptdo[:c] + _dot(kdb[:c], dsmb)
    dkd_a = _dot_nt(vnb[:c], dsmb)
    dgl_a = egl_a * tot(s_a * dsm) + tot(dkd_a * kd[:c])
    ds_new = egl_a * dsm + _dot_tn(cat([qdb[:c], -wb[:c]]), cat([dob[:c], dvn_a.astype(BF)]))
    ya = _dot_nt(cat([dob[:c], dvn_a.astype(BF)]), s_a.astype(BF))
    yb = _dot_nt(cat([dob[c:], dvn_b.astype(BF)]), s_b.astype(BF))
    dqd = cat([ya[:c], yb[:c]])
    dw = -cat([ya[c:], yb[c:]])
    dvn = cat([dvn_a, dvn_b])
    dkd = cat([dkd_a, dkd_b])
    dq = dqd * e
    dgc = rsum(dqd * qd) - rsum(dkd * kd)
    dk = dkd * loc["edec"]
    dpm = jnp.where(loc["incl"], _dot_nt(dob, vnb), 0.0)
    duw = cat([dvn, dw], 1).astype(BF)
    dt = _dot_nt(duw, cat([vb, kbe], 1).astype(BF))
    tt = _dot_tn(tm, duw)
    dvb, dkbe = tt[:, :LANES], tt[:, LANES:]
    tn_dims = (((0,), (0,)), ((), ()))
    nt_dims = (((1,), (1,)), ((), ()))
    da = jnp.where(loc["strict"], -_dot3(_dot3(tf, dt, tn_dims), tf, nt_dims), 0.0)
    st = cat([da * dm, dpm * dm]).astype(BF)
    z = _dot(st, kv.astype(BF))
    dkb = z[:PAIR] + dkbe * e
    dq = dq + z[PAIR:]
    dk = dk + _dot_tn(st, cat([kb, qv]).astype(BF))
    gmat = dpm * pmat + da * amat
    dgc = dgc + rsum(dkbe * kbe) + rsum(gmat)
    ridx = loc["ridx"]
    dgc = dgc + jnp.where(ridx == c - 1, dgl_a, 0.0) + jnp.where(ridx == PAIR - 1, dgl_b, 0.0)
    dgc_row = jnp.sum(gmat, axis=0, keepdims=True)
    db = rsum(dvb * vv) + rsum(dkb * kv)
    return dq, dk + dkb * bv, dvb * bv, dgg, dgc, dgc_row, db, dwn, ds_new


def _gdn_bwd(q, k, v, proj, gates, grow, wn, tinv_all, states_all, dy, nb, seq, name):
    n, seg, nseg, sp, blk, gg, gates_spec, rowb, per_pair = _gdn_specs(nb, seq, True)
    dh = GDN_HEAD_DIM
    chains = [(b, hh) for b in range(nb) for hh in range(GDN_HEADS)]

    def body(q_ref, k_ref, v_ref, gg_ref, gt_ref, gr_ref, wn_ref, tn_ref, sn_ref, dy_ref,
             dq_ref, dk_ref, dv_ref, dgg_ref, dgt_ref, dwn_ref, ds_ref):
        @pl.when(pl.program_id(0) == 0)
        def _():
            dwn_ref[...] = jnp.zeros_like(dwn_ref)
            ds_ref[...] = jnp.zeros_like(ds_ref)

        wnv = wn_ref[...]
        lane = lax.broadcasted_iota(jnp.int32, (PAIR, LANES), 1)

        def bwd_step(j, carry):
            pi = sp - 1 - j
            rows = pl.ds(pl.multiple_of(pi * PAIR, PAIR), PAIR)
            ins = _gdn_head_inputs((q_ref, k_ref, v_ref), gt_ref, gr_ref, rows, pi, lane, chains)
            lanes_of = lambda hh: slice(hh * LANES, (hh + 1) * LANES)
            saved = [jnp.stack([r[b, hh, pi] for b, hh in chains]) for r in (tn_ref, sn_ref)]
            g2 = jnp.stack([gg_ref[b, rows, lanes_of(hh)] for b, hh in chains])
            dy2 = jnp.stack([dy_ref[b, rows, lanes_of(hh)].astype(F32) for b, hh in chains])
            dq, dk, dv, dgg, dgc, dgc_row, db, dwn, ds_new = jax.vmap(
                _gdn_pair_bwd, in_axes=(0,) * 11 + (None,))(*ins, *saved, ds_ref[...], g2, dy2, wnv)
            ds_ref[...] = ds_new
            dgt = [jnp.zeros((PAIR, LANES), F32) for _ in range(nb)]
            for c, (b, hh) in enumerate(chains):
                cols = lanes_of(hh)
                dq_ref[b, rows, cols] = dq[c]
                dk_ref[b, rows, cols] = dk[c]
                dv_ref[b, rows, cols] = dv[c]
                dgg_ref[b, rows, cols] = dgg[c].astype(BF)
                dwn_ref[...] += dwn[c]
                row_as_col = jnp.transpose(jnp.broadcast_to(dgc_row[c], (PAIR, LANES)))
                dgt[b] = (dgt[b] + jnp.where(lane == A_LANE + hh, dgc[c] - row_as_col, 0.0)
                          + jnp.where(lane == B_LANE + hh, db[c], 0.0))
            for b in range(nb):
                dgt_ref[b, rows, :] = dgt[b]
            return carry

        lax.fori_loop(0, sp, bwd_step, 0)

    f32_out = jax.ShapeDtypeStruct((nb, seq, GDN_WIDTH), F32)
    return pl.pallas_call(
        body, name=name, grid=(nseg,),
        in_specs=[blk, blk, blk, gg, gates_spec, rowb, _full((1, LANES)), per_pair, per_pair, blk],
        out_specs=[blk, blk, blk, blk, gates_spec, _full((1, LANES))],
        out_shape=[f32_out, f32_out, f32_out, jax.ShapeDtypeStruct((nb, seq, GDN_WIDTH), BF),
                   jax.ShapeDtypeStruct((nb, seq, LANES), F32), jax.ShapeDtypeStruct((1, LANES), F32)],
        scratch_shapes=[pltpu.VMEM((len(chains), dh, dh), F32)],
        compiler_params=_params(("arbitrary",)),
    )(q, k, v, proj, gates, grow, wn, tinv_all, states_all, dy)


def _mix_to_padded(w):
    pad = jnp.zeros(w.shape[:-1] + (N_PAD - N_IN,), w.dtype)
    return jnp.concatenate([w[..., 0:1536], w[..., 1544:3080], w[..., 3088:3600], w[..., 1536:1544],
                            w[..., 3080:3088], pad], axis=-1)


def _pad_lanes(vec, start):
    return jnp.pad(vec[None, :], ((0, 0), (start, LANES - start - vec.shape[0])))


def _heads_to_rows(block, lane0, nheads, nb, seq):
    return block[:, lane0:lane0 + nheads].reshape(nb, seq, nheads).transpose(0, 2, 1).reshape(nb * nheads, seq)


def _mixer_small(p, l):
    wq_t = jnp.tile(p["fox_q_norm"][l], FOX_HEADS)[None, :]
    wk_t = jnp.tile(p["fox_k_norm"][l], FOX_HEADS)[None, :]
    bias = _pad_lanes(p["fox_f_bias"][l], 0)
    a_pad = _pad_lanes(p["gdn_a_log"][l], A_LANE)
    dt_pad = _pad_lanes(p["gdn_dt_bias"][l], A_LANE)
    wn = p["gdn_out_norm"][l][None, :]
    return wq_t, wk_t, bias, a_pad, dt_pad, wn


def _layer_fwd(x, p, l, nb, seq, rider=None, ffn1_rider=None, after_ffn1=None):
    npair = FOX_HEADS // 2
    n = seq // CHUNK
    x1, h1, *rode1 = _ffn_fwd(x, p["ffn1_norm"][l][None, :], p["ffn1_w_in"][l], p["ffn1_w_out"][l],
                              f"ffn1_fwd_{l}", ffn1_rider)
    if after_ffn1 is not None:
        after_ffn1(rode1)
    wq_t, wk_t, bias, a_pad, dt_pad, wn = _mixer_small(p, l)
    proj = _norm_matmul(x1, p["mix_norm"][l][None, :], p["w_mix"][l], f"mix_in_{l}")
    fq, fk, fv, cum = _fox_prep(proj, wq_t, wk_t, bias, seq, f"fox_prep_{l}")
    c8 = _heads_to_rows(cum, 0, FOX_HEADS, nb, seq)
    ck = c8.reshape(nb * npair, 2, 1, seq)
    o, lse, *rode = _fox_attn(fq, fk, fv, ck, nb, seq, f"fox_attn_{l}", rider)
    gq, gk, gv, gates = _gdn_prep(proj, p["gdn_conv"][l], a_pad, dt_pad, seq, f"gdn_prep_{l}")
    gc4 = _heads_to_rows(gates, A_LANE, GDN_HEADS, nb, seq)
    grow = jnp.broadcast_to(gc4.reshape(nb, GDN_HEADS, n // 2, 1, PAIR), (nb, GDN_HEADS, n // 2, HALO, PAIR))
    per_example = lambda a: a.reshape(nb, seq, a.shape[-1])
    gq, gk, gv, gates = per_example(gq), per_example(gk), per_example(gv), per_example(gates)
    y, tinv, states = _gdn_fwd(gq, gk, gv, per_example(proj), gates, grow, wn, nb, seq, f"gdn_fwd_{l}")
    y = y.reshape(nb * seq, GDN_WIDTH)
    x2 = _mix_out(x1, o, y, p["w_out"][l], f"mix_out_{l}")
    x3, h2 = _ffn_fwd(x2, p["ffn2_norm"][l][None, :], p["ffn2_w_in"][l], p["ffn2_w_out"][l], f"ffn2_fwd_{l}")
    saved = dict(x=x, h1=h1, x1=x1, proj=proj, fq=fq, fk=fk, fv=fv, ck=ck, o=o, lse=lse,
                 gq=gq, gk=gk, gv=gv, gates=gates, grow=grow, tinv=tinv, states=states, y=y, x2=x2, h2=h2)
    return x3, saved, rode


def _ffn_grads(dy, x, h, gain, win, wout, l, tag, rider=None, after_w_out=None):
    t, d = x.shape
    fs = win.shape[2]
    dx, dh, a, hn, dyh, dgain, *rode = _ffn_bwd(dy, x, h, gain, win, wout, f"{tag}_bwd_{l}", rider)
    g_out = _wgrad(a, dyh, jax.ShapeDtypeStruct((2 * fs, d), BF),
                   pl.BlockSpec((fs, d), lambda i, j, k: (i, j)), fs, d, f"{tag}_gw_out_{l}").reshape(4, fs // 2, d)
    w_rider = after_w_out(g_out) if after_w_out is not None else None
    g_in = _wgrad(hn, dh, jax.ShapeDtypeStruct((4, d, fs), BF),
                  pl.BlockSpec((None, d, fs), lambda i, j, k: (j, i, 0)), d, fs, f"{tag}_gw_in_{l}", rider=w_rider)
    rode_w = []
    if w_rider is not None:
        g_in, *rode_w = g_in
    return dx, dgain[0], g_in, g_out, rode, rode_w


def _layer_bwd(dx3, p, l, sv, nb, seq, rider=None, before_ffn1=None, ffn2_rider=None, after_ffn2=None,
               after_ffn1_w_out=None):
    npair = FOX_HEADS // 2
    d = dx3.shape[1]
    wq_t, wk_t, bias, a_pad, dt_pad, wn = _mixer_small(p, l)
    g = {}
    dx2, g["ffn2_norm"], g["ffn2_w_in"], g["ffn2_w_out"], rode2, _ = _ffn_grads(
        dx3, sv["x2"], sv["h2"], p["ffn2_norm"][l][None, :], p["ffn2_w_in"][l], p["ffn2_w_out"][l], l, "ffn2",
        ffn2_rider)
    if after_ffn2 is not None:
        rider = after_ffn2(rode2)
    dyf, dyg, dxb = _mix_out_bwd(dx2, p["w_out"][l], f"mix_out_bwd_{l}")
    half = lambda a, nm: _wgrad(a, dxb, jax.ShapeDtypeStruct((FOX_WIDTH, d), BF),
                                pl.BlockSpec((FOX_WIDTH, d), lambda i, j, k: (i, j)), FOX_WIDTH, d, nm)
    g["w_out"] = jnp.concatenate([half(sv["o"], f"gw_out_fox_{l}"), half(sv["y"], f"gw_out_gdn_{l}")], axis=0)
    dqa, dqb, dk, dv, dkx, *rode = _fox_attn_bwd(sv["fq"], sv["fk"], sv["fv"], sv["o"], dyf, sv["lse"], sv["ck"],
                                                 nb, seq, f"fox_attn_bwd_{l}", rider)

    dpf, dff, dwq, dwk, dbias = _fox_prep_bwd(sv["proj"], dqa, dqb, dk, dv, dkx, wq_t, wk_t, bias, seq,
                                              f"fox_prep_bwd_{l}")
    g["fox_q_norm"] = dwq[0, :FOX_HEAD_DIM]
    g["fox_k_norm"] = dwk[0, :FOX_HEAD_DIM]
    g["fox_f_bias"] = dbias[0, :FOX_HEADS]
    per_example = lambda a: a.reshape(nb, seq, a.shape[-1])
    flat = lambda a: a.reshape(nb * seq, a.shape[-1])
    dgq, dgk, dgv, dgg, dgates, dwn = _gdn_bwd(
        sv["gq"], sv["gk"], sv["gv"], per_example(sv["proj"]), sv["gates"], sv["grow"], wn, sv["tinv"],
        sv["states"], per_example(dyg), nb, seq, f"gdn_bwd_{l}")
    dgq, dgk, dgv, dgg, dgates = flat(dgq), flat(dgk), flat(dgv), flat(dgg), flat(dgates)
    dpg, dgate_blk, dconv, da, ddt = _gdn_prep_bwd(sv["proj"], dgq, dgk, dgv, dgates, dff, p["gdn_conv"][l],
                                                   a_pad, dt_pad, seq, f"gdn_prep_bwd_{l}")
    g["gdn_conv"] = dconv
    g["gdn_a_log"] = da[0, A_LANE:B_LANE]
    g["gdn_dt_bias"] = ddt[0, A_LANE:B_LANE]
    g["gdn_out_norm"] = dwn[0]
    dparts = [dpf, dpg, dgg, dgate_blk]
    dx1, hnm, dgm = _norm_matmul_bwd(dx2, dparts, sv["x1"], p["mix_norm"][l][None, :], p["w_mix"][l],
                                     f"mix_in_bwd_{l}")
    g["mix_norm"] = dgm[0]
    gp = _wgrad_parts(hnm, dparts, d // 2, f"gw_mix_{l}")
    gate = GATE_COL
    g["w_in"] = jnp.concatenate([gp[:, :GDN_COL], gp[:, gate:gate + FOX_HEADS], gp[:, GDN_COL:GG_COL],
                                 gp[:, gate + A_LANE:gate + B_LANE + GDN_HEADS], gp[:, GG_COL:gate]], axis=1)
    ffn1_rider = before_ffn1(g) if before_ffn1 is not None else None
    dx0, g["ffn1_norm"], g["ffn1_w_in"], g["ffn1_w_out"], rode1, rode_w = _ffn_grads(
        dx1, sv["x"], sv["h1"], p["ffn1_norm"][l][None, :], p["ffn1_w_in"][l], p["ffn1_w_out"][l], l, "ffn1",
        ffn1_rider, after_ffn1_w_out)
    return dx0, g, rode, rode1, rode_w


N_CHIPS = 4


def _mesh_pos():
    return lax.axis_index("x"), lax.axis_index("y"), lax.axis_index("c")


def _other_chips(x, y):
    return [(1 - x, y), (x, 1 - y), (1 - x, 1 - y)]


def _remote(src, dst, send_sem, recv_sem, to):
    return pltpu.make_async_remote_copy(src_ref=src, dst_ref=dst, send_sem=send_sem, recv_sem=recv_sem,
                                        device_id=to, device_id_type=MESH)


def _hbm_call(body, name, ins, out_shape, scratch):
    return pl.pallas_call(
        body, name=name, out_shape=out_shape, in_specs=[HBM] * len(ins),
        out_specs=jax.tree.map(lambda _: HBM, out_shape), scratch_shapes=scratch,
        compiler_params=pltpu.CompilerParams(has_side_effects=True),
    )(*ins)


def _gather_phases(n, layer):
    def copies(ins, outs, sems):
        send1, recv1, send2, recv2 = sems
        x, y, c = _mesh_pos()
        out, back, fwd = [], [], []
        for i in range(n):
            for j, (px, py) in enumerate(_other_chips(x, y)):
                k = 3 * i + j
                blk = outs[i].at[2 * px + py]
                out.append(_remote(ins[i], outs[i].at[2 * x + y], send1.at[k], recv1.at[k], (px, py, c)))
                back.append(_remote(blk, blk, send1.at[k], recv1.at[k], (px, py, c)))
                fwd.append(_remote(blk, blk, send2.at[k], recv2.at[k], (x, y, 1 - c)))
        return c, out, back, fwd

    def first(ins, outs, sems):
        c, out, _, _ = copies(ins, outs, sems)

        @pl.when(c == layer)
        def _():
            for cp in out:
                cp.start()

    def middle(ins, outs, sems):
        c, _, back, fwd = copies(ins, outs, sems)

        @pl.when(c == layer)
        def _():
            for arrived, onward in zip(back, fwd):
                arrived.wait_recv()
                onward.start()

    def last(ins, outs, sems):
        c, out, _, fwd = copies(ins, outs, sems)

        @pl.when(c == layer)
        def _():
            for cp in out + fwd:
                cp.wait_send()

        @pl.when(c != layer)
        def _():
            for cp in fwd:
                cp.wait_recv()

    return first, middle, last


def _scatter_phases(n, layer):
    def copies(ins, outs, sems):
        send, recv = sems
        x, y, c = _mesh_pos()
        return c, [_remote(ins[i].at[2 * px + py], outs[i].at[j], send.at[3 * i + j], recv.at[3 * i + j], (px, py, c))
                   for i in range(n) for j, (px, py) in enumerate(_other_chips(x, y))]

    def first(ins, outs, sems):
        c, cps = copies(ins, outs, sems)

        @pl.when(c == layer)
        def _():
            for cp in cps:
                cp.start()

    def middle(ins, outs, sems):
        pass

    def last(ins, outs, sems):
        c, cps = copies(ins, outs, sems)

        @pl.when(c == layer)
        def _():
            for cp in cps:
                cp.wait()

    return first, middle, last


def _exchange(blocks, out_shapes, n_sems, phases, name, rider):
    sems = [pltpu.SemaphoreType.DMA((3 * len(blocks),))] * n_sems
    if rider:
        return _Rider(blocks, out_shapes, sems, phases)
    n = len(blocks)

    def body(*refs):
        for phase in phases:
            phase(refs[:n], refs[n:2 * n], refs[2 * n:])

    return list(_hbm_call(body, name, blocks, out_shapes, sems))


def _gather_layer(blocks, layer, name=None, rider=False):
    outs = [jax.ShapeDtypeStruct((N_CHIPS,) + b.shape, b.dtype) for b in blocks]
    return _exchange(blocks, outs, 4, _gather_phases(len(blocks), layer), name, rider)


def _scatter_layer(sums, layer, name=None, rider=False):
    outs = [jax.ShapeDtypeStruct((3,) + s.shape[1:], s.dtype) for s in sums]
    return _exchange(sums, outs, 2, _scatter_phases(len(sums), layer), name, rider)


def _to_sibling(gs, layer, name=None, rider=False):
    n = len(gs)

    def copies(ins, outs, sems):
        send, recv = sems
        x, y, c = _mesh_pos()
        return c, [_remote(ins[i], outs[i], send.at[i], recv.at[i], (x, y, 1 - c)) for i in range(n)]

    def first(ins, outs, sems):
        c, cps = copies(ins, outs, sems)

        @pl.when(c != layer)
        def _():
            for cp in cps:
                cp.start()

    def middle(ins, outs, sems):
        pass

    def last(ins, outs, sems):
        c, cps = copies(ins, outs, sems)

        @pl.when(c != layer)
        def _():
            for cp in cps:
                cp.wait_send()

        @pl.when(c == layer)
        def _():
            for cp in cps:
                cp.wait_recv()

    sems = [pltpu.SemaphoreType.DMA((n,))] * 2
    outs = [jax.ShapeDtypeStruct(g.shape, g.dtype) for g in gs]
    if rider:
        return _Rider(gs, outs, sems, (first, middle, last))

    def body(*refs):
        for phase in (first, middle, last):
            phase(refs[:n], refs[n:2 * n], refs[2 * n:])

    return list(_hbm_call(body, name, gs, outs, sems))


def _sibling_swap(rs, name):
    n = len(rs)

    def body(*refs):
        ins, outs = refs[:n], refs[n:2 * n]
        send, recv = refs[2 * n:]
        x, y, c = _mesh_pos()
        cps = [_remote(ins[i], outs[i], send.at[i], recv.at[i], (x, y, 1 - c)) for i in range(n)]
        for cp in cps:
            cp.start()
        for cp in cps:
            cp.wait()

    sem = pltpu.SemaphoreType.DMA((n,))
    return _hbm_call(body, name, rs, [jax.ShapeDtypeStruct(r.shape, r.dtype) for r in rs], [sem, sem])


def _small_all_reduce(vec, name):
    r = vec.shape[0]
    ndev = 8

    def body(v_ref, o_ref, buf, send, recv):
        x, y, c = _mesh_pos()
        me = 4 * x + 2 * y + c
        buf[me] = v_ref[...]
        cps = []
        for rel in range(1, ndev):
            px = 1 - x if rel & 4 else x
            py = 1 - y if rel & 2 else y
            pc = 1 - c if rel & 1 else c
            cps.append((_remote(v_ref, buf.at[me], send.at[rel - 1], recv.at[rel - 1], (px, py, pc)),
                        4 * px + 2 * py + pc))
        for cp, _ in cps:
            cp.start()
        for k, (cp, peer) in enumerate(cps):
            slot = buf.at[peer]
            _remote(slot, slot, send.at[k], recv.at[k], (x, y, c)).wait_recv()
        for cp, _ in cps:
            cp.wait_send()
        acc = buf[0]
        for k in range(1, ndev):
            acc = acc + buf[k]
        o_ref[...] = acc

    vm = pl.BlockSpec(memory_space=pltpu.VMEM)
    return pl.pallas_call(
        body, name=name, out_shape=jax.ShapeDtypeStruct(vec.shape, F32), in_specs=[vm], out_specs=vm,
        scratch_shapes=[pltpu.VMEM((ndev, r, LANES), F32), pltpu.SemaphoreType.DMA((ndev - 1,)),
                        pltpu.SemaphoreType.DMA((ndev - 1,))],
        compiler_params=pltpu.CompilerParams(has_side_effects=True),
    )(vec)


def _row_tile(rows, cap=SUM_ROWS):
    for t in range(min(rows, cap), 0, -1):
        if rows % t == 0 and (t % 16 == 0 or t == rows):
            return t
    raise ValueError(rows)


def _add_pairs(a, b, name):
    k, r, c = a.shape
    tr = _row_tile(r)

    def body(a_ref, b_ref, o_ref):
        o_ref[...] = (a_ref[...].astype(F32) + b_ref[...].astype(F32)).astype(o_ref.dtype)

    spec = pl.BlockSpec((None, tr, c), lambda i, j: (i, j, 0))
    return pl.pallas_call(body, name=name, grid=(k, r // tr), in_specs=[spec, spec], out_specs=spec,
                          out_shape=jax.ShapeDtypeStruct(a.shape, a.dtype),
                          compiler_params=_params(("parallel", "parallel")))(a, b)


def _final_sum(own, sib, others, name):
    r, c = own.shape
    tr = _row_tile(r)

    def body(a_ref, b_ref, o_ref_in, out_ref):
        acc = a_ref[...].astype(F32) + b_ref[...].astype(F32)
        for k in range(3):
            acc = acc + o_ref_in[k].astype(F32)
        out_ref[...] = acc

    spec = pl.BlockSpec((tr, c), lambda i: (i, 0))
    return pl.pallas_call(body, name=name, grid=(r // tr,),
                          in_specs=[spec, spec, pl.BlockSpec((3, tr, c), lambda i: (0, i, 0))], out_specs=spec,
                          out_shape=jax.ShapeDtypeStruct((r, c), F32),
                          compiler_params=_params(("parallel",)))(own, sib, others)


def _adamw(g, w, m, v, name):
    r, c = g.shape
    tr = _row_tile(r, ADAM_ROWS)

    def body(g_ref, w_ref, m_ref, v_ref, d_ref, mo_ref, vo_ref):
        gv = g_ref[...]
        mn = ADAM_B1 * m_ref[...] + (1.0 - ADAM_B1) * gv
        vn = ADAM_B2 * v_ref[...] + (1.0 - ADAM_B2) * (gv * gv)
        m_hat = mn / (1.0 - ADAM_B1 ** ADAM_STEP)
        v_hat = vn / (1.0 - ADAM_B2 ** ADAM_STEP)
        d_ref[...] = -ADAM_LR * (m_hat / (jnp.sqrt(v_hat) + ADAM_EPS) + ADAM_WD * w_ref[...])
        mo_ref[...] = mn
        vo_ref[...] = vn

    spec = pl.BlockSpec((tr, c), lambda i: (i, 0))
    shp = jax.ShapeDtypeStruct((r, c), F32)
    return pl.pallas_call(body, name=name, grid=(r // tr,), in_specs=[spec] * 4, out_specs=[spec] * 3,
                          out_shape=[shp] * 3, compiler_params=_params(("parallel",)))(g, w, m, v)


def _pack(arrays):
    flat = jnp.concatenate([a.reshape(-1).astype(F32) for a in arrays])
    pad = (-flat.shape[0]) % (8 * LANES)
    return jnp.concatenate([flat, jnp.zeros((pad,), F32)]).reshape(-1, LANES)


def _unpack(packed, shapes):
    flat = packed.reshape(-1)
    out, off = [], 0
    for s in shapes:
        size = 1
        for dim in s:
            size *= dim
        out.append(flat[off:off + size].reshape(s))
        off += size
    return out


BIG = ("ffn1_w_in", "ffn1_w_out", "w_in", "w_out", "ffn2_w_in", "ffn2_w_out")
SMALL = ("ffn1_norm", "mix_norm", "fox_q_norm", "fox_k_norm", "fox_f_bias", "gdn_a_log", "gdn_dt_bias",
         "gdn_out_norm", "ffn2_norm", "gdn_conv")
WEIGHTS = ("ffn1_norm", "ffn1_w_in", "ffn1_w_out", "mix_norm", "w_in", "fox_q_norm", "fox_k_norm", "fox_f_bias",
           "gdn_conv", "gdn_a_log", "gdn_dt_bias", "gdn_out_norm", "w_out", "ffn2_norm", "ffn2_w_in", "ffn2_w_out")


def _step(x, target, w, m, v):
    xi, yi, ci = _mesh_pos()
    me = 2 * xi + yi
    depth = DEPTH
    d = x.shape[-1]

    nb, seq, _ = x.shape
    assert depth == 2

    p ={k: w[k] for k in SMALL if k != "gdn_conv"}
    for k in ("ffn1_w_in", "ffn1_w_out", "ffn2_w_in", "ffn2_w_out", "w_mix", "w_out", "gdn_conv"):
        p[k] = [None] * depth

    first, rest = BIG[:2], BIG[2:] + ("gdn_conv",)

    def shards(l, names):
        return [w[k][l] if k == "gdn_conv" else w[k][l].astype(BF) for k in names]

    def place(l, names, gathered):
        blocks = dict(zip(names, [lax.dynamic_update_index_in_dim(g, s, me, 0)
                                  for g, s in zip(gathered, shards(l, names))]))
        for k in ("ffn1_w_in", "ffn1_w_out", "ffn2_w_in", "ffn2_w_out"):
            if k in blocks:
                p[k][l] = blocks[k]
        if "w_in" in blocks:
            p["w_mix"][l] = _mix_to_padded(blocks["w_in"].transpose(1, 0, 2).reshape(d, N_IN))
            p["w_out"][l] = blocks["w_out"].reshape(2 * FOX_WIDTH, d)
            p["gdn_conv"][l] = blocks["gdn_conv"].transpose(1, 0, 2).reshape(CONV_WIDTH, -1)

    place(0, first, _gather_layer(shards(0, first), 0, "gather_first_ffn0"))
    xt = x.reshape(nb * seq, d)
    xt, saved0, gathered1 = _layer_fwd(
        xt, p, 0, nb, seq, _gather_layer(shards(1, first + rest), 1, rider=True),
        _gather_layer(shards(0, rest), 0, rider=True), lambda got: place(0, rest, got))
    place(1, first + rest, gathered1)
    xt, saved1, _ = _layer_fwd(xt, p, 1, nb, seq)
    loss, dx = _loss_grad(xt, target.reshape(nb * seq, d), "loss")

    def transport(g, names):
        out = []
        for k in names:
            if k == "w_in":
                out.append(g["w_in"].reshape(d, N_CHIPS, N_IN // N_CHIPS).transpose(1, 0, 2).astype(BF))
            elif k == "w_out":
                out.append(g["w_out"].reshape(N_CHIPS, -1, d))
            else:
                out.append(g[k])
        return out

    def chip_sums(g, l, names, tag):
        own = transport(g, names)
        sib = _to_sibling(own, l, f"grad{l}{tag}_to_sibling")
        return own, sib, [_add_pairs(a, b, f"grad{l}{tag}_chip_sum_{k}") for a, b, k in zip(own, sib, names)]

    dx, grads1, _, _, _ = _layer_bwd(dx, p, 1, saved1, nb, seq)
    own1 = transport(grads1, BIG)
    before = {}

    def after_ffn2(from_sibling):
        before["sib1"] = from_sibling
        sums1 = [_add_pairs(a, b, f"grad1_chip_sum_{k}") for a, b, k in zip(own1, from_sibling, BIG)]
        return _scatter_layer(sums1, 1, rider=True)

    def before_ffn1(g):
        before["own"], before["sib"], sums = chip_sums(g, 0, BIG[2:], "_rest")
        return _scatter_layer(sums, 0, rider=True)

    def after_w_out(g_out):
        before["own_out"], before["sib_out"], sums = chip_sums({"ffn1_w_out": g_out}, 0, first[1:], "_w_out")
        return _scatter_layer(sums, 0, rider=True)

    dx, grads0, chips1, chips0_rest, chips0_out = _layer_bwd(
        dx, p, 0, saved0, nb, seq, None, before_ffn1, _to_sibling(own1, 1, rider=True), after_ffn2, after_w_out)
    sib1 = before["sib1"]
    own0, sib0, sums0 = chip_sums(grads0, 0, first[:1], "_w_in")
    chips0 = _scatter_layer(sums0, 0, "grad0_w_in_to_chips") + chips0_out + chips0_rest
    own0 = own0 + before["own_out"] + before["own"]
    sib0 = sib0 + before["sib_out"] + before["sib"]
    grads = [grads0, grads1]
    dx = dx.reshape(nb, seq, d)

    mine = lambda a0, a1: jnp.where(ci == 0, a0, a1)
    at_me = lambda a: lax.dynamic_index_in_dim(a, me, 0, keepdims=False)
    reduced = [_final_sum(mine(at_me(own0[i]), at_me(own1[i])), mine(at_me(sib0[i]), at_me(sib1[i])),
                          mine(chips0[i], chips1[i]), f"grad_final_sum_{k}") for i, k in enumerate(BIG)]
    from_sib_final = _sibling_swap(reduced, "grad_swap_layers")
    full = {k: jnp.stack([jnp.where(ci == 0, a, b), jnp.where(ci == 0, b, a)])
            for k, a, b in zip(BIG, reduced, from_sib_final)}

    out_g, out_d, out_m, out_v = {}, {}, {}, {}
    for k in BIG:
        shp = w[k].shape
        two_d = lambda a: a.reshape(shp[0] * shp[1], shp[2])
        dl, mn, vn = _adamw(two_d(full[k]), two_d(w[k]), two_d(m[k]), two_d(v[k]), f"adamw_{k}")
        out_g[k], out_d[k], out_m[k], out_v[k] = full[k], dl.reshape(shp), mn.reshape(shp), vn.reshape(shp)

    small_local = [jnp.stack([grads[l][k] for l in range(depth)]) for k in SMALL] + [loss.reshape(1)]
    summed = _unpack(_small_all_reduce(_pack(small_local), "small_all_reduce"), [a.shape for a in small_local])
    total = summed.pop()[0]
    sg = dict(zip(SMALL, summed))
    cs = w["gdn_conv"].shape[-1]
    sg["gdn_conv"] = lax.dynamic_slice_in_dim(sg["gdn_conv"], me * cs, cs, axis=2)
    shapes = [w[k].shape for k in SMALL]
    packs = [_pack([src[k] for k in SMALL]) for src in (sg, w, m, v)]
    dl, mn, vn = _adamw(*packs, "adamw_small")
    for k, a, b, c2 in zip(SMALL, _unpack(dl, shapes), _unpack(mn, shapes), _unpack(vn, shapes)):
        out_g[k], out_d[k], out_m[k], out_v[k] = sg[k], a, b, c2

    return (total, dx, *[out_g[k] for k in WEIGHTS], *[out_d[k] for k in WEIGHTS],
            *[out_m[k] for k in WEIGHTS], *[out_v[k] for k in WEIGHTS])


def kernel(x, ffn1_norm, ffn1_w_in, ffn1_w_out, mix_norm, w_in, fox_q_norm, fox_k_norm, fox_f_bias, gdn_conv, gdn_a_log, gdn_dt_bias, gdn_out_norm, w_out, ffn2_norm, ffn2_w_in, ffn2_w_out, loss_target, m_ffn1_norm, m_ffn1_w_in, m_ffn1_w_out, m_mix_norm, m_w_in, m_fox_q_norm, m_fox_k_norm, m_fox_f_bias, m_gdn_conv, m_gdn_a_log, m_gdn_dt_bias, m_gdn_out_norm, m_w_out, m_ffn2_norm, m_ffn2_w_in, m_ffn2_w_out, v_ffn1_norm, v_ffn1_w_in, v_ffn1_w_out, v_mix_norm, v_w_in, v_fox_q_norm, v_fox_k_norm, v_fox_f_bias, v_gdn_conv, v_gdn_a_log, v_gdn_dt_bias, v_gdn_out_norm, v_w_out, v_ffn2_norm, v_ffn2_w_in, v_ffn2_w_out):
    w = dict(ffn1_norm=ffn1_norm, ffn1_w_in=ffn1_w_in, ffn1_w_out=ffn1_w_out, mix_norm=mix_norm, w_in=w_in,
             fox_q_norm=fox_q_norm, fox_k_norm=fox_k_norm, fox_f_bias=fox_f_bias, gdn_conv=gdn_conv,
             gdn_a_log=gdn_a_log, gdn_dt_bias=gdn_dt_bias, gdn_out_norm=gdn_out_norm, w_out=w_out,
             ffn2_norm=ffn2_norm, ffn2_w_in=ffn2_w_in, ffn2_w_out=ffn2_w_out)
    m = dict(ffn1_norm=m_ffn1_norm, ffn1_w_in=m_ffn1_w_in, ffn1_w_out=m_ffn1_w_out, mix_norm=m_mix_norm, w_in=m_w_in,
             fox_q_norm=m_fox_q_norm, fox_k_norm=m_fox_k_norm, fox_f_bias=m_fox_f_bias, gdn_conv=m_gdn_conv,
             gdn_a_log=m_gdn_a_log, gdn_dt_bias=m_gdn_dt_bias, gdn_out_norm=m_gdn_out_norm, w_out=m_w_out,
             ffn2_norm=m_ffn2_norm, ffn2_w_in=m_ffn2_w_in, ffn2_w_out=m_ffn2_w_out)
    v = dict(ffn1_norm=v_ffn1_norm, ffn1_w_in=v_ffn1_w_in, ffn1_w_out=v_ffn1_w_out, mix_norm=v_mix_norm, w_in=v_w_in,
             fox_q_norm=v_fox_q_norm, fox_k_norm=v_fox_k_norm, fox_f_bias=v_fox_f_bias, gdn_conv=v_gdn_conv,
             gdn_a_log=v_gdn_a_log, gdn_dt_bias=v_gdn_dt_bias, gdn_out_norm=v_gdn_out_norm, w_out=v_w_out,
             ffn2_norm=v_ffn2_norm, ffn2_w_in=v_ffn2_w_in, ffn2_w_out=v_ffn2_w_out)
    return _step(x, loss_target, w, m, v)
```
